```python
import jax
import jax.numpy as jnp
from jax import lax
import numpy as np

D_MODEL = 2048
BATCH = 4
SEQ = 2048
DEPTH = 2
DEC_BATCH = 128
DEC_SEQ = 4
PAST_LEN = 16384
PAGE_SIZE = 128

N_EVEN = (DEPTH + 1) // 2
N_ODD = DEPTH // 2
W_A = D_MODEL // 2
CONV_W = 3
W_B = D_MODEL // 2
H_B = 8
DH_B = W_B // H_B
E_IN = 4 * W_A + 4 * W_B
W_C = D_MODEL // 2
H_C = 4
DV_C = W_C // H_C
DQK_C = DV_C // 2
W_D = D_MODEL // 2
G_D = 8
CHUNK_D = 128
O_IN = 2 * H_C * DQK_C + 2 * W_C + 2 * H_C + 3 * W_D
CHUNK = 128
ROPE_BASE = 10000.0
EPS = 1e-6

kernel_name = "hybrid_conv_retention_mlstm_chunkmlp_step"


def rms_norm(x, g):
    xf = x.astype(jnp.float32)
    y = xf * lax.rsqrt(jnp.mean(xf * xf, axis=-1, keepdims=True) + EPS)
    return (y * g.astype(jnp.float32)).astype(x.dtype)


def layer_norm(x, g, b):
    xf = x.astype(jnp.float32)
    xc = xf - jnp.mean(xf, axis=-1, keepdims=True)
    y = xc * lax.rsqrt(jnp.mean(xc * xc, axis=-1, keepdims=True) + EPS)
    return (y * g.astype(jnp.float32) + b.astype(jnp.float32)).astype(x.dtype)


def head_norm(y, g, n_heads):
    shp = y.shape
    yh = y.astype(jnp.float32).reshape(shp[:-1] + (n_heads, shp[-1] // n_heads))
    yc = yh - jnp.mean(yh, axis=-1, keepdims=True)
    yh = yc * lax.rsqrt(jnp.mean(yc * yc, axis=-1, keepdims=True) + EPS)
    return (yh.reshape(shp) * g.astype(jnp.float32)).astype(y.dtype)


def split_cols(p, sizes):
    idx = np.cumsum(sizes)[:-1].tolist()
    return jnp.split(p, idx, axis=-1)


def rope(x, pos):
    d = x.shape[-1]
    inv = ROPE_BASE ** (-jnp.arange(0, d, 2, dtype=jnp.float32) / d)
    ang = pos.astype(jnp.float32)[:, None] * inv[None, :]
    cos = jnp.cos(ang)[None, :, None, :]
    sin = jnp.sin(ang)[None, :, None, :]
    xf = x.astype(jnp.float32)
    x1, x2 = xf[..., : d // 2], xf[..., d // 2:]
    return jnp.concatenate([x1 * cos - x2 * sin, x1 * sin + x2 * cos], axis=-1).astype(x.dtype)


def run_chunks(step, state, xs):
    T = xs[0].shape[1]
    L = CHUNK if T % CHUNK == 0 else T
    nc = T // L
    xs_c = tuple(jnp.moveaxis(a.reshape((a.shape[0], nc, L) + a.shape[2:]), 1, 0) for a in xs)
    state, ys = lax.scan(step, state, xs_c)
    ys = jnp.moveaxis(ys, 0, 1)
    return state, ys.reshape((ys.shape[0], T) + ys.shape[3:])


def make_retention_step(log_gamma):
    def step(S, qkv):
        q, k, v = qkv
        qf, kf, vf = (a.astype(jnp.float32) for a in (q, k, v))
        Sf = S.astype(jnp.float32)
        L = q.shape[1]
        i = jnp.arange(L, dtype=jnp.float32)
        diff = i[:, None] - i[None, :]
        decay = jnp.where(diff[None] >= 0,
                          jnp.exp(log_gamma[:, None, None] * jnp.maximum(diff, 0.0)[None]), 0.0)
        scores = jnp.einsum('bihd,bjhd->bhij', qf, kf) * decay[None]
        inner = jnp.einsum('bhij,bjhe->bihe', scores, vf)
        q_decay = jnp.exp(log_gamma[None, :] * (i[:, None] + 1.0))
        cross = jnp.einsum('bihd,bhde->bihe', qf, Sf) * q_decay[None, :, :, None]
        k_decay = jnp.exp(log_gamma[None, :] * (L - 1.0 - i[:, None]))
        S_new = jnp.exp(log_gamma * L)[None, :, None, None] * Sf + \
            jnp.einsum('bjhd,bjhe->bhde', kf * k_decay[None, :, :, None], vf)
        return S_new.astype(S.dtype), (inner + cross).astype(q.dtype)
    return step


def mlstm_step(carry, xs):
    C, n, m = carry
    q, k, v, ig, lf = xs
    qf, kf, vf = (a.astype(jnp.float32) for a in (q, k, v))
    Cf, nf, mf = C.astype(jnp.float32), n.astype(jnp.float32), m.astype(jnp.float32)
    ig = ig.astype(jnp.float32)
    lf = lf.astype(jnp.float32)
    L = q.shape[1]
    b = jnp.cumsum(lf, axis=1)
    causal = jnp.tril(jnp.ones((L, L), dtype=bool))
    log_d = b[:, :, None, :] - b[:, None, :, :] + ig[:, None, :, :]
    log_d = jnp.where(causal[None, :, :, None], log_d, -jnp.inf)
    log_inter = b + mf[:, None, :]
    m_t = jnp.maximum(log_inter, jnp.max(log_d, axis=2))
    w = jnp.exp(log_d - m_t[:, :, None, :])
    w_inter = jnp.exp(log_inter - m_t)
    scores = jnp.einsum('bthd,bshd->btsh', qf, kf) * w
    num = jnp.einsum('btsh,bshe->bthe', scores, vf) + \
        w_inter[..., None] * jnp.einsum('bhed,bthd->bthe', Cf, qf)
    den = jnp.sum(scores, axis=2) + w_inter * jnp.einsum('bhd,bthd->bth', nf, qf)
    h = num / jnp.maximum(jnp.abs(den), jnp.exp(-m_t))[..., None]
    m_new = m_t[:, -1]
    w_end = jnp.exp(b[:, -1:, :] - b + ig - m_new[:, None, :])
    carry_decay = jnp.exp(b[:, -1] + mf - m_new)
    C_new = carry_decay[..., None, None] * Cf + jnp.einsum('bsh,bshe,bshd->bhed', w_end, vf, kf)
    n_new = carry_decay[..., None] * nf + jnp.einsum('bsh,bshd->bhd', w_end, kf)
    return (C_new.astype(C.dtype), n_new.astype(n.dtype), m_new.astype(m.dtype)), h.astype(q.dtype)


def layer_even(x, pos, conv_st, ret_st, g_norm, w_in, conv_w, g_ret, w_out):
    B, T, _ = x.shape
    h = rms_norm(x, g_norm)
    p = h @ w_in
    a_b, a_c, a_x, a_z, r_q, r_k, r_v, r_z = split_cols(p, [W_A, W_A, W_A, W_A, W_B, W_B, W_B, W_B])
    u = a_c * a_x
    buf = jnp.concatenate([conv_st.astype(u.dtype), u], axis=1)
    conv = conv_w[0] * buf[:, 0:T]
    for j in range(1, CONV_W):
        conv = conv + conv_w[j] * buf[:, j:j + T]
    y_a = a_b * conv * jax.nn.silu(a_z)
    new_conv = buf[:, T:]
    q = rope(r_q.reshape(B, T, H_B, DH_B), pos)
    k = rope(r_k.reshape(B, T, H_B, DH_B), pos) * (DH_B ** -0.5)
    v = r_v.reshape(B, T, H_B, DH_B)
    log_gamma = jnp.log(1.0 - 2.0 ** (-5.0 - jnp.arange(H_B, dtype=jnp.float32)))
    S_new, o = run_chunks(make_retention_step(log_gamma), ret_st, (q, k, v))
    y_b = head_norm(o.reshape(B, T, W_B), g_ret, H_B) * jax.nn.silu(r_z)
    out = jnp.concatenate([y_a, y_b], axis=-1) @ w_out
    return x + out, new_conv, S_new


def layer_odd(x, c_st, n_st, m_st, g_norm, w_in, b_gate, g_mlstm, ln_g, ln_b, w_s, b_s, w_out):
    B, T, _ = x.shape
    h = rms_norm(x, g_norm)
    p = h @ w_in
    q, k, v, z_c, ig, fg, d_u, d_v, d_z = split_cols(
        p, [H_C * DQK_C, H_C * DQK_C, W_C, W_C, H_C, H_C, W_D, W_D, W_D])
    q = q.reshape(B, T, H_C, DQK_C) * (DQK_C ** -0.5)
    k = k.reshape(B, T, H_C, DQK_C)
    v = v.reshape(B, T, H_C, DV_C)
    ig = ig.astype(jnp.float32) + b_gate[:H_C].astype(jnp.float32)
    lf = jax.nn.log_sigmoid(fg.astype(jnp.float32) + b_gate[H_C:].astype(jnp.float32))
    (c_new, n_new, m_new), o = run_chunks(mlstm_step, (c_st, n_st, m_st), (q, k, v, ig, lf))
    y_c = head_norm(o.reshape(B, T, W_C), g_mlstm, H_C) * jax.nn.silu(z_c)
    vn = layer_norm(d_v, ln_g, ln_b)
    Lc = CHUNK_D if T % CHUNK_D == 0 else T
    ws = jnp.tril(w_s[:, :Lc, :Lc])
    vc = vn.reshape(B, T // Lc, Lc, G_D, W_D // G_D)
    s = jnp.einsum('gts,bcsgd->bctgd', ws, vc) + b_s[:, :Lc].T[None, None, :, :, None]
    y_d = d_u * s.reshape(B, T, W_D).astype(d_u.dtype) * jax.nn.silu(d_z)
    out = jnp.concatenate([y_c, y_d], axis=-1) @ w_out
    return x + out, c_new, n_new, m_new, vn


def trunk(x, pos, conv_st, ret_st, c_st, n_st, m_st,
          norm_even, w_in_even, conv_w, ret_norm, w_out_even,
          norm_odd, w_in_odd, b_gate_odd, mlstm_norm, ln_v_g, ln_v_b,
          w_spatial, b_spatial, w_out_odd, norm_final):
    new_conv, new_ret, new_c, new_n, new_m, new_dv = [], [], [], [], [], []
    for l in range(DEPTH):
        i = l // 2
        if l % 2 == 0:
            x, cs, rs = layer_even(x, pos, conv_st[i], ret_st[i], norm_even[i], w_in_even[i],
                                   conv_w[i], ret_norm[i], w_out_even[i])
            new_conv.append(cs)
            new_ret.append(rs)
        else:
            x, cs, ns, ms, dv = layer_odd(x, c_st[i], n_st[i], m_st[i], norm_odd[i], w_in_odd[i],
                                          b_gate_odd[i], mlstm_norm[i], ln_v_g[i], ln_v_b[i],
                                          w_spatial[i], b_spatial[i], w_out_odd[i])
            new_c.append(cs)
            new_n.append(ns)
            new_m.append(ms)
            new_dv.append(dv)
    y = rms_norm(x, norm_final)
    return (y, jnp.stack(new_conv), jnp.stack(new_ret), jnp.stack(new_c),
            jnp.stack(new_n), jnp.stack(new_m), jnp.stack(new_dv))


def setup_inputs(seed: int = 0) -> dict:
    key = jax.random.key(seed)
    ks = jax.random.split(key, 24)
    f32 = jnp.float32

    def nrm(k, shape, scale):
        return jax.random.normal(k, shape, f32) * scale

    forget_bias = jnp.linspace(3.0, 6.0, H_C, dtype=f32)[None, :] + nrm(ks[14], (N_ODD, H_C), 0.1)
    input_bias = nrm(ks[13], (N_ODD, H_C), 0.1)
    return {
        "x_prompt": nrm(ks[0], (BATCH, SEQ, D_MODEL), 1.0),
        "x_sample": nrm(ks[1], (DEC_BATCH, DEC_SEQ, D_MODEL), 1.0),
        "state_conv": nrm(ks[2], (N_EVEN, DEC_BATCH, CONV_W - 1, W_A), 1.0),
        "state_ret": nrm(ks[3], (N_EVEN, DEC_BATCH, H_B, DH_B, DH_B), 0.5),
        "state_mlstm_C": nrm(ks[4], (N_ODD, DEC_BATCH, H_C, DV_C, DQK_C), 0.3),
        "state_mlstm_n": nrm(ks[5], (N_ODD, DEC_BATCH, H_C, DQK_C), 0.3),
        "state_mlstm_m": jax.random.uniform(ks[6], (N_ODD, DEC_BATCH, H_C), f32, 0.0, 4.0),
        "norm_even": 1.0 + nrm(ks[7], (N_EVEN, D_MODEL), 0.02),
        "w_in_even": nrm(ks[8], (N_EVEN, D_MODEL, E_IN), D_MODEL ** -0.5),
        "conv_w": nrm(ks[9], (N_EVEN, CONV_W, W_A), CONV_W ** -0.5),
        "ret_norm": 1.0 + nrm(ks[10], (N_EVEN, W_B), 0.02),
        "w_out_even": nrm(ks[11], (N_EVEN, W_A + W_B, D_MODEL), (W_A + W_B) ** -0.5),
        "norm_odd": 1.0 + nrm(ks[12], (N_ODD, D_MODEL), 0.02),
        "w_in_odd": nrm(ks[15], (N_ODD, D_MODEL, O_IN), D_MODEL ** -0.5),
        "b_gate_odd": jnp.concatenate([input_bias, forget_bias], axis=-1),
        "mlstm_norm": 1.0 + nrm(ks[16], (N_ODD, W_C), 0.02),
        "ln_v_g": 1.0 + nrm(ks[17], (N_ODD, W_D), 0.02),
        "ln_v_b": nrm(ks[18], (N_ODD, W_D), 0.02),
        "w_spatial": nrm(ks[19], (N_ODD, G_D, CHUNK_D, CHUNK_D), CHUNK_D ** -0.5),
        "b_spatial": 1.0 + nrm(ks[20], (N_ODD, G_D, CHUNK_D), 0.02),
        "w_out_odd": nrm(ks[21], (N_ODD, W_C + W_D, D_MODEL), (W_C + W_D) ** -0.5),
        "norm_final": 1.0 + nrm(ks[22], (D_MODEL,), 0.02),
    }


def reference(x_prompt, x_sample, state_conv, state_ret, state_mlstm_C, state_mlstm_n, state_mlstm_m,
              norm_even, w_in_even, conv_w, ret_norm, w_out_even,
              norm_odd, w_in_odd, b_gate_odd, mlstm_norm, ln_v_g, ln_v_b,
              w_spatial, b_spatial, w_out_odd, norm_final):
    weights = (norm_even, w_in_even, conv_w, ret_norm, w_out_even,
               norm_odd, w_in_odd, b_gate_odd, mlstm_norm, ln_v_g, ln_v_b,
               w_spatial, b_spatial, w_out_odd, norm_final)
    dt = x_prompt.dtype
    pos_p = jnp.arange(SEQ, dtype=jnp.int32)
    zc = jnp.zeros((N_EVEN, BATCH, CONV_W - 1, W_A), dt)
    zr = jnp.zeros((N_EVEN, BATCH, H_B, DH_B, DH_B), dt)
    zC = jnp.zeros((N_ODD, BATCH, H_C, DV_C, DQK_C), dt)
    zn = jnp.zeros((N_ODD, BATCH, H_C, DQK_C), dt)
    zm = jnp.zeros((N_ODD, BATCH, H_C), dt)
    y_prompt, conv_p, ret_p, c_p, n_p, m_p, _ = trunk(x_prompt, pos_p, zc, zr, zC, zn, zm, *weights)

    pos_s = PAST_LEN + jnp.arange(DEC_SEQ, dtype=jnp.int32)
    y_sample, conv_s, ret_s, c_s, n_s, m_s, dv_s = trunk(
        x_sample, pos_s, state_conv, state_ret, state_mlstm_C, state_mlstm_n, state_mlstm_m, *weights)
    return (y_prompt, y_sample, conv_p, conv_s, ret_p, ret_s, c_p, c_s, n_p, n_s, m_p, m_s, dv_s)
```

```python
import functools
import math

import jax
import jax.numpy as jnp
from jax import lax
from jax.experimental import pallas as pl
from jax.experimental.pallas import tpu as pltpu

F32 = jnp.float32
BF16 = jnp.bfloat16

D_MODEL = 2048
BATCH = 4
SEQ = 2048
DEC_BATCH = 128
DEC_SEQ = 4
PAST_LEN = 16384
W_A = 1024
CONV_W = 3
W_B = 1024
H_B = 8
DH_B = 128
E_IN = 8192
W_C = 1024
H_C = 4
DV_C = 256
DQK_C = 128
W_D = 1024
G_D = 8
CHUNK = 128
O_IN = 6152
O_PAD = 6400
GATE_COL = 6144
ROPE_BASE = 10000.0
EPS = 1e-6
LOG_GAMMA = tuple(math.log(1.0 - 2.0 ** (-5.0 - h)) for h in range(H_B))
NEG_INF = float("-inf")
VMEM_LIMIT = 56 * 1024 * 1024

NT_DIMS = (((1,), (1,)), ((), ()))
TN_DIMS = (((0,), (0,)), ((), ()))


def _silu(z):
    return z * (1.0 / (1.0 + jnp.exp(-z)))


def _log_sigmoid(x):
    return jnp.minimum(x, 0.0) - jnp.log1p(jnp.exp(-jnp.abs(x)))


def _dot(a, b):
    return jnp.dot(a, b, preferred_element_type=F32)


def _dot_nt(a, b):
    return lax.dot_general(a, b, NT_DIMS, preferred_element_type=F32)


def _dot_tn(a, b):
    return lax.dot_general(a, b, TN_DIMS, preferred_element_type=F32)


def _dot_hi(a, b):
    return jnp.dot(a, b, preferred_element_type=F32, precision=lax.Precision.HIGHEST)


def _head_norm(o, g):
    mu = jnp.mean(o, axis=-1, keepdims=True)
    oc = o - mu
    var = jnp.mean(oc * oc, axis=-1, keepdims=True)
    return oc * lax.rsqrt(var + EPS) * g


def _params(sem):
    return pltpu.CompilerParams(dimension_semantics=sem, vmem_limit_bytes=VMEM_LIMIT)


def _norm_cast_kernel(x_ref, g_ref, h_ref):
    x = x_ref[...]
    ms = jnp.mean(x * x, axis=-1, keepdims=True)
    h_ref[...] = (x * lax.rsqrt(ms + EPS) * g_ref[...]).astype(BF16)


def _norm_cast(x, g, tm):
    m, d = x.shape
    return pl.pallas_call(
        _norm_cast_kernel,
        grid=(m // tm,),
        in_specs=[pl.BlockSpec((tm, d), lambda i: (i, 0)),
                  pl.BlockSpec((1, d), lambda i: (0, 0))],
        out_specs=pl.BlockSpec((tm, d), lambda i: (i, 0)),
        out_shape=jax.ShapeDtypeStruct((m, d), BF16),
        compiler_params=_params(("arbitrary",)),
        name="norm_cast",
    )(x, g)


def _mm_kernel(h_ref, w_ref, o_ref):
    o_ref[...] = _dot(h_ref[...], w_ref[...])


def _matmul(h, w, tm, tn):
    m, k = h.shape
    n = w.shape[1]
    return pl.pallas_call(
        _mm_kernel,
        grid=(m // tm, n // tn),
        in_specs=[pl.BlockSpec((tm, k), lambda i, j: (i, 0)),
                  pl.BlockSpec((k, tn), lambda i, j: (0, j))],
        out_specs=pl.BlockSpec((tm, tn), lambda i, j: (i, j)),
        out_shape=jax.ShapeDtypeStruct((m, n), F32),
        compiler_params=_params(("arbitrary", "arbitrary")),
        name="in_proj",
    )(h, w)


def _outproj_kernel(ya_ref, yb_ref, w_ref, x_ref, g_ref, *out_refs, final):
    half = ya_ref.shape[1]
    acc = _dot(ya_ref[...], w_ref[0:half, :]) + _dot(yb_ref[...], w_ref[half:2 * half, :])
    x1 = x_ref[...] + acc
    ms = jnp.mean(x1 * x1, axis=-1, keepdims=True)
    hn = x1 * lax.rsqrt(ms + EPS) * g_ref[...]
    if final:
        out_refs[0][...] = hn
    else:
        out_refs[0][...] = x1
        out_refs[1][...] = hn.astype(BF16)


def _outproj(ya, yb, w, x, g, tm, final):
    m, half = ya.shape
    d = w.shape[1]
    row = lambda i: (i, 0)
    if final:
        out_shape = jax.ShapeDtypeStruct((m, d), F32)
        out_specs = pl.BlockSpec((tm, d), row)
    else:
        out_shape = (jax.ShapeDtypeStruct((m, d), F32), jax.ShapeDtypeStruct((m, d), BF16))
        out_specs = (pl.BlockSpec((tm, d), row), pl.BlockSpec((tm, d), row))
    return pl.pallas_call(
        functools.partial(_outproj_kernel, final=final),
        grid=(m // tm,),
        in_specs=[pl.BlockSpec((tm, half), row),
                  pl.BlockSpec((tm, half), row),
                  pl.BlockSpec((2 * half, d), lambda i: (0, 0)),
                  pl.BlockSpec((tm, d), row),
                  pl.BlockSpec((1, d), lambda i: (0, 0))],
        out_specs=out_specs,
        out_shape=out_shape,
        compiler_params=_params(("arbitrary",)),
        name="out_proj_final" if final else "out_proj",
    )(ya, yb, w, x, g)


def _rope(x, cosf, sins):
    return x * cosf + pltpu.roll(x, DH_B // 2, 1) * sins


def _even_prompt_kernel(p_ref, cw_ref, gret_ref, cos_ref, sin_ref,
                        ya_ref, yb_ref, conv_ref, s_ref, ubuf):
    c = pl.program_id(1)
    L = CHUNK

    @pl.when(c == 0)
    def _():
        ubuf[0:8, :] = jnp.zeros((8, W_A), F32)
        s_ref[...] = jnp.zeros_like(s_ref)

    for j in range(W_A // 128):
        sl = slice(j * 128, (j + 1) * 128)
        a_b = p_ref[:, j * 128:(j + 1) * 128]
        a_c = p_ref[:, W_A + j * 128:W_A + (j + 1) * 128]
        a_x = p_ref[:, 2 * W_A + j * 128:2 * W_A + (j + 1) * 128]
        a_z = p_ref[:, 3 * W_A + j * 128:3 * W_A + (j + 1) * 128]
        u = a_c * a_x
        ubuf[8:8 + L, sl] = u
        t0 = ubuf[6:6 + L, sl]
        t1 = ubuf[7:7 + L, sl]
        conv = cw_ref[0:1, sl] * t0 + cw_ref[1:2, sl] * t1 + cw_ref[2:3, sl] * u
        ya_ref[:, sl] = (a_b * conv * _silu(a_z)).astype(BF16)
        ubuf[0:8, sl] = u[L - 8:L, :]
    conv_ref[0] = ubuf[6:8, :]

    cosf = cos_ref[...]
    sins = sin_ref[...]
    row = lax.broadcasted_iota(jnp.int32, (L, L), 0)
    col = lax.broadcasted_iota(jnp.int32, (L, L), 1)
    causal = row >= col
    diff = jnp.maximum(row - col, 0).astype(F32)
    ti = lax.broadcasted_iota(jnp.int32, (L, 1), 0).astype(F32)
    base = 4 * W_A
    for h in range(H_B):
        lg = LOG_GAMMA[h]
        sl = slice(h * DH_B, (h + 1) * DH_B)
        q = p_ref[:, base + h * DH_B:base + (h + 1) * DH_B]
        k = p_ref[:, base + W_B + h * DH_B:base + W_B + (h + 1) * DH_B]
        v = p_ref[:, base + 2 * W_B + h * DH_B:base + 2 * W_B + (h + 1) * DH_B]
        z = p_ref[:, base + 3 * W_B + h * DH_B:base + 3 * W_B + (h + 1) * DH_B]
        qr = _rope(q, cosf, sins)
        kr = _rope(k, cosf, sins) * (DH_B ** -0.5)
        decay = jnp.where(causal, jnp.exp(lg * diff), 0.0)
        qb = qr.astype(BF16)
        kb = kr.astype(BF16)
        vb = v.astype(BF16)
        sc = _dot_nt(qb, kb) * decay
        inner = _dot(sc.astype(BF16), vb)
        s_old = s_ref[0, h]
        cross = _dot(qb, s_old.astype(BF16)) * jnp.exp(lg * (ti + 1.0))
        kd = (kr * jnp.exp(lg * (L - 1.0 - ti))).astype(BF16)
        s_ref[0, h] = math.exp(lg * L) * s_old + _dot_tn(kd, vb)
        o = inner + cross
        yb_ref[:, sl] = (_head_norm(o, gret_ref[0:1, sl]) * _silu(z)).astype(BF16)


def _even_prompt(p, conv_w, g_ret, cosf, sins):
    nc = SEQ // CHUNK
    rows = lambda b, c: (b * nc + c, 0)
    const2 = lambda b, c: (0, 0)
    return pl.pallas_call(
        _even_prompt_kernel,
        grid=(BATCH, nc),
        in_specs=[pl.BlockSpec((CHUNK, E_IN), rows),
                  pl.BlockSpec((CONV_W, W_A), const2),
                  pl.BlockSpec((1, W_B), const2),
                  pl.BlockSpec((CHUNK, DH_B), lambda b, c: (c, 0)),
                  pl.BlockSpec((CHUNK, DH_B), lambda b, c: (c, 0))],
        out_specs=(pl.BlockSpec((CHUNK, W_A), rows),
                   pl.BlockSpec((CHUNK, W_B), rows),
                   pl.BlockSpec((1, CONV_W - 1, W_A), lambda b, c: (b, 0, 0)),
                   pl.BlockSpec((1, H_B, DH_B, DH_B), lambda b, c: (b, 0, 0, 0))),
        out_shape=(jax.ShapeDtypeStruct((BATCH * SEQ, W_A), BF16),
                   jax.ShapeDtypeStruct((BATCH * SEQ, W_B), BF16),
                   jax.ShapeDtypeStruct((BATCH, CONV_W - 1, W_A), F32),
                   jax.ShapeDtypeStruct((BATCH, H_B, DH_B, DH_B), F32)),
        scratch_shapes=[pltpu.VMEM((CHUNK + 8, W_A), F32)],
        compiler_params=_params(("arbitrary", "arbitrary")),
        name="even_prompt",
    )(p, conv_w, g_ret, cosf, sins)


SB = 32
SR = SB * DEC_SEQ


def _even_sample_kernel(pa_ref, pq_ref, pk_ref, pv_ref, pz_ref, st_ref, s_ref,
                        cw_ref, gret_ref, cos_ref, sin_ref, lg_ref,
                        ya_ref, u_ref, yb_ref, so_ref, cross_scr):
    h = pl.program_id(1)
    row = lax.broadcasted_iota(jnp.int32, (SR, SR), 0)
    col = lax.broadcasted_iota(jnp.int32, (SR, SR), 1)
    trow = row & 3

    @pl.when(h == 0)
    def _():
        for j in range(W_A // 128):
            sl = slice(j * 128, (j + 1) * 128)
            a_b = pa_ref[:, j * 128:(j + 1) * 128]
            a_c = pa_ref[:, W_A + j * 128:W_A + (j + 1) * 128]
            a_x = pa_ref[:, 2 * W_A + j * 128:2 * W_A + (j + 1) * 128]
            a_z = pa_ref[:, 3 * W_A + j * 128:3 * W_A + (j + 1) * 128]
            u = a_c * a_x
            e = st_ref[:, sl]
            tap1 = jnp.where(trow >= 1, pltpu.roll(u, 1, 0), pltpu.roll(e, SR - 1, 0))
            tap0 = jnp.where(trow >= 2, pltpu.roll(u, 2, 0), e)
            conv = cw_ref[0:1, sl] * tap0 + cw_ref[1:2, sl] * tap1 + cw_ref[2:3, sl] * u
            ya_ref[:, sl] = (a_b * conv * _silu(a_z)).astype(BF16)
            u_ref[:, sl] = u

    lg = lg_ref[0][:, 0:1]
    same = (row >> 2) == (col >> 2)
    dd = trow - (col & 3)
    mask = jnp.where(same, dd, -1) >= 0
    decay = jnp.where(mask, jnp.exp(lg * jnp.maximum(dd, 0).astype(F32)), 0.0)
    tcol = (lax.broadcasted_iota(jnp.int32, (SR, 1), 0) & 3).astype(F32)
    cosf = cos_ref[...]
    sins = sin_ref[...]
    qr = _rope(pq_ref[...], cosf, sins)
    kr = _rope(pk_ref[...], cosf, sins) * (DH_B ** -0.5)
    qb = qr.astype(BF16)
    kb = kr.astype(BF16)
    vb = pv_ref[...].astype(BF16)
    sc = _dot_nt(qb, kb) * decay
    inner = _dot(sc.astype(BF16), vb)
    kdt = (kr * jnp.exp(lg * (DEC_SEQ - 1.0 - tcol))).T
    gamma_l = jnp.exp(lg * float(DEC_SEQ))
    lane_b = col >> 2
    sub = lax.broadcasted_iota(jnp.int32, (8, DH_B), 0)
    for g in range(SR // 8):
        q8 = qb[8 * g:8 * g + 8, :]
        res = []
        for beta in range(2):
            b = 2 * g + beta
            s_old = s_ref[b, 0]
            res.append(_dot(q8, s_old.astype(BF16)))
            lhs = jnp.where(lane_b == b, kdt, 0.0).astype(BF16)
            so_ref[b, 0] = gamma_l * s_old + _dot(lhs, vb)
        cross_scr[8 * g:8 * g + 8, :] = jnp.where(sub < DEC_SEQ, res[0], res[1])
    o = inner + cross_scr[...] * jnp.exp(lg * (tcol + 1.0))
    yb_ref[...] = (_head_norm(o, gret_ref[...]) * _silu(pz_ref[...])).astype(BF16)


def _even_sample(p, st_exp, s_state, conv_w, g_ret, cosf, sins, lg_tab):
    nb = DEC_BATCH // SB
    qoff = 4 * W_A // DH_B
    hb = W_B // DH_B
    const2 = lambda i, h: (0, 0)
    return pl.pallas_call(
        _even_sample_kernel,
        grid=(nb, H_B),
        in_specs=[pl.BlockSpec((SR, 4 * W_A), lambda i, h: (i, 0)),
                  pl.BlockSpec((SR, DH_B), lambda i, h: (i, qoff + h)),
                  pl.BlockSpec((SR, DH_B), lambda i, h: (i, qoff + hb + h)),
                  pl.BlockSpec((SR, DH_B), lambda i, h: (i, qoff + 2 * hb + h)),
                  pl.BlockSpec((SR, DH_B), lambda i, h: (i, qoff + 3 * hb + h)),
                  pl.BlockSpec((SR, W_A), lambda i, h: (i, 0)),
                  pl.BlockSpec((SB, 1, DH_B, DH_B), lambda i, h: (i, h, 0, 0)),
                  pl.BlockSpec((CONV_W, W_A), const2),
                  pl.BlockSpec((1, DH_B), lambda i, h: (0, h)),
                  pl.BlockSpec((SR, DH_B), const2),
                  pl.BlockSpec((SR, DH_B), const2),
                  pl.BlockSpec((1, 1, 128), lambda i, h: (h, 0, 0))],
        out_specs=(pl.BlockSpec((SR, W_A), lambda i, h: (i, 0)),
                   pl.BlockSpec((SR, W_A), lambda i, h: (i, 0)),
                   pl.BlockSpec((SR, DH_B), lambda i, h: (i, h)),
                   pl.BlockSpec((SB, 1, DH_B, DH_B), lambda i, h: (i, h, 0, 0))),
        out_shape=(jax.ShapeDtypeStruct((DEC_BATCH * DEC_SEQ, W_A), BF16),
                   jax.ShapeDtypeStruct((DEC_BATCH * DEC_SEQ, W_A), F32),
                   jax.ShapeDtypeStruct((DEC_BATCH * DEC_SEQ, W_B), BF16),
                   jax.ShapeDtypeStruct((DEC_BATCH, H_B, DH_B, DH_B), F32)),
        scratch_shapes=[pltpu.VMEM((SR, DH_B), F32)],
        compiler_params=_params(("arbitrary", "arbitrary")),
        name="even_sample",
    )(p, p, p, p, p, st_exp, s_state, conv_w, g_ret, cosf, sins, lg_tab)


def _odd_prompt_kernel(p_ref, bg_ref, gm_ref, lng_ref, lnb_ref, ws_ref, bst_ref,
                       yc_ref, yd_ref, c_ref, n_ref, m_ref):
    c = pl.program_id(1)
    L = CHUNK

    @pl.when(c == 0)
    def _():
        c_ref[...] = jnp.zeros_like(c_ref)
        n_ref[...] = jnp.zeros_like(n_ref)
        m_ref[...] = jnp.zeros_like(m_ref)

    row = lax.broadcasted_iota(jnp.int32, (L, L), 0)
    col = lax.broadcasted_iota(jnp.int32, (L, L), 1)
    tri = row >= col

    pre = p_ref[:, GATE_COL:GATE_COL + 128] + bg_ref[...]
    lf = _log_sigmoid(pre)
    b_c = _dot_hi(jnp.where(tri, 1.0, 0.0), lf)
    b_r = b_c.T
    pre_r = pre.T
    for h in range(H_C):
        bc = b_c[:, H_C + h:H_C + h + 1]
        br = b_r[H_C + h:H_C + h + 1, :]
        igr = pre_r[h:h + 1, :]
        igc = pre[:, h:h + 1]
        m_prev = m_ref[0, h:h + 1, 0:1]
        log_d = jnp.where(tri, bc - br + igr, NEG_INF)
        log_inter = bc + m_prev
        m_t = jnp.maximum(log_inter, jnp.max(log_d, axis=-1, keepdims=True))
        w = jnp.exp(log_d - m_t)
        w_inter = jnp.exp(log_inter - m_t)
        q = p_ref[:, h * DQK_C:(h + 1) * DQK_C] * (DQK_C ** -0.5)
        k = p_ref[:, H_C * DQK_C + h * DQK_C:H_C * DQK_C + (h + 1) * DQK_C]
        v = p_ref[:, 2 * H_C * DQK_C + h * DV_C:2 * H_C * DQK_C + (h + 1) * DV_C]
        z = p_ref[:, 2 * H_C * DQK_C + W_C + h * DV_C:2 * H_C * DQK_C + W_C + (h + 1) * DV_C]
        qb = q.astype(BF16)
        kb = k.astype(BF16)
        vb = v.astype(BF16)
        sc = _dot_nt(qb, kb) * w
        c_old = c_ref[0, h]
        n_old = n_ref[0, h:h + 1, :]
        num = _dot(sc.astype(BF16), vb) + w_inter * _dot_nt(qb, c_old.astype(BF16))
        den = jnp.sum(sc, axis=-1, keepdims=True) + w_inter * jnp.sum(q * n_old, axis=-1, keepdims=True)
        hh = num / jnp.maximum(jnp.abs(den), jnp.exp(-m_t))
        m_new = m_t[L - 1:L, :]
        b_last = bc[L - 1:L, :]
        w_end = jnp.exp(b_last - bc + igc - m_new)
        cd = jnp.exp(b_last + m_prev - m_new)
        c_ref[0, h] = cd * c_old + _dot_tn((v * w_end).astype(BF16), kb)
        n_ref[0, h:h + 1, :] = cd * n_old + jnp.sum(w_end * k, axis=0, keepdims=True)
        m_ref[0, h:h + 1, :] = jnp.broadcast_to(m_new, (1, 128))
        sl = slice(h * DV_C, (h + 1) * DV_C)
        yc_ref[:, sl] = (_head_norm(hh, gm_ref[0:1, sl]) * _silu(z)).astype(BF16)

    dbase = 2 * H_C * DQK_C + 2 * W_C
    dv = p_ref[:, dbase + W_D:dbase + 2 * W_D]
    mu = jnp.mean(dv, axis=-1, keepdims=True)
    xc = dv - mu
    var = jnp.mean(xc * xc, axis=-1, keepdims=True)
    rstd = lax.rsqrt(var + EPS)
    for g in range(G_D):
        sl = slice(g * 128, (g + 1) * 128)
        vn = xc[:, sl] * rstd * lng_ref[0:1, sl] + lnb_ref[0:1, sl]
        ws = jnp.where(tri, ws_ref[g], 0.0).astype(BF16)
        s = _dot(ws, vn.astype(BF16)) + bst_ref[:, g:g + 1]
        d_u = p_ref[:, dbase + g * 128:dbase + (g + 1) * 128]
        d_z = p_ref[:, dbase + 2 * W_D + g * 128:dbase + 2 * W_D + (g + 1) * 128]
        yd_ref[:, sl] = (d_u * s * _silu(d_z)).astype(BF16)


def _odd_prompt(p, bg, gm, lng, lnb, ws, bst):
    nc = SEQ // CHUNK
    rows = lambda b, c: (b * nc + c, 0)
    const2 = lambda b, c: (0, 0)
    return pl.pallas_call(
        _odd_prompt_kernel,
        grid=(BATCH, nc),
        in_specs=[pl.BlockSpec((CHUNK, O_PAD), rows),
                  pl.BlockSpec((1, 128), const2),
                  pl.BlockSpec((1, W_C), const2),
                  pl.BlockSpec((1, W_D), const2),
                  pl.BlockSpec((1, W_D), const2),
                  pl.BlockSpec((G_D, CHUNK, CHUNK), lambda b, c: (0, 0, 0)),
                  pl.BlockSpec((CHUNK, G_D), const2)],
        out_specs=(pl.BlockSpec((CHUNK, W_C), rows),
                   pl.BlockSpec((CHUNK, W_D), rows),
                   pl.BlockSpec((1, H_C, DV_C, DQK_C), lambda b, c: (b, 0, 0, 0)),
                   pl.BlockSpec((1, H_C, DQK_C), lambda b, c: (b, 0, 0)),
                   pl.BlockSpec((1, 8, 128), lambda b, c: (b, 0, 0))),
        out_shape=(jax.ShapeDtypeStruct((BATCH * SEQ, W_C), BF16),
                   jax.ShapeDtypeStruct((BATCH * SEQ, W_D), BF16),
                   jax.ShapeDtypeStruct((BATCH, H_C, DV_C, DQK_C), F32),
                   jax.ShapeDtypeStruct((BATCH, H_C, DQK_C), F32),
                   jax.ShapeDtypeStruct((BATCH, 8, 128), F32)),
        compiler_params=_params(("arbitrary", "arbitrary")),
        name="odd_prompt",
    )(p, bg, gm, lng, lnb, ws, bst)


def _odd_sample_kernel(pq_ref, pk_ref, pv_ref, pz_ref, pg_ref, pd_ref,
                       c_ref, nrow_ref, mrow_ref, bg_ref, gm_ref, lng_ref, lnb_ref,
                       rtab_ref, btab_ref,
                       yc_ref, yd_ref, vn_ref, co_ref, no_ref, mo_ref,
                       inter_scr):
    h = pl.program_id(1)
    row = lax.broadcasted_iota(jnp.int32, (SR, SR), 0)
    col = lax.broadcasted_iota(jnp.int32, (SR, SR), 1)
    trow = row & 3

    @pl.when(h == 0)
    def _():
        dv = pd_ref[:, W_D:2 * W_D]
        mu = jnp.mean(dv, axis=-1, keepdims=True)
        xc = dv - mu
        var = jnp.mean(xc * xc, axis=-1, keepdims=True)
        rstd = lax.rsqrt(var + EPS)
        for g in range(G_D):
            sl = slice(g * 128, (g + 1) * 128)
            vn = xc[:, sl] * rstd * lng_ref[0:1, sl] + lnb_ref[0:1, sl]
            vn_ref[:, sl] = vn
            s = rtab_ref[0, :, sl] * vn + btab_ref[:, sl]
            for j in range(1, DEC_SEQ):
                s = s + jnp.where(trow >= j, rtab_ref[j, :, sl] * pltpu.roll(vn, j, 0), 0.0)
            d_u = pd_ref[:, g * 128:(g + 1) * 128]
            d_z = pd_ref[:, 2 * W_D + g * 128:2 * W_D + (g + 1) * 128]
            yd_ref[:, sl] = (d_u * s * _silu(d_z)).astype(BF16)

    same = (row >> 2) == (col >> 2)
    mask = jnp.where(same, trow - (col & 3), -1) >= 0
    pre = pg_ref[...] + bg_ref[...]
    lf = _log_sigmoid(pre)
    b_full = _dot_hi(jnp.where(mask, 1.0, 0.0), lf)
    sel_i = col == h
    sel_f = col == h + H_C
    ig_c = jnp.sum(jnp.where(sel_i, pre, 0.0), axis=-1, keepdims=True)
    b_c = jnp.sum(jnp.where(sel_f, b_full, 0.0), axis=-1, keepdims=True)
    sel_ir = row == h
    sel_fr = row == h + H_C
    ig_r = jnp.sum(jnp.where(sel_ir, pre.T, 0.0), axis=0, keepdims=True)
    b_r = jnp.sum(jnp.where(sel_fr, b_full.T, 0.0), axis=0, keepdims=True)
    m_prev = mrow_ref[0]
    log_d = jnp.where(mask, b_c - b_r + ig_r, NEG_INF)
    log_inter = b_c + m_prev
    m_t = jnp.maximum(log_inter, jnp.max(log_d, axis=-1, keepdims=True))
    w = jnp.exp(log_d - m_t)
    w_inter = jnp.exp(log_inter - m_t)
    q = pq_ref[...] * (DQK_C ** -0.5)
    k = pk_ref[...]
    v = pv_ref[...]
    qb = q.astype(BF16)
    kb = k.astype(BF16)
    vb = v.astype(BF16)
    sc = _dot_nt(qb, kb) * w
    sub = lax.broadcasted_iota(jnp.int32, (8, DV_C), 0)
    for g in range(SR // 8):
        q8 = qb[8 * g:8 * g + 8, :]
        r0 = _dot_nt(q8, c_ref[2 * g, 0].astype(BF16))
        r1 = _dot_nt(q8, c_ref[2 * g + 1, 0].astype(BF16))
        inter_scr[8 * g:8 * g + 8, :] = jnp.where(sub < DEC_SEQ, r0, r1)
    n_rows = nrow_ref[0]
    num = _dot(sc.astype(BF16), vb) + w_inter * inter_scr[...]
    den = jnp.sum(sc, axis=-1, keepdims=True) + w_inter * jnp.sum(q * n_rows, axis=-1, keepdims=True)
    hh = num / jnp.maximum(jnp.abs(den), jnp.exp(-m_t))
    yc_ref[...] = (_head_norm(hh, gm_ref[...]) * _silu(pz_ref[...])).astype(BF16)

    stats = jnp.where(col == 0, m_t, jnp.where(col == 1, b_c, 0.0))
    last = _dot_hi(jnp.where(col == (row | 3), 1.0, 0.0), stats)
    m_new = last[:, 0:1]
    b_last = last[:, 1:2]
    w_end = jnp.exp(b_last - b_c + ig_c - m_new)
    cd = jnp.exp(b_last + m_prev - m_new)
    mo_ref[0] = m_new
    no_ref[0] = cd * n_rows + _dot_hi(jnp.where(same, 1.0, 0.0), w_end * k)
    vwt = (v * w_end).T
    lane_b = lax.broadcasted_iota(jnp.int32, (DV_C, SR), 1) >> 2
    for b in range(SB):
        lhs = jnp.where(lane_b == b, vwt, 0.0).astype(BF16)
        cd_b = cd[4 * b + 3:4 * b + 4, :]
        co_ref[b, 0] = cd_b * c_ref[b, 0] + _dot(lhs, kb)


def _odd_sample(p, c_state, n_rows, m_rows, bg, gm, lng, lnb, rtab, btab):
    nb = DEC_BATCH // SB
    const2 = lambda i, h: (0, 0)
    koff = H_C * DQK_C // DQK_C
    voff = 2 * H_C * DQK_C // DV_C
    zoff = (2 * H_C * DQK_C + W_C) // DV_C
    return pl.pallas_call(
        _odd_sample_kernel,
        grid=(nb, H_C),
        in_specs=[pl.BlockSpec((SR, DQK_C), lambda i, h: (i, h)),
                  pl.BlockSpec((SR, DQK_C), lambda i, h: (i, koff + h)),
                  pl.BlockSpec((SR, DV_C), lambda i, h: (i, voff + h)),
                  pl.BlockSpec((SR, DV_C), lambda i, h: (i, zoff + h)),
                  pl.BlockSpec((SR, 128), lambda i, h: (i, GATE_COL // 128)),
                  pl.BlockSpec((SR, 3 * W_D), lambda i, h: (i, 1)),
                  pl.BlockSpec((SB, 1, DV_C, DQK_C), lambda i, h: (i, h, 0, 0)),
                  pl.BlockSpec((1, SR, DQK_C), lambda i, h: (h, i, 0)),
                  pl.BlockSpec((1, SR, 1), lambda i, h: (h, i, 0)),
                  pl.BlockSpec((1, 128), const2),
                  pl.BlockSpec((1, DV_C), lambda i, h: (0, h)),
                  pl.BlockSpec((1, W_D), const2),
                  pl.BlockSpec((1, W_D), const2),
                  pl.BlockSpec((DEC_SEQ, SR, W_D), lambda i, h: (0, 0, 0)),
                  pl.BlockSpec((SR, W_D), const2)],
        out_specs=(pl.BlockSpec((SR, DV_C), lambda i, h: (i, h)),
                   pl.BlockSpec((SR, W_D), lambda i, h: (i, 0)),
                   pl.BlockSpec((SR, W_D), lambda i, h: (i, 0)),
                   pl.BlockSpec((SB, 1, DV_C, DQK_C), lambda i, h: (i, h, 0, 0)),
                   pl.BlockSpec((1, SR, DQK_C), lambda i, h: (h, i, 0)),
                   pl.BlockSpec((1, SR, 1), lambda i, h: (h, i, 0))),
        out_shape=(jax.ShapeDtypeStruct((DEC_BATCH * DEC_SEQ, W_C), BF16),
                   jax.ShapeDtypeStruct((DEC_BATCH * DEC_SEQ, W_D), BF16),
                   jax.ShapeDtypeStruct((DEC_BATCH * DEC_SEQ, W_D), F32),
                   jax.ShapeDtypeStruct((DEC_BATCH, H_C, DV_C, DQK_C), F32),
                   jax.ShapeDtypeStruct((H_C, DEC_BATCH * DEC_SEQ, DQK_C), F32),
                   jax.ShapeDtypeStruct((H_C, DEC_BATCH * DEC_SEQ, 1), F32)),
        scratch_shapes=[pltpu.VMEM((SR, DV_C), F32)],
        compiler_params=_params(("arbitrary", "arbitrary")),
        name="odd_sample",
    )(p, p, p, p, p, p, c_state, n_rows, m_rows, bg, gm, lng, lnb, rtab, btab)


def _rope_tables(pos):
    inv = ROPE_BASE ** (-jnp.arange(0, DH_B, 2, dtype=F32) / DH_B)
    ang = pos.astype(F32)[:, None] * inv[None, :]
    cos = jnp.cos(ang)
    sin = jnp.sin(ang)
    return jnp.concatenate([cos, cos], axis=-1), jnp.concatenate([-sin, sin], axis=-1)


def kernel(x_prompt, x_sample, state_conv, state_ret, state_mlstm_C, state_mlstm_n, state_mlstm_m,
           norm_even, w_in_even, conv_w, ret_norm, w_out_even,
           norm_odd, w_in_odd, b_gate_odd, mlstm_norm, ln_v_g, ln_v_b,
           w_spatial, b_spatial, w_out_odd, norm_final):
    w_in_e = w_in_even[0].astype(BF16)
    w_out_e = w_out_even[0].astype(BF16)
    gcol = 2 * H_C * DQK_C + 2 * W_C
    w_o = w_in_odd[0]
    w_in_o = jnp.concatenate(
        [w_o[:, :gcol], w_o[:, gcol + 2 * H_C:], w_o[:, gcol:gcol + 2 * H_C],
         jnp.zeros((D_MODEL, O_PAD - O_IN), w_o.dtype)], axis=1).astype(BF16)
    w_out_o = w_out_odd[0].astype(BF16)
    g_even = norm_even[0][None, :]
    g_odd = norm_odd[0][None, :]
    g_fin = norm_final[None, :]
    cw = conv_w[0]
    g_ret = ret_norm[0][None, :]
    bg = jnp.concatenate([b_gate_odd[0], jnp.zeros((128 - 2 * H_C,), F32)])[None, :]
    gm = mlstm_norm[0][None, :]
    lng = ln_v_g[0][None, :]
    lnb = ln_v_b[0][None, :]
    ws = w_spatial[0]
    bst = b_spatial[0].T

    cos_p, sin_p = _rope_tables(jnp.arange(SEQ, dtype=jnp.int32))
    cos_s, sin_s = _rope_tables(PAST_LEN + jnp.arange(DEC_SEQ, dtype=jnp.int32))
    cos_s = jnp.tile(cos_s, (SB, 1))
    sin_s = jnp.tile(sin_s, (SB, 1))
    lg_tab = jnp.broadcast_to(jnp.asarray(LOG_GAMMA, F32)[:, None, None], (H_B, 1, 128))

    ws4 = ws[:, :DEC_SEQ, :DEC_SEQ]
    t_idx = jnp.arange(DEC_SEQ)
    rtab = []
    for j in range(DEC_SEQ):
        coef = ws4[:, t_idx, (t_idx - j) % DEC_SEQ]
        tab = jnp.repeat(coef.T[:, :, None], 128, axis=2).reshape(DEC_SEQ, W_D)
        rtab.append(jnp.tile(tab, (SB, 1)))
    rtab = jnp.stack(rtab)
    btab = jnp.tile(jnp.repeat(b_spatial[0][:, :DEC_SEQ].T[:, :, None], 128, axis=2)
                    .reshape(DEC_SEQ, W_D), (SB, 1))

    xp = x_prompt.reshape(BATCH * SEQ, D_MODEL)
    hp = _norm_cast(xp, g_even, 512)
    pp = _matmul(hp, w_in_e, 1024, 1024)
    ya, yb, conv_p, ret_p = _even_prompt(pp, cw, g_ret, cos_p, sin_p)
    x1, h1 = _outproj(ya, yb, w_out_e, xp, g_odd, 512, final=False)
    p2 = _matmul(h1, w_in_o, 1024, 1280)
    yc, yd, c_p, n_p, m_p = _odd_prompt(p2, bg, gm, lng, lnb, ws, bst)
    y_prompt = _outproj(yc, yd, w_out_o, x1, g_fin, 512, final=True)

    rs = DEC_BATCH * DEC_SEQ
    xs = x_sample.reshape(rs, D_MODEL)
    hs = _norm_cast(xs, g_even, 512)
    ps = _matmul(hs, w_in_e, 512, 1024)
    st_exp = jnp.pad(state_conv[0], ((0, 0), (0, DEC_SEQ - (CONV_W - 1)), (0, 0))).reshape(rs, W_A)
    ya_s, u_s, yb_s, ret_s = _even_sample(ps, st_exp, state_ret[0], cw, g_ret, cos_s, sin_s, lg_tab)
    x1s, h1s = _outproj(ya_s, yb_s, w_out_e, xs, g_odd, 512, final=False)
    p2s = _matmul(h1s, w_in_o, 512, 1280)
    n_rows = jnp.repeat(jnp.transpose(state_mlstm_n[0], (1, 0, 2)), DEC_SEQ, axis=1)
    m_rows = jnp.repeat(state_mlstm_m[0].T, DEC_SEQ, axis=1)[:, :, None]
    yc_s, yd_s, vn_s, c_s, no_s, mo_s = _odd_sample(
        p2s, state_mlstm_C[0], n_rows, m_rows, bg, gm, lng, lnb, rtab, btab)
    y_sample = _outproj(yc_s, yd_s, w_out_o, x1s, g_fin, 512, final=True)

    conv_s = u_s.reshape(DEC_BATCH, DEC_SEQ, W_A)[:, DEC_SEQ - (CONV_W - 1):, :]
    n_s = jnp.transpose(no_s[:, DEC_SEQ - 1::DEC_SEQ, :], (1, 0, 2))
    m_s = mo_s[:, DEC_SEQ - 1::DEC_SEQ, 0].T
    return (y_prompt.reshape(BATCH, SEQ, D_MODEL),
            y_sample.reshape(DEC_BATCH, DEC_SEQ, D_MODEL),
            conv_p[None], conv_s[None],
            ret_p[None], ret_s[None],
            c_p[None], c_s[None],
            n_p[None], n_s[None],
            m_p[:, :H_C, 0][None], m_s[None],
            vn_s.reshape(DEC_BATCH, DEC_SEQ, W_D)[None])
```

```python
import functools
import math

import jax
import jax.numpy as jnp
from jax import lax
from jax.experimental import pallas as pl
from jax.experimental.pallas import tpu as pltpu

F32 = jnp.float32
BF16 = jnp.bfloat16

D_MODEL = 2048
BATCH = 4
SEQ = 2048
DEC_BATCH = 128
DEC_SEQ = 4
PAST_LEN = 16384
W_A = 1024
CONV_W = 3
W_B = 1024
H_B = 8
DH_B = 128
E_IN = 8192
W_C = 1024
H_C = 4
DV_C = 256
DQK_C = 128
W_D = 1024
G_D = 8
CHUNK = 128
O_GATE = 2 * H_C * DQK_C + 2 * W_C
N_GATE = 2 * H_C
O_N = O_GATE + 3 * W_D
ROPE_BASE = 10000.0
EPS = 1e-6
LOG_GAMMA = tuple(math.log(1.0 - 2.0 ** (-5.0 - h)) for h in range(H_B))
NEG_INF = float("-inf")
VMEM_LIMIT = 56 * 1024 * 1024

NT_DIMS = (((1,), (1,)), ((), ()))
TN_DIMS = (((0,), (0,)), ((), ()))


def _silu(z):
    return z * (1.0 / (1.0 + jnp.exp(-z)))


def _log_sigmoid(x):
    return jnp.minimum(x, 0.0) - jnp.log1p(jnp.exp(-jnp.abs(x)))


def _dot(a, b):
    return jnp.dot(a, b, preferred_element_type=F32)


def _dot_nt(a, b):
    return lax.dot_general(a, b, NT_DIMS, preferred_element_type=F32)


def _dot_tn(a, b):
    return lax.dot_general(a, b, TN_DIMS, preferred_element_type=F32)


def _dot_hi(a, b):
    return jnp.dot(a, b, preferred_element_type=F32, precision=lax.Precision.HIGHEST)


def _head_norm(o, g):
    mu = jnp.mean(o, axis=-1, keepdims=True)
    oc = o - mu
    var = jnp.mean(oc * oc, axis=-1, keepdims=True)
    return oc * lax.rsqrt(var + EPS) * g


def _params(sem):
    return pltpu.CompilerParams(dimension_semantics=sem, vmem_limit_bytes=VMEM_LIMIT)


def _norm_cast_kernel(x_ref, g_ref, h_ref):
    x = x_ref[...]
    ms = jnp.mean(x * x, axis=-1, keepdims=True)
    h_ref[...] = (x * lax.rsqrt(ms + EPS) * g_ref[...]).astype(BF16)


def _norm_cast(x, g, tm):
    m, d = x.shape
    return pl.pallas_call(
        _norm_cast_kernel,
        grid=(m // tm,),
        in_specs=[pl.BlockSpec((tm, d), lambda i: (i, 0)),
                  pl.BlockSpec((1, d), lambda i: (0, 0))],
        out_specs=pl.BlockSpec((tm, d), lambda i: (i, 0)),
        out_shape=jax.ShapeDtypeStruct((m, d), BF16),
        compiler_params=_params(("arbitrary",)),
        name="norm_cast",
    )(x, g)


IN_TM = 1024
IN_TN = 1024


def _in_proj_kernel(*refs, n_prompt, shift_from, shift):
    if shift:
        hp_ref, hs_ref, w_ref, wn_ref, op_ref, os_ref, wb = refs
    else:
        hp_ref, hs_ref, w_ref, op_ref, os_ref, wb = refs
    j = pl.program_id(0)
    i = pl.program_id(1)

    if shift:
        @pl.when(jnp.logical_and(i == 0, j < shift_from))
        def _():
            wb[...] = w_ref[...].astype(BF16)

        @pl.when(jnp.logical_and(i == 0, j >= shift_from))
        def _():
            wb[...] = jnp.concatenate([w_ref[:, shift:], wn_ref[:, 0:shift]], axis=1).astype(BF16)
    else:
        @pl.when(i == 0)
        def _():
            wb[...] = w_ref[...].astype(BF16)

    @pl.when(i < n_prompt)
    def _():
        op_ref[...] = _dot(hp_ref[...], wb[...])

    @pl.when(i == n_prompt)
    def _():
        os_ref[...] = _dot(hs_ref[...], wb[...])


def _in_proj(hp, hs, w, n_out, shift_from=0, shift=0):
    mp, k = hp.shape
    ms = hs.shape[0]
    n_prompt = mp // IN_TM
    last = n_prompt - 1
    in_specs = [pl.BlockSpec((IN_TM, k), lambda j, i: (jnp.minimum(i, last), 0)),
                pl.BlockSpec((ms, k), lambda j, i: (0, 0)),
                pl.BlockSpec((k, IN_TN), lambda j, i: (0, j))]
    args = [hp, hs, w]
    if shift:
        in_specs.append(pl.BlockSpec((k, 128), lambda j, i: (0, (j + 1) * (IN_TN // 128))))
        args.append(w)
    return pl.pallas_call(
        functools.partial(_in_proj_kernel, n_prompt=n_prompt, shift_from=shift_from, shift=shift),
        grid=(n_out // IN_TN, n_prompt + 1),
        in_specs=in_specs,
        out_specs=(pl.BlockSpec((IN_TM, IN_TN), lambda j, i: (jnp.minimum(i, last), j)),
                   pl.BlockSpec((ms, IN_TN), lambda j, i: (0, j))),
        out_shape=(jax.ShapeDtypeStruct((mp, n_out), F32),
                   jax.ShapeDtypeStruct((ms, n_out), F32)),
        scratch_shapes=[pltpu.VMEM((k, IN_TN), BF16)],
        compiler_params=_params(("arbitrary", "arbitrary")),
        name="in_proj",
    )(*args)


def _outproj_kernel(ya_ref, yb_ref, w_ref, x_ref, g_ref, *out_refs, final):
    half = ya_ref.shape[1]
    acc = _dot(ya_ref[...], w_ref[0:half, :]) + _dot(yb_ref[...], w_ref[half:2 * half, :])
    x1 = x_ref[...] + acc
    ms = jnp.mean(x1 * x1, axis=-1, keepdims=True)
    hn = x1 * lax.rsqrt(ms + EPS) * g_ref[...]
    if final:
        out_refs[0][...] = hn
    else:
        out_refs[0][...] = x1
        out_refs[1][...] = hn.astype(BF16)


def _outproj(ya, yb, w, x, g, tm, final):
    m, half = ya.shape
    d = w.shape[1]
    row = lambda i: (i, 0)
    if final:
        out_shape = jax.ShapeDtypeStruct((m, d), F32)
        out_specs = pl.BlockSpec((tm, d), row)
    else:
        out_shape = (jax.ShapeDtypeStruct((m, d), F32), jax.ShapeDtypeStruct((m, d), BF16))
        out_specs = (pl.BlockSpec((tm, d), row), pl.BlockSpec((tm, d), row))
    return pl.pallas_call(
        functools.partial(_outproj_kernel, final=final),
        grid=(m // tm,),
        in_specs=[pl.BlockSpec((tm, half), row),
                  pl.BlockSpec((tm, half), row),
                  pl.BlockSpec((2 * half, d), lambda i: (0, 0)),
                  pl.BlockSpec((tm, d), row),
                  pl.BlockSpec((1, d), lambda i: (0, 0))],
        out_specs=out_specs,
        out_shape=out_shape,
        compiler_params=_params(("arbitrary",)),
        name="out_proj_final" if final else "out_proj",
    )(ya, yb, w, x, g)


def _rope(x, cosf, sins):
    return x * cosf + pltpu.roll(x, DH_B // 2, 1) * sins


def _even_prompt_kernel(p_ref, cw_ref, gret_ref, cos_ref, sin_ref,
                        ya_ref, yb_ref, conv_ref, s_ref, ubuf):
    c = pl.program_id(1)
    L = CHUNK

    @pl.when(c == 0)
    def _():
        ubuf[0:8, :] = jnp.zeros((8, W_A), F32)
        s_ref[...] = jnp.zeros_like(s_ref)

    for j in range(W_A // 128):
        sl = slice(j * 128, (j + 1) * 128)
        a_b = p_ref[:, j * 128:(j + 1) * 128]
        a_c = p_ref[:, W_A + j * 128:W_A + (j + 1) * 128]
        a_x = p_ref[:, 2 * W_A + j * 128:2 * W_A + (j + 1) * 128]
        a_z = p_ref[:, 3 * W_A + j * 128:3 * W_A + (j + 1) * 128]
        u = a_c * a_x
        ubuf[8:8 + L, sl] = u
        t0 = ubuf[6:6 + L, sl]
        t1 = ubuf[7:7 + L, sl]
        conv = cw_ref[0:1, sl] * t0 + cw_ref[1:2, sl] * t1 + cw_ref[2:3, sl] * u
        ya_ref[:, sl] = (a_b * conv * _silu(a_z)).astype(BF16)
        ubuf[0:8, sl] = u[L - 8:L, :]
    conv_ref[0] = ubuf[6:8, :]

    cosf = cos_ref[...]
    sins = sin_ref[...]
    row = lax.broadcasted_iota(jnp.int32, (L, L), 0)
    col = lax.broadcasted_iota(jnp.int32, (L, L), 1)
    causal = row >= col
    diff = jnp.maximum(row - col, 0).astype(F32)
    ti = lax.broadcasted_iota(jnp.int32, (L, 1), 0).astype(F32)
    base = 4 * W_A
    for h in range(H_B):
        lg = LOG_GAMMA[h]
        sl = slice(h * DH_B, (h + 1) * DH_B)
        q = p_ref[:, base + h * DH_B:base + (h + 1) * DH_B]
        k = p_ref[:, base + W_B + h * DH_B:base + W_B + (h + 1) * DH_B]
        v = p_ref[:, base + 2 * W_B + h * DH_B:base + 2 * W_B + (h + 1) * DH_B]
        z = p_ref[:, base + 3 * W_B + h * DH_B:base + 3 * W_B + (h + 1) * DH_B]
        qr = _rope(q, cosf, sins)
        kr = _rope(k, cosf, sins) * (DH_B ** -0.5)
        decay = jnp.where(causal, jnp.exp(lg * diff), 0.0)
        qb = qr.astype(BF16)
        kb = kr.astype(BF16)
        vb = v.astype(BF16)
        sc = _dot_nt(qb, kb) * decay
        inner = _dot(sc.astype(BF16), vb)
        s_old = s_ref[0, h]
        cross = _dot(qb, s_old.astype(BF16)) * jnp.exp(lg * (ti + 1.0))
        kd = (kr * jnp.exp(lg * (L - 1.0 - ti))).astype(BF16)
        s_ref[0, h] = math.exp(lg * L) * s_old + _dot_tn(kd, vb)
        o = inner + cross
        yb_ref[:, sl] = (_head_norm(o, gret_ref[0:1, sl]) * _silu(z)).astype(BF16)


def _even_prompt(p, conv_w, g_ret, cosf, sins):
    nc = SEQ // CHUNK
    rows = lambda b, c: (b * nc + c, 0)
    const2 = lambda b, c: (0, 0)
    return pl.pallas_call(
        _even_prompt_kernel,
        grid=(BATCH, nc),
        in_specs=[pl.BlockSpec((CHUNK, E_IN), rows),
                  pl.BlockSpec((CONV_W, W_A), const2),
                  pl.BlockSpec((1, W_B), const2),
                  pl.BlockSpec((CHUNK, DH_B), lambda b, c: (c, 0)),
                  pl.BlockSpec((CHUNK, DH_B), lambda b, c: (c, 0))],
        out_specs=(pl.BlockSpec((CHUNK, W_A), rows),
                   pl.BlockSpec((CHUNK, W_B), rows),
                   pl.BlockSpec((1, CONV_W - 1, W_A), lambda b, c: (b, 0, 0)),
                   pl.BlockSpec((1, H_B, DH_B, DH_B), lambda b, c: (b, 0, 0, 0))),
        out_shape=(jax.ShapeDtypeStruct((BATCH * SEQ, W_A), BF16),
                   jax.ShapeDtypeStruct((BATCH * SEQ, W_B), BF16),
                   jax.ShapeDtypeStruct((BATCH, CONV_W - 1, W_A), F32),
                   jax.ShapeDtypeStruct((BATCH, H_B, DH_B, DH_B), F32)),
        scratch_shapes=[pltpu.VMEM((CHUNK + 8, W_A), F32)],
        compiler_params=_params(("arbitrary", "arbitrary")),
        name="even_prompt",
    )(p, conv_w, g_ret, cosf, sins)


SB = 32
SR = SB * DEC_SEQ


def _even_sample_kernel(pa_ref, pq_ref, pk_ref, pv_ref, pz_ref, st_ref, s_ref,
                        cw_ref, gret_ref, cos_ref, sin_ref, lg_ref,
                        ya_ref, u_ref, yb_ref, so_ref, cross_scr):
    h = pl.program_id(1)
    row = lax.broadcasted_iota(jnp.int32, (SR, SR), 0)
    col = lax.broadcasted_iota(jnp.int32, (SR, SR), 1)
    trow = row & 3

    @pl.when(h == 0)
    def _():
        for j in range(W_A // 128):
            sl = slice(j * 128, (j + 1) * 128)
            a_b = pa_ref[:, j * 128:(j + 1) * 128]
            a_c = pa_ref[:, W_A + j * 128:W_A + (j + 1) * 128]
            a_x = pa_ref[:, 2 * W_A + j * 128:2 * W_A + (j + 1) * 128]
            a_z = pa_ref[:, 3 * W_A + j * 128:3 * W_A + (j + 1) * 128]
            u = a_c * a_x
            e = st_ref[:, sl]
            tap1 = jnp.where(trow >= 1, pltpu.roll(u, 1, 0), pltpu.roll(e, SR - 1, 0))
            tap0 = jnp.where(trow >= 2, pltpu.roll(u, 2, 0), e)
            conv = cw_ref[0:1, sl] * tap0 + cw_ref[1:2, sl] * tap1 + cw_ref[2:3, sl] * u
            ya_ref[:, sl] = (a_b * conv * _silu(a_z)).astype(BF16)
            u_ref[:, sl] = u

    lg = lg_ref[0][:, 0:1]
    same = (row >> 2) == (col >> 2)
    dd = trow - (col & 3)
    mask = jnp.where(same, dd, -1) >= 0
    decay = jnp.where(mask, jnp.exp(lg * jnp.maximum(dd, 0).astype(F32)), 0.0)
    tcol = (lax.broadcasted_iota(jnp.int32, (SR, 1), 0) & 3).astype(F32)
    cosf = cos_ref[...]
    sins = sin_ref[...]
    qr = _rope(pq_ref[...], cosf, sins)
    kr = _rope(pk_ref[...], cosf, sins) * (DH_B ** -0.5)
    qb = qr.astype(BF16)
    kb = kr.astype(BF16)
    vb = pv_ref[...].astype(BF16)
    sc = _dot_nt(qb, kb) * decay
    inner = _dot(sc.astype(BF16), vb)
    kdt = (kr * jnp.exp(lg * (DEC_SEQ - 1.0 - tcol))).T
    gamma_l = jnp.exp(lg * float(DEC_SEQ))
    lane_b = col >> 2
    sub = lax.broadcasted_iota(jnp.int32, (8, DH_B), 0)
    for g in range(SR // 8):
        q8 = qb[8 * g:8 * g + 8, :]
        res = []
        for beta in range(2):
            b = 2 * g + beta
            s_old = s_ref[b, 0]
            res.append(_dot(q8, s_old.astype(BF16)))
            lhs = jnp.where(lane_b == b, kdt, 0.0).astype(BF16)
            so_ref[b, 0] = gamma_l * s_old + _dot(lhs, vb)
        cross_scr[8 * g:8 * g + 8, :] = jnp.where(sub < DEC_SEQ, res[0], res[1])
    o = inner + cross_scr[...] * jnp.exp(lg * (tcol + 1.0))
    yb_ref[...] = (_head_norm(o, gret_ref[...]) * _silu(pz_ref[...])).astype(BF16)


def _even_sample(p, st_exp, s_state, conv_w, g_ret, cosf, sins, lg_tab):
    nb = DEC_BATCH // SB
    qoff = 4 * W_A // DH_B
    hb = W_B // DH_B
    const2 = lambda i, h: (0, 0)
    return pl.pallas_call(
        _even_sample_kernel,
        grid=(nb, H_B),
        in_specs=[pl.BlockSpec((SR, 4 * W_A), lambda i, h: (i, 0)),
                  pl.BlockSpec((SR, DH_B), lambda i, h: (i, qoff + h)),
                  pl.BlockSpec((SR, DH_B), lambda i, h: (i, qoff + hb + h)),
                  pl.BlockSpec((SR, DH_B), lambda i, h: (i, qoff + 2 * hb + h)),
                  pl.BlockSpec((SR, DH_B), lambda i, h: (i, qoff + 3 * hb + h)),
                  pl.BlockSpec((SR, W_A), lambda i, h: (i, 0)),
                  pl.BlockSpec((SB, 1, DH_B, DH_B), lambda i, h: (i, h, 0, 0)),
                  pl.BlockSpec((CONV_W, W_A), const2),
                  pl.BlockSpec((1, DH_B), lambda i, h: (0, h)),
                  pl.BlockSpec((SR, DH_B), const2),
                  pl.BlockSpec((SR, DH_B), const2),
                  pl.BlockSpec((1, 1, 128), lambda i, h: (h, 0, 0))],
        out_specs=(pl.BlockSpec((SR, W_A), lambda i, h: (i, 0)),
                   pl.BlockSpec((SR, W_A), lambda i, h: (i, 0)),
                   pl.BlockSpec((SR, DH_B), lambda i, h: (i, h)),
                   pl.BlockSpec((SB, 1, DH_B, DH_B), lambda i, h: (i, h, 0, 0))),
        out_shape=(jax.ShapeDtypeStruct((DEC_BATCH * DEC_SEQ, W_A), BF16),
                   jax.ShapeDtypeStruct((DEC_BATCH * DEC_SEQ, W_A), F32),
                   jax.ShapeDtypeStruct((DEC_BATCH * DEC_SEQ, W_B), BF16),
                   jax.ShapeDtypeStruct((DEC_BATCH, H_B, DH_B, DH_B), F32)),
        scratch_shapes=[pltpu.VMEM((SR, DH_B), F32)],
        compiler_params=_params(("arbitrary", "arbitrary")),
        name="even_sample",
    )(p, p, p, p, p, st_exp, s_state, conv_w, g_ret, cosf, sins, lg_tab)


def _odd_prompt_kernel(p_ref, h_ref, wg_ref, bg_ref, gm_ref, lng_ref, lnb_ref, ws_ref, bst_ref,
                       yc_ref, yd_ref, c_ref, n_ref, m_ref, wgb):
    c = pl.program_id(1)
    L = CHUNK

    @pl.when(jnp.logical_and(pl.program_id(0) == 0, c == 0))
    def _():
        wgb[...] = wg_ref[...].astype(BF16)

    @pl.when(c == 0)
    def _():
        c_ref[...] = jnp.zeros_like(c_ref)
        n_ref[...] = jnp.zeros_like(n_ref)
        m_ref[...] = jnp.zeros_like(m_ref)

    row = lax.broadcasted_iota(jnp.int32, (L, L), 0)
    col = lax.broadcasted_iota(jnp.int32, (L, L), 1)
    tri = row >= col

    pre = _dot(h_ref[...], wgb[...]) + bg_ref[...]
    lf = _log_sigmoid(pre)
    b_c = _dot_hi(jnp.where(tri, 1.0, 0.0), lf)
    b_r = b_c.T
    pre_r = pre.T
    for h in range(H_C):
        bc = b_c[:, H_C + h:H_C + h + 1]
        br = b_r[H_C + h:H_C + h + 1, :]
        igr = pre_r[h:h + 1, :]
        igc = pre[:, h:h + 1]
        m_prev = m_ref[0, h:h + 1, 0:1]
        log_d = jnp.where(tri, bc - br + igr, NEG_INF)
        log_inter = bc + m_prev
        m_t = jnp.maximum(log_inter, jnp.max(log_d, axis=-1, keepdims=True))
        w = jnp.exp(log_d - m_t)
        w_inter = jnp.exp(log_inter - m_t)
        q = p_ref[:, h * DQK_C:(h + 1) * DQK_C] * (DQK_C ** -0.5)
        k = p_ref[:, H_C * DQK_C + h * DQK_C:H_C * DQK_C + (h + 1) * DQK_C]
        v = p_ref[:, 2 * H_C * DQK_C + h * DV_C:2 * H_C * DQK_C + (h + 1) * DV_C]
        z = p_ref[:, 2 * H_C * DQK_C + W_C + h * DV_C:2 * H_C * DQK_C + W_C + (h + 1) * DV_C]
        qb = q.astype(BF16)
        kb = k.astype(BF16)
        vb = v.astype(BF16)
        sc = _dot_nt(qb, kb) * w
        c_old = c_ref[0, h]
        n_old = n_ref[0, h:h + 1, :]
        num = _dot(sc.astype(BF16), vb) + w_inter * _dot_nt(qb, c_old.astype(BF16))
        den = jnp.sum(sc, axis=-1, keepdims=True) + w_inter * jnp.sum(q * n_old, axis=-1, keepdims=True)
        hh = num / jnp.maximum(jnp.abs(den), jnp.exp(-m_t))
        m_new = m_t[L - 1:L, :]
        b_last = bc[L - 1:L, :]
        w_end = jnp.exp(b_last - bc + igc - m_new)
        cd = jnp.exp(b_last + m_prev - m_new)
        c_ref[0, h] = cd * c_old + _dot_tn((v * w_end).astype(BF16), kb)
        n_ref[0, h:h + 1, :] = cd * n_old + jnp.sum(w_end * k, axis=0, keepdims=True)
        m_ref[0, h:h + 1, :] = jnp.broadcast_to(m_new, (1, 128))
        sl = slice(h * DV_C, (h + 1) * DV_C)
        yc_ref[:, sl] = (_head_norm(hh, gm_ref[0:1, sl]) * _silu(z)).astype(BF16)

    dbase = 2 * H_C * DQK_C + 2 * W_C
    dv = p_ref[:, dbase + W_D:dbase + 2 * W_D]
    mu = jnp.mean(dv, axis=-1, keepdims=True)
    xc = dv - mu
    var = jnp.mean(xc * xc, axis=-1, keepdims=True)
    rstd = lax.rsqrt(var + EPS)
    for g in range(G_D):
        sl = slice(g * 128, (g + 1) * 128)
        vn = xc[:, sl] * rstd * lng_ref[0:1, sl] + lnb_ref[0:1, sl]
        ws = jnp.where(tri, ws_ref[g], 0.0).astype(BF16)
        s = _dot(ws, vn.astype(BF16)) + bst_ref[:, g:g + 1]
        d_u = p_ref[:, dbase + g * 128:dbase + (g + 1) * 128]
        d_z = p_ref[:, dbase + 2 * W_D + g * 128:dbase + 2 * W_D + (g + 1) * 128]
        yd_ref[:, sl] = (d_u * s * _silu(d_z)).astype(BF16)


def _odd_prompt(p, h, w_o, bg, gm, lng, lnb, ws, bst):
    nc = SEQ // CHUNK
    rows = lambda b, c: (b * nc + c, 0)
    const2 = lambda b, c: (0, 0)
    return pl.pallas_call(
        _odd_prompt_kernel,
        grid=(BATCH, nc),
        in_specs=[pl.BlockSpec((CHUNK, O_N), rows),
                  pl.BlockSpec((CHUNK, D_MODEL), rows),
                  pl.BlockSpec((D_MODEL, 128), lambda b, c: (0, O_GATE // 128)),
                  pl.BlockSpec((1, 128), const2),
                  pl.BlockSpec((1, W_C), const2),
                  pl.BlockSpec((1, W_D), const2),
                  pl.BlockSpec((1, W_D), const2),
                  pl.BlockSpec((G_D, CHUNK, CHUNK), lambda b, c: (0, 0, 0)),
                  pl.BlockSpec((CHUNK, G_D), const2)],
        out_specs=(pl.BlockSpec((CHUNK, W_C), rows),
                   pl.BlockSpec((CHUNK, W_D), rows),
                   pl.BlockSpec((1, H_C, DV_C, DQK_C), lambda b, c: (b, 0, 0, 0)),
                   pl.BlockSpec((1, H_C, DQK_C), lambda b, c: (b, 0, 0)),
                   pl.BlockSpec((1, 8, 128), lambda b, c: (b, 0, 0))),
        out_shape=(jax.ShapeDtypeStruct((BATCH * SEQ, W_C), BF16),
                   jax.ShapeDtypeStruct((BATCH * SEQ, W_D), BF16),
                   jax.ShapeDtypeStruct((BATCH, H_C, DV_C, DQK_C), F32),
                   jax.ShapeDtypeStruct((BATCH, H_C, DQK_C), F32),
                   jax.ShapeDtypeStruct((BATCH, 8, 128), F32)),
        scratch_shapes=[pltpu.VMEM((D_MODEL, 128), BF16)],
        compiler_params=_params(("arbitrary", "arbitrary")),
        name="odd_prompt",
    )(p, h, w_o, bg, gm, lng, lnb, ws, bst)


def _odd_sample_kernel(pq_ref, pk_ref, pv_ref, pz_ref, h_ref, wg_ref, pd_ref,
                       c_ref, nrow_ref, mrow_ref, bg_ref, gm_ref, lng_ref, lnb_ref,
                       rtab_ref, btab_ref,
                       yc_ref, yd_ref, vn_ref, co_ref, no_ref, mo_ref,
                       inter_scr):
    h = pl.program_id(1)
    row = lax.broadcasted_iota(jnp.int32, (SR, SR), 0)
    col = lax.broadcasted_iota(jnp.int32, (SR, SR), 1)
    trow = row & 3

    @pl.when(h == 0)
    def _():
        dv = pd_ref[:, W_D:2 * W_D]
        mu = jnp.mean(dv, axis=-1, keepdims=True)
        xc = dv - mu
        var = jnp.mean(xc * xc, axis=-1, keepdims=True)
        rstd = lax.rsqrt(var + EPS)
        for g in range(G_D):
            sl = slice(g * 128, (g + 1) * 128)
            vn = xc[:, sl] * rstd * lng_ref[0:1, sl] + lnb_ref[0:1, sl]
            vn_ref[:, sl] = vn
            s = rtab_ref[0, :, sl] * vn + btab_ref[:, sl]
            for j in range(1, DEC_SEQ):
                s = s + jnp.where(trow >= j, rtab_ref[j, :, sl] * pltpu.roll(vn, j, 0), 0.0)
            d_u = pd_ref[:, g * 128:(g + 1) * 128]
            d_z = pd_ref[:, 2 * W_D + g * 128:2 * W_D + (g + 1) * 128]
            yd_ref[:, sl] = (d_u * s * _silu(d_z)).astype(BF16)

    same = (row >> 2) == (col >> 2)
    mask = jnp.where(same, trow - (col & 3), -1) >= 0
    pre = _dot(h_ref[...], wg_ref[...].astype(BF16)) + bg_ref[...]
    lf = _log_sigmoid(pre)
    b_full = _dot_hi(jnp.where(mask, 1.0, 0.0), lf)
    sel_i = col == h
    sel_f = col == h + H_C
    ig_c = jnp.sum(jnp.where(sel_i, pre, 0.0), axis=-1, keepdims=True)
    b_c = jnp.sum(jnp.where(sel_f, b_full, 0.0), axis=-1, keepdims=True)
    sel_ir = row == h
    sel_fr = row == h + H_C
    ig_r = jnp.sum(jnp.where(sel_ir, pre.T, 0.0), axis=0, keepdims=True)
    b_r = jnp.sum(jnp.where(sel_fr, b_full.T, 0.0), axis=0, keepdims=True)
    m_prev = mrow_ref[0]
    log_d = jnp.where(mask, b_c - b_r + ig_r, NEG_INF)
    log_inter = b_c + m_prev
    m_t = jnp.maximum(log_inter, jnp.max(log_d, axis=-1, keepdims=True))
    w = jnp.exp(log_d - m_t)
    w_inter = jnp.exp(log_inter - m_t)
    q = pq_ref[...] * (DQK_C ** -0.5)
    k = pk_ref[...]
    v = pv_ref[...]
    qb = q.astype(BF16)
    kb = k.astype(BF16)
    vb = v.astype(BF16)
    sc = _dot_nt(qb, kb) * w
    sub = lax.broadcasted_iota(jnp.int32, (8, DV_C), 0)
    for g in range(SR // 8):
        q8 = qb[8 * g:8 * g + 8, :]
        r0 = _dot_nt(q8, c_ref[2 * g, 0].astype(BF16))
        r1 = _dot_nt(q8, c_ref[2 * g + 1, 0].astype(BF16))
        inter_scr[8 * g:8 * g + 8, :] = jnp.where(sub < DEC_SEQ, r0, r1)
    n_rows = nrow_ref[0]
    num = _dot(sc.astype(BF16), vb) + w_inter * inter_scr[...]
    den = jnp.sum(sc, axis=-1, keepdims=True) + w_inter * jnp.sum(q * n_rows, axis=-1, keepdims=True)
    hh = num / jnp.maximum(jnp.abs(den), jnp.exp(-m_t))
    yc_ref[...] = (_head_norm(hh, gm_ref[...]) * _silu(pz_ref[...])).astype(BF16)

    stats = jnp.where(col == 0, m_t, jnp.where(col == 1, b_c, 0.0))
    last = _dot_hi(jnp.where(col == (row | 3), 1.0, 0.0), stats)
    m_new = last[:, 0:1]
    b_last = last[:, 1:2]
    w_end = jnp.exp(b_last - b_c + ig_c - m_new)
    cd = jnp.exp(b_last + m_prev - m_new)
    mo_ref[0] = m_new
    no_ref[0] = cd * n_rows + _dot_hi(jnp.where(same, 1.0, 0.0), w_end * k)
    vwt = (v * w_end).T
    lane_b = lax.broadcasted_iota(jnp.int32, (DV_C, SR), 1) >> 2
    for b in range(SB):
        lhs = jnp.where(lane_b == b, vwt, 0.0).astype(BF16)
        cd_b = cd[4 * b + 3:4 * b + 4, :]
        co_ref[b, 0] = cd_b * c_ref[b, 0] + _dot(lhs, kb)


def _odd_sample(p, h, w_o, c_state, n_rows, m_rows, bg, gm, lng, lnb, rtab, btab):
    nb = DEC_BATCH // SB
    const2 = lambda i, h: (0, 0)
    koff = H_C * DQK_C // DQK_C
    voff = 2 * H_C * DQK_C // DV_C
    zoff = (2 * H_C * DQK_C + W_C) // DV_C
    return pl.pallas_call(
        _odd_sample_kernel,
        grid=(nb, H_C),
        in_specs=[pl.BlockSpec((SR, DQK_C), lambda i, h: (i, h)),
                  pl.BlockSpec((SR, DQK_C), lambda i, h: (i, koff + h)),
                  pl.BlockSpec((SR, DV_C), lambda i, h: (i, voff + h)),
                  pl.BlockSpec((SR, DV_C), lambda i, h: (i, zoff + h)),
                  pl.BlockSpec((SR, D_MODEL), lambda i, h: (i, 0)),
                  pl.BlockSpec((D_MODEL, 128), lambda i, h: (0, O_GATE // 128)),
                  pl.BlockSpec((SR, 3 * W_D), lambda i, h: (i, 1)),
                  pl.BlockSpec((SB, 1, DV_C, DQK_C), lambda i, h: (i, h, 0, 0)),
                  pl.BlockSpec((1, SR, DQK_C), lambda i, h: (h, i, 0)),
                  pl.BlockSpec((1, SR, 1), lambda i, h: (h, i, 0)),
                  pl.BlockSpec((1, 128), const2),
                  pl.BlockSpec((1, DV_C), lambda i, h: (0, h)),
                  pl.BlockSpec((1, W_D), const2),
                  pl.BlockSpec((1, W_D), const2),
                  pl.BlockSpec((DEC_SEQ, SR, W_D), lambda i, h: (0, 0, 0)),
                  pl.BlockSpec((SR, W_D), const2)],
        out_specs=(pl.BlockSpec((SR, DV_C), lambda i, h: (i, h)),
                   pl.BlockSpec((SR, W_D), lambda i, h: (i, 0)),
                   pl.BlockSpec((SR, W_D), lambda i, h: (i, 0)),
                   pl.BlockSpec((SB, 1, DV_C, DQK_C), lambda i, h: (i, h, 0, 0)),
                   pl.BlockSpec((1, SR, DQK_C), lambda i, h: (h, i, 0)),
                   pl.BlockSpec((1, SR, 1), lambda i, h: (h, i, 0))),
        out_shape=(jax.ShapeDtypeStruct((DEC_BATCH * DEC_SEQ, W_C), BF16),
                   jax.ShapeDtypeStruct((DEC_BATCH * DEC_SEQ, W_D), BF16),
                   jax.ShapeDtypeStruct((DEC_BATCH * DEC_SEQ, W_D), F32),
                   jax.ShapeDtypeStruct((DEC_BATCH, H_C, DV_C, DQK_C), F32),
                   jax.ShapeDtypeStruct((H_C, DEC_BATCH * DEC_SEQ, DQK_C), F32),
                   jax.ShapeDtypeStruct((H_C, DEC_BATCH * DEC_SEQ, 1), F32)),
        scratch_shapes=[pltpu.VMEM((SR, DV_C), F32)],
        compiler_params=_params(("arbitrary", "arbitrary")),
        name="odd_sample",
    )(p, p, p, p, h, w_o, p, c_state, n_rows, m_rows, bg, gm, lng, lnb, rtab, btab)


def _rope_tables(pos):
    inv = ROPE_BASE ** (-jnp.arange(0, DH_B, 2, dtype=F32) / DH_B)
    ang = pos.astype(F32)[:, None] * inv[None, :]
    cos = jnp.cos(ang)
    sin = jnp.sin(ang)
    return jnp.concatenate([cos, cos], axis=-1), jnp.concatenate([-sin, sin], axis=-1)


def kernel(x_prompt, x_sample, state_conv, state_ret, state_mlstm_C, state_mlstm_n, state_mlstm_m,
           norm_even, w_in_even, conv_w, ret_norm, w_out_even,
           norm_odd, w_in_odd, b_gate_odd, mlstm_norm, ln_v_g, ln_v_b,
           w_spatial, b_spatial, w_out_odd, norm_final):
    w_in_e = w_in_even[0]
    w_out_e = w_out_even[0].astype(BF16)
    w_o = w_in_odd[0]
    w_out_o = w_out_odd[0].astype(BF16)
    g_even = norm_even[0][None, :]
    g_odd = norm_odd[0][None, :]
    g_fin = norm_final[None, :]
    cw = conv_w[0]
    g_ret = ret_norm[0][None, :]
    bg = jnp.concatenate([b_gate_odd[0], jnp.zeros((128 - 2 * H_C,), F32)])[None, :]
    gm = mlstm_norm[0][None, :]
    lng = ln_v_g[0][None, :]
    lnb = ln_v_b[0][None, :]
    ws = w_spatial[0]
    bst = b_spatial[0].T

    cos_p, sin_p = _rope_tables(jnp.arange(SEQ, dtype=jnp.int32))
    cos_s, sin_s = _rope_tables(PAST_LEN + jnp.arange(DEC_SEQ, dtype=jnp.int32))
    cos_s = jnp.tile(cos_s, (SB, 1))
    sin_s = jnp.tile(sin_s, (SB, 1))
    lg_tab = jnp.broadcast_to(jnp.asarray(LOG_GAMMA, F32)[:, None, None], (H_B, 1, 128))

    ws4 = ws[:, :DEC_SEQ, :DEC_SEQ]
    t_idx = jnp.arange(DEC_SEQ)
    rtab = []
    for j in range(DEC_SEQ):
        coef = ws4[:, t_idx, (t_idx - j) % DEC_SEQ]
        tab = jnp.repeat(coef.T[:, :, None], 128, axis=2).reshape(DEC_SEQ, W_D)
        rtab.append(jnp.tile(tab, (SB, 1)))
    rtab = jnp.stack(rtab)
    btab = jnp.tile(jnp.repeat(b_spatial[0][:, :DEC_SEQ].T[:, :, None], 128, axis=2)
                    .reshape(DEC_SEQ, W_D), (SB, 1))

    rs = DEC_BATCH * DEC_SEQ
    xp = x_prompt.reshape(BATCH * SEQ, D_MODEL)
    xs = x_sample.reshape(rs, D_MODEL)
    hp = _norm_cast(xp, g_even, 512)
    hs = _norm_cast(xs, g_even, 512)
    pp, ps = _in_proj(hp, hs, w_in_e, E_IN)
    ya, yb, conv_p, ret_p = _even_prompt(pp, cw, g_ret, cos_p, sin_p)
    st_exp = jnp.pad(state_conv[0], ((0, 0), (0, DEC_SEQ - (CONV_W - 1)), (0, 0))).reshape(rs, W_A)
    ya_s, u_s, yb_s, ret_s = _even_sample(ps, st_exp, state_ret[0], cw, g_ret, cos_s, sin_s, lg_tab)
    x1, h1 = _outproj(ya, yb, w_out_e, xp, g_odd, 512, final=False)
    x1s, h1s = _outproj(ya_s, yb_s, w_out_e, xs, g_odd, 512, final=False)

    p2, p2s = _in_proj(h1, h1s, w_o, O_N, shift_from=O_GATE // IN_TN, shift=N_GATE)
    yc, yd, c_p, n_p, m_p = _odd_prompt(p2, h1, w_o, bg, gm, lng, lnb, ws, bst)
    n_rows = jnp.repeat(jnp.transpose(state_mlstm_n[0], (1, 0, 2)), DEC_SEQ, axis=1)
    m_rows = jnp.repeat(state_mlstm_m[0].T, DEC_SEQ, axis=1)[:, :, None]
    yc_s, yd_s, vn_s, c_s, no_s, mo_s = _odd_sample(
        p2s, h1s, w_o, state_mlstm_C[0], n_rows, m_rows, bg, gm, lng, lnb, rtab, btab)
    y_prompt = _outproj(yc, yd, w_out_o, x1, g_fin, 512, final=True)
    y_sample = _outproj(yc_s, yd_s, w_out_o, x1s, g_fin, 512, final=True)

    conv_s = u_s.reshape(DEC_BATCH, DEC_SEQ, W_A)[:, DEC_SEQ - (CONV_W - 1):, :]
    n_s = jnp.transpose(no_s[:, DEC_SEQ - 1::DEC_SEQ, :], (1, 0, 2))
    m_s = mo_s[:, DEC_SEQ - 1::DEC_SEQ, 0].T
    return (y_prompt.reshape(BATCH, SEQ, D_MODEL),
            y_sample.reshape(DEC_BATCH, DEC_SEQ, D_MODEL),
            conv_p[None], conv_s[None],
            ret_p[None], ret_s[None],
            c_p[None], c_s[None],
            n_p[None], n_s[None],
            m_p[:, :H_C, 0][None], m_s[None],
            vn_s.reshape(DEC_BATCH, DEC_SEQ, W_D)[None])
```

```python
import functools
import math

import jax
import jax.numpy as jnp
from jax import lax
from jax.experimental import pallas as pl
from jax.experimental.pallas import tpu as pltpu

F32 = jnp.float32
BF16 = jnp.bfloat16

D_MODEL = 2048
BATCH = 4
SEQ = 2048
DEC_BATCH = 128
DEC_SEQ = 4
PAST_LEN = 16384
W_A = 1024
CONV_W = 3
W_B = 1024
H_B = 8
DH_B = 128
E_IN = 8192
W_C = 1024
H_C = 4
DV_C = 256
DQK_C = 128
W_D = 1024
G_D = 8
CHUNK = 128
O_GATE = 2 * H_C * DQK_C + 2 * W_C
N_GATE = 2 * H_C
O_N = O_GATE + 3 * W_D
ROPE_BASE = 10000.0
EPS = 1e-6
LOG_GAMMA = tuple(math.log(1.0 - 2.0 ** (-5.0 - h)) for h in range(H_B))
NEG_INF = float("-inf")
VMEM_LIMIT = 56 * 1024 * 1024

NT_DIMS = (((1,), (1,)), ((), ()))
TN_DIMS = (((0,), (0,)), ((), ()))


def _silu(z):
    return z * (1.0 / (1.0 + jnp.exp(-z)))


def _log_sigmoid(x):
    return jnp.minimum(x, 0.0) - jnp.log1p(jnp.exp(-jnp.abs(x)))


def _dot(a, b):
    return jnp.dot(a, b, preferred_element_type=F32)


def _dot_nt(a, b):
    return lax.dot_general(a, b, NT_DIMS, preferred_element_type=F32)


def _dot_tn(a, b):
    return lax.dot_general(a, b, TN_DIMS, preferred_element_type=F32)


def _dot_hi(a, b):
    return jnp.dot(a, b, preferred_element_type=F32, precision=lax.Precision.HIGHEST)


def _head_norm(o, g):
    mu = jnp.mean(o, axis=-1, keepdims=True)
    oc = o - mu
    var = jnp.mean(oc * oc, axis=-1, keepdims=True)
    return oc * lax.rsqrt(var + EPS) * g


def _params(sem):
    return pltpu.CompilerParams(dimension_semantics=sem, vmem_limit_bytes=VMEM_LIMIT)


def _norm_cast_kernel(x_ref, g_ref, h_ref):
    x = x_ref[...]
    ms = jnp.mean(x * x, axis=-1, keepdims=True)
    h_ref[...] = (x * lax.rsqrt(ms + EPS) * g_ref[...]).astype(BF16)


def _norm_cast(x, g, tm):
    m, d = x.shape
    return pl.pallas_call(
        _norm_cast_kernel,
        grid=(m // tm,),
        in_specs=[pl.BlockSpec((tm, d), lambda i: (i, 0)),
                  pl.BlockSpec((1, d), lambda i: (0, 0))],
        out_specs=pl.BlockSpec((tm, d), lambda i: (i, 0)),
        out_shape=jax.ShapeDtypeStruct((m, d), BF16),
        compiler_params=_params(("arbitrary",)),
        name="norm_cast",
    )(x, g)


IN_TM = 1024
IN_TN = 1024


def _in_proj_kernel(*refs, shift_from, shift):
    if shift:
        hp_ref, hs_ref, w_ref, wn_ref, op_ref, os_ref, wb = refs
    else:
        hp_ref, hs_ref, w_ref, op_ref, os_ref, wb = refs
    j = pl.program_id(0)
    i = pl.program_id(1)

    if shift:
        @pl.when(jnp.logical_and(i == 0, j < shift_from))
        def _():
            wb[...] = w_ref[...].astype(BF16)

        @pl.when(jnp.logical_and(i == 0, j >= shift_from))
        def _():
            wb[...] = jnp.concatenate([w_ref[shift:IN_TN, :], wn_ref[...]], axis=0).astype(BF16)

        mm = _dot_nt
    else:
        @pl.when(i == 0)
        def _():
            wb[...] = w_ref[...].astype(BF16)

        mm = _dot

    @pl.when(i == 0)
    def _():
        os_ref[...] = mm(hs_ref[...], wb[...])

    @pl.when(i > 0)
    def _():
        op_ref[...] = mm(hp_ref[...], wb[...])


def _in_proj(hp, hs, w, n_out, shift_from=0, shift=0):
    mp, k = hp.shape
    ms = hs.shape[0]
    n_prompt = mp // IN_TM
    prow = lambda j, i: (jnp.maximum(i - 1, 0), 0)
    in_specs = [pl.BlockSpec((IN_TM, k), prow),
                pl.BlockSpec((ms, k), lambda j, i: (0, 0))]
    args = [hp, hs, w]
    if shift:
        in_specs.append(pl.BlockSpec((IN_TN, k), lambda j, i: (j, 0)))
        in_specs.append(pl.BlockSpec((shift, k), lambda j, i: ((j + 1) * (IN_TN // shift), 0)))
        args.append(w)
        wb_shape = (IN_TN, k)
    else:
        in_specs.append(pl.BlockSpec((k, IN_TN), lambda j, i: (0, j)))
        wb_shape = (k, IN_TN)
    return pl.pallas_call(
        functools.partial(_in_proj_kernel, shift_from=shift_from, shift=shift),
        grid=(n_out // IN_TN, n_prompt + 1),
        in_specs=in_specs,
        out_specs=(pl.BlockSpec((IN_TM, IN_TN), lambda j, i: (jnp.maximum(i - 1, 0), j)),
                   pl.BlockSpec((ms, IN_TN), lambda j, i: (0, j))),
        out_shape=(jax.ShapeDtypeStruct((mp, n_out), F32),
                   jax.ShapeDtypeStruct((ms, n_out), F32)),
        scratch_shapes=[pltpu.VMEM(wb_shape, BF16)],
        compiler_params=_params(("arbitrary", "arbitrary")),
        name="in_proj",
    )(*args)


def _outproj_kernel(ya_ref, yb_ref, w_ref, x_ref, g_ref, *out_refs, final):
    half = ya_ref.shape[1]
    acc = _dot(ya_ref[...], w_ref[0:half, :]) + _dot(yb_ref[...], w_ref[half:2 * half, :])
    x1 = x_ref[...] + acc
    ms = jnp.mean(x1 * x1, axis=-1, keepdims=True)
    hn = x1 * lax.rsqrt(ms + EPS) * g_ref[...]
    if final:
        out_refs[0][...] = hn
    else:
        out_refs[0][...] = x1
        out_refs[1][...] = hn.astype(BF16)


def _outproj(ya, yb, w, x, g, tm, final):
    m, half = ya.shape
    d = w.shape[1]
    row = lambda i: (i, 0)
    if final:
        out_shape = jax.ShapeDtypeStruct((m, d), F32)
        out_specs = pl.BlockSpec((tm, d), row)
    else:
        out_shape = (jax.ShapeDtypeStruct((m, d), F32), jax.ShapeDtypeStruct((m, d), BF16))
        out_specs = (pl.BlockSpec((tm, d), row), pl.BlockSpec((tm, d), row))
    return pl.pallas_call(
        functools.partial(_outproj_kernel, final=final),
        grid=(m // tm,),
        in_specs=[pl.BlockSpec((tm, half), row),
                  pl.BlockSpec((tm, half), row),
                  pl.BlockSpec((2 * half, d), lambda i: (0, 0)),
                  pl.BlockSpec((tm, d), row),
                  pl.BlockSpec((1, d), lambda i: (0, 0))],
        out_specs=out_specs,
        out_shape=out_shape,
        compiler_params=_params(("arbitrary",)),
        name="out_proj_final" if final else "out_proj",
    )(ya, yb, w, x, g)


def _rope(x, cosf, sins):
    return x * cosf + pltpu.roll(x, DH_B // 2, 1) * sins


def _even_prompt_kernel(p_ref, cw_ref, gret_ref, cos_ref, sin_ref,
                        ya_ref, yb_ref, conv_ref, s_ref, ubuf):
    c = pl.program_id(1)
    L = CHUNK

    @pl.when(c == 0)
    def _():
        ubuf[0:8, :] = jnp.zeros((8, W_A), F32)
        s_ref[...] = jnp.zeros_like(s_ref)

    for j in range(W_A // 128):
        sl = slice(j * 128, (j + 1) * 128)
        a_b = p_ref[:, j * 128:(j + 1) * 128]
        a_c = p_ref[:, W_A + j * 128:W_A + (j + 1) * 128]
        a_x = p_ref[:, 2 * W_A + j * 128:2 * W_A + (j + 1) * 128]
        a_z = p_ref[:, 3 * W_A + j * 128:3 * W_A + (j + 1) * 128]
        u = a_c * a_x
        ubuf[8:8 + L, sl] = u
        t0 = ubuf[6:6 + L, sl]
        t1 = ubuf[7:7 + L, sl]
        conv = cw_ref[0:1, sl] * t0 + cw_ref[1:2, sl] * t1 + cw_ref[2:3, sl] * u
        ya_ref[:, sl] = (a_b * conv * _silu(a_z)).astype(BF16)
        ubuf[0:8, sl] = u[L - 8:L, :]
    conv_ref[0] = ubuf[6:8, :]

    cosf = cos_ref[...]
    sins = sin_ref[...]
    row = lax.broadcasted_iota(jnp.int32, (L, L), 0)
    col = lax.broadcasted_iota(jnp.int32, (L, L), 1)
    causal = row >= col
    diff = jnp.maximum(row - col, 0).astype(F32)
    ti = lax.broadcasted_iota(jnp.int32, (L, 1), 0).astype(F32)
    base = 4 * W_A
    for h in range(H_B):
        lg = LOG_GAMMA[h]
        sl = slice(h * DH_B, (h + 1) * DH_B)
        q = p_ref[:, base + h * DH_B:base + (h + 1) * DH_B]
        k = p_ref[:, base + W_B + h * DH_B:base + W_B + (h + 1) * DH_B]
        v = p_ref[:, base + 2 * W_B + h * DH_B:base + 2 * W_B + (h + 1) * DH_B]
        z = p_ref[:, base + 3 * W_B + h * DH_B:base + 3 * W_B + (h + 1) * DH_B]
        qr = _rope(q, cosf, sins)
        kr = _rope(k, cosf, sins) * (DH_B ** -0.5)
        decay = jnp.where(causal, jnp.exp(lg * diff), 0.0)
        qb = qr.astype(BF16)
        kb = kr.astype(BF16)
        vb = v.astype(BF16)
        sc = _dot_nt(qb, kb) * decay
        inner = _dot(sc.astype(BF16), vb)
        s_old = s_ref[0, h]
        cross = _dot(qb, s_old.astype(BF16)) * jnp.exp(lg * (ti + 1.0))
        kd = (kr * jnp.exp(lg * (L - 1.0 - ti))).astype(BF16)
        s_ref[0, h] = math.exp(lg * L) * s_old + _dot_tn(kd, vb)
        o = inner + cross
        yb_ref[:, sl] = (_head_norm(o, gret_ref[0:1, sl]) * _silu(z)).astype(BF16)


def _even_prompt(p, conv_w, g_ret, cosf, sins):
    nc = SEQ // CHUNK
    rows = lambda b, c: (b * nc + c, 0)
    const2 = lambda b, c: (0, 0)
    return pl.pallas_call(
        _even_prompt_kernel,
        grid=(BATCH, nc),
        in_specs=[pl.BlockSpec((CHUNK, E_IN), rows),
                  pl.BlockSpec((CONV_W, W_A), const2),
                  pl.BlockSpec((1, W_B), const2),
                  pl.BlockSpec((CHUNK, DH_B), lambda b, c: (c, 0)),
                  pl.BlockSpec((CHUNK, DH_B), lambda b, c: (c, 0))],
        out_specs=(pl.BlockSpec((CHUNK, W_A), rows),
                   pl.BlockSpec((CHUNK, W_B), rows),
                   pl.BlockSpec((1, CONV_W - 1, W_A), lambda b, c: (b, 0, 0)),
                   pl.BlockSpec((1, H_B, DH_B, DH_B), lambda b, c: (b, 0, 0, 0))),
        out_shape=(jax.ShapeDtypeStruct((BATCH * SEQ, W_A), BF16),
                   jax.ShapeDtypeStruct((BATCH * SEQ, W_B), BF16),
                   jax.ShapeDtypeStruct((BATCH, CONV_W - 1, W_A), F32),
                   jax.ShapeDtypeStruct((BATCH, H_B, DH_B, DH_B), F32)),
        scratch_shapes=[pltpu.VMEM((CHUNK + 8, W_A), F32)],
        compiler_params=_params(("arbitrary", "arbitrary")),
        name="even_prompt",
    )(p, conv_w, g_ret, cosf, sins)


SB = 32
SR = SB * DEC_SEQ


def _even_sample_kernel(pa_ref, pq_ref, pk_ref, pv_ref, pz_ref, st_ref, s_ref,
                        cw_ref, gret_ref, cos_ref, sin_ref, lg_ref,
                        ya_ref, u_ref, yb_ref, so_ref, cross_scr):
    h = pl.program_id(1)
    row = lax.broadcasted_iota(jnp.int32, (SR, SR), 0)
    col = lax.broadcasted_iota(jnp.int32, (SR, SR), 1)
    trow = row & 3

    @pl.when(h == 0)
    def _():
        for j in range(W_A // 128):
            sl = slice(j * 128, (j + 1) * 128)
            a_b = pa_ref[:, j * 128:(j + 1) * 128]
            a_c = pa_ref[:, W_A + j * 128:W_A + (j + 1) * 128]
            a_x = pa_ref[:, 2 * W_A + j * 128:2 * W_A + (j + 1) * 128]
            a_z = pa_ref[:, 3 * W_A + j * 128:3 * W_A + (j + 1) * 128]
            u = a_c * a_x
            e = st_ref[:, sl]
            tap1 = jnp.where(trow >= 1, pltpu.roll(u, 1, 0), pltpu.roll(e, SR - 1, 0))
            tap0 = jnp.where(trow >= 2, pltpu.roll(u, 2, 0), e)
            conv = cw_ref[0:1, sl] * tap0 + cw_ref[1:2, sl] * tap1 + cw_ref[2:3, sl] * u
            ya_ref[:, sl] = (a_b * conv * _silu(a_z)).astype(BF16)
            u_ref[:, sl] = u

    lg = lg_ref[0][:, 0:1]
    same = (row >> 2) == (col >> 2)
    dd = trow - (col & 3)
    mask = jnp.where(same, dd, -1) >= 0
    decay = jnp.where(mask, jnp.exp(lg * jnp.maximum(dd, 0).astype(F32)), 0.0)
    tcol = (lax.broadcasted_iota(jnp.int32, (SR, 1), 0) & 3).astype(F32)
    cosf = cos_ref[...]
    sins = sin_ref[...]
    qr = _rope(pq_ref[...], cosf, sins)
    kr = _rope(pk_ref[...], cosf, sins) * (DH_B ** -0.5)
    qb = qr.astype(BF16)
    kb = kr.astype(BF16)
    vb = pv_ref[...].astype(BF16)
    sc = _dot_nt(qb, kb) * decay
    inner = _dot(sc.astype(BF16), vb)
    kdt = (kr * jnp.exp(lg * (DEC_SEQ - 1.0 - tcol))).T
    gamma_l = jnp.exp(lg * float(DEC_SEQ))
    lane_b = col >> 2
    sub = lax.broadcasted_iota(jnp.int32, (8, DH_B), 0)
    for g in range(SR // 8):
        q8 = qb[8 * g:8 * g + 8, :]
        res = []
        for beta in range(2):
            b = 2 * g + beta
            s_old = s_ref[b, 0]
            res.append(_dot(q8, s_old.astype(BF16)))
            lhs = jnp.where(lane_b == b, kdt, 0.0).astype(BF16)
            so_ref[b, 0] = gamma_l * s_old + _dot(lhs, vb)
        cross_scr[8 * g:8 * g + 8, :] = jnp.where(sub < DEC_SEQ, res[0], res[1])
    o = inner + cross_scr[...] * jnp.exp(lg * (tcol + 1.0))
    yb_ref[...] = (_head_norm(o, gret_ref[...]) * _silu(pz_ref[...])).astype(BF16)


def _even_sample(p, st_exp, s_state, conv_w, g_ret, cosf, sins, lg_tab):
    nb = DEC_BATCH // SB
    qoff = 4 * W_A // DH_B
    hb = W_B // DH_B
    const2 = lambda i, h: (0, 0)
    return pl.pallas_call(
        _even_sample_kernel,
        grid=(nb, H_B),
        in_specs=[pl.BlockSpec((SR, 4 * W_A), lambda i, h: (i, 0)),
                  pl.BlockSpec((SR, DH_B), lambda i, h: (i, qoff + h)),
                  pl.BlockSpec((SR, DH_B), lambda i, h: (i, qoff + hb + h)),
                  pl.BlockSpec((SR, DH_B), lambda i, h: (i, qoff + 2 * hb + h)),
                  pl.BlockSpec((SR, DH_B), lambda i, h: (i, qoff + 3 * hb + h)),
                  pl.BlockSpec((SR, W_A), lambda i, h: (i, 0)),
                  pl.BlockSpec((SB, 1, DH_B, DH_B), lambda i, h: (i, h, 0, 0)),
                  pl.BlockSpec((CONV_W, W_A), const2),
                  pl.BlockSpec((1, DH_B), lambda i, h: (0, h)),
                  pl.BlockSpec((SR, DH_B), const2),
                  pl.BlockSpec((SR, DH_B), const2),
                  pl.BlockSpec((1, 1, 128), lambda i, h: (h, 0, 0))],
        out_specs=(pl.BlockSpec((SR, W_A), lambda i, h: (i, 0)),
                   pl.BlockSpec((SR, W_A), lambda i, h: (i, 0)),
                   pl.BlockSpec((SR, DH_B), lambda i, h: (i, h)),
                   pl.BlockSpec((SB, 1, DH_B, DH_B), lambda i, h: (i, h, 0, 0))),
        out_shape=(jax.ShapeDtypeStruct((DEC_BATCH * DEC_SEQ, W_A), BF16),
                   jax.ShapeDtypeStruct((DEC_BATCH * DEC_SEQ, W_A), F32),
                   jax.ShapeDtypeStruct((DEC_BATCH * DEC_SEQ, W_B), BF16),
                   jax.ShapeDtypeStruct((DEC_BATCH, H_B, DH_B, DH_B), F32)),
        scratch_shapes=[pltpu.VMEM((SR, DH_B), F32)],
        compiler_params=_params(("arbitrary", "arbitrary")),
        name="even_sample",
    )(p, p, p, p, p, st_exp, s_state, conv_w, g_ret, cosf, sins, lg_tab)


def _odd_prompt_kernel(p_ref, h_ref, wg_ref, bg_ref, gm_ref, lng_ref, lnb_ref, ws_ref, bst_ref,
                       yc_ref, yd_ref, c_ref, n_ref, m_ref, wgb):
    c = pl.program_id(1)
    L = CHUNK

    @pl.when(jnp.logical_and(pl.program_id(0) == 0, c == 0))
    def _():
        wgb[...] = wg_ref[...].astype(BF16)

    @pl.when(c == 0)
    def _():
        c_ref[...] = jnp.zeros_like(c_ref)
        n_ref[...] = jnp.zeros_like(n_ref)
        m_ref[...] = jnp.zeros_like(m_ref)

    row = lax.broadcasted_iota(jnp.int32, (L, L), 0)
    col = lax.broadcasted_iota(jnp.int32, (L, L), 1)
    tri = row >= col

    pre = _dot_nt(h_ref[...], wgb[...]) + bg_ref[...]
    lf = _log_sigmoid(pre)
    b_c = _dot_hi(jnp.where(tri, 1.0, 0.0), lf)
    b_r = b_c.T
    pre_r = pre.T
    for h in range(H_C):
        bc = b_c[:, H_C + h:H_C + h + 1]
        br = b_r[H_C + h:H_C + h + 1, :]
        igr = pre_r[h:h + 1, :]
        igc = pre[:, h:h + 1]
        m_prev = m_ref[0, h:h + 1, 0:1]
        log_d = jnp.where(tri, bc - br + igr, NEG_INF)
        log_inter = bc + m_prev
        m_t = jnp.maximum(log_inter, jnp.max(log_d, axis=-1, keepdims=True))
        w = jnp.exp(log_d - m_t)
        w_inter = jnp.exp(log_inter - m_t)
        q = p_ref[:, h * DQK_C:(h + 1) * DQK_C] * (DQK_C ** -0.5)
        k = p_ref[:, H_C * DQK_C + h * DQK_C:H_C * DQK_C + (h + 1) * DQK_C]
        v = p_ref[:, 2 * H_C * DQK_C + h * DV_C:2 * H_C * DQK_C + (h + 1) * DV_C]
        z = p_ref[:, 2 * H_C * DQK_C + W_C + h * DV_C:2 * H_C * DQK_C + W_C + (h + 1) * DV_C]
        qb = q.astype(BF16)
        kb = k.astype(BF16)
        vb = v.astype(BF16)
        sc = _dot_nt(qb, kb) * w
        c_old = c_ref[0, h]
        n_old = n_ref[0, h:h + 1, :]
        num = _dot(sc.astype(BF16), vb) + w_inter * _dot_nt(qb, c_old.astype(BF16))
        den = jnp.sum(sc, axis=-1, keepdims=True) + w_inter * jnp.sum(q * n_old, axis=-1, keepdims=True)
        hh = num / jnp.maximum(jnp.abs(den), jnp.exp(-m_t))
        m_new = m_t[L - 1:L, :]
        b_last = bc[L - 1:L, :]
        w_end = jnp.exp(b_last - bc + igc - m_new)
        cd = jnp.exp(b_last + m_prev - m_new)
        c_ref[0, h] = cd * c_old + _dot_tn((v * w_end).astype(BF16), kb)
        n_ref[0, h:h + 1, :] = cd * n_old + jnp.sum(w_end * k, axis=0, keepdims=True)
        m_ref[0, h:h + 1, :] = jnp.broadcast_to(m_new, (1, 128))
        sl = slice(h * DV_C, (h + 1) * DV_C)
        yc_ref[:, sl] = (_head_norm(hh, gm_ref[0:1, sl]) * _silu(z)).astype(BF16)

    dbase = 2 * H_C * DQK_C + 2 * W_C
    dv = p_ref[:, dbase + W_D:dbase + 2 * W_D]
    mu = jnp.mean(dv, axis=-1, keepdims=True)
    xc = dv - mu
    var = jnp.mean(xc * xc, axis=-1, keepdims=True)
    rstd = lax.rsqrt(var + EPS)
    for g in range(G_D):
        sl = slice(g * 128, (g + 1) * 128)
        vn = xc[:, sl] * rstd * lng_ref[0:1, sl] + lnb_ref[0:1, sl]
        ws = jnp.where(tri, ws_ref[g], 0.0).astype(BF16)
        s = _dot(ws, vn.astype(BF16)) + bst_ref[:, g:g + 1]
        d_u = p_ref[:, dbase + g * 128:dbase + (g + 1) * 128]
        d_z = p_ref[:, dbase + 2 * W_D + g * 128:dbase + 2 * W_D + (g + 1) * 128]
        yd_ref[:, sl] = (d_u * s * _silu(d_z)).astype(BF16)


def _odd_prompt(p, h, w_o, bg, gm, lng, lnb, ws, bst):
    nc = SEQ // CHUNK
    rows = lambda b, c: (b * nc + c, 0)
    const2 = lambda b, c: (0, 0)
    return pl.pallas_call(
        _odd_prompt_kernel,
        grid=(BATCH, nc),
        in_specs=[pl.BlockSpec((CHUNK, O_N), rows),
                  pl.BlockSpec((CHUNK, D_MODEL), rows),
                  pl.BlockSpec((128, D_MODEL), lambda b, c: (O_GATE // 128, 0)),
                  pl.BlockSpec((1, 128), const2),
                  pl.BlockSpec((1, W_C), const2),
                  pl.BlockSpec((1, W_D), const2),
                  pl.BlockSpec((1, W_D), const2),
                  pl.BlockSpec((G_D, CHUNK, CHUNK), lambda b, c: (0, 0, 0)),
                  pl.BlockSpec((CHUNK, G_D), const2)],
        out_specs=(pl.BlockSpec((CHUNK, W_C), rows),
                   pl.BlockSpec((CHUNK, W_D), rows),
                   pl.BlockSpec((1, H_C, DV_C, DQK_C), lambda b, c: (b, 0, 0, 0)),
                   pl.BlockSpec((1, H_C, DQK_C), lambda b, c: (b, 0, 0)),
                   pl.BlockSpec((1, 8, 128), lambda b, c: (b, 0, 0))),
        out_shape=(jax.ShapeDtypeStruct((BATCH * SEQ, W_C), BF16),
                   jax.ShapeDtypeStruct((BATCH * SEQ, W_D), BF16),
                   jax.ShapeDtypeStruct((BATCH, H_C, DV_C, DQK_C), F32),
                   jax.ShapeDtypeStruct((BATCH, H_C, DQK_C), F32),
                   jax.ShapeDtypeStruct((BATCH, 8, 128), F32)),
        scratch_shapes=[pltpu.VMEM((128, D_MODEL), BF16)],
        compiler_params=_params(("arbitrary", "arbitrary")),
        name="odd_prompt",
    )(p, h, w_o, bg, gm, lng, lnb, ws, bst)


def _odd_sample_kernel(pq_ref, pk_ref, pv_ref, pz_ref, h_ref, wg_ref, pd_ref,
                       c_ref, nrow_ref, mrow_ref, bg_ref, gm_ref, lng_ref, lnb_ref,
                       rtab_ref, btab_ref,
                       yc_ref, yd_ref, vn_ref, co_ref, no_ref, mo_ref,
                       inter_scr):
    h = pl.program_id(1)
    row = lax.broadcasted_iota(jnp.int32, (SR, SR), 0)
    col = lax.broadcasted_iota(jnp.int32, (SR, SR), 1)
    trow = row & 3

    @pl.when(h == 0)
    def _():
        dv = pd_ref[:, W_D:2 * W_D]
        mu = jnp.mean(dv, axis=-1, keepdims=True)
        xc = dv - mu
        var = jnp.mean(xc * xc, axis=-1, keepdims=True)
        rstd = lax.rsqrt(var + EPS)
        for g in range(G_D):
            sl = slice(g * 128, (g + 1) * 128)
            vn = xc[:, sl] * rstd * lng_ref[0:1, sl] + lnb_ref[0:1, sl]
            vn_ref[:, sl] = vn
            s = rtab_ref[0, :, sl] * vn + btab_ref[:, sl]
            for j in range(1, DEC_SEQ):
                s = s + jnp.where(trow >= j, rtab_ref[j, :, sl] * pltpu.roll(vn, j, 0), 0.0)
            d_u = pd_ref[:, g * 128:(g + 1) * 128]
            d_z = pd_ref[:, 2 * W_D + g * 128:2 * W_D + (g + 1) * 128]
            yd_ref[:, sl] = (d_u * s * _silu(d_z)).astype(BF16)

    same = (row >> 2) == (col >> 2)
    mask = jnp.where(same, trow - (col & 3), -1) >= 0
    pre = _dot_nt(h_ref[...], wg_ref[...].astype(BF16)) + bg_ref[...]
    lf = _log_sigmoid(pre)
    b_full = _dot_hi(jnp.where(mask, 1.0, 0.0), lf)
    sel_i = col == h
    sel_f = col == h + H_C
    ig_c = jnp.sum(jnp.where(sel_i, pre, 0.0), axis=-1, keepdims=True)
    b_c = jnp.sum(jnp.where(sel_f, b_full, 0.0), axis=-1, keepdims=True)
    sel_ir = row == h
    sel_fr = row == h + H_C
    ig_r = jnp.sum(jnp.where(sel_ir, pre.T, 0.0), axis=0, keepdims=True)
    b_r = jnp.sum(jnp.where(sel_fr, b_full.T, 0.0), axis=0, keepdims=True)
    m_prev = mrow_ref[0]
    log_d = jnp.where(mask, b_c - b_r + ig_r, NEG_INF)
    log_inter = b_c + m_prev
    m_t = jnp.maximum(log_inter, jnp.max(log_d, axis=-1, keepdims=True))
    w = jnp.exp(log_d - m_t)
    w_inter = jnp.exp(log_inter - m_t)
    q = pq_ref[...] * (DQK_C ** -0.5)
    k = pk_ref[...]
    v = pv_ref[...]
    qb = q.astype(BF16)
    kb = k.astype(BF16)
    vb = v.astype(BF16)
    sc = _dot_nt(qb, kb) * w
    sub = lax.broadcasted_iota(jnp.int32, (8, DV_C), 0)
    for g in range(SR // 8):
        q8 = qb[8 * g:8 * g + 8, :]
        r0 = _dot_nt(q8, c_ref[2 * g, 0].astype(BF16))
        r1 = _dot_nt(q8, c_ref[2 * g + 1, 0].astype(BF16))
        inter_scr[8 * g:8 * g + 8, :] = jnp.where(sub < DEC_SEQ, r0, r1)
    n_rows = nrow_ref[0]
    num = _dot(sc.astype(BF16), vb) + w_inter * inter_scr[...]
    den = jnp.sum(sc, axis=-1, keepdims=True) + w_inter * jnp.sum(q * n_rows, axis=-1, keepdims=True)
    hh = num / jnp.maximum(jnp.abs(den), jnp.exp(-m_t))
    yc_ref[...] = (_head_norm(hh, gm_ref[...]) * _silu(pz_ref[...])).astype(BF16)

    stats = jnp.where(col == 0, m_t, jnp.where(col == 1, b_c, 0.0))
    last = _dot_hi(jnp.where(col == (row | 3), 1.0, 0.0), stats)
    m_new = last[:, 0:1]
    b_last = last[:, 1:2]
    w_end = jnp.exp(b_last - b_c + ig_c - m_new)
    cd = jnp.exp(b_last + m_prev - m_new)
    mo_ref[0] = m_new
    no_ref[0] = cd * n_rows + _dot_hi(jnp.where(same, 1.0, 0.0), w_end * k)
    vwt = (v * w_end).T
    lane_b = lax.broadcasted_iota(jnp.int32, (DV_C, SR), 1) >> 2
    for b in range(SB):
        lhs = jnp.where(lane_b == b, vwt, 0.0).astype(BF16)
        cd_b = cd[4 * b + 3:4 * b + 4, :]
        co_ref[b, 0] = cd_b * c_ref[b, 0] + _dot(lhs, kb)


def _odd_sample(p, h, w_o, c_state, n_rows, m_rows, bg, gm, lng, lnb, rtab, btab):
    nb = DEC_BATCH // SB
    const2 = lambda i, h: (0, 0)
    koff = H_C * DQK_C // DQK_C
    voff = 2 * H_C * DQK_C // DV_C
    zoff = (2 * H_C * DQK_C + W_C) // DV_C
    return pl.pallas_call(
        _odd_sample_kernel,
        grid=(nb, H_C),
        in_specs=[pl.BlockSpec((SR, DQK_C), lambda i, h: (i, h)),
                  pl.BlockSpec((SR, DQK_C), lambda i, h: (i, koff + h)),
                  pl.BlockSpec((SR, DV_C), lambda i, h: (i, voff + h)),
                  pl.BlockSpec((SR, DV_C), lambda i, h: (i, zoff + h)),
                  pl.BlockSpec((SR, D_MODEL), lambda i, h: (i, 0)),
                  pl.BlockSpec((128, D_MODEL), lambda i, h: (O_GATE // 128, 0)),
                  pl.BlockSpec((SR, 3 * W_D), lambda i, h: (i, 1)),
                  pl.BlockSpec((SB, 1, DV_C, DQK_C), lambda i, h: (i, h, 0, 0)),
                  pl.BlockSpec((1, SR, DQK_C), lambda i, h: (h, i, 0)),
                  pl.BlockSpec((1, SR, 1), lambda i, h: (h, i, 0)),
                  pl.BlockSpec((1, 128), const2),
                  pl.BlockSpec((1, DV_C), lambda i, h: (0, h)),
                  pl.BlockSpec((1, W_D), const2),
                  pl.BlockSpec((1, W_D), const2),
                  pl.BlockSpec((DEC_SEQ, SR, W_D), lambda i, h: (0, 0, 0)),
                  pl.BlockSpec((SR, W_D), const2)],
        out_specs=(pl.BlockSpec((SR, DV_C), lambda i, h: (i, h)),
                   pl.BlockSpec((SR, W_D), lambda i, h: (i, 0)),
                   pl.BlockSpec((SR, W_D), lambda i, h: (i, 0)),
                   pl.BlockSpec((SB, 1, DV_C, DQK_C), lambda i, h: (i, h, 0, 0)),
                   pl.BlockSpec((1, SR, DQK_C), lambda i, h: (h, i, 0)),
                   pl.BlockSpec((1, SR, 1), lambda i, h: (h, i, 0))),
        out_shape=(jax.ShapeDtypeStruct((DEC_BATCH * DEC_SEQ, W_C), BF16),
                   jax.ShapeDtypeStruct((DEC_BATCH * DEC_SEQ, W_D), BF16),
                   jax.ShapeDtypeStruct((DEC_BATCH * DEC_SEQ, W_D), F32),
                   jax.ShapeDtypeStruct((DEC_BATCH, H_C, DV_C, DQK_C), F32),
                   jax.ShapeDtypeStruct((H_C, DEC_BATCH * DEC_SEQ, DQK_C), F32),
                   jax.ShapeDtypeStruct((H_C, DEC_BATCH * DEC_SEQ, 1), F32)),
        scratch_shapes=[pltpu.VMEM((SR, DV_C), F32)],
        compiler_params=_params(("arbitrary", "arbitrary")),
        name="odd_sample",
    )(p, p, p, p, h, w_o, p, c_state, n_rows, m_rows, bg, gm, lng, lnb, rtab, btab)


def _rope_tables(pos):
    inv = ROPE_BASE ** (-jnp.arange(0, DH_B, 2, dtype=F32) / DH_B)
    ang = pos.astype(F32)[:, None] * inv[None, :]
    cos = jnp.cos(ang)
    sin = jnp.sin(ang)
    return jnp.concatenate([cos, cos], axis=-1), jnp.concatenate([-sin, sin], axis=-1)


def kernel(x_prompt, x_sample, state_conv, state_ret, state_mlstm_C, state_mlstm_n, state_mlstm_m,
           norm_even, w_in_even, conv_w, ret_norm, w_out_even,
           norm_odd, w_in_odd, b_gate_odd, mlstm_norm, ln_v_g, ln_v_b,
           w_spatial, b_spatial, w_out_odd, norm_final):
    w_in_e = w_in_even[0]
    w_out_e = w_out_even[0].astype(BF16)
    w_o = w_in_odd[0].T
    w_out_o = w_out_odd[0].astype(BF16)
    g_even = norm_even[0][None, :]
    g_odd = norm_odd[0][None, :]
    g_fin = norm_final[None, :]
    cw = conv_w[0]
    g_ret = ret_norm[0][None, :]
    bg = jnp.concatenate([b_gate_odd[0], jnp.zeros((128 - 2 * H_C,), F32)])[None, :]
    gm = mlstm_norm[0][None, :]
    lng = ln_v_g[0][None, :]
    lnb = ln_v_b[0][None, :]
    ws = w_spatial[0]
    bst = b_spatial[0].T

    cos_p, sin_p = _rope_tables(jnp.arange(SEQ, dtype=jnp.int32))
    cos_s, sin_s = _rope_tables(PAST_LEN + jnp.arange(DEC_SEQ, dtype=jnp.int32))
    cos_s = jnp.tile(cos_s, (SB, 1))
    sin_s = jnp.tile(sin_s, (SB, 1))
    lg_tab = jnp.broadcast_to(jnp.asarray(LOG_GAMMA, F32)[:, None, None], (H_B, 1, 128))

    ws4 = ws[:, :DEC_SEQ, :DEC_SEQ]
    t_idx = jnp.arange(DEC_SEQ)
    rtab = []
    for j in range(DEC_SEQ):
        coef = ws4[:, t_idx, (t_idx - j) % DEC_SEQ]
        tab = jnp.repeat(coef.T[:, :, None], 128, axis=2).reshape(DEC_SEQ, W_D)
        rtab.append(jnp.tile(tab, (SB, 1)))
    rtab = jnp.stack(rtab)
    btab = jnp.tile(jnp.repeat(b_spatial[0][:, :DEC_SEQ].T[:, :, None], 128, axis=2)
                    .reshape(DEC_SEQ, W_D), (SB, 1))

    rs = DEC_BATCH * DEC_SEQ
    xp = x_prompt.reshape(BATCH * SEQ, D_MODEL)
    xs = x_sample.reshape(rs, D_MODEL)
    hp = _norm_cast(xp, g_even, 512)
    hs = _norm_cast(xs, g_even, 512)
    pp, ps = _in_proj(hp, hs, w_in_e, E_IN)
    ya, yb, conv_p, ret_p = _even_prompt(pp, cw, g_ret, cos_p, sin_p)
    st_exp = jnp.pad(state_conv[0], ((0, 0), (0, DEC_SEQ - (CONV_W - 1)), (0, 0))).reshape(rs, W_A)
    ya_s, u_s, yb_s, ret_s = _even_sample(ps, st_exp, state_ret[0], cw, g_ret, cos_s, sin_s, lg_tab)
    x1, h1 = _outproj(ya, yb, w_out_e, xp, g_odd, 512, final=False)
    x1s, h1s = _outproj(ya_s, yb_s, w_out_e, xs, g_odd, 512, final=False)

    p2, p2s = _in_proj(h1, h1s, w_o, O_N, shift_from=O_GATE // IN_TN, shift=N_GATE)
    yc, yd, c_p, n_p, m_p = _odd_prompt(p2, h1, w_o, bg, gm, lng, lnb, ws, bst)
    n_rows = jnp.repeat(jnp.transpose(state_mlstm_n[0], (1, 0, 2)), DEC_SEQ, axis=1)
    m_rows = jnp.repeat(state_mlstm_m[0].T, DEC_SEQ, axis=1)[:, :, None]
    yc_s, yd_s, vn_s, c_s, no_s, mo_s = _odd_sample(
        p2s, h1s, w_o, state_mlstm_C[0], n_rows, m_rows, bg, gm, lng, lnb, rtab, btab)
    y_prompt = _outproj(yc, yd, w_out_o, x1, g_fin, 512, final=True)
    y_sample = _outproj(yc_s, yd_s, w_out_o, x1s, g_fin, 512, final=True)

    conv_s = u_s.reshape(DEC_BATCH, DEC_SEQ, W_A)[:, DEC_SEQ - (CONV_W - 1):, :]
    n_s = jnp.transpose(no_s[:, DEC_SEQ - 1::DEC_SEQ, :], (1, 0, 2))
    m_s = mo_s[:, DEC_SEQ - 1::DEC_SEQ, 0].T
    return (y_prompt.reshape(BATCH, SEQ, D_MODEL),
            y_sample.reshape(DEC_BATCH, DEC_SEQ, D_MODEL),
            conv_p[None], conv_s[None],
            ret_p[None], ret_s[None],
            c_p[None], c_s[None],
            n_p[None], n_s[None],
            m_p[:, :H_C, 0][None], m_s[None],
            vn_s.reshape(DEC_BATCH, DEC_SEQ, W_D)[None])
```

```python
import functools
import math

import jax
import jax.numpy as jnp
from jax import lax
from jax.experimental import pallas as pl
from jax.experimental.pallas import tpu as pltpu

F32 = jnp.float32
BF16 = jnp.bfloat16

D_MODEL = 2048
BATCH = 4
SEQ = 2048
DEC_BATCH = 128
DEC_SEQ = 4
PAST_LEN = 16384
W_A = 1024
CONV_W = 3
W_B = 1024
H_B = 8
DH_B = 128
E_IN = 8192
W_C = 1024
H_C = 4
DV_C = 256
DQK_C = 128
W_D = 1024
G_D = 8
CHUNK = 128
O_GATE = 2 * H_C * DQK_C + 2 * W_C
N_GATE = 2 * H_C
O_N = O_GATE + 3 * W_D
ROPE_BASE = 10000.0
EPS = 1e-6
LOG_GAMMA = tuple(math.log(1.0 - 2.0 ** (-5.0 - h)) for h in range(H_B))
NEG_INF = float("-inf")
VMEM_LIMIT = 56 * 1024 * 1024

NT_DIMS = (((1,), (1,)), ((), ()))
TN_DIMS = (((0,), (0,)), ((), ()))


def _silu(z):
    return z * (1.0 / (1.0 + jnp.exp(-z)))


def _log_sigmoid(x):
    return jnp.minimum(x, 0.0) - jnp.log1p(jnp.exp(-jnp.abs(x)))


def _dot(a, b):
    return jnp.dot(a, b, preferred_element_type=F32)


def _dot_nt(a, b):
    return lax.dot_general(a, b, NT_DIMS, preferred_element_type=F32)


def _dot_tn(a, b):
    return lax.dot_general(a, b, TN_DIMS, preferred_element_type=F32)


def _dot_hi(a, b):
    return jnp.dot(a, b, preferred_element_type=F32, precision=lax.Precision.HIGHEST)


def _head_norm(o, g):
    mu = jnp.mean(o, axis=-1, keepdims=True)
    oc = o - mu
    var = jnp.mean(oc * oc, axis=-1, keepdims=True)
    return oc * lax.rsqrt(var + EPS) * g


def _params(sem):
    return pltpu.CompilerParams(dimension_semantics=sem, vmem_limit_bytes=VMEM_LIMIT)


def _norm_cast_kernel(x_ref, g_ref, h_ref):
    x = x_ref[...]
    ms = jnp.mean(x * x, axis=-1, keepdims=True)
    h_ref[...] = (x * lax.rsqrt(ms + EPS) * g_ref[...]).astype(BF16)


def _norm_cast(x, g, tm):
    m, d = x.shape
    return pl.pallas_call(
        _norm_cast_kernel,
        grid=(m // tm,),
        in_specs=[pl.BlockSpec((tm, d), lambda i: (i, 0)),
                  pl.BlockSpec((1, d), lambda i: (0, 0))],
        out_specs=pl.BlockSpec((tm, d), lambda i: (i, 0)),
        out_shape=jax.ShapeDtypeStruct((m, d), BF16),
        compiler_params=_params(("arbitrary",)),
        name="norm_cast",
    )(x, g)


IN_TM = 1024
IN_TN = 1024


def _in_proj_kernel(*refs, shift_from, shift):
    if shift:
        hp_ref, hs_ref, w_ref, wn_ref, op_ref, os_ref, wb = refs
    else:
        hp_ref, hs_ref, w_ref, op_ref, os_ref, wb = refs
    j = pl.program_id(0)
    i = pl.program_id(1)

    if shift:
        @pl.when(jnp.logical_and(i == 0, j < shift_from))
        def _():
            wb[...] = w_ref[...].astype(BF16)

        @pl.when(jnp.logical_and(i == 0, j >= shift_from))
        def _():
            wb[...] = jnp.concatenate([w_ref[shift:IN_TN, :], wn_ref[...]], axis=0).astype(BF16)

        mm = _dot_nt
    else:
        @pl.when(i == 0)
        def _():
            wb[...] = w_ref[...].astype(BF16)

        mm = _dot

    @pl.when(i == 0)
    def _():
        os_ref[...] = mm(hs_ref[...], wb[...])

    @pl.when(i > 0)
    def _():
        op_ref[...] = mm(hp_ref[...], wb[...])


def _in_proj(hp, hs, w, n_out, shift_from=0, shift=0):
    mp, k = hp.shape
    ms = hs.shape[0]
    n_prompt = mp // IN_TM
    prow = lambda j, i: (jnp.maximum(i - 1, 0), 0)
    in_specs = [pl.BlockSpec((IN_TM, k), prow),
                pl.BlockSpec((ms, k), lambda j, i: (0, 0))]
    args = [hp, hs, w]
    if shift:
        in_specs.append(pl.BlockSpec((IN_TN, k), lambda j, i: (j, 0)))
        in_specs.append(pl.BlockSpec((shift, k), lambda j, i: ((j + 1) * (IN_TN // shift), 0)))
        args.append(w)
        wb_shape = (IN_TN, k)
    else:
        in_specs.append(pl.BlockSpec((k, IN_TN), lambda j, i: (0, j)))
        wb_shape = (k, IN_TN)
    return pl.pallas_call(
        functools.partial(_in_proj_kernel, shift_from=shift_from, shift=shift),
        grid=(n_out // IN_TN, n_prompt + 1),
        in_specs=in_specs,
        out_specs=(pl.BlockSpec((IN_TM, IN_TN), lambda j, i: (jnp.maximum(i - 1, 0), j)),
                   pl.BlockSpec((ms, IN_TN), lambda j, i: (0, j))),
        out_shape=(jax.ShapeDtypeStruct((mp, n_out), F32),
                   jax.ShapeDtypeStruct((ms, n_out), F32)),
        scratch_shapes=[pltpu.VMEM(wb_shape, BF16)],
        compiler_params=_params(("arbitrary", "arbitrary")),
        name="in_proj",
    )(*args)


def _in_proj_sample_kernel(*refs, shift_from, shift):
    if shift:
        hs_ref, w_ref, wn_ref, os_ref = refs
        j = pl.program_id(0)

        @pl.when(j < shift_from)
        def _():
            os_ref[...] = _dot_nt(hs_ref[...], w_ref[...].astype(BF16))

        @pl.when(j >= shift_from)
        def _():
            wsh = jnp.concatenate([w_ref[shift:IN_TN, :], wn_ref[...]], axis=0)
            os_ref[...] = _dot_nt(hs_ref[...], wsh.astype(BF16))
    else:
        hs_ref, w_ref, os_ref = refs
        os_ref[...] = _dot(hs_ref[...], w_ref[...].astype(BF16))


def _in_proj_sample(hs, w, n_out, shift_from=0, shift=0):
    ms, k = hs.shape
    in_specs = [pl.BlockSpec((ms, k), lambda j: (0, 0))]
    args = [hs, w]
    if shift:
        in_specs.append(pl.BlockSpec((IN_TN, k), lambda j: (j, 0)))
        in_specs.append(pl.BlockSpec((shift, k), lambda j: ((j + 1) * (IN_TN // shift), 0)))
        args.append(w)
    else:
        in_specs.append(pl.BlockSpec((k, IN_TN), lambda j: (0, j)))
    return pl.pallas_call(
        functools.partial(_in_proj_sample_kernel, shift_from=shift_from, shift=shift),
        grid=(n_out // IN_TN,),
        in_specs=in_specs,
        out_specs=pl.BlockSpec((ms, IN_TN), lambda j: (0, j)),
        out_shape=jax.ShapeDtypeStruct((ms, n_out), F32),
        compiler_params=_params(("arbitrary",)),
        name="in_proj_sample",
    )(*args)


def _outproj_kernel(ya_ref, yb_ref, w_ref, x_ref, g_ref, *out_refs, final):
    half = ya_ref.shape[1]
    acc = _dot(ya_ref[...], w_ref[0:half, :]) + _dot(yb_ref[...], w_ref[half:2 * half, :])
    x1 = x_ref[...] + acc
    ms = jnp.mean(x1 * x1, axis=-1, keepdims=True)
    hn = x1 * lax.rsqrt(ms + EPS) * g_ref[...]
    if final:
        out_refs[0][...] = hn
    else:
        out_refs[0][...] = x1
        out_refs[1][...] = hn.astype(BF16)


def _outproj(ya, yb, w, x, g, tm, final):
    m, half = ya.shape
    d = w.shape[1]
    row = lambda i: (i, 0)
    if final:
        out_shape = jax.ShapeDtypeStruct((m, d), F32)
        out_specs = pl.BlockSpec((tm, d), row)
    else:
        out_shape = (jax.ShapeDtypeStruct((m, d), F32), jax.ShapeDtypeStruct((m, d), BF16))
        out_specs = (pl.BlockSpec((tm, d), row), pl.BlockSpec((tm, d), row))
    return pl.pallas_call(
        functools.partial(_outproj_kernel, final=final),
        grid=(m // tm,),
        in_specs=[pl.BlockSpec((tm, half), row),
                  pl.BlockSpec((tm, half), row),
                  pl.BlockSpec((2 * half, d), lambda i: (0, 0)),
                  pl.BlockSpec((tm, d), row),
                  pl.BlockSpec((1, d), lambda i: (0, 0))],
        out_specs=out_specs,
        out_shape=out_shape,
        compiler_params=_params(("arbitrary",)),
        name="out_proj_final" if final else "out_proj",
    )(ya, yb, w, x, g)


def _rope(x, cosf, sins):
    return x * cosf + pltpu.roll(x, DH_B // 2, 1) * sins


def _even_prompt_kernel(p_ref, cw_ref, gret_ref, cos_ref, sin_ref,
                        ya_ref, yb_ref, conv_ref, s_ref, ubuf):
    c = pl.program_id(1)
    L = CHUNK

    @pl.when(c == 0)
    def _():
        ubuf[0:8, :] = jnp.zeros((8, W_A), F32)
        s_ref[...] = jnp.zeros_like(s_ref)

    for j in range(W_A // 128):
        sl = slice(j * 128, (j + 1) * 128)
        a_b = p_ref[:, j * 128:(j + 1) * 128]
        a_c = p_ref[:, W_A + j * 128:W_A + (j + 1) * 128]
        a_x = p_ref[:, 2 * W_A + j * 128:2 * W_A + (j + 1) * 128]
        a_z = p_ref[:, 3 * W_A + j * 128:3 * W_A + (j + 1) * 128]
        u = a_c * a_x
        ubuf[8:8 + L, sl] = u
        t0 = ubuf[6:6 + L, sl]
        t1 = ubuf[7:7 + L, sl]
        conv = cw_ref[0:1, sl] * t0 + cw_ref[1:2, sl] * t1 + cw_ref[2:3, sl] * u
        ya_ref[:, sl] = (a_b * conv * _silu(a_z)).astype(BF16)
        ubuf[0:8, sl] = u[L - 8:L, :]
    conv_ref[0] = ubuf[6:8, :]

    cosf = cos_ref[...]
    sins = sin_ref[...]
    row = lax.broadcasted_iota(jnp.int32, (L, L), 0)
    col = lax.broadcasted_iota(jnp.int32, (L, L), 1)
    causal = row >= col
    diff = jnp.maximum(row - col, 0).astype(F32)
    ti = lax.broadcasted_iota(jnp.int32, (L, 1), 0).astype(F32)
    base = 4 * W_A
    for h in range(H_B):
        lg = LOG_GAMMA[h]
        sl = slice(h * DH_B, (h + 1) * DH_B)
        q = p_ref[:, base + h * DH_B:base + (h + 1) * DH_B]
        k = p_ref[:, base + W_B + h * DH_B:base + W_B + (h + 1) * DH_B]
        v = p_ref[:, base + 2 * W_B + h * DH_B:base + 2 * W_B + (h + 1) * DH_B]
        z = p_ref[:, base + 3 * W_B + h * DH_B:base + 3 * W_B + (h + 1) * DH_B]
        qr = _rope(q, cosf, sins)
        kr = _rope(k, cosf, sins) * (DH_B ** -0.5)
        decay = jnp.where(causal, jnp.exp(lg * diff), 0.0)
        qb = qr.astype(BF16)
        kb = kr.astype(BF16)
        vb = v.astype(BF16)
        sc = _dot_nt(qb, kb) * decay
        inner = _dot(sc.astype(BF16), vb)
        s_old = s_ref[0, h]
        cross = _dot(qb, s_old.astype(BF16)) * jnp.exp(lg * (ti + 1.0))
        kd = (kr * jnp.exp(lg * (L - 1.0 - ti))).astype(BF16)
        s_ref[0, h] = math.exp(lg * L) * s_old + _dot_tn(kd, vb)
        o = inner + cross
        yb_ref[:, sl] = (_head_norm(o, gret_ref[0:1, sl]) * _silu(z)).astype(BF16)


def _even_prompt(p, conv_w, g_ret, cosf, sins):
    nc = SEQ // CHUNK
    rows = lambda b, c: (b * nc + c, 0)
    const2 = lambda b, c: (0, 0)
    return pl.pallas_call(
        _even_prompt_kernel,
        grid=(BATCH, nc),
        in_specs=[pl.BlockSpec((CHUNK, E_IN), rows),
                  pl.BlockSpec((CONV_W, W_A), const2),
                  pl.BlockSpec((1, W_B), const2),
                  pl.BlockSpec((CHUNK, DH_B), lambda b, c: (c, 0)),
                  pl.BlockSpec((CHUNK, DH_B), lambda b, c: (c, 0))],
        out_specs=(pl.BlockSpec((CHUNK, W_A), rows),
                   pl.BlockSpec((CHUNK, W_B), rows),
                   pl.BlockSpec((1, CONV_W - 1, W_A), lambda b, c: (b, 0, 0)),
                   pl.BlockSpec((1, H_B, DH_B, DH_B), lambda b, c: (b, 0, 0, 0))),
        out_shape=(jax.ShapeDtypeStruct((BATCH * SEQ, W_A), BF16),
                   jax.ShapeDtypeStruct((BATCH * SEQ, W_B), BF16),
                   jax.ShapeDtypeStruct((BATCH, CONV_W - 1, W_A), F32),
                   jax.ShapeDtypeStruct((BATCH, H_B, DH_B, DH_B), F32)),
        scratch_shapes=[pltpu.VMEM((CHUNK + 8, W_A), F32)],
        compiler_params=_params(("arbitrary", "arbitrary")),
        name="even_prompt",
    )(p, conv_w, g_ret, cosf, sins)


FT = 1024
FN = BATCH * SEQ // FT
FPB = SEQ // FT


def _fused_steps(s, body, p0, p1):
    @pl.when(s == 0)
    def _():
        body(p0, None)

    @pl.when(s % 2 == 1)
    def _():
        body(p1, p0)

    @pl.when(jnp.logical_and(s > 0, jnp.logical_and(s < FN, s % 2 == 0)))
    def _():
        body(p0, p1)

    @pl.when(s == FN)
    def _():
        body(None, p1)


def _interleave(n, piece, stages):
    sa, sb, sc, sd, se = stages
    for c in range(n + 2):
        if c < n:
            if piece is not None:
                piece(c)
            if sa is not None:
                sa(c)
        if sa is None:
            continue
        if 1 <= c <= n:
            sc(c - 1)
        if c < n:
            sb(c)
        if 1 <= c <= n:
            sd(c - 1)
        if 2 <= c:
            se(c - 2)


def _even_heads_kernel(hp_ref, wq_ref, wk_ref, wv_ref, wz_ref, gret_ref, cos_ref, sin_ref, lg_ref,
                       yb_ref, s_ref, wb, p0, p1, s_scr):
    s = pl.program_id(1)
    L = CHUNK

    @pl.when(s == 0)
    def _():
        wb[:, 0:DH_B] = wq_ref[...].astype(BF16)
        wb[:, DH_B:2 * DH_B] = wk_ref[...].astype(BF16)
        wb[:, 2 * DH_B:3 * DH_B] = wv_ref[...].astype(BF16)
        wb[:, 3 * DH_B:4 * DH_B] = wz_ref[...].astype(BF16)

    def body(p_new, p_old):
        n = FT // L
        rows = lambda c: slice(c * L, (c + 1) * L)
        piece = None
        if p_new is not None:
            def piece(c):
                p_new[rows(c), :] = _dot(hp_ref[rows(c), :], wb[...])
        if p_old is None:
            _interleave(n, piece, (None,) * 5)
            return

        t = s - 1
        lg = lg_ref[0][:, 0:1]
        row = lax.broadcasted_iota(jnp.int32, (L, L), 0)
        col = lax.broadcasted_iota(jnp.int32, (L, L), 1)
        decay = jnp.where(row >= col, jnp.exp(lg * jnp.maximum(row - col, 0).astype(F32)), 0.0)
        ti = lax.broadcasted_iota(jnp.int32, (L, 1), 0).astype(F32)
        q_decay = jnp.exp(lg * (ti + 1.0))
        k_decay = jnp.exp(lg * (L - 1.0 - ti))
        gamma_l = jnp.exp(lg * float(L))
        g = gret_ref[...]
        state = {0: jnp.where(t % FPB == 0, 0.0, s_scr[...])}
        v = {}

        def stage_a(c):
            cosf = cos_ref[rows(c), :]
            sins = sin_ref[rows(c), :]
            kr = _rope(p_old[rows(c), DH_B:2 * DH_B], cosf, sins) * (DH_B ** -0.5)
            v[c] = dict(qb=_rope(p_old[rows(c), 0:DH_B], cosf, sins).astype(BF16),
                        kb=kr.astype(BF16),
                        kd=(kr * k_decay).astype(BF16),
                        vb=p_old[rows(c), 2 * DH_B:3 * DH_B].astype(BF16))

        def stage_b(c):
            d = v[c]
            d["sc"] = _dot_nt(d["qb"], d["kb"])
            d["cross"] = _dot(d["qb"], state[c].astype(BF16))
            d["upd"] = _dot_tn(d["kd"], d["vb"])

        def stage_c(c):
            d = v[c]
            d["sc"] = (d["sc"] * decay).astype(BF16)
            state[c + 1] = gamma_l * state[c] + d["upd"]

        def stage_d(c):
            d = v[c]
            d["inner"] = _dot(d["sc"], d["vb"])

        def stage_e(c):
            d = v.pop(c)
            o = d["inner"] + d["cross"] * q_decay
            z = p_old[rows(c), 3 * DH_B:4 * DH_B]
            yb_ref[rows(c), :] = (_head_norm(o, g) * _silu(z)).astype(BF16)

        _interleave(n, piece, (stage_a, stage_b, stage_c, stage_d, stage_e))
        s_scr[...] = state[n]
        s_ref[0, 0] = state[n]

    _fused_steps(s, body, p0, p1)


def _even_heads(hp, w, g_ret, cosf, sins, lg_tab):
    k = hp.shape[1]
    base = 4 * W_A // DH_B
    nh = W_B // DH_B
    tile = lambda h, s: (jnp.minimum(s, FN - 1), 0)
    prev = lambda s: jnp.maximum(s - 1, 0)
    wspec = lambda part: pl.BlockSpec((k, DH_B), lambda h, s: (0, base + part * nh + h))
    return pl.pallas_call(
        _even_heads_kernel,
        grid=(H_B, FN + 1),
        in_specs=[pl.BlockSpec((FT, k), tile),
                  wspec(0), wspec(1), wspec(2), wspec(3),
                  pl.BlockSpec((1, DH_B), lambda h, s: (0, h)),
                  pl.BlockSpec((FT, DH_B), lambda h, s: (prev(s) % FPB, 0)),
                  pl.BlockSpec((FT, DH_B), lambda h, s: (prev(s) % FPB, 0)),
                  pl.BlockSpec((1, 1, 128), lambda h, s: (h, 0, 0))],
        out_specs=(pl.BlockSpec((FT, DH_B), lambda h, s: (prev(s), h)),
                   pl.BlockSpec((1, 1, DH_B, DH_B), lambda h, s: (prev(s) // FPB, h, 0, 0))),
        out_shape=(jax.ShapeDtypeStruct((BATCH * SEQ, W_B), BF16),
                   jax.ShapeDtypeStruct((BATCH, H_B, DH_B, DH_B), F32)),
        scratch_shapes=[pltpu.VMEM((k, 4 * DH_B), BF16),
                        pltpu.VMEM((FT, 4 * DH_B), F32),
                        pltpu.VMEM((FT, 4 * DH_B), F32),
                        pltpu.VMEM((DH_B, DH_B), F32)],
        compiler_params=_params(("arbitrary", "arbitrary")),
        name="even_heads",
    )(hp, w, w, w, w, g_ret, cosf, sins, lg_tab)


def _even_conv_kernel(hp_ref, wb_ref, wc_ref, wx_ref, wz_ref, cw_ref,
                      ya_ref, conv_ref, wb, p0, p1, ubuf):
    s = pl.program_id(1)
    L = CHUNK
    G = 128

    @pl.when(s == 0)
    def _():
        wb[:, 0:G] = wb_ref[...].astype(BF16)
        wb[:, G:2 * G] = wc_ref[...].astype(BF16)
        wb[:, 2 * G:3 * G] = wx_ref[...].astype(BF16)
        wb[:, 3 * G:4 * G] = wz_ref[...].astype(BF16)

    def body(p_new, p_old):
        n = FT // L
        rows = lambda c: slice(c * L, (c + 1) * L)
        if p_old is not None:
            t = s - 1

            @pl.when(t % FPB == 0)
            def _():
                ubuf[0:8, :] = jnp.zeros((8, G), F32)

            @pl.when(t % FPB != 0)
            def _():
                ubuf[0:8, :] = ubuf[FT:FT + 8, :]

        for c in range(n):
            if p_new is not None:
                p_new[rows(c), :] = _dot(hp_ref[rows(c), :], wb[...])
            if p_old is not None:
                u = p_old[rows(c), G:2 * G] * p_old[rows(c), 2 * G:3 * G]
                ubuf[8 + c * L:8 + (c + 1) * L, :] = u
                t0 = ubuf[6 + c * L:6 + (c + 1) * L, :]
                t1 = ubuf[7 + c * L:7 + (c + 1) * L, :]
                conv = cw_ref[0:1, :] * t0 + cw_ref[1:2, :] * t1 + cw_ref[2:3, :] * u
                ya_ref[rows(c), :] = (p_old[rows(c), 0:G] * conv * _silu(p_old[rows(c), 3 * G:4 * G])).astype(BF16)
        if p_old is not None:
            conv_ref[0] = ubuf[FT + 6:FT + 8, :]

    _fused_steps(s, body, p0, p1)


def _even_conv(hp, w, conv_w):
    k = hp.shape[1]
    ng = W_A // 128
    tile = lambda g, s: (jnp.minimum(s, FN - 1), 0)
    prev = lambda s: jnp.maximum(s - 1, 0)
    wspec = lambda part: pl.BlockSpec((k, 128), lambda g, s: (0, part * ng + g))
    return pl.pallas_call(
        _even_conv_kernel,
        grid=(ng, FN + 1),
        in_specs=[pl.BlockSpec((FT, k), tile),
                  wspec(0), wspec(1), wspec(2), wspec(3),
                  pl.BlockSpec((CONV_W, 128), lambda g, s: (0, g))],
        out_specs=(pl.BlockSpec((FT, 128), lambda g, s: (prev(s), g)),
                   pl.BlockSpec((1, CONV_W - 1, 128), lambda g, s: (prev(s) // FPB, 0, g))),
        out_shape=(jax.ShapeDtypeStruct((BATCH * SEQ, W_A), BF16),
                   jax.ShapeDtypeStruct((BATCH, CONV_W - 1, W_A), F32)),
        scratch_shapes=[pltpu.VMEM((k, 4 * 128), BF16),
                        pltpu.VMEM((FT, 4 * 128), F32),
                        pltpu.VMEM((FT, 4 * 128), F32),
                        pltpu.VMEM((FT + 8, 128), F32)],
        compiler_params=_params(("arbitrary", "arbitrary")),
        name="even_conv",
    )(hp, w, w, w, w, conv_w)


SB = 32
SR = SB * DEC_SEQ


def _even_sample_kernel(pa_ref, pq_ref, pk_ref, pv_ref, pz_ref, st_ref, s_ref,
                        cw_ref, gret_ref, cos_ref, sin_ref, lg_ref,
                        ya_ref, u_ref, yb_ref, so_ref, cross_scr):
    h = pl.program_id(1)
    row = lax.broadcasted_iota(jnp.int32, (SR, SR), 0)
    col = lax.broadcasted_iota(jnp.int32, (SR, SR), 1)
    trow = row & 3

    @pl.when(h == 0)
    def _():
        for j in range(W_A // 128):
            sl = slice(j * 128, (j + 1) * 128)
            a_b = pa_ref[:, j * 128:(j + 1) * 128]
            a_c = pa_ref[:, W_A + j * 128:W_A + (j + 1) * 128]
            a_x = pa_ref[:, 2 * W_A + j * 128:2 * W_A + (j + 1) * 128]
            a_z = pa_ref[:, 3 * W_A + j * 128:3 * W_A + (j + 1) * 128]
            u = a_c * a_x
            e = st_ref[:, sl]
            tap1 = jnp.where(trow >= 1, pltpu.roll(u, 1, 0), pltpu.roll(e, SR - 1, 0))
            tap0 = jnp.where(trow >= 2, pltpu.roll(u, 2, 0), e)
            conv = cw_ref[0:1, sl] * tap0 + cw_ref[1:2, sl] * tap1 + cw_ref[2:3, sl] * u
            ya_ref[:, sl] = (a_b * conv * _silu(a_z)).astype(BF16)
            u_ref[:, sl] = u

    lg = lg_ref[0][:, 0:1]
    same = (row >> 2) == (col >> 2)
    dd = trow - (col & 3)
    mask = jnp.where(same, dd, -1) >= 0
    decay = jnp.where(mask, jnp.exp(lg * jnp.maximum(dd, 0).astype(F32)), 0.0)
    tcol = (lax.broadcasted_iota(jnp.int32, (SR, 1), 0) & 3).astype(F32)
    cosf = cos_ref[...]
    sins = sin_ref[...]
    qr = _rope(pq_ref[...], cosf, sins)
    kr = _rope(pk_ref[...], cosf, sins) * (DH_B ** -0.5)
    qb = qr.astype(BF16)
    kb = kr.astype(BF16)
    vb = pv_ref[...].astype(BF16)
    sc = _dot_nt(qb, kb) * decay
    inner = _dot(sc.astype(BF16), vb)
    kdt = (kr * jnp.exp(lg * (DEC_SEQ - 1.0 - tcol))).T
    gamma_l = jnp.exp(lg * float(DEC_SEQ))
    lane_b = col >> 2
    sub = lax.broadcasted_iota(jnp.int32, (8, DH_B), 0)
    for g in range(SR // 8):
        q8 = qb[8 * g:8 * g + 8, :]
        res = []
        for beta in range(2):
            b = 2 * g + beta
            s_old = s_ref[b, 0]
            res.append(_dot(q8, s_old.astype(BF16)))
            lhs = jnp.where(lane_b == b, kdt, 0.0).astype(BF16)
            so_ref[b, 0] = gamma_l * s_old + _dot(lhs, vb)
        cross_scr[8 * g:8 * g + 8, :] = jnp.where(sub < DEC_SEQ, res[0], res[1])
    o = inner + cross_scr[...] * jnp.exp(lg * (tcol + 1.0))
    yb_ref[...] = (_head_norm(o, gret_ref[...]) * _silu(pz_ref[...])).astype(BF16)


def _even_sample(p, st_exp, s_state, conv_w, g_ret, cosf, sins, lg_tab):
    nb = DEC_BATCH // SB
    qoff = 4 * W_A // DH_B
    hb = W_B // DH_B
    const2 = lambda i, h: (0, 0)
    return pl.pallas_call(
        _even_sample_kernel,
        grid=(nb, H_B),
        in_specs=[pl.BlockSpec((SR, 4 * W_A), lambda i, h: (i, 0)),
                  pl.BlockSpec((SR, DH_B), lambda i, h: (i, qoff + h)),
                  pl.BlockSpec((SR, DH_B), lambda i, h: (i, qoff + hb + h)),
                  pl.BlockSpec((SR, DH_B), lambda i, h: (i, qoff + 2 * hb + h)),
                  pl.BlockSpec((SR, DH_B), lambda i, h: (i, qoff + 3 * hb + h)),
                  pl.BlockSpec((SR, W_A), lambda i, h: (i, 0)),
                  pl.BlockSpec((SB, 1, DH_B, DH_B), lambda i, h: (i, h, 0, 0)),
                  pl.BlockSpec((CONV_W, W_A), const2),
                  pl.BlockSpec((1, DH_B), lambda i, h: (0, h)),
                  pl.BlockSpec((SR, DH_B), const2),
                  pl.BlockSpec((SR, DH_B), const2),
                  pl.BlockSpec((1, 1, 128), lambda i, h: (h, 0, 0))],
        out_specs=(pl.BlockSpec((SR, W_A), lambda i, h: (i, 0)),
                   pl.BlockSpec((SR, W_A), lambda i, h: (i, 0)),
                   pl.BlockSpec((SR, DH_B), lambda i, h: (i, h)),
                   pl.BlockSpec((SB, 1, DH_B, DH_B), lambda i, h: (i, h, 0, 0))),
        out_shape=(jax.ShapeDtypeStruct((DEC_BATCH * DEC_SEQ, W_A), BF16),
                   jax.ShapeDtypeStruct((DEC_BATCH * DEC_SEQ, W_A), F32),
                   jax.ShapeDtypeStruct((DEC_BATCH * DEC_SEQ, W_B), BF16),
                   jax.ShapeDtypeStruct((DEC_BATCH, H_B, DH_B, DH_B), F32)),
        scratch_shapes=[pltpu.VMEM((SR, DH_B), F32)],
        compiler_params=_params(("arbitrary", "arbitrary")),
        name="even_sample",
    )(p, p, p, p, p, st_exp, s_state, conv_w, g_ret, cosf, sins, lg_tab)


def _odd_prompt_kernel(p_ref, h_ref, wg_ref, bg_ref, gm_ref, lng_ref, lnb_ref, ws_ref, bst_ref,
                       yc_ref, yd_ref, c_ref, n_ref, m_ref, wgb):
    c = pl.program_id(1)
    L = CHUNK

    @pl.when(jnp.logical_and(pl.program_id(0) == 0, c == 0))
    def _():
        wgb[...] = wg_ref[...].astype(BF16)

    @pl.when(c == 0)
    def _():
        c_ref[...] = jnp.zeros_like(c_ref)
        n_ref[...] = jnp.zeros_like(n_ref)
        m_ref[...] = jnp.zeros_like(m_ref)

    row = lax.broadcasted_iota(jnp.int32, (L, L), 0)
    col = lax.broadcasted_iota(jnp.int32, (L, L), 1)
    tri = row >= col

    pre = _dot_nt(h_ref[...], wgb[...]) + bg_ref[...]
    lf = _log_sigmoid(pre)
    b_c = _dot_hi(jnp.where(tri, 1.0, 0.0), lf)
    b_r = b_c.T
    pre_r = pre.T
    for h in range(H_C):
        bc = b_c[:, H_C + h:H_C + h + 1]
        br = b_r[H_C + h:H_C + h + 1, :]
        igr = pre_r[h:h + 1, :]
        igc = pre[:, h:h + 1]
        m_prev = m_ref[0, h:h + 1, 0:1]
        log_d = jnp.where(tri, bc - br + igr, NEG_INF)
        log_inter = bc + m_prev
        m_t = jnp.maximum(log_inter, jnp.max(log_d, axis=-1, keepdims=True))
        w = jnp.exp(log_d - m_t)
        w_inter = jnp.exp(log_inter - m_t)
        q = p_ref[:, h * DQK_C:(h + 1) * DQK_C] * (DQK_C ** -0.5)
        k = p_ref[:, H_C * DQK_C + h * DQK_C:H_C * DQK_C + (h + 1) * DQK_C]
        v = p_ref[:, 2 * H_C * DQK_C + h * DV_C:2 * H_C * DQK_C + (h + 1) * DV_C]
        z = p_ref[:, 2 * H_C * DQK_C + W_C + h * DV_C:2 * H_C * DQK_C + W_C + (h + 1) * DV_C]
        qb = q.astype(BF16)
        kb = k.astype(BF16)
        vb = v.astype(BF16)
        sc = _dot_nt(qb, kb) * w
        c_old = c_ref[0, h]
        n_old = n_ref[0, h:h + 1, :]
        num = _dot(sc.astype(BF16), vb) + w_inter * _dot_nt(qb, c_old.astype(BF16))
        den = jnp.sum(sc, axis=-1, keepdims=True) + w_inter * jnp.sum(q * n_old, axis=-1, keepdims=True)
        hh = num / jnp.maximum(jnp.abs(den), jnp.exp(-m_t))
        m_new = m_t[L - 1:L, :]
        b_last = bc[L - 1:L, :]
        w_end = jnp.exp(b_last - bc + igc - m_new)
        cd = jnp.exp(b_last + m_prev - m_new)
        c_ref[0, h] = cd * c_old + _dot_tn((v * w_end).astype(BF16), kb)
        n_ref[0, h:h + 1, :] = cd * n_old + jnp.sum(w_end * k, axis=0, keepdims=True)
        m_ref[0, h:h + 1, :] = jnp.broadcast_to(m_new, (1, 128))
        sl = slice(h * DV_C, (h + 1) * DV_C)
        yc_ref[:, sl] = (_head_norm(hh, gm_ref[0:1, sl]) * _silu(z)).astype(BF16)

    dbase = 2 * H_C * DQK_C + 2 * W_C
    dv = p_ref[:, dbase + W_D:dbase + 2 * W_D]
    mu = jnp.mean(dv, axis=-1, keepdims=True)
    xc = dv - mu
    var = jnp.mean(xc * xc, axis=-1, keepdims=True)
    rstd = lax.rsqrt(var + EPS)
    for g in range(G_D):
        sl = slice(g * 128, (g + 1) * 128)
        vn = xc[:, sl] * rstd * lng_ref[0:1, sl] + lnb_ref[0:1, sl]
        ws = jnp.where(tri, ws_ref[g], 0.0).astype(BF16)
        s = _dot(ws, vn.astype(BF16)) + bst_ref[:, g:g + 1]
        d_u = p_ref[:, dbase + g * 128:dbase + (g + 1) * 128]
        d_z = p_ref[:, dbase + 2 * W_D + g * 128:dbase + 2 * W_D + (g + 1) * 128]
        yd_ref[:, sl] = (d_u * s * _silu(d_z)).astype(BF16)


def _odd_prompt(p, h, w_o, bg, gm, lng, lnb, ws, bst):
    nc = SEQ // CHUNK
    rows = lambda b, c: (b * nc + c, 0)
    const2 = lambda b, c: (0, 0)
    return pl.pallas_call(
        _odd_prompt_kernel,
        grid=(BATCH, nc),
        in_specs=[pl.BlockSpec((CHUNK, O_N), rows),
                  pl.BlockSpec((CHUNK, D_MODEL), rows),
                  pl.BlockSpec((128, D_MODEL), lambda b, c: (O_GATE // 128, 0)),
                  pl.BlockSpec((1, 128), const2),
                  pl.BlockSpec((1, W_C), const2),
                  pl.BlockSpec((1, W_D), const2),
                  pl.BlockSpec((1, W_D), const2),
                  pl.BlockSpec((G_D, CHUNK, CHUNK), lambda b, c: (0, 0, 0)),
                  pl.BlockSpec((CHUNK, G_D), const2)],
        out_specs=(pl.BlockSpec((CHUNK, W_C), rows),
                   pl.BlockSpec((CHUNK, W_D), rows),
                   pl.BlockSpec((1, H_C, DV_C, DQK_C), lambda b, c: (b, 0, 0, 0)),
                   pl.BlockSpec((1, H_C, DQK_C), lambda b, c: (b, 0, 0)),
                   pl.BlockSpec((1, 8, 128), lambda b, c: (b, 0, 0))),
        out_shape=(jax.ShapeDtypeStruct((BATCH * SEQ, W_C), BF16),
                   jax.ShapeDtypeStruct((BATCH * SEQ, W_D), BF16),
                   jax.ShapeDtypeStruct((BATCH, H_C, DV_C, DQK_C), F32),
                   jax.ShapeDtypeStruct((BATCH, H_C, DQK_C), F32),
                   jax.ShapeDtypeStruct((BATCH, 8, 128), F32)),
        scratch_shapes=[pltpu.VMEM((128, D_MODEL), BF16)],
        compiler_params=_params(("arbitrary", "arbitrary")),
        name="odd_prompt",
    )(p, h, w_o, bg, gm, lng, lnb, ws, bst)


def _odd_sample_kernel(pq_ref, pk_ref, pv_ref, pz_ref, h_ref, wg_ref, pd_ref,
                       c_ref, nrow_ref, mrow_ref, bg_ref, gm_ref, lng_ref, lnb_ref,
                       rtab_ref, btab_ref,
                       yc_ref, yd_ref, vn_ref, co_ref, no_ref, mo_ref,
                       inter_scr):
    h = pl.program_id(1)
    row = lax.broadcasted_iota(jnp.int32, (SR, SR), 0)
    col = lax.broadcasted_iota(jnp.int32, (SR, SR), 1)
    trow = row & 3

    @pl.when(h == 0)
    def _():
        dv = pd_ref[:, W_D:2 * W_D]
        mu = jnp.mean(dv, axis=-1, keepdims=True)
        xc = dv - mu
        var = jnp.mean(xc * xc, axis=-1, keepdims=True)
        rstd = lax.rsqrt(var + EPS)
        for g in range(G_D):
            sl = slice(g * 128, (g + 1) * 128)
            vn = xc[:, sl] * rstd * lng_ref[0:1, sl] + lnb_ref[0:1, sl]
            vn_ref[:, sl] = vn
            s = rtab_ref[0, :, sl] * vn + btab_ref[:, sl]
            for j in range(1, DEC_SEQ):
                s = s + jnp.where(trow >= j, rtab_ref[j, :, sl] * pltpu.roll(vn, j, 0), 0.0)
            d_u = pd_ref[:, g * 128:(g + 1) * 128]
            d_z = pd_ref[:, 2 * W_D + g * 128:2 * W_D + (g + 1) * 128]
            yd_ref[:, sl] = (d_u * s * _silu(d_z)).astype(BF16)

    same = (row >> 2) == (col >> 2)
    mask = jnp.where(same, trow - (col & 3), -1) >= 0
    pre = _dot_nt(h_ref[...], wg_ref[...].astype(BF16)) + bg_ref[...]
    lf = _log_sigmoid(pre)
    b_full = _dot_hi(jnp.where(mask, 1.0, 0.0), lf)
    sel_i = col == h
    sel_f = col == h + H_C
    ig_c = jnp.sum(jnp.where(sel_i, pre, 0.0), axis=-1, keepdims=True)
    b_c = jnp.sum(jnp.where(sel_f, b_full, 0.0), axis=-1, keepdims=True)
    sel_ir = row == h
    sel_fr = row == h + H_C
    ig_r = jnp.sum(jnp.where(sel_ir, pre.T, 0.0), axis=0, keepdims=True)
    b_r = jnp.sum(jnp.where(sel_fr, b_full.T, 0.0), axis=0, keepdims=True)
    m_prev = mrow_ref[0]
    log_d = jnp.where(mask, b_c - b_r + ig_r, NEG_INF)
    log_inter = b_c + m_prev
    m_t = jnp.maximum(log_inter, jnp.max(log_d, axis=-1, keepdims=True))
    w = jnp.exp(log_d - m_t)
    w_inter = jnp.exp(log_inter - m_t)
    q = pq_ref[...] * (DQK_C ** -0.5)
    k = pk_ref[...]
    v = pv_ref[...]
    qb = q.astype(BF16)
    kb = k.astype(BF16)
    vb = v.astype(BF16)
    sc = _dot_nt(qb, kb) * w
    sub = lax.broadcasted_iota(jnp.int32, (8, DV_C), 0)
    for g in range(SR // 8):
        q8 = qb[8 * g:8 * g + 8, :]
        r0 = _dot_nt(q8, c_ref[2 * g, 0].astype(BF16))
        r1 = _dot_nt(q8, c_ref[2 * g + 1, 0].astype(BF16))
        inter_scr[8 * g:8 * g + 8, :] = jnp.where(sub < DEC_SEQ, r0, r1)
    n_rows = nrow_ref[0]
    num = _dot(sc.astype(BF16), vb) + w_inter * inter_scr[...]
    den = jnp.sum(sc, axis=-1, keepdims=True) + w_inter * jnp.sum(q * n_rows, axis=-1, keepdims=True)
    hh = num / jnp.maximum(jnp.abs(den), jnp.exp(-m_t))
    yc_ref[...] = (_head_norm(hh, gm_ref[...]) * _silu(pz_ref[...])).astype(BF16)

    stats = jnp.where(col == 0, m_t, jnp.where(col == 1, b_c, 0.0))
    last = _dot_hi(jnp.where(col == (row | 3), 1.0, 0.0), stats)
    m_new = last[:, 0:1]
    b_last = last[:, 1:2]
    w_end = jnp.exp(b_last - b_c + ig_c - m_new)
    cd = jnp.exp(b_last + m_prev - m_new)
    mo_ref[0] = m_new
    no_ref[0] = cd * n_rows + _dot_hi(jnp.where(same, 1.0, 0.0), w_end * k)
    vwt = (v * w_end).T
    lane_b = lax.broadcasted_iota(jnp.int32, (DV_C, SR), 1) >> 2
    for b in range(SB):
        lhs = jnp.where(lane_b == b, vwt, 0.0).astype(BF16)
        cd_b = cd[4 * b + 3:4 * b + 4, :]
        co_ref[b, 0] = cd_b * c_ref[b, 0] + _dot(lhs, kb)


def _odd_sample(p, h, w_o, c_state, n_rows, m_rows, bg, gm, lng, lnb, rtab, btab):
    nb = DEC_BATCH // SB
    const2 = lambda i, h: (0, 0)
    koff = H_C * DQK_C // DQK_C
    voff = 2 * H_C * DQK_C // DV_C
    zoff = (2 * H_C * DQK_C + W_C) // DV_C
    return pl.pallas_call(
        _odd_sample_kernel,
        grid=(nb, H_C),
        in_specs=[pl.BlockSpec((SR, DQK_C), lambda i, h: (i, h)),
                  pl.BlockSpec((SR, DQK_C), lambda i, h: (i, koff + h)),
                  pl.BlockSpec((SR, DV_C), lambda i, h: (i, voff + h)),
                  pl.BlockSpec((SR, DV_C), lambda i, h: (i, zoff + h)),
                  pl.BlockSpec((SR, D_MODEL), lambda i, h: (i, 0)),
                  pl.BlockSpec((128, D_MODEL), lambda i, h: (O_GATE // 128, 0)),
                  pl.BlockSpec((SR, 3 * W_D), lambda i, h: (i, 1)),
                  pl.BlockSpec((SB, 1, DV_C, DQK_C), lambda i, h: (i, h, 0, 0)),
                  pl.BlockSpec((1, SR, DQK_C), lambda i, h: (h, i, 0)),
                  pl.BlockSpec((1, SR, 1), lambda i, h: (h, i, 0)),
                  pl.BlockSpec((1, 128), const2),
                  pl.BlockSpec((1, DV_C), lambda i, h: (0, h)),
                  pl.BlockSpec((1, W_D), const2),
                  pl.BlockSpec((1, W_D), const2),
                  pl.BlockSpec((DEC_SEQ, SR, W_D), lambda i, h: (0, 0, 0)),
                  pl.BlockSpec((SR, W_D), const2)],
        out_specs=(pl.BlockSpec((SR, DV_C), lambda i, h: (i, h)),
                   pl.BlockSpec((SR, W_D), lambda i, h: (i, 0)),
                   pl.BlockSpec((SR, W_D), lambda i, h: (i, 0)),
                   pl.BlockSpec((SB, 1, DV_C, DQK_C), lambda i, h: (i, h, 0, 0)),
                   pl.BlockSpec((1, SR, DQK_C), lambda i, h: (h, i, 0)),
                   pl.BlockSpec((1, SR, 1), lambda i, h: (h, i, 0))),
        out_shape=(jax.ShapeDtypeStruct((DEC_BATCH * DEC_SEQ, W_C), BF16),
                   jax.ShapeDtypeStruct((DEC_BATCH * DEC_SEQ, W_D), BF16),
                   jax.ShapeDtypeStruct((DEC_BATCH * DEC_SEQ, W_D), F32),
                   jax.ShapeDtypeStruct((DEC_BATCH, H_C, DV_C, DQK_C), F32),
                   jax.ShapeDtypeStruct((H_C, DEC_BATCH * DEC_SEQ, DQK_C), F32),
                   jax.ShapeDtypeStruct((H_C, DEC_BATCH * DEC_SEQ, 1), F32)),
        scratch_shapes=[pltpu.VMEM((SR, DV_C), F32)],
        compiler_params=_params(("arbitrary", "arbitrary")),
        name="odd_sample",
    )(p, p, p, p, h, w_o, p, c_state, n_rows, m_rows, bg, gm, lng, lnb, rtab, btab)


def _rope_tables(pos):
    inv = ROPE_BASE ** (-jnp.arange(0, DH_B, 2, dtype=F32) / DH_B)
    ang = pos.astype(F32)[:, None] * inv[None, :]
    cos = jnp.cos(ang)
    sin = jnp.sin(ang)
    return jnp.concatenate([cos, cos], axis=-1), jnp.concatenate([-sin, sin], axis=-1)


def kernel(x_prompt, x_sample, state_conv, state_ret, state_mlstm_C, state_mlstm_n, state_mlstm_m,
           norm_even, w_in_even, conv_w, ret_norm, w_out_even,
           norm_odd, w_in_odd, b_gate_odd, mlstm_norm, ln_v_g, ln_v_b,
           w_spatial, b_spatial, w_out_odd, norm_final):
    w_in_e = w_in_even[0]
    w_out_e = w_out_even[0].astype(BF16)
    w_o = w_in_odd[0].T
    w_out_o = w_out_odd[0].astype(BF16)
    g_even = norm_even[0][None, :]
    g_odd = norm_odd[0][None, :]
    g_fin = norm_final[None, :]
    cw = conv_w[0]
    g_ret = ret_norm[0][None, :]
    bg = jnp.concatenate([b_gate_odd[0], jnp.zeros((128 - 2 * H_C,), F32)])[None, :]
    gm = mlstm_norm[0][None, :]
    lng = ln_v_g[0][None, :]
    lnb = ln_v_b[0][None, :]
    ws = w_spatial[0]
    bst = b_spatial[0].T

    cos_p, sin_p = _rope_tables(jnp.arange(SEQ, dtype=jnp.int32))
    cos_s, sin_s = _rope_tables(PAST_LEN + jnp.arange(DEC_SEQ, dtype=jnp.int32))
    cos_s = jnp.tile(cos_s, (SB, 1))
    sin_s = jnp.tile(sin_s, (SB, 1))
    lg_tab = jnp.broadcast_to(jnp.asarray(LOG_GAMMA, F32)[:, None, None], (H_B, 1, 128))

    ws4 = ws[:, :DEC_SEQ, :DEC_SEQ]
    t_idx = jnp.arange(DEC_SEQ)
    rtab = []
    for j in range(DEC_SEQ):
        coef = ws4[:, t_idx, (t_idx - j) % DEC_SEQ]
        tab = jnp.repeat(coef.T[:, :, None], 128, axis=2).reshape(DEC_SEQ, W_D)
        rtab.append(jnp.tile(tab, (SB, 1)))
    rtab = jnp.stack(rtab)
    btab = jnp.tile(jnp.repeat(b_spatial[0][:, :DEC_SEQ].T[:, :, None], 128, axis=2)
                    .reshape(DEC_SEQ, W_D), (SB, 1))

    rs = DEC_BATCH * DEC_SEQ
    xp = x_prompt.reshape(BATCH * SEQ, D_MODEL)
    xs = x_sample.reshape(rs, D_MODEL)
    hp = _norm_cast(xp, g_even, 512)
    hs = _norm_cast(xs, g_even, 512)
    ya, conv_p = _even_conv(hp, w_in_e, cw)
    yb, ret_p = _even_heads(hp, w_in_e, g_ret, cos_p, sin_p, lg_tab)
    ps = _in_proj_sample(hs, w_in_e, E_IN)
    st_exp = jnp.pad(state_conv[0], ((0, 0), (0, DEC_SEQ - (CONV_W - 1)), (0, 0))).reshape(rs, W_A)
    ya_s, u_s, yb_s, ret_s = _even_sample(ps, st_exp, state_ret[0], cw, g_ret, cos_s, sin_s, lg_tab)
    x1, h1 = _outproj(ya, yb, w_out_e, xp, g_odd, 512, final=False)
    x1s, h1s = _outproj(ya_s, yb_s, w_out_e, xs, g_odd, 512, final=False)

    p2, p2s = _in_proj(h1, h1s, w_o, O_N, shift_from=O_GATE // IN_TN, shift=N_GATE)
    yc, yd, c_p, n_p, m_p = _odd_prompt(p2, h1, w_o, bg, gm, lng, lnb, ws, bst)
    n_rows = jnp.repeat(jnp.transpose(state_mlstm_n[0], (1, 0, 2)), DEC_SEQ, axis=1)
    m_rows = jnp.repeat(state_mlstm_m[0].T, DEC_SEQ, axis=1)[:, :, None]
    yc_s, yd_s, vn_s, c_s, no_s, mo_s = _odd_sample(
        p2s, h1s, w_o, state_mlstm_C[0], n_rows, m_rows, bg, gm, lng, lnb, rtab, btab)
    y_prompt = _outproj(yc, yd, w_out_o, x1, g_fin, 512, final=True)
    y_sample = _outproj(yc_s, yd_s, w_out_o, x1s, g_fin, 512, final=True)

    conv_s = u_s.reshape(DEC_BATCH, DEC_SEQ, W_A)[:, DEC_SEQ - (CONV_W - 1):, :]
    n_s = jnp.transpose(no_s[:, DEC_SEQ - 1::DEC_SEQ, :], (1, 0, 2))
    m_s = mo_s[:, DEC_SEQ - 1::DEC_SEQ, 0].T
    return (y_prompt.reshape(BATCH, SEQ, D_MODEL),
            y_sample.reshape(DEC_BATCH, DEC_SEQ, D_MODEL),
            conv_p[None], conv_s[None],
            ret_p[None], ret_s[None],
            c_p[None], c_s[None],
            n_p[None], n_s[None],
            m_p[:, :H_C, 0][None], m_s[None],
            vn_s.reshape(DEC_BATCH, DEC_SEQ, W_D)[None])
```

```python
import functools
import math

import jax
import jax.numpy as jnp
from jax import lax
from jax.experimental import pallas as pl
from jax.experimental.pallas import tpu as pltpu

F32 = jnp.float32
BF16 = jnp.bfloat16

D_MODEL = 2048
BATCH = 4
SEQ = 2048
DEC_BATCH = 128
DEC_SEQ = 4
PAST_LEN = 16384
W_A = 1024
CONV_W = 3
W_B = 1024
H_B = 8
DH_B = 128
E_IN = 8192
W_C = 1024
H_C = 4
DV_C = 256
DQK_C = 128
W_D = 1024
G_D = 8
CHUNK = 128
O_GATE = 2 * H_C * DQK_C + 2 * W_C
N_GATE = 2 * H_C
O_N = O_GATE + 3 * W_D
ROPE_BASE = 10000.0
EPS = 1e-6
LOG_GAMMA = tuple(math.log(1.0 - 2.0 ** (-5.0 - h)) for h in range(H_B))
NEG_INF = float("-inf")
VMEM_LIMIT = 56 * 1024 * 1024

NT_DIMS = (((1,), (1,)), ((), ()))
TN_DIMS = (((0,), (0,)), ((), ()))


def _silu(z):
    return z * (1.0 / (1.0 + jnp.exp(-z)))


def _log_sigmoid(x):
    return jnp.minimum(x, 0.0) - jnp.log1p(jnp.exp(-jnp.abs(x)))


def _dot(a, b):
    return jnp.dot(a, b, preferred_element_type=F32)


def _dot_nt(a, b):
    return lax.dot_general(a, b, NT_DIMS, preferred_element_type=F32)


def _dot_tn(a, b):
    return lax.dot_general(a, b, TN_DIMS, preferred_element_type=F32)


def _dot_hi(a, b):
    return jnp.dot(a, b, preferred_element_type=F32, precision=lax.Precision.HIGHEST)


def _head_norm(o, g):
    mu = jnp.mean(o, axis=-1, keepdims=True)
    oc = o - mu
    var = jnp.mean(oc * oc, axis=-1, keepdims=True)
    return oc * lax.rsqrt(var + EPS) * g


def _params(sem):
    return pltpu.CompilerParams(dimension_semantics=sem, vmem_limit_bytes=VMEM_LIMIT)


def _norm_cast_kernel(x_ref, g_ref, h_ref):
    x = x_ref[...]
    ms = jnp.mean(x * x, axis=-1, keepdims=True)
    h_ref[...] = (x * lax.rsqrt(ms + EPS) * g_ref[...]).astype(BF16)


def _norm_cast(x, g, tm):
    m, d = x.shape
    return pl.pallas_call(
        _norm_cast_kernel,
        grid=(m // tm,),
        in_specs=[pl.BlockSpec((tm, d), lambda i: (i, 0)),
                  pl.BlockSpec((1, d), lambda i: (0, 0))],
        out_specs=pl.BlockSpec((tm, d), lambda i: (i, 0)),
        out_shape=jax.ShapeDtypeStruct((m, d), BF16),
        compiler_params=_params(("arbitrary",)),
        name="norm_cast",
    )(x, g)


IN_TM = 1024
IN_TN = 1024


def _in_proj_kernel(*refs, shift_from, shift):
    if shift:
        hp_ref, hs_ref, w_ref, wn_ref, op_ref, os_ref, wb = refs
    else:
        hp_ref, hs_ref, w_ref, op_ref, os_ref, wb = refs
    j = pl.program_id(0)
    i = pl.program_id(1)

    if shift:
        @pl.when(jnp.logical_and(i == 0, j < shift_from))
        def _():
            wb[...] = w_ref[...].astype(BF16)

        @pl.when(jnp.logical_and(i == 0, j >= shift_from))
        def _():
            wb[...] = jnp.concatenate([w_ref[shift:IN_TN, :], wn_ref[...]], axis=0).astype(BF16)

        mm = _dot_nt
    else:
        @pl.when(i == 0)
        def _():
            wb[...] = w_ref[...].astype(BF16)

        mm = _dot

    @pl.when(i == 0)
    def _():
        os_ref[...] = mm(hs_ref[...], wb[...])

    @pl.when(i > 0)
    def _():
        op_ref[...] = mm(hp_ref[...], wb[...])


def _in_proj(hp, hs, w, n_out, shift_from=0, shift=0):
    mp, k = hp.shape
    ms = hs.shape[0]
    n_prompt = mp // IN_TM
    prow = lambda j, i: (jnp.maximum(i - 1, 0), 0)
    in_specs = [pl.BlockSpec((IN_TM, k), prow),
                pl.BlockSpec((ms, k), lambda j, i: (0, 0))]
    args = [hp, hs, w]
    if shift:
        in_specs.append(pl.BlockSpec((IN_TN, k), lambda j, i: (j, 0)))
        in_specs.append(pl.BlockSpec((shift, k), lambda j, i: ((j + 1) * (IN_TN // shift), 0)))
        args.append(w)
        wb_shape = (IN_TN, k)
    else:
        in_specs.append(pl.BlockSpec((k, IN_TN), lambda j, i: (0, j)))
        wb_shape = (k, IN_TN)
    return pl.pallas_call(
        functools.partial(_in_proj_kernel, shift_from=shift_from, shift=shift),
        grid=(n_out // IN_TN, n_prompt + 1),
        in_specs=in_specs,
        out_specs=(pl.BlockSpec((IN_TM, IN_TN), lambda j, i: (jnp.maximum(i - 1, 0), j)),
                   pl.BlockSpec((ms, IN_TN), lambda j, i: (0, j))),
        out_shape=(jax.ShapeDtypeStruct((mp, n_out), F32),
                   jax.ShapeDtypeStruct((ms, n_out), F32)),
        scratch_shapes=[pltpu.VMEM(wb_shape, BF16)],
        compiler_params=_params(("arbitrary", "arbitrary")),
        name="in_proj",
    )(*args)


def _outproj_kernel(ya_ref, yb_ref, w_ref, x_ref, g_ref, *out_refs, final):
    half = ya_ref.shape[1]
    acc = _dot(ya_ref[...], w_ref[0:half, :]) + _dot(yb_ref[...], w_ref[half:2 * half, :])
    x1 = x_ref[...] + acc
    ms = jnp.mean(x1 * x1, axis=-1, keepdims=True)
    hn = x1 * lax.rsqrt(ms + EPS) * g_ref[...]
    if final:
        out_refs[0][...] = hn
    else:
        out_refs[0][...] = x1
        out_refs[1][...] = hn.astype(BF16)


def _outproj(ya, yb, w, x, g, tm, final):
    m, half = ya.shape
    d = w.shape[1]
    row = lambda i: (i, 0)
    if final:
        out_shape = jax.ShapeDtypeStruct((m, d), F32)
        out_specs = pl.BlockSpec((tm, d), row)
    else:
        out_shape = (jax.ShapeDtypeStruct((m, d), F32), jax.ShapeDtypeStruct((m, d), BF16))
        out_specs = (pl.BlockSpec((tm, d), row), pl.BlockSpec((tm, d), row))
    return pl.pallas_call(
        functools.partial(_outproj_kernel, final=final),
        grid=(m // tm,),
        in_specs=[pl.BlockSpec((tm, half), row),
                  pl.BlockSpec((tm, half), row),
                  pl.BlockSpec((2 * half, d), lambda i: (0, 0)),
                  pl.BlockSpec((tm, d), row),
                  pl.BlockSpec((1, d), lambda i: (0, 0))],
        out_specs=out_specs,
        out_shape=out_shape,
        compiler_params=_params(("arbitrary",)),
        name="out_proj_final" if final else "out_proj",
    )(ya, yb, w, x, g)


MIX_ROWS = 512


def _rope(x, cosf, sins):
    return x * cosf + pltpu.roll(x, DH_B // 2, 1) * sins


def _even_prompt_kernel(p_ref, cw_ref, gret_ref, cos_ref, sin_ref,
                        ya_ref, yb_ref, conv_ref, s_ref, ubuf):
    L = CHUNK

    @pl.when(pl.program_id(1) == 0)
    def _():
        ubuf[0:8, :] = jnp.zeros((8, W_A), F32)
        s_ref[...] = jnp.zeros_like(s_ref)

    def chunk(ci, carry):
        rs = pl.ds(pl.multiple_of(ci * L, L), L)

        for j in range(W_A // 128):
            sl = slice(j * 128, (j + 1) * 128)
            a_b = p_ref[rs, j * 128:(j + 1) * 128]
            a_c = p_ref[rs, W_A + j * 128:W_A + (j + 1) * 128]
            a_x = p_ref[rs, 2 * W_A + j * 128:2 * W_A + (j + 1) * 128]
            a_z = p_ref[rs, 3 * W_A + j * 128:3 * W_A + (j + 1) * 128]
            u = a_c * a_x
            ubuf[8:8 + L, sl] = u
            t0 = ubuf[6:6 + L, sl]
            t1 = ubuf[7:7 + L, sl]
            conv = cw_ref[0:1, sl] * t0 + cw_ref[1:2, sl] * t1 + cw_ref[2:3, sl] * u
            ya_ref[rs, sl] = (a_b * conv * _silu(a_z)).astype(BF16)
            ubuf[0:8, sl] = u[L - 8:L, :]

        cosf = cos_ref[rs, :]
        sins = sin_ref[rs, :]
        row = lax.broadcasted_iota(jnp.int32, (L, L), 0)
        col = lax.broadcasted_iota(jnp.int32, (L, L), 1)
        causal = row >= col
        diff = jnp.maximum(row - col, 0).astype(F32)
        ti = lax.broadcasted_iota(jnp.int32, (L, 1), 0).astype(F32)
        base = 4 * W_A
        for h in range(H_B):
            lg = LOG_GAMMA[h]
            sl = slice(h * DH_B, (h + 1) * DH_B)
            q = p_ref[rs, base + h * DH_B:base + (h + 1) * DH_B]
            k = p_ref[rs, base + W_B + h * DH_B:base + W_B + (h + 1) * DH_B]
            v = p_ref[rs, base + 2 * W_B + h * DH_B:base + 2 * W_B + (h + 1) * DH_B]
            z = p_ref[rs, base + 3 * W_B + h * DH_B:base + 3 * W_B + (h + 1) * DH_B]
            qr = _rope(q, cosf, sins)
            kr = _rope(k, cosf, sins) * (DH_B ** -0.5)
            decay = jnp.where(causal, jnp.exp(lg * diff), 0.0)
            qb = qr.astype(BF16)
            kb = kr.astype(BF16)
            vb = v.astype(BF16)
            sc = _dot_nt(qb, kb) * decay
            inner = _dot(sc.astype(BF16), vb)
            s_old = s_ref[0, h]
            cross = _dot(qb, s_old.astype(BF16)) * jnp.exp(lg * (ti + 1.0))
            kd = (kr * jnp.exp(lg * (L - 1.0 - ti))).astype(BF16)
            s_ref[0, h] = math.exp(lg * L) * s_old + _dot_tn(kd, vb)
            o = inner + cross
            yb_ref[rs, sl] = (_head_norm(o, gret_ref[0:1, sl]) * _silu(z)).astype(BF16)
        return carry

    lax.fori_loop(0, MIX_ROWS // L, chunk, 0)
    conv_ref[0] = ubuf[6:8, :]


def _even_prompt(p, conv_w, g_ret, cosf, sins):
    nc = SEQ // MIX_ROWS
    rows = lambda b, c: (b * nc + c, 0)
    const2 = lambda b, c: (0, 0)
    return pl.pallas_call(
        _even_prompt_kernel,
        grid=(BATCH, nc),
        in_specs=[pl.BlockSpec((MIX_ROWS, E_IN), rows),
                  pl.BlockSpec((CONV_W, W_A), const2),
                  pl.BlockSpec((1, W_B), const2),
                  pl.BlockSpec((MIX_ROWS, DH_B), lambda b, c: (c, 0)),
                  pl.BlockSpec((MIX_ROWS, DH_B), lambda b, c: (c, 0))],
        out_specs=(pl.BlockSpec((MIX_ROWS, W_A), rows),
                   pl.BlockSpec((MIX_ROWS, W_B), rows),
                   pl.BlockSpec((1, CONV_W - 1, W_A), lambda b, c: (b, 0, 0)),
                   pl.BlockSpec((1, H_B, DH_B, DH_B), lambda b, c: (b, 0, 0, 0))),
        out_shape=(jax.ShapeDtypeStruct((BATCH * SEQ, W_A), BF16),
                   jax.ShapeDtypeStruct((BATCH * SEQ, W_B), BF16),
                   jax.ShapeDtypeStruct((BATCH, CONV_W - 1, W_A), F32),
                   jax.ShapeDtypeStruct((BATCH, H_B, DH_B, DH_B), F32)),
        scratch_shapes=[pltpu.VMEM((CHUNK + 8, W_A), F32)],
        compiler_params=_params(("arbitrary", "arbitrary")),
        name="even_prompt",
    )(p, conv_w, g_ret, cosf, sins)


SB = 32
SR = SB * DEC_SEQ


def _even_sample_kernel(pa_ref, pq_ref, pk_ref, pv_ref, pz_ref, st_ref, s_ref,
                        cw_ref, gret_ref, cos_ref, sin_ref, lg_ref,
                        ya_ref, u_ref, yb_ref, so_ref, cross_scr):
    h = pl.program_id(1)
    row = lax.broadcasted_iota(jnp.int32, (SR, SR), 0)
    col = lax.broadcasted_iota(jnp.int32, (SR, SR), 1)
    trow = row & 3

    @pl.when(h == 0)
    def _():
        for j in range(W_A // 128):
            sl = slice(j * 128, (j + 1) * 128)
            a_b = pa_ref[:, j * 128:(j + 1) * 128]
            a_c = pa_ref[:, W_A + j * 128:W_A + (j + 1) * 128]
            a_x = pa_ref[:, 2 * W_A + j * 128:2 * W_A + (j + 1) * 128]
            a_z = pa_ref[:, 3 * W_A + j * 128:3 * W_A + (j + 1) * 128]
            u = a_c * a_x
            e = st_ref[:, sl]
            tap1 = jnp.where(trow >= 1, pltpu.roll(u, 1, 0), pltpu.roll(e, SR - 1, 0))
            tap0 = jnp.where(trow >= 2, pltpu.roll(u, 2, 0), e)
            conv = cw_ref[0:1, sl] * tap0 + cw_ref[1:2, sl] * tap1 + cw_ref[2:3, sl] * u
            ya_ref[:, sl] = (a_b * conv * _silu(a_z)).astype(BF16)
            u_ref[:, sl] = u

    lg = lg_ref[0][:, 0:1]
    same = (row >> 2) == (col >> 2)
    dd = trow - (col & 3)
    mask = jnp.where(same, dd, -1) >= 0
    decay = jnp.where(mask, jnp.exp(lg * jnp.maximum(dd, 0).astype(F32)), 0.0)
    tcol = (lax.broadcasted_iota(jnp.int32, (SR, 1), 0) & 3).astype(F32)
    cosf = cos_ref[...]
    sins = sin_ref[...]
    qr = _rope(pq_ref[...], cosf, sins)
    kr = _rope(pk_ref[...], cosf, sins) * (DH_B ** -0.5)
    qb = qr.astype(BF16)
    kb = kr.astype(BF16)
    vb = pv_ref[...].astype(BF16)
    sc = _dot_nt(qb, kb) * decay
    inner = _dot(sc.astype(BF16), vb)
    kdt = (kr * jnp.exp(lg * (DEC_SEQ - 1.0 - tcol))).T
    gamma_l = jnp.exp(lg * float(DEC_SEQ))
    lane_b = col >> 2
    sub = lax.broadcasted_iota(jnp.int32, (8, DH_B), 0)
    for g in range(SR // 8):
        q8 = qb[8 * g:8 * g + 8, :]
        res = []
        for beta in range(2):
            b = 2 * g + beta
            s_old = s_ref[b, 0]
            res.append(_dot(q8, s_old.astype(BF16)))
            lhs = jnp.where(lane_b == b, kdt, 0.0).astype(BF16)
            so_ref[b, 0] = gamma_l * s_old + _dot(lhs, vb)
        cross_scr[8 * g:8 * g + 8, :] = jnp.where(sub < DEC_SEQ, res[0], res[1])
    o = inner + cross_scr[...] * jnp.exp(lg * (tcol + 1.0))
    yb_ref[...] = (_head_norm(o, gret_ref[...]) * _silu(pz_ref[...])).astype(BF16)


def _even_sample(p, st_exp, s_state, conv_w, g_ret, cosf, sins, lg_tab):
    nb = DEC_BATCH // SB
    qoff = 4 * W_A // DH_B
    hb = W_B // DH_B
    const2 = lambda i, h: (0, 0)
    return pl.pallas_call(
        _even_sample_kernel,
        grid=(nb, H_B),
        in_specs=[pl.BlockSpec((SR, 4 * W_A), lambda i, h: (i, 0)),
                  pl.BlockSpec((SR, DH_B), lambda i, h: (i, qoff + h)),
                  pl.BlockSpec((SR, DH_B), lambda i, h: (i, qoff + hb + h)),
                  pl.BlockSpec((SR, DH_B), lambda i, h: (i, qoff + 2 * hb + h)),
                  pl.BlockSpec((SR, DH_B), lambda i, h: (i, qoff + 3 * hb + h)),
                  pl.BlockSpec((SR, W_A), lambda i, h: (i, 0)),
                  pl.BlockSpec((SB, 1, DH_B, DH_B), lambda i, h: (i, h, 0, 0)),
                  pl.BlockSpec((CONV_W, W_A), const2),
                  pl.BlockSpec((1, DH_B), lambda i, h: (0, h)),
                  pl.BlockSpec((SR, DH_B), const2),
                  pl.BlockSpec((SR, DH_B), const2),
                  pl.BlockSpec((1, 1, 128), lambda i, h: (h, 0, 0))],
        out_specs=(pl.BlockSpec((SR, W_A), lambda i, h: (i, 0)),
                   pl.BlockSpec((SR, W_A), lambda i, h: (i, 0)),
                   pl.BlockSpec((SR, DH_B), lambda i, h: (i, h)),
                   pl.BlockSpec((SB, 1, DH_B, DH_B), lambda i, h: (i, h, 0, 0))),
        out_shape=(jax.ShapeDtypeStruct((DEC_BATCH * DEC_SEQ, W_A), BF16),
                   jax.ShapeDtypeStruct((DEC_BATCH * DEC_SEQ, W_A), F32),
                   jax.ShapeDtypeStruct((DEC_BATCH * DEC_SEQ, W_B), BF16),
                   jax.ShapeDtypeStruct((DEC_BATCH, H_B, DH_B, DH_B), F32)),
        scratch_shapes=[pltpu.VMEM((SR, DH_B), F32)],
        compiler_params=_params(("arbitrary", "arbitrary")),
        name="even_sample",
    )(p, p, p, p, p, st_exp, s_state, conv_w, g_ret, cosf, sins, lg_tab)


def _odd_prompt_kernel(p_ref, h_ref, wg_ref, bg_ref, gm_ref, lng_ref, lnb_ref, ws_ref, bst_ref,
                       yc_ref, yd_ref, c_ref, n_ref, m_ref, wgb, wsb):
    L = CHUNK

    @pl.when(jnp.logical_and(pl.program_id(0) == 0, pl.program_id(1) == 0))
    def _():
        wgb[...] = wg_ref[...].astype(BF16)
        keep = (lax.broadcasted_iota(jnp.int32, (L, L), 0) >= lax.broadcasted_iota(jnp.int32, (L, L), 1))
        for g in range(G_D):
            wsb[g] = jnp.where(keep, ws_ref[g], 0.0).astype(BF16)

    @pl.when(pl.program_id(1) == 0)
    def _():
        c_ref[...] = jnp.zeros_like(c_ref)
        n_ref[...] = jnp.zeros_like(n_ref)
        m_ref[...] = jnp.zeros_like(m_ref)

    def chunk(ci, carry):
        rs = pl.ds(pl.multiple_of(ci * L, L), L)
        row = lax.broadcasted_iota(jnp.int32, (L, L), 0)
        col = lax.broadcasted_iota(jnp.int32, (L, L), 1)
        tri = row >= col

        pre = _dot_nt(h_ref[rs, :], wgb[...]) + bg_ref[...]
        lf = _log_sigmoid(pre)
        b_c = _dot_hi(jnp.where(tri, 1.0, 0.0), lf)
        b_r = b_c.T
        pre_r = pre.T
        for h in range(H_C):
            bc = b_c[:, H_C + h:H_C + h + 1]
            br = b_r[H_C + h:H_C + h + 1, :]
            igr = pre_r[h:h + 1, :]
            igc = pre[:, h:h + 1]
            m_prev = m_ref[0, h:h + 1, 0:1]
            log_d = jnp.where(tri, bc - br + igr, NEG_INF)
            log_inter = bc + m_prev
            m_t = jnp.maximum(log_inter, jnp.max(log_d, axis=-1, keepdims=True))
            w = jnp.exp(log_d - m_t)
            w_inter = jnp.exp(log_inter - m_t)
            q = p_ref[rs, h * DQK_C:(h + 1) * DQK_C] * (DQK_C ** -0.5)
            k = p_ref[rs, H_C * DQK_C + h * DQK_C:H_C * DQK_C + (h + 1) * DQK_C]
            v = p_ref[rs, 2 * H_C * DQK_C + h * DV_C:2 * H_C * DQK_C + (h + 1) * DV_C]
            z = p_ref[rs, O_GATE - W_C + h * DV_C:O_GATE - W_C + (h + 1) * DV_C]
            qb = q.astype(BF16)
            kb = k.astype(BF16)
            vb = v.astype(BF16)
            sc = _dot_nt(qb, kb) * w
            c_old = c_ref[0, h]
            n_old = n_ref[0, h:h + 1, :]
            num = _dot(sc.astype(BF16), vb) + w_inter * _dot_nt(qb, c_old.astype(BF16))
            den = jnp.sum(sc, axis=-1, keepdims=True) + w_inter * jnp.sum(q * n_old, axis=-1, keepdims=True)
            hh = num / jnp.maximum(jnp.abs(den), jnp.exp(-m_t))
            m_new = m_t[L - 1:L, :]
            b_last = bc[L - 1:L, :]
            w_end = jnp.exp(b_last - bc + igc - m_new)
            cd = jnp.exp(b_last + m_prev - m_new)
            c_ref[0, h] = cd * c_old + _dot_tn((v * w_end).astype(BF16), kb)
            n_ref[0, h:h + 1, :] = cd * n_old + jnp.sum(w_end * k, axis=0, keepdims=True)
            m_ref[0, h:h + 1, :] = jnp.broadcast_to(m_new, (1, 128))
            sl = slice(h * DV_C, (h + 1) * DV_C)
            yc_ref[rs, sl] = (_head_norm(hh, gm_ref[0:1, sl]) * _silu(z)).astype(BF16)

        dv = lambda g: p_ref[rs, O_GATE + W_D + g * 128:O_GATE + W_D + (g + 1) * 128]
        tot = dv(0)
        for g in range(1, G_D):
            tot = tot + dv(g)
        mu = jnp.sum(tot, axis=-1, keepdims=True) * (1.0 / W_D)
        sq = (dv(0) - mu) * (dv(0) - mu)
        for g in range(1, G_D):
            sq = sq + (dv(g) - mu) * (dv(g) - mu)
        rstd = lax.rsqrt(jnp.sum(sq, axis=-1, keepdims=True) * (1.0 / W_D) + EPS)
        for g in range(G_D):
            sl = slice(g * 128, (g + 1) * 128)
            vn = (dv(g) - mu) * rstd * lng_ref[0:1, sl] + lnb_ref[0:1, sl]
            s = _dot(wsb[g], vn.astype(BF16)) + bst_ref[:, g:g + 1]
            d_u = p_ref[rs, O_GATE + g * 128:O_GATE + (g + 1) * 128]
            d_z = p_ref[rs, O_GATE + 2 * W_D + g * 128:O_GATE + 2 * W_D + (g + 1) * 128]
            yd_ref[rs, sl] = (d_u * s * _silu(d_z)).astype(BF16)
        return carry

    lax.fori_loop(0, MIX_ROWS // L, chunk, 0)


def _odd_prompt(p, h, w_o, bg, gm, lng, lnb, ws, bst):
    nc = SEQ // MIX_ROWS
    rows = lambda b, c: (b * nc + c, 0)
    const2 = lambda b, c: (0, 0)
    return pl.pallas_call(
        _odd_prompt_kernel,
        grid=(BATCH, nc),
        in_specs=[pl.BlockSpec((MIX_ROWS, O_N), rows),
                  pl.BlockSpec((MIX_ROWS, D_MODEL), rows),
                  pl.BlockSpec((128, D_MODEL), lambda b, c: (O_GATE // 128, 0)),
                  pl.BlockSpec((1, 128), const2),
                  pl.BlockSpec((1, W_C), const2),
                  pl.BlockSpec((1, W_D), const2),
                  pl.BlockSpec((1, W_D), const2),
                  pl.BlockSpec((G_D, CHUNK, CHUNK), lambda b, c: (0, 0, 0)),
                  pl.BlockSpec((CHUNK, G_D), const2)],
        out_specs=(pl.BlockSpec((MIX_ROWS, W_C), rows),
                   pl.BlockSpec((MIX_ROWS, W_D), rows),
                   pl.BlockSpec((1, H_C, DV_C, DQK_C), lambda b, c: (b, 0, 0, 0)),
                   pl.BlockSpec((1, H_C, DQK_C), lambda b, c: (b, 0, 0)),
                   pl.BlockSpec((1, 8, 128), lambda b, c: (b, 0, 0))),
        out_shape=(jax.ShapeDtypeStruct((BATCH * SEQ, W_C), BF16),
                   jax.ShapeDtypeStruct((BATCH * SEQ, W_D), BF16),
                   jax.ShapeDtypeStruct((BATCH, H_C, DV_C, DQK_C), F32),
                   jax.ShapeDtypeStruct((BATCH, H_C, DQK_C), F32),
                   jax.ShapeDtypeStruct((BATCH, 8, 128), F32)),
        scratch_shapes=[pltpu.VMEM((128, D_MODEL), BF16),
                        pltpu.VMEM((G_D, CHUNK, CHUNK), BF16)],
        compiler_params=_params(("arbitrary", "arbitrary")),
        name="odd_prompt",
    )(p, h, w_o, bg, gm, lng, lnb, ws, bst)


def _odd_sample_kernel(pq_ref, pk_ref, pv_ref, pz_ref, h_ref, wg_ref, pd_ref,
                       c_ref, nrow_ref, mrow_ref, bg_ref, gm_ref, lng_ref, lnb_ref,
                       rtab_ref, btab_ref,
                       yc_ref, yd_ref, vn_ref, co_ref, no_ref, mo_ref,
                       inter_scr):
    h = pl.program_id(1)
    row = lax.broadcasted_iota(jnp.int32, (SR, SR), 0)
    col = lax.broadcasted_iota(jnp.int32, (SR, SR), 1)
    trow = row & 3

    @pl.when(h == 0)
    def _():
        dv = pd_ref[:, W_D:2 * W_D]
        mu = jnp.mean(dv, axis=-1, keepdims=True)
        xc = dv - mu
        var = jnp.mean(xc * xc, axis=-1, keepdims=True)
        rstd = lax.rsqrt(var + EPS)
        for g in range(G_D):
            sl = slice(g * 128, (g + 1) * 128)
            vn = xc[:, sl] * rstd * lng_ref[0:1, sl] + lnb_ref[0:1, sl]
            vn_ref[:, sl] = vn
            s = rtab_ref[0, :, sl] * vn + btab_ref[:, sl]
            for j in range(1, DEC_SEQ):
                s = s + jnp.where(trow >= j, rtab_ref[j, :, sl] * pltpu.roll(vn, j, 0), 0.0)
            d_u = pd_ref[:, g * 128:(g + 1) * 128]
            d_z = pd_ref[:, 2 * W_D + g * 128:2 * W_D + (g + 1) * 128]
            yd_ref[:, sl] = (d_u * s * _silu(d_z)).astype(BF16)

    same = (row >> 2) == (col >> 2)
    mask = jnp.where(same, trow - (col & 3), -1) >= 0
    pre = _dot_nt(h_ref[...], wg_ref[...].astype(BF16)) + bg_ref[...]
    lf = _log_sigmoid(pre)
    b_full = _dot_hi(jnp.where(mask, 1.0, 0.0), lf)
    sel_i = col == h
    sel_f = col == h + H_C
    ig_c = jnp.sum(jnp.where(sel_i, pre, 0.0), axis=-1, keepdims=True)
    b_c = jnp.sum(jnp.where(sel_f, b_full, 0.0), axis=-1, keepdims=True)
    sel_ir = row == h
    sel_fr = row == h + H_C
    ig_r = jnp.sum(jnp.where(sel_ir, pre.T, 0.0), axis=0, keepdims=True)
    b_r = jnp.sum(jnp.where(sel_fr, b_full.T, 0.0), axis=0, keepdims=True)
    m_prev = mrow_ref[0]
    log_d = jnp.where(mask, b_c - b_r + ig_r, NEG_INF)
    log_inter = b_c + m_prev
    m_t = jnp.maximum(log_inter, jnp.max(log_d, axis=-1, keepdims=True))
    w = jnp.exp(log_d - m_t)
    w_inter = jnp.exp(log_inter - m_t)
    q = pq_ref[...] * (DQK_C ** -0.5)
    k = pk_ref[...]
    v = pv_ref[...]
    qb = q.astype(BF16)
    kb = k.astype(BF16)
    vb = v.astype(BF16)
    sc = _dot_nt(qb, kb) * w
    sub = lax.broadcasted_iota(jnp.int32, (8, DV_C), 0)
    for g in range(SR // 8):
        q8 = qb[8 * g:8 * g + 8, :]
        r0 = _dot_nt(q8, c_ref[2 * g, 0].astype(BF16))
        r1 = _dot_nt(q8, c_ref[2 * g + 1, 0].astype(BF16))
        inter_scr[8 * g:8 * g + 8, :] = jnp.where(sub < DEC_SEQ, r0, r1)
    n_rows = nrow_ref[0]
    num = _dot(sc.astype(BF16), vb) + w_inter * inter_scr[...]
    den = jnp.sum(sc, axis=-1, keepdims=True) + w_inter * jnp.sum(q * n_rows, axis=-1, keepdims=True)
    hh = num / jnp.maximum(jnp.abs(den), jnp.exp(-m_t))
    yc_ref[...] = (_head_norm(hh, gm_ref[...]) * _silu(pz_ref[...])).astype(BF16)

    stats = jnp.where(col == 0, m_t, jnp.where(col == 1, b_c, 0.0))
    last = _dot_hi(jnp.where(col == (row | 3), 1.0, 0.0), stats)
    m_new = last[:, 0:1]
    b_last = last[:, 1:2]
    w_end = jnp.exp(b_last - b_c + ig_c - m_new)
    cd = jnp.exp(b_last + m_prev - m_new)
    mo_ref[0] = m_new
    no_ref[0] = cd * n_rows + _dot_hi(jnp.where(same, 1.0, 0.0), w_end * k)
    vwt = (v * w_end).T
    lane_b = lax.broadcasted_iota(jnp.int32, (DV_C, SR), 1) >> 2
    for b in range(SB):
        lhs = jnp.where(lane_b == b, vwt, 0.0).astype(BF16)
        cd_b = cd[4 * b + 3:4 * b + 4, :]
        co_ref[b, 0] = cd_b * c_ref[b, 0] + _dot(lhs, kb)


def _odd_sample(p, h, w_o, c_state, n_rows, m_rows, bg, gm, lng, lnb, rtab, btab):
    nb = DEC_BATCH // SB
    const2 = lambda i, h: (0, 0)
    koff = H_C * DQK_C // DQK_C
    voff = 2 * H_C * DQK_C // DV_C
    zoff = (2 * H_C * DQK_C + W_C) // DV_C
    return pl.pallas_call(
        _odd_sample_kernel,
        grid=(nb, H_C),
        in_specs=[pl.BlockSpec((SR, DQK_C), lambda i, h: (i, h)),
                  pl.BlockSpec((SR, DQK_C), lambda i, h: (i, koff + h)),
                  pl.BlockSpec((SR, DV_C), lambda i, h: (i, voff + h)),
                  pl.BlockSpec((SR, DV_C), lambda i, h: (i, zoff + h)),
                  pl.BlockSpec((SR, D_MODEL), lambda i, h: (i, 0)),
                  pl.BlockSpec((128, D_MODEL), lambda i, h: (O_GATE // 128, 0)),
                  pl.BlockSpec((SR, 3 * W_D), lambda i, h: (i, 1)),
                  pl.BlockSpec((SB, 1, DV_C, DQK_C), lambda i, h: (i, h, 0, 0)),
                  pl.BlockSpec((1, SR, DQK_C), lambda i, h: (h, i, 0)),
                  pl.BlockSpec((1, SR, 1), lambda i, h: (h, i, 0)),
                  pl.BlockSpec((1, 128), const2),
                  pl.BlockSpec((1, DV_C), lambda i, h: (0, h)),
                  pl.BlockSpec((1, W_D), const2),
                  pl.BlockSpec((1, W_D), const2),
                  pl.BlockSpec((DEC_SEQ, SR, W_D), lambda i, h: (0, 0, 0)),
                  pl.BlockSpec((SR, W_D), const2)],
        out_specs=(pl.BlockSpec((SR, DV_C), lambda i, h: (i, h)),
                   pl.BlockSpec((SR, W_D), lambda i, h: (i, 0)),
                   pl.BlockSpec((SR, W_D), lambda i, h: (i, 0)),
                   pl.BlockSpec((SB, 1, DV_C, DQK_C), lambda i, h: (i, h, 0, 0)),
                   pl.BlockSpec((1, SR, DQK_C), lambda i, h: (h, i, 0)),
                   pl.BlockSpec((1, SR, 1), lambda i, h: (h, i, 0))),
        out_shape=(jax.ShapeDtypeStruct((DEC_BATCH * DEC_SEQ, W_C), BF16),
                   jax.ShapeDtypeStruct((DEC_BATCH * DEC_SEQ, W_D), BF16),
                   jax.ShapeDtypeStruct((DEC_BATCH * DEC_SEQ, W_D), F32),
                   jax.ShapeDtypeStruct((DEC_BATCH, H_C, DV_C, DQK_C), F32),
                   jax.ShapeDtypeStruct((H_C, DEC_BATCH * DEC_SEQ, DQK_C), F32),
                   jax.ShapeDtypeStruct((H_C, DEC_BATCH * DEC_SEQ, 1), F32)),
        scratch_shapes=[pltpu.VMEM((SR, DV_C), F32)],
        compiler_params=_params(("arbitrary", "arbitrary")),
        name="odd_sample",
    )(p, p, p, p, h, w_o, p, c_state, n_rows, m_rows, bg, gm, lng, lnb, rtab, btab)


def _rope_tables(pos):
    inv = ROPE_BASE ** (-jnp.arange(0, DH_B, 2, dtype=F32) / DH_B)
    ang = pos.astype(F32)[:, None] * inv[None, :]
    cos = jnp.cos(ang)
    sin = jnp.sin(ang)
    return jnp.concatenate([cos, cos], axis=-1), jnp.concatenate([-sin, sin], axis=-1)


def kernel(x_prompt, x_sample, state_conv, state_ret, state_mlstm_C, state_mlstm_n, state_mlstm_m,
           norm_even, w_in_even, conv_w, ret_norm, w_out_even,
           norm_odd, w_in_odd, b_gate_odd, mlstm_norm, ln_v_g, ln_v_b,
           w_spatial, b_spatial, w_out_odd, norm_final):
    w_in_e = w_in_even[0]
    w_out_e = w_out_even[0].astype(BF16)
    w_o = w_in_odd[0].T
    w_out_o = w_out_odd[0].astype(BF16)
    g_even = norm_even[0][None, :]
    g_odd = norm_odd[0][None, :]
    g_fin = norm_final[None, :]
    cw = conv_w[0]
    g_ret = ret_norm[0][None, :]
    bg = jnp.concatenate([b_gate_odd[0], jnp.zeros((128 - 2 * H_C,), F32)])[None, :]
    gm = mlstm_norm[0][None, :]
    lng = ln_v_g[0][None, :]
    lnb = ln_v_b[0][None, :]
    ws = w_spatial[0]
    bst = b_spatial[0].T

    cos_p, sin_p = _rope_tables(jnp.arange(SEQ, dtype=jnp.int32))
    cos_s, sin_s = _rope_tables(PAST_LEN + jnp.arange(DEC_SEQ, dtype=jnp.int32))
    cos_s = jnp.tile(cos_s, (SB, 1))
    sin_s = jnp.tile(sin_s, (SB, 1))
    lg_tab = jnp.broadcast_to(jnp.asarray(LOG_GAMMA, F32)[:, None, None], (H_B, 1, 128))

    ws4 = ws[:, :DEC_SEQ, :DEC_SEQ]
    t_idx = jnp.arange(DEC_SEQ)
    rtab = []
    for j in range(DEC_SEQ):
        coef = ws4[:, t_idx, (t_idx - j) % DEC_SEQ]
        tab = jnp.repeat(coef.T[:, :, None], 128, axis=2).reshape(DEC_SEQ, W_D)
        rtab.append(jnp.tile(tab, (SB, 1)))
    rtab = jnp.stack(rtab)
    btab = jnp.tile(jnp.repeat(b_spatial[0][:, :DEC_SEQ].T[:, :, None], 128, axis=2)
                    .reshape(DEC_SEQ, W_D), (SB, 1))

    rs = DEC_BATCH * DEC_SEQ
    xp = x_prompt.reshape(BATCH * SEQ, D_MODEL)
    xs = x_sample.reshape(rs, D_MODEL)
    hp = _norm_cast(xp, g_even, 512)
    hs = _norm_cast(xs, g_even, 512)
    pp, ps = _in_proj(hp, hs, w_in_e, E_IN)
    ya, yb, conv_p, ret_p = _even_prompt(pp, cw, g_ret, cos_p, sin_p)
    st_exp = jnp.pad(state_conv[0], ((0, 0), (0, DEC_SEQ - (CONV_W - 1)), (0, 0))).reshape(rs, W_A)
    ya_s, u_s, yb_s, ret_s = _even_sample(ps, st_exp, state_ret[0], cw, g_ret, cos_s, sin_s, lg_tab)
    x1, h1 = _outproj(ya, yb, w_out_e, xp, g_odd, 512, final=False)
    x1s, h1s = _outproj(ya_s, yb_s, w_out_e, xs, g_odd, 512, final=False)

    p2, p2s = _in_proj(h1, h1s, w_o, O_N, shift_from=O_GATE // IN_TN, shift=N_GATE)
    yc, yd, c_p, n_p, m_p = _odd_prompt(p2, h1, w_o, bg, gm, lng, lnb, ws, bst)
    n_rows = jnp.repeat(jnp.transpose(state_mlstm_n[0], (1, 0, 2)), DEC_SEQ, axis=1)
    m_rows = jnp.repeat(state_mlstm_m[0].T, DEC_SEQ, axis=1)[:, :, None]
    yc_s, yd_s, vn_s, c_s, no_s, mo_s = _odd_sample(
        p2s, h1s, w_o, state_mlstm_C[0], n_rows, m_rows, bg, gm, lng, lnb, rtab, btab)
    y_prompt = _outproj(yc, yd, w_out_o, x1, g_fin, 512, final=True)
    y_sample = _outproj(yc_s, yd_s, w_out_o, x1s, g_fin, 512, final=True)

    conv_s = u_s.reshape(DEC_BATCH, DEC_SEQ, W_A)[:, DEC_SEQ - (CONV_W - 1):, :]
    n_s = jnp.transpose(no_s[:, DEC_SEQ - 1::DEC_SEQ, :], (1, 0, 2))
    m_s = mo_s[:, DEC_SEQ - 1::DEC_SEQ, 0].T
    return (y_prompt.reshape(BATCH, SEQ, D_MODEL),
            y_sample.reshape(DEC_BATCH, DEC_SEQ, D_MODEL),
            conv_p[None], conv_s[None],
            ret_p[None], ret_s[None],
            c_p[None], c_s[None],
            n_p[None], n_s[None],
            m_p[:, :H_C, 0][None], m_s[None],
            vn_s.reshape(DEC_BATCH, DEC_SEQ, W_D)[None])
```

```python
import functools
import math

import jax
import jax.numpy as jnp
from jax import lax
from jax.experimental import pallas as pl
from jax.experimental.pallas import tpu as pltpu

F32 = jnp.float32
BF16 = jnp.bfloat16

D_MODEL = 2048
BATCH = 4
SEQ = 2048
DEC_BATCH = 128
DEC_SEQ = 4
PAST_LEN = 16384
W_A = 1024
CONV_W = 3
W_B = 1024
H_B = 8
DH_B = 128
E_IN = 8192
W_C = 1024
H_C = 4
DV_C = 256
DQK_C = 128
W_D = 1024
G_D = 8
CHUNK = 128
O_GATE = 2 * H_C * DQK_C + 2 * W_C
N_GATE = 2 * H_C
O_N = O_GATE + 3 * W_D
ROPE_BASE = 10000.0
EPS = 1e-6
LOG_GAMMA = tuple(math.log(1.0 - 2.0 ** (-5.0 - h)) for h in range(H_B))
NEG_INF = float("-inf")
VMEM_LIMIT = 56 * 1024 * 1024

NT_DIMS = (((1,), (1,)), ((), ()))
TN_DIMS = (((0,), (0,)), ((), ()))


def _silu(z):
    return z * (1.0 / (1.0 + jnp.exp(-z)))


def _log_sigmoid(x):
    return jnp.minimum(x, 0.0) - jnp.log1p(jnp.exp(-jnp.abs(x)))


def _dot(a, b):
    return jnp.dot(a, b, preferred_element_type=F32)


def _dot_nt(a, b):
    return lax.dot_general(a, b, NT_DIMS, preferred_element_type=F32)


def _dot_tn(a, b):
    return lax.dot_general(a, b, TN_DIMS, preferred_element_type=F32)


def _dot_hi(a, b):
    return jnp.dot(a, b, preferred_element_type=F32, precision=lax.Precision.HIGHEST)


def _head_norm(o, g):
    mu = jnp.mean(o, axis=-1, keepdims=True)
    oc = o - mu
    var = jnp.mean(oc * oc, axis=-1, keepdims=True)
    return oc * lax.rsqrt(var + EPS) * g


def _params(sem):
    return pltpu.CompilerParams(dimension_semantics=sem, vmem_limit_bytes=VMEM_LIMIT)


def _norm_cast_kernel(x_ref, g_ref, h_ref):
    x = x_ref[...]
    ms = jnp.mean(x * x, axis=-1, keepdims=True)
    h_ref[...] = (x * lax.rsqrt(ms + EPS) * g_ref[...]).astype(BF16)


def _norm_cast(x, g, tm):
    m, d = x.shape
    return pl.pallas_call(
        _norm_cast_kernel,
        grid=(m // tm,),
        in_specs=[pl.BlockSpec((tm, d), lambda i: (i, 0)),
                  pl.BlockSpec((1, d), lambda i: (0, 0))],
        out_specs=pl.BlockSpec((tm, d), lambda i: (i, 0)),
        out_shape=jax.ShapeDtypeStruct((m, d), BF16),
        compiler_params=_params(("arbitrary",)),
        name="norm_cast",
    )(x, g)


IN_TM = 1024
IN_TN = 1024


def _in_proj_kernel(*refs, shift_from, shift):
    if shift:
        hp_ref, hs_ref, w_ref, wn_ref, op_ref, os_ref, wb = refs
    else:
        hp_ref, hs_ref, w_ref, op_ref, os_ref, wb = refs
    j = pl.program_id(0)
    i = pl.program_id(1)

    if shift:
        @pl.when(jnp.logical_and(i == 0, j < shift_from))
        def _():
            wb[...] = w_ref[...].astype(BF16)

        @pl.when(jnp.logical_and(i == 0, j >= shift_from))
        def _():
            wb[...] = jnp.concatenate([w_ref[shift:IN_TN, :], wn_ref[...]], axis=0).astype(BF16)

        mm = _dot_nt
    else:
        @pl.when(i == 0)
        def _():
            wb[...] = w_ref[...].astype(BF16)

        mm = _dot

    @pl.when(i == 0)
    def _():
        os_ref[...] = mm(hs_ref[...], wb[...])

    @pl.when(i > 0)
    def _():
        op_ref[...] = mm(hp_ref[...], wb[...])


def _in_proj(hp, hs, w, n_out, shift_from=0, shift=0):
    mp, k = hp.shape
    ms = hs.shape[0]
    n_prompt = mp // IN_TM
    prow = lambda j, i: (jnp.maximum(i - 1, 0), 0)
    in_specs = [pl.BlockSpec((IN_TM, k), prow),
                pl.BlockSpec((ms, k), lambda j, i: (0, 0))]
    args = [hp, hs, w]
    if shift:
        in_specs.append(pl.BlockSpec((IN_TN, k), lambda j, i: (j, 0)))
        in_specs.append(pl.BlockSpec((shift, k), lambda j, i: ((j + 1) * (IN_TN // shift), 0)))
        args.append(w)
        wb_shape = (IN_TN, k)
    else:
        in_specs.append(pl.BlockSpec((k, IN_TN), lambda j, i: (0, j)))
        wb_shape = (k, IN_TN)
    return pl.pallas_call(
        functools.partial(_in_proj_kernel, shift_from=shift_from, shift=shift),
        grid=(n_out // IN_TN, n_prompt + 1),
        in_specs=in_specs,
        out_specs=(pl.BlockSpec((IN_TM, IN_TN), lambda j, i: (jnp.maximum(i - 1, 0), j)),
                   pl.BlockSpec((ms, IN_TN), lambda j, i: (0, j))),
        out_shape=(jax.ShapeDtypeStruct((mp, n_out), F32),
                   jax.ShapeDtypeStruct((ms, n_out), F32)),
        scratch_shapes=[pltpu.VMEM(wb_shape, BF16)],
        compiler_params=_params(("arbitrary", "arbitrary")),
        name="in_proj",
    )(*args)


def _outproj_kernel(ya_ref, yb_ref, w_ref, x_ref, g_ref, *out_refs, final):
    half = ya_ref.shape[1]
    acc = _dot(ya_ref[...], w_ref[0:half, :]) + _dot(yb_ref[...], w_ref[half:2 * half, :])
    x1 = x_ref[...] + acc
    ms = jnp.mean(x1 * x1, axis=-1, keepdims=True)
    hn = x1 * lax.rsqrt(ms + EPS) * g_ref[...]
    if final:
        out_refs[0][...] = hn
    else:
        out_refs[0][...] = x1
        out_refs[1][...] = hn.astype(BF16)


def _outproj(ya, yb, w, x, g, tm, final):
    m, half = ya.shape
    d = w.shape[1]
    row = lambda i: (i, 0)
    if final:
        out_shape = jax.ShapeDtypeStruct((m, d), F32)
        out_specs = pl.BlockSpec((tm, d), row)
    else:
        out_shape = (jax.ShapeDtypeStruct((m, d), F32), jax.ShapeDtypeStruct((m, d), BF16))
        out_specs = (pl.BlockSpec((tm, d), row), pl.BlockSpec((tm, d), row))
    return pl.pallas_call(
        functools.partial(_outproj_kernel, final=final),
        grid=(m // tm,),
        in_specs=[pl.BlockSpec((tm, half), row),
                  pl.BlockSpec((tm, half), row),
                  pl.BlockSpec((2 * half, d), lambda i: (0, 0)),
                  pl.BlockSpec((tm, d), row),
                  pl.BlockSpec((1, d), lambda i: (0, 0))],
        out_specs=out_specs,
        out_shape=out_shape,
        compiler_params=_params(("arbitrary",)),
        name="out_proj_final" if final else "out_proj",
    )(ya, yb, w, x, g)


MIX_ROWS = 512


def _rope(x, cosf, sins):
    return x * cosf + pltpu.roll(x, DH_B // 2, 1) * sins


def _even_prompt_kernel(p_ref, cw_ref, gret_ref, cos_ref, sin_ref,
                        ya_ref, yb_ref, conv_ref, s_ref, ubuf):
    L = CHUNK

    @pl.when(pl.program_id(1) == 0)
    def _():
        ubuf[0:8, :] = jnp.zeros((8, W_A), F32)
        s_ref[...] = jnp.zeros_like(s_ref)

    def chunk(ci, carry):
        rs = pl.ds(pl.multiple_of(ci * L, L), L)

        for j in range(W_A // 128):
            sl = slice(j * 128, (j + 1) * 128)
            a_b = p_ref[rs, j * 128:(j + 1) * 128]
            a_c = p_ref[rs, W_A + j * 128:W_A + (j + 1) * 128]
            a_x = p_ref[rs, 2 * W_A + j * 128:2 * W_A + (j + 1) * 128]
            a_z = p_ref[rs, 3 * W_A + j * 128:3 * W_A + (j + 1) * 128]
            u = a_c * a_x
            ubuf[8:8 + L, sl] = u
            t0 = ubuf[6:6 + L, sl]
            t1 = ubuf[7:7 + L, sl]
            conv = cw_ref[0:1, sl] * t0 + cw_ref[1:2, sl] * t1 + cw_ref[2:3, sl] * u
            ya_ref[rs, sl] = (a_b * conv * _silu(a_z)).astype(BF16)
            ubuf[0:8, sl] = u[L - 8:L, :]

        cosf = cos_ref[rs, :]
        sins = sin_ref[rs, :]
        row = lax.broadcasted_iota(jnp.int32, (L, L), 0)
        col = lax.broadcasted_iota(jnp.int32, (L, L), 1)
        causal = row >= col
        diff = jnp.maximum(row - col, 0).astype(F32)
        ti = lax.broadcasted_iota(jnp.int32, (L, 1), 0).astype(F32)
        base = 4 * W_A
        for h in range(H_B):
            lg = LOG_GAMMA[h]
            sl = slice(h * DH_B, (h + 1) * DH_B)
            q = p_ref[rs, base + h * DH_B:base + (h + 1) * DH_B]
            k = p_ref[rs, base + W_B + h * DH_B:base + W_B + (h + 1) * DH_B]
            v = p_ref[rs, base + 2 * W_B + h * DH_B:base + 2 * W_B + (h + 1) * DH_B]
            z = p_ref[rs, base + 3 * W_B + h * DH_B:base + 3 * W_B + (h + 1) * DH_B]
            qr = _rope(q, cosf, sins)
            kr = _rope(k, cosf, sins) * (DH_B ** -0.5)
            decay = jnp.where(causal, jnp.exp(lg * diff), 0.0)
            qb = qr.astype(BF16)
            kb = kr.astype(BF16)
            vb = v.astype(BF16)
            sc = _dot_nt(qb, kb) * decay
            inner = _dot(sc.astype(BF16), vb)
            s_old = s_ref[0, h]
            cross = _dot(qb, s_old.astype(BF16)) * jnp.exp(lg * (ti + 1.0))
            kd = (kr * jnp.exp(lg * (L - 1.0 - ti))).astype(BF16)
            s_ref[0, h] = math.exp(lg * L) * s_old + _dot_tn(kd, vb)
            o = inner + cross
            yb_ref[rs, sl] = (_head_norm(o, gret_ref[0:1, sl]) * _silu(z)).astype(BF16)
        return carry

    lax.fori_loop(0, MIX_ROWS // L, chunk, 0)
    conv_ref[0] = ubuf[6:8, :]


def _even_prompt(p, conv_w, g_ret, cosf, sins):
    nc = SEQ // MIX_ROWS
    rows = lambda b, c: (b * nc + c, 0)
    const2 = lambda b, c: (0, 0)
    return pl.pallas_call(
        _even_prompt_kernel,
        grid=(BATCH, nc),
        in_specs=[pl.BlockSpec((MIX_ROWS, E_IN), rows),
                  pl.BlockSpec((CONV_W, W_A), const2),
                  pl.BlockSpec((1, W_B), const2),
                  pl.BlockSpec((MIX_ROWS, DH_B), lambda b, c: (c, 0)),
                  pl.BlockSpec((MIX_ROWS, DH_B), lambda b, c: (c, 0))],
        out_specs=(pl.BlockSpec((MIX_ROWS, W_A), rows),
                   pl.BlockSpec((MIX_ROWS, W_B), rows),
                   pl.BlockSpec((1, CONV_W - 1, W_A), lambda b, c: (b, 0, 0)),
                   pl.BlockSpec((1, H_B, DH_B, DH_B), lambda b, c: (b, 0, 0, 0))),
        out_shape=(jax.ShapeDtypeStruct((BATCH * SEQ, W_A), BF16),
                   jax.ShapeDtypeStruct((BATCH * SEQ, W_B), BF16),
                   jax.ShapeDtypeStruct((BATCH, CONV_W - 1, W_A), F32),
                   jax.ShapeDtypeStruct((BATCH, H_B, DH_B, DH_B), F32)),
        scratch_shapes=[pltpu.VMEM((CHUNK + 8, W_A), F32)],
        compiler_params=_params(("arbitrary", "arbitrary")),
        name="even_prompt",
    )(p, conv_w, g_ret, cosf, sins)


FT = 1024
FN = BATCH * SEQ // FT
FPB = SEQ // FT
HG = 2
GW = HG * 128


def _chunk_pipeline(n, piece, stages):
    sa, sb, sc, sd, se = stages
    piece(0)
    for c in range(n + 2):
        if c + 1 < n:
            piece(c + 1)
        if c < n:
            sa(c)
        if 1 <= c <= n:
            sc(c - 1)
        if c < n:
            sb(c)
        if 1 <= c <= n:
            sd(c - 1)
        if c >= 2:
            se(c - 2)


def _even_heads_kernel(hp_ref, hs_ref, wq_ref, wk_ref, wv_ref, wz_ref, gret_ref, cos_ref, sin_ref, lg_ref,
                       yb_ref, s_ref, sq_ref, sk_ref, sv_ref, sz_ref, wb, pt, s_scr):
    s = pl.program_id(1)
    L = CHUNK

    @pl.when(s == 0)
    def _():
        for part, w_ref in enumerate((wq_ref, wk_ref, wv_ref, wz_ref)):
            wb[:, part * GW:(part + 1) * GW] = w_ref[...].astype(BF16)
        ps = _dot(hs_ref[...], wb[...])
        for part, o_ref in enumerate((sq_ref, sk_ref, sv_ref, sz_ref)):
            o_ref[...] = ps[:, part * GW:(part + 1) * GW]

    @pl.when(s > 0)
    def _():
        t = s - 1
        n = FT // L
        rows = lambda c: slice(c * L, (c + 1) * L)
        cols = lambda part, i: slice(part * GW + i * DH_B, part * GW + (i + 1) * DH_B)

        def piece(c):
            pt[rows(c), :] = _dot(hp_ref[rows(c), :], wb[...])

        row = lax.broadcasted_iota(jnp.int32, (L, L), 0)
        col = lax.broadcasted_iota(jnp.int32, (L, L), 1)
        causal = row >= col
        diff = jnp.maximum(row - col, 0).astype(F32)
        ti = lax.broadcasted_iota(jnp.int32, (L, 1), 0).astype(F32)
        lgs = [lg_ref[i][:, 0:1] for i in range(HG)]
        decay = [jnp.where(causal, jnp.exp(lg * diff), 0.0) for lg in lgs]
        q_decay = [jnp.exp(lg * (ti + 1.0)) for lg in lgs]
        k_decay = [jnp.exp(lg * (L - 1.0 - ti)) for lg in lgs]
        gamma_l = [jnp.exp(lg * float(L)) for lg in lgs]
        state = {0: [jnp.where(t % FPB == 0, 0.0, s_scr[i]) for i in range(HG)]}
        v = {}

        def stage_a(c):
            cosf = cos_ref[rows(c), :]
            sins = sin_ref[rows(c), :]
            v[c] = []
            for i in range(HG):
                kr = _rope(pt[rows(c), cols(1, i)], cosf, sins) * (DH_B ** -0.5)
                v[c].append(dict(qb=_rope(pt[rows(c), cols(0, i)], cosf, sins).astype(BF16),
                                 kb=kr.astype(BF16),
                                 kd=(kr * k_decay[i]).astype(BF16),
                                 vb=pt[rows(c), cols(2, i)].astype(BF16)))

        def stage_b(c):
            for i, d in enumerate(v[c]):
                d["sc"] = _dot_nt(d["qb"], d["kb"])
                d["cross"] = _dot(d["qb"], state[c][i].astype(BF16))
                d["upd"] = _dot_tn(d["kd"], d["vb"])

        def stage_c(c):
            state[c + 1] = []
            for i, d in enumerate(v[c]):
                d["sc"] = (d["sc"] * decay[i]).astype(BF16)
                state[c + 1].append(gamma_l[i] * state[c][i] + d["upd"])

        def stage_d(c):
            for d in v[c]:
                d["inner"] = _dot(d["sc"], d["vb"])

        def stage_e(c):
            for i, d in enumerate(v.pop(c)):
                o = d["inner"] + d["cross"] * q_decay[i]
                g = gret_ref[0:1, i * DH_B:(i + 1) * DH_B]
                z = pt[rows(c), cols(3, i)]
                yb_ref[rows(c), i * DH_B:(i + 1) * DH_B] = (_head_norm(o, g) * _silu(z)).astype(BF16)

        _chunk_pipeline(n, piece, (stage_a, stage_b, stage_c, stage_d, stage_e))
        for i in range(HG):
            s_scr[i] = state[n][i]
            s_ref[0, i] = state[n][i]


def _even_heads(hp, hs, w, g_ret, cosf, sins, lg_tab):
    k = hp.shape[1]
    ms = hs.shape[0]
    ng = H_B // HG
    base = 4 * W_A // GW
    tile = lambda s: jnp.maximum(s - 1, 0)
    wspec = lambda part: pl.BlockSpec((k, GW), lambda g, s: (0, base + part * ng + g))
    sspec = pl.BlockSpec((ms, GW), lambda g, s: (0, g))
    sshape = jax.ShapeDtypeStruct((ms, W_B), F32)
    return pl.pallas_call(
        _even_heads_kernel,
        grid=(ng, FN + 1),
        in_specs=[pl.BlockSpec((FT, k), lambda g, s: (tile(s), 0)),
                  pl.BlockSpec((ms, k), lambda g, s: (0, 0)),
                  wspec(0), wspec(1), wspec(2), wspec(3),
                  pl.BlockSpec((1, GW), lambda g, s: (0, g)),
                  pl.BlockSpec((FT, DH_B), lambda g, s: (tile(s) % FPB, 0)),
                  pl.BlockSpec((FT, DH_B), lambda g, s: (tile(s) % FPB, 0)),
                  pl.BlockSpec((HG, 1, 128), lambda g, s: (g, 0, 0))],
        out_specs=(pl.BlockSpec((FT, GW), lambda g, s: (tile(s), g)),
                   pl.BlockSpec((1, HG, DH_B, DH_B), lambda g, s: (tile(s) // FPB, g, 0, 0)),
                   sspec, sspec, sspec, sspec),
        out_shape=(jax.ShapeDtypeStruct((BATCH * SEQ, W_B), BF16),
                   jax.ShapeDtypeStruct((BATCH, H_B, DH_B, DH_B), F32),
                   sshape, sshape, sshape, sshape),
        scratch_shapes=[pltpu.VMEM((k, 4 * GW), BF16),
                        pltpu.VMEM((FT, 4 * GW), F32),
                        pltpu.VMEM((HG, DH_B, DH_B), F32)],
        compiler_params=_params(("arbitrary", "arbitrary")),
        name="even_heads",
    )(hp, hs, w, w, w, w, g_ret, cosf, sins, lg_tab)


def _even_conv_kernel(hp_ref, hs_ref, wb_ref, wc_ref, wx_ref, wz_ref, cw_ref,
                      ya_ref, conv_ref, sb_ref, sc_ref, sx_ref, sz_ref, wb, pt, ubuf):
    s = pl.program_id(1)
    L = CHUNK

    @pl.when(s == 0)
    def _():
        for part, w_ref in enumerate((wb_ref, wc_ref, wx_ref, wz_ref)):
            wb[:, part * GW:(part + 1) * GW] = w_ref[...].astype(BF16)
        ps = _dot(hs_ref[...], wb[...])
        for part, o_ref in enumerate((sb_ref, sc_ref, sx_ref, sz_ref)):
            o_ref[...] = ps[:, part * GW:(part + 1) * GW]

    @pl.when(s > 0)
    def _():
        t = s - 1
        n = FT // L
        rows = lambda c: slice(c * L, (c + 1) * L)
        part = lambda p, c: pt[rows(c), p * GW:(p + 1) * GW]

        @pl.when(t % FPB == 0)
        def _():
            ubuf[0:8, :] = jnp.zeros((8, GW), F32)

        @pl.when(t % FPB != 0)
        def _():
            ubuf[0:8, :] = ubuf[FT:FT + 8, :]

        def piece(c):
            pt[rows(c), :] = _dot(hp_ref[rows(c), :], wb[...])

        piece(0)
        for c in range(n):
            if c + 1 < n:
                piece(c + 1)
            u = part(1, c) * part(2, c)
            ubuf[8 + c * L:8 + (c + 1) * L, :] = u
            t0 = ubuf[6 + c * L:6 + (c + 1) * L, :]
            t1 = ubuf[7 + c * L:7 + (c + 1) * L, :]
            conv = cw_ref[0:1, :] * t0 + cw_ref[1:2, :] * t1 + cw_ref[2:3, :] * u
            ya_ref[rows(c), :] = (part(0, c) * conv * _silu(part(3, c))).astype(BF16)
        conv_ref[0] = ubuf[FT + 6:FT + 8, :]


def _even_conv(hp, hs, w, conv_w):
    k = hp.shape[1]
    ms = hs.shape[0]
    ng = W_A // GW
    tile = lambda s: jnp.maximum(s - 1, 0)
    wspec = lambda part: pl.BlockSpec((k, GW), lambda g, s: (0, part * ng + g))
    sspec = pl.BlockSpec((ms, GW), lambda g, s: (0, g))
    sshape = jax.ShapeDtypeStruct((ms, W_A), F32)
    return pl.pallas_call(
        _even_conv_kernel,
        grid=(ng, FN + 1),
        in_specs=[pl.BlockSpec((FT, k), lambda g, s: (tile(s), 0)),
                  pl.BlockSpec((ms, k), lambda g, s: (0, 0)),
                  wspec(0), wspec(1), wspec(2), wspec(3),
                  pl.BlockSpec((CONV_W, GW), lambda g, s: (0, g))],
        out_specs=(pl.BlockSpec((FT, GW), lambda g, s: (tile(s), g)),
                   pl.BlockSpec((1, CONV_W - 1, GW), lambda g, s: (tile(s) // FPB, 0, g)),
                   sspec, sspec, sspec, sspec),
        out_shape=(jax.ShapeDtypeStruct((BATCH * SEQ, W_A), BF16),
                   jax.ShapeDtypeStruct((BATCH, CONV_W - 1, W_A), F32),
                   sshape, sshape, sshape, sshape),
        scratch_shapes=[pltpu.VMEM((k, 4 * GW), BF16),
                        pltpu.VMEM((FT, 4 * GW), F32),
                        pltpu.VMEM((FT + 8, GW), F32)],
        compiler_params=_params(("arbitrary", "arbitrary")),
        name="even_conv",
    )(hp, hs, w, w, w, w, conv_w)


SB = 32
SR = SB * DEC_SEQ


def _even_sample_kernel(ab_ref, ac_ref, ax_ref, az_ref, pq_ref, pk_ref, pv_ref, pz_ref, st_ref, s_ref,
                        cw_ref, gret_ref, cos_ref, sin_ref, lg_ref,
                        ya_ref, u_ref, yb_ref, so_ref, cross_scr):
    h = pl.program_id(1)
    row = lax.broadcasted_iota(jnp.int32, (SR, SR), 0)
    col = lax.broadcasted_iota(jnp.int32, (SR, SR), 1)
    trow = row & 3

    @pl.when(h == 0)
    def _():
        for j in range(W_A // 128):
            sl = slice(j * 128, (j + 1) * 128)
            a_b = ab_ref[:, sl]
            a_c = ac_ref[:, sl]
            a_x = ax_ref[:, sl]
            a_z = az_ref[:, sl]
            u = a_c * a_x
            e = st_ref[:, sl]
            tap1 = jnp.where(trow >= 1, pltpu.roll(u, 1, 0), pltpu.roll(e, SR - 1, 0))
            tap0 = jnp.where(trow >= 2, pltpu.roll(u, 2, 0), e)
            conv = cw_ref[0:1, sl] * tap0 + cw_ref[1:2, sl] * tap1 + cw_ref[2:3, sl] * u
            ya_ref[:, sl] = (a_b * conv * _silu(a_z)).astype(BF16)
            u_ref[:, sl] = u

    lg = lg_ref[0][:, 0:1]
    same = (row >> 2) == (col >> 2)
    dd = trow - (col & 3)
    mask = jnp.where(same, dd, -1) >= 0
    decay = jnp.where(mask, jnp.exp(lg * jnp.maximum(dd, 0).astype(F32)), 0.0)
    tcol = (lax.broadcasted_iota(jnp.int32, (SR, 1), 0) & 3).astype(F32)
    cosf = cos_ref[...]
    sins = sin_ref[...]
    qr = _rope(pq_ref[...], cosf, sins)
    kr = _rope(pk_ref[...], cosf, sins) * (DH_B ** -0.5)
    qb = qr.astype(BF16)
    kb = kr.astype(BF16)
    vb = pv_ref[...].astype(BF16)
    sc = _dot_nt(qb, kb) * decay
    inner = _dot(sc.astype(BF16), vb)
    kdt = (kr * jnp.exp(lg * (DEC_SEQ - 1.0 - tcol))).T
    gamma_l = jnp.exp(lg * float(DEC_SEQ))
    lane_b = col >> 2
    sub = lax.broadcasted_iota(jnp.int32, (8, DH_B), 0)
    for g in range(SR // 8):
        q8 = qb[8 * g:8 * g + 8, :]
        res = []
        for beta in range(2):
            b = 2 * g + beta
            s_old = s_ref[b, 0]
            res.append(_dot(q8, s_old.astype(BF16)))
            lhs = jnp.where(lane_b == b, kdt, 0.0).astype(BF16)
            so_ref[b, 0] = gamma_l * s_old + _dot(lhs, vb)
        cross_scr[8 * g:8 * g + 8, :] = jnp.where(sub < DEC_SEQ, res[0], res[1])
    o = inner + cross_scr[...] * jnp.exp(lg * (tcol + 1.0))
    yb_ref[...] = (_head_norm(o, gret_ref[...]) * _silu(pz_ref[...])).astype(BF16)


def _even_sample(pa, pb, st_exp, s_state, conv_w, g_ret, cosf, sins, lg_tab):
    nb = DEC_BATCH // SB
    const2 = lambda i, h: (0, 0)
    aspec = pl.BlockSpec((SR, W_A), lambda i, h: (i, 0))
    hspec = pl.BlockSpec((SR, DH_B), lambda i, h: (i, h))
    return pl.pallas_call(
        _even_sample_kernel,
        grid=(nb, H_B),
        in_specs=[aspec, aspec, aspec, aspec,
                  hspec, hspec, hspec, hspec,
                  pl.BlockSpec((SR, W_A), lambda i, h: (i, 0)),
                  pl.BlockSpec((SB, 1, DH_B, DH_B), lambda i, h: (i, h, 0, 0)),
                  pl.BlockSpec((CONV_W, W_A), const2),
                  pl.BlockSpec((1, DH_B), lambda i, h: (0, h)),
                  pl.BlockSpec((SR, DH_B), const2),
                  pl.BlockSpec((SR, DH_B), const2),
                  pl.BlockSpec((1, 1, 128), lambda i, h: (h, 0, 0))],
        out_specs=(pl.BlockSpec((SR, W_A), lambda i, h: (i, 0)),
                   pl.BlockSpec((SR, W_A), lambda i, h: (i, 0)),
                   pl.BlockSpec((SR, DH_B), lambda i, h: (i, h)),
                   pl.BlockSpec((SB, 1, DH_B, DH_B), lambda i, h: (i, h, 0, 0))),
        out_shape=(jax.ShapeDtypeStruct((DEC_BATCH * DEC_SEQ, W_A), BF16),
                   jax.ShapeDtypeStruct((DEC_BATCH * DEC_SEQ, W_A), F32),
                   jax.ShapeDtypeStruct((DEC_BATCH * DEC_SEQ, W_B), BF16),
                   jax.ShapeDtypeStruct((DEC_BATCH, H_B, DH_B, DH_B), F32)),
        scratch_shapes=[pltpu.VMEM((SR, DH_B), F32)],
        compiler_params=_params(("arbitrary", "arbitrary")),
        name="even_sample",
    )(*pa, *pb, st_exp, s_state, conv_w, g_ret, cosf, sins, lg_tab)


def _odd_prompt_kernel(p_ref, h_ref, wg_ref, bg_ref, gm_ref, lng_ref, lnb_ref, ws_ref, bst_ref,
                       yc_ref, yd_ref, c_ref, n_ref, m_ref, wgb, wsb):
    L = CHUNK

    @pl.when(jnp.logical_and(pl.program_id(0) == 0, pl.program_id(1) == 0))
    def _():
        wgb[...] = wg_ref[...].astype(BF16)
        keep = (lax.broadcasted_iota(jnp.int32, (L, L), 0) >= lax.broadcasted_iota(jnp.int32, (L, L), 1))
        for g in range(G_D):
            wsb[g] = jnp.where(keep, ws_ref[g], 0.0).astype(BF16)

    @pl.when(pl.program_id(1) == 0)
    def _():
        c_ref[...] = jnp.zeros_like(c_ref)
        n_ref[...] = jnp.zeros_like(n_ref)
        m_ref[...] = jnp.zeros_like(m_ref)

    def chunk(ci, carry):
        rs = pl.ds(pl.multiple_of(ci * L, L), L)
        row = lax.broadcasted_iota(jnp.int32, (L, L), 0)
        col = lax.broadcasted_iota(jnp.int32, (L, L), 1)
        tri = row >= col

        pre = _dot_nt(h_ref[rs, :], wgb[...]) + bg_ref[...]
        lf = _log_sigmoid(pre)
        b_c = _dot_hi(jnp.where(tri, 1.0, 0.0), lf)
        b_r = b_c.T
        pre_r = pre.T
        for h in range(H_C):
            bc = b_c[:, H_C + h:H_C + h + 1]
            br = b_r[H_C + h:H_C + h + 1, :]
            igr = pre_r[h:h + 1, :]
            igc = pre[:, h:h + 1]
            m_prev = m_ref[0, h:h + 1, 0:1]
            log_d = jnp.where(tri, bc - br + igr, NEG_INF)
            log_inter = bc + m_prev
            m_t = jnp.maximum(log_inter, jnp.max(log_d, axis=-1, keepdims=True))
            w = jnp.exp(log_d - m_t)
            w_inter = jnp.exp(log_inter - m_t)
            q = p_ref[rs, h * DQK_C:(h + 1) * DQK_C] * (DQK_C ** -0.5)
            k = p_ref[rs, H_C * DQK_C + h * DQK_C:H_C * DQK_C + (h + 1) * DQK_C]
            v = p_ref[rs, 2 * H_C * DQK_C + h * DV_C:2 * H_C * DQK_C + (h + 1) * DV_C]
            z = p_ref[rs, O_GATE - W_C + h * DV_C:O_GATE - W_C + (h + 1) * DV_C]
            qb = q.astype(BF16)
            kb = k.astype(BF16)
            vb = v.astype(BF16)
            sc = _dot_nt(qb, kb) * w
            c_old = c_ref[0, h]
            n_old = n_ref[0, h:h + 1, :]
            num = _dot(sc.astype(BF16), vb) + w_inter * _dot_nt(qb, c_old.astype(BF16))
            den = jnp.sum(sc, axis=-1, keepdims=True) + w_inter * jnp.sum(q * n_old, axis=-1, keepdims=True)
            hh = num / jnp.maximum(jnp.abs(den), jnp.exp(-m_t))
            m_new = m_t[L - 1:L, :]
            b_last = bc[L - 1:L, :]
            w_end = jnp.exp(b_last - bc + igc - m_new)
            cd = jnp.exp(b_last + m_prev - m_new)
            c_ref[0, h] = cd * c_old + _dot_tn((v * w_end).astype(BF16), kb)
            n_ref[0, h:h + 1, :] = cd * n_old + jnp.sum(w_end * k, axis=0, keepdims=True)
            m_ref[0, h:h + 1, :] = jnp.broadcast_to(m_new, (1, 128))
            sl = slice(h * DV_C, (h + 1) * DV_C)
            yc_ref[rs, sl] = (_head_norm(hh, gm_ref[0:1, sl]) * _silu(z)).astype(BF16)

        dv = lambda g: p_ref[rs, O_GATE + W_D + g * 128:O_GATE + W_D + (g + 1) * 128]
        tot = dv(0)
        for g in range(1, G_D):
            tot = tot + dv(g)
        mu = jnp.sum(tot, axis=-1, keepdims=True) * (1.0 / W_D)
        sq = (dv(0) - mu) * (dv(0) - mu)
        for g in range(1, G_D):
            sq = sq + (dv(g) - mu) * (dv(g) - mu)
        rstd = lax.rsqrt(jnp.sum(sq, axis=-1, keepdims=True) * (1.0 / W_D) + EPS)
        for g in range(G_D):
            sl = slice(g * 128, (g + 1) * 128)
            vn = (dv(g) - mu) * rstd * lng_ref[0:1, sl] + lnb_ref[0:1, sl]
            s = _dot(wsb[g], vn.astype(BF16)) + bst_ref[:, g:g + 1]
            d_u = p_ref[rs, O_GATE + g * 128:O_GATE + (g + 1) * 128]
            d_z = p_ref[rs, O_GATE + 2 * W_D + g * 128:O_GATE + 2 * W_D + (g + 1) * 128]
            yd_ref[rs, sl] = (d_u * s * _silu(d_z)).astype(BF16)
        return carry

    lax.fori_loop(0, MIX_ROWS // L, chunk, 0)


def _odd_prompt(p, h, w_o, bg, gm, lng, lnb, ws, bst):
    nc = SEQ // MIX_ROWS
    rows = lambda b, c: (b * nc + c, 0)
    const2 = lambda b, c: (0, 0)
    return pl.pallas_call(
        _odd_prompt_kernel,
        grid=(BATCH, nc),
        in_specs=[pl.BlockSpec((MIX_ROWS, O_N), rows),
                  pl.BlockSpec((MIX_ROWS, D_MODEL), rows),
                  pl.BlockSpec((128, D_MODEL), lambda b, c: (O_GATE // 128, 0)),
                  pl.BlockSpec((1, 128), const2),
                  pl.BlockSpec((1, W_C), const2),
                  pl.BlockSpec((1, W_D), const2),
                  pl.BlockSpec((1, W_D), const2),
                  pl.BlockSpec((G_D, CHUNK, CHUNK), lambda b, c: (0, 0, 0)),
                  pl.BlockSpec((CHUNK, G_D), const2)],
        out_specs=(pl.BlockSpec((MIX_ROWS, W_C), rows),
                   pl.BlockSpec((MIX_ROWS, W_D), rows),
                   pl.BlockSpec((1, H_C, DV_C, DQK_C), lambda b, c: (b, 0, 0, 0)),
                   pl.BlockSpec((1, H_C, DQK_C), lambda b, c: (b, 0, 0)),
                   pl.BlockSpec((1, 8, 128), lambda b, c: (b, 0, 0))),
        out_shape=(jax.ShapeDtypeStruct((BATCH * SEQ, W_C), BF16),
                   jax.ShapeDtypeStruct((BATCH * SEQ, W_D), BF16),
                   jax.ShapeDtypeStruct((BATCH, H_C, DV_C, DQK_C), F32),
                   jax.ShapeDtypeStruct((BATCH, H_C, DQK_C), F32),
                   jax.ShapeDtypeStruct((BATCH, 8, 128), F32)),
        scratch_shapes=[pltpu.VMEM((128, D_MODEL), BF16),
                        pltpu.VMEM((G_D, CHUNK, CHUNK), BF16)],
        compiler_params=_params(("arbitrary", "arbitrary")),
        name="odd_prompt",
    )(p, h, w_o, bg, gm, lng, lnb, ws, bst)


def _odd_sample_kernel(pq_ref, pk_ref, pv_ref, pz_ref, h_ref, wg_ref, pd_ref,
                       c_ref, nrow_ref, mrow_ref, bg_ref, gm_ref, lng_ref, lnb_ref,
                       rtab_ref, btab_ref,
                       yc_ref, yd_ref, vn_ref, co_ref, no_ref, mo_ref,
                       inter_scr):
    h = pl.program_id(1)
    row = lax.broadcasted_iota(jnp.int32, (SR, SR), 0)
    col = lax.broadcasted_iota(jnp.int32, (SR, SR), 1)
    trow = row & 3

    @pl.when(h == 0)
    def _():
        dv = pd_ref[:, W_D:2 * W_D]
        mu = jnp.mean(dv, axis=-1, keepdims=True)
        xc = dv - mu
        var = jnp.mean(xc * xc, axis=-1, keepdims=True)
        rstd = lax.rsqrt(var + EPS)
        for g in range(G_D):
            sl = slice(g * 128, (g + 1) * 128)
            vn = xc[:, sl] * rstd * lng_ref[0:1, sl] + lnb_ref[0:1, sl]
            vn_ref[:, sl] = vn
            s = rtab_ref[0, :, sl] * vn + btab_ref[:, sl]
            for j in range(1, DEC_SEQ):
                s = s + jnp.where(trow >= j, rtab_ref[j, :, sl] * pltpu.roll(vn, j, 0), 0.0)
            d_u = pd_ref[:, g * 128:(g + 1) * 128]
            d_z = pd_ref[:, 2 * W_D + g * 128:2 * W_D + (g + 1) * 128]
            yd_ref[:, sl] = (d_u * s * _silu(d_z)).astype(BF16)

    same = (row >> 2) == (col >> 2)
    mask = jnp.where(same, trow - (col & 3), -1) >= 0
    pre = _dot_nt(h_ref[...], wg_ref[...].astype(BF16)) + bg_ref[...]
    lf = _log_sigmoid(pre)
    b_full = _dot_hi(jnp.where(mask, 1.0, 0.0), lf)
    sel_i = col == h
    sel_f = col == h + H_C
    ig_c = jnp.sum(jnp.where(sel_i, pre, 0.0), axis=-1, keepdims=True)
    b_c = jnp.sum(jnp.where(sel_f, b_full, 0.0), axis=-1, keepdims=True)
    sel_ir = row == h
    sel_fr = row == h + H_C
    ig_r = jnp.sum(jnp.where(sel_ir, pre.T, 0.0), axis=0, keepdims=True)
    b_r = jnp.sum(jnp.where(sel_fr, b_full.T, 0.0), axis=0, keepdims=True)
    m_prev = mrow_ref[0]
    log_d = jnp.where(mask, b_c - b_r + ig_r, NEG_INF)
    log_inter = b_c + m_prev
    m_t = jnp.maximum(log_inter, jnp.max(log_d, axis=-1, keepdims=True))
    w = jnp.exp(log_d - m_t)
    w_inter = jnp.exp(log_inter - m_t)
    q = pq_ref[...] * (DQK_C ** -0.5)
    k = pk_ref[...]
    v = pv_ref[...]
    qb = q.astype(BF16)
    kb = k.astype(BF16)
    vb = v.astype(BF16)
    sc = _dot_nt(qb, kb) * w
    sub = lax.broadcasted_iota(jnp.int32, (8, DV_C), 0)
    for g in range(SR // 8):
        q8 = qb[8 * g:8 * g + 8, :]
        r0 = _dot_nt(q8, c_ref[2 * g, 0].astype(BF16))
        r1 = _dot_nt(q8, c_ref[2 * g + 1, 0].astype(BF16))
        inter_scr[8 * g:8 * g + 8, :] = jnp.where(sub < DEC_SEQ, r0, r1)
    n_rows = nrow_ref[0]
    num = _dot(sc.astype(BF16), vb) + w_inter * inter_scr[...]
    den = jnp.sum(sc, axis=-1, keepdims=True) + w_inter * jnp.sum(q * n_rows, axis=-1, keepdims=True)
    hh = num / jnp.maximum(jnp.abs(den), jnp.exp(-m_t))
    yc_ref[...] = (_head_norm(hh, gm_ref[...]) * _silu(pz_ref[...])).astype(BF16)

    stats = jnp.where(col == 0, m_t, jnp.where(col == 1, b_c, 0.0))
    last = _dot_hi(jnp.where(col == (row | 3), 1.0, 0.0), stats)
    m_new = last[:, 0:1]
    b_last = last[:, 1:2]
    w_end = jnp.exp(b_last - b_c + ig_c - m_new)
    cd = jnp.exp(b_last + m_prev - m_new)
    mo_ref[0] = m_new
    no_ref[0] = cd * n_rows + _dot_hi(jnp.where(same, 1.0, 0.0), w_end * k)
    vwt = (v * w_end).T
    lane_b = lax.broadcasted_iota(jnp.int32, (DV_C, SR), 1) >> 2
    for b in range(SB):
        lhs = jnp.where(lane_b == b, vwt, 0.0).astype(BF16)
        cd_b = cd[4 * b + 3:4 * b + 4, :]
        co_ref[b, 0] = cd_b * c_ref[b, 0] + _dot(lhs, kb)


def _odd_sample(p, h, w_o, c_state, n_rows, m_rows, bg, gm, lng, lnb, rtab, btab):
    nb = DEC_BATCH // SB
    const2 = lambda i, h: (0, 0)
    koff = H_C * DQK_C // DQK_C
    voff = 2 * H_C * DQK_C // DV_C
    zoff = (2 * H_C * DQK_C + W_C) // DV_C
    return pl.pallas_call(
        _odd_sample_kernel,
        grid=(nb, H_C),
        in_specs=[pl.BlockSpec((SR, DQK_C), lambda i, h: (i, h)),
                  pl.BlockSpec((SR, DQK_C), lambda i, h: (i, koff + h)),
                  pl.BlockSpec((SR, DV_C), lambda i, h: (i, voff + h)),
                  pl.BlockSpec((SR, DV_C), lambda i, h: (i, zoff + h)),
                  pl.BlockSpec((SR, D_MODEL), lambda i, h: (i, 0)),
                  pl.BlockSpec((128, D_MODEL), lambda i, h: (O_GATE // 128, 0)),
                  pl.BlockSpec((SR, 3 * W_D), lambda i, h: (i, 1)),
                  pl.BlockSpec((SB, 1, DV_C, DQK_C), lambda i, h: (i, h, 0, 0)),
                  pl.BlockSpec((1, SR, DQK_C), lambda i, h: (h, i, 0)),
                  pl.BlockSpec((1, SR, 1), lambda i, h: (h, i, 0)),
                  pl.BlockSpec((1, 128), const2),
                  pl.BlockSpec((1, DV_C), lambda i, h: (0, h)),
                  pl.BlockSpec((1, W_D), const2),
                  pl.BlockSpec((1, W_D), const2),
                  pl.BlockSpec((DEC_SEQ, SR, W_D), lambda i, h: (0, 0, 0)),
                  pl.BlockSpec((SR, W_D), const2)],
        out_specs=(pl.BlockSpec((SR, DV_C), lambda i, h: (i, h)),
                   pl.BlockSpec((SR, W_D), lambda i, h: (i, 0)),
                   pl.BlockSpec((SR, W_D), lambda i, h: (i, 0)),
                   pl.BlockSpec((SB, 1, DV_C, DQK_C), lambda i, h: (i, h, 0, 0)),
                   pl.BlockSpec((1, SR, DQK_C), lambda i, h: (h, i, 0)),
                   pl.BlockSpec((1, SR, 1), lambda i, h: (h, i, 0))),
        out_shape=(jax.ShapeDtypeStruct((DEC_BATCH * DEC_SEQ, W_C), BF16),
                   jax.ShapeDtypeStruct((DEC_BATCH * DEC_SEQ, W_D), BF16),
                   jax.ShapeDtypeStruct((DEC_BATCH * DEC_SEQ, W_D), F32),
                   jax.ShapeDtypeStruct((DEC_BATCH, H_C, DV_C, DQK_C), F32),
                   jax.ShapeDtypeStruct((H_C, DEC_BATCH * DEC_SEQ, DQK_C), F32),
                   jax.ShapeDtypeStruct((H_C, DEC_BATCH * DEC_SEQ, 1), F32)),
        scratch_shapes=[pltpu.VMEM((SR, DV_C), F32)],
        compiler_params=_params(("arbitrary", "arbitrary")),
        name="odd_sample",
    )(p, p, p, p, h, w_o, p, c_state, n_rows, m_rows, bg, gm, lng, lnb, rtab, btab)


def _rope_tables(pos):
    inv = ROPE_BASE ** (-jnp.arange(0, DH_B, 2, dtype=F32) / DH_B)
    ang = pos.astype(F32)[:, None] * inv[None, :]
    cos = jnp.cos(ang)
    sin = jnp.sin(ang)
    return jnp.concatenate([cos, cos], axis=-1), jnp.concatenate([-sin, sin], axis=-1)


def kernel(x_prompt, x_sample, state_conv, state_ret, state_mlstm_C, state_mlstm_n, state_mlstm_m,
           norm_even, w_in_even, conv_w, ret_norm, w_out_even,
           norm_odd, w_in_odd, b_gate_odd, mlstm_norm, ln_v_g, ln_v_b,
           w_spatial, b_spatial, w_out_odd, norm_final):
    w_in_e = w_in_even[0]
    w_out_e = w_out_even[0].astype(BF16)
    w_o = w_in_odd[0].T
    w_out_o = w_out_odd[0].astype(BF16)
    g_even = norm_even[0][None, :]
    g_odd = norm_odd[0][None, :]
    g_fin = norm_final[None, :]
    cw = conv_w[0]
    g_ret = ret_norm[0][None, :]
    bg = jnp.concatenate([b_gate_odd[0], jnp.zeros((128 - 2 * H_C,), F32)])[None, :]
    gm = mlstm_norm[0][None, :]
    lng = ln_v_g[0][None, :]
    lnb = ln_v_b[0][None, :]
    ws = w_spatial[0]
    bst = b_spatial[0].T

    cos_p, sin_p = _rope_tables(jnp.arange(SEQ, dtype=jnp.int32))
    cos_s, sin_s = _rope_tables(PAST_LEN + jnp.arange(DEC_SEQ, dtype=jnp.int32))
    cos_s = jnp.tile(cos_s, (SB, 1))
    sin_s = jnp.tile(sin_s, (SB, 1))
    lg_tab = jnp.broadcast_to(jnp.asarray(LOG_GAMMA, F32)[:, None, None], (H_B, 1, 128))

    ws4 = ws[:, :DEC_SEQ, :DEC_SEQ]
    t_idx = jnp.arange(DEC_SEQ)
    rtab = []
    for j in range(DEC_SEQ):
        coef = ws4[:, t_idx, (t_idx - j) % DEC_SEQ]
        tab = jnp.repeat(coef.T[:, :, None], 128, axis=2).reshape(DEC_SEQ, W_D)
        rtab.append(jnp.tile(tab, (SB, 1)))
    rtab = jnp.stack(rtab)
    btab = jnp.tile(jnp.repeat(b_spatial[0][:, :DEC_SEQ].T[:, :, None], 128, axis=2)
                    .reshape(DEC_SEQ, W_D), (SB, 1))

    rs = DEC_BATCH * DEC_SEQ
    xp = x_prompt.reshape(BATCH * SEQ, D_MODEL)
    xs = x_sample.reshape(rs, D_MODEL)
    hp = _norm_cast(xp, g_even, 512)
    hs = _norm_cast(xs, g_even, 512)
    ya, conv_p, *ps_a = _even_conv(hp, hs, w_in_e, cw)
    yb, ret_p, *ps_b = _even_heads(hp, hs, w_in_e, g_ret, cos_p, sin_p, lg_tab)
    st_exp = jnp.pad(state_conv[0], ((0, 0), (0, DEC_SEQ - (CONV_W - 1)), (0, 0))).reshape(rs, W_A)
    ya_s, u_s, yb_s, ret_s = _even_sample(ps_a, ps_b, st_exp, state_ret[0], cw, g_ret, cos_s, sin_s, lg_tab)
    x1, h1 = _outproj(ya, yb, w_out_e, xp, g_odd, 512, final=False)
    x1s, h1s = _outproj(ya_s, yb_s, w_out_e, xs, g_odd, 512, final=False)

    p2, p2s = _in_proj(h1, h1s, w_o, O_N, shift_from=O_GATE // IN_TN, shift=N_GATE)
    yc, yd, c_p, n_p, m_p = _odd_prompt(p2, h1, w_o, bg, gm, lng, lnb, ws, bst)
    n_rows = jnp.repeat(jnp.transpose(state_mlstm_n[0], (1, 0, 2)), DEC_SEQ, axis=1)
    m_rows = jnp.repeat(state_mlstm_m[0].T, DEC_SEQ, axis=1)[:, :, None]
    yc_s, yd_s, vn_s, c_s, no_s, mo_s = _odd_sample(
        p2s, h1s, w_o, state_mlstm_C[0], n_rows, m_rows, bg, gm, lng, lnb, rtab, btab)
    y_prompt = _outproj(yc, yd, w_out_o, x1, g_fin, 512, final=True)
    y_sample = _outproj(yc_s, yd_s, w_out_o, x1s, g_fin, 512, final=True)

    conv_s = u_s.reshape(DEC_BATCH, DEC_SEQ, W_A)[:, DEC_SEQ - (CONV_W - 1):, :]
    n_s = jnp.transpose(no_s[:, DEC_SEQ - 1::DEC_SEQ, :], (1, 0, 2))
    m_s = mo_s[:, DEC_SEQ - 1::DEC_SEQ, 0].T
    return (y_prompt.reshape(BATCH, SEQ, D_MODEL),
            y_sample.reshape(DEC_BATCH, DEC_SEQ, D_MODEL),
            conv_p[None], conv_s[None],
            ret_p[None], ret_s[None],
            c_p[None], c_s[None],
            n_p[None], n_s[None],
            m_p[:, :H_C, 0][None], m_s[None],
            vn_s.reshape(DEC_BATCH, DEC_SEQ, W_D)[None])
```

```python
import functools
import math

import jax
import jax.numpy as jnp
from jax import lax
from jax.experimental import pallas as pl
from jax.experimental.pallas import tpu as pltpu

F32 = jnp.float32
BF16 = jnp.bfloat16

D_MODEL = 2048
BATCH = 4
SEQ = 2048
DEC_BATCH = 128
DEC_SEQ = 4
PAST_LEN = 16384
W_A = 1024
CONV_W = 3
W_B = 1024
H_B = 8
DH_B = 128
E_IN = 8192
W_C = 1024
H_C = 4
DV_C = 256
DQK_C = 128
W_D = 1024
G_D = 8
CHUNK = 128
O_GATE = 2 * H_C * DQK_C + 2 * W_C
N_GATE = 2 * H_C
O_N = O_GATE + 3 * W_D
ROPE_BASE = 10000.0
EPS = 1e-6
LOG_GAMMA = tuple(math.log(1.0 - 2.0 ** (-5.0 - h)) for h in range(H_B))
NEG_INF = float("-inf")
VMEM_LIMIT = 56 * 1024 * 1024

NT_DIMS = (((1,), (1,)), ((), ()))
TN_DIMS = (((0,), (0,)), ((), ()))


def _silu(z):
    return z * (1.0 / (1.0 + jnp.exp(-z)))


def _log_sigmoid(x):
    return jnp.minimum(x, 0.0) - jnp.log1p(jnp.exp(-jnp.abs(x)))


def _dot(a, b):
    return jnp.dot(a, b, preferred_element_type=F32)


def _dot_nt(a, b):
    return lax.dot_general(a, b, NT_DIMS, preferred_element_type=F32)


def _dot_tn(a, b):
    return lax.dot_general(a, b, TN_DIMS, preferred_element_type=F32)


def _dot_hi(a, b):
    return jnp.dot(a, b, preferred_element_type=F32, precision=lax.Precision.HIGHEST)


def _head_norm(o, g):
    mu = jnp.mean(o, axis=-1, keepdims=True)
    oc = o - mu
    var = jnp.mean(oc * oc, axis=-1, keepdims=True)
    return oc * lax.rsqrt(var + EPS) * g


def _params(sem):
    return pltpu.CompilerParams(dimension_semantics=sem, vmem_limit_bytes=VMEM_LIMIT)


def _norm_cast_kernel(x_ref, g_ref, h_ref):
    x = x_ref[...]
    ms = jnp.mean(x * x, axis=-1, keepdims=True)
    h_ref[...] = (x * lax.rsqrt(ms + EPS) * g_ref[...]).astype(BF16)


def _norm_cast(x, g, tm):
    m, d = x.shape
    return pl.pallas_call(
        _norm_cast_kernel,
        grid=(m // tm,),
        in_specs=[pl.BlockSpec((tm, d), lambda i: (i, 0)),
                  pl.BlockSpec((1, d), lambda i: (0, 0))],
        out_specs=pl.BlockSpec((tm, d), lambda i: (i, 0)),
        out_shape=jax.ShapeDtypeStruct((m, d), BF16),
        compiler_params=_params(("arbitrary",)),
        name="norm_cast",
    )(x, g)


IN_TM = 1024
IN_TN = 1024


def _in_proj_kernel(*refs, shift_from, shift):
    if shift:
        hp_ref, hs_ref, w_ref, wn_ref, op_ref, os_ref, wb = refs
    else:
        hp_ref, hs_ref, w_ref, op_ref, os_ref, wb = refs
    j = pl.program_id(0)
    i = pl.program_id(1)

    if shift:
        @pl.when(jnp.logical_and(i == 0, j < shift_from))
        def _():
            wb[...] = w_ref[...].astype(BF16)

        @pl.when(jnp.logical_and(i == 0, j >= shift_from))
        def _():
            wb[...] = jnp.concatenate([w_ref[shift:IN_TN, :], wn_ref[...]], axis=0).astype(BF16)

        mm = _dot_nt
    else:
        @pl.when(i == 0)
        def _():
            wb[...] = w_ref[...].astype(BF16)

        mm = _dot

    @pl.when(i == 0)
    def _():
        os_ref[...] = mm(hs_ref[...], wb[...])

    @pl.when(i > 0)
    def _():
        op_ref[...] = mm(hp_ref[...], wb[...])


def _in_proj(hp, hs, w, n_out, shift_from=0, shift=0):
    mp, k = hp.shape
    ms = hs.shape[0]
    n_prompt = mp // IN_TM
    prow = lambda j, i: (jnp.maximum(i - 1, 0), 0)
    in_specs = [pl.BlockSpec((IN_TM, k), prow),
                pl.BlockSpec((ms, k), lambda j, i: (0, 0))]
    args = [hp, hs, w]
    if shift:
        in_specs.append(pl.BlockSpec((IN_TN, k), lambda j, i: (j, 0)))
        in_specs.append(pl.BlockSpec((shift, k), lambda j, i: ((j + 1) * (IN_TN // shift), 0)))
        args.append(w)
        wb_shape = (IN_TN, k)
    else:
        in_specs.append(pl.BlockSpec((k, IN_TN), lambda j, i: (0, j)))
        wb_shape = (k, IN_TN)
    return pl.pallas_call(
        functools.partial(_in_proj_kernel, shift_from=shift_from, shift=shift),
        grid=(n_out // IN_TN, n_prompt + 1),
        in_specs=in_specs,
        out_specs=(pl.BlockSpec((IN_TM, IN_TN), lambda j, i: (jnp.maximum(i - 1, 0), j)),
                   pl.BlockSpec((ms, IN_TN), lambda j, i: (0, j))),
        out_shape=(jax.ShapeDtypeStruct((mp, n_out), F32),
                   jax.ShapeDtypeStruct((ms, n_out), F32)),
        scratch_shapes=[pltpu.VMEM(wb_shape, BF16)],
        compiler_params=_params(("arbitrary", "arbitrary")),
        name="in_proj",
    )(*args)


def _outproj_kernel(ya_ref, yb_ref, w_ref, x_ref, g_ref, *out_refs, final):
    half = ya_ref.shape[1]
    acc = _dot(ya_ref[...], w_ref[0:half, :]) + _dot(yb_ref[...], w_ref[half:2 * half, :])
    x1 = x_ref[...] + acc
    ms = jnp.mean(x1 * x1, axis=-1, keepdims=True)
    hn = x1 * lax.rsqrt(ms + EPS) * g_ref[...]
    if final:
        out_refs[0][...] = hn
    else:
        out_refs[0][...] = x1
        out_refs[1][...] = hn.astype(BF16)


def _outproj(ya, yb, w, x, g, tm, final):
    m, half = ya.shape
    d = w.shape[1]
    row = lambda i: (i, 0)
    if final:
        out_shape = jax.ShapeDtypeStruct((m, d), F32)
        out_specs = pl.BlockSpec((tm, d), row)
    else:
        out_shape = (jax.ShapeDtypeStruct((m, d), F32), jax.ShapeDtypeStruct((m, d), BF16))
        out_specs = (pl.BlockSpec((tm, d), row), pl.BlockSpec((tm, d), row))
    return pl.pallas_call(
        functools.partial(_outproj_kernel, final=final),
        grid=(m // tm,),
        in_specs=[pl.BlockSpec((tm, half), row),
                  pl.BlockSpec((tm, half), row),
                  pl.BlockSpec((2 * half, d), lambda i: (0, 0)),
                  pl.BlockSpec((tm, d), row),
                  pl.BlockSpec((1, d), lambda i: (0, 0))],
        out_specs=out_specs,
        out_shape=out_shape,
        compiler_params=_params(("arbitrary",)),
        name="out_proj_final" if final else "out_proj",
    )(ya, yb, w, x, g)


MIX_ROWS = 512


def _rope(x, cosf, sins):
    return x * cosf + pltpu.roll(x, DH_B // 2, 1) * sins


def _even_prompt_kernel(p_ref, cw_ref, gret_ref, cos_ref, sin_ref,
                        ya_ref, yb_ref, conv_ref, s_ref, ubuf):
    L = CHUNK

    @pl.when(pl.program_id(1) == 0)
    def _():
        ubuf[0:8, :] = jnp.zeros((8, W_A), F32)
        s_ref[...] = jnp.zeros_like(s_ref)

    def chunk(ci, carry):
        rs = pl.ds(pl.multiple_of(ci * L, L), L)

        for j in range(W_A // 128):
            sl = slice(j * 128, (j + 1) * 128)
            a_b = p_ref[rs, j * 128:(j + 1) * 128]
            a_c = p_ref[rs, W_A + j * 128:W_A + (j + 1) * 128]
            a_x = p_ref[rs, 2 * W_A + j * 128:2 * W_A + (j + 1) * 128]
            a_z = p_ref[rs, 3 * W_A + j * 128:3 * W_A + (j + 1) * 128]
            u = a_c * a_x
            ubuf[8:8 + L, sl] = u
            t0 = ubuf[6:6 + L, sl]
            t1 = ubuf[7:7 + L, sl]
            conv = cw_ref[0:1, sl] * t0 + cw_ref[1:2, sl] * t1 + cw_ref[2:3, sl] * u
            ya_ref[rs, sl] = (a_b * conv * _silu(a_z)).astype(BF16)
            ubuf[0:8, sl] = u[L - 8:L, :]

        cosf = cos_ref[rs, :]
        sins = sin_ref[rs, :]
        row = lax.broadcasted_iota(jnp.int32, (L, L), 0)
        col = lax.broadcasted_iota(jnp.int32, (L, L), 1)
        causal = row >= col
        diff = jnp.maximum(row - col, 0).astype(F32)
        ti = lax.broadcasted_iota(jnp.int32, (L, 1), 0).astype(F32)
        base = 4 * W_A
        for h in range(H_B):
            lg = LOG_GAMMA[h]
            sl = slice(h * DH_B, (h + 1) * DH_B)
            q = p_ref[rs, base + h * DH_B:base + (h + 1) * DH_B]
            k = p_ref[rs, base + W_B + h * DH_B:base + W_B + (h + 1) * DH_B]
            v = p_ref[rs, base + 2 * W_B + h * DH_B:base + 2 * W_B + (h + 1) * DH_B]
            z = p_ref[rs, base + 3 * W_B + h * DH_B:base + 3 * W_B + (h + 1) * DH_B]
            qr = _rope(q, cosf, sins)
            kr = _rope(k, cosf, sins) * (DH_B ** -0.5)
            decay = jnp.where(causal, jnp.exp(lg * diff), 0.0)
            qb = qr.astype(BF16)
            kb = kr.astype(BF16)
            vb = v.astype(BF16)
            sc = _dot_nt(qb, kb) * decay
            inner = _dot(sc.astype(BF16), vb)
            s_old = s_ref[0, h]
            cross = _dot(qb, s_old.astype(BF16)) * jnp.exp(lg * (ti + 1.0))
            kd = (kr * jnp.exp(lg * (L - 1.0 - ti))).astype(BF16)
            s_ref[0, h] = math.exp(lg * L) * s_old + _dot_tn(kd, vb)
            o = inner + cross
            yb_ref[rs, sl] = (_head_norm(o, gret_ref[0:1, sl]) * _silu(z)).astype(BF16)
        return carry

    lax.fori_loop(0, MIX_ROWS // L, chunk, 0)
    conv_ref[0] = ubuf[6:8, :]


def _even_prompt(p, conv_w, g_ret, cosf, sins):
    nc = SEQ // MIX_ROWS
    rows = lambda b, c: (b * nc + c, 0)
    const2 = lambda b, c: (0, 0)
    return pl.pallas_call(
        _even_prompt_kernel,
        grid=(BATCH, nc),
        in_specs=[pl.BlockSpec((MIX_ROWS, E_IN), rows),
                  pl.BlockSpec((CONV_W, W_A), const2),
                  pl.BlockSpec((1, W_B), const2),
                  pl.BlockSpec((MIX_ROWS, DH_B), lambda b, c: (c, 0)),
                  pl.BlockSpec((MIX_ROWS, DH_B), lambda b, c: (c, 0))],
        out_specs=(pl.BlockSpec((MIX_ROWS, W_A), rows),
                   pl.BlockSpec((MIX_ROWS, W_B), rows),
                   pl.BlockSpec((1, CONV_W - 1, W_A), lambda b, c: (b, 0, 0)),
                   pl.BlockSpec((1, H_B, DH_B, DH_B), lambda b, c: (b, 0, 0, 0))),
        out_shape=(jax.ShapeDtypeStruct((BATCH * SEQ, W_A), BF16),
                   jax.ShapeDtypeStruct((BATCH * SEQ, W_B), BF16),
                   jax.ShapeDtypeStruct((BATCH, CONV_W - 1, W_A), F32),
                   jax.ShapeDtypeStruct((BATCH, H_B, DH_B, DH_B), F32)),
        scratch_shapes=[pltpu.VMEM((CHUNK + 8, W_A), F32)],
        compiler_params=_params(("arbitrary", "arbitrary")),
        name="even_prompt",
    )(p, conv_w, g_ret, cosf, sins)


FT = 1024
FN = BATCH * SEQ // FT
FPB = SEQ // FT
HG = 2
GW = HG * 128
PCH = 2


def _chunk_pipeline(n, piece, stages):
    sa, sb, sc, sd, se = stages
    piece(0)
    for c in range(n + 2):
        if c % PCH == 0 and c + PCH < n:
            piece(c // PCH + 1)
        if c < n:
            sa(c)
        if 1 <= c <= n:
            sc(c - 1)
        if c < n:
            sb(c)
        if 1 <= c <= n:
            sd(c - 1)
        if c >= 2:
            se(c - 2)


def _even_heads_kernel(hp_ref, hs_ref, wq_ref, wk_ref, wv_ref, wz_ref, gret_ref, cos_ref, sin_ref, lg_ref,
                       yb_ref, s_ref, sq_ref, sk_ref, sv_ref, sz_ref, wb, pt, s_scr):
    s = pl.program_id(1)
    L = CHUNK

    @pl.when(s == 0)
    def _():
        for part, w_ref in enumerate((wq_ref, wk_ref, wv_ref, wz_ref)):
            wb[:, part * GW:(part + 1) * GW] = w_ref[...].astype(BF16)
        ps = _dot(hs_ref[...], wb[...])
        for part, o_ref in enumerate((sq_ref, sk_ref, sv_ref, sz_ref)):
            o_ref[...] = ps[:, part * GW:(part + 1) * GW]

    @pl.when(s > 0)
    def _():
        t = s - 1
        n = FT // L
        rows = lambda c: slice(c * L, (c + 1) * L)
        cols = lambda part, i: slice(part * GW + i * DH_B, part * GW + (i + 1) * DH_B)

        def piece(k):
            pr = slice(k * PCH * L, (k + 1) * PCH * L)
            pt[pr, :] = _dot(hp_ref[pr, :], wb[...])

        row = lax.broadcasted_iota(jnp.int32, (L, L), 0)
        col = lax.broadcasted_iota(jnp.int32, (L, L), 1)
        causal = row >= col
        diff = jnp.maximum(row - col, 0).astype(F32)
        ti = lax.broadcasted_iota(jnp.int32, (L, 1), 0).astype(F32)
        lgs = [lg_ref[i][:, 0:1] for i in range(HG)]
        decay = [jnp.where(causal, jnp.exp(lg * diff), 0.0) for lg in lgs]
        q_decay = [jnp.exp(lg * (ti + 1.0)) for lg in lgs]
        k_decay = [jnp.exp(lg * (L - 1.0 - ti)) for lg in lgs]
        gamma_l = [jnp.exp(lg * float(L)) for lg in lgs]
        state = {0: [jnp.where(t % FPB == 0, 0.0, s_scr[i]) for i in range(HG)]}
        v = {}

        def stage_a(c):
            cosf = cos_ref[rows(c), :]
            sins = sin_ref[rows(c), :]
            v[c] = []
            for i in range(HG):
                kr = _rope(pt[rows(c), cols(1, i)], cosf, sins) * (DH_B ** -0.5)
                v[c].append(dict(qb=_rope(pt[rows(c), cols(0, i)], cosf, sins).astype(BF16),
                                 kb=kr.astype(BF16),
                                 kd=(kr * k_decay[i]).astype(BF16),
                                 vb=pt[rows(c), cols(2, i)].astype(BF16)))

        def stage_b(c):
            for i, d in enumerate(v[c]):
                d["sc"] = _dot_nt(d["qb"], d["kb"])
                d["cross"] = _dot(d["qb"], state[c][i].astype(BF16))
                d["upd"] = _dot_tn(d["kd"], d["vb"])

        def stage_c(c):
            state[c + 1] = []
            for i, d in enumerate(v[c]):
                d["sc"] = (d["sc"] * decay[i]).astype(BF16)
                state[c + 1].append(gamma_l[i] * state[c][i] + d["upd"])

        def stage_d(c):
            for d in v[c]:
                d["inner"] = _dot(d["sc"], d["vb"])

        def stage_e(c):
            for i, d in enumerate(v.pop(c)):
                o = d["inner"] + d["cross"] * q_decay[i]
                g = gret_ref[0:1, i * DH_B:(i + 1) * DH_B]
                z = pt[rows(c), cols(3, i)]
                yb_ref[rows(c), i * DH_B:(i + 1) * DH_B] = (_head_norm(o, g) * _silu(z)).astype(BF16)

        _chunk_pipeline(n, piece, (stage_a, stage_b, stage_c, stage_d, stage_e))
        for i in range(HG):
            s_scr[i] = state[n][i]
            s_ref[0, i] = state[n][i]


def _even_heads(hp, hs, w, g_ret, cosf, sins, lg_tab):
    k = hp.shape[1]
    ms = hs.shape[0]
    ng = H_B // HG
    base = 4 * W_A // GW
    tile = lambda s: jnp.maximum(s - 1, 0)
    wspec = lambda part: pl.BlockSpec((k, GW), lambda g, s: (0, base + part * ng + g))
    sspec = pl.BlockSpec((ms, GW), lambda g, s: (0, g))
    sshape = jax.ShapeDtypeStruct((ms, W_B), F32)
    return pl.pallas_call(
        _even_heads_kernel,
        grid=(ng, FN + 1),
        in_specs=[pl.BlockSpec((FT, k), lambda g, s: (tile(s), 0)),
                  pl.BlockSpec((ms, k), lambda g, s: (0, 0)),
                  wspec(0), wspec(1), wspec(2), wspec(3),
                  pl.BlockSpec((1, GW), lambda g, s: (0, g)),
                  pl.BlockSpec((FT, DH_B), lambda g, s: (tile(s) % FPB, 0)),
                  pl.BlockSpec((FT, DH_B), lambda g, s: (tile(s) % FPB, 0)),
                  pl.BlockSpec((HG, 1, 128), lambda g, s: (g, 0, 0))],
        out_specs=(pl.BlockSpec((FT, GW), lambda g, s: (tile(s), g)),
                   pl.BlockSpec((1, HG, DH_B, DH_B), lambda g, s: (tile(s) // FPB, g, 0, 0)),
                   sspec, sspec, sspec, sspec),
        out_shape=(jax.ShapeDtypeStruct((BATCH * SEQ, W_B), BF16),
                   jax.ShapeDtypeStruct((BATCH, H_B, DH_B, DH_B), F32),
                   sshape, sshape, sshape, sshape),
        scratch_shapes=[pltpu.VMEM((k, 4 * GW), BF16),
                        pltpu.VMEM((FT, 4 * GW), F32),
                        pltpu.VMEM((HG, DH_B, DH_B), F32)],
        compiler_params=_params(("arbitrary", "arbitrary")),
        name="even_heads",
    )(hp, hs, w, w, w, w, g_ret, cosf, sins, lg_tab)


def _even_conv_kernel(hp_ref, hs_ref, wb_ref, wc_ref, wx_ref, wz_ref, cw_ref,
                      ya_ref, conv_ref, sb_ref, sc_ref, sx_ref, sz_ref, wb, pt, ubuf):
    s = pl.program_id(1)
    L = CHUNK

    @pl.when(s == 0)
    def _():
        for part, w_ref in enumerate((wb_ref, wc_ref, wx_ref, wz_ref)):
            wb[:, part * GW:(part + 1) * GW] = w_ref[...].astype(BF16)
        ps = _dot(hs_ref[...], wb[...])
        for part, o_ref in enumerate((sb_ref, sc_ref, sx_ref, sz_ref)):
            o_ref[...] = ps[:, part * GW:(part + 1) * GW]

    @pl.when(s > 0)
    def _():
        t = s - 1
        n = FT // L
        rows = lambda c: slice(c * L, (c + 1) * L)
        part = lambda p, c: pt[rows(c), p * GW:(p + 1) * GW]

        @pl.when(t % FPB == 0)
        def _():
            ubuf[0:8, :] = jnp.zeros((8, GW), F32)

        @pl.when(t % FPB != 0)
        def _():
            ubuf[0:8, :] = ubuf[FT:FT + 8, :]

        def piece(k):
            pr = slice(k * PCH * L, (k + 1) * PCH * L)
            pt[pr, :] = _dot(hp_ref[pr, :], wb[...])

        piece(0)
        for c in range(n):
            if c % PCH == 0 and c + PCH < n:
                piece(c // PCH + 1)
            u = part(1, c) * part(2, c)
            ubuf[8 + c * L:8 + (c + 1) * L, :] = u
            t0 = ubuf[6 + c * L:6 + (c + 1) * L, :]
            t1 = ubuf[7 + c * L:7 + (c + 1) * L, :]
            conv = cw_ref[0:1, :] * t0 + cw_ref[1:2, :] * t1 + cw_ref[2:3, :] * u
            ya_ref[rows(c), :] = (part(0, c) * conv * _silu(part(3, c))).astype(BF16)
        conv_ref[0] = ubuf[FT + 6:FT + 8, :]


def _even_conv(hp, hs, w, conv_w):
    k = hp.shape[1]
    ms = hs.shape[0]
    ng = W_A // GW
    tile = lambda s: jnp.maximum(s - 1, 0)
    wspec = lambda part: pl.BlockSpec((k, GW), lambda g, s: (0, part * ng + g))
    sspec = pl.BlockSpec((ms, GW), lambda g, s: (0, g))
    sshape = jax.ShapeDtypeStruct((ms, W_A), F32)
    return pl.pallas_call(
        _even_conv_kernel,
        grid=(ng, FN + 1),
        in_specs=[pl.BlockSpec((FT, k), lambda g, s: (tile(s), 0)),
                  pl.BlockSpec((ms, k), lambda g, s: (0, 0)),
                  wspec(0), wspec(1), wspec(2), wspec(3),
                  pl.BlockSpec((CONV_W, GW), lambda g, s: (0, g))],
        out_specs=(pl.BlockSpec((FT, GW), lambda g, s: (tile(s), g)),
                   pl.BlockSpec((1, CONV_W - 1, GW), lambda g, s: (tile(s) // FPB, 0, g)),
                   sspec, sspec, sspec, sspec),
        out_shape=(jax.ShapeDtypeStruct((BATCH * SEQ, W_A), BF16),
                   jax.ShapeDtypeStruct((BATCH, CONV_W - 1, W_A), F32),
                   sshape, sshape, sshape, sshape),
        scratch_shapes=[pltpu.VMEM((k, 4 * GW), BF16),
                        pltpu.VMEM((FT, 4 * GW), F32),
                        pltpu.VMEM((FT + 8, GW), F32)],
        compiler_params=_params(("arbitrary", "arbitrary")),
        name="even_conv",
    )(hp, hs, w, w, w, w, conv_w)


SB = 32
SR = SB * DEC_SEQ


def _even_sample_kernel(ab_ref, ac_ref, ax_ref, az_ref, pq_ref, pk_ref, pv_ref, pz_ref, st_ref, s_ref,
                        cw_ref, gret_ref, cos_ref, sin_ref, lg_ref,
                        ya_ref, u_ref, yb_ref, so_ref, cross_scr):
    h = pl.program_id(1)
    row = lax.broadcasted_iota(jnp.int32, (SR, SR), 0)
    col = lax.broadcasted_iota(jnp.int32, (SR, SR), 1)
    trow = row & 3

    @pl.when(h == 0)
    def _():
        for j in range(W_A // 128):
            sl = slice(j * 128, (j + 1) * 128)
            a_b = ab_ref[:, sl]
            a_c = ac_ref[:, sl]
            a_x = ax_ref[:, sl]
            a_z = az_ref[:, sl]
            u = a_c * a_x
            e = st_ref[:, sl]
            tap1 = jnp.where(trow >= 1, pltpu.roll(u, 1, 0), pltpu.roll(e, SR - 1, 0))
            tap0 = jnp.where(trow >= 2, pltpu.roll(u, 2, 0), e)
            conv = cw_ref[0:1, sl] * tap0 + cw_ref[1:2, sl] * tap1 + cw_ref[2:3, sl] * u
            ya_ref[:, sl] = (a_b * conv * _silu(a_z)).astype(BF16)
            u_ref[:, sl] = u

    lg = lg_ref[0][:, 0:1]
    same = (row >> 2) == (col >> 2)
    dd = trow - (col & 3)
    mask = jnp.where(same, dd, -1) >= 0
    decay = jnp.where(mask, jnp.exp(lg * jnp.maximum(dd, 0).astype(F32)), 0.0)
    tcol = (lax.broadcasted_iota(jnp.int32, (SR, 1), 0) & 3).astype(F32)
    cosf = cos_ref[...]
    sins = sin_ref[...]
    qr = _rope(pq_ref[...], cosf, sins)
    kr = _rope(pk_ref[...], cosf, sins) * (DH_B ** -0.5)
    qb = qr.astype(BF16)
    kb = kr.astype(BF16)
    vb = pv_ref[...].astype(BF16)
    sc = _dot_nt(qb, kb) * decay
    inner = _dot(sc.astype(BF16), vb)
    kdt = (kr * jnp.exp(lg * (DEC_SEQ - 1.0 - tcol))).T
    gamma_l = jnp.exp(lg * float(DEC_SEQ))
    lane_b = col >> 2
    sub = lax.broadcasted_iota(jnp.int32, (8, DH_B), 0)
    for g in range(SR // 8):
        q8 = qb[8 * g:8 * g + 8, :]
        res = []
        for beta in range(2):
            b = 2 * g + beta
            s_old = s_ref[b, 0]
            res.append(_dot(q8, s_old.astype(BF16)))
            lhs = jnp.where(lane_b == b, kdt, 0.0).astype(BF16)
            so_ref[b, 0] = gamma_l * s_old + _dot(lhs, vb)
        cross_scr[8 * g:8 * g + 8, :] = jnp.where(sub < DEC_SEQ, res[0], res[1])
    o = inner + cross_scr[...] * jnp.exp(lg * (tcol + 1.0))
    yb_ref[...] = (_head_norm(o, gret_ref[...]) * _silu(pz_ref[...])).astype(BF16)


def _even_sample(pa, pb, st_exp, s_state, conv_w, g_ret, cosf, sins, lg_tab):
    nb = DEC_BATCH // SB
    const2 = lambda i, h: (0, 0)
    aspec = pl.BlockSpec((SR, W_A), lambda i, h: (i, 0))
    hspec = pl.BlockSpec((SR, DH_B), lambda i, h: (i, h))
    return pl.pallas_call(
        _even_sample_kernel,
        grid=(nb, H_B),
        in_specs=[aspec, aspec, aspec, aspec,
                  hspec, hspec, hspec, hspec,
                  pl.BlockSpec((SR, W_A), lambda i, h: (i, 0)),
                  pl.BlockSpec((SB, 1, DH_B, DH_B), lambda i, h: (i, h, 0, 0)),
                  pl.BlockSpec((CONV_W, W_A), const2),
                  pl.BlockSpec((1, DH_B), lambda i, h: (0, h)),
                  pl.BlockSpec((SR, DH_B), const2),
                  pl.BlockSpec((SR, DH_B), const2),
                  pl.BlockSpec((1, 1, 128), lambda i, h: (h, 0, 0))],
        out_specs=(pl.BlockSpec((SR, W_A), lambda i, h: (i, 0)),
                   pl.BlockSpec((SR, W_A), lambda i, h: (i, 0)),
                   pl.BlockSpec((SR, DH_B), lambda i, h: (i, h)),
                   pl.BlockSpec((SB, 1, DH_B, DH_B), lambda i, h: (i, h, 0, 0))),
        out_shape=(jax.ShapeDtypeStruct((DEC_BATCH * DEC_SEQ, W_A), BF16),
                   jax.ShapeDtypeStruct((DEC_BATCH * DEC_SEQ, W_A), F32),
                   jax.ShapeDtypeStruct((DEC_BATCH * DEC_SEQ, W_B), BF16),
                   jax.ShapeDtypeStruct((DEC_BATCH, H_B, DH_B, DH_B), F32)),
        scratch_shapes=[pltpu.VMEM((SR, DH_B), F32)],
        compiler_params=_params(("arbitrary", "arbitrary")),
        name="even_sample",
    )(*pa, *pb, st_exp, s_state, conv_w, g_ret, cosf, sins, lg_tab)


def _odd_prompt_kernel(p_ref, h_ref, wg_ref, bg_ref, gm_ref, lng_ref, lnb_ref, ws_ref, bst_ref,
                       yc_ref, yd_ref, c_ref, n_ref, m_ref, wgb, wsb):
    L = CHUNK

    @pl.when(jnp.logical_and(pl.program_id(0) == 0, pl.program_id(1) == 0))
    def _():
        wgb[...] = wg_ref[...].astype(BF16)
        keep = (lax.broadcasted_iota(jnp.int32, (L, L), 0) >= lax.broadcasted_iota(jnp.int32, (L, L), 1))
        for g in range(G_D):
            wsb[g] = jnp.where(keep, ws_ref[g], 0.0).astype(BF16)

    @pl.when(pl.program_id(1) == 0)
    def _():
        c_ref[...] = jnp.zeros_like(c_ref)
        n_ref[...] = jnp.zeros_like(n_ref)
        m_ref[...] = jnp.zeros_like(m_ref)

    def chunk(ci, carry):
        rs = pl.ds(pl.multiple_of(ci * L, L), L)
        row = lax.broadcasted_iota(jnp.int32, (L, L), 0)
        col = lax.broadcasted_iota(jnp.int32, (L, L), 1)
        tri = row >= col

        pre = _dot_nt(h_ref[rs, :], wgb[...]) + bg_ref[...]
        lf = _log_sigmoid(pre)
        b_c = _dot_hi(jnp.where(tri, 1.0, 0.0), lf)
        b_r = b_c.T
        pre_r = pre.T
        for h in range(H_C):
            bc = b_c[:, H_C + h:H_C + h + 1]
            br = b_r[H_C + h:H_C + h + 1, :]
            igr = pre_r[h:h + 1, :]
            igc = pre[:, h:h + 1]
            m_prev = m_ref[0, h:h + 1, 0:1]
            log_d = jnp.where(tri, bc - br + igr, NEG_INF)
            log_inter = bc + m_prev
            m_t = jnp.maximum(log_inter, jnp.max(log_d, axis=-1, keepdims=True))
            w = jnp.exp(log_d - m_t)
            w_inter = jnp.exp(log_inter - m_t)
            q = p_ref[rs, h * DQK_C:(h + 1) * DQK_C] * (DQK_C ** -0.5)
            k = p_ref[rs, H_C * DQK_C + h * DQK_C:H_C * DQK_C + (h + 1) * DQK_C]
            v = p_ref[rs, 2 * H_C * DQK_C + h * DV_C:2 * H_C * DQK_C + (h + 1) * DV_C]
            z = p_ref[rs, O_GATE - W_C + h * DV_C:O_GATE - W_C + (h + 1) * DV_C]
            qb = q.astype(BF16)
            kb = k.astype(BF16)
            vb = v.astype(BF16)
            sc = _dot_nt(qb, kb) * w
            c_old = c_ref[0, h]
            n_old = n_ref[0, h:h + 1, :]
            num = _dot(sc.astype(BF16), vb) + w_inter * _dot_nt(qb, c_old.astype(BF16))
            den = jnp.sum(sc, axis=-1, keepdims=True) + w_inter * jnp.sum(q * n_old, axis=-1, keepdims=True)
            hh = num / jnp.maximum(jnp.abs(den), jnp.exp(-m_t))
            m_new = m_t[L - 1:L, :]
            b_last = bc[L - 1:L, :]
            w_end = jnp.exp(b_last - bc + igc - m_new)
            cd = jnp.exp(b_last + m_prev - m_new)
            c_ref[0, h] = cd * c_old + _dot_tn((v * w_end).astype(BF16), kb)
            n_ref[0, h:h + 1, :] = cd * n_old + jnp.sum(w_end * k, axis=0, keepdims=True)
            m_ref[0, h:h + 1, :] = jnp.broadcast_to(m_new, (1, 128))
            sl = slice(h * DV_C, (h + 1) * DV_C)
            yc_ref[rs, sl] = (_head_norm(hh, gm_ref[0:1, sl]) * _silu(z)).astype(BF16)

        dv = lambda g: p_ref[rs, O_GATE + W_D + g * 128:O_GATE + W_D + (g + 1) * 128]
        tot = dv(0)
        for g in range(1, G_D):
            tot = tot + dv(g)
        mu = jnp.sum(tot, axis=-1, keepdims=True) * (1.0 / W_D)
        sq = (dv(0) - mu) * (dv(0) - mu)
        for g in range(1, G_D):
            sq = sq + (dv(g) - mu) * (dv(g) - mu)
        rstd = lax.rsqrt(jnp.sum(sq, axis=-1, keepdims=True) * (1.0 / W_D) + EPS)
        for g in range(G_D):
            sl = slice(g * 128, (g + 1) * 128)
            vn = (dv(g) - mu) * rstd * lng_ref[0:1, sl] + lnb_ref[0:1, sl]
            s = _dot(wsb[g], vn.astype(BF16)) + bst_ref[:, g:g + 1]
            d_u = p_ref[rs, O_GATE + g * 128:O_GATE + (g + 1) * 128]
            d_z = p_ref[rs, O_GATE + 2 * W_D + g * 128:O_GATE + 2 * W_D + (g + 1) * 128]
            yd_ref[rs, sl] = (d_u * s * _silu(d_z)).astype(BF16)
        return carry

    lax.fori_loop(0, MIX_ROWS // L, chunk, 0)


def _odd_prompt(p, h, w_o, bg, gm, lng, lnb, ws, bst):
    nc = SEQ // MIX_ROWS
    rows = lambda b, c: (b * nc + c, 0)
    const2 = lambda b, c: (0, 0)
    return pl.pallas_call(
        _odd_prompt_kernel,
        grid=(BATCH, nc),
        in_specs=[pl.BlockSpec((MIX_ROWS, O_N), rows),
                  pl.BlockSpec((MIX_ROWS, D_MODEL), rows),
                  pl.BlockSpec((128, D_MODEL), lambda b, c: (O_GATE // 128, 0)),
                  pl.BlockSpec((1, 128), const2),
                  pl.BlockSpec((1, W_C), const2),
                  pl.BlockSpec((1, W_D), const2),
                  pl.BlockSpec((1, W_D), const2),
                  pl.BlockSpec((G_D, CHUNK, CHUNK), lambda b, c: (0, 0, 0)),
                  pl.BlockSpec((CHUNK, G_D), const2)],
        out_specs=(pl.BlockSpec((MIX_ROWS, W_C), rows),
                   pl.BlockSpec((MIX_ROWS, W_D), rows),
                   pl.BlockSpec((1, H_C, DV_C, DQK_C), lambda b, c: (b, 0, 0, 0)),
                   pl.BlockSpec((1, H_C, DQK_C), lambda b, c: (b, 0, 0)),
                   pl.BlockSpec((1, 8, 128), lambda b, c: (b, 0, 0))),
        out_shape=(jax.ShapeDtypeStruct((BATCH * SEQ, W_C), BF16),
                   jax.ShapeDtypeStruct((BATCH * SEQ, W_D), BF16),
                   jax.ShapeDtypeStruct((BATCH, H_C, DV_C, DQK_C), F32),
                   jax.ShapeDtypeStruct((BATCH, H_C, DQK_C), F32),
                   jax.ShapeDtypeStruct((BATCH, 8, 128), F32)),
        scratch_shapes=[pltpu.VMEM((128, D_MODEL), BF16),
                        pltpu.VMEM((G_D, CHUNK, CHUNK), BF16)],
        compiler_params=_params(("arbitrary", "arbitrary")),
        name="odd_prompt",
    )(p, h, w_o, bg, gm, lng, lnb, ws, bst)


def _odd_sample_kernel(pq_ref, pk_ref, pv_ref, pz_ref, h_ref, wg_ref, pd_ref,
                       c_ref, nrow_ref, mrow_ref, bg_ref, gm_ref, lng_ref, lnb_ref,
                       rtab_ref, btab_ref,
                       yc_ref, yd_ref, vn_ref, co_ref, no_ref, mo_ref,
                       inter_scr):
    h = pl.program_id(1)
    row = lax.broadcasted_iota(jnp.int32, (SR, SR), 0)
    col = lax.broadcasted_iota(jnp.int32, (SR, SR), 1)
    trow = row & 3

    @pl.when(h == 0)
    def _():
        dv = pd_ref[:, W_D:2 * W_D]
        mu = jnp.mean(dv, axis=-1, keepdims=True)
        xc = dv - mu
        var = jnp.mean(xc * xc, axis=-1, keepdims=True)
        rstd = lax.rsqrt(var + EPS)
        for g in range(G_D):
            sl = slice(g * 128, (g + 1) * 128)
            vn = xc[:, sl] * rstd * lng_ref[0:1, sl] + lnb_ref[0:1, sl]
            vn_ref[:, sl] = vn
            s = rtab_ref[0, :, sl] * vn + btab_ref[:, sl]
            for j in range(1, DEC_SEQ):
                s = s + jnp.where(trow >= j, rtab_ref[j, :, sl] * pltpu.roll(vn, j, 0), 0.0)
            d_u = pd_ref[:, g * 128:(g + 1) * 128]
            d_z = pd_ref[:, 2 * W_D + g * 128:2 * W_D + (g + 1) * 128]
            yd_ref[:, sl] = (d_u * s * _silu(d_z)).astype(BF16)

    same = (row >> 2) == (col >> 2)
    mask = jnp.where(same, trow - (col & 3), -1) >= 0
    pre = _dot_nt(h_ref[...], wg_ref[...].astype(BF16)) + bg_ref[...]
    lf = _log_sigmoid(pre)
    b_full = _dot_hi(jnp.where(mask, 1.0, 0.0), lf)
    sel_i = col == h
    sel_f = col == h + H_C
    ig_c = jnp.sum(jnp.where(sel_i, pre, 0.0), axis=-1, keepdims=True)
    b_c = jnp.sum(jnp.where(sel_f, b_full, 0.0), axis=-1, keepdims=True)
    sel_ir = row == h
    sel_fr = row == h + H_C
    ig_r = jnp.sum(jnp.where(sel_ir, pre.T, 0.0), axis=0, keepdims=True)
    b_r = jnp.sum(jnp.where(sel_fr, b_full.T, 0.0), axis=0, keepdims=True)
    m_prev = mrow_ref[0]
    log_d = jnp.where(mask, b_c - b_r + ig_r, NEG_INF)
    log_inter = b_c + m_prev
    m_t = jnp.maximum(log_inter, jnp.max(log_d, axis=-1, keepdims=True))
    w = jnp.exp(log_d - m_t)
    w_inter = jnp.exp(log_inter - m_t)
    q = pq_ref[...] * (DQK_C ** -0.5)
    k = pk_ref[...]
    v = pv_ref[...]
    qb = q.astype(BF16)
    kb = k.astype(BF16)
    vb = v.astype(BF16)
    sc = _dot_nt(qb, kb) * w
    sub = lax.broadcasted_iota(jnp.int32, (8, DV_C), 0)
    for g in range(SR // 8):
        q8 = qb[8 * g:8 * g + 8, :]
        r0 = _dot_nt(q8, c_ref[2 * g, 0].astype(BF16))
        r1 = _dot_nt(q8, c_ref[2 * g + 1, 0].astype(BF16))
        inter_scr[8 * g:8 * g + 8, :] = jnp.where(sub < DEC_SEQ, r0, r1)
    n_rows = nrow_ref[0]
    num = _dot(sc.astype(BF16), vb) + w_inter * inter_scr[...]
    den = jnp.sum(sc, axis=-1, keepdims=True) + w_inter * jnp.sum(q * n_rows, axis=-1, keepdims=True)
    hh = num / jnp.maximum(jnp.abs(den), jnp.exp(-m_t))
    yc_ref[...] = (_head_norm(hh, gm_ref[...]) * _silu(pz_ref[...])).astype(BF16)

    stats = jnp.where(col == 0, m_t, jnp.where(col == 1, b_c, 0.0))
    last = _dot_hi(jnp.where(col == (row | 3), 1.0, 0.0), stats)
    m_new = last[:, 0:1]
    b_last = last[:, 1:2]
    w_end = jnp.exp(b_last - b_c + ig_c - m_new)
    cd = jnp.exp(b_last + m_prev - m_new)
    mo_ref[0] = m_new
    no_ref[0] = cd * n_rows + _dot_hi(jnp.where(same, 1.0, 0.0), w_end * k)
    vwt = (v * w_end).T
    lane_b = lax.broadcasted_iota(jnp.int32, (DV_C, SR), 1) >> 2
    for b in range(SB):
        lhs = jnp.where(lane_b == b, vwt, 0.0).astype(BF16)
        cd_b = cd[4 * b + 3:4 * b + 4, :]
        co_ref[b, 0] = cd_b * c_ref[b, 0] + _dot(lhs, kb)


def _odd_sample(p, h, w_o, c_state, n_rows, m_rows, bg, gm, lng, lnb, rtab, btab):
    nb = DEC_BATCH // SB
    const2 = lambda i, h: (0, 0)
    koff = H_C * DQK_C // DQK_C
    voff = 2 * H_C * DQK_C // DV_C
    zoff = (2 * H_C * DQK_C + W_C) // DV_C
    return pl.pallas_call(
        _odd_sample_kernel,
        grid=(nb, H_C),
        in_specs=[pl.BlockSpec((SR, DQK_C), lambda i, h: (i, h)),
                  pl.BlockSpec((SR, DQK_C), lambda i, h: (i, koff + h)),
                  pl.BlockSpec((SR, DV_C), lambda i, h: (i, voff + h)),
                  pl.BlockSpec((SR, DV_C), lambda i, h: (i, zoff + h)),
                  pl.BlockSpec((SR, D_MODEL), lambda i, h: (i, 0)),
                  pl.BlockSpec((128, D_MODEL), lambda i, h: (O_GATE // 128, 0)),
                  pl.BlockSpec((SR, 3 * W_D), lambda i, h: (i, 1)),
                  pl.BlockSpec((SB, 1, DV_C, DQK_C), lambda i, h: (i, h, 0, 0)),
                  pl.BlockSpec((1, SR, DQK_C), lambda i, h: (h, i, 0)),
                  pl.BlockSpec((1, SR, 1), lambda i, h: (h, i, 0)),
                  pl.BlockSpec((1, 128), const2),
                  pl.BlockSpec((1, DV_C), lambda i, h: (0, h)),
                  pl.BlockSpec((1, W_D), const2),
                  pl.BlockSpec((1, W_D), const2),
                  pl.BlockSpec((DEC_SEQ, SR, W_D), lambda i, h: (0, 0, 0)),
                  pl.BlockSpec((SR, W_D), const2)],
        out_specs=(pl.BlockSpec((SR, DV_C), lambda i, h: (i, h)),
                   pl.BlockSpec((SR, W_D), lambda i, h: (i, 0)),
                   pl.BlockSpec((SR, W_D), lambda i, h: (i, 0)),
                   pl.BlockSpec((SB, 1, DV_C, DQK_C), lambda i, h: (i, h, 0, 0)),
                   pl.BlockSpec((1, SR, DQK_C), lambda i, h: (h, i, 0)),
                   pl.BlockSpec((1, SR, 1), lambda i, h: (h, i, 0))),
        out_shape=(jax.ShapeDtypeStruct((DEC_BATCH * DEC_SEQ, W_C), BF16),
                   jax.ShapeDtypeStruct((DEC_BATCH * DEC_SEQ, W_D), BF16),
                   jax.ShapeDtypeStruct((DEC_BATCH * DEC_SEQ, W_D), F32),
                   jax.ShapeDtypeStruct((DEC_BATCH, H_C, DV_C, DQK_C), F32),
                   jax.ShapeDtypeStruct((H_C, DEC_BATCH * DEC_SEQ, DQK_C), F32),
                   jax.ShapeDtypeStruct((H_C, DEC_BATCH * DEC_SEQ, 1), F32)),
        scratch_shapes=[pltpu.VMEM((SR, DV_C), F32)],
        compiler_params=_params(("arbitrary", "arbitrary")),
        name="odd_sample",
    )(p, p, p, p, h, w_o, p, c_state, n_rows, m_rows, bg, gm, lng, lnb, rtab, btab)


def _rope_tables(pos):
    inv = ROPE_BASE ** (-jnp.arange(0, DH_B, 2, dtype=F32) / DH_B)
    ang = pos.astype(F32)[:, None] * inv[None, :]
    cos = jnp.cos(ang)
    sin = jnp.sin(ang)
    return jnp.concatenate([cos, cos], axis=-1), jnp.concatenate([-sin, sin], axis=-1)


def kernel(x_prompt, x_sample, state_conv, state_ret, state_mlstm_C, state_mlstm_n, state_mlstm_m,
           norm_even, w_in_even, conv_w, ret_norm, w_out_even,
           norm_odd, w_in_odd, b_gate_odd, mlstm_norm, ln_v_g, ln_v_b,
           w_spatial, b_spatial, w_out_odd, norm_final):
    w_in_e = w_in_even[0]
    w_out_e = w_out_even[0].astype(BF16)
    w_o = w_in_odd[0].T
    w_out_o = w_out_odd[0].astype(BF16)
    g_even = norm_even[0][None, :]
    g_odd = norm_odd[0][None, :]
    g_fin = norm_final[None, :]
    cw = conv_w[0]
    g_ret = ret_norm[0][None, :]
    bg = jnp.concatenate([b_gate_odd[0], jnp.zeros((128 - 2 * H_C,), F32)])[None, :]
    gm = mlstm_norm[0][None, :]
    lng = ln_v_g[0][None, :]
    lnb = ln_v_b[0][None, :]
    ws = w_spatial[0]
    bst = b_spatial[0].T

    cos_p, sin_p = _rope_tables(jnp.arange(SEQ, dtype=jnp.int32))
    cos_s, sin_s = _rope_tables(PAST_LEN + jnp.arange(DEC_SEQ, dtype=jnp.int32))
    cos_s = jnp.tile(cos_s, (SB, 1))
    sin_s = jnp.tile(sin_s, (SB, 1))
    lg_tab = jnp.broadcast_to(jnp.asarray(LOG_GAMMA, F32)[:, None, None], (H_B, 1, 128))

    ws4 = ws[:, :DEC_SEQ, :DEC_SEQ]
    t_idx = jnp.arange(DEC_SEQ)
    rtab = []
    for j in range(DEC_SEQ):
        coef = ws4[:, t_idx, (t_idx - j) % DEC_SEQ]
        tab = jnp.repeat(coef.T[:, :, None], 128, axis=2).reshape(DEC_SEQ, W_D)
        rtab.append(jnp.tile(tab, (SB, 1)))
    rtab = jnp.stack(rtab)
    btab = jnp.tile(jnp.repeat(b_spatial[0][:, :DEC_SEQ].T[:, :, None], 128, axis=2)
                    .reshape(DEC_SEQ, W_D), (SB, 1))

    rs = DEC_BATCH * DEC_SEQ
    xp = x_prompt.reshape(BATCH * SEQ, D_MODEL)
    xs = x_sample.reshape(rs, D_MODEL)
    hp = _norm_cast(xp, g_even, 512)
    hs = _norm_cast(xs, g_even, 512)
    ya, conv_p, *ps_a = _even_conv(hp, hs, w_in_e, cw)
    yb, ret_p, *ps_b = _even_heads(hp, hs, w_in_e, g_ret, cos_p, sin_p, lg_tab)
    st_exp = jnp.pad(state_conv[0], ((0, 0), (0, DEC_SEQ - (CONV_W - 1)), (0, 0))).reshape(rs, W_A)
    ya_s, u_s, yb_s, ret_s = _even_sample(ps_a, ps_b, st_exp, state_ret[0], cw, g_ret, cos_s, sin_s, lg_tab)
    x1, h1 = _outproj(ya, yb, w_out_e, xp, g_odd, 512, final=False)
    x1s, h1s = _outproj(ya_s, yb_s, w_out_e, xs, g_odd, 512, final=False)

    p2, p2s = _in_proj(h1, h1s, w_o, O_N, shift_from=O_GATE // IN_TN, shift=N_GATE)
    yc, yd, c_p, n_p, m_p = _odd_prompt(p2, h1, w_o, bg, gm, lng, lnb, ws, bst)
    n_rows = jnp.repeat(jnp.transpose(state_mlstm_n[0], (1, 0, 2)), DEC_SEQ, axis=1)
    m_rows = jnp.repeat(state_mlstm_m[0].T, DEC_SEQ, axis=1)[:, :, None]
    yc_s, yd_s, vn_s, c_s, no_s, mo_s = _odd_sample(
        p2s, h1s, w_o, state_mlstm_C[0], n_rows, m_rows, bg, gm, lng, lnb, rtab, btab)
    y_prompt = _outproj(yc, yd, w_out_o, x1, g_fin, 512, final=True)
    y_sample = _outproj(yc_s, yd_s, w_out_o, x1s, g_fin, 512, final=True)

    conv_s = u_s.reshape(DEC_BATCH, DEC_SEQ, W_A)[:, DEC_SEQ - (CONV_W - 1):, :]
    n_s = jnp.transpose(no_s[:, DEC_SEQ - 1::DEC_SEQ, :], (1, 0, 2))
    m_s = mo_s[:, DEC_SEQ - 1::DEC_SEQ, 0].T
    return (y_prompt.reshape(BATCH, SEQ, D_MODEL),
            y_sample.reshape(DEC_BATCH, DEC_SEQ, D_MODEL),
            conv_p[None], conv_s[None],
            ret_p[None], ret_s[None],
            c_p[None], c_s[None],
            n_p[None], n_s[None],
            m_p[:, :H_C, 0][None], m_s[None],
            vn_s.reshape(DEC_BATCH, DEC_SEQ, W_D)[None])
```

```python
import functools
import math

import jax
import jax.numpy as jnp
from jax import lax
from jax.experimental import pallas as pl
from jax.experimental.pallas import tpu as pltpu

F32 = jnp.float32
BF16 = jnp.bfloat16

D_MODEL = 2048
BATCH = 4
SEQ = 2048
DEC_BATCH = 128
DEC_SEQ = 4
PAST_LEN = 16384
W_A = 1024
CONV_W = 3
W_B = 1024
H_B = 8
DH_B = 128
E_IN = 8192
W_C = 1024
H_C = 4
DV_C = 256
DQK_C = 128
W_D = 1024
G_D = 8
CHUNK = 128
O_GATE = 2 * H_C * DQK_C + 2 * W_C
N_GATE = 2 * H_C
O_N = O_GATE + 3 * W_D
ROPE_BASE = 10000.0
EPS = 1e-6
LOG_GAMMA = tuple(math.log(1.0 - 2.0 ** (-5.0 - h)) for h in range(H_B))
NEG_INF = float("-inf")
VMEM_LIMIT = 56 * 1024 * 1024

NT_DIMS = (((1,), (1,)), ((), ()))
TN_DIMS = (((0,), (0,)), ((), ()))


def _silu(z):
    return z * (1.0 / (1.0 + jnp.exp(-z)))


def _log_sigmoid(x):
    return jnp.minimum(x, 0.0) - jnp.log1p(jnp.exp(-jnp.abs(x)))


def _dot(a, b):
    return jnp.dot(a, b, preferred_element_type=F32)


def _dot_nt(a, b):
    return lax.dot_general(a, b, NT_DIMS, preferred_element_type=F32)


def _dot_tn(a, b):
    return lax.dot_general(a, b, TN_DIMS, preferred_element_type=F32)


def _dot_hi(a, b):
    return jnp.dot(a, b, preferred_element_type=F32, precision=lax.Precision.HIGHEST)


def _head_norm(o, g):
    mu = jnp.mean(o, axis=-1, keepdims=True)
    oc = o - mu
    var = jnp.mean(oc * oc, axis=-1, keepdims=True)
    return oc * lax.rsqrt(var + EPS) * g


def _params(sem):
    return pltpu.CompilerParams(dimension_semantics=sem, vmem_limit_bytes=VMEM_LIMIT)


def _norm_cast_kernel(x_ref, g_ref, h_ref):
    x = x_ref[...]
    ms = jnp.mean(x * x, axis=-1, keepdims=True)
    h_ref[...] = (x * lax.rsqrt(ms + EPS) * g_ref[...]).astype(BF16)


def _norm_cast(x, g, tm):
    m, d = x.shape
    return pl.pallas_call(
        _norm_cast_kernel,
        grid=(m // tm,),
        in_specs=[pl.BlockSpec((tm, d), lambda i: (i, 0)),
                  pl.BlockSpec((1, d), lambda i: (0, 0))],
        out_specs=pl.BlockSpec((tm, d), lambda i: (i, 0)),
        out_shape=jax.ShapeDtypeStruct((m, d), BF16),
        compiler_params=_params(("arbitrary",)),
        name="norm_cast",
    )(x, g)


IN_TM = 1024
IN_TN = 1024


def _in_proj_kernel(*refs, shift_from, shift):
    if shift:
        hp_ref, hs_ref, w_ref, wn_ref, op_ref, os_ref, wb = refs
    else:
        hp_ref, hs_ref, w_ref, op_ref, os_ref, wb = refs
    j = pl.program_id(0)
    i = pl.program_id(1)

    if shift:
        @pl.when(jnp.logical_and(i == 0, j < shift_from))
        def _():
            wb[...] = w_ref[...].astype(BF16)

        @pl.when(jnp.logical_and(i == 0, j >= shift_from))
        def _():
            wb[...] = jnp.concatenate([w_ref[shift:IN_TN, :], wn_ref[...]], axis=0).astype(BF16)

        mm = _dot_nt
    else:
        @pl.when(i == 0)
        def _():
            wb[...] = w_ref[...].astype(BF16)

        mm = _dot

    @pl.when(i == 0)
    def _():
        os_ref[...] = mm(hs_ref[...], wb[...])

    @pl.when(i > 0)
    def _():
        op_ref[...] = mm(hp_ref[...], wb[...])


def _in_proj(hp, hs, w, n_out, shift_from=0, shift=0, tile0=0):
    mp, k = hp.shape
    ms = hs.shape[0]
    n_prompt = mp // IN_TM
    prow = lambda j, i: (jnp.maximum(i - 1, 0), 0)
    in_specs = [pl.BlockSpec((IN_TM, k), prow),
                pl.BlockSpec((ms, k), lambda j, i: (0, 0))]
    args = [hp, hs, w]
    if shift:
        in_specs.append(pl.BlockSpec((IN_TN, k), lambda j, i: (j + tile0, 0)))
        in_specs.append(pl.BlockSpec((shift, k), lambda j, i: ((j + tile0 + 1) * (IN_TN // shift), 0)))
        args.append(w)
        wb_shape = (IN_TN, k)
    else:
        in_specs.append(pl.BlockSpec((k, IN_TN), lambda j, i: (0, j)))
        wb_shape = (k, IN_TN)
    return pl.pallas_call(
        functools.partial(_in_proj_kernel, shift_from=shift_from, shift=shift),
        grid=(n_out // IN_TN, n_prompt + 1),
        in_specs=in_specs,
        out_specs=(pl.BlockSpec((IN_TM, IN_TN), lambda j, i: (jnp.maximum(i - 1, 0), j)),
                   pl.BlockSpec((ms, IN_TN), lambda j, i: (0, j))),
        out_shape=(jax.ShapeDtypeStruct((mp, n_out), F32),
                   jax.ShapeDtypeStruct((ms, n_out), F32)),
        scratch_shapes=[pltpu.VMEM(wb_shape, BF16)],
        compiler_params=_params(("arbitrary", "arbitrary")),
        name="in_proj",
    )(*args)


def _outproj_kernel(ya_ref, yb_ref, w_ref, x_ref, g_ref, *out_refs, final):
    half = ya_ref.shape[1]
    acc = _dot(ya_ref[...], w_ref[0:half, :]) + _dot(yb_ref[...], w_ref[half:2 * half, :])
    x1 = x_ref[...] + acc
    ms = jnp.mean(x1 * x1, axis=-1, keepdims=True)
    hn = x1 * lax.rsqrt(ms + EPS) * g_ref[...]
    if final:
        out_refs[0][...] = hn
    else:
        out_refs[0][...] = x1
        out_refs[1][...] = hn.astype(BF16)


def _outproj(ya, yb, w, x, g, tm, final):
    m, half = ya.shape
    d = w.shape[1]
    row = lambda i: (i, 0)
    if final:
        out_shape = jax.ShapeDtypeStruct((m, d), F32)
        out_specs = pl.BlockSpec((tm, d), row)
    else:
        out_shape = (jax.ShapeDtypeStruct((m, d), F32), jax.ShapeDtypeStruct((m, d), BF16))
        out_specs = (pl.BlockSpec((tm, d), row), pl.BlockSpec((tm, d), row))
    return pl.pallas_call(
        functools.partial(_outproj_kernel, final=final),
        grid=(m // tm,),
        in_specs=[pl.BlockSpec((tm, half), row),
                  pl.BlockSpec((tm, half), row),
                  pl.BlockSpec((2 * half, d), lambda i: (0, 0)),
                  pl.BlockSpec((tm, d), row),
                  pl.BlockSpec((1, d), lambda i: (0, 0))],
        out_specs=out_specs,
        out_shape=out_shape,
        compiler_params=_params(("arbitrary",)),
        name="out_proj_final" if final else "out_proj",
    )(ya, yb, w, x, g)


MIX_ROWS = 512


def _rope(x, cosf, sins):
    return x * cosf + pltpu.roll(x, DH_B // 2, 1) * sins


def _even_prompt_kernel(p_ref, cw_ref, gret_ref, cos_ref, sin_ref,
                        ya_ref, yb_ref, conv_ref, s_ref, ubuf):
    L = CHUNK

    @pl.when(pl.program_id(1) == 0)
    def _():
        ubuf[0:8, :] = jnp.zeros((8, W_A), F32)
        s_ref[...] = jnp.zeros_like(s_ref)

    def chunk(ci, carry):
        rs = pl.ds(pl.multiple_of(ci * L, L), L)

        for j in range(W_A // 128):
            sl = slice(j * 128, (j + 1) * 128)
            a_b = p_ref[rs, j * 128:(j + 1) * 128]
            a_c = p_ref[rs, W_A + j * 128:W_A + (j + 1) * 128]
            a_x = p_ref[rs, 2 * W_A + j * 128:2 * W_A + (j + 1) * 128]
            a_z = p_ref[rs, 3 * W_A + j * 128:3 * W_A + (j + 1) * 128]
            u = a_c * a_x
            ubuf[8:8 + L, sl] = u
            t0 = ubuf[6:6 + L, sl]
            t1 = ubuf[7:7 + L, sl]
            conv = cw_ref[0:1, sl] * t0 + cw_ref[1:2, sl] * t1 + cw_ref[2:3, sl] * u
            ya_ref[rs, sl] = (a_b * conv * _silu(a_z)).astype(BF16)
            ubuf[0:8, sl] = u[L - 8:L, :]

        cosf = cos_ref[rs, :]
        sins = sin_ref[rs, :]
        row = lax.broadcasted_iota(jnp.int32, (L, L), 0)
        col = lax.broadcasted_iota(jnp.int32, (L, L), 1)
        causal = row >= col
        diff = jnp.maximum(row - col, 0).astype(F32)
        ti = lax.broadcasted_iota(jnp.int32, (L, 1), 0).astype(F32)
        base = 4 * W_A
        for h in range(H_B):
            lg = LOG_GAMMA[h]
            sl = slice(h * DH_B, (h + 1) * DH_B)
            q = p_ref[rs, base + h * DH_B:base + (h + 1) * DH_B]
            k = p_ref[rs, base + W_B + h * DH_B:base + W_B + (h + 1) * DH_B]
            v = p_ref[rs, base + 2 * W_B + h * DH_B:base + 2 * W_B + (h + 1) * DH_B]
            z = p_ref[rs, base + 3 * W_B + h * DH_B:base + 3 * W_B + (h + 1) * DH_B]
            qr = _rope(q, cosf, sins)
            kr = _rope(k, cosf, sins) * (DH_B ** -0.5)
            decay = jnp.where(causal, jnp.exp(lg * diff), 0.0)
            qb = qr.astype(BF16)
            kb = kr.astype(BF16)
            vb = v.astype(BF16)
            sc = _dot_nt(qb, kb) * decay
            inner = _dot(sc.astype(BF16), vb)
            s_old = s_ref[0, h]
            cross = _dot(qb, s_old.astype(BF16)) * jnp.exp(lg * (ti + 1.0))
            kd = (kr * jnp.exp(lg * (L - 1.0 - ti))).astype(BF16)
            s_ref[0, h] = math.exp(lg * L) * s_old + _dot_tn(kd, vb)
            o = inner + cross
            yb_ref[rs, sl] = (_head_norm(o, gret_ref[0:1, sl]) * _silu(z)).astype(BF16)
        return carry

    lax.fori_loop(0, MIX_ROWS // L, chunk, 0)
    conv_ref[0] = ubuf[6:8, :]


def _even_prompt(p, conv_w, g_ret, cosf, sins):
    nc = SEQ // MIX_ROWS
    rows = lambda b, c: (b * nc + c, 0)
    const2 = lambda b, c: (0, 0)
    return pl.pallas_call(
        _even_prompt_kernel,
        grid=(BATCH, nc),
        in_specs=[pl.BlockSpec((MIX_ROWS, E_IN), rows),
                  pl.BlockSpec((CONV_W, W_A), const2),
                  pl.BlockSpec((1, W_B), const2),
                  pl.BlockSpec((MIX_ROWS, DH_B), lambda b, c: (c, 0)),
                  pl.BlockSpec((MIX_ROWS, DH_B), lambda b, c: (c, 0))],
        out_specs=(pl.BlockSpec((MIX_ROWS, W_A), rows),
                   pl.BlockSpec((MIX_ROWS, W_B), rows),
                   pl.BlockSpec((1, CONV_W - 1, W_A), lambda b, c: (b, 0, 0)),
                   pl.BlockSpec((1, H_B, DH_B, DH_B), lambda b, c: (b, 0, 0, 0))),
        out_shape=(jax.ShapeDtypeStruct((BATCH * SEQ, W_A), BF16),
                   jax.ShapeDtypeStruct((BATCH * SEQ, W_B), BF16),
                   jax.ShapeDtypeStruct((BATCH, CONV_W - 1, W_A), F32),
                   jax.ShapeDtypeStruct((BATCH, H_B, DH_B, DH_B), F32)),
        scratch_shapes=[pltpu.VMEM((CHUNK + 8, W_A), F32)],
        compiler_params=_params(("arbitrary", "arbitrary")),
        name="even_prompt",
    )(p, conv_w, g_ret, cosf, sins)


FT = 1024
FN = BATCH * SEQ // FT
FPB = SEQ // FT
HG = 2
GW = HG * 128
PCH = 2


def _chunk_pipeline(n, piece, stages, gate=None):
    sa, sb, sc, sd, se = stages
    npieces = n // PCH
    for j in range(PCH):
        piece(0, j)
    if gate is not None:
        gate(0)
    for c in range(n + 2):
        k, j = c // PCH + 1, c % PCH
        if k < npieces:
            piece(k, j)
        if c < n:
            sa(c)
        if 1 <= c <= n:
            sc(c - 1)
        if c < n:
            sb(c)
        if 1 <= c <= n:
            sd(c - 1)
        if c >= 2:
            se(c - 2)
        if gate is not None and j == PCH - 1 and k < npieces:
            gate(k)


def _even_heads_kernel(hp_ref, hs_ref, wq_ref, wk_ref, wv_ref, wz_ref, gret_ref, cos_ref, sin_ref, lg_ref,
                       yb_ref, s_ref, sq_ref, sk_ref, sv_ref, sz_ref, wb, pt, s_scr):
    s = pl.program_id(1)
    L = CHUNK

    @pl.when(s == 0)
    def _():
        for part, w_ref in enumerate((wq_ref, wk_ref, wv_ref, wz_ref)):
            wb[:, part * GW:(part + 1) * GW] = w_ref[...].astype(BF16)
        ps = _dot(hs_ref[...], wb[...])
        for part, o_ref in enumerate((sq_ref, sk_ref, sv_ref, sz_ref)):
            o_ref[...] = ps[:, part * GW:(part + 1) * GW]

    @pl.when(s > 0)
    def _():
        t = s - 1
        n = FT // L
        rows = lambda c: slice(c * L, (c + 1) * L)
        cols = lambda part, i: slice(part * GW + i * DH_B, part * GW + (i + 1) * DH_B)

        def piece(k, j):
            pr = slice(k * PCH * L, (k + 1) * PCH * L)
            pc = slice(j * 2 * GW, (j + 1) * 2 * GW)
            pt[pr, pc] = _dot(hp_ref[pr, :], wb[:, pc])

        row = lax.broadcasted_iota(jnp.int32, (L, L), 0)
        col = lax.broadcasted_iota(jnp.int32, (L, L), 1)
        causal = row >= col
        diff = jnp.maximum(row - col, 0).astype(F32)
        ti = lax.broadcasted_iota(jnp.int32, (L, 1), 0).astype(F32)
        lgs = [lg_ref[i][:, 0:1] for i in range(HG)]
        decay = [jnp.where(causal, jnp.exp(lg * diff), 0.0) for lg in lgs]
        q_decay = [jnp.exp(lg * (ti + 1.0)) for lg in lgs]
        k_decay = [jnp.exp(lg * (L - 1.0 - ti)) for lg in lgs]
        gamma_l = [jnp.exp(lg * float(L)) for lg in lgs]
        state = {0: [jnp.where(t % FPB == 0, 0.0, s_scr[i]) for i in range(HG)]}
        v = {}

        def stage_a(c):
            cosf = cos_ref[rows(c), :]
            sins = sin_ref[rows(c), :]
            v[c] = []
            for i in range(HG):
                kr = _rope(pt[rows(c), cols(1, i)], cosf, sins) * (DH_B ** -0.5)
                v[c].append(dict(qb=_rope(pt[rows(c), cols(0, i)], cosf, sins).astype(BF16),
                                 kb=kr.astype(BF16),
                                 kd=(kr * k_decay[i]).astype(BF16),
                                 vb=pt[rows(c), cols(2, i)].astype(BF16)))

        def stage_b(c):
            for i, d in enumerate(v[c]):
                d["sc"] = _dot_nt(d["qb"], d["kb"])
                d["cross"] = _dot(d["qb"], state[c][i].astype(BF16))
                d["upd"] = _dot_tn(d["kd"], d["vb"])

        def stage_c(c):
            state[c + 1] = []
            for i, d in enumerate(v[c]):
                d["sc"] = (d["sc"] * decay[i]).astype(BF16)
                state[c + 1].append(gamma_l[i] * state[c][i] + d["upd"])

        def stage_d(c):
            for d in v[c]:
                d["inner"] = _dot(d["sc"], d["vb"])

        def stage_e(c):
            for i, d in enumerate(v.pop(c)):
                o = d["inner"] + d["cross"] * q_decay[i]
                g = gret_ref[0:1, i * DH_B:(i + 1) * DH_B]
                z = pt[rows(c), cols(3, i)]
                yb_ref[rows(c), i * DH_B:(i + 1) * DH_B] = (_head_norm(o, g) * _silu(z)).astype(BF16)

        _chunk_pipeline(n, piece, (stage_a, stage_b, stage_c, stage_d, stage_e))
        for i in range(HG):
            s_scr[i] = state[n][i]
            s_ref[0, i] = state[n][i]


def _even_heads(hp, hs, w, g_ret, cosf, sins, lg_tab):
    k = hp.shape[1]
    ms = hs.shape[0]
    ng = H_B // HG
    base = 4 * W_A // GW
    tile = lambda s: jnp.maximum(s - 1, 0)
    wspec = lambda part: pl.BlockSpec((k, GW), lambda g, s: (0, base + part * ng + g))
    sspec = pl.BlockSpec((ms, GW), lambda g, s: (0, g))
    sshape = jax.ShapeDtypeStruct((ms, W_B), F32)
    return pl.pallas_call(
        _even_heads_kernel,
        grid=(ng, FN + 1),
        in_specs=[pl.BlockSpec((FT, k), lambda g, s: (tile(s), 0)),
                  pl.BlockSpec((ms, k), lambda g, s: (0, 0)),
                  wspec(0), wspec(1), wspec(2), wspec(3),
                  pl.BlockSpec((1, GW), lambda g, s: (0, g)),
                  pl.BlockSpec((FT, DH_B), lambda g, s: (tile(s) % FPB, 0)),
                  pl.BlockSpec((FT, DH_B), lambda g, s: (tile(s) % FPB, 0)),
                  pl.BlockSpec((HG, 1, 128), lambda g, s: (g, 0, 0))],
        out_specs=(pl.BlockSpec((FT, GW), lambda g, s: (tile(s), g)),
                   pl.BlockSpec((1, HG, DH_B, DH_B), lambda g, s: (tile(s) // FPB, g, 0, 0)),
                   sspec, sspec, sspec, sspec),
        out_shape=(jax.ShapeDtypeStruct((BATCH * SEQ, W_B), BF16),
                   jax.ShapeDtypeStruct((BATCH, H_B, DH_B, DH_B), F32),
                   sshape, sshape, sshape, sshape),
        scratch_shapes=[pltpu.VMEM((k, 4 * GW), BF16),
                        pltpu.VMEM((FT, 4 * GW), F32),
                        pltpu.VMEM((HG, DH_B, DH_B), F32)],
        compiler_params=_params(("arbitrary", "arbitrary")),
        name="even_heads",
    )(hp, hs, w, w, w, w, g_ret, cosf, sins, lg_tab)


def _even_conv_kernel(hp_ref, hs_ref, wb_ref, wc_ref, wx_ref, wz_ref, cw_ref,
                      ya_ref, conv_ref, sb_ref, sc_ref, sx_ref, sz_ref, wb, pt, ubuf):
    s = pl.program_id(1)
    L = CHUNK

    @pl.when(s == 0)
    def _():
        for part, w_ref in enumerate((wb_ref, wc_ref, wx_ref, wz_ref)):
            wb[:, part * GW:(part + 1) * GW] = w_ref[...].astype(BF16)
        ps = _dot(hs_ref[...], wb[...])
        for part, o_ref in enumerate((sb_ref, sc_ref, sx_ref, sz_ref)):
            o_ref[...] = ps[:, part * GW:(part + 1) * GW]

    @pl.when(s > 0)
    def _():
        t = s - 1
        n = FT // L
        rows = lambda c: slice(c * L, (c + 1) * L)
        part = lambda p, c: pt[rows(c), p * GW:(p + 1) * GW]

        @pl.when(t % FPB == 0)
        def _():
            ubuf[0:8, :] = jnp.zeros((8, GW), F32)

        @pl.when(t % FPB != 0)
        def _():
            ubuf[0:8, :] = ubuf[FT:FT + 8, :]

        def piece(k, j):
            pr = slice(k * PCH * L, (k + 1) * PCH * L)
            pc = slice(j * 2 * GW, (j + 1) * 2 * GW)
            pt[pr, pc] = _dot(hp_ref[pr, :], wb[:, pc])

        for j in range(PCH):
            piece(0, j)
        for c in range(n):
            if c // PCH + 1 < n // PCH:
                piece(c // PCH + 1, c % PCH)
            u = part(1, c) * part(2, c)
            ubuf[8 + c * L:8 + (c + 1) * L, :] = u
            t0 = ubuf[6 + c * L:6 + (c + 1) * L, :]
            t1 = ubuf[7 + c * L:7 + (c + 1) * L, :]
            conv = cw_ref[0:1, :] * t0 + cw_ref[1:2, :] * t1 + cw_ref[2:3, :] * u
            ya_ref[rows(c), :] = (part(0, c) * conv * _silu(part(3, c))).astype(BF16)
        conv_ref[0] = ubuf[FT + 6:FT + 8, :]


def _even_conv(hp, hs, w, conv_w):
    k = hp.shape[1]
    ms = hs.shape[0]
    ng = W_A // GW
    tile = lambda s: jnp.maximum(s - 1, 0)
    wspec = lambda part: pl.BlockSpec((k, GW), lambda g, s: (0, part * ng + g))
    sspec = pl.BlockSpec((ms, GW), lambda g, s: (0, g))
    sshape = jax.ShapeDtypeStruct((ms, W_A), F32)
    return pl.pallas_call(
        _even_conv_kernel,
        grid=(ng, FN + 1),
        in_specs=[pl.BlockSpec((FT, k), lambda g, s: (tile(s), 0)),
                  pl.BlockSpec((ms, k), lambda g, s: (0, 0)),
                  wspec(0), wspec(1), wspec(2), wspec(3),
                  pl.BlockSpec((CONV_W, GW), lambda g, s: (0, g))],
        out_specs=(pl.BlockSpec((FT, GW), lambda g, s: (tile(s), g)),
                   pl.BlockSpec((1, CONV_W - 1, GW), lambda g, s: (tile(s) // FPB, 0, g)),
                   sspec, sspec, sspec, sspec),
        out_shape=(jax.ShapeDtypeStruct((BATCH * SEQ, W_A), BF16),
                   jax.ShapeDtypeStruct((BATCH, CONV_W - 1, W_A), F32),
                   sshape, sshape, sshape, sshape),
        scratch_shapes=[pltpu.VMEM((k, 4 * GW), BF16),
                        pltpu.VMEM((FT, 4 * GW), F32),
                        pltpu.VMEM((FT + 8, GW), F32)],
        compiler_params=_params(("arbitrary", "arbitrary")),
        name="even_conv",
    )(hp, hs, w, w, w, w, conv_w)


SB = 32
SR = SB * DEC_SEQ


def _even_sample_kernel(ab_ref, ac_ref, ax_ref, az_ref, pq_ref, pk_ref, pv_ref, pz_ref, st_ref, s_ref,
                        cw_ref, gret_ref, cos_ref, sin_ref, lg_ref,
                        ya_ref, u_ref, yb_ref, so_ref, cross_scr):
    h = pl.program_id(1)
    row = lax.broadcasted_iota(jnp.int32, (SR, SR), 0)
    col = lax.broadcasted_iota(jnp.int32, (SR, SR), 1)
    trow = row & 3

    @pl.when(h == 0)
    def _():
        for j in range(W_A // 128):
            sl = slice(j * 128, (j + 1) * 128)
            a_b = ab_ref[:, sl]
            a_c = ac_ref[:, sl]
            a_x = ax_ref[:, sl]
            a_z = az_ref[:, sl]
            u = a_c * a_x
            e = st_ref[:, sl]
            tap1 = jnp.where(trow >= 1, pltpu.roll(u, 1, 0), pltpu.roll(e, SR - 1, 0))
            tap0 = jnp.where(trow >= 2, pltpu.roll(u, 2, 0), e)
            conv = cw_ref[0:1, sl] * tap0 + cw_ref[1:2, sl] * tap1 + cw_ref[2:3, sl] * u
            ya_ref[:, sl] = (a_b * conv * _silu(a_z)).astype(BF16)
            u_ref[:, sl] = u

    lg = lg_ref[0][:, 0:1]
    same = (row >> 2) == (col >> 2)
    dd = trow - (col & 3)
    mask = jnp.where(same, dd, -1) >= 0
    decay = jnp.where(mask, jnp.exp(lg * jnp.maximum(dd, 0).astype(F32)), 0.0)
    tcol = (lax.broadcasted_iota(jnp.int32, (SR, 1), 0) & 3).astype(F32)
    cosf = cos_ref[...]
    sins = sin_ref[...]
    qr = _rope(pq_ref[...], cosf, sins)
    kr = _rope(pk_ref[...], cosf, sins) * (DH_B ** -0.5)
    qb = qr.astype(BF16)
    kb = kr.astype(BF16)
    vb = pv_ref[...].astype(BF16)
    sc = _dot_nt(qb, kb) * decay
    inner = _dot(sc.astype(BF16), vb)
    kdt = (kr * jnp.exp(lg * (DEC_SEQ - 1.0 - tcol))).T
    gamma_l = jnp.exp(lg * float(DEC_SEQ))
    lane_b = col >> 2
    sub = lax.broadcasted_iota(jnp.int32, (8, DH_B), 0)
    for g in range(SR // 8):
        q8 = qb[8 * g:8 * g + 8, :]
        res = []
        for beta in range(2):
            b = 2 * g + beta
            s_old = s_ref[b, 0]
            res.append(_dot(q8, s_old.astype(BF16)))
            lhs = jnp.where(lane_b == b, kdt, 0.0).astype(BF16)
            so_ref[b, 0] = gamma_l * s_old + _dot(lhs, vb)
        cross_scr[8 * g:8 * g + 8, :] = jnp.where(sub < DEC_SEQ, res[0], res[1])
    o = inner + cross_scr[...] * jnp.exp(lg * (tcol + 1.0))
    yb_ref[...] = (_head_norm(o, gret_ref[...]) * _silu(pz_ref[...])).astype(BF16)


def _even_sample(pa, pb, st_exp, s_state, conv_w, g_ret, cosf, sins, lg_tab):
    nb = DEC_BATCH // SB
    const2 = lambda i, h: (0, 0)
    aspec = pl.BlockSpec((SR, W_A), lambda i, h: (i, 0))
    hspec = pl.BlockSpec((SR, DH_B), lambda i, h: (i, h))
    return pl.pallas_call(
        _even_sample_kernel,
        grid=(nb, H_B),
        in_specs=[aspec, aspec, aspec, aspec,
                  hspec, hspec, hspec, hspec,
                  pl.BlockSpec((SR, W_A), lambda i, h: (i, 0)),
                  pl.BlockSpec((SB, 1, DH_B, DH_B), lambda i, h: (i, h, 0, 0)),
                  pl.BlockSpec((CONV_W, W_A), const2),
                  pl.BlockSpec((1, DH_B), lambda i, h: (0, h)),
                  pl.BlockSpec((SR, DH_B), const2),
                  pl.BlockSpec((SR, DH_B), const2),
                  pl.BlockSpec((1, 1, 128), lambda i, h: (h, 0, 0))],
        out_specs=(pl.BlockSpec((SR, W_A), lambda i, h: (i, 0)),
                   pl.BlockSpec((SR, W_A), lambda i, h: (i, 0)),
                   pl.BlockSpec((SR, DH_B), lambda i, h: (i, h)),
                   pl.BlockSpec((SB, 1, DH_B, DH_B), lambda i, h: (i, h, 0, 0))),
        out_shape=(jax.ShapeDtypeStruct((DEC_BATCH * DEC_SEQ, W_A), BF16),
                   jax.ShapeDtypeStruct((DEC_BATCH * DEC_SEQ, W_A), F32),
                   jax.ShapeDtypeStruct((DEC_BATCH * DEC_SEQ, W_B), BF16),
                   jax.ShapeDtypeStruct((DEC_BATCH, H_B, DH_B, DH_B), F32)),
        scratch_shapes=[pltpu.VMEM((SR, DH_B), F32)],
        compiler_params=_params(("arbitrary", "arbitrary")),
        name="even_sample",
    )(*pa, *pb, st_exp, s_state, conv_w, g_ret, cosf, sins, lg_tab)


def _odd_prompt_kernel(p_ref, h_ref, wg_ref, bg_ref, gm_ref, lng_ref, lnb_ref, ws_ref, bst_ref,
                       yc_ref, yd_ref, c_ref, n_ref, m_ref, wgb, wsb):
    L = CHUNK

    @pl.when(jnp.logical_and(pl.program_id(0) == 0, pl.program_id(1) == 0))
    def _():
        wgb[...] = wg_ref[...].astype(BF16)
        keep = (lax.broadcasted_iota(jnp.int32, (L, L), 0) >= lax.broadcasted_iota(jnp.int32, (L, L), 1))
        for g in range(G_D):
            wsb[g] = jnp.where(keep, ws_ref[g], 0.0).astype(BF16)

    @pl.when(pl.program_id(1) == 0)
    def _():
        c_ref[...] = jnp.zeros_like(c_ref)
        n_ref[...] = jnp.zeros_like(n_ref)
        m_ref[...] = jnp.zeros_like(m_ref)

    def chunk(ci, carry):
        rs = pl.ds(pl.multiple_of(ci * L, L), L)
        row = lax.broadcasted_iota(jnp.int32, (L, L), 0)
        col = lax.broadcasted_iota(jnp.int32, (L, L), 1)
        tri = row >= col

        pre = _dot_nt(h_ref[rs, :], wgb[...]) + bg_ref[...]
        lf = _log_sigmoid(pre)
        b_c = _dot_hi(jnp.where(tri, 1.0, 0.0), lf)
        b_r = b_c.T
        pre_r = pre.T
        for h in range(H_C):
            bc = b_c[:, H_C + h:H_C + h + 1]
            br = b_r[H_C + h:H_C + h + 1, :]
            igr = pre_r[h:h + 1, :]
            igc = pre[:, h:h + 1]
            m_prev = m_ref[0, h:h + 1, 0:1]
            log_d = jnp.where(tri, bc - br + igr, NEG_INF)
            log_inter = bc + m_prev
            m_t = jnp.maximum(log_inter, jnp.max(log_d, axis=-1, keepdims=True))
            w = jnp.exp(log_d - m_t)
            w_inter = jnp.exp(log_inter - m_t)
            q = p_ref[rs, h * DQK_C:(h + 1) * DQK_C] * (DQK_C ** -0.5)
            k = p_ref[rs, H_C * DQK_C + h * DQK_C:H_C * DQK_C + (h + 1) * DQK_C]
            v = p_ref[rs, 2 * H_C * DQK_C + h * DV_C:2 * H_C * DQK_C + (h + 1) * DV_C]
            z = p_ref[rs, O_GATE - W_C + h * DV_C:O_GATE - W_C + (h + 1) * DV_C]
            qb = q.astype(BF16)
            kb = k.astype(BF16)
            vb = v.astype(BF16)
            sc = _dot_nt(qb, kb) * w
            c_old = c_ref[0, h]
            n_old = n_ref[0, h:h + 1, :]
            num = _dot(sc.astype(BF16), vb) + w_inter * _dot_nt(qb, c_old.astype(BF16))
            den = jnp.sum(sc, axis=-1, keepdims=True) + w_inter * jnp.sum(q * n_old, axis=-1, keepdims=True)
            hh = num / jnp.maximum(jnp.abs(den), jnp.exp(-m_t))
            m_new = m_t[L - 1:L, :]
            b_last = bc[L - 1:L, :]
            w_end = jnp.exp(b_last - bc + igc - m_new)
            cd = jnp.exp(b_last + m_prev - m_new)
            c_ref[0, h] = cd * c_old + _dot_tn((v * w_end).astype(BF16), kb)
            n_ref[0, h:h + 1, :] = cd * n_old + jnp.sum(w_end * k, axis=0, keepdims=True)
            m_ref[0, h:h + 1, :] = jnp.broadcast_to(m_new, (1, 128))
            sl = slice(h * DV_C, (h + 1) * DV_C)
            yc_ref[rs, sl] = (_head_norm(hh, gm_ref[0:1, sl]) * _silu(z)).astype(BF16)

        dv = lambda g: p_ref[rs, O_GATE + W_D + g * 128:O_GATE + W_D + (g + 1) * 128]
        tot = dv(0)
        for g in range(1, G_D):
            tot = tot + dv(g)
        mu = jnp.sum(tot, axis=-1, keepdims=True) * (1.0 / W_D)
        sq = (dv(0) - mu) * (dv(0) - mu)
        for g in range(1, G_D):
            sq = sq + (dv(g) - mu) * (dv(g) - mu)
        rstd = lax.rsqrt(jnp.sum(sq, axis=-1, keepdims=True) * (1.0 / W_D) + EPS)
        for g in range(G_D):
            sl = slice(g * 128, (g + 1) * 128)
            vn = (dv(g) - mu) * rstd * lng_ref[0:1, sl] + lnb_ref[0:1, sl]
            s = _dot(wsb[g], vn.astype(BF16)) + bst_ref[:, g:g + 1]
            d_u = p_ref[rs, O_GATE + g * 128:O_GATE + (g + 1) * 128]
            d_z = p_ref[rs, O_GATE + 2 * W_D + g * 128:O_GATE + 2 * W_D + (g + 1) * 128]
            yd_ref[rs, sl] = (d_u * s * _silu(d_z)).astype(BF16)
        return carry

    lax.fori_loop(0, MIX_ROWS // L, chunk, 0)


def _odd_prompt(p, h, w_o, bg, gm, lng, lnb, ws, bst):
    nc = SEQ // MIX_ROWS
    rows = lambda b, c: (b * nc + c, 0)
    const2 = lambda b, c: (0, 0)
    return pl.pallas_call(
        _odd_prompt_kernel,
        grid=(BATCH, nc),
        in_specs=[pl.BlockSpec((MIX_ROWS, O_N), rows),
                  pl.BlockSpec((MIX_ROWS, D_MODEL), rows),
                  pl.BlockSpec((128, D_MODEL), lambda b, c: (O_GATE // 128, 0)),
                  pl.BlockSpec((1, 128), const2),
                  pl.BlockSpec((1, W_C), const2),
                  pl.BlockSpec((1, W_D), const2),
                  pl.BlockSpec((1, W_D), const2),
                  pl.BlockSpec((G_D, CHUNK, CHUNK), lambda b, c: (0, 0, 0)),
                  pl.BlockSpec((CHUNK, G_D), const2)],
        out_specs=(pl.BlockSpec((MIX_ROWS, W_C), rows),
                   pl.BlockSpec((MIX_ROWS, W_D), rows),
                   pl.BlockSpec((1, H_C, DV_C, DQK_C), lambda b, c: (b, 0, 0, 0)),
                   pl.BlockSpec((1, H_C, DQK_C), lambda b, c: (b, 0, 0)),
                   pl.BlockSpec((1, 8, 128), lambda b, c: (b, 0, 0))),
        out_shape=(jax.ShapeDtypeStruct((BATCH * SEQ, W_C), BF16),
                   jax.ShapeDtypeStruct((BATCH * SEQ, W_D), BF16),
                   jax.ShapeDtypeStruct((BATCH, H_C, DV_C, DQK_C), F32),
                   jax.ShapeDtypeStruct((BATCH, H_C, DQK_C), F32),
                   jax.ShapeDtypeStruct((BATCH, 8, 128), F32)),
        scratch_shapes=[pltpu.VMEM((128, D_MODEL), BF16),
                        pltpu.VMEM((G_D, CHUNK, CHUNK), BF16)],
        compiler_params=_params(("arbitrary", "arbitrary")),
        name="odd_prompt",
    )(p, h, w_o, bg, gm, lng, lnb, ws, bst)


CW = 2 * DQK_C + 2 * DV_C + 128


def _odd_heads_kernel(hp_ref, hs_ref, wq_ref, wk_ref, wv_ref, wz_ref, wg_ref, bg_ref, gm_ref,
                      yc_ref, c_ref, n_ref, m_ref, sq_ref, sk_ref, sv_ref, sz_ref,
                      wb, pt, c_scr, n_scr, m_scr, gt_scr):
    hd = pl.program_id(0)
    s = pl.program_id(1)
    L = CHUNK
    gc = slice(0, 128)
    qc = slice(128, 128 + DQK_C)
    kc = slice(qc.stop, qc.stop + DQK_C)
    vc = slice(kc.stop, kc.stop + DV_C)
    zc = slice(vc.stop, vc.stop + DV_C)

    @pl.when(s == 0)
    def _():
        wb[qc, :] = wq_ref[...].astype(BF16)
        wb[kc, :] = wk_ref[...].astype(BF16)
        wb[vc, :] = wv_ref[...].astype(BF16)
        wb[zc, :] = wz_ref[...].astype(BF16)
        wb[gc, :] = wg_ref[...].astype(BF16)
        ps = _dot_nt(hs_ref[...], wb[qc.start:CW, :])
        off = lambda sl: slice(sl.start - qc.start, sl.stop - qc.start)
        sq_ref[...] = ps[:, off(qc)]
        sk_ref[...] = ps[:, off(kc)]
        sv_ref[...] = ps[:, off(vc)]
        sz_ref[...] = ps[:, off(zc)]

    @pl.when(s > 0)
    def _():
        t = s - 1
        n = FT // L
        rows = lambda c: slice(c * L, (c + 1) * L)

        def piece(k, j):
            pr = slice(k * PCH * L, (k + 1) * PCH * L)
            pc = (slice(0, vc.start), slice(vc.start, CW))[j]
            pt[pr, pc] = _dot_nt(hp_ref[pr, :], wb[pc, :])

        row = lax.broadcasted_iota(jnp.int32, (L, L), 0)
        col = lax.broadcasted_iota(jnp.int32, (L, L), 1)
        tri = row >= col
        fresh = t % FPB == 0
        cst = {0: jnp.where(fresh, 0.0, c_scr[...])}
        nst = {0: jnp.where(fresh, 0.0, n_scr[...])}
        mst = {0: jnp.where(fresh, 0.0, m_scr[0:1, 0:1])}
        v = {}
        gates = {}

        def gate(k):
            cs = range(k * PCH, (k + 1) * PCH)
            for c in cs:
                gt_scr[c] = (pt[rows(c), gc] + bg_ref[...]).T
            pad = jnp.zeros((8 - PCH, L), F32)
            ig_rows = jnp.concatenate([gt_scr[c, pl.ds(hd, 1), :] for c in cs] + [pad], axis=0)
            lf_rows = jnp.concatenate(
                [_log_sigmoid(gt_scr[c, pl.ds(hd + H_C, 1), :]) for c in cs] + [pad], axis=0)
            b_rows = _dot_hi(lf_rows, jnp.where(row <= col, 1.0, 0.0))
            tall = jnp.zeros((L - 8, L), F32)
            gates[k] = dict(ig_rows=ig_rows, b_rows=b_rows,
                            b_cols=jnp.concatenate([b_rows, tall], axis=0).T,
                            ig_cols=jnp.concatenate([ig_rows, tall], axis=0).T)

        def stage_a(c):
            gk = gates[c // PCH]
            i = c % PCH
            b_r = gk["b_rows"][i:i + 1, :]
            ig_r = gk["ig_rows"][i:i + 1, :]
            b_c = gk["b_cols"][:, i:i + 1]
            ig_c = gk["ig_cols"][:, i:i + 1]
            m_prev = mst[c]
            log_d = jnp.where(tri, b_c - b_r + ig_r, NEG_INF)
            log_inter = b_c + m_prev
            m_t = jnp.maximum(log_inter, jnp.max(log_d, axis=-1, keepdims=True))
            m_new = m_t[L - 1:L, :]
            b_last = b_c[L - 1:L, :]
            w_end = jnp.exp(b_last - b_c + ig_c - m_new)
            cd = jnp.exp(b_last + m_prev - m_new)
            q = pt[rows(c), qc] * (DQK_C ** -0.5)
            k = pt[rows(c), kc]
            vv = pt[rows(c), vc]
            kb = k.astype(BF16)
            nst[c + 1] = cd * nst[c] + jnp.sum(w_end * k, axis=0, keepdims=True)
            mst[c + 1] = m_new
            v[c] = dict(w=jnp.exp(log_d - m_t), w_inter=jnp.exp(log_inter - m_t), floor=jnp.exp(-m_t),
                        cd=cd, qb=q.astype(BF16), kb=kb, vb=vv.astype(BF16),
                        vw=(vv * w_end).astype(BF16),
                        qn=jnp.sum(q * nst[c], axis=-1, keepdims=True))

        def stage_b(c):
            d = v[c]
            d["sc"] = _dot_nt(d["qb"], d["kb"])
            d["upd"] = _dot_tn(d["vw"], d["kb"])
            d["inter"] = _dot_nt(d["qb"], cst[c].astype(BF16))

        def stage_c(c):
            d = v[c]
            sc = d["sc"] * d["w"]
            d["den"] = jnp.sum(sc, axis=-1, keepdims=True) + d["w_inter"] * d["qn"]
            d["sc"] = sc.astype(BF16)
            cst[c + 1] = d["cd"] * cst[c] + d["upd"]

        def stage_d(c):
            d = v[c]
            d["num"] = _dot(d["sc"], d["vb"])

        def stage_e(c):
            d = v.pop(c)
            num = d["num"] + d["w_inter"] * d["inter"]
            hh = num / jnp.maximum(jnp.abs(d["den"]), d["floor"])
            z = pt[rows(c), zc]
            yc_ref[rows(c), :] = (_head_norm(hh, gm_ref[...]) * _silu(z)).astype(BF16)

        _chunk_pipeline(n, piece, (stage_a, stage_b, stage_c, stage_d, stage_e), gate)
        c_scr[...] = cst[n]
        n_scr[...] = nst[n]
        m_scr[...] = jnp.broadcast_to(mst[n], (8, 128))
        c_ref[0, 0] = cst[n]
        n_ref[0, 0] = nst[n]
        m_ref[0, 0] = jnp.broadcast_to(mst[n], (1, 128))


def _odd_heads(hp, hs, w_t, bg, gm):
    k = hp.shape[1]
    ms = hs.shape[0]
    tile = lambda s: jnp.maximum(s - 1, 0)
    koff = H_C * DQK_C // DQK_C
    voff = 2 * H_C * DQK_C // DV_C
    zoff = (2 * H_C * DQK_C + W_C) // DV_C
    seq = lambda h, s: (tile(s) // FPB, h, 0, 0)
    return pl.pallas_call(
        _odd_heads_kernel,
        grid=(H_C, FN + 1),
        in_specs=[pl.BlockSpec((FT, k), lambda h, s: (tile(s), 0)),
                  pl.BlockSpec((ms, k), lambda h, s: (0, 0)),
                  pl.BlockSpec((DQK_C, k), lambda h, s: (h, 0)),
                  pl.BlockSpec((DQK_C, k), lambda h, s: (koff + h, 0)),
                  pl.BlockSpec((DV_C, k), lambda h, s: (voff + h, 0)),
                  pl.BlockSpec((DV_C, k), lambda h, s: (zoff + h, 0)),
                  pl.BlockSpec((128, k), lambda h, s: (O_GATE // 128, 0)),
                  pl.BlockSpec((1, 128), lambda h, s: (0, 0)),
                  pl.BlockSpec((1, DV_C), lambda h, s: (0, h))],
        out_specs=(pl.BlockSpec((FT, DV_C), lambda h, s: (tile(s), h)),
                   pl.BlockSpec((1, 1, DV_C, DQK_C), seq),
                   pl.BlockSpec((1, 1, 1, DQK_C), seq),
                   pl.BlockSpec((1, 1, 1, 128), seq),
                   pl.BlockSpec((ms, DQK_C), lambda h, s: (0, h)),
                   pl.BlockSpec((ms, DQK_C), lambda h, s: (0, h)),
                   pl.BlockSpec((ms, DV_C), lambda h, s: (0, h)),
                   pl.BlockSpec((ms, DV_C), lambda h, s: (0, h))),
        out_shape=(jax.ShapeDtypeStruct((BATCH * SEQ, W_C), BF16),
                   jax.ShapeDtypeStruct((BATCH, H_C, DV_C, DQK_C), F32),
                   jax.ShapeDtypeStruct((BATCH, H_C, 1, DQK_C), F32),
                   jax.ShapeDtypeStruct((BATCH, H_C, 1, 128), F32),
                   jax.ShapeDtypeStruct((ms, H_C * DQK_C), F32),
                   jax.ShapeDtypeStruct((ms, H_C * DQK_C), F32),
                   jax.ShapeDtypeStruct((ms, W_C), F32),
                   jax.ShapeDtypeStruct((ms, W_C), F32)),
        scratch_shapes=[pltpu.VMEM((CW, k), BF16),
                        pltpu.VMEM((FT, CW), F32),
                        pltpu.VMEM((DV_C, DQK_C), F32),
                        pltpu.VMEM((1, DQK_C), F32),
                        pltpu.VMEM((8, 128), F32),
                        pltpu.VMEM((FT // CHUNK, CHUNK, CHUNK), F32)],
        compiler_params=_params(("arbitrary", "arbitrary")),
        name="odd_heads",
    )(hp, hs, w_t, w_t, w_t, w_t, w_t, bg, gm)


def _odd_mlp_kernel(p_ref, lng_ref, lnb_ref, ws_ref, bst_ref, yd_ref, wsb):
    L = CHUNK

    @pl.when(pl.program_id(0) == 0)
    def _():
        keep = (lax.broadcasted_iota(jnp.int32, (L, L), 0) >= lax.broadcasted_iota(jnp.int32, (L, L), 1))
        for g in range(G_D):
            wsb[g] = jnp.where(keep, ws_ref[g], 0.0).astype(BF16)

    def chunk(ci, carry):
        rs = pl.ds(pl.multiple_of(ci * L, L), L)
        dv = lambda g: p_ref[rs, W_D + g * 128:W_D + (g + 1) * 128]
        tot = dv(0)
        for g in range(1, G_D):
            tot = tot + dv(g)
        mu = jnp.sum(tot, axis=-1, keepdims=True) * (1.0 / W_D)
        sq = (dv(0) - mu) * (dv(0) - mu)
        for g in range(1, G_D):
            sq = sq + (dv(g) - mu) * (dv(g) - mu)
        rstd = lax.rsqrt(jnp.sum(sq, axis=-1, keepdims=True) * (1.0 / W_D) + EPS)
        for g in range(G_D):
            sl = slice(g * 128, (g + 1) * 128)
            vn = (dv(g) - mu) * rstd * lng_ref[0:1, sl] + lnb_ref[0:1, sl]
            s = _dot(wsb[g], vn.astype(BF16)) + bst_ref[:, g:g + 1]
            d_u = p_ref[rs, g * 128:(g + 1) * 128]
            d_z = p_ref[rs, 2 * W_D + g * 128:2 * W_D + (g + 1) * 128]
            yd_ref[rs, sl] = (d_u * s * _silu(d_z)).astype(BF16)
        return carry

    lax.fori_loop(0, p_ref.shape[0] // L, chunk, 0)


def _odd_mlp(p, lng, lnb, ws, bst, rows_per_step):
    m = p.shape[0]
    rows = lambda i: (i, 0)
    const2 = lambda i: (0, 0)
    return pl.pallas_call(
        _odd_mlp_kernel,
        grid=(m // rows_per_step,),
        in_specs=[pl.BlockSpec((rows_per_step, 3 * W_D), rows),
                  pl.BlockSpec((1, W_D), const2),
                  pl.BlockSpec((1, W_D), const2),
                  pl.BlockSpec((G_D, CHUNK, CHUNK), lambda i: (0, 0, 0)),
                  pl.BlockSpec((CHUNK, G_D), const2)],
        out_specs=pl.BlockSpec((rows_per_step, W_D), rows),
        out_shape=jax.ShapeDtypeStruct((m, W_D), BF16),
        scratch_shapes=[pltpu.VMEM((G_D, CHUNK, CHUNK), BF16)],
        compiler_params=_params(("arbitrary",)),
        name="odd_mlp",
    )(p, lng, lnb, ws, bst)


def _odd_sample_kernel(pq_ref, pk_ref, pv_ref, pz_ref, h_ref, wg_ref, pd_ref,
                       c_ref, nrow_ref, mrow_ref, bg_ref, gm_ref, lng_ref, lnb_ref,
                       rtab_ref, btab_ref,
                       yc_ref, yd_ref, vn_ref, co_ref, no_ref, mo_ref,
                       inter_scr):
    h = pl.program_id(1)
    row = lax.broadcasted_iota(jnp.int32, (SR, SR), 0)
    col = lax.broadcasted_iota(jnp.int32, (SR, SR), 1)
    trow = row & 3

    @pl.when(h == 0)
    def _():
        dv = pd_ref[:, W_D:2 * W_D]
        mu = jnp.mean(dv, axis=-1, keepdims=True)
        xc = dv - mu
        var = jnp.mean(xc * xc, axis=-1, keepdims=True)
        rstd = lax.rsqrt(var + EPS)
        for g in range(G_D):
            sl = slice(g * 128, (g + 1) * 128)
            vn = xc[:, sl] * rstd * lng_ref[0:1, sl] + lnb_ref[0:1, sl]
            vn_ref[:, sl] = vn
            s = rtab_ref[0, :, sl] * vn + btab_ref[:, sl]
            for j in range(1, DEC_SEQ):
                s = s + jnp.where(trow >= j, rtab_ref[j, :, sl] * pltpu.roll(vn, j, 0), 0.0)
            d_u = pd_ref[:, g * 128:(g + 1) * 128]
            d_z = pd_ref[:, 2 * W_D + g * 128:2 * W_D + (g + 1) * 128]
            yd_ref[:, sl] = (d_u * s * _silu(d_z)).astype(BF16)

    same = (row >> 2) == (col >> 2)
    mask = jnp.where(same, trow - (col & 3), -1) >= 0
    pre = _dot_nt(h_ref[...], wg_ref[...].astype(BF16)) + bg_ref[...]
    lf = _log_sigmoid(pre)
    b_full = _dot_hi(jnp.where(mask, 1.0, 0.0), lf)
    sel_i = col == h
    sel_f = col == h + H_C
    ig_c = jnp.sum(jnp.where(sel_i, pre, 0.0), axis=-1, keepdims=True)
    b_c = jnp.sum(jnp.where(sel_f, b_full, 0.0), axis=-1, keepdims=True)
    sel_ir = row == h
    sel_fr = row == h + H_C
    ig_r = jnp.sum(jnp.where(sel_ir, pre.T, 0.0), axis=0, keepdims=True)
    b_r = jnp.sum(jnp.where(sel_fr, b_full.T, 0.0), axis=0, keepdims=True)
    m_prev = mrow_ref[0]
    log_d = jnp.where(mask, b_c - b_r + ig_r, NEG_INF)
    log_inter = b_c + m_prev
    m_t = jnp.maximum(log_inter, jnp.max(log_d, axis=-1, keepdims=True))
    w = jnp.exp(log_d - m_t)
    w_inter = jnp.exp(log_inter - m_t)
    q = pq_ref[...] * (DQK_C ** -0.5)
    k = pk_ref[...]
    v = pv_ref[...]
    qb = q.astype(BF16)
    kb = k.astype(BF16)
    vb = v.astype(BF16)
    sc = _dot_nt(qb, kb) * w
    sub = lax.broadcasted_iota(jnp.int32, (8, DV_C), 0)
    for g in range(SR // 8):
        q8 = qb[8 * g:8 * g + 8, :]
        r0 = _dot_nt(q8, c_ref[2 * g, 0].astype(BF16))
        r1 = _dot_nt(q8, c_ref[2 * g + 1, 0].astype(BF16))
        inter_scr[8 * g:8 * g + 8, :] = jnp.where(sub < DEC_SEQ, r0, r1)
    n_rows = nrow_ref[0]
    num = _dot(sc.astype(BF16), vb) + w_inter * inter_scr[...]
    den = jnp.sum(sc, axis=-1, keepdims=True) + w_inter * jnp.sum(q * n_rows, axis=-1, keepdims=True)
    hh = num / jnp.maximum(jnp.abs(den), jnp.exp(-m_t))
    yc_ref[...] = (_head_norm(hh, gm_ref[...]) * _silu(pz_ref[...])).astype(BF16)

    stats = jnp.where(col == 0, m_t, jnp.where(col == 1, b_c, 0.0))
    last = _dot_hi(jnp.where(col == (row | 3), 1.0, 0.0), stats)
    m_new = last[:, 0:1]
    b_last = last[:, 1:2]
    w_end = jnp.exp(b_last - b_c + ig_c - m_new)
    cd = jnp.exp(b_last + m_prev - m_new)
    mo_ref[0] = m_new
    no_ref[0] = cd * n_rows + _dot_hi(jnp.where(same, 1.0, 0.0), w_end * k)
    vwt = (v * w_end).T
    lane_b = lax.broadcasted_iota(jnp.int32, (DV_C, SR), 1) >> 2
    for b in range(SB):
        lhs = jnp.where(lane_b == b, vwt, 0.0).astype(BF16)
        cd_b = cd[4 * b + 3:4 * b + 4, :]
        co_ref[b, 0] = cd_b * c_ref[b, 0] + _dot(lhs, kb)


def _odd_sample(pc, pd, h, w_o, c_state, n_rows, m_rows, bg, gm, lng, lnb, rtab, btab):
    nb = DEC_BATCH // SB
    const2 = lambda i, h: (0, 0)
    return pl.pallas_call(
        _odd_sample_kernel,
        grid=(nb, H_C),
        in_specs=[pl.BlockSpec((SR, DQK_C), lambda i, h: (i, h)),
                  pl.BlockSpec((SR, DQK_C), lambda i, h: (i, h)),
                  pl.BlockSpec((SR, DV_C), lambda i, h: (i, h)),
                  pl.BlockSpec((SR, DV_C), lambda i, h: (i, h)),
                  pl.BlockSpec((SR, D_MODEL), lambda i, h: (i, 0)),
                  pl.BlockSpec((128, D_MODEL), lambda i, h: (O_GATE // 128, 0)),
                  pl.BlockSpec((SR, 3 * W_D), lambda i, h: (i, 0)),
                  pl.BlockSpec((SB, 1, DV_C, DQK_C), lambda i, h: (i, h, 0, 0)),
                  pl.BlockSpec((1, SR, DQK_C), lambda i, h: (h, i, 0)),
                  pl.BlockSpec((1, SR, 1), lambda i, h: (h, i, 0)),
                  pl.BlockSpec((1, 128), const2),
                  pl.BlockSpec((1, DV_C), lambda i, h: (0, h)),
                  pl.BlockSpec((1, W_D), const2),
                  pl.BlockSpec((1, W_D), const2),
                  pl.BlockSpec((DEC_SEQ, SR, W_D), lambda i, h: (0, 0, 0)),
                  pl.BlockSpec((SR, W_D), const2)],
        out_specs=(pl.BlockSpec((SR, DV_C), lambda i, h: (i, h)),
                   pl.BlockSpec((SR, W_D), lambda i, h: (i, 0)),
                   pl.BlockSpec((SR, W_D), lambda i, h: (i, 0)),
                   pl.BlockSpec((SB, 1, DV_C, DQK_C), lambda i, h: (i, h, 0, 0)),
                   pl.BlockSpec((1, SR, DQK_C), lambda i, h: (h, i, 0)),
                   pl.BlockSpec((1, SR, 1), lambda i, h: (h, i, 0))),
        out_shape=(jax.ShapeDtypeStruct((DEC_BATCH * DEC_SEQ, W_C), BF16),
                   jax.ShapeDtypeStruct((DEC_BATCH * DEC_SEQ, W_D), BF16),
                   jax.ShapeDtypeStruct((DEC_BATCH * DEC_SEQ, W_D), F32),
                   jax.ShapeDtypeStruct((DEC_BATCH, H_C, DV_C, DQK_C), F32),
                   jax.ShapeDtypeStruct((H_C, DEC_BATCH * DEC_SEQ, DQK_C), F32),
                   jax.ShapeDtypeStruct((H_C, DEC_BATCH * DEC_SEQ, 1), F32)),
        scratch_shapes=[pltpu.VMEM((SR, DV_C), F32)],
        compiler_params=_params(("arbitrary", "arbitrary")),
        name="odd_sample",
    )(*pc, h, w_o, pd, c_state, n_rows, m_rows, bg, gm, lng, lnb, rtab, btab)


def _rope_tables(pos):
    inv = ROPE_BASE ** (-jnp.arange(0, DH_B, 2, dtype=F32) / DH_B)
    ang = pos.astype(F32)[:, None] * inv[None, :]
    cos = jnp.cos(ang)
    sin = jnp.sin(ang)
    return jnp.concatenate([cos, cos], axis=-1), jnp.concatenate([-sin, sin], axis=-1)


def kernel(x_prompt, x_sample, state_conv, state_ret, state_mlstm_C, state_mlstm_n, state_mlstm_m,
           norm_even, w_in_even, conv_w, ret_norm, w_out_even,
           norm_odd, w_in_odd, b_gate_odd, mlstm_norm, ln_v_g, ln_v_b,
           w_spatial, b_spatial, w_out_odd, norm_final):
    w_in_e = w_in_even[0]
    w_out_e = w_out_even[0].astype(BF16)
    w_o = w_in_odd[0].T
    w_out_o = w_out_odd[0].astype(BF16)
    g_even = norm_even[0][None, :]
    g_odd = norm_odd[0][None, :]
    g_fin = norm_final[None, :]
    cw = conv_w[0]
    g_ret = ret_norm[0][None, :]
    bg = jnp.concatenate([b_gate_odd[0], jnp.zeros((128 - 2 * H_C,), F32)])[None, :]
    gm = mlstm_norm[0][None, :]
    lng = ln_v_g[0][None, :]
    lnb = ln_v_b[0][None, :]
    ws = w_spatial[0]
    bst = b_spatial[0].T

    cos_p, sin_p = _rope_tables(jnp.arange(SEQ, dtype=jnp.int32))
    cos_s, sin_s = _rope_tables(PAST_LEN + jnp.arange(DEC_SEQ, dtype=jnp.int32))
    cos_s = jnp.tile(cos_s, (SB, 1))
    sin_s = jnp.tile(sin_s, (SB, 1))
    lg_tab = jnp.broadcast_to(jnp.asarray(LOG_GAMMA, F32)[:, None, None], (H_B, 1, 128))

    ws4 = ws[:, :DEC_SEQ, :DEC_SEQ]
    t_idx = jnp.arange(DEC_SEQ)
    rtab = []
    for j in range(DEC_SEQ):
        coef = ws4[:, t_idx, (t_idx - j) % DEC_SEQ]
        tab = jnp.repeat(coef.T[:, :, None], 128, axis=2).reshape(DEC_SEQ, W_D)
        rtab.append(jnp.tile(tab, (SB, 1)))
    rtab = jnp.stack(rtab)
    btab = jnp.tile(jnp.repeat(b_spatial[0][:, :DEC_SEQ].T[:, :, None], 128, axis=2)
                    .reshape(DEC_SEQ, W_D), (SB, 1))

    rs = DEC_BATCH * DEC_SEQ
    xp = x_prompt.reshape(BATCH * SEQ, D_MODEL)
    xs = x_sample.reshape(rs, D_MODEL)
    hp = _norm_cast(xp, g_even, 512)
    hs = _norm_cast(xs, g_even, 512)
    ya, conv_p, *ps_a = _even_conv(hp, hs, w_in_e, cw)
    yb, ret_p, *ps_b = _even_heads(hp, hs, w_in_e, g_ret, cos_p, sin_p, lg_tab)
    st_exp = jnp.pad(state_conv[0], ((0, 0), (0, DEC_SEQ - (CONV_W - 1)), (0, 0))).reshape(rs, W_A)
    ya_s, u_s, yb_s, ret_s = _even_sample(ps_a, ps_b, st_exp, state_ret[0], cw, g_ret, cos_s, sin_s, lg_tab)
    x1, h1 = _outproj(ya, yb, w_out_e, xp, g_odd, 512, final=False)
    x1s, h1s = _outproj(ya_s, yb_s, w_out_e, xs, g_odd, 512, final=False)

    yc, c_p, n_p, m_p, *ps_c = _odd_heads(h1, h1s, w_o, bg, gm)
    pd, pd_s = _in_proj(h1, h1s, w_o, 3 * W_D, shift=N_GATE, tile0=O_GATE // IN_TN)
    yd = _odd_mlp(pd, lng, lnb, ws, bst, MIX_ROWS)
    n_rows = jnp.repeat(jnp.transpose(state_mlstm_n[0], (1, 0, 2)), DEC_SEQ, axis=1)
    m_rows = jnp.repeat(state_mlstm_m[0].T, DEC_SEQ, axis=1)[:, :, None]
    yc_s, yd_s, vn_s, c_s, no_s, mo_s = _odd_sample(
        ps_c, pd_s, h1s, w_o, state_mlstm_C[0], n_rows, m_rows, bg, gm, lng, lnb, rtab, btab)
    y_prompt = _outproj(yc, yd, w_out_o, x1, g_fin, 512, final=True)
    y_sample = _outproj(yc_s, yd_s, w_out_o, x1s, g_fin, 512, final=True)

    conv_s = u_s.reshape(DEC_BATCH, DEC_SEQ, W_A)[:, DEC_SEQ - (CONV_W - 1):, :]
    n_s = jnp.transpose(no_s[:, DEC_SEQ - 1::DEC_SEQ, :], (1, 0, 2))
    m_s = mo_s[:, DEC_SEQ - 1::DEC_SEQ, 0].T
    return (y_prompt.reshape(BATCH, SEQ, D_MODEL),
            y_sample.reshape(DEC_BATCH, DEC_SEQ, D_MODEL),
            conv_p[None], conv_s[None],
            ret_p[None], ret_s[None],
            c_p[None], c_s[None],
            n_p[:, :, 0, :][None], n_s[None],
            m_p[:, :, 0, 0][None], m_s[None],
            vn_s.reshape(DEC_BATCH, DEC_SEQ, W_D)[None])
```

```python
import functools
import math

import jax
import jax.numpy as jnp
from jax import lax
from jax.experimental import pallas as pl
from jax.experimental.pallas import tpu as pltpu

F32 = jnp.float32
BF16 = jnp.bfloat16

D_MODEL = 2048
BATCH = 4
SEQ = 2048
DEC_BATCH = 128
DEC_SEQ = 4
PAST_LEN = 16384
W_A = 1024
CONV_W = 3
W_B = 1024
H_B = 8
DH_B = 128
E_IN = 8192
W_C = 1024
H_C = 4
DV_C = 256
DQK_C = 128
W_D = 1024
G_D = 8
CHUNK = 128
O_GATE = 2 * H_C * DQK_C + 2 * W_C
N_GATE = 2 * H_C
O_N = O_GATE + 3 * W_D
ROPE_BASE = 10000.0
EPS = 1e-6
LOG_GAMMA = tuple(math.log(1.0 - 2.0 ** (-5.0 - h)) for h in range(H_B))
NEG_INF = float("-inf")
VMEM_LIMIT = 56 * 1024 * 1024

NT_DIMS = (((1,), (1,)), ((), ()))
TN_DIMS = (((0,), (0,)), ((), ()))


def _silu(z):
    return z * (1.0 / (1.0 + jnp.exp(-z)))


def _log_sigmoid(x):
    return jnp.minimum(x, 0.0) - jnp.log1p(jnp.exp(-jnp.abs(x)))


def _dot(a, b):
    return jnp.dot(a, b, preferred_element_type=F32)


def _dot_nt(a, b):
    return lax.dot_general(a, b, NT_DIMS, preferred_element_type=F32)


def _dot_tn(a, b):
    return lax.dot_general(a, b, TN_DIMS, preferred_element_type=F32)


def _dot_hi(a, b):
    return jnp.dot(a, b, preferred_element_type=F32, precision=lax.Precision.HIGHEST)


def _head_norm(o, g):
    mu = jnp.mean(o, axis=-1, keepdims=True)
    oc = o - mu
    var = jnp.mean(oc * oc, axis=-1, keepdims=True)
    return oc * lax.rsqrt(var + EPS) * g


def _params(sem):
    return pltpu.CompilerParams(dimension_semantics=sem, vmem_limit_bytes=VMEM_LIMIT)


def _norm_cast_kernel(x_ref, g_ref, h_ref):
    x = x_ref[...]
    ms = jnp.mean(x * x, axis=-1, keepdims=True)
    h_ref[...] = (x * lax.rsqrt(ms + EPS) * g_ref[...]).astype(BF16)


def _norm_cast(x, g, tm):
    m, d = x.shape
    return pl.pallas_call(
        _norm_cast_kernel,
        grid=(m // tm,),
        in_specs=[pl.BlockSpec((tm, d), lambda i: (i, 0)),
                  pl.BlockSpec((1, d), lambda i: (0, 0))],
        out_specs=pl.BlockSpec((tm, d), lambda i: (i, 0)),
        out_shape=jax.ShapeDtypeStruct((m, d), BF16),
        compiler_params=_params(("arbitrary",)),
        name="norm_cast",
    )(x, g)


IN_TM = 1024
IN_TN = 1024


def _in_proj_kernel(*refs, shift_from, shift):
    if shift:
        hp_ref, hs_ref, w_ref, wn_ref, op_ref, os_ref, wb = refs
    else:
        hp_ref, hs_ref, w_ref, op_ref, os_ref, wb = refs
    j = pl.program_id(0)
    i = pl.program_id(1)

    if shift:
        @pl.when(jnp.logical_and(i == 0, j < shift_from))
        def _():
            wb[...] = w_ref[...].astype(BF16)

        @pl.when(jnp.logical_and(i == 0, j >= shift_from))
        def _():
            wb[...] = jnp.concatenate([w_ref[shift:IN_TN, :], wn_ref[...]], axis=0).astype(BF16)

        mm = _dot_nt
    else:
        @pl.when(i == 0)
        def _():
            wb[...] = w_ref[...].astype(BF16)

        mm = _dot

    @pl.when(i == 0)
    def _():
        os_ref[...] = mm(hs_ref[...], wb[...])

    @pl.when(i > 0)
    def _():
        op_ref[...] = mm(hp_ref[...], wb[...])


def _in_proj(hp, hs, w, n_out, shift_from=0, shift=0, tile0=0):
    mp, k = hp.shape
    ms = hs.shape[0]
    n_prompt = mp // IN_TM
    prow = lambda j, i: (jnp.maximum(i - 1, 0), 0)
    in_specs = [pl.BlockSpec((IN_TM, k), prow),
                pl.BlockSpec((ms, k), lambda j, i: (0, 0))]
    args = [hp, hs, w]
    if shift:
        in_specs.append(pl.BlockSpec((IN_TN, k), lambda j, i: (j + tile0, 0)))
        in_specs.append(pl.BlockSpec((shift, k), lambda j, i: ((j + tile0 + 1) * (IN_TN // shift), 0)))
        args.append(w)
        wb_shape = (IN_TN, k)
    else:
        in_specs.append(pl.BlockSpec((k, IN_TN), lambda j, i: (0, j)))
        wb_shape = (k, IN_TN)
    return pl.pallas_call(
        functools.partial(_in_proj_kernel, shift_from=shift_from, shift=shift),
        grid=(n_out // IN_TN, n_prompt + 1),
        in_specs=in_specs,
        out_specs=(pl.BlockSpec((IN_TM, IN_TN), lambda j, i: (jnp.maximum(i - 1, 0), j)),
                   pl.BlockSpec((ms, IN_TN), lambda j, i: (0, j))),
        out_shape=(jax.ShapeDtypeStruct((mp, n_out), F32),
                   jax.ShapeDtypeStruct((ms, n_out), F32)),
        scratch_shapes=[pltpu.VMEM(wb_shape, BF16)],
        compiler_params=_params(("arbitrary", "arbitrary")),
        name="in_proj",
    )(*args)


def _outproj_kernel(ya_ref, yb_ref, w_ref, x_ref, g_ref, *out_refs, final):
    half = ya_ref.shape[1]
    acc = _dot(ya_ref[...], w_ref[0:half, :]) + _dot(yb_ref[...], w_ref[half:2 * half, :])
    x1 = x_ref[...] + acc
    ms = jnp.mean(x1 * x1, axis=-1, keepdims=True)
    hn = x1 * lax.rsqrt(ms + EPS) * g_ref[...]
    if final:
        out_refs[0][...] = hn
    else:
        out_refs[0][...] = x1
        out_refs[1][...] = hn.astype(BF16)


def _outproj(ya, yb, w, x, g, tm, final):
    m, half = ya.shape
    d = w.shape[1]
    row = lambda i: (i, 0)
    if final:
        out_shape = jax.ShapeDtypeStruct((m, d), F32)
        out_specs = pl.BlockSpec((tm, d), row)
    else:
        out_shape = (jax.ShapeDtypeStruct((m, d), F32), jax.ShapeDtypeStruct((m, d), BF16))
        out_specs = (pl.BlockSpec((tm, d), row), pl.BlockSpec((tm, d), row))
    return pl.pallas_call(
        functools.partial(_outproj_kernel, final=final),
        grid=(m // tm,),
        in_specs=[pl.BlockSpec((tm, half), row),
                  pl.BlockSpec((tm, half), row),
                  pl.BlockSpec((2 * half, d), lambda i: (0, 0)),
                  pl.BlockSpec((tm, d), row),
                  pl.BlockSpec((1, d), lambda i: (0, 0))],
        out_specs=out_specs,
        out_shape=out_shape,
        compiler_params=_params(("arbitrary",)),
        name="out_proj_final" if final else "out_proj",
    )(ya, yb, w, x, g)


MIX_ROWS = 512


def _rope(x, cosf, sins):
    return x * cosf + pltpu.roll(x, DH_B // 2, 1) * sins


def _even_prompt_kernel(p_ref, cw_ref, gret_ref, cos_ref, sin_ref,
                        ya_ref, yb_ref, conv_ref, s_ref, ubuf):
    L = CHUNK

    @pl.when(pl.program_id(1) == 0)
    def _():
        ubuf[0:8, :] = jnp.zeros((8, W_A), F32)
        s_ref[...] = jnp.zeros_like(s_ref)

    def chunk(ci, carry):
        rs = pl.ds(pl.multiple_of(ci * L, L), L)

        for j in range(W_A // 128):
            sl = slice(j * 128, (j + 1) * 128)
            a_b = p_ref[rs, j * 128:(j + 1) * 128]
            a_c = p_ref[rs, W_A + j * 128:W_A + (j + 1) * 128]
            a_x = p_ref[rs, 2 * W_A + j * 128:2 * W_A + (j + 1) * 128]
            a_z = p_ref[rs, 3 * W_A + j * 128:3 * W_A + (j + 1) * 128]
            u = a_c * a_x
            ubuf[8:8 + L, sl] = u
            t0 = ubuf[6:6 + L, sl]
            t1 = ubuf[7:7 + L, sl]
            conv = cw_ref[0:1, sl] * t0 + cw_ref[1:2, sl] * t1 + cw_ref[2:3, sl] * u
            ya_ref[rs, sl] = (a_b * conv * _silu(a_z)).astype(BF16)
            ubuf[0:8, sl] = u[L - 8:L, :]

        cosf = cos_ref[rs, :]
        sins = sin_ref[rs, :]
        row = lax.broadcasted_iota(jnp.int32, (L, L), 0)
        col = lax.broadcasted_iota(jnp.int32, (L, L), 1)
        causal = row >= col
        diff = jnp.maximum(row - col, 0).astype(F32)
        ti = lax.broadcasted_iota(jnp.int32, (L, 1), 0).astype(F32)
        base = 4 * W_A
        for h in range(H_B):
            lg = LOG_GAMMA[h]
            sl = slice(h * DH_B, (h + 1) * DH_B)
            q = p_ref[rs, base + h * DH_B:base + (h + 1) * DH_B]
            k = p_ref[rs, base + W_B + h * DH_B:base + W_B + (h + 1) * DH_B]
            v = p_ref[rs, base + 2 * W_B + h * DH_B:base + 2 * W_B + (h + 1) * DH_B]
            z = p_ref[rs, base + 3 * W_B + h * DH_B:base + 3 * W_B + (h + 1) * DH_B]
            qr = _rope(q, cosf, sins)
            kr = _rope(k, cosf, sins) * (DH_B ** -0.5)
            decay = jnp.where(causal, jnp.exp(lg * diff), 0.0)
            qb = qr.astype(BF16)
            kb = kr.astype(BF16)
            vb = v.astype(BF16)
            sc = _dot_nt(qb, kb) * decay
            inner = _dot(sc.astype(BF16), vb)
            s_old = s_ref[0, h]
            cross = _dot(qb, s_old.astype(BF16)) * jnp.exp(lg * (ti + 1.0))
            kd = (kr * jnp.exp(lg * (L - 1.0 - ti))).astype(BF16)
            s_ref[0, h] = math.exp(lg * L) * s_old + _dot_tn(kd, vb)
            o = inner + cross
            yb_ref[rs, sl] = (_head_norm(o, gret_ref[0:1, sl]) * _silu(z)).astype(BF16)
        return carry

    lax.fori_loop(0, MIX_ROWS // L, chunk, 0)
    conv_ref[0] = ubuf[6:8, :]


def _even_prompt(p, conv_w, g_ret, cosf, sins):
    nc = SEQ // MIX_ROWS
    rows = lambda b, c: (b * nc + c, 0)
    const2 = lambda b, c: (0, 0)
    return pl.pallas_call(
        _even_prompt_kernel,
        grid=(BATCH, nc),
        in_specs=[pl.BlockSpec((MIX_ROWS, E_IN), rows),
                  pl.BlockSpec((CONV_W, W_A), const2),
                  pl.BlockSpec((1, W_B), const2),
                  pl.BlockSpec((MIX_ROWS, DH_B), lambda b, c: (c, 0)),
                  pl.BlockSpec((MIX_ROWS, DH_B), lambda b, c: (c, 0))],
        out_specs=(pl.BlockSpec((MIX_ROWS, W_A), rows),
                   pl.BlockSpec((MIX_ROWS, W_B), rows),
                   pl.BlockSpec((1, CONV_W - 1, W_A), lambda b, c: (b, 0, 0)),
                   pl.BlockSpec((1, H_B, DH_B, DH_B), lambda b, c: (b, 0, 0, 0))),
        out_shape=(jax.ShapeDtypeStruct((BATCH * SEQ, W_A), BF16),
                   jax.ShapeDtypeStruct((BATCH * SEQ, W_B), BF16),
                   jax.ShapeDtypeStruct((BATCH, CONV_W - 1, W_A), F32),
                   jax.ShapeDtypeStruct((BATCH, H_B, DH_B, DH_B), F32)),
        scratch_shapes=[pltpu.VMEM((CHUNK + 8, W_A), F32)],
        compiler_params=_params(("arbitrary", "arbitrary")),
        name="even_prompt",
    )(p, conv_w, g_ret, cosf, sins)


FT = 1024
FN = BATCH * SEQ // FT
FPB = SEQ // FT
HG = 2
GW = HG * 128
PCH = 2


def _chunk_pipeline(n, piece, stages, gate=None):
    sa, sb, sc, sd, se = stages
    npieces = n // PCH
    for j in range(PCH):
        piece(0, j)
    if gate is not None:
        gate(0)
    for c in range(n + 2):
        k, j = c // PCH + 1, c % PCH
        if k < npieces:
            piece(k, j)
        if c < n:
            sa(c)
        if 1 <= c <= n:
            sc(c - 1)
        if c < n:
            sb(c)
        if 1 <= c <= n:
            sd(c - 1)
        if c >= 2:
            se(c - 2)
        if gate is not None and j == PCH - 1 and k < npieces:
            gate(k)


def _even_heads_kernel(hp_ref, hs_ref, wq_ref, wk_ref, wv_ref, wz_ref, gret_ref, cos_ref, sin_ref, lg_ref,
                       yb_ref, s_ref, sq_ref, sk_ref, sv_ref, sz_ref, wb, pt, s_scr):
    s = pl.program_id(1)
    L = CHUNK

    @pl.when(s == 0)
    def _():
        for part, w_ref in enumerate((wq_ref, wk_ref, wv_ref, wz_ref)):
            wb[:, part * GW:(part + 1) * GW] = w_ref[...].astype(BF16)
        ps = _dot(hs_ref[...], wb[...])
        for part, o_ref in enumerate((sq_ref, sk_ref, sv_ref, sz_ref)):
            o_ref[...] = ps[:, part * GW:(part + 1) * GW]

    @pl.when(s > 0)
    def _():
        t = s - 1
        n = FT // L
        rows = lambda c: slice(c * L, (c + 1) * L)
        cols = lambda part, i: slice(part * GW + i * DH_B, part * GW + (i + 1) * DH_B)

        def piece(k, j):
            pr = slice(k * PCH * L, (k + 1) * PCH * L)
            pc = slice(j * 2 * GW, (j + 1) * 2 * GW)
            pt[pr, pc] = _dot(hp_ref[pr, :], wb[:, pc])

        row = lax.broadcasted_iota(jnp.int32, (L, L), 0)
        col = lax.broadcasted_iota(jnp.int32, (L, L), 1)
        causal = row >= col
        diff = jnp.maximum(row - col, 0).astype(F32)
        ti = lax.broadcasted_iota(jnp.int32, (L, 1), 0).astype(F32)
        lgs = [lg_ref[i][:, 0:1] for i in range(HG)]
        decay = [jnp.where(causal, jnp.exp(lg * diff), 0.0) for lg in lgs]
        q_decay = [jnp.exp(lg * (ti + 1.0)) for lg in lgs]
        k_decay = [jnp.exp(lg * (L - 1.0 - ti)) for lg in lgs]
        gamma_l = [jnp.exp(lg * float(L)) for lg in lgs]
        state = {0: [jnp.where(t % FPB == 0, 0.0, s_scr[i]) for i in range(HG)]}
        v = {}

        def stage_a(c):
            cosf = cos_ref[rows(c), :]
            sins = sin_ref[rows(c), :]
            v[c] = []
            for i in range(HG):
                kr = _rope(pt[rows(c), cols(1, i)], cosf, sins) * (DH_B ** -0.5)
                v[c].append(dict(qb=_rope(pt[rows(c), cols(0, i)], cosf, sins).astype(BF16),
                                 kb=kr.astype(BF16),
                                 kd=(kr * k_decay[i]).astype(BF16),
                                 vb=pt[rows(c), cols(2, i)].astype(BF16)))

        def stage_b(c):
            for i, d in enumerate(v[c]):
                d["sc"] = _dot_nt(d["qb"], d["kb"])
                d["cross"] = _dot(d["qb"], state[c][i].astype(BF16))
                d["upd"] = _dot_tn(d["kd"], d["vb"])

        def stage_c(c):
            state[c + 1] = []
            for i, d in enumerate(v[c]):
                d["sc"] = (d["sc"] * decay[i]).astype(BF16)
                state[c + 1].append(gamma_l[i] * state[c][i] + d["upd"])

        def stage_d(c):
            for d in v[c]:
                d["inner"] = _dot(d["sc"], d["vb"])

        def stage_e(c):
            for i, d in enumerate(v.pop(c)):
                o = d["inner"] + d["cross"] * q_decay[i]
                g = gret_ref[0:1, i * DH_B:(i + 1) * DH_B]
                z = pt[rows(c), cols(3, i)]
                yb_ref[rows(c), i * DH_B:(i + 1) * DH_B] = (_head_norm(o, g) * _silu(z)).astype(BF16)

        _chunk_pipeline(n, piece, (stage_a, stage_b, stage_c, stage_d, stage_e))
        for i in range(HG):
            s_scr[i] = state[n][i]
            s_ref[0, i] = state[n][i]


def _even_heads(hp, hs, w, g_ret, cosf, sins, lg_tab):
    k = hp.shape[1]
    ms = hs.shape[0]
    ng = H_B // HG
    base = 4 * W_A // GW
    tile = lambda s: jnp.maximum(s - 1, 0)
    wspec = lambda part: pl.BlockSpec((k, GW), lambda g, s: (0, base + part * ng + g))
    sspec = pl.BlockSpec((ms, GW), lambda g, s: (0, g))
    sshape = jax.ShapeDtypeStruct((ms, W_B), F32)
    return pl.pallas_call(
        _even_heads_kernel,
        grid=(ng, FN + 1),
        in_specs=[pl.BlockSpec((FT, k), lambda g, s: (tile(s), 0)),
                  pl.BlockSpec((ms, k), lambda g, s: (0, 0)),
                  wspec(0), wspec(1), wspec(2), wspec(3),
                  pl.BlockSpec((1, GW), lambda g, s: (0, g)),
                  pl.BlockSpec((FT, DH_B), lambda g, s: (tile(s) % FPB, 0)),
                  pl.BlockSpec((FT, DH_B), lambda g, s: (tile(s) % FPB, 0)),
                  pl.BlockSpec((HG, 1, 128), lambda g, s: (g, 0, 0))],
        out_specs=(pl.BlockSpec((FT, GW), lambda g, s: (tile(s), g)),
                   pl.BlockSpec((1, HG, DH_B, DH_B), lambda g, s: (tile(s) // FPB, g, 0, 0)),
                   sspec, sspec, sspec, sspec),
        out_shape=(jax.ShapeDtypeStruct((BATCH * SEQ, W_B), BF16),
                   jax.ShapeDtypeStruct((BATCH, H_B, DH_B, DH_B), F32),
                   sshape, sshape, sshape, sshape),
        scratch_shapes=[pltpu.VMEM((k, 4 * GW), BF16),
                        pltpu.VMEM((FT, 4 * GW), F32),
                        pltpu.VMEM((HG, DH_B, DH_B), F32)],
        compiler_params=_params(("arbitrary", "arbitrary")),
        name="even_heads",
    )(hp, hs, w, w, w, w, g_ret, cosf, sins, lg_tab)


def _even_conv_kernel(hp_ref, hs_ref, wb_ref, wc_ref, wx_ref, wz_ref, cw_ref,
                      ya_ref, conv_ref, sb_ref, sc_ref, sx_ref, sz_ref, wb, pt, ubuf):
    s = pl.program_id(1)
    L = CHUNK

    @pl.when(s == 0)
    def _():
        for part, w_ref in enumerate((wb_ref, wc_ref, wx_ref, wz_ref)):
            wb[:, part * GW:(part + 1) * GW] = w_ref[...].astype(BF16)
        ps = _dot(hs_ref[...], wb[...])
        for part, o_ref in enumerate((sb_ref, sc_ref, sx_ref, sz_ref)):
            o_ref[...] = ps[:, part * GW:(part + 1) * GW]

    @pl.when(s > 0)
    def _():
        t = s - 1
        n = FT // L
        rows = lambda c: slice(c * L, (c + 1) * L)
        part = lambda p, c: pt[rows(c), p * GW:(p + 1) * GW]

        @pl.when(t % FPB == 0)
        def _():
            ubuf[0:8, :] = jnp.zeros((8, GW), F32)

        @pl.when(t % FPB != 0)
        def _():
            ubuf[0:8, :] = ubuf[FT:FT + 8, :]

        def piece(k, j):
            pr = slice(k * PCH * L, (k + 1) * PCH * L)
            pc = slice(j * 2 * GW, (j + 1) * 2 * GW)
            pt[pr, pc] = _dot(hp_ref[pr, :], wb[:, pc])

        for j in range(PCH):
            piece(0, j)
        for c in range(n):
            if c // PCH + 1 < n // PCH:
                piece(c // PCH + 1, c % PCH)
            u = part(1, c) * part(2, c)
            ubuf[8 + c * L:8 + (c + 1) * L, :] = u
            t0 = ubuf[6 + c * L:6 + (c + 1) * L, :]
            t1 = ubuf[7 + c * L:7 + (c + 1) * L, :]
            conv = cw_ref[0:1, :] * t0 + cw_ref[1:2, :] * t1 + cw_ref[2:3, :] * u
            ya_ref[rows(c), :] = (part(0, c) * conv * _silu(part(3, c))).astype(BF16)
        conv_ref[0] = ubuf[FT + 6:FT + 8, :]


def _even_conv(hp, hs, w, conv_w):
    k = hp.shape[1]
    ms = hs.shape[0]
    ng = W_A // GW
    tile = lambda s: jnp.maximum(s - 1, 0)
    wspec = lambda part: pl.BlockSpec((k, GW), lambda g, s: (0, part * ng + g))
    sspec = pl.BlockSpec((ms, GW), lambda g, s: (0, g))
    sshape = jax.ShapeDtypeStruct((ms, W_A), F32)
    return pl.pallas_call(
        _even_conv_kernel,
        grid=(ng, FN + 1),
        in_specs=[pl.BlockSpec((FT, k), lambda g, s: (tile(s), 0)),
                  pl.BlockSpec((ms, k), lambda g, s: (0, 0)),
                  wspec(0), wspec(1), wspec(2), wspec(3),
                  pl.BlockSpec((CONV_W, GW), lambda g, s: (0, g))],
        out_specs=(pl.BlockSpec((FT, GW), lambda g, s: (tile(s), g)),
                   pl.BlockSpec((1, CONV_W - 1, GW), lambda g, s: (tile(s) // FPB, 0, g)),
                   sspec, sspec, sspec, sspec),
        out_shape=(jax.ShapeDtypeStruct((BATCH * SEQ, W_A), BF16),
                   jax.ShapeDtypeStruct((BATCH, CONV_W - 1, W_A), F32),
                   sshape, sshape, sshape, sshape),
        scratch_shapes=[pltpu.VMEM((k, 4 * GW), BF16),
                        pltpu.VMEM((FT, 4 * GW), F32),
                        pltpu.VMEM((FT + 8, GW), F32)],
        compiler_params=_params(("arbitrary", "arbitrary")),
        name="even_conv",
    )(hp, hs, w, w, w, w, conv_w)


SB = 32
SR = SB * DEC_SEQ


def _even_sample_kernel(ab_ref, ac_ref, ax_ref, az_ref, pq_ref, pk_ref, pv_ref, pz_ref, st_ref, s_ref,
                        cw_ref, gret_ref, cos_ref, sin_ref, lg_ref,
                        ya_ref, u_ref, yb_ref, so_ref, cross_scr):
    h = pl.program_id(1)
    row = lax.broadcasted_iota(jnp.int32, (SR, SR), 0)
    col = lax.broadcasted_iota(jnp.int32, (SR, SR), 1)
    trow = row & 3

    @pl.when(h == 0)
    def _():
        for j in range(W_A // 128):
            sl = slice(j * 128, (j + 1) * 128)
            a_b = ab_ref[:, sl]
            a_c = ac_ref[:, sl]
            a_x = ax_ref[:, sl]
            a_z = az_ref[:, sl]
            u = a_c * a_x
            e = st_ref[:, sl]
            tap1 = jnp.where(trow >= 1, pltpu.roll(u, 1, 0), pltpu.roll(e, SR - 1, 0))
            tap0 = jnp.where(trow >= 2, pltpu.roll(u, 2, 0), e)
            conv = cw_ref[0:1, sl] * tap0 + cw_ref[1:2, sl] * tap1 + cw_ref[2:3, sl] * u
            ya_ref[:, sl] = (a_b * conv * _silu(a_z)).astype(BF16)
            u_ref[:, sl] = u

    lg = lg_ref[0][:, 0:1]
    same = (row >> 2) == (col >> 2)
    dd = trow - (col & 3)
    mask = jnp.where(same, dd, -1) >= 0
    decay = jnp.where(mask, jnp.exp(lg * jnp.maximum(dd, 0).astype(F32)), 0.0)
    tcol = (lax.broadcasted_iota(jnp.int32, (SR, 1), 0) & 3).astype(F32)
    cosf = cos_ref[...]
    sins = sin_ref[...]
    qr = _rope(pq_ref[...], cosf, sins)
    kr = _rope(pk_ref[...], cosf, sins) * (DH_B ** -0.5)
    qb = qr.astype(BF16)
    kb = kr.astype(BF16)
    vb = pv_ref[...].astype(BF16)
    sc = _dot_nt(qb, kb) * decay
    inner = _dot(sc.astype(BF16), vb)
    kdt = (kr * jnp.exp(lg * (DEC_SEQ - 1.0 - tcol))).T
    gamma_l = jnp.exp(lg * float(DEC_SEQ))
    lane_b = col >> 2
    sub = lax.broadcasted_iota(jnp.int32, (8, DH_B), 0)
    for g in range(SR // 8):
        q8 = qb[8 * g:8 * g + 8, :]
        res = []
        for beta in range(2):
            b = 2 * g + beta
            s_old = s_ref[b, 0]
            res.append(_dot(q8, s_old.astype(BF16)))
            lhs = jnp.where(lane_b == b, kdt, 0.0).astype(BF16)
            so_ref[b, 0] = gamma_l * s_old + _dot(lhs, vb)
        cross_scr[8 * g:8 * g + 8, :] = jnp.where(sub < DEC_SEQ, res[0], res[1])
    o = inner + cross_scr[...] * jnp.exp(lg * (tcol + 1.0))
    yb_ref[...] = (_head_norm(o, gret_ref[...]) * _silu(pz_ref[...])).astype(BF16)


def _even_sample(pa, pb, st_exp, s_state, conv_w, g_ret, cosf, sins, lg_tab):
    nb = DEC_BATCH // SB
    const2 = lambda i, h: (0, 0)
    aspec = pl.BlockSpec((SR, W_A), lambda i, h: (i, 0))
    hspec = pl.BlockSpec((SR, DH_B), lambda i, h: (i, h))
    return pl.pallas_call(
        _even_sample_kernel,
        grid=(nb, H_B),
        in_specs=[aspec, aspec, aspec, aspec,
                  hspec, hspec, hspec, hspec,
                  pl.BlockSpec((SR, W_A), lambda i, h: (i, 0)),
                  pl.BlockSpec((SB, 1, DH_B, DH_B), lambda i, h: (i, h, 0, 0)),
                  pl.BlockSpec((CONV_W, W_A), const2),
                  pl.BlockSpec((1, DH_B), lambda i, h: (0, h)),
                  pl.BlockSpec((SR, DH_B), const2),
                  pl.BlockSpec((SR, DH_B), const2),
                  pl.BlockSpec((1, 1, 128), lambda i, h: (h, 0, 0))],
        out_specs=(pl.BlockSpec((SR, W_A), lambda i, h: (i, 0)),
                   pl.BlockSpec((SR, W_A), lambda i, h: (i, 0)),
                   pl.BlockSpec((SR, DH_B), lambda i, h: (i, h)),
                   pl.BlockSpec((SB, 1, DH_B, DH_B), lambda i, h: (i, h, 0, 0))),
        out_shape=(jax.ShapeDtypeStruct((DEC_BATCH * DEC_SEQ, W_A), BF16),
                   jax.ShapeDtypeStruct((DEC_BATCH * DEC_SEQ, W_A), F32),
                   jax.ShapeDtypeStruct((DEC_BATCH * DEC_SEQ, W_B), BF16),
                   jax.ShapeDtypeStruct((DEC_BATCH, H_B, DH_B, DH_B), F32)),
        scratch_shapes=[pltpu.VMEM((SR, DH_B), F32)],
        compiler_params=_params(("arbitrary", "arbitrary")),
        name="even_sample",
    )(*pa, *pb, st_exp, s_state, conv_w, g_ret, cosf, sins, lg_tab)


def _odd_prompt_kernel(p_ref, h_ref, wg_ref, bg_ref, gm_ref, lng_ref, lnb_ref, ws_ref, bst_ref,
                       yc_ref, yd_ref, c_ref, n_ref, m_ref, wgb, wsb):
    L = CHUNK

    @pl.when(jnp.logical_and(pl.program_id(0) == 0, pl.program_id(1) == 0))
    def _():
        wgb[...] = wg_ref[...].astype(BF16)
        keep = (lax.broadcasted_iota(jnp.int32, (L, L), 0) >= lax.broadcasted_iota(jnp.int32, (L, L), 1))
        for g in range(G_D):
            wsb[g] = jnp.where(keep, ws_ref[g], 0.0).astype(BF16)

    @pl.when(pl.program_id(1) == 0)
    def _():
        c_ref[...] = jnp.zeros_like(c_ref)
        n_ref[...] = jnp.zeros_like(n_ref)
        m_ref[...] = jnp.zeros_like(m_ref)

    def chunk(ci, carry):
        rs = pl.ds(pl.multiple_of(ci * L, L), L)
        row = lax.broadcasted_iota(jnp.int32, (L, L), 0)
        col = lax.broadcasted_iota(jnp.int32, (L, L), 1)
        tri = row >= col

        pre = _dot_nt(h_ref[rs, :], wgb[...]) + bg_ref[...]
        lf = _log_sigmoid(pre)
        b_c = _dot_hi(jnp.where(tri, 1.0, 0.0), lf)
        b_r = b_c.T
        pre_r = pre.T
        for h in range(H_C):
            bc = b_c[:, H_C + h:H_C + h + 1]
            br = b_r[H_C + h:H_C + h + 1, :]
            igr = pre_r[h:h + 1, :]
            igc = pre[:, h:h + 1]
            m_prev = m_ref[0, h:h + 1, 0:1]
            log_d = jnp.where(tri, bc - br + igr, NEG_INF)
            log_inter = bc + m_prev
            m_t = jnp.maximum(log_inter, jnp.max(log_d, axis=-1, keepdims=True))
            w = jnp.exp(log_d - m_t)
            w_inter = jnp.exp(log_inter - m_t)
            q = p_ref[rs, h * DQK_C:(h + 1) * DQK_C] * (DQK_C ** -0.5)
            k = p_ref[rs, H_C * DQK_C + h * DQK_C:H_C * DQK_C + (h + 1) * DQK_C]
            v = p_ref[rs, 2 * H_C * DQK_C + h * DV_C:2 * H_C * DQK_C + (h + 1) * DV_C]
            z = p_ref[rs, O_GATE - W_C + h * DV_C:O_GATE - W_C + (h + 1) * DV_C]
            qb = q.astype(BF16)
            kb = k.astype(BF16)
            vb = v.astype(BF16)
            sc = _dot_nt(qb, kb) * w
            c_old = c_ref[0, h]
            n_old = n_ref[0, h:h + 1, :]
            num = _dot(sc.astype(BF16), vb) + w_inter * _dot_nt(qb, c_old.astype(BF16))
            den = jnp.sum(sc, axis=-1, keepdims=True) + w_inter * jnp.sum(q * n_old, axis=-1, keepdims=True)
            hh = num / jnp.maximum(jnp.abs(den), jnp.exp(-m_t))
            m_new = m_t[L - 1:L, :]
            b_last = bc[L - 1:L, :]
            w_end = jnp.exp(b_last - bc + igc - m_new)
            cd = jnp.exp(b_last + m_prev - m_new)
            c_ref[0, h] = cd * c_old + _dot_tn((v * w_end).astype(BF16), kb)
            n_ref[0, h:h + 1, :] = cd * n_old + jnp.sum(w_end * k, axis=0, keepdims=True)
            m_ref[0, h:h + 1, :] = jnp.broadcast_to(m_new, (1, 128))
            sl = slice(h * DV_C, (h + 1) * DV_C)
            yc_ref[rs, sl] = (_head_norm(hh, gm_ref[0:1, sl]) * _silu(z)).astype(BF16)

        dv = lambda g: p_ref[rs, O_GATE + W_D + g * 128:O_GATE + W_D + (g + 1) * 128]
        tot = dv(0)
        for g in range(1, G_D):
            tot = tot + dv(g)
        mu = jnp.sum(tot, axis=-1, keepdims=True) * (1.0 / W_D)
        sq = (dv(0) - mu) * (dv(0) - mu)
        for g in range(1, G_D):
            sq = sq + (dv(g) - mu) * (dv(g) - mu)
        rstd = lax.rsqrt(jnp.sum(sq, axis=-1, keepdims=True) * (1.0 / W_D) + EPS)
        for g in range(G_D):
            sl = slice(g * 128, (g + 1) * 128)
            vn = (dv(g) - mu) * rstd * lng_ref[0:1, sl] + lnb_ref[0:1, sl]
            s = _dot(wsb[g], vn.astype(BF16)) + bst_ref[:, g:g + 1]
            d_u = p_ref[rs, O_GATE + g * 128:O_GATE + (g + 1) * 128]
            d_z = p_ref[rs, O_GATE + 2 * W_D + g * 128:O_GATE + 2 * W_D + (g + 1) * 128]
            yd_ref[rs, sl] = (d_u * s * _silu(d_z)).astype(BF16)
        return carry

    lax.fori_loop(0, MIX_ROWS // L, chunk, 0)


def _odd_prompt(p, h, w_o, bg, gm, lng, lnb, ws, bst):
    nc = SEQ // MIX_ROWS
    rows = lambda b, c: (b * nc + c, 0)
    const2 = lambda b, c: (0, 0)
    return pl.pallas_call(
        _odd_prompt_kernel,
        grid=(BATCH, nc),
        in_specs=[pl.BlockSpec((MIX_ROWS, O_N), rows),
                  pl.BlockSpec((MIX_ROWS, D_MODEL), rows),
                  pl.BlockSpec((128, D_MODEL), lambda b, c: (O_GATE // 128, 0)),
                  pl.BlockSpec((1, 128), const2),
                  pl.BlockSpec((1, W_C), const2),
                  pl.BlockSpec((1, W_D), const2),
                  pl.BlockSpec((1, W_D), const2),
                  pl.BlockSpec((G_D, CHUNK, CHUNK), lambda b, c: (0, 0, 0)),
                  pl.BlockSpec((CHUNK, G_D), const2)],
        out_specs=(pl.BlockSpec((MIX_ROWS, W_C), rows),
                   pl.BlockSpec((MIX_ROWS, W_D), rows),
                   pl.BlockSpec((1, H_C, DV_C, DQK_C), lambda b, c: (b, 0, 0, 0)),
                   pl.BlockSpec((1, H_C, DQK_C), lambda b, c: (b, 0, 0)),
                   pl.BlockSpec((1, 8, 128), lambda b, c: (b, 0, 0))),
        out_shape=(jax.ShapeDtypeStruct((BATCH * SEQ, W_C), BF16),
                   jax.ShapeDtypeStruct((BATCH * SEQ, W_D), BF16),
                   jax.ShapeDtypeStruct((BATCH, H_C, DV_C, DQK_C), F32),
                   jax.ShapeDtypeStruct((BATCH, H_C, DQK_C), F32),
                   jax.ShapeDtypeStruct((BATCH, 8, 128), F32)),
        scratch_shapes=[pltpu.VMEM((128, D_MODEL), BF16),
                        pltpu.VMEM((G_D, CHUNK, CHUNK), BF16)],
        compiler_params=_params(("arbitrary", "arbitrary")),
        name="odd_prompt",
    )(p, h, w_o, bg, gm, lng, lnb, ws, bst)


CG = 2
CW = 128 + CG * (2 * DQK_C + 2 * DV_C)
CT = 1024
CN = BATCH * SEQ // CT
CPB = SEQ // CT


def _odd_heads_kernel(hp_ref, hs_ref, wq_ref, wk_ref, wv_ref, wz_ref, wg_ref, bg_ref, gm_ref,
                      yc_ref, c_ref, n_ref, m_ref, sq_ref, sk_ref, sv_ref, sz_ref,
                      wb, pt, c_scr, n_scr, m_scr, gt_scr):
    grp = pl.program_id(0)
    s = pl.program_id(1)
    L = CHUNK
    gc = slice(0, 128)
    qc = slice(128, 128 + CG * DQK_C)
    kc = slice(qc.stop, qc.stop + CG * DQK_C)
    vc = slice(kc.stop, kc.stop + CG * DV_C)
    zc = slice(vc.stop, vc.stop + CG * DV_C)
    head = lambda sl, i, w: slice(sl.start + i * w, sl.start + (i + 1) * w)

    @pl.when(s == 0)
    def _():
        wb[qc, :] = wq_ref[...].astype(BF16)
        wb[kc, :] = wk_ref[...].astype(BF16)
        wb[vc, :] = wv_ref[...].astype(BF16)
        wb[zc, :] = wz_ref[...].astype(BF16)
        wb[gc, :] = wg_ref[...].astype(BF16)
        ps = _dot_nt(hs_ref[...], wb[qc.start:CW, :])
        off = lambda sl: slice(sl.start - qc.start, sl.stop - qc.start)
        sq_ref[...] = ps[:, off(qc)]
        sk_ref[...] = ps[:, off(kc)]
        sv_ref[...] = ps[:, off(vc)]
        sz_ref[...] = ps[:, off(zc)]

    @pl.when(s > 0)
    def _():
        t = s - 1
        n = CT // L
        rows = lambda c: slice(c * L, (c + 1) * L)

        def piece(k, j):
            pr = slice(k * PCH * L, (k + 1) * PCH * L)
            pc = (slice(0, vc.start), slice(vc.start, CW))[j]
            pt[pr, pc] = _dot_nt(hp_ref[pr, :], wb[pc, :])

        row = lax.broadcasted_iota(jnp.int32, (L, L), 0)
        col = lax.broadcasted_iota(jnp.int32, (L, L), 1)
        tri = row >= col
        fresh = t % CPB == 0
        cst = {0: [jnp.where(fresh, 0.0, c_scr[i]) for i in range(CG)]}
        nst = {0: [jnp.where(fresh, 0.0, n_scr[i]) for i in range(CG)]}
        mst = {0: [jnp.where(fresh, 0.0, m_scr[i, 0:1, 0:1]) for i in range(CG)]}
        v = {}
        gates = {}

        def gate(k):
            cs = range(k * PCH, (k + 1) * PCH)
            for c in cs:
                gt_scr[c] = (pt[rows(c), gc] + bg_ref[...]).T
            pad = jnp.zeros((8 - CG * PCH, L), F32)
            ig_rows = jnp.concatenate(
                [gt_scr[c, pl.ds(grp * CG + i, 1), :] for i in range(CG) for c in cs] + [pad], axis=0)
            lf_rows = jnp.concatenate(
                [_log_sigmoid(gt_scr[c, pl.ds(grp * CG + i + H_C, 1), :]) for i in range(CG) for c in cs]
                + [pad], axis=0)
            b_rows = _dot_hi(lf_rows, jnp.where(row <= col, 1.0, 0.0))
            tall = jnp.zeros((L - 8, L), F32)
            gates[k] = dict(ig_rows=ig_rows, b_rows=b_rows,
                            b_cols=jnp.concatenate([b_rows, tall], axis=0).T,
                            ig_cols=jnp.concatenate([ig_rows, tall], axis=0).T)

        def stage_a(c):
            gk = gates[c // PCH]
            v[c] = []
            nst[c + 1] = []
            mst[c + 1] = []
            for i in range(CG):
                r = i * PCH + c % PCH
                b_r = gk["b_rows"][r:r + 1, :]
                ig_r = gk["ig_rows"][r:r + 1, :]
                b_c = gk["b_cols"][:, r:r + 1]
                ig_c = gk["ig_cols"][:, r:r + 1]
                m_prev = mst[c][i]
                log_d = jnp.where(tri, b_c - b_r + ig_r, NEG_INF)
                log_inter = b_c + m_prev
                m_t = jnp.maximum(log_inter, jnp.max(log_d, axis=-1, keepdims=True))
                m_new = m_t[L - 1:L, :]
                b_last = b_c[L - 1:L, :]
                w_end = jnp.exp(b_last - b_c + ig_c - m_new)
                cd = jnp.exp(b_last + m_prev - m_new)
                q = pt[rows(c), head(qc, i, DQK_C)] * (DQK_C ** -0.5)
                k = pt[rows(c), head(kc, i, DQK_C)]
                vv = pt[rows(c), head(vc, i, DV_C)]
                nst[c + 1].append(cd * nst[c][i] + jnp.sum(w_end * k, axis=0, keepdims=True))
                mst[c + 1].append(m_new)
                v[c].append(dict(w=jnp.exp(log_d - m_t), w_inter=jnp.exp(log_inter - m_t),
                                 floor=jnp.exp(-m_t), cd=cd, qb=q.astype(BF16), kb=k.astype(BF16),
                                 vb=vv.astype(BF16), vw=(vv * w_end).astype(BF16),
                                 qn=jnp.sum(q * nst[c][i], axis=-1, keepdims=True)))

        def stage_b(c):
            for i, d in enumerate(v[c]):
                d["sc"] = _dot_nt(d["qb"], d["kb"])
                d["upd"] = _dot_tn(d["vw"], d["kb"])
            for i, d in enumerate(v[c]):
                d["inter"] = _dot_nt(d["qb"], cst[c][i].astype(BF16))

        def stage_c(c):
            cst[c + 1] = []
            for i, d in enumerate(v[c]):
                sc = d["sc"] * d["w"]
                d["den"] = jnp.sum(sc, axis=-1, keepdims=True) + d["w_inter"] * d["qn"]
                d["sc"] = sc.astype(BF16)
                cst[c + 1].append(d["cd"] * cst[c][i] + d["upd"])

        def stage_d(c):
            for d in v[c]:
                d["num"] = _dot(d["sc"], d["vb"])

        def stage_e(c):
            for i, d in enumerate(v.pop(c)):
                num = d["num"] + d["w_inter"] * d["inter"]
                hh = num / jnp.maximum(jnp.abs(d["den"]), d["floor"])
                z = pt[rows(c), head(zc, i, DV_C)]
                ys = slice(i * DV_C, (i + 1) * DV_C)
                yc_ref[rows(c), ys] = (_head_norm(hh, gm_ref[0:1, ys]) * _silu(z)).astype(BF16)

        _chunk_pipeline(n, piece, (stage_a, stage_b, stage_c, stage_d, stage_e), gate)
        for i in range(CG):
            c_scr[i] = cst[n][i]
            n_scr[i] = nst[n][i]
            m_scr[i] = jnp.broadcast_to(mst[n][i], (8, 128))
            c_ref[0, i] = cst[n][i]
            n_ref[0, i] = nst[n][i]
            m_ref[0, i] = jnp.broadcast_to(mst[n][i], (1, 128))


def _odd_heads(hp, hs, w_t, bg, gm):
    k = hp.shape[1]
    ms = hs.shape[0]
    tile = lambda s: jnp.maximum(s - 1, 0)
    qw, vw = CG * DQK_C, CG * DV_C
    koff = H_C * DQK_C // qw
    voff = 2 * H_C * DQK_C // vw
    zoff = (2 * H_C * DQK_C + W_C) // vw
    seq = lambda g, s: (tile(s) // CPB, g, 0, 0)
    once = pl.Buffered(1)
    return pl.pallas_call(
        _odd_heads_kernel,
        grid=(H_C // CG, CN + 1),
        in_specs=[pl.BlockSpec((CT, k), lambda g, s: (tile(s), 0)),
                  pl.BlockSpec((ms, k), lambda g, s: (0, 0), pipeline_mode=once),
                  pl.BlockSpec((qw, k), lambda g, s: (g, 0), pipeline_mode=once),
                  pl.BlockSpec((qw, k), lambda g, s: (koff + g, 0), pipeline_mode=once),
                  pl.BlockSpec((vw, k), lambda g, s: (voff + g, 0), pipeline_mode=once),
                  pl.BlockSpec((vw, k), lambda g, s: (zoff + g, 0), pipeline_mode=once),
                  pl.BlockSpec((128, k), lambda g, s: (O_GATE // 128, 0), pipeline_mode=once),
                  pl.BlockSpec((1, 128), lambda g, s: (0, 0)),
                  pl.BlockSpec((1, vw), lambda g, s: (0, g))],
        out_specs=(pl.BlockSpec((CT, vw), lambda g, s: (tile(s), g)),
                   pl.BlockSpec((1, CG, DV_C, DQK_C), seq),
                   pl.BlockSpec((1, CG, 1, DQK_C), seq),
                   pl.BlockSpec((1, CG, 1, 128), seq),
                   pl.BlockSpec((ms, qw), lambda g, s: (0, g)),
                   pl.BlockSpec((ms, qw), lambda g, s: (0, g)),
                   pl.BlockSpec((ms, vw), lambda g, s: (0, g)),
                   pl.BlockSpec((ms, vw), lambda g, s: (0, g))),
        out_shape=(jax.ShapeDtypeStruct((BATCH * SEQ, W_C), BF16),
                   jax.ShapeDtypeStruct((BATCH, H_C, DV_C, DQK_C), F32),
                   jax.ShapeDtypeStruct((BATCH, H_C, 1, DQK_C), F32),
                   jax.ShapeDtypeStruct((BATCH, H_C, 1, 128), F32),
                   jax.ShapeDtypeStruct((ms, H_C * DQK_C), F32),
                   jax.ShapeDtypeStruct((ms, H_C * DQK_C), F32),
                   jax.ShapeDtypeStruct((ms, W_C), F32),
                   jax.ShapeDtypeStruct((ms, W_C), F32)),
        scratch_shapes=[pltpu.VMEM((CW, k), BF16),
                        pltpu.VMEM((CT, CW), F32),
                        pltpu.VMEM((CG, DV_C, DQK_C), F32),
                        pltpu.VMEM((CG, 1, DQK_C), F32),
                        pltpu.VMEM((CG, 8, 128), F32),
                        pltpu.VMEM((CT // CHUNK, CHUNK, CHUNK), F32)],
        compiler_params=_params(("arbitrary", "arbitrary")),
        name="odd_heads",
    )(hp, hs, w_t, w_t, w_t, w_t, w_t, bg, gm)


def _odd_mlp_kernel(p_ref, lng_ref, lnb_ref, ws_ref, bst_ref, yd_ref, wsb):
    L = CHUNK

    @pl.when(pl.program_id(0) == 0)
    def _():
        keep = (lax.broadcasted_iota(jnp.int32, (L, L), 0) >= lax.broadcasted_iota(jnp.int32, (L, L), 1))
        for g in range(G_D):
            wsb[g] = jnp.where(keep, ws_ref[g], 0.0).astype(BF16)

    def chunk(ci, carry):
        rs = pl.ds(pl.multiple_of(ci * L, L), L)
        dv = lambda g: p_ref[rs, W_D + g * 128:W_D + (g + 1) * 128]
        tot = dv(0)
        for g in range(1, G_D):
            tot = tot + dv(g)
        mu = jnp.sum(tot, axis=-1, keepdims=True) * (1.0 / W_D)
        sq = (dv(0) - mu) * (dv(0) - mu)
        for g in range(1, G_D):
            sq = sq + (dv(g) - mu) * (dv(g) - mu)
        rstd = lax.rsqrt(jnp.sum(sq, axis=-1, keepdims=True) * (1.0 / W_D) + EPS)
        for g in range(G_D):
            sl = slice(g * 128, (g + 1) * 128)
            vn = (dv(g) - mu) * rstd * lng_ref[0:1, sl] + lnb_ref[0:1, sl]
            s = _dot(wsb[g], vn.astype(BF16)) + bst_ref[:, g:g + 1]
            d_u = p_ref[rs, g * 128:(g + 1) * 128]
            d_z = p_ref[rs, 2 * W_D + g * 128:2 * W_D + (g + 1) * 128]
            yd_ref[rs, sl] = (d_u * s * _silu(d_z)).astype(BF16)
        return carry

    lax.fori_loop(0, p_ref.shape[0] // L, chunk, 0)


def _odd_mlp(p, lng, lnb, ws, bst, rows_per_step):
    m = p.shape[0]
    rows = lambda i: (i, 0)
    const2 = lambda i: (0, 0)
    return pl.pallas_call(
        _odd_mlp_kernel,
        grid=(m // rows_per_step,),
        in_specs=[pl.BlockSpec((rows_per_step, 3 * W_D), rows),
                  pl.BlockSpec((1, W_D), const2),
                  pl.BlockSpec((1, W_D), const2),
                  pl.BlockSpec((G_D, CHUNK, CHUNK), lambda i: (0, 0, 0)),
                  pl.BlockSpec((CHUNK, G_D), const2)],
        out_specs=pl.BlockSpec((rows_per_step, W_D), rows),
        out_shape=jax.ShapeDtypeStruct((m, W_D), BF16),
        scratch_shapes=[pltpu.VMEM((G_D, CHUNK, CHUNK), BF16)],
        compiler_params=_params(("arbitrary",)),
        name="odd_mlp",
    )(p, lng, lnb, ws, bst)


def _odd_sample_kernel(pq_ref, pk_ref, pv_ref, pz_ref, h_ref, wg_ref, pd_ref,
                       c_ref, nrow_ref, mrow_ref, bg_ref, gm_ref, lng_ref, lnb_ref,
                       rtab_ref, btab_ref,
                       yc_ref, yd_ref, vn_ref, co_ref, no_ref, mo_ref,
                       inter_scr):
    h = pl.program_id(1)
    row = lax.broadcasted_iota(jnp.int32, (SR, SR), 0)
    col = lax.broadcasted_iota(jnp.int32, (SR, SR), 1)
    trow = row & 3

    @pl.when(h == 0)
    def _():
        dv = pd_ref[:, W_D:2 * W_D]
        mu = jnp.mean(dv, axis=-1, keepdims=True)
        xc = dv - mu
        var = jnp.mean(xc * xc, axis=-1, keepdims=True)
        rstd = lax.rsqrt(var + EPS)
        for g in range(G_D):
            sl = slice(g * 128, (g + 1) * 128)
            vn = xc[:, sl] * rstd * lng_ref[0:1, sl] + lnb_ref[0:1, sl]
            vn_ref[:, sl] = vn
            s = rtab_ref[0, :, sl] * vn + btab_ref[:, sl]
            for j in range(1, DEC_SEQ):
                s = s + jnp.where(trow >= j, rtab_ref[j, :, sl] * pltpu.roll(vn, j, 0), 0.0)
            d_u = pd_ref[:, g * 128:(g + 1) * 128]
            d_z = pd_ref[:, 2 * W_D + g * 128:2 * W_D + (g + 1) * 128]
            yd_ref[:, sl] = (d_u * s * _silu(d_z)).astype(BF16)

    same = (row >> 2) == (col >> 2)
    mask = jnp.where(same, trow - (col & 3), -1) >= 0
    pre = _dot_nt(h_ref[...], wg_ref[...].astype(BF16)) + bg_ref[...]
    lf = _log_sigmoid(pre)
    b_full = _dot_hi(jnp.where(mask, 1.0, 0.0), lf)
    sel_i = col == h
    sel_f = col == h + H_C
    ig_c = jnp.sum(jnp.where(sel_i, pre, 0.0), axis=-1, keepdims=True)
    b_c = jnp.sum(jnp.where(sel_f, b_full, 0.0), axis=-1, keepdims=True)
    sel_ir = row == h
    sel_fr = row == h + H_C
    ig_r = jnp.sum(jnp.where(sel_ir, pre.T, 0.0), axis=0, keepdims=True)
    b_r = jnp.sum(jnp.where(sel_fr, b_full.T, 0.0), axis=0, keepdims=True)
    m_prev = mrow_ref[0]
    log_d = jnp.where(mask, b_c - b_r + ig_r, NEG_INF)
    log_inter = b_c + m_prev
    m_t = jnp.maximum(log_inter, jnp.max(log_d, axis=-1, keepdims=True))
    w = jnp.exp(log_d - m_t)
    w_inter = jnp.exp(log_inter - m_t)
    q = pq_ref[...] * (DQK_C ** -0.5)
    k = pk_ref[...]
    v = pv_ref[...]
    qb = q.astype(BF16)
    kb = k.astype(BF16)
    vb = v.astype(BF16)
    sc = _dot_nt(qb, kb) * w
    sub = lax.broadcasted_iota(jnp.int32, (8, DV_C), 0)
    for g in range(SR // 8):
        q8 = qb[8 * g:8 * g + 8, :]
        r0 = _dot_nt(q8, c_ref[2 * g, 0].astype(BF16))
        r1 = _dot_nt(q8, c_ref[2 * g + 1, 0].astype(BF16))
        inter_scr[8 * g:8 * g + 8, :] = jnp.where(sub < DEC_SEQ, r0, r1)
    n_rows = nrow_ref[0]
    num = _dot(sc.astype(BF16), vb) + w_inter * inter_scr[...]
    den = jnp.sum(sc, axis=-1, keepdims=True) + w_inter * jnp.sum(q * n_rows, axis=-1, keepdims=True)
    hh = num / jnp.maximum(jnp.abs(den), jnp.exp(-m_t))
    yc_ref[...] = (_head_norm(hh, gm_ref[...]) * _silu(pz_ref[...])).astype(BF16)

    stats = jnp.where(col == 0, m_t, jnp.where(col == 1, b_c, 0.0))
    last = _dot_hi(jnp.where(col == (row | 3), 1.0, 0.0), stats)
    m_new = last[:, 0:1]
    b_last = last[:, 1:2]
    w_end = jnp.exp(b_last - b_c + ig_c - m_new)
    cd = jnp.exp(b_last + m_prev - m_new)
    mo_ref[0] = m_new
    no_ref[0] = cd * n_rows + _dot_hi(jnp.where(same, 1.0, 0.0), w_end * k)
    vwt = (v * w_end).T
    lane_b = lax.broadcasted_iota(jnp.int32, (DV_C, SR), 1) >> 2
    for b in range(SB):
        lhs = jnp.where(lane_b == b, vwt, 0.0).astype(BF16)
        cd_b = cd[4 * b + 3:4 * b + 4, :]
        co_ref[b, 0] = cd_b * c_ref[b, 0] + _dot(lhs, kb)


def _odd_sample(pc, pd, h, w_o, c_state, n_rows, m_rows, bg, gm, lng, lnb, rtab, btab):
    nb = DEC_BATCH // SB
    const2 = lambda i, h: (0, 0)
    return pl.pallas_call(
        _odd_sample_kernel,
        grid=(nb, H_C),
        in_specs=[pl.BlockSpec((SR, DQK_C), lambda i, h: (i, h)),
                  pl.BlockSpec((SR, DQK_C), lambda i, h: (i, h)),
                  pl.BlockSpec((SR, DV_C), lambda i, h: (i, h)),
                  pl.BlockSpec((SR, DV_C), lambda i, h: (i, h)),
                  pl.BlockSpec((SR, D_MODEL), lambda i, h: (i, 0)),
                  pl.BlockSpec((128, D_MODEL), lambda i, h: (O_GATE // 128, 0)),
                  pl.BlockSpec((SR, 3 * W_D), lambda i, h: (i, 0)),
                  pl.BlockSpec((SB, 1, DV_C, DQK_C), lambda i, h: (i, h, 0, 0)),
                  pl.BlockSpec((1, SR, DQK_C), lambda i, h: (h, i, 0)),
                  pl.BlockSpec((1, SR, 1), lambda i, h: (h, i, 0)),
                  pl.BlockSpec((1, 128), const2),
                  pl.BlockSpec((1, DV_C), lambda i, h: (0, h)),
                  pl.BlockSpec((1, W_D), const2),
                  pl.BlockSpec((1, W_D), const2),
                  pl.BlockSpec((DEC_SEQ, SR, W_D), lambda i, h: (0, 0, 0)),
                  pl.BlockSpec((SR, W_D), const2)],
        out_specs=(pl.BlockSpec((SR, DV_C), lambda i, h: (i, h)),
                   pl.BlockSpec((SR, W_D), lambda i, h: (i, 0)),
                   pl.BlockSpec((SR, W_D), lambda i, h: (i, 0)),
                   pl.BlockSpec((SB, 1, DV_C, DQK_C), lambda i, h: (i, h, 0, 0)),
                   pl.BlockSpec((1, SR, DQK_C), lambda i, h: (h, i, 0)),
                   pl.BlockSpec((1, SR, 1), lambda i, h: (h, i, 0))),
        out_shape=(jax.ShapeDtypeStruct((DEC_BATCH * DEC_SEQ, W_C), BF16),
                   jax.ShapeDtypeStruct((DEC_BATCH * DEC_SEQ, W_D), BF16),
                   jax.ShapeDtypeStruct((DEC_BATCH * DEC_SEQ, W_D), F32),
                   jax.ShapeDtypeStruct((DEC_BATCH, H_C, DV_C, DQK_C), F32),
                   jax.ShapeDtypeStruct((H_C, DEC_BATCH * DEC_SEQ, DQK_C), F32),
                   jax.ShapeDtypeStruct((H_C, DEC_BATCH * DEC_SEQ, 1), F32)),
        scratch_shapes=[pltpu.VMEM((SR, DV_C), F32)],
        compiler_params=_params(("arbitrary", "arbitrary")),
        name="odd_sample",
    )(*pc, h, w_o, pd, c_state, n_rows, m_rows, bg, gm, lng, lnb, rtab, btab)


def _rope_tables(pos):
    inv = ROPE_BASE ** (-jnp.arange(0, DH_B, 2, dtype=F32) / DH_B)
    ang = pos.astype(F32)[:, None] * inv[None, :]
    cos = jnp.cos(ang)
    sin = jnp.sin(ang)
    return jnp.concatenate([cos, cos], axis=-1), jnp.concatenate([-sin, sin], axis=-1)


def kernel(x_prompt, x_sample, state_conv, state_ret, state_mlstm_C, state_mlstm_n, state_mlstm_m,
           norm_even, w_in_even, conv_w, ret_norm, w_out_even,
           norm_odd, w_in_odd, b_gate_odd, mlstm_norm, ln_v_g, ln_v_b,
           w_spatial, b_spatial, w_out_odd, norm_final):
    w_in_e = w_in_even[0]
    w_out_e = w_out_even[0].astype(BF16)
    w_o = w_in_odd[0].T
    w_out_o = w_out_odd[0].astype(BF16)
    g_even = norm_even[0][None, :]
    g_odd = norm_odd[0][None, :]
    g_fin = norm_final[None, :]
    cw = conv_w[0]
    g_ret = ret_norm[0][None, :]
    bg = jnp.concatenate([b_gate_odd[0], jnp.zeros((128 - 2 * H_C,), F32)])[None, :]
    gm = mlstm_norm[0][None, :]
    lng = ln_v_g[0][None, :]
    lnb = ln_v_b[0][None, :]
    ws = w_spatial[0]
    bst = b_spatial[0].T

    cos_p, sin_p = _rope_tables(jnp.arange(SEQ, dtype=jnp.int32))
    cos_s, sin_s = _rope_tables(PAST_LEN + jnp.arange(DEC_SEQ, dtype=jnp.int32))
    cos_s = jnp.tile(cos_s, (SB, 1))
    sin_s = jnp.tile(sin_s, (SB, 1))
    lg_tab = jnp.broadcast_to(jnp.asarray(LOG_GAMMA, F32)[:, None, None], (H_B, 1, 128))

    ws4 = ws[:, :DEC_SEQ, :DEC_SEQ]
    t_idx = jnp.arange(DEC_SEQ)
    rtab = []
    for j in range(DEC_SEQ):
        coef = ws4[:, t_idx, (t_idx - j) % DEC_SEQ]
        tab = jnp.repeat(coef.T[:, :, None], 128, axis=2).reshape(DEC_SEQ, W_D)
        rtab.append(jnp.tile(tab, (SB, 1)))
    rtab = jnp.stack(rtab)
    btab = jnp.tile(jnp.repeat(b_spatial[0][:, :DEC_SEQ].T[:, :, None], 128, axis=2)
                    .reshape(DEC_SEQ, W_D), (SB, 1))

    rs = DEC_BATCH * DEC_SEQ
    xp = x_prompt.reshape(BATCH * SEQ, D_MODEL)
    xs = x_sample.reshape(rs, D_MODEL)
    hp = _norm_cast(xp, g_even, 512)
    hs = _norm_cast(xs, g_even, 512)
    ya, conv_p, *ps_a = _even_conv(hp, hs, w_in_e, cw)
    yb, ret_p, *ps_b = _even_heads(hp, hs, w_in_e, g_ret, cos_p, sin_p, lg_tab)
    st_exp = jnp.pad(state_conv[0], ((0, 0), (0, DEC_SEQ - (CONV_W - 1)), (0, 0))).reshape(rs, W_A)
    ya_s, u_s, yb_s, ret_s = _even_sample(ps_a, ps_b, st_exp, state_ret[0], cw, g_ret, cos_s, sin_s, lg_tab)
    x1, h1 = _outproj(ya, yb, w_out_e, xp, g_odd, 512, final=False)
    x1s, h1s = _outproj(ya_s, yb_s, w_out_e, xs, g_odd, 512, final=False)

    yc, c_p, n_p, m_p, *ps_c = _odd_heads(h1, h1s, w_o, bg, gm)
    pd, pd_s = _in_proj(h1, h1s, w_o, 3 * W_D, shift=N_GATE, tile0=O_GATE // IN_TN)
    yd = _odd_mlp(pd, lng, lnb, ws, bst, MIX_ROWS)
    n_rows = jnp.repeat(jnp.transpose(state_mlstm_n[0], (1, 0, 2)), DEC_SEQ, axis=1)
    m_rows = jnp.repeat(state_mlstm_m[0].T, DEC_SEQ, axis=1)[:, :, None]
    yc_s, yd_s, vn_s, c_s, no_s, mo_s = _odd_sample(
        ps_c, pd_s, h1s, w_o, state_mlstm_C[0], n_rows, m_rows, bg, gm, lng, lnb, rtab, btab)
    y_prompt = _outproj(yc, yd, w_out_o, x1, g_fin, 512, final=True)
    y_sample = _outproj(yc_s, yd_s, w_out_o, x1s, g_fin, 512, final=True)

    conv_s = u_s.reshape(DEC_BATCH, DEC_SEQ, W_A)[:, DEC_SEQ - (CONV_W - 1):, :]
    n_s = jnp.transpose(no_s[:, DEC_SEQ - 1::DEC_SEQ, :], (1, 0, 2))
    m_s = mo_s[:, DEC_SEQ - 1::DEC_SEQ, 0].T
    return (y_prompt.reshape(BATCH, SEQ, D_MODEL),
            y_sample.reshape(DEC_BATCH, DEC_SEQ, D_MODEL),
            conv_p[None], conv_s[None],
            ret_p[None], ret_s[None],
            c_p[None], c_s[None],
            n_p[:, :, 0, :][None], n_s[None],
            m_p[:, :, 0, 0][None], m_s[None],
            vn_s.reshape(DEC_BATCH, DEC_SEQ, W_D)[None])
```

```python
import functools
import math

import jax
import jax.numpy as jnp
from jax import lax
from jax.experimental import pallas as pl
from jax.experimental.pallas import tpu as pltpu

F32 = jnp.float32
BF16 = jnp.bfloat16

D_MODEL = 2048
BATCH = 4
SEQ = 2048
DEC_BATCH = 128
DEC_SEQ = 4
PAST_LEN = 16384
W_A = 1024
CONV_W = 3
W_B = 1024
H_B = 8
DH_B = 128
E_IN = 8192
W_C = 1024
H_C = 4
DV_C = 256
DQK_C = 128
W_D = 1024
G_D = 8
CHUNK = 128
O_GATE = 2 * H_C * DQK_C + 2 * W_C
N_GATE = 2 * H_C
O_N = O_GATE + 3 * W_D
ROPE_BASE = 10000.0
EPS = 1e-6
LOG_GAMMA = tuple(math.log(1.0 - 2.0 ** (-5.0 - h)) for h in range(H_B))
NEG_INF = float("-inf")
VMEM_LIMIT = 56 * 1024 * 1024

NT_DIMS = (((1,), (1,)), ((), ()))
TN_DIMS = (((0,), (0,)), ((), ()))


def _silu(z):
    return z * (1.0 / (1.0 + jnp.exp(-z)))


def _log_sigmoid(x):
    return jnp.minimum(x, 0.0) - jnp.log1p(jnp.exp(-jnp.abs(x)))


def _dot(a, b):
    return jnp.dot(a, b, preferred_element_type=F32)


def _dot_nt(a, b):
    return lax.dot_general(a, b, NT_DIMS, preferred_element_type=F32)


def _dot_tn(a, b):
    return lax.dot_general(a, b, TN_DIMS, preferred_element_type=F32)


def _dot_hi(a, b):
    return jnp.dot(a, b, preferred_element_type=F32, precision=lax.Precision.HIGHEST)


def _head_norm(o, g):
    mu = jnp.mean(o, axis=-1, keepdims=True)
    oc = o - mu
    var = jnp.mean(oc * oc, axis=-1, keepdims=True)
    return oc * lax.rsqrt(var + EPS) * g


def _params(sem):
    return pltpu.CompilerParams(dimension_semantics=sem, vmem_limit_bytes=VMEM_LIMIT)


def _norm_cast_kernel(x_ref, g_ref, h_ref):
    x = x_ref[...]
    ms = jnp.mean(x * x, axis=-1, keepdims=True)
    h_ref[...] = (x * lax.rsqrt(ms + EPS) * g_ref[...]).astype(BF16)


def _norm_cast(x, g, tm):
    m, d = x.shape
    return pl.pallas_call(
        _norm_cast_kernel,
        grid=(m // tm,),
        in_specs=[pl.BlockSpec((tm, d), lambda i: (i, 0)),
                  pl.BlockSpec((1, d), lambda i: (0, 0))],
        out_specs=pl.BlockSpec((tm, d), lambda i: (i, 0)),
        out_shape=jax.ShapeDtypeStruct((m, d), BF16),
        compiler_params=_params(("arbitrary",)),
        name="norm_cast",
    )(x, g)


IN_TM = 1024
IN_TN = 1024


def _in_proj_kernel(*refs, shift_from, shift):
    if shift:
        hp_ref, hs_ref, w_ref, wn_ref, op_ref, os_ref, wb = refs
    else:
        hp_ref, hs_ref, w_ref, op_ref, os_ref, wb = refs
    j = pl.program_id(0)
    i = pl.program_id(1)

    if shift:
        @pl.when(jnp.logical_and(i == 0, j < shift_from))
        def _():
            wb[...] = w_ref[...].astype(BF16)

        @pl.when(jnp.logical_and(i == 0, j >= shift_from))
        def _():
            wb[...] = jnp.concatenate([w_ref[shift:IN_TN, :], wn_ref[...]], axis=0).astype(BF16)

        mm = _dot_nt
    else:
        @pl.when(i == 0)
        def _():
            wb[...] = w_ref[...].astype(BF16)

        mm = _dot

    @pl.when(i == 0)
    def _():
        os_ref[...] = mm(hs_ref[...], wb[...])

    @pl.when(i > 0)
    def _():
        op_ref[...] = mm(hp_ref[...], wb[...])


def _in_proj(hp, hs, w, n_out, shift_from=0, shift=0, tile0=0):
    mp, k = hp.shape
    ms = hs.shape[0]
    n_prompt = mp // IN_TM
    prow = lambda j, i: (jnp.maximum(i - 1, 0), 0)
    in_specs = [pl.BlockSpec((IN_TM, k), prow),
                pl.BlockSpec((ms, k), lambda j, i: (0, 0))]
    args = [hp, hs, w]
    if shift:
        in_specs.append(pl.BlockSpec((IN_TN, k), lambda j, i: (j + tile0, 0)))
        in_specs.append(pl.BlockSpec((shift, k), lambda j, i: ((j + tile0 + 1) * (IN_TN // shift), 0)))
        args.append(w)
        wb_shape = (IN_TN, k)
    else:
        in_specs.append(pl.BlockSpec((k, IN_TN), lambda j, i: (0, j)))
        wb_shape = (k, IN_TN)
    return pl.pallas_call(
        functools.partial(_in_proj_kernel, shift_from=shift_from, shift=shift),
        grid=(n_out // IN_TN, n_prompt + 1),
        in_specs=in_specs,
        out_specs=(pl.BlockSpec((IN_TM, IN_TN), lambda j, i: (jnp.maximum(i - 1, 0), j)),
                   pl.BlockSpec((ms, IN_TN), lambda j, i: (0, j))),
        out_shape=(jax.ShapeDtypeStruct((mp, n_out), F32),
                   jax.ShapeDtypeStruct((ms, n_out), F32)),
        scratch_shapes=[pltpu.VMEM(wb_shape, BF16)],
        compiler_params=_params(("arbitrary", "arbitrary")),
        name="in_proj",
    )(*args)


def _outproj_kernel(ya_ref, yb_ref, w_ref, x_ref, g_ref, *out_refs, final):
    half = ya_ref.shape[1]
    acc = _dot(ya_ref[...], w_ref[0:half, :]) + _dot(yb_ref[...], w_ref[half:2 * half, :])
    x1 = x_ref[...] + acc
    ms = jnp.mean(x1 * x1, axis=-1, keepdims=True)
    hn = x1 * lax.rsqrt(ms + EPS) * g_ref[...]
    if final:
        out_refs[0][...] = hn
    else:
        out_refs[0][...] = x1
        out_refs[1][...] = hn.astype(BF16)


def _outproj(ya, yb, w, x, g, tm, final):
    m, half = ya.shape
    d = w.shape[1]
    row = lambda i: (i, 0)
    if final:
        out_shape = jax.ShapeDtypeStruct((m, d), F32)
        out_specs = pl.BlockSpec((tm, d), row)
    else:
        out_shape = (jax.ShapeDtypeStruct((m, d), F32), jax.ShapeDtypeStruct((m, d), BF16))
        out_specs = (pl.BlockSpec((tm, d), row), pl.BlockSpec((tm, d), row))
    return pl.pallas_call(
        functools.partial(_outproj_kernel, final=final),
        grid=(m // tm,),
        in_specs=[pl.BlockSpec((tm, half), row),
                  pl.BlockSpec((tm, half), row),
                  pl.BlockSpec((2 * half, d), lambda i: (0, 0)),
                  pl.BlockSpec((tm, d), row),
                  pl.BlockSpec((1, d), lambda i: (0, 0))],
        out_specs=out_specs,
        out_shape=out_shape,
        compiler_params=_params(("arbitrary",)),
        name="out_proj_final" if final else "out_proj",
    )(ya, yb, w, x, g)


MIX_ROWS = 512


def _rope(x, cosf, sins):
    return x * cosf + pltpu.roll(x, DH_B // 2, 1) * sins


def _even_prompt_kernel(p_ref, cw_ref, gret_ref, cos_ref, sin_ref,
                        ya_ref, yb_ref, conv_ref, s_ref, ubuf):
    L = CHUNK

    @pl.when(pl.program_id(1) == 0)
    def _():
        ubuf[0:8, :] = jnp.zeros((8, W_A), F32)
        s_ref[...] = jnp.zeros_like(s_ref)

    def chunk(ci, carry):
        rs = pl.ds(pl.multiple_of(ci * L, L), L)

        for j in range(W_A // 128):
            sl = slice(j * 128, (j + 1) * 128)
            a_b = p_ref[rs, j * 128:(j + 1) * 128]
            a_c = p_ref[rs, W_A + j * 128:W_A + (j + 1) * 128]
            a_x = p_ref[rs, 2 * W_A + j * 128:2 * W_A + (j + 1) * 128]
            a_z = p_ref[rs, 3 * W_A + j * 128:3 * W_A + (j + 1) * 128]
            u = a_c * a_x
            ubuf[8:8 + L, sl] = u
            t0 = ubuf[6:6 + L, sl]
            t1 = ubuf[7:7 + L, sl]
            conv = cw_ref[0:1, sl] * t0 + cw_ref[1:2, sl] * t1 + cw_ref[2:3, sl] * u
            ya_ref[rs, sl] = (a_b * conv * _silu(a_z)).astype(BF16)
            ubuf[0:8, sl] = u[L - 8:L, :]

        cosf = cos_ref[rs, :]
        sins = sin_ref[rs, :]
        row = lax.broadcasted_iota(jnp.int32, (L, L), 0)
        col = lax.broadcasted_iota(jnp.int32, (L, L), 1)
        causal = row >= col
        diff = jnp.maximum(row - col, 0).astype(F32)
        ti = lax.broadcasted_iota(jnp.int32, (L, 1), 0).astype(F32)
        base = 4 * W_A
        for h in range(H_B):
            lg = LOG_GAMMA[h]
            sl = slice(h * DH_B, (h + 1) * DH_B)
            q = p_ref[rs, base + h * DH_B:base + (h + 1) * DH_B]
            k = p_ref[rs, base + W_B + h * DH_B:base + W_B + (h + 1) * DH_B]
            v = p_ref[rs, base + 2 * W_B + h * DH_B:base + 2 * W_B + (h + 1) * DH_B]
            z = p_ref[rs, base + 3 * W_B + h * DH_B:base + 3 * W_B + (h + 1) * DH_B]
            qr = _rope(q, cosf, sins)
            kr = _rope(k, cosf, sins) * (DH_B ** -0.5)
            decay = jnp.where(causal, jnp.exp(lg * diff), 0.0)
            qb = qr.astype(BF16)
            kb = kr.astype(BF16)
            vb = v.astype(BF16)
            sc = _dot_nt(qb, kb) * decay
            inner = _dot(sc.astype(BF16), vb)
            s_old = s_ref[0, h]
            cross = _dot(qb, s_old.astype(BF16)) * jnp.exp(lg * (ti + 1.0))
            kd = (kr * jnp.exp(lg * (L - 1.0 - ti))).astype(BF16)
            s_ref[0, h] = math.exp(lg * L) * s_old + _dot_tn(kd, vb)
            o = inner + cross
            yb_ref[rs, sl] = (_head_norm(o, gret_ref[0:1, sl]) * _silu(z)).astype(BF16)
        return carry

    lax.fori_loop(0, MIX_ROWS // L, chunk, 0)
    conv_ref[0] = ubuf[6:8, :]


def _even_prompt(p, conv_w, g_ret, cosf, sins):
    nc = SEQ // MIX_ROWS
    rows = lambda b, c: (b * nc + c, 0)
    const2 = lambda b, c: (0, 0)
    return pl.pallas_call(
        _even_prompt_kernel,
        grid=(BATCH, nc),
        in_specs=[pl.BlockSpec((MIX_ROWS, E_IN), rows),
                  pl.BlockSpec((CONV_W, W_A), const2),
                  pl.BlockSpec((1, W_B), const2),
                  pl.BlockSpec((MIX_ROWS, DH_B), lambda b, c: (c, 0)),
                  pl.BlockSpec((MIX_ROWS, DH_B), lambda b, c: (c, 0))],
        out_specs=(pl.BlockSpec((MIX_ROWS, W_A), rows),
                   pl.BlockSpec((MIX_ROWS, W_B), rows),
                   pl.BlockSpec((1, CONV_W - 1, W_A), lambda b, c: (b, 0, 0)),
                   pl.BlockSpec((1, H_B, DH_B, DH_B), lambda b, c: (b, 0, 0, 0))),
        out_shape=(jax.ShapeDtypeStruct((BATCH * SEQ, W_A), BF16),
                   jax.ShapeDtypeStruct((BATCH * SEQ, W_B), BF16),
                   jax.ShapeDtypeStruct((BATCH, CONV_W - 1, W_A), F32),
                   jax.ShapeDtypeStruct((BATCH, H_B, DH_B, DH_B), F32)),
        scratch_shapes=[pltpu.VMEM((CHUNK + 8, W_A), F32)],
        compiler_params=_params(("arbitrary", "arbitrary")),
        name="even_prompt",
    )(p, conv_w, g_ret, cosf, sins)


FT = 1024
FN = BATCH * SEQ // FT
FPB = SEQ // FT
HG = 2
GW = HG * 128
PCH = 2


def _chunk_pipeline(n, piece, stages, gate=None):
    sa, sb, sc, sd, se = stages
    npieces = n // PCH
    for j in range(PCH):
        piece(0, j)
    if gate is not None:
        gate(0)
    for c in range(n + 2):
        k, j = c // PCH + 1, c % PCH
        if k < npieces:
            piece(k, j)
        if c < n:
            sa(c)
        if 1 <= c <= n:
            sc(c - 1)
        if c < n:
            sb(c)
        if 1 <= c <= n:
            sd(c - 1)
        if c >= 2:
            se(c - 2)
        if gate is not None and j == PCH - 1 and k < npieces:
            gate(k)


def _even_heads_kernel(hp_ref, hs_ref, wq_ref, wk_ref, wv_ref, wz_ref, gret_ref, cos_ref, sin_ref, lg_ref,
                       yb_ref, s_ref, sq_ref, sk_ref, sv_ref, sz_ref, wb, pt, s_scr):
    s = pl.program_id(1)
    L = CHUNK

    @pl.when(s == 0)
    def _():
        for part, w_ref in enumerate((wq_ref, wk_ref, wv_ref, wz_ref)):
            wb[:, part * GW:(part + 1) * GW] = w_ref[...].astype(BF16)
        ps = _dot(hs_ref[...], wb[...])
        for part, o_ref in enumerate((sq_ref, sk_ref, sv_ref, sz_ref)):
            o_ref[...] = ps[:, part * GW:(part + 1) * GW]

    @pl.when(s > 0)
    def _():
        t = s - 1
        n = FT // L
        rows = lambda c: slice(c * L, (c + 1) * L)
        cols = lambda part, i: slice(part * GW + i * DH_B, part * GW + (i + 1) * DH_B)

        def piece(k, j):
            pr = slice(k * PCH * L, (k + 1) * PCH * L)
            pc = slice(j * 2 * GW, (j + 1) * 2 * GW)
            pt[pr, pc] = _dot(hp_ref[pr, :], wb[:, pc])

        row = lax.broadcasted_iota(jnp.int32, (L, L), 0)
        col = lax.broadcasted_iota(jnp.int32, (L, L), 1)
        causal = row >= col
        diff = jnp.maximum(row - col, 0).astype(F32)
        ti = lax.broadcasted_iota(jnp.int32, (L, 1), 0).astype(F32)
        lgs = [lg_ref[i][:, 0:1] for i in range(HG)]
        decay = [jnp.where(causal, jnp.exp(lg * diff), 0.0) for lg in lgs]
        q_decay = [jnp.exp(lg * (ti + 1.0)) for lg in lgs]
        k_decay = [jnp.exp(lg * (L - 1.0 - ti)) for lg in lgs]
        gamma_l = [jnp.exp(lg * float(L)) for lg in lgs]
        state = {0: [jnp.where(t % FPB == 0, 0.0, s_scr[i]) for i in range(HG)]}
        v = {}

        def stage_a(c):
            cosf = cos_ref[rows(c), :]
            sins = sin_ref[rows(c), :]
            v[c] = []
            for i in range(HG):
                kr = _rope(pt[rows(c), cols(1, i)], cosf, sins) * (DH_B ** -0.5)
                v[c].append(dict(qb=_rope(pt[rows(c), cols(0, i)], cosf, sins).astype(BF16),
                                 kb=kr.astype(BF16),
                                 kd=(kr * k_decay[i]).astype(BF16),
                                 vb=pt[rows(c), cols(2, i)].astype(BF16)))

        def stage_b(c):
            for i, d in enumerate(v[c]):
                d["sc"] = _dot_nt(d["qb"], d["kb"])
                d["cross"] = _dot(d["qb"], state[c][i].astype(BF16))
                d["upd"] = _dot_tn(d["kd"], d["vb"])

        def stage_c(c):
            state[c + 1] = []
            for i, d in enumerate(v[c]):
                d["sc"] = (d["sc"] * decay[i]).astype(BF16)
                state[c + 1].append(gamma_l[i] * state[c][i] + d["upd"])

        def stage_d(c):
            for d in v[c]:
                d["inner"] = _dot(d["sc"], d["vb"])

        def stage_e(c):
            for i, d in enumerate(v.pop(c)):
                o = d["inner"] + d["cross"] * q_decay[i]
                g = gret_ref[0:1, i * DH_B:(i + 1) * DH_B]
                z = pt[rows(c), cols(3, i)]
                yb_ref[rows(c), i * DH_B:(i + 1) * DH_B] = (_head_norm(o, g) * _silu(z)).astype(BF16)

        _chunk_pipeline(n, piece, (stage_a, stage_b, stage_c, stage_d, stage_e))
        for i in range(HG):
            s_scr[i] = state[n][i]
            s_ref[0, i] = state[n][i]


def _even_heads(hp, hs, w, g_ret, cosf, sins, lg_tab):
    k = hp.shape[1]
    ms = hs.shape[0]
    ng = H_B // HG
    base = 4 * W_A // GW
    tile = lambda s: jnp.maximum(s - 1, 0)
    wspec = lambda part: pl.BlockSpec((k, GW), lambda g, s: (0, base + part * ng + g))
    sspec = pl.BlockSpec((ms, GW), lambda g, s: (0, g))
    sshape = jax.ShapeDtypeStruct((ms, W_B), F32)
    return pl.pallas_call(
        _even_heads_kernel,
        grid=(ng, FN + 1),
        in_specs=[pl.BlockSpec((FT, k), lambda g, s: (tile(s), 0)),
                  pl.BlockSpec((ms, k), lambda g, s: (0, 0)),
                  wspec(0), wspec(1), wspec(2), wspec(3),
                  pl.BlockSpec((1, GW), lambda g, s: (0, g)),
                  pl.BlockSpec((FT, DH_B), lambda g, s: (tile(s) % FPB, 0)),
                  pl.BlockSpec((FT, DH_B), lambda g, s: (tile(s) % FPB, 0)),
                  pl.BlockSpec((HG, 1, 128), lambda g, s: (g, 0, 0))],
        out_specs=(pl.BlockSpec((FT, GW), lambda g, s: (tile(s), g)),
                   pl.BlockSpec((1, HG, DH_B, DH_B), lambda g, s: (tile(s) // FPB, g, 0, 0)),
                   sspec, sspec, sspec, sspec),
        out_shape=(jax.ShapeDtypeStruct((BATCH * SEQ, W_B), BF16),
                   jax.ShapeDtypeStruct((BATCH, H_B, DH_B, DH_B), F32),
                   sshape, sshape, sshape, sshape),
        scratch_shapes=[pltpu.VMEM((k, 4 * GW), BF16),
                        pltpu.VMEM((FT, 4 * GW), F32),
                        pltpu.VMEM((HG, DH_B, DH_B), F32)],
        compiler_params=_params(("arbitrary", "arbitrary")),
        name="even_heads",
    )(hp, hs, w, w, w, w, g_ret, cosf, sins, lg_tab)


def _even_conv_kernel(hp_ref, hs_ref, wb_ref, wc_ref, wx_ref, wz_ref, cw_ref,
                      ya_ref, conv_ref, sb_ref, sc_ref, sx_ref, sz_ref, wb, pt, ubuf):
    s = pl.program_id(1)
    L = CHUNK

    @pl.when(s == 0)
    def _():
        for part, w_ref in enumerate((wb_ref, wc_ref, wx_ref, wz_ref)):
            wb[:, part * GW:(part + 1) * GW] = w_ref[...].astype(BF16)
        ps = _dot(hs_ref[...], wb[...])
        for part, o_ref in enumerate((sb_ref, sc_ref, sx_ref, sz_ref)):
            o_ref[...] = ps[:, part * GW:(part + 1) * GW]

    @pl.when(s > 0)
    def _():
        t = s - 1
        n = FT // L
        rows = lambda c: slice(c * L, (c + 1) * L)
        part = lambda p, c: pt[rows(c), p * GW:(p + 1) * GW]

        @pl.when(t % FPB == 0)
        def _():
            ubuf[0:8, :] = jnp.zeros((8, GW), F32)

        @pl.when(t % FPB != 0)
        def _():
            ubuf[0:8, :] = ubuf[FT:FT + 8, :]

        def piece(k, j):
            pr = slice(k * PCH * L, (k + 1) * PCH * L)
            pc = slice(j * 2 * GW, (j + 1) * 2 * GW)
            pt[pr, pc] = _dot(hp_ref[pr, :], wb[:, pc])

        for j in range(PCH):
            piece(0, j)
        for c in range(n):
            if c // PCH + 1 < n // PCH:
                piece(c // PCH + 1, c % PCH)
            u = part(1, c) * part(2, c)
            ubuf[8 + c * L:8 + (c + 1) * L, :] = u
            t0 = ubuf[6 + c * L:6 + (c + 1) * L, :]
            t1 = ubuf[7 + c * L:7 + (c + 1) * L, :]
            conv = cw_ref[0:1, :] * t0 + cw_ref[1:2, :] * t1 + cw_ref[2:3, :] * u
            ya_ref[rows(c), :] = (part(0, c) * conv * _silu(part(3, c))).astype(BF16)
        conv_ref[0] = ubuf[FT + 6:FT + 8, :]


def _even_conv(hp, hs, w, conv_w):
    k = hp.shape[1]
    ms = hs.shape[0]
    ng = W_A // GW
    tile = lambda s: jnp.maximum(s - 1, 0)
    wspec = lambda part: pl.BlockSpec((k, GW), lambda g, s: (0, part * ng + g))
    sspec = pl.BlockSpec((ms, GW), lambda g, s: (0, g))
    sshape = jax.ShapeDtypeStruct((ms, W_A), F32)
    return pl.pallas_call(
        _even_conv_kernel,
        grid=(ng, FN + 1),
        in_specs=[pl.BlockSpec((FT, k), lambda g, s: (tile(s), 0)),
                  pl.BlockSpec((ms, k), lambda g, s: (0, 0)),
                  wspec(0), wspec(1), wspec(2), wspec(3),
                  pl.BlockSpec((CONV_W, GW), lambda g, s: (0, g))],
        out_specs=(pl.BlockSpec((FT, GW), lambda g, s: (tile(s), g)),
                   pl.BlockSpec((1, CONV_W - 1, GW), lambda g, s: (tile(s) // FPB, 0, g)),
                   sspec, sspec, sspec, sspec),
        out_shape=(jax.ShapeDtypeStruct((BATCH * SEQ, W_A), BF16),
                   jax.ShapeDtypeStruct((BATCH, CONV_W - 1, W_A), F32),
                   sshape, sshape, sshape, sshape),
        scratch_shapes=[pltpu.VMEM((k, 4 * GW), BF16),
                        pltpu.VMEM((FT, 4 * GW), F32),
                        pltpu.VMEM((FT + 8, GW), F32)],
        compiler_params=_params(("arbitrary", "arbitrary")),
        name="even_conv",
    )(hp, hs, w, w, w, w, conv_w)


SB = 32
SR = SB * DEC_SEQ


def _even_sample_kernel(ab_ref, ac_ref, ax_ref, az_ref, pq_ref, pk_ref, pv_ref, pz_ref, st_ref, s_ref,
                        cw_ref, gret_ref, cos_ref, sin_ref, lg_ref,
                        ya_ref, u_ref, yb_ref, so_ref, cross_scr):
    h = pl.program_id(1)
    row = lax.broadcasted_iota(jnp.int32, (SR, SR), 0)
    col = lax.broadcasted_iota(jnp.int32, (SR, SR), 1)
    trow = row & 3

    @pl.when(h == 0)
    def _():
        for j in range(W_A // 128):
            sl = slice(j * 128, (j + 1) * 128)
            a_b = ab_ref[:, sl]
            a_c = ac_ref[:, sl]
            a_x = ax_ref[:, sl]
            a_z = az_ref[:, sl]
            u = a_c * a_x
            e = st_ref[:, sl]
            tap1 = jnp.where(trow >= 1, pltpu.roll(u, 1, 0), pltpu.roll(e, SR - 1, 0))
            tap0 = jnp.where(trow >= 2, pltpu.roll(u, 2, 0), e)
            conv = cw_ref[0:1, sl] * tap0 + cw_ref[1:2, sl] * tap1 + cw_ref[2:3, sl] * u
            ya_ref[:, sl] = (a_b * conv * _silu(a_z)).astype(BF16)
            u_ref[:, sl] = u

    lg = lg_ref[0][:, 0:1]
    same = (row >> 2) == (col >> 2)
    dd = trow - (col & 3)
    mask = jnp.where(same, dd, -1) >= 0
    decay = jnp.where(mask, jnp.exp(lg * jnp.maximum(dd, 0).astype(F32)), 0.0)
    tcol = (lax.broadcasted_iota(jnp.int32, (SR, 1), 0) & 3).astype(F32)
    cosf = cos_ref[...]
    sins = sin_ref[...]
    qr = _rope(pq_ref[...], cosf, sins)
    kr = _rope(pk_ref[...], cosf, sins) * (DH_B ** -0.5)
    qb = qr.astype(BF16)
    kb = kr.astype(BF16)
    vb = pv_ref[...].astype(BF16)
    sc = _dot_nt(qb, kb) * decay
    inner = _dot(sc.astype(BF16), vb)
    kdt = (kr * jnp.exp(lg * (DEC_SEQ - 1.0 - tcol))).T
    gamma_l = jnp.exp(lg * float(DEC_SEQ))
    lane_b = col >> 2
    sub = lax.broadcasted_iota(jnp.int32, (8, DH_B), 0)
    for g in range(SR // 8):
        q8 = qb[8 * g:8 * g + 8, :]
        res = []
        for beta in range(2):
            b = 2 * g + beta
            s_old = s_ref[b, 0]
            res.append(_dot(q8, s_old.astype(BF16)))
            lhs = jnp.where(lane_b == b, kdt, 0.0).astype(BF16)
            so_ref[b, 0] = gamma_l * s_old + _dot(lhs, vb)
        cross_scr[8 * g:8 * g + 8, :] = jnp.where(sub < DEC_SEQ, res[0], res[1])
    o = inner + cross_scr[...] * jnp.exp(lg * (tcol + 1.0))
    yb_ref[...] = (_head_norm(o, gret_ref[...]) * _silu(pz_ref[...])).astype(BF16)


def _even_sample(pa, pb, st_exp, s_state, conv_w, g_ret, cosf, sins, lg_tab):
    nb = DEC_BATCH // SB
    const2 = lambda i, h: (0, 0)
    aspec = pl.BlockSpec((SR, W_A), lambda i, h: (i, 0))
    hspec = pl.BlockSpec((SR, DH_B), lambda i, h: (i, h))
    return pl.pallas_call(
        _even_sample_kernel,
        grid=(nb, H_B),
        in_specs=[aspec, aspec, aspec, aspec,
                  hspec, hspec, hspec, hspec,
                  pl.BlockSpec((SR, W_A), lambda i, h: (i, 0)),
                  pl.BlockSpec((SB, 1, DH_B, DH_B), lambda i, h: (i, h, 0, 0)),
                  pl.BlockSpec((CONV_W, W_A), const2),
                  pl.BlockSpec((1, DH_B), lambda i, h: (0, h)),
                  pl.BlockSpec((SR, DH_B), const2),
                  pl.BlockSpec((SR, DH_B), const2),
                  pl.BlockSpec((1, 1, 128), lambda i, h: (h, 0, 0))],
        out_specs=(pl.BlockSpec((SR, W_A), lambda i, h: (i, 0)),
                   pl.BlockSpec((SR, W_A), lambda i, h: (i, 0)),
                   pl.BlockSpec((SR, DH_B), lambda i, h: (i, h)),
                   pl.BlockSpec((SB, 1, DH_B, DH_B), lambda i, h: (i, h, 0, 0))),
        out_shape=(jax.ShapeDtypeStruct((DEC_BATCH * DEC_SEQ, W_A), BF16),
                   jax.ShapeDtypeStruct((DEC_BATCH * DEC_SEQ, W_A), F32),
                   jax.ShapeDtypeStruct((DEC_BATCH * DEC_SEQ, W_B), BF16),
                   jax.ShapeDtypeStruct((DEC_BATCH, H_B, DH_B, DH_B), F32)),
        scratch_shapes=[pltpu.VMEM((SR, DH_B), F32)],
        compiler_params=_params(("arbitrary", "arbitrary")),
        name="even_sample",
    )(*pa, *pb, st_exp, s_state, conv_w, g_ret, cosf, sins, lg_tab)


def _odd_prompt_kernel(p_ref, h_ref, wg_ref, bg_ref, gm_ref, lng_ref, lnb_ref, ws_ref, bst_ref,
                       yc_ref, yd_ref, c_ref, n_ref, m_ref, wgb, wsb):
    L = CHUNK

    @pl.when(jnp.logical_and(pl.program_id(0) == 0, pl.program_id(1) == 0))
    def _():
        wgb[...] = wg_ref[...].astype(BF16)
        keep = (lax.broadcasted_iota(jnp.int32, (L, L), 0) >= lax.broadcasted_iota(jnp.int32, (L, L), 1))
        for g in range(G_D):
            wsb[g] = jnp.where(keep, ws_ref[g], 0.0).astype(BF16)

    @pl.when(pl.program_id(1) == 0)
    def _():
        c_ref[...] = jnp.zeros_like(c_ref)
        n_ref[...] = jnp.zeros_like(n_ref)
        m_ref[...] = jnp.zeros_like(m_ref)

    def chunk(ci, carry):
        rs = pl.ds(pl.multiple_of(ci * L, L), L)
        row = lax.broadcasted_iota(jnp.int32, (L, L), 0)
        col = lax.broadcasted_iota(jnp.int32, (L, L), 1)
        tri = row >= col

        pre = _dot_nt(h_ref[rs, :], wgb[...]) + bg_ref[...]
        lf = _log_sigmoid(pre)
        b_c = _dot_hi(jnp.where(tri, 1.0, 0.0), lf)
        b_r = b_c.T
        pre_r = pre.T
        for h in range(H_C):
            bc = b_c[:, H_C + h:H_C + h + 1]
            br = b_r[H_C + h:H_C + h + 1, :]
            igr = pre_r[h:h + 1, :]
            igc = pre[:, h:h + 1]
            m_prev = m_ref[0, h:h + 1, 0:1]
            log_d = jnp.where(tri, bc - br + igr, NEG_INF)
            log_inter = bc + m_prev
            m_t = jnp.maximum(log_inter, jnp.max(log_d, axis=-1, keepdims=True))
            w = jnp.exp(log_d - m_t)
            w_inter = jnp.exp(log_inter - m_t)
            q = p_ref[rs, h * DQK_C:(h + 1) * DQK_C] * (DQK_C ** -0.5)
            k = p_ref[rs, H_C * DQK_C + h * DQK_C:H_C * DQK_C + (h + 1) * DQK_C]
            v = p_ref[rs, 2 * H_C * DQK_C + h * DV_C:2 * H_C * DQK_C + (h + 1) * DV_C]
            z = p_ref[rs, O_GATE - W_C + h * DV_C:O_GATE - W_C + (h + 1) * DV_C]
            qb = q.astype(BF16)
            kb = k.astype(BF16)
            vb = v.astype(BF16)
            sc = _dot_nt(qb, kb) * w
            c_old = c_ref[0, h]
            n_old = n_ref[0, h:h + 1, :]
            num = _dot(sc.astype(BF16), vb) + w_inter * _dot_nt(qb, c_old.astype(BF16))
            den = jnp.sum(sc, axis=-1, keepdims=True) + w_inter * jnp.sum(q * n_old, axis=-1, keepdims=True)
            hh = num / jnp.maximum(jnp.abs(den), jnp.exp(-m_t))
            m_new = m_t[L - 1:L, :]
            b_last = bc[L - 1:L, :]
            w_end = jnp.exp(b_last - bc + igc - m_new)
            cd = jnp.exp(b_last + m_prev - m_new)
            c_ref[0, h] = cd * c_old + _dot_tn((v * w_end).astype(BF16), kb)
            n_ref[0, h:h + 1, :] = cd * n_old + jnp.sum(w_end * k, axis=0, keepdims=True)
            m_ref[0, h:h + 1, :] = jnp.broadcast_to(m_new, (1, 128))
            sl = slice(h * DV_C, (h + 1) * DV_C)
            yc_ref[rs, sl] = (_head_norm(hh, gm_ref[0:1, sl]) * _silu(z)).astype(BF16)

        dv = lambda g: p_ref[rs, O_GATE + W_D + g * 128:O_GATE + W_D + (g + 1) * 128]
        tot = dv(0)
        for g in range(1, G_D):
            tot = tot + dv(g)
        mu = jnp.sum(tot, axis=-1, keepdims=True) * (1.0 / W_D)
        sq = (dv(0) - mu) * (dv(0) - mu)
        for g in range(1, G_D):
            sq = sq + (dv(g) - mu) * (dv(g) - mu)
        rstd = lax.rsqrt(jnp.sum(sq, axis=-1, keepdims=True) * (1.0 / W_D) + EPS)
        for g in range(G_D):
            sl = slice(g * 128, (g + 1) * 128)
            vn = (dv(g) - mu) * rstd * lng_ref[0:1, sl] + lnb_ref[0:1, sl]
            s = _dot(wsb[g], vn.astype(BF16)) + bst_ref[:, g:g + 1]
            d_u = p_ref[rs, O_GATE + g * 128:O_GATE + (g + 1) * 128]
            d_z = p_ref[rs, O_GATE + 2 * W_D + g * 128:O_GATE + 2 * W_D + (g + 1) * 128]
            yd_ref[rs, sl] = (d_u * s * _silu(d_z)).astype(BF16)
        return carry

    lax.fori_loop(0, MIX_ROWS // L, chunk, 0)


def _odd_prompt(p, h, w_o, bg, gm, lng, lnb, ws, bst):
    nc = SEQ // MIX_ROWS
    rows = lambda b, c: (b * nc + c, 0)
    const2 = lambda b, c: (0, 0)
    return pl.pallas_call(
        _odd_prompt_kernel,
        grid=(BATCH, nc),
        in_specs=[pl.BlockSpec((MIX_ROWS, O_N), rows),
                  pl.BlockSpec((MIX_ROWS, D_MODEL), rows),
                  pl.BlockSpec((128, D_MODEL), lambda b, c: (O_GATE // 128, 0)),
                  pl.BlockSpec((1, 128), const2),
                  pl.BlockSpec((1, W_C), const2),
                  pl.BlockSpec((1, W_D), const2),
                  pl.BlockSpec((1, W_D), const2),
                  pl.BlockSpec((G_D, CHUNK, CHUNK), lambda b, c: (0, 0, 0)),
                  pl.BlockSpec((CHUNK, G_D), const2)],
        out_specs=(pl.BlockSpec((MIX_ROWS, W_C), rows),
                   pl.BlockSpec((MIX_ROWS, W_D), rows),
                   pl.BlockSpec((1, H_C, DV_C, DQK_C), lambda b, c: (b, 0, 0, 0)),
                   pl.BlockSpec((1, H_C, DQK_C), lambda b, c: (b, 0, 0)),
                   pl.BlockSpec((1, 8, 128), lambda b, c: (b, 0, 0))),
        out_shape=(jax.ShapeDtypeStruct((BATCH * SEQ, W_C), BF16),
                   jax.ShapeDtypeStruct((BATCH * SEQ, W_D), BF16),
                   jax.ShapeDtypeStruct((BATCH, H_C, DV_C, DQK_C), F32),
                   jax.ShapeDtypeStruct((BATCH, H_C, DQK_C), F32),
                   jax.ShapeDtypeStruct((BATCH, 8, 128), F32)),
        scratch_shapes=[pltpu.VMEM((128, D_MODEL), BF16),
                        pltpu.VMEM((G_D, CHUNK, CHUNK), BF16)],
        compiler_params=_params(("arbitrary", "arbitrary")),
        name="odd_prompt",
    )(p, h, w_o, bg, gm, lng, lnb, ws, bst)


CG = 2
CW = 128 + CG * (2 * DQK_C + 2 * DV_C)
CT = 1024
CN = BATCH * SEQ // CT
CPB = SEQ // CT


def _odd_heads_kernel(hp_ref, hs_ref, wq_ref, wk_ref, wv_ref, wz_ref, wg_ref, bg_ref, gm_ref,
                      yc_ref, c_ref, n_ref, m_ref, sq_ref, sk_ref, sv_ref, sz_ref,
                      wb, pt, c_scr, n_scr, m_scr, gt_scr):
    grp = pl.program_id(0)
    s = pl.program_id(1)
    L = CHUNK
    gc = slice(0, 128)
    qc = slice(128, 128 + CG * DQK_C)
    kc = slice(qc.stop, qc.stop + CG * DQK_C)
    vc = slice(kc.stop, kc.stop + CG * DV_C)
    zc = slice(vc.stop, vc.stop + CG * DV_C)
    head = lambda sl, i, w: slice(sl.start + i * w, sl.start + (i + 1) * w)

    @pl.when(s == 0)
    def _():
        wb[qc, :] = wq_ref[...].astype(BF16)
        wb[kc, :] = wk_ref[...].astype(BF16)
        wb[vc, :] = wv_ref[...].astype(BF16)
        wb[zc, :] = wz_ref[...].astype(BF16)
        wb[gc, :] = wg_ref[...].astype(BF16)
        ps = _dot_nt(hs_ref[...], wb[qc.start:CW, :])
        off = lambda sl: slice(sl.start - qc.start, sl.stop - qc.start)
        sq_ref[...] = ps[:, off(qc)]
        sk_ref[...] = ps[:, off(kc)]
        sv_ref[...] = ps[:, off(vc)]
        sz_ref[...] = ps[:, off(zc)]

    @pl.when(s > 0)
    def _():
        t = s - 1
        n = CT // L
        rows = lambda c: slice(c * L, (c + 1) * L)

        def piece(k, j):
            pr = slice(k * PCH * L, (k + 1) * PCH * L)
            pc = (slice(0, vc.start), slice(vc.start, CW))[j]
            pt[pr, pc] = _dot_nt(hp_ref[pr, :], wb[pc, :])

        row = lax.broadcasted_iota(jnp.int32, (L, L), 0)
        col = lax.broadcasted_iota(jnp.int32, (L, L), 1)
        tri = row >= col
        fresh = t % CPB == 0
        cst = {0: [jnp.where(fresh, 0.0, c_scr[i]) for i in range(CG)]}
        nst = {0: [jnp.where(fresh, 0.0, n_scr[i]) for i in range(CG)]}
        mst = {0: [jnp.where(fresh, 0.0, m_scr[i, 0:1, 0:1]) for i in range(CG)]}
        v = {}
        gates = {}

        def gate(k):
            cs = range(k * PCH, (k + 1) * PCH)
            for c in cs:
                gt_scr[c] = (pt[rows(c), gc] + bg_ref[...]).T
            pad = jnp.zeros((8 - CG * PCH, L), F32)
            ig_rows = jnp.concatenate(
                [gt_scr[c, pl.ds(grp * CG + i, 1), :] for i in range(CG) for c in cs] + [pad], axis=0)
            lf_rows = jnp.concatenate(
                [_log_sigmoid(gt_scr[c, pl.ds(grp * CG + i + H_C, 1), :]) for i in range(CG) for c in cs]
                + [pad], axis=0)
            b_rows = _dot_hi(lf_rows, jnp.where(row <= col, 1.0, 0.0))
            tall = jnp.zeros((L - 8, L), F32)
            gates[k] = dict(ig_rows=ig_rows, b_rows=b_rows,
                            b_cols=jnp.concatenate([b_rows, tall], axis=0).T,
                            ig_cols=jnp.concatenate([ig_rows, tall], axis=0).T)

        def stage_a(c):
            gk = gates[c // PCH]
            v[c] = []
            nst[c + 1] = []
            mst[c + 1] = []
            for i in range(CG):
                r = i * PCH + c % PCH
                b_r = gk["b_rows"][r:r + 1, :]
                ig_r = gk["ig_rows"][r:r + 1, :]
                b_c = gk["b_cols"][:, r:r + 1]
                ig_c = gk["ig_cols"][:, r:r + 1]
                m_prev = mst[c][i]
                log_d = jnp.where(tri, b_c - b_r + ig_r, NEG_INF)
                log_inter = b_c + m_prev
                m_t = jnp.maximum(log_inter, jnp.max(log_d, axis=-1, keepdims=True))
                m_new = m_t[L - 1:L, :]
                b_last = b_c[L - 1:L, :]
                w_end = jnp.exp(b_last - b_c + ig_c - m_new)
                cd = jnp.exp(b_last + m_prev - m_new)
                q = pt[rows(c), head(qc, i, DQK_C)] * (DQK_C ** -0.5)
                k = pt[rows(c), head(kc, i, DQK_C)]
                vv = pt[rows(c), head(vc, i, DV_C)]
                nst[c + 1].append(cd * nst[c][i] + jnp.sum(w_end * k, axis=0, keepdims=True))
                mst[c + 1].append(m_new)
                v[c].append(dict(w=jnp.exp(log_d - m_t), w_inter=jnp.exp(log_inter - m_t),
                                 floor=jnp.exp(-m_t), cd=cd, qb=q.astype(BF16), kb=k.astype(BF16),
                                 vb=vv.astype(BF16), vw=(vv * w_end).astype(BF16),
                                 qn=jnp.sum(q * nst[c][i], axis=-1, keepdims=True)))

        def stage_b(c):
            for i, d in enumerate(v[c]):
                d["sc"] = _dot_nt(d["qb"], d["kb"])
                d["upd"] = _dot_tn(d["vw"], d["kb"])
            for i, d in enumerate(v[c]):
                d["inter"] = _dot_nt(d["qb"], cst[c][i].astype(BF16))

        def stage_c(c):
            cst[c + 1] = []
            for i, d in enumerate(v[c]):
                sc = d["sc"] * d["w"]
                d["den"] = jnp.sum(sc, axis=-1, keepdims=True) + d["w_inter"] * d["qn"]
                d["sc"] = sc.astype(BF16)
                cst[c + 1].append(d["cd"] * cst[c][i] + d["upd"])

        def stage_d(c):
            for d in v[c]:
                d["num"] = _dot(d["sc"], d["vb"])

        def stage_e(c):
            for i, d in enumerate(v.pop(c)):
                num = d["num"] + d["w_inter"] * d["inter"]
                hh = num / jnp.maximum(jnp.abs(d["den"]), d["floor"])
                z = pt[rows(c), head(zc, i, DV_C)]
                ys = slice(i * DV_C, (i + 1) * DV_C)
                yc_ref[rows(c), ys] = (_head_norm(hh, gm_ref[0:1, ys]) * _silu(z)).astype(BF16)

        _chunk_pipeline(n, piece, (stage_a, stage_b, stage_c, stage_d, stage_e), gate)
        for i in range(CG):
            c_scr[i] = cst[n][i]
            n_scr[i] = nst[n][i]
            m_scr[i] = jnp.broadcast_to(mst[n][i], (8, 128))
            c_ref[0, i] = cst[n][i]
            n_ref[0, i] = nst[n][i]
            m_ref[0, i] = jnp.broadcast_to(mst[n][i], (1, 128))


def _odd_heads(hp, hs, w_t, bg, gm):
    k = hp.shape[1]
    ms = hs.shape[0]
    tile = lambda s: jnp.maximum(s - 1, 0)
    qw, vw = CG * DQK_C, CG * DV_C
    koff = H_C * DQK_C // qw
    voff = 2 * H_C * DQK_C // vw
    zoff = (2 * H_C * DQK_C + W_C) // vw
    seq = lambda g, s: (tile(s) // CPB, g, 0, 0)
    once = pl.Buffered(1)
    return pl.pallas_call(
        _odd_heads_kernel,
        grid=(H_C // CG, CN + 1),
        in_specs=[pl.BlockSpec((CT, k), lambda g, s: (tile(s), 0)),
                  pl.BlockSpec((ms, k), lambda g, s: (0, 0), pipeline_mode=once),
                  pl.BlockSpec((qw, k), lambda g, s: (g, 0), pipeline_mode=once),
                  pl.BlockSpec((qw, k), lambda g, s: (koff + g, 0), pipeline_mode=once),
                  pl.BlockSpec((vw, k), lambda g, s: (voff + g, 0), pipeline_mode=once),
                  pl.BlockSpec((vw, k), lambda g, s: (zoff + g, 0), pipeline_mode=once),
                  pl.BlockSpec((128, k), lambda g, s: (O_GATE // 128, 0), pipeline_mode=once),
                  pl.BlockSpec((1, 128), lambda g, s: (0, 0)),
                  pl.BlockSpec((1, vw), lambda g, s: (0, g))],
        out_specs=(pl.BlockSpec((CT, vw), lambda g, s: (tile(s), g)),
                   pl.BlockSpec((1, CG, DV_C, DQK_C), seq),
                   pl.BlockSpec((1, CG, 1, DQK_C), seq),
                   pl.BlockSpec((1, CG, 1, 128), seq),
                   pl.BlockSpec((ms, qw), lambda g, s: (0, g)),
                   pl.BlockSpec((ms, qw), lambda g, s: (0, g)),
                   pl.BlockSpec((ms, vw), lambda g, s: (0, g)),
                   pl.BlockSpec((ms, vw), lambda g, s: (0, g))),
        out_shape=(jax.ShapeDtypeStruct((BATCH * SEQ, W_C), BF16),
                   jax.ShapeDtypeStruct((BATCH, H_C, DV_C, DQK_C), F32),
                   jax.ShapeDtypeStruct((BATCH, H_C, 1, DQK_C), F32),
                   jax.ShapeDtypeStruct((BATCH, H_C, 1, 128), F32),
                   jax.ShapeDtypeStruct((ms, H_C * DQK_C), F32),
                   jax.ShapeDtypeStruct((ms, H_C * DQK_C), F32),
                   jax.ShapeDtypeStruct((ms, W_C), F32),
                   jax.ShapeDtypeStruct((ms, W_C), F32)),
        scratch_shapes=[pltpu.VMEM((CW, k), BF16),
                        pltpu.VMEM((CT, CW), F32),
                        pltpu.VMEM((CG, DV_C, DQK_C), F32),
                        pltpu.VMEM((CG, 1, DQK_C), F32),
                        pltpu.VMEM((CG, 8, 128), F32),
                        pltpu.VMEM((CT // CHUNK, CHUNK, CHUNK), F32)],
        compiler_params=_params(("arbitrary", "arbitrary")),
        name="odd_heads",
    )(hp, hs, w_t, w_t, w_t, w_t, w_t, bg, gm)


def _odd_mlp_kernel(p_ref, lng_ref, lnb_ref, ws_ref, bst_ref, yd_ref, wsb):
    L = CHUNK

    @pl.when(pl.program_id(0) == 0)
    def _():
        keep = (lax.broadcasted_iota(jnp.int32, (L, L), 0) >= lax.broadcasted_iota(jnp.int32, (L, L), 1))
        for g in range(G_D):
            wsb[g] = jnp.where(keep, ws_ref[g], 0.0).astype(BF16)

    def chunk(ci, carry):
        rs = pl.ds(pl.multiple_of(ci * L, L), L)
        dv = lambda g: p_ref[rs, W_D + g * 128:W_D + (g + 1) * 128]
        tot = dv(0)
        for g in range(1, G_D):
            tot = tot + dv(g)
        mu = jnp.sum(tot, axis=-1, keepdims=True) * (1.0 / W_D)
        sq = (dv(0) - mu) * (dv(0) - mu)
        for g in range(1, G_D):
            sq = sq + (dv(g) - mu) * (dv(g) - mu)
        rstd = lax.rsqrt(jnp.sum(sq, axis=-1, keepdims=True) * (1.0 / W_D) + EPS)
        for g in range(G_D):
            sl = slice(g * 128, (g + 1) * 128)
            vn = (dv(g) - mu) * rstd * lng_ref[0:1, sl] + lnb_ref[0:1, sl]
            s = _dot(wsb[g], vn.astype(BF16)) + bst_ref[:, g:g + 1]
            d_u = p_ref[rs, g * 128:(g + 1) * 128]
            d_z = p_ref[rs, 2 * W_D + g * 128:2 * W_D + (g + 1) * 128]
            yd_ref[rs, sl] = (d_u * s * _silu(d_z)).astype(BF16)
        return carry

    lax.fori_loop(0, p_ref.shape[0] // L, chunk, 0)


def _odd_mlp(p, lng, lnb, ws, bst, rows_per_step):
    m = p.shape[0]
    rows = lambda i: (i, 0)
    const2 = lambda i: (0, 0)
    return pl.pallas_call(
        _odd_mlp_kernel,
        grid=(m // rows_per_step,),
        in_specs=[pl.BlockSpec((rows_per_step, 3 * W_D), rows),
                  pl.BlockSpec((1, W_D), const2),
                  pl.BlockSpec((1, W_D), const2),
                  pl.BlockSpec((G_D, CHUNK, CHUNK), lambda i: (0, 0, 0)),
                  pl.BlockSpec((CHUNK, G_D), const2)],
        out_specs=pl.BlockSpec((rows_per_step, W_D), rows),
        out_shape=jax.ShapeDtypeStruct((m, W_D), BF16),
        scratch_shapes=[pltpu.VMEM((G_D, CHUNK, CHUNK), BF16)],
        compiler_params=_params(("arbitrary",)),
        name="odd_mlp",
    )(p, lng, lnb, ws, bst)


DT = 512


def _odd_mlp_fused_kernel(hp_ref, w0_ref, w1_ref, w2_ref, wt_ref, lng_ref, lnb_ref, ws_ref, bst_ref,
                          yd_ref, wb, pt, wsb):
    s = pl.program_id(0)
    L = CHUNK
    uc = slice(0, W_D)
    vc = slice(W_D, 2 * W_D)
    zc = slice(2 * W_D, 3 * W_D)

    @pl.when(s == 0)
    def _():
        sh = N_GATE
        wb[uc, :] = jnp.concatenate([w0_ref[sh:, :], w1_ref[0:sh, :]], axis=0).astype(BF16)
        wb[vc, :] = jnp.concatenate([w1_ref[sh:, :], w2_ref[0:sh, :]], axis=0).astype(BF16)
        wb[zc, :] = jnp.concatenate([w2_ref[sh:, :], wt_ref[...]], axis=0).astype(BF16)
        keep = (lax.broadcasted_iota(jnp.int32, (L, L), 0) >= lax.broadcasted_iota(jnp.int32, (L, L), 1))
        for g in range(G_D):
            wsb[g] = jnp.where(keep, ws_ref[g], 0.0).astype(BF16)

    @pl.when(s > 0)
    def _():
        n = DT // L
        rows = lambda c: slice(c * L, (c + 1) * L)
        grp = lambda sl, g: slice(sl.start + g * 128, sl.start + (g + 1) * 128)
        vn = {}
        mix = {}

        def project(pc):
            pt[:, pc] = _dot_nt(hp_ref[...], wb[pc, :])

        def stage_a(c):
            dv = lambda g: pt[rows(c), grp(vc, g)]
            tot = dv(0)
            for g in range(1, G_D):
                tot = tot + dv(g)
            mu = jnp.sum(tot, axis=-1, keepdims=True) * (1.0 / W_D)
            sq = (dv(0) - mu) * (dv(0) - mu)
            for g in range(1, G_D):
                sq = sq + (dv(g) - mu) * (dv(g) - mu)
            rstd = lax.rsqrt(jnp.sum(sq, axis=-1, keepdims=True) * (1.0 / W_D) + EPS)
            vn[c] = [((dv(g) - mu) * rstd * lng_ref[0:1, g * 128:(g + 1) * 128]
                      + lnb_ref[0:1, g * 128:(g + 1) * 128]).astype(BF16) for g in range(G_D)]

        def stage_b(c):
            mix[c] = [_dot(wsb[g], vn[c][g]) for g in range(G_D)]

        def stage_e(c):
            for g in range(G_D):
                sg = mix[c][g] + bst_ref[:, g:g + 1]
                d_u = pt[rows(c), grp(uc, g)]
                d_z = pt[rows(c), grp(zc, g)]
                yd_ref[rows(c), g * 128:(g + 1) * 128] = (d_u * sg * _silu(d_z)).astype(BF16)

        project(vc)
        for c in range(n):
            stage_a(c)
        project(uc)
        for c in range(n):
            stage_b(c)
        project(zc)
        for c in range(n):
            stage_e(c)


def _odd_mlp_fused(hp, w_t, lng, lnb, ws, bst):
    k = hp.shape[1]
    m = hp.shape[0]
    tile = lambda s: (jnp.maximum(s - 1, 0), 0)
    const2 = lambda s: (0, 0)
    t0 = O_GATE // IN_TN
    once = pl.Buffered(1)
    wspec = lambda j: pl.BlockSpec((IN_TN, k), lambda s: (t0 + j, 0), pipeline_mode=once)
    return pl.pallas_call(
        _odd_mlp_fused_kernel,
        grid=(m // DT + 1,),
        in_specs=[pl.BlockSpec((DT, k), tile),
                  wspec(0), wspec(1), wspec(2),
                  pl.BlockSpec((N_GATE, k), lambda s: ((t0 + 3) * (IN_TN // N_GATE), 0), pipeline_mode=once),
                  pl.BlockSpec((1, W_D), const2),
                  pl.BlockSpec((1, W_D), const2),
                  pl.BlockSpec((G_D, CHUNK, CHUNK), lambda s: (0, 0, 0)),
                  pl.BlockSpec((CHUNK, G_D), const2)],
        out_specs=pl.BlockSpec((DT, W_D), tile),
        out_shape=jax.ShapeDtypeStruct((m, W_D), BF16),
        scratch_shapes=[pltpu.VMEM((3 * W_D, k), BF16),
                        pltpu.VMEM((DT, 3 * W_D), F32),
                        pltpu.VMEM((G_D, CHUNK, CHUNK), BF16)],
        compiler_params=_params(("arbitrary",)),
        name="odd_mlp_fused",
    )(hp, w_t, w_t, w_t, w_t, lng, lnb, ws, bst)


def _in_proj_rows_kernel(h_ref, w_ref, wn_ref, o_ref, *, shift):
    wsh = jnp.concatenate([w_ref[shift:, :], wn_ref[...]], axis=0)
    o_ref[...] = _dot_nt(h_ref[...], wsh.astype(BF16))


def _in_proj_rows(h, w_t, n_out, shift, tile0):
    ms, k = h.shape
    return pl.pallas_call(
        functools.partial(_in_proj_rows_kernel, shift=shift),
        grid=(n_out // IN_TN,),
        in_specs=[pl.BlockSpec((ms, k), lambda j: (0, 0)),
                  pl.BlockSpec((IN_TN, k), lambda j: (j + tile0, 0)),
                  pl.BlockSpec((shift, k), lambda j: ((j + tile0 + 1) * (IN_TN // shift), 0))],
        out_specs=pl.BlockSpec((ms, IN_TN), lambda j: (0, j)),
        out_shape=jax.ShapeDtypeStruct((ms, n_out), F32),
        compiler_params=_params(("arbitrary",)),
        name="in_proj_rows",
    )(h, w_t, w_t)


def _odd_sample_kernel(pq_ref, pk_ref, pv_ref, pz_ref, h_ref, wg_ref, pd_ref,
                       c_ref, nrow_ref, mrow_ref, bg_ref, gm_ref, lng_ref, lnb_ref,
                       rtab_ref, btab_ref,
                       yc_ref, yd_ref, vn_ref, co_ref, no_ref, mo_ref,
                       inter_scr):
    h = pl.program_id(1)
    row = lax.broadcasted_iota(jnp.int32, (SR, SR), 0)
    col = lax.broadcasted_iota(jnp.int32, (SR, SR), 1)
    trow = row & 3

    @pl.when(h == 0)
    def _():
        dv = pd_ref[:, W_D:2 * W_D]
        mu = jnp.mean(dv, axis=-1, keepdims=True)
        xc = dv - mu
        var = jnp.mean(xc * xc, axis=-1, keepdims=True)
        rstd = lax.rsqrt(var + EPS)
        for g in range(G_D):
            sl = slice(g * 128, (g + 1) * 128)
            vn = xc[:, sl] * rstd * lng_ref[0:1, sl] + lnb_ref[0:1, sl]
            vn_ref[:, sl] = vn
            s = rtab_ref[0, :, sl] * vn + btab_ref[:, sl]
            for j in range(1, DEC_SEQ):
                s = s + jnp.where(trow >= j, rtab_ref[j, :, sl] * pltpu.roll(vn, j, 0), 0.0)
            d_u = pd_ref[:, g * 128:(g + 1) * 128]
            d_z = pd_ref[:, 2 * W_D + g * 128:2 * W_D + (g + 1) * 128]
            yd_ref[:, sl] = (d_u * s * _silu(d_z)).astype(BF16)

    same = (row >> 2) == (col >> 2)
    mask = jnp.where(same, trow - (col & 3), -1) >= 0
    pre = _dot_nt(h_ref[...], wg_ref[...].astype(BF16)) + bg_ref[...]
    lf = _log_sigmoid(pre)
    b_full = _dot_hi(jnp.where(mask, 1.0, 0.0), lf)
    sel_i = col == h
    sel_f = col == h + H_C
    ig_c = jnp.sum(jnp.where(sel_i, pre, 0.0), axis=-1, keepdims=True)
    b_c = jnp.sum(jnp.where(sel_f, b_full, 0.0), axis=-1, keepdims=True)
    sel_ir = row == h
    sel_fr = row == h + H_C
    ig_r = jnp.sum(jnp.where(sel_ir, pre.T, 0.0), axis=0, keepdims=True)
    b_r = jnp.sum(jnp.where(sel_fr, b_full.T, 0.0), axis=0, keepdims=True)
    m_prev = mrow_ref[0]
    log_d = jnp.where(mask, b_c - b_r + ig_r, NEG_INF)
    log_inter = b_c + m_prev
    m_t = jnp.maximum(log_inter, jnp.max(log_d, axis=-1, keepdims=True))
    w = jnp.exp(log_d - m_t)
    w_inter = jnp.exp(log_inter - m_t)
    q = pq_ref[...] * (DQK_C ** -0.5)
    k = pk_ref[...]
    v = pv_ref[...]
    qb = q.astype(BF16)
    kb = k.astype(BF16)
    vb = v.astype(BF16)
    sc = _dot_nt(qb, kb) * w
    sub = lax.broadcasted_iota(jnp.int32, (8, DV_C), 0)
    for g in range(SR // 8):
        q8 = qb[8 * g:8 * g + 8, :]
        r0 = _dot_nt(q8, c_ref[2 * g, 0].astype(BF16))
        r1 = _dot_nt(q8, c_ref[2 * g + 1, 0].astype(BF16))
        inter_scr[8 * g:8 * g + 8, :] = jnp.where(sub < DEC_SEQ, r0, r1)
    n_rows = nrow_ref[0]
    num = _dot(sc.astype(BF16), vb) + w_inter * inter_scr[...]
    den = jnp.sum(sc, axis=-1, keepdims=True) + w_inter * jnp.sum(q * n_rows, axis=-1, keepdims=True)
    hh = num / jnp.maximum(jnp.abs(den), jnp.exp(-m_t))
    yc_ref[...] = (_head_norm(hh, gm_ref[...]) * _silu(pz_ref[...])).astype(BF16)

    stats = jnp.where(col == 0, m_t, jnp.where(col == 1, b_c, 0.0))
    last = _dot_hi(jnp.where(col == (row | 3), 1.0, 0.0), stats)
    m_new = last[:, 0:1]
    b_last = last[:, 1:2]
    w_end = jnp.exp(b_last - b_c + ig_c - m_new)
    cd = jnp.exp(b_last + m_prev - m_new)
    mo_ref[0] = m_new
    no_ref[0] = cd * n_rows + _dot_hi(jnp.where(same, 1.0, 0.0), w_end * k)
    vwt = (v * w_end).T
    lane_b = lax.broadcasted_iota(jnp.int32, (DV_C, SR), 1) >> 2
    for b in range(SB):
        lhs = jnp.where(lane_b == b, vwt, 0.0).astype(BF16)
        cd_b = cd[4 * b + 3:4 * b + 4, :]
        co_ref[b, 0] = cd_b * c_ref[b, 0] + _dot(lhs, kb)


def _odd_sample(pc, pd, h, w_o, c_state, n_rows, m_rows, bg, gm, lng, lnb, rtab, btab):
    nb = DEC_BATCH // SB
    const2 = lambda i, h: (0, 0)
    return pl.pallas_call(
        _odd_sample_kernel,
        grid=(nb, H_C),
        in_specs=[pl.BlockSpec((SR, DQK_C), lambda i, h: (i, h)),
                  pl.BlockSpec((SR, DQK_C), lambda i, h: (i, h)),
                  pl.BlockSpec((SR, DV_C), lambda i, h: (i, h)),
                  pl.BlockSpec((SR, DV_C), lambda i, h: (i, h)),
                  pl.BlockSpec((SR, D_MODEL), lambda i, h: (i, 0)),
                  pl.BlockSpec((128, D_MODEL), lambda i, h: (O_GATE // 128, 0)),
                  pl.BlockSpec((SR, 3 * W_D), lambda i, h: (i, 0)),
                  pl.BlockSpec((SB, 1, DV_C, DQK_C), lambda i, h: (i, h, 0, 0)),
                  pl.BlockSpec((1, SR, DQK_C), lambda i, h: (h, i, 0)),
                  pl.BlockSpec((1, SR, 1), lambda i, h: (h, i, 0)),
                  pl.BlockSpec((1, 128), const2),
                  pl.BlockSpec((1, DV_C), lambda i, h: (0, h)),
                  pl.BlockSpec((1, W_D), const2),
                  pl.BlockSpec((1, W_D), const2),
                  pl.BlockSpec((DEC_SEQ, SR, W_D), lambda i, h: (0, 0, 0)),
                  pl.BlockSpec((SR, W_D), const2)],
        out_specs=(pl.BlockSpec((SR, DV_C), lambda i, h: (i, h)),
                   pl.BlockSpec((SR, W_D), lambda i, h: (i, 0)),
                   pl.BlockSpec((SR, W_D), lambda i, h: (i, 0)),
                   pl.BlockSpec((SB, 1, DV_C, DQK_C), lambda i, h: (i, h, 0, 0)),
                   pl.BlockSpec((1, SR, DQK_C), lambda i, h: (h, i, 0)),
                   pl.BlockSpec((1, SR, 1), lambda i, h: (h, i, 0))),
        out_shape=(jax.ShapeDtypeStruct((DEC_BATCH * DEC_SEQ, W_C), BF16),
                   jax.ShapeDtypeStruct((DEC_BATCH * DEC_SEQ, W_D), BF16),
                   jax.ShapeDtypeStruct((DEC_BATCH * DEC_SEQ, W_D), F32),
                   jax.ShapeDtypeStruct((DEC_BATCH, H_C, DV_C, DQK_C), F32),
                   jax.ShapeDtypeStruct((H_C, DEC_BATCH * DEC_SEQ, DQK_C), F32),
                   jax.ShapeDtypeStruct((H_C, DEC_BATCH * DEC_SEQ, 1), F32)),
        scratch_shapes=[pltpu.VMEM((SR, DV_C), F32)],
        compiler_params=_params(("arbitrary", "arbitrary")),
        name="odd_sample",
    )(*pc, h, w_o, pd, c_state, n_rows, m_rows, bg, gm, lng, lnb, rtab, btab)


def _rope_tables(pos):
    inv = ROPE_BASE ** (-jnp.arange(0, DH_B, 2, dtype=F32) / DH_B)
    ang = pos.astype(F32)[:, None] * inv[None, :]
    cos = jnp.cos(ang)
    sin = jnp.sin(ang)
    return jnp.concatenate([cos, cos], axis=-1), jnp.concatenate([-sin, sin], axis=-1)


def kernel(x_prompt, x_sample, state_conv, state_ret, state_mlstm_C, state_mlstm_n, state_mlstm_m,
           norm_even, w_in_even, conv_w, ret_norm, w_out_even,
           norm_odd, w_in_odd, b_gate_odd, mlstm_norm, ln_v_g, ln_v_b,
           w_spatial, b_spatial, w_out_odd, norm_final):
    w_in_e = w_in_even[0]
    w_out_e = w_out_even[0].astype(BF16)
    w_o = w_in_odd[0].T
    w_out_o = w_out_odd[0].astype(BF16)
    g_even = norm_even[0][None, :]
    g_odd = norm_odd[0][None, :]
    g_fin = norm_final[None, :]
    cw = conv_w[0]
    g_ret = ret_norm[0][None, :]
    bg = jnp.concatenate([b_gate_odd[0], jnp.zeros((128 - 2 * H_C,), F32)])[None, :]
    gm = mlstm_norm[0][None, :]
    lng = ln_v_g[0][None, :]
    lnb = ln_v_b[0][None, :]
    ws = w_spatial[0]
    bst = b_spatial[0].T

    cos_p, sin_p = _rope_tables(jnp.arange(SEQ, dtype=jnp.int32))
    cos_s, sin_s = _rope_tables(PAST_LEN + jnp.arange(DEC_SEQ, dtype=jnp.int32))
    cos_s = jnp.tile(cos_s, (SB, 1))
    sin_s = jnp.tile(sin_s, (SB, 1))
    lg_tab = jnp.broadcast_to(jnp.asarray(LOG_GAMMA, F32)[:, None, None], (H_B, 1, 128))

    ws4 = ws[:, :DEC_SEQ, :DEC_SEQ]
    t_idx = jnp.arange(DEC_SEQ)
    rtab = []
    for j in range(DEC_SEQ):
        coef = ws4[:, t_idx, (t_idx - j) % DEC_SEQ]
        tab = jnp.repeat(coef.T[:, :, None], 128, axis=2).reshape(DEC_SEQ, W_D)
        rtab.append(jnp.tile(tab, (SB, 1)))
    rtab = jnp.stack(rtab)
    btab = jnp.tile(jnp.repeat(b_spatial[0][:, :DEC_SEQ].T[:, :, None], 128, axis=2)
                    .reshape(DEC_SEQ, W_D), (SB, 1))

    rs = DEC_BATCH * DEC_SEQ
    xp = x_prompt.reshape(BATCH * SEQ, D_MODEL)
    xs = x_sample.reshape(rs, D_MODEL)
    hp = _norm_cast(xp, g_even, 512)
    hs = _norm_cast(xs, g_even, 512)
    ya, conv_p, *ps_a = _even_conv(hp, hs, w_in_e, cw)
    yb, ret_p, *ps_b = _even_heads(hp, hs, w_in_e, g_ret, cos_p, sin_p, lg_tab)
    st_exp = jnp.pad(state_conv[0], ((0, 0), (0, DEC_SEQ - (CONV_W - 1)), (0, 0))).reshape(rs, W_A)
    ya_s, u_s, yb_s, ret_s = _even_sample(ps_a, ps_b, st_exp, state_ret[0], cw, g_ret, cos_s, sin_s, lg_tab)
    x1, h1 = _outproj(ya, yb, w_out_e, xp, g_odd, 512, final=False)
    x1s, h1s = _outproj(ya_s, yb_s, w_out_e, xs, g_odd, 512, final=False)

    yc, c_p, n_p, m_p, *ps_c = _odd_heads(h1, h1s, w_o, bg, gm)
    yd = _odd_mlp_fused(h1, w_o, lng, lnb, ws, bst)
    pd_s = _in_proj_rows(h1s, w_o, 3 * W_D, N_GATE, O_GATE // IN_TN)
    n_rows = jnp.repeat(jnp.transpose(state_mlstm_n[0], (1, 0, 2)), DEC_SEQ, axis=1)
    m_rows = jnp.repeat(state_mlstm_m[0].T, DEC_SEQ, axis=1)[:, :, None]
    yc_s, yd_s, vn_s, c_s, no_s, mo_s = _odd_sample(
        ps_c, pd_s, h1s, w_o, state_mlstm_C[0], n_rows, m_rows, bg, gm, lng, lnb, rtab, btab)
    y_prompt = _outproj(yc, yd, w_out_o, x1, g_fin, 512, final=True)
    y_sample = _outproj(yc_s, yd_s, w_out_o, x1s, g_fin, 512, final=True)

    conv_s = u_s.reshape(DEC_BATCH, DEC_SEQ, W_A)[:, DEC_SEQ - (CONV_W - 1):, :]
    n_s = jnp.transpose(no_s[:, DEC_SEQ - 1::DEC_SEQ, :], (1, 0, 2))
    m_s = mo_s[:, DEC_SEQ - 1::DEC_SEQ, 0].T
    return (y_prompt.reshape(BATCH, SEQ, D_MODEL),
            y_sample.reshape(DEC_BATCH, DEC_SEQ, D_MODEL),
            conv_p[None], conv_s[None],
            ret_p[None], ret_s[None],
            c_p[None], c_s[None],
            n_p[:, :, 0, :][None], n_s[None],
            m_p[:, :, 0, 0][None], m_s[None],
            vn_s.reshape(DEC_BATCH, DEC_SEQ, W_D)[None])
```

```python
import functools
import math

import jax
import jax.numpy as jnp
from jax import lax
from jax.experimental import pallas as pl
from jax.experimental.pallas import tpu as pltpu

F32 = jnp.float32
BF16 = jnp.bfloat16

D_MODEL = 2048
BATCH = 4
SEQ = 2048
DEC_BATCH = 128
DEC_SEQ = 4
PAST_LEN = 16384
W_A = 1024
CONV_W = 3
W_B = 1024
H_B = 8
DH_B = 128
E_IN = 8192
W_C = 1024
H_C = 4
DV_C = 256
DQK_C = 128
W_D = 1024
G_D = 8
CHUNK = 128
O_GATE = 2 * H_C * DQK_C + 2 * W_C
N_GATE = 2 * H_C
O_N = O_GATE + 3 * W_D
ROPE_BASE = 10000.0
EPS = 1e-6
LOG_GAMMA = tuple(math.log(1.0 - 2.0 ** (-5.0 - h)) for h in range(H_B))
NEG_INF = float("-inf")
VMEM_LIMIT = 56 * 1024 * 1024

NT_DIMS = (((1,), (1,)), ((), ()))
TN_DIMS = (((0,), (0,)), ((), ()))


def _silu(z):
    return z * (1.0 / (1.0 + jnp.exp(-z)))


def _log_sigmoid(x):
    return jnp.minimum(x, 0.0) - jnp.log1p(jnp.exp(-jnp.abs(x)))


def _dot(a, b):
    return jnp.dot(a, b, preferred_element_type=F32)


def _dot_nt(a, b):
    return lax.dot_general(a, b, NT_DIMS, preferred_element_type=F32)


def _dot_tn(a, b):
    return lax.dot_general(a, b, TN_DIMS, preferred_element_type=F32)


def _dot_hi(a, b):
    return jnp.dot(a, b, preferred_element_type=F32, precision=lax.Precision.HIGHEST)


def _head_norm(o, g):
    mu = jnp.mean(o, axis=-1, keepdims=True)
    oc = o - mu
    var = jnp.mean(oc * oc, axis=-1, keepdims=True)
    return oc * lax.rsqrt(var + EPS) * g


def _params(sem):
    return pltpu.CompilerParams(dimension_semantics=sem, vmem_limit_bytes=VMEM_LIMIT)


def _norm_cast_kernel(x_ref, g_ref, h_ref):
    x = x_ref[...]
    ms = jnp.mean(x * x, axis=-1, keepdims=True)
    h_ref[...] = (x * lax.rsqrt(ms + EPS) * g_ref[...]).astype(BF16)


def _norm_cast(x, g, tm):
    m, d = x.shape
    return pl.pallas_call(
        _norm_cast_kernel,
        grid=(m // tm,),
        in_specs=[pl.BlockSpec((tm, d), lambda i: (i, 0)),
                  pl.BlockSpec((1, d), lambda i: (0, 0))],
        out_specs=pl.BlockSpec((tm, d), lambda i: (i, 0)),
        out_shape=jax.ShapeDtypeStruct((m, d), BF16),
        compiler_params=_params(("arbitrary",)),
        name="norm_cast",
    )(x, g)


IN_TM = 1024
IN_TN = 1024


def _in_proj_kernel(*refs, shift_from, shift):
    if shift:
        hp_ref, hs_ref, w_ref, wn_ref, op_ref, os_ref, wb = refs
    else:
        hp_ref, hs_ref, w_ref, op_ref, os_ref, wb = refs
    j = pl.program_id(0)
    i = pl.program_id(1)

    if shift:
        @pl.when(jnp.logical_and(i == 0, j < shift_from))
        def _():
            wb[...] = w_ref[...].astype(BF16)

        @pl.when(jnp.logical_and(i == 0, j >= shift_from))
        def _():
            wb[...] = jnp.concatenate([w_ref[shift:IN_TN, :], wn_ref[...]], axis=0).astype(BF16)

        mm = _dot_nt
    else:
        @pl.when(i == 0)
        def _():
            wb[...] = w_ref[...].astype(BF16)

        mm = _dot

    @pl.when(i == 0)
    def _():
        os_ref[...] = mm(hs_ref[...], wb[...])

    @pl.when(i > 0)
    def _():
        op_ref[...] = mm(hp_ref[...], wb[...])


def _in_proj(hp, hs, w, n_out, shift_from=0, shift=0, tile0=0):
    mp, k = hp.shape
    ms = hs.shape[0]
    n_prompt = mp // IN_TM
    prow = lambda j, i: (jnp.maximum(i - 1, 0), 0)
    in_specs = [pl.BlockSpec((IN_TM, k), prow),
                pl.BlockSpec((ms, k), lambda j, i: (0, 0))]
    args = [hp, hs, w]
    if shift:
        in_specs.append(pl.BlockSpec((IN_TN, k), lambda j, i: (j + tile0, 0)))
        in_specs.append(pl.BlockSpec((shift, k), lambda j, i: ((j + tile0 + 1) * (IN_TN // shift), 0)))
        args.append(w)
        wb_shape = (IN_TN, k)
    else:
        in_specs.append(pl.BlockSpec((k, IN_TN), lambda j, i: (0, j)))
        wb_shape = (k, IN_TN)
    return pl.pallas_call(
        functools.partial(_in_proj_kernel, shift_from=shift_from, shift=shift),
        grid=(n_out // IN_TN, n_prompt + 1),
        in_specs=in_specs,
        out_specs=(pl.BlockSpec((IN_TM, IN_TN), lambda j, i: (jnp.maximum(i - 1, 0), j)),
                   pl.BlockSpec((ms, IN_TN), lambda j, i: (0, j))),
        out_shape=(jax.ShapeDtypeStruct((mp, n_out), F32),
                   jax.ShapeDtypeStruct((ms, n_out), F32)),
        scratch_shapes=[pltpu.VMEM(wb_shape, BF16)],
        compiler_params=_params(("arbitrary", "arbitrary")),
        name="in_proj",
    )(*args)


def _outproj_kernel(ya_ref, yb_ref, w_ref, x_ref, g_ref, *out_refs, final):
    half = ya_ref.shape[1]
    acc = _dot(ya_ref[...], w_ref[0:half, :]) + _dot(yb_ref[...], w_ref[half:2 * half, :])
    x1 = x_ref[...] + acc
    ms = jnp.mean(x1 * x1, axis=-1, keepdims=True)
    hn = x1 * lax.rsqrt(ms + EPS) * g_ref[...]
    if final:
        out_refs[0][...] = hn
    else:
        out_refs[0][...] = x1
        out_refs[1][...] = hn.astype(BF16)


def _outproj(ya, yb, w, x, g, tm, final):
    m, half = ya.shape
    d = w.shape[1]
    row = lambda i: (i, 0)
    if final:
        out_shape = jax.ShapeDtypeStruct((m, d), F32)
        out_specs = pl.BlockSpec((tm, d), row)
    else:
        out_shape = (jax.ShapeDtypeStruct((m, d), F32), jax.ShapeDtypeStruct((m, d), BF16))
        out_specs = (pl.BlockSpec((tm, d), row), pl.BlockSpec((tm, d), row))
    return pl.pallas_call(
        functools.partial(_outproj_kernel, final=final),
        grid=(m // tm,),
        in_specs=[pl.BlockSpec((tm, half), row),
                  pl.BlockSpec((tm, half), row),
                  pl.BlockSpec((2 * half, d), lambda i: (0, 0)),
                  pl.BlockSpec((tm, d), row),
                  pl.BlockSpec((1, d), lambda i: (0, 0))],
        out_specs=out_specs,
        out_shape=out_shape,
        compiler_params=_params(("arbitrary",)),
        name="out_proj_final" if final else "out_proj",
    )(ya, yb, w, x, g)


MIX_ROWS = 512


def _rope(x, cosf, sins):
    return x * cosf + pltpu.roll(x, DH_B // 2, 1) * sins


def _even_prompt_kernel(p_ref, cw_ref, gret_ref, cos_ref, sin_ref,
                        ya_ref, yb_ref, conv_ref, s_ref, ubuf):
    L = CHUNK

    @pl.when(pl.program_id(1) == 0)
    def _():
        ubuf[0:8, :] = jnp.zeros((8, W_A), F32)
        s_ref[...] = jnp.zeros_like(s_ref)

    def chunk(ci, carry):
        rs = pl.ds(pl.multiple_of(ci * L, L), L)

        for j in range(W_A // 128):
            sl = slice(j * 128, (j + 1) * 128)
            a_b = p_ref[rs, j * 128:(j + 1) * 128]
            a_c = p_ref[rs, W_A + j * 128:W_A + (j + 1) * 128]
            a_x = p_ref[rs, 2 * W_A + j * 128:2 * W_A + (j + 1) * 128]
            a_z = p_ref[rs, 3 * W_A + j * 128:3 * W_A + (j + 1) * 128]
            u = a_c * a_x
            ubuf[8:8 + L, sl] = u
            t0 = ubuf[6:6 + L, sl]
            t1 = ubuf[7:7 + L, sl]
            conv = cw_ref[0:1, sl] * t0 + cw_ref[1:2, sl] * t1 + cw_ref[2:3, sl] * u
            ya_ref[rs, sl] = (a_b * conv * _silu(a_z)).astype(BF16)
            ubuf[0:8, sl] = u[L - 8:L, :]

        cosf = cos_ref[rs, :]
        sins = sin_ref[rs, :]
        row = lax.broadcasted_iota(jnp.int32, (L, L), 0)
        col = lax.broadcasted_iota(jnp.int32, (L, L), 1)
        causal = row >= col
        diff = jnp.maximum(row - col, 0).astype(F32)
        ti = lax.broadcasted_iota(jnp.int32, (L, 1), 0).astype(F32)
        base = 4 * W_A
        for h in range(H_B):
            lg = LOG_GAMMA[h]
            sl = slice(h * DH_B, (h + 1) * DH_B)
            q = p_ref[rs, base + h * DH_B:base + (h + 1) * DH_B]
            k = p_ref[rs, base + W_B + h * DH_B:base + W_B + (h + 1) * DH_B]
            v = p_ref[rs, base + 2 * W_B + h * DH_B:base + 2 * W_B + (h + 1) * DH_B]
            z = p_ref[rs, base + 3 * W_B + h * DH_B:base + 3 * W_B + (h + 1) * DH_B]
            qr = _rope(q, cosf, sins)
            kr = _rope(k, cosf, sins) * (DH_B ** -0.5)
            decay = jnp.where(causal, jnp.exp(lg * diff), 0.0)
            qb = qr.astype(BF16)
            kb = kr.astype(BF16)
            vb = v.astype(BF16)
            sc = _dot_nt(qb, kb) * decay
            inner = _dot(sc.astype(BF16), vb)
            s_old = s_ref[0, h]
            cross = _dot(qb, s_old.astype(BF16)) * jnp.exp(lg * (ti + 1.0))
            kd = (kr * jnp.exp(lg * (L - 1.0 - ti))).astype(BF16)
            s_ref[0, h] = math.exp(lg * L) * s_old + _dot_tn(kd, vb)
            o = inner + cross
            yb_ref[rs, sl] = (_head_norm(o, gret_ref[0:1, sl]) * _silu(z)).astype(BF16)
        return carry

    lax.fori_loop(0, MIX_ROWS // L, chunk, 0)
    conv_ref[0] = ubuf[6:8, :]


def _even_prompt(p, conv_w, g_ret, cosf, sins):
    nc = SEQ // MIX_ROWS
    rows = lambda b, c: (b * nc + c, 0)
    const2 = lambda b, c: (0, 0)
    return pl.pallas_call(
        _even_prompt_kernel,
        grid=(BATCH, nc),
        in_specs=[pl.BlockSpec((MIX_ROWS, E_IN), rows),
                  pl.BlockSpec((CONV_W, W_A), const2),
                  pl.BlockSpec((1, W_B), const2),
                  pl.BlockSpec((MIX_ROWS, DH_B), lambda b, c: (c, 0)),
                  pl.BlockSpec((MIX_ROWS, DH_B), lambda b, c: (c, 0))],
        out_specs=(pl.BlockSpec((MIX_ROWS, W_A), rows),
                   pl.BlockSpec((MIX_ROWS, W_B), rows),
                   pl.BlockSpec((1, CONV_W - 1, W_A), lambda b, c: (b, 0, 0)),
                   pl.BlockSpec((1, H_B, DH_B, DH_B), lambda b, c: (b, 0, 0, 0))),
        out_shape=(jax.ShapeDtypeStruct((BATCH * SEQ, W_A), BF16),
                   jax.ShapeDtypeStruct((BATCH * SEQ, W_B), BF16),
                   jax.ShapeDtypeStruct((BATCH, CONV_W - 1, W_A), F32),
                   jax.ShapeDtypeStruct((BATCH, H_B, DH_B, DH_B), F32)),
        scratch_shapes=[pltpu.VMEM((CHUNK + 8, W_A), F32)],
        compiler_params=_params(("arbitrary", "arbitrary")),
        name="even_prompt",
    )(p, conv_w, g_ret, cosf, sins)


FT = 1024
FN = BATCH * SEQ // FT
FPB = SEQ // FT
HG = 2
GW = HG * 128
PCH = 2
CONV_PCH = 4


def _chunk_pipeline(n, piece, stages, gate=None):
    sa, sb, sc, sd, se = stages
    npieces = n // PCH
    for j in range(PCH):
        piece(0, j)
    if gate is not None:
        gate(0)
    for c in range(n + 2):
        k, j = c // PCH + 1, c % PCH
        if k < npieces:
            piece(k, j)
        if c < n:
            sa(c)
        if 1 <= c <= n:
            sc(c - 1)
        if c < n:
            sb(c)
        if 1 <= c <= n:
            sd(c - 1)
        if c >= 2:
            se(c - 2)
        if gate is not None and j == PCH - 1 and k < npieces:
            gate(k)


def _even_heads_kernel(hp_ref, hs_ref, wq_ref, wk_ref, wv_ref, wz_ref, gret_ref, cos_ref, sin_ref, lg_ref,
                       yb_ref, s_ref, sq_ref, sk_ref, sv_ref, sz_ref, wb, pt, s_scr):
    s = pl.program_id(1)
    L = CHUNK

    @pl.when(s == 0)
    def _():
        for part, w_ref in enumerate((wq_ref, wk_ref, wv_ref, wz_ref)):
            wb[:, part * GW:(part + 1) * GW] = w_ref[...].astype(BF16)
        ps = _dot(hs_ref[...], wb[...])
        for part, o_ref in enumerate((sq_ref, sk_ref, sv_ref, sz_ref)):
            o_ref[...] = ps[:, part * GW:(part + 1) * GW]

    @pl.when(s > 0)
    def _():
        t = s - 1
        n = FT // L
        rows = lambda c: slice(c * L, (c + 1) * L)
        cols = lambda part, i: slice(part * GW + i * DH_B, part * GW + (i + 1) * DH_B)

        def piece(k, j):
            pr = slice(k * PCH * L, (k + 1) * PCH * L)
            pc = slice(j * 2 * GW, (j + 1) * 2 * GW)
            pt[pr, pc] = _dot(hp_ref[pr, :], wb[:, pc])

        row = lax.broadcasted_iota(jnp.int32, (L, L), 0)
        col = lax.broadcasted_iota(jnp.int32, (L, L), 1)
        causal = row >= col
        diff = jnp.maximum(row - col, 0).astype(F32)
        ti = lax.broadcasted_iota(jnp.int32, (L, 1), 0).astype(F32)
        lgs = [lg_ref[i][:, 0:1] for i in range(HG)]
        decay = [jnp.where(causal, jnp.exp(lg * diff), 0.0) for lg in lgs]
        q_decay = [jnp.exp(lg * (ti + 1.0)) for lg in lgs]
        k_decay = [jnp.exp(lg * (L - 1.0 - ti)) for lg in lgs]
        gamma_l = [jnp.exp(lg * float(L)) for lg in lgs]
        state = {0: [jnp.where(t % FPB == 0, 0.0, s_scr[i]) for i in range(HG)]}
        v = {}

        def stage_a(c):
            cosf = cos_ref[rows(c), :]
            sins = sin_ref[rows(c), :]
            v[c] = []
            for i in range(HG):
                kr = _rope(pt[rows(c), cols(1, i)], cosf, sins) * (DH_B ** -0.5)
                v[c].append(dict(qb=_rope(pt[rows(c), cols(0, i)], cosf, sins).astype(BF16),
                                 kb=kr.astype(BF16),
                                 kd=(kr * k_decay[i]).astype(BF16),
                                 vb=pt[rows(c), cols(2, i)].astype(BF16)))

        def stage_b(c):
            for i, d in enumerate(v[c]):
                d["sc"] = _dot_nt(d["qb"], d["kb"])
                d["cross"] = _dot(d["qb"], state[c][i].astype(BF16))
                d["upd"] = _dot_tn(d["kd"], d["vb"])

        def stage_c(c):
            state[c + 1] = []
            for i, d in enumerate(v[c]):
                d["sc"] = (d["sc"] * decay[i]).astype(BF16)
                state[c + 1].append(gamma_l[i] * state[c][i] + d["upd"])

        def stage_d(c):
            for d in v[c]:
                d["inner"] = _dot(d["sc"], d["vb"])

        def stage_e(c):
            for i, d in enumerate(v.pop(c)):
                o = d["inner"] + d["cross"] * q_decay[i]
                g = gret_ref[0:1, i * DH_B:(i + 1) * DH_B]
                z = pt[rows(c), cols(3, i)]
                yb_ref[rows(c), i * DH_B:(i + 1) * DH_B] = (_head_norm(o, g) * _silu(z)).astype(BF16)

        _chunk_pipeline(n, piece, (stage_a, stage_b, stage_c, stage_d, stage_e))
        for i in range(HG):
            s_scr[i] = state[n][i]
            s_ref[0, i] = state[n][i]


def _even_heads(hp, hs, w, g_ret, cosf, sins, lg_tab):
    k = hp.shape[1]
    ms = hs.shape[0]
    ng = H_B // HG
    base = 4 * W_A // GW
    tile = lambda s: jnp.maximum(s - 1, 0)
    wspec = lambda part: pl.BlockSpec((k, GW), lambda g, s: (0, base + part * ng + g))
    sspec = pl.BlockSpec((ms, GW), lambda g, s: (0, g))
    sshape = jax.ShapeDtypeStruct((ms, W_B), F32)
    return pl.pallas_call(
        _even_heads_kernel,
        grid=(ng, FN + 1),
        in_specs=[pl.BlockSpec((FT, k), lambda g, s: (tile(s), 0)),
                  pl.BlockSpec((ms, k), lambda g, s: (0, 0)),
                  wspec(0), wspec(1), wspec(2), wspec(3),
                  pl.BlockSpec((1, GW), lambda g, s: (0, g)),
                  pl.BlockSpec((FT, DH_B), lambda g, s: (tile(s) % FPB, 0)),
                  pl.BlockSpec((FT, DH_B), lambda g, s: (tile(s) % FPB, 0)),
                  pl.BlockSpec((HG, 1, 128), lambda g, s: (g, 0, 0))],
        out_specs=(pl.BlockSpec((FT, GW), lambda g, s: (tile(s), g)),
                   pl.BlockSpec((1, HG, DH_B, DH_B), lambda g, s: (tile(s) // FPB, g, 0, 0)),
                   sspec, sspec, sspec, sspec),
        out_shape=(jax.ShapeDtypeStruct((BATCH * SEQ, W_B), BF16),
                   jax.ShapeDtypeStruct((BATCH, H_B, DH_B, DH_B), F32),
                   sshape, sshape, sshape, sshape),
        scratch_shapes=[pltpu.VMEM((k, 4 * GW), BF16),
                        pltpu.VMEM((FT, 4 * GW), F32),
                        pltpu.VMEM((HG, DH_B, DH_B), F32)],
        compiler_params=_params(("arbitrary", "arbitrary")),
        name="even_heads",
    )(hp, hs, w, w, w, w, g_ret, cosf, sins, lg_tab)


def _even_conv_kernel(hp_ref, hs_ref, wb_ref, wc_ref, wx_ref, wz_ref, cw_ref,
                      ya_ref, conv_ref, sb_ref, sc_ref, sx_ref, sz_ref, wb, pt, ubuf):
    s = pl.program_id(1)
    L = CHUNK

    @pl.when(s == 0)
    def _():
        for part, w_ref in enumerate((wb_ref, wc_ref, wx_ref, wz_ref)):
            wb[:, part * GW:(part + 1) * GW] = w_ref[...].astype(BF16)
        ps = _dot(hs_ref[...], wb[...])
        for part, o_ref in enumerate((sb_ref, sc_ref, sx_ref, sz_ref)):
            o_ref[...] = ps[:, part * GW:(part + 1) * GW]

    @pl.when(s > 0)
    def _():
        t = s - 1
        n = FT // L
        rows = lambda c: slice(c * L, (c + 1) * L)
        part = lambda p, c: pt[rows(c), p * GW:(p + 1) * GW]

        @pl.when(t % FPB == 0)
        def _():
            ubuf[0:8, :] = jnp.zeros((8, GW), F32)

        @pl.when(t % FPB != 0)
        def _():
            ubuf[0:8, :] = ubuf[FT:FT + 8, :]

        def piece(k, j):
            pr = slice(k * CONV_PCH * L, (k + 1) * CONV_PCH * L)
            pc = slice(j * 2 * GW, (j + 1) * 2 * GW)
            pt[pr, pc] = _dot(hp_ref[pr, :], wb[:, pc])

        todo = [(k, j) for k in range(n // CONV_PCH) for j in range(2)]
        piece(*todo.pop(0))
        piece(*todo.pop(0))
        for c in range(n):
            if todo:
                piece(*todo.pop(0))
            u = part(1, c) * part(2, c)
            ubuf[8 + c * L:8 + (c + 1) * L, :] = u
            t0 = ubuf[6 + c * L:6 + (c + 1) * L, :]
            t1 = ubuf[7 + c * L:7 + (c + 1) * L, :]
            conv = cw_ref[0:1, :] * t0 + cw_ref[1:2, :] * t1 + cw_ref[2:3, :] * u
            ya_ref[rows(c), :] = (part(0, c) * conv * _silu(part(3, c))).astype(BF16)
        conv_ref[0] = ubuf[FT + 6:FT + 8, :]


def _even_conv(hp, hs, w, conv_w):
    k = hp.shape[1]
    ms = hs.shape[0]
    ng = W_A // GW
    tile = lambda s: jnp.maximum(s - 1, 0)
    wspec = lambda part: pl.BlockSpec((k, GW), lambda g, s: (0, part * ng + g))
    sspec = pl.BlockSpec((ms, GW), lambda g, s: (0, g))
    sshape = jax.ShapeDtypeStruct((ms, W_A), F32)
    return pl.pallas_call(
        _even_conv_kernel,
        grid=(ng, FN + 1),
        in_specs=[pl.BlockSpec((FT, k), lambda g, s: (tile(s), 0)),
                  pl.BlockSpec((ms, k), lambda g, s: (0, 0)),
                  wspec(0), wspec(1), wspec(2), wspec(3),
                  pl.BlockSpec((CONV_W, GW), lambda g, s: (0, g))],
        out_specs=(pl.BlockSpec((FT, GW), lambda g, s: (tile(s), g)),
                   pl.BlockSpec((1, CONV_W - 1, GW), lambda g, s: (tile(s) // FPB, 0, g)),
                   sspec, sspec, sspec, sspec),
        out_shape=(jax.ShapeDtypeStruct((BATCH * SEQ, W_A), BF16),
                   jax.ShapeDtypeStruct((BATCH, CONV_W - 1, W_A), F32),
                   sshape, sshape, sshape, sshape),
        scratch_shapes=[pltpu.VMEM((k, 4 * GW), BF16),
                        pltpu.VMEM((FT, 4 * GW), F32),
                        pltpu.VMEM((FT + 8, GW), F32)],
        compiler_params=_params(("arbitrary", "arbitrary")),
        name="even_conv",
    )(hp, hs, w, w, w, w, conv_w)


SB = 32
SR = SB * DEC_SEQ


def _even_sample_kernel(ab_ref, ac_ref, ax_ref, az_ref, pq_ref, pk_ref, pv_ref, pz_ref, st_ref, s_ref,
                        cw_ref, gret_ref, cos_ref, sin_ref, lg_ref,
                        ya_ref, u_ref, yb_ref, so_ref, cross_scr):
    h = pl.program_id(1)
    row = lax.broadcasted_iota(jnp.int32, (SR, SR), 0)
    col = lax.broadcasted_iota(jnp.int32, (SR, SR), 1)
    trow = row & 3

    @pl.when(h == 0)
    def _():
        for j in range(W_A // 128):
            sl = slice(j * 128, (j + 1) * 128)
            a_b = ab_ref[:, sl]
            a_c = ac_ref[:, sl]
            a_x = ax_ref[:, sl]
            a_z = az_ref[:, sl]
            u = a_c * a_x
            e = st_ref[:, sl]
            tap1 = jnp.where(trow >= 1, pltpu.roll(u, 1, 0), pltpu.roll(e, SR - 1, 0))
            tap0 = jnp.where(trow >= 2, pltpu.roll(u, 2, 0), e)
            conv = cw_ref[0:1, sl] * tap0 + cw_ref[1:2, sl] * tap1 + cw_ref[2:3, sl] * u
            ya_ref[:, sl] = (a_b * conv * _silu(a_z)).astype(BF16)
            u_ref[:, sl] = u

    lg = lg_ref[0][:, 0:1]
    same = (row >> 2) == (col >> 2)
    dd = trow - (col & 3)
    mask = jnp.where(same, dd, -1) >= 0
    decay = jnp.where(mask, jnp.exp(lg * jnp.maximum(dd, 0).astype(F32)), 0.0)
    tcol = (lax.broadcasted_iota(jnp.int32, (SR, 1), 0) & 3).astype(F32)
    cosf = cos_ref[...]
    sins = sin_ref[...]
    qr = _rope(pq_ref[...], cosf, sins)
    kr = _rope(pk_ref[...], cosf, sins) * (DH_B ** -0.5)
    qb = qr.astype(BF16)
    kb = kr.astype(BF16)
    vb = pv_ref[...].astype(BF16)
    sc = _dot_nt(qb, kb) * decay
    inner = _dot(sc.astype(BF16), vb)
    kdt = (kr * jnp.exp(lg * (DEC_SEQ - 1.0 - tcol))).T
    gamma_l = jnp.exp(lg * float(DEC_SEQ))
    lane_b = col >> 2
    sub = lax.broadcasted_iota(jnp.int32, (8, DH_B), 0)
    for g in range(SR // 8):
        q8 = qr[8 * g:8 * g + 8, :]
        q2 = jnp.concatenate([jnp.where(sub < DEC_SEQ, q8, 0.0), jnp.where(sub < DEC_SEQ, 0.0, q8)], axis=1)
        s_pair = [s_ref[2 * g + beta, 0] for beta in range(2)]
        cross_scr[8 * g:8 * g + 8, :] = _dot(
            q2.astype(BF16), jnp.concatenate([sp.astype(BF16) for sp in s_pair], axis=0))
        for beta in range(2):
            b = 2 * g + beta
            lhs = jnp.where(lane_b == b, kdt, 0.0).astype(BF16)
            so_ref[b, 0] = gamma_l * s_pair[beta] + _dot(lhs, vb)
    o = inner + cross_scr[...] * jnp.exp(lg * (tcol + 1.0))
    yb_ref[...] = (_head_norm(o, gret_ref[...]) * _silu(pz_ref[...])).astype(BF16)


def _even_sample(pa, pb, st_exp, s_state, conv_w, g_ret, cosf, sins, lg_tab):
    nb = DEC_BATCH // SB
    const2 = lambda i, h: (0, 0)
    aspec = pl.BlockSpec((SR, W_A), lambda i, h: (i, 0))
    hspec = pl.BlockSpec((SR, DH_B), lambda i, h: (i, h))
    return pl.pallas_call(
        _even_sample_kernel,
        grid=(nb, H_B),
        in_specs=[aspec, aspec, aspec, aspec,
                  hspec, hspec, hspec, hspec,
                  pl.BlockSpec((SR, W_A), lambda i, h: (i, 0)),
                  pl.BlockSpec((SB, 1, DH_B, DH_B), lambda i, h: (i, h, 0, 0)),
                  pl.BlockSpec((CONV_W, W_A), const2),
                  pl.BlockSpec((1, DH_B), lambda i, h: (0, h)),
                  pl.BlockSpec((SR, DH_B), const2),
                  pl.BlockSpec((SR, DH_B), const2),
                  pl.BlockSpec((1, 1, 128), lambda i, h: (h, 0, 0))],
        out_specs=(pl.BlockSpec((SR, W_A), lambda i, h: (i, 0)),
                   pl.BlockSpec((SR, W_A), lambda i, h: (i, 0)),
                   pl.BlockSpec((SR, DH_B), lambda i, h: (i, h)),
                   pl.BlockSpec((SB, 1, DH_B, DH_B), lambda i, h: (i, h, 0, 0))),
        out_shape=(jax.ShapeDtypeStruct((DEC_BATCH * DEC_SEQ, W_A), BF16),
                   jax.ShapeDtypeStruct((DEC_BATCH * DEC_SEQ, W_A), F32),
                   jax.ShapeDtypeStruct((DEC_BATCH * DEC_SEQ, W_B), BF16),
                   jax.ShapeDtypeStruct((DEC_BATCH, H_B, DH_B, DH_B), F32)),
        scratch_shapes=[pltpu.VMEM((SR, DH_B), F32)],
        compiler_params=_params(("arbitrary", "arbitrary")),
        name="even_sample",
    )(*pa, *pb, st_exp, s_state, conv_w, g_ret, cosf, sins, lg_tab)


def _odd_prompt_kernel(p_ref, h_ref, wg_ref, bg_ref, gm_ref, lng_ref, lnb_ref, ws_ref, bst_ref,
                       yc_ref, yd_ref, c_ref, n_ref, m_ref, wgb, wsb):
    L = CHUNK

    @pl.when(jnp.logical_and(pl.program_id(0) == 0, pl.program_id(1) == 0))
    def _():
        wgb[...] = wg_ref[...].astype(BF16)
        keep = (lax.broadcasted_iota(jnp.int32, (L, L), 0) >= lax.broadcasted_iota(jnp.int32, (L, L), 1))
        for g in range(G_D):
            wsb[g] = jnp.where(keep, ws_ref[g], 0.0).astype(BF16)

    @pl.when(pl.program_id(1) == 0)
    def _():
        c_ref[...] = jnp.zeros_like(c_ref)
        n_ref[...] = jnp.zeros_like(n_ref)
        m_ref[...] = jnp.zeros_like(m_ref)

    def chunk(ci, carry):
        rs = pl.ds(pl.multiple_of(ci * L, L), L)
        row = lax.broadcasted_iota(jnp.int32, (L, L), 0)
        col = lax.broadcasted_iota(jnp.int32, (L, L), 1)
        tri = row >= col

        pre = _dot_nt(h_ref[rs, :], wgb[...]) + bg_ref[...]
        lf = _log_sigmoid(pre)
        b_c = _dot_hi(jnp.where(tri, 1.0, 0.0), lf)
        b_r = b_c.T
        pre_r = pre.T
        for h in range(H_C):
            bc = b_c[:, H_C + h:H_C + h + 1]
            br = b_r[H_C + h:H_C + h + 1, :]
            igr = pre_r[h:h + 1, :]
            igc = pre[:, h:h + 1]
            m_prev = m_ref[0, h:h + 1, 0:1]
            log_d = jnp.where(tri, bc - br + igr, NEG_INF)
            log_inter = bc + m_prev
            m_t = jnp.maximum(log_inter, jnp.max(log_d, axis=-1, keepdims=True))
            w = jnp.exp(log_d - m_t)
            w_inter = jnp.exp(log_inter - m_t)
            q = p_ref[rs, h * DQK_C:(h + 1) * DQK_C] * (DQK_C ** -0.5)
            k = p_ref[rs, H_C * DQK_C + h * DQK_C:H_C * DQK_C + (h + 1) * DQK_C]
            v = p_ref[rs, 2 * H_C * DQK_C + h * DV_C:2 * H_C * DQK_C + (h + 1) * DV_C]
            z = p_ref[rs, O_GATE - W_C + h * DV_C:O_GATE - W_C + (h + 1) * DV_C]
            qb = q.astype(BF16)
            kb = k.astype(BF16)
            vb = v.astype(BF16)
            sc = _dot_nt(qb, kb) * w
            c_old = c_ref[0, h]
            n_old = n_ref[0, h:h + 1, :]
            num = _dot(sc.astype(BF16), vb) + w_inter * _dot_nt(qb, c_old.astype(BF16))
            den = jnp.sum(sc, axis=-1, keepdims=True) + w_inter * jnp.sum(q * n_old, axis=-1, keepdims=True)
            hh = num / jnp.maximum(jnp.abs(den), jnp.exp(-m_t))
            m_new = m_t[L - 1:L, :]
            b_last = bc[L - 1:L, :]
            w_end = jnp.exp(b_last - bc + igc - m_new)
            cd = jnp.exp(b_last + m_prev - m_new)
            c_ref[0, h] = cd * c_old + _dot_tn((v * w_end).astype(BF16), kb)
            n_ref[0, h:h + 1, :] = cd * n_old + jnp.sum(w_end * k, axis=0, keepdims=True)
            m_ref[0, h:h + 1, :] = jnp.broadcast_to(m_new, (1, 128))
            sl = slice(h * DV_C, (h + 1) * DV_C)
            yc_ref[rs, sl] = (_head_norm(hh, gm_ref[0:1, sl]) * _silu(z)).astype(BF16)

        dv = lambda g: p_ref[rs, O_GATE + W_D + g * 128:O_GATE + W_D + (g + 1) * 128]
        tot = dv(0)
        for g in range(1, G_D):
            tot = tot + dv(g)
        mu = jnp.sum(tot, axis=-1, keepdims=True) * (1.0 / W_D)
        sq = (dv(0) - mu) * (dv(0) - mu)
        for g in range(1, G_D):
            sq = sq + (dv(g) - mu) * (dv(g) - mu)
        rstd = lax.rsqrt(jnp.sum(sq, axis=-1, keepdims=True) * (1.0 / W_D) + EPS)
        for g in range(G_D):
            sl = slice(g * 128, (g + 1) * 128)
            vn = (dv(g) - mu) * rstd * lng_ref[0:1, sl] + lnb_ref[0:1, sl]
            s = _dot(wsb[g], vn.astype(BF16)) + bst_ref[:, g:g + 1]
            d_u = p_ref[rs, O_GATE + g * 128:O_GATE + (g + 1) * 128]
            d_z = p_ref[rs, O_GATE + 2 * W_D + g * 128:O_GATE + 2 * W_D + (g + 1) * 128]
            yd_ref[rs, sl] = (d_u * s * _silu(d_z)).astype(BF16)
        return carry

    lax.fori_loop(0, MIX_ROWS // L, chunk, 0)


def _odd_prompt(p, h, w_o, bg, gm, lng, lnb, ws, bst):
    nc = SEQ // MIX_ROWS
    rows = lambda b, c: (b * nc + c, 0)
    const2 = lambda b, c: (0, 0)
    return pl.pallas_call(
        _odd_prompt_kernel,
        grid=(BATCH, nc),
        in_specs=[pl.BlockSpec((MIX_ROWS, O_N), rows),
                  pl.BlockSpec((MIX_ROWS, D_MODEL), rows),
                  pl.BlockSpec((128, D_MODEL), lambda b, c: (O_GATE // 128, 0)),
                  pl.BlockSpec((1, 128), const2),
                  pl.BlockSpec((1, W_C), const2),
                  pl.BlockSpec((1, W_D), const2),
                  pl.BlockSpec((1, W_D), const2),
                  pl.BlockSpec((G_D, CHUNK, CHUNK), lambda b, c: (0, 0, 0)),
                  pl.BlockSpec((CHUNK, G_D), const2)],
        out_specs=(pl.BlockSpec((MIX_ROWS, W_C), rows),
                   pl.BlockSpec((MIX_ROWS, W_D), rows),
                   pl.BlockSpec((1, H_C, DV_C, DQK_C), lambda b, c: (b, 0, 0, 0)),
                   pl.BlockSpec((1, H_C, DQK_C), lambda b, c: (b, 0, 0)),
                   pl.BlockSpec((1, 8, 128), lambda b, c: (b, 0, 0))),
        out_shape=(jax.ShapeDtypeStruct((BATCH * SEQ, W_C), BF16),
                   jax.ShapeDtypeStruct((BATCH * SEQ, W_D), BF16),
                   jax.ShapeDtypeStruct((BATCH, H_C, DV_C, DQK_C), F32),
                   jax.ShapeDtypeStruct((BATCH, H_C, DQK_C), F32),
                   jax.ShapeDtypeStruct((BATCH, 8, 128), F32)),
        scratch_shapes=[pltpu.VMEM((128, D_MODEL), BF16),
                        pltpu.VMEM((G_D, CHUNK, CHUNK), BF16)],
        compiler_params=_params(("arbitrary", "arbitrary")),
        name="odd_prompt",
    )(p, h, w_o, bg, gm, lng, lnb, ws, bst)


CG = 2
CW = 128 + CG * (2 * DQK_C + 2 * DV_C)
CT = 1024
CN = BATCH * SEQ // CT
CPB = SEQ // CT


def _odd_heads_kernel(hp_ref, hs_ref, wq_ref, wk_ref, wv_ref, wz_ref, wg_ref, bg_ref, gm_ref,
                      yc_ref, c_ref, n_ref, m_ref, sq_ref, sk_ref, sv_ref, sz_ref,
                      wb, pt, c_scr, n_scr, m_scr, gt_scr):
    grp = pl.program_id(0)
    s = pl.program_id(1)
    L = CHUNK
    gc = slice(0, 128)
    qc = slice(128, 128 + CG * DQK_C)
    kc = slice(qc.stop, qc.stop + CG * DQK_C)
    vc = slice(kc.stop, kc.stop + CG * DV_C)
    zc = slice(vc.stop, vc.stop + CG * DV_C)
    head = lambda sl, i, w: slice(sl.start + i * w, sl.start + (i + 1) * w)

    @pl.when(s == 0)
    def _():
        wb[qc, :] = wq_ref[...].astype(BF16)
        wb[kc, :] = wk_ref[...].astype(BF16)
        wb[vc, :] = wv_ref[...].astype(BF16)
        wb[zc, :] = wz_ref[...].astype(BF16)
        wb[gc, :] = wg_ref[...].astype(BF16)
        ps = _dot_nt(hs_ref[...], wb[qc.start:CW, :])
        off = lambda sl: slice(sl.start - qc.start, sl.stop - qc.start)
        sq_ref[...] = ps[:, off(qc)]
        sk_ref[...] = ps[:, off(kc)]
        sv_ref[...] = ps[:, off(vc)]
        sz_ref[...] = ps[:, off(zc)]

    @pl.when(s > 0)
    def _():
        t = s - 1
        n = CT // L
        rows = lambda c: slice(c * L, (c + 1) * L)

        def piece(k, j):
            pr = slice(k * PCH * L, (k + 1) * PCH * L)
            pc = (slice(0, vc.start), slice(vc.start, CW))[j]
            pt[pr, pc] = _dot_nt(hp_ref[pr, :], wb[pc, :])

        row = lax.broadcasted_iota(jnp.int32, (L, L), 0)
        col = lax.broadcasted_iota(jnp.int32, (L, L), 1)
        tri = row >= col
        fresh = t % CPB == 0
        cst = {0: [jnp.where(fresh, 0.0, c_scr[i]) for i in range(CG)]}
        nst = {0: [jnp.where(fresh, 0.0, n_scr[i]) for i in range(CG)]}
        mst = {0: [jnp.where(fresh, 0.0, m_scr[i, 0:1, 0:1]) for i in range(CG)]}
        v = {}
        gates = {}

        def gate(k):
            cs = range(k * PCH, (k + 1) * PCH)
            for c in cs:
                gt_scr[c] = (pt[rows(c), gc] + bg_ref[...]).T
            pad = jnp.zeros((8 - CG * PCH, L), F32)
            ig_rows = jnp.concatenate(
                [gt_scr[c, pl.ds(grp * CG + i, 1), :] for i in range(CG) for c in cs] + [pad], axis=0)
            lf_rows = jnp.concatenate(
                [_log_sigmoid(gt_scr[c, pl.ds(grp * CG + i + H_C, 1), :]) for i in range(CG) for c in cs]
                + [pad], axis=0)
            b_rows = _dot_hi(lf_rows, jnp.where(row <= col, 1.0, 0.0))
            tall = jnp.zeros((L - 8, L), F32)
            gates[k] = dict(ig_rows=ig_rows, b_rows=b_rows,
                            b_cols=jnp.concatenate([b_rows, tall], axis=0).T,
                            ig_cols=jnp.concatenate([ig_rows, tall], axis=0).T)

        def stage_a(c):
            gk = gates[c // PCH]
            v[c] = []
            nst[c + 1] = []
            mst[c + 1] = []
            for i in range(CG):
                r = i * PCH + c % PCH
                b_r = gk["b_rows"][r:r + 1, :]
                ig_r = gk["ig_rows"][r:r + 1, :]
                b_c = gk["b_cols"][:, r:r + 1]
                ig_c = gk["ig_cols"][:, r:r + 1]
                m_prev = mst[c][i]
                log_d = jnp.where(tri, b_c - b_r + ig_r, NEG_INF)
                log_inter = b_c + m_prev
                m_t = jnp.maximum(log_inter, jnp.max(log_d, axis=-1, keepdims=True))
                m_new = m_t[L - 1:L, :]
                b_last = b_c[L - 1:L, :]
                w_end = jnp.exp(b_last - b_c + ig_c - m_new)
                cd = jnp.exp(b_last + m_prev - m_new)
                q = pt[rows(c), head(qc, i, DQK_C)] * (DQK_C ** -0.5)
                k = pt[rows(c), head(kc, i, DQK_C)]
                vv = pt[rows(c), head(vc, i, DV_C)]
                nst[c + 1].append(cd * nst[c][i] + jnp.sum(w_end * k, axis=0, keepdims=True))
                mst[c + 1].append(m_new)
                v[c].append(dict(w=jnp.exp(log_d - m_t), w_inter=jnp.exp(log_inter - m_t),
                                 floor=jnp.exp(-m_t), cd=cd, qb=q.astype(BF16), kb=k.astype(BF16),
                                 vb=vv.astype(BF16), vw=(vv * w_end).astype(BF16),
                                 qn=jnp.sum(q * nst[c][i], axis=-1, keepdims=True)))

        def stage_b(c):
            for i, d in enumerate(v[c]):
                d["sc"] = _dot_nt(d["qb"], d["kb"])
                d["upd"] = _dot_tn(d["vw"], d["kb"])
            for i, d in enumerate(v[c]):
                d["inter"] = _dot_nt(d["qb"], cst[c][i].astype(BF16))

        def stage_c(c):
            cst[c + 1] = []
            for i, d in enumerate(v[c]):
                sc = d["sc"] * d["w"]
                d["den"] = jnp.sum(sc, axis=-1, keepdims=True) + d["w_inter"] * d["qn"]
                d["sc"] = sc.astype(BF16)
                cst[c + 1].append(d["cd"] * cst[c][i] + d["upd"])

        def stage_d(c):
            for d in v[c]:
                d["num"] = _dot(d["sc"], d["vb"])

        def stage_e(c):
            for i, d in enumerate(v.pop(c)):
                num = d["num"] + d["w_inter"] * d["inter"]
                hh = num / jnp.maximum(jnp.abs(d["den"]), d["floor"])
                z = pt[rows(c), head(zc, i, DV_C)]
                ys = slice(i * DV_C, (i + 1) * DV_C)
                yc_ref[rows(c), ys] = (_head_norm(hh, gm_ref[0:1, ys]) * _silu(z)).astype(BF16)

        _chunk_pipeline(n, piece, (stage_a, stage_b, stage_c, stage_d, stage_e), gate)
        for i in range(CG):
            c_scr[i] = cst[n][i]
            n_scr[i] = nst[n][i]
            m_scr[i] = jnp.broadcast_to(mst[n][i], (8, 128))
            c_ref[0, i] = cst[n][i]
            n_ref[0, i] = nst[n][i]
            m_ref[0, i] = jnp.broadcast_to(mst[n][i], (1, 128))


def _odd_heads(hp, hs, w_t, bg, gm):
    k = hp.shape[1]
    ms = hs.shape[0]
    tile = lambda s: jnp.maximum(s - 1, 0)
    qw, vw = CG * DQK_C, CG * DV_C
    koff = H_C * DQK_C // qw
    voff = 2 * H_C * DQK_C // vw
    zoff = (2 * H_C * DQK_C + W_C) // vw
    seq = lambda g, s: (tile(s) // CPB, g, 0, 0)
    once = pl.Buffered(1)
    return pl.pallas_call(
        _odd_heads_kernel,
        grid=(H_C // CG, CN + 1),
        in_specs=[pl.BlockSpec((CT, k), lambda g, s: (tile(s), 0)),
                  pl.BlockSpec((ms, k), lambda g, s: (0, 0), pipeline_mode=once),
                  pl.BlockSpec((qw, k), lambda g, s: (g, 0), pipeline_mode=once),
                  pl.BlockSpec((qw, k), lambda g, s: (koff + g, 0), pipeline_mode=once),
                  pl.BlockSpec((vw, k), lambda g, s: (voff + g, 0), pipeline_mode=once),
                  pl.BlockSpec((vw, k), lambda g, s: (zoff + g, 0), pipeline_mode=once),
                  pl.BlockSpec((128, k), lambda g, s: (O_GATE // 128, 0), pipeline_mode=once),
                  pl.BlockSpec((1, 128), lambda g, s: (0, 0)),
                  pl.BlockSpec((1, vw), lambda g, s: (0, g))],
        out_specs=(pl.BlockSpec((CT, vw), lambda g, s: (tile(s), g)),
                   pl.BlockSpec((1, CG, DV_C, DQK_C), seq),
                   pl.BlockSpec((1, CG, 1, DQK_C), seq),
                   pl.BlockSpec((1, CG, 1, 128), seq),
                   pl.BlockSpec((ms, qw), lambda g, s: (0, g)),
                   pl.BlockSpec((ms, qw), lambda g, s: (0, g)),
                   pl.BlockSpec((ms, vw), lambda g, s: (0, g)),
                   pl.BlockSpec((ms, vw), lambda g, s: (0, g))),
        out_shape=(jax.ShapeDtypeStruct((BATCH * SEQ, W_C), BF16),
                   jax.ShapeDtypeStruct((BATCH, H_C, DV_C, DQK_C), F32),
                   jax.ShapeDtypeStruct((BATCH, H_C, 1, DQK_C), F32),
                   jax.ShapeDtypeStruct((BATCH, H_C, 1, 128), F32),
                   jax.ShapeDtypeStruct((ms, H_C * DQK_C), F32),
                   jax.ShapeDtypeStruct((ms, H_C * DQK_C), F32),
                   jax.ShapeDtypeStruct((ms, W_C), F32),
                   jax.ShapeDtypeStruct((ms, W_C), F32)),
        scratch_shapes=[pltpu.VMEM((CW, k), BF16),
                        pltpu.VMEM((CT, CW), F32),
                        pltpu.VMEM((CG, DV_C, DQK_C), F32),
                        pltpu.VMEM((CG, 1, DQK_C), F32),
                        pltpu.VMEM((CG, 8, 128), F32),
                        pltpu.VMEM((CT // CHUNK, CHUNK, CHUNK), F32)],
        compiler_params=_params(("arbitrary", "arbitrary")),
        name="odd_heads",
    )(hp, hs, w_t, w_t, w_t, w_t, w_t, bg, gm)


def _odd_mlp_kernel(p_ref, lng_ref, lnb_ref, ws_ref, bst_ref, yd_ref, wsb):
    L = CHUNK

    @pl.when(pl.program_id(0) == 0)
    def _():
        keep = (lax.broadcasted_iota(jnp.int32, (L, L), 0) >= lax.broadcasted_iota(jnp.int32, (L, L), 1))
        for g in range(G_D):
            wsb[g] = jnp.where(keep, ws_ref[g], 0.0).astype(BF16)

    def chunk(ci, carry):
        rs = pl.ds(pl.multiple_of(ci * L, L), L)
        dv = lambda g: p_ref[rs, W_D + g * 128:W_D + (g + 1) * 128]
        tot = dv(0)
        for g in range(1, G_D):
            tot = tot + dv(g)
        mu = jnp.sum(tot, axis=-1, keepdims=True) * (1.0 / W_D)
        sq = (dv(0) - mu) * (dv(0) - mu)
        for g in range(1, G_D):
            sq = sq + (dv(g) - mu) * (dv(g) - mu)
        rstd = lax.rsqrt(jnp.sum(sq, axis=-1, keepdims=True) * (1.0 / W_D) + EPS)
        for g in range(G_D):
            sl = slice(g * 128, (g + 1) * 128)
            vn = (dv(g) - mu) * rstd * lng_ref[0:1, sl] + lnb_ref[0:1, sl]
            s = _dot(wsb[g], vn.astype(BF16)) + bst_ref[:, g:g + 1]
            d_u = p_ref[rs, g * 128:(g + 1) * 128]
            d_z = p_ref[rs, 2 * W_D + g * 128:2 * W_D + (g + 1) * 128]
            yd_ref[rs, sl] = (d_u * s * _silu(d_z)).astype(BF16)
        return carry

    lax.fori_loop(0, p_ref.shape[0] // L, chunk, 0)


def _odd_mlp(p, lng, lnb, ws, bst, rows_per_step):
    m = p.shape[0]
    rows = lambda i: (i, 0)
    const2 = lambda i: (0, 0)
    return pl.pallas_call(
        _odd_mlp_kernel,
        grid=(m // rows_per_step,),
        in_specs=[pl.BlockSpec((rows_per_step, 3 * W_D), rows),
                  pl.BlockSpec((1, W_D), const2),
                  pl.BlockSpec((1, W_D), const2),
                  pl.BlockSpec((G_D, CHUNK, CHUNK), lambda i: (0, 0, 0)),
                  pl.BlockSpec((CHUNK, G_D), const2)],
        out_specs=pl.BlockSpec((rows_per_step, W_D), rows),
        out_shape=jax.ShapeDtypeStruct((m, W_D), BF16),
        scratch_shapes=[pltpu.VMEM((G_D, CHUNK, CHUNK), BF16)],
        compiler_params=_params(("arbitrary",)),
        name="odd_mlp",
    )(p, lng, lnb, ws, bst)


DT = 512


def _odd_mlp_fused_kernel(hp_ref, w0_ref, w1_ref, w2_ref, wt_ref, lng_ref, lnb_ref, ws_ref, bst_ref,
                          yd_ref, wb, pt, wsb):
    s = pl.program_id(0)
    L = CHUNK
    uc = slice(0, W_D)
    vc = slice(W_D, 2 * W_D)
    zc = slice(2 * W_D, 3 * W_D)

    @pl.when(s == 0)
    def _():
        sh = N_GATE
        wb[uc, :] = jnp.concatenate([w0_ref[sh:, :], w1_ref[0:sh, :]], axis=0).astype(BF16)
        wb[vc, :] = jnp.concatenate([w1_ref[sh:, :], w2_ref[0:sh, :]], axis=0).astype(BF16)
        wb[zc, :] = jnp.concatenate([w2_ref[sh:, :], wt_ref[...]], axis=0).astype(BF16)
        keep = (lax.broadcasted_iota(jnp.int32, (L, L), 0) >= lax.broadcasted_iota(jnp.int32, (L, L), 1))
        for g in range(G_D):
            wsb[g] = jnp.where(keep, ws_ref[g], 0.0).astype(BF16)

    @pl.when(s > 0)
    def _():
        n = DT // L
        rows = lambda c: slice(c * L, (c + 1) * L)
        grp = lambda sl, g: slice(sl.start + g * 128, sl.start + (g + 1) * 128)
        vn = {}
        mix = {}

        def project(pc):
            pt[:, pc] = _dot_nt(hp_ref[...], wb[pc, :])

        def stage_a(c):
            dv = lambda g: pt[rows(c), grp(vc, g)]
            tot = dv(0)
            for g in range(1, G_D):
                tot = tot + dv(g)
            mu = jnp.sum(tot, axis=-1, keepdims=True) * (1.0 / W_D)
            sq = (dv(0) - mu) * (dv(0) - mu)
            for g in range(1, G_D):
                sq = sq + (dv(g) - mu) * (dv(g) - mu)
            rstd = lax.rsqrt(jnp.sum(sq, axis=-1, keepdims=True) * (1.0 / W_D) + EPS)
            vn[c] = [((dv(g) - mu) * rstd * lng_ref[0:1, g * 128:(g + 1) * 128]
                      + lnb_ref[0:1, g * 128:(g + 1) * 128]).astype(BF16) for g in range(G_D)]

        def stage_b(c):
            mix[c] = [_dot(wsb[g], vn[c][g]) for g in range(G_D)]

        def stage_e(c):
            for g in range(G_D):
                sg = mix[c][g] + bst_ref[:, g:g + 1]
                d_u = pt[rows(c), grp(uc, g)]
                d_z = pt[rows(c), grp(zc, g)]
                yd_ref[rows(c), g * 128:(g + 1) * 128] = (d_u * sg * _silu(d_z)).astype(BF16)

        project(vc)
        for c in range(n):
            stage_a(c)
        project(uc)
        for c in range(n):
            stage_b(c)
        project(zc)
        for c in range(n):
            stage_e(c)


def _odd_mlp_fused(hp, w_t, lng, lnb, ws, bst):
    k = hp.shape[1]
    m = hp.shape[0]
    tile = lambda s: (jnp.maximum(s - 1, 0), 0)
    const2 = lambda s: (0, 0)
    t0 = O_GATE // IN_TN
    once = pl.Buffered(1)
    wspec = lambda j: pl.BlockSpec((IN_TN, k), lambda s: (t0 + j, 0), pipeline_mode=once)
    return pl.pallas_call(
        _odd_mlp_fused_kernel,
        grid=(m // DT + 1,),
        in_specs=[pl.BlockSpec((DT, k), tile),
                  wspec(0), wspec(1), wspec(2),
                  pl.BlockSpec((N_GATE, k), lambda s: ((t0 + 3) * (IN_TN // N_GATE), 0), pipeline_mode=once),
                  pl.BlockSpec((1, W_D), const2),
                  pl.BlockSpec((1, W_D), const2),
                  pl.BlockSpec((G_D, CHUNK, CHUNK), lambda s: (0, 0, 0)),
                  pl.BlockSpec((CHUNK, G_D), const2)],
        out_specs=pl.BlockSpec((DT, W_D), tile),
        out_shape=jax.ShapeDtypeStruct((m, W_D), BF16),
        scratch_shapes=[pltpu.VMEM((3 * W_D, k), BF16),
                        pltpu.VMEM((DT, 3 * W_D), F32),
                        pltpu.VMEM((G_D, CHUNK, CHUNK), BF16)],
        compiler_params=_params(("arbitrary",)),
        name="odd_mlp_fused",
    )(hp, w_t, w_t, w_t, w_t, lng, lnb, ws, bst)


def _in_proj_rows_kernel(h_ref, w_ref, wn_ref, o_ref, *, shift):
    wsh = jnp.concatenate([w_ref[shift:, :], wn_ref[...]], axis=0)
    o_ref[...] = _dot_nt(h_ref[...], wsh.astype(BF16))


def _in_proj_rows(h, w_t, n_out, shift, tile0):
    ms, k = h.shape
    return pl.pallas_call(
        functools.partial(_in_proj_rows_kernel, shift=shift),
        grid=(n_out // IN_TN,),
        in_specs=[pl.BlockSpec((ms, k), lambda j: (0, 0)),
                  pl.BlockSpec((IN_TN, k), lambda j: (j + tile0, 0)),
                  pl.BlockSpec((shift, k), lambda j: ((j + tile0 + 1) * (IN_TN // shift), 0))],
        out_specs=pl.BlockSpec((ms, IN_TN), lambda j: (0, j)),
        out_shape=jax.ShapeDtypeStruct((ms, n_out), F32),
        compiler_params=_params(("arbitrary",)),
        name="in_proj_rows",
    )(h, w_t, w_t)


def _odd_sample_kernel(pq_ref, pk_ref, pv_ref, pz_ref, h_ref, wg_ref, pd_ref,
                       c_ref, nrow_ref, mrow_ref, bg_ref, gm_ref, lng_ref, lnb_ref,
                       rtab_ref, btab_ref,
                       yc_ref, yd_ref, vn_ref, co_ref, no_ref, mo_ref,
                       inter_scr):
    h = pl.program_id(1)
    row = lax.broadcasted_iota(jnp.int32, (SR, SR), 0)
    col = lax.broadcasted_iota(jnp.int32, (SR, SR), 1)
    trow = row & 3

    @pl.when(h == 0)
    def _():
        dv = pd_ref[:, W_D:2 * W_D]
        mu = jnp.mean(dv, axis=-1, keepdims=True)
        xc = dv - mu
        var = jnp.mean(xc * xc, axis=-1, keepdims=True)
        rstd = lax.rsqrt(var + EPS)
        for g in range(G_D):
            sl = slice(g * 128, (g + 1) * 128)
            vn = xc[:, sl] * rstd * lng_ref[0:1, sl] + lnb_ref[0:1, sl]
            vn_ref[:, sl] = vn
            s = rtab_ref[0, :, sl] * vn + btab_ref[:, sl]
            for j in range(1, DEC_SEQ):
                s = s + jnp.where(trow >= j, rtab_ref[j, :, sl] * pltpu.roll(vn, j, 0), 0.0)
            d_u = pd_ref[:, g * 128:(g + 1) * 128]
            d_z = pd_ref[:, 2 * W_D + g * 128:2 * W_D + (g + 1) * 128]
            yd_ref[:, sl] = (d_u * s * _silu(d_z)).astype(BF16)

    same = (row >> 2) == (col >> 2)
    mask = jnp.where(same, trow - (col & 3), -1) >= 0
    pre = _dot_nt(h_ref[...], wg_ref[...].astype(BF16)) + bg_ref[...]
    lf = _log_sigmoid(pre)
    b_full = _dot_hi(jnp.where(mask, 1.0, 0.0), lf)
    sel_i = col == h
    sel_f = col == h + H_C
    ig_c = jnp.sum(jnp.where(sel_i, pre, 0.0), axis=-1, keepdims=True)
    b_c = jnp.sum(jnp.where(sel_f, b_full, 0.0), axis=-1, keepdims=True)
    sel_ir = row == h
    sel_fr = row == h + H_C
    ig_r = jnp.sum(jnp.where(sel_ir, pre.T, 0.0), axis=0, keepdims=True)
    b_r = jnp.sum(jnp.where(sel_fr, b_full.T, 0.0), axis=0, keepdims=True)
    m_prev = mrow_ref[0]
    log_d = jnp.where(mask, b_c - b_r + ig_r, NEG_INF)
    log_inter = b_c + m_prev
    m_t = jnp.maximum(log_inter, jnp.max(log_d, axis=-1, keepdims=True))
    w = jnp.exp(log_d - m_t)
    w_inter = jnp.exp(log_inter - m_t)
    q = pq_ref[...] * (DQK_C ** -0.5)
    k = pk_ref[...]
    v = pv_ref[...]
    qb = q.astype(BF16)
    kb = k.astype(BF16)
    vb = v.astype(BF16)
    sc = _dot_nt(qb, kb) * w
    sub = lax.broadcasted_iota(jnp.int32, (8, DV_C), 0)
    sub8 = lax.broadcasted_iota(jnp.int32, (8, DQK_C), 0)
    for g in range(SR // 8):
        q8 = q[8 * g:8 * g + 8, :]
        q2 = jnp.concatenate([jnp.where(sub8 < DEC_SEQ, q8, 0.0), jnp.where(sub8 < DEC_SEQ, 0.0, q8)], axis=1)
        c_pair = jnp.concatenate([c_ref[2 * g + beta, 0].astype(BF16) for beta in range(2)], axis=1)
        inter_scr[8 * g:8 * g + 8, :] = _dot_nt(q2.astype(BF16), c_pair)
    n_rows = nrow_ref[0]
    num = _dot(sc.astype(BF16), vb) + w_inter * inter_scr[...]
    den = jnp.sum(sc, axis=-1, keepdims=True) + w_inter * jnp.sum(q * n_rows, axis=-1, keepdims=True)
    hh = num / jnp.maximum(jnp.abs(den), jnp.exp(-m_t))
    yc_ref[...] = (_head_norm(hh, gm_ref[...]) * _silu(pz_ref[...])).astype(BF16)

    stats = jnp.where(col == 0, m_t, jnp.where(col == 1, b_c, 0.0))
    last = _dot_hi(jnp.where(col == (row | 3), 1.0, 0.0), stats)
    m_new = last[:, 0:1]
    b_last = last[:, 1:2]
    w_end = jnp.exp(b_last - b_c + ig_c - m_new)
    cd = jnp.exp(b_last + m_prev - m_new)
    mo_ref[0] = m_new
    no_ref[0] = cd * n_rows + _dot_hi(jnp.where(same, 1.0, 0.0), w_end * k)
    vwt = (v * w_end).T
    lane_b = lax.broadcasted_iota(jnp.int32, (DV_C, SR), 1) >> 2
    for b in range(SB):
        lhs = jnp.where(lane_b == b, vwt, 0.0).astype(BF16)
        cd_b = cd[4 * b + 3:4 * b + 4, :]
        co_ref[b, 0] = cd_b * c_ref[b, 0] + _dot(lhs, kb)


def _odd_sample(pc, pd, h, w_o, c_state, n_rows, m_rows, bg, gm, lng, lnb, rtab, btab):
    nb = DEC_BATCH // SB
    const2 = lambda i, h: (0, 0)
    return pl.pallas_call(
        _odd_sample_kernel,
        grid=(nb, H_C),
        in_specs=[pl.BlockSpec((SR, DQK_C), lambda i, h: (i, h)),
                  pl.BlockSpec((SR, DQK_C), lambda i, h: (i, h)),
                  pl.BlockSpec((SR, DV_C), lambda i, h: (i, h)),
                  pl.BlockSpec((SR, DV_C), lambda i, h: (i, h)),
                  pl.BlockSpec((SR, D_MODEL), lambda i, h: (i, 0)),
                  pl.BlockSpec((128, D_MODEL), lambda i, h: (O_GATE // 128, 0)),
                  pl.BlockSpec((SR, 3 * W_D), lambda i, h: (i, 0)),
                  pl.BlockSpec((SB, 1, DV_C, DQK_C), lambda i, h: (i, h, 0, 0)),
                  pl.BlockSpec((1, SR, DQK_C), lambda i, h: (h, i, 0)),
                  pl.BlockSpec((1, SR, 1), lambda i, h: (h, i, 0)),
                  pl.BlockSpec((1, 128), const2),
                  pl.BlockSpec((1, DV_C), lambda i, h: (0, h)),
                  pl.BlockSpec((1, W_D), const2),
                  pl.BlockSpec((1, W_D), const2),
                  pl.BlockSpec((DEC_SEQ, SR, W_D), lambda i, h: (0, 0, 0)),
                  pl.BlockSpec((SR, W_D), const2)],
        out_specs=(pl.BlockSpec((SR, DV_C), lambda i, h: (i, h)),
                   pl.BlockSpec((SR, W_D), lambda i, h: (i, 0)),
                   pl.BlockSpec((SR, W_D), lambda i, h: (i, 0)),
                   pl.BlockSpec((SB, 1, DV_C, DQK_C), lambda i, h: (i, h, 0, 0)),
                   pl.BlockSpec((1, SR, DQK_C), lambda i, h: (h, i, 0)),
                   pl.BlockSpec((1, SR, 1), lambda i, h: (h, i, 0))),
        out_shape=(jax.ShapeDtypeStruct((DEC_BATCH * DEC_SEQ, W_C), BF16),
                   jax.ShapeDtypeStruct((DEC_BATCH * DEC_SEQ, W_D), BF16),
                   jax.ShapeDtypeStruct((DEC_BATCH * DEC_SEQ, W_D), F32),
                   jax.ShapeDtypeStruct((DEC_BATCH, H_C, DV_C, DQK_C), F32),
                   jax.ShapeDtypeStruct((H_C, DEC_BATCH * DEC_SEQ, DQK_C), F32),
                   jax.ShapeDtypeStruct((H_C, DEC_BATCH * DEC_SEQ, 1), F32)),
        scratch_shapes=[pltpu.VMEM((SR, DV_C), F32)],
        compiler_params=_params(("arbitrary", "arbitrary")),
        name="odd_sample",
    )(*pc, h, w_o, pd, c_state, n_rows, m_rows, bg, gm, lng, lnb, rtab, btab)


def _rope_tables(pos):
    inv = ROPE_BASE ** (-jnp.arange(0, DH_B, 2, dtype=F32) / DH_B)
    ang = pos.astype(F32)[:, None] * inv[None, :]
    cos = jnp.cos(ang)
    sin = jnp.sin(ang)
    return jnp.concatenate([cos, cos], axis=-1), jnp.concatenate([-sin, sin], axis=-1)


def kernel(x_prompt, x_sample, state_conv, state_ret, state_mlstm_C, state_mlstm_n, state_mlstm_m,
           norm_even, w_in_even, conv_w, ret_norm, w_out_even,
           norm_odd, w_in_odd, b_gate_odd, mlstm_norm, ln_v_g, ln_v_b,
           w_spatial, b_spatial, w_out_odd, norm_final):
    w_in_e = w_in_even[0]
    w_out_e = w_out_even[0].astype(BF16)
    w_o = w_in_odd[0].T
    w_out_o = w_out_odd[0].astype(BF16)
    g_even = norm_even[0][None, :]
    g_odd = norm_odd[0][None, :]
    g_fin = norm_final[None, :]
    cw = conv_w[0]
    g_ret = ret_norm[0][None, :]
    bg = jnp.concatenate([b_gate_odd[0], jnp.zeros((128 - 2 * H_C,), F32)])[None, :]
    gm = mlstm_norm[0][None, :]
    lng = ln_v_g[0][None, :]
    lnb = ln_v_b[0][None, :]
    ws = w_spatial[0]
    bst = b_spatial[0].T

    cos_p, sin_p = _rope_tables(jnp.arange(SEQ, dtype=jnp.int32))
    cos_s, sin_s = _rope_tables(PAST_LEN + jnp.arange(DEC_SEQ, dtype=jnp.int32))
    cos_s = jnp.tile(cos_s, (SB, 1))
    sin_s = jnp.tile(sin_s, (SB, 1))
    lg_tab = jnp.broadcast_to(jnp.asarray(LOG_GAMMA, F32)[:, None, None], (H_B, 1, 128))

    ws4 = ws[:, :DEC_SEQ, :DEC_SEQ]
    t_idx = jnp.arange(DEC_SEQ)
    rtab = []
    for j in range(DEC_SEQ):
        coef = ws4[:, t_idx, (t_idx - j) % DEC_SEQ]
        tab = jnp.repeat(coef.T[:, :, None], 128, axis=2).reshape(DEC_SEQ, W_D)
        rtab.append(jnp.tile(tab, (SB, 1)))
    rtab = jnp.stack(rtab)
    btab = jnp.tile(jnp.repeat(b_spatial[0][:, :DEC_SEQ].T[:, :, None], 128, axis=2)
                    .reshape(DEC_SEQ, W_D), (SB, 1))

    rs = DEC_BATCH * DEC_SEQ
    xp = x_prompt.reshape(BATCH * SEQ, D_MODEL)
    xs = x_sample.reshape(rs, D_MODEL)
    hp = _norm_cast(xp, g_even, 512)
    hs = _norm_cast(xs, g_even, 512)
    ya, conv_p, *ps_a = _even_conv(hp, hs, w_in_e, cw)
    yb, ret_p, *ps_b = _even_heads(hp, hs, w_in_e, g_ret, cos_p, sin_p, lg_tab)
    st_exp = jnp.pad(state_conv[0], ((0, 0), (0, DEC_SEQ - (CONV_W - 1)), (0, 0))).reshape(rs, W_A)
    ya_s, u_s, yb_s, ret_s = _even_sample(ps_a, ps_b, st_exp, state_ret[0], cw, g_ret, cos_s, sin_s, lg_tab)
    x1, h1 = _outproj(ya, yb, w_out_e, xp, g_odd, 512, final=False)
    x1s, h1s = _outproj(ya_s, yb_s, w_out_e, xs, g_odd, 512, final=False)

    yc, c_p, n_p, m_p, *ps_c = _odd_heads(h1, h1s, w_o, bg, gm)
    yd = _odd_mlp_fused(h1, w_o, lng, lnb, ws, bst)
    pd_s = _in_proj_rows(h1s, w_o, 3 * W_D, N_GATE, O_GATE // IN_TN)
    n_rows = jnp.repeat(jnp.transpose(state_mlstm_n[0], (1, 0, 2)), DEC_SEQ, axis=1)
    m_rows = jnp.repeat(state_mlstm_m[0].T, DEC_SEQ, axis=1)[:, :, None]
    yc_s, yd_s, vn_s, c_s, no_s, mo_s = _odd_sample(
        ps_c, pd_s, h1s, w_o, state_mlstm_C[0], n_rows, m_rows, bg, gm, lng, lnb, rtab, btab)
    y_prompt = _outproj(yc, yd, w_out_o, x1, g_fin, 512, final=True)
    y_sample = _outproj(yc_s, yd_s, w_out_o, x1s, g_fin, 512, final=True)

    conv_s = u_s.reshape(DEC_BATCH, DEC_SEQ, W_A)[:, DEC_SEQ - (CONV_W - 1):, :]
    n_s = jnp.transpose(no_s[:, DEC_SEQ - 1::DEC_SEQ, :], (1, 0, 2))
    m_s = mo_s[:, DEC_SEQ - 1::DEC_SEQ, 0].T
    return (y_prompt.reshape(BATCH, SEQ, D_MODEL),
            y_sample.reshape(DEC_BATCH, DEC_SEQ, D_MODEL),
            conv_p[None], conv_s[None],
            ret_p[None], ret_s[None],
            c_p[None], c_s[None],
            n_p[:, :, 0, :][None], n_s[None],
            m_p[:, :, 0, 0][None], m_s[None],
            vn_s.reshape(DEC_BATCH, DEC_SEQ, W_D)[None])
```

```python
import functools
import math

import jax
import jax.numpy as jnp
from jax import lax
from jax.experimental import pallas as pl
from jax.experimental.pallas import tpu as pltpu

F32 = jnp.float32
BF16 = jnp.bfloat16

D_MODEL = 2048
BATCH = 4
SEQ = 2048
DEC_BATCH = 128
DEC_SEQ = 4
PAST_LEN = 16384
W_A = 1024
CONV_W = 3
W_B = 1024
H_B = 8
DH_B = 128
E_IN = 8192
W_C = 1024
H_C = 4
DV_C = 256
DQK_C = 128
W_D = 1024
G_D = 8
CHUNK = 128
O_GATE = 2 * H_C * DQK_C + 2 * W_C
N_GATE = 2 * H_C
O_N = O_GATE + 3 * W_D
ROPE_BASE = 10000.0
EPS = 1e-6
LOG_GAMMA = tuple(math.log(1.0 - 2.0 ** (-5.0 - h)) for h in range(H_B))
NEG_INF = float("-inf")
VMEM_LIMIT = 56 * 1024 * 1024

NT_DIMS = (((1,), (1,)), ((), ()))
TN_DIMS = (((0,), (0,)), ((), ()))


def _silu(z):
    return z * (1.0 / (1.0 + jnp.exp(-z)))


def _log_sigmoid(x):
    return jnp.minimum(x, 0.0) - jnp.log1p(jnp.exp(-jnp.abs(x)))


def _dot(a, b):
    return jnp.dot(a, b, preferred_element_type=F32)


def _dot_nt(a, b):
    return lax.dot_general(a, b, NT_DIMS, preferred_element_type=F32)


def _dot_tn(a, b):
    return lax.dot_general(a, b, TN_DIMS, preferred_element_type=F32)


def _dot_hi(a, b):
    return jnp.dot(a, b, preferred_element_type=F32, precision=lax.Precision.HIGHEST)


def _head_norm(o, g):
    mu = jnp.mean(o, axis=-1, keepdims=True)
    oc = o - mu
    var = jnp.mean(oc * oc, axis=-1, keepdims=True)
    return oc * lax.rsqrt(var + EPS) * g


def _params(sem):
    return pltpu.CompilerParams(dimension_semantics=sem, vmem_limit_bytes=VMEM_LIMIT)


def _norm_cast_kernel(x_ref, g_ref, h_ref):
    x = x_ref[...]
    ms = jnp.mean(x * x, axis=-1, keepdims=True)
    h_ref[...] = (x * lax.rsqrt(ms + EPS) * g_ref[...]).astype(BF16)


def _norm_cast(x, g, tm):
    m, d = x.shape
    return pl.pallas_call(
        _norm_cast_kernel,
        grid=(m // tm,),
        in_specs=[pl.BlockSpec((tm, d), lambda i: (i, 0)),
                  pl.BlockSpec((1, d), lambda i: (0, 0))],
        out_specs=pl.BlockSpec((tm, d), lambda i: (i, 0)),
        out_shape=jax.ShapeDtypeStruct((m, d), BF16),
        compiler_params=_params(("arbitrary",)),
        name="norm_cast",
    )(x, g)


IN_TM = 1024
IN_TN = 1024


def _in_proj_kernel(*refs, shift_from, shift):
    if shift:
        hp_ref, hs_ref, w_ref, wn_ref, op_ref, os_ref, wb = refs
    else:
        hp_ref, hs_ref, w_ref, op_ref, os_ref, wb = refs
    j = pl.program_id(0)
    i = pl.program_id(1)

    if shift:
        @pl.when(jnp.logical_and(i == 0, j < shift_from))
        def _():
            wb[...] = w_ref[...].astype(BF16)

        @pl.when(jnp.logical_and(i == 0, j >= shift_from))
        def _():
            wb[...] = jnp.concatenate([w_ref[shift:IN_TN, :], wn_ref[...]], axis=0).astype(BF16)

        mm = _dot_nt
    else:
        @pl.when(i == 0)
        def _():
            wb[...] = w_ref[...].astype(BF16)

        mm = _dot

    @pl.when(i == 0)
    def _():
        os_ref[...] = mm(hs_ref[...], wb[...])

    @pl.when(i > 0)
    def _():
        op_ref[...] = mm(hp_ref[...], wb[...])


def _in_proj(hp, hs, w, n_out, shift_from=0, shift=0, tile0=0):
    mp, k = hp.shape
    ms = hs.shape[0]
    n_prompt = mp // IN_TM
    prow = lambda j, i: (jnp.maximum(i - 1, 0), 0)
    in_specs = [pl.BlockSpec((IN_TM, k), prow),
                pl.BlockSpec((ms, k), lambda j, i: (0, 0))]
    args = [hp, hs, w]
    if shift:
        in_specs.append(pl.BlockSpec((IN_TN, k), lambda j, i: (j + tile0, 0)))
        in_specs.append(pl.BlockSpec((shift, k), lambda j, i: ((j + tile0 + 1) * (IN_TN // shift), 0)))
        args.append(w)
        wb_shape = (IN_TN, k)
    else:
        in_specs.append(pl.BlockSpec((k, IN_TN), lambda j, i: (0, j)))
        wb_shape = (k, IN_TN)
    return pl.pallas_call(
        functools.partial(_in_proj_kernel, shift_from=shift_from, shift=shift),
        grid=(n_out // IN_TN, n_prompt + 1),
        in_specs=in_specs,
        out_specs=(pl.BlockSpec((IN_TM, IN_TN), lambda j, i: (jnp.maximum(i - 1, 0), j)),
                   pl.BlockSpec((ms, IN_TN), lambda j, i: (0, j))),
        out_shape=(jax.ShapeDtypeStruct((mp, n_out), F32),
                   jax.ShapeDtypeStruct((ms, n_out), F32)),
        scratch_shapes=[pltpu.VMEM(wb_shape, BF16)],
        compiler_params=_params(("arbitrary", "arbitrary")),
        name="in_proj",
    )(*args)


OUT_TM = 512


def _outproj_kernel(ya_ref, yb_ref, x_ref, yas_ref, ybs_ref, xs_ref, w_ref, g_ref, *out_refs,
                    final, n_prompt):
    i = pl.program_id(0)
    half = ya_ref.shape[1]
    n_out = 1 if final else 2

    def tile(ya, yb, x, outs):
        acc = _dot(ya[...], w_ref[0:half, :]) + _dot(yb[...], w_ref[half:2 * half, :])
        x1 = x[...] + acc
        ms = jnp.mean(x1 * x1, axis=-1, keepdims=True)
        hn = x1 * lax.rsqrt(ms + EPS) * g_ref[...]
        if final:
            outs[0][...] = hn
        else:
            outs[0][...] = x1
            outs[1][...] = hn.astype(BF16)

    @pl.when(i < n_prompt)
    def _():
        tile(ya_ref, yb_ref, x_ref, out_refs[:n_out])

    @pl.when(i == n_prompt)
    def _():
        tile(yas_ref, ybs_ref, xs_ref, out_refs[n_out:])


def _outproj(ya, yb, x, ya_s, yb_s, x_s, w, g, final):
    m, half = ya.shape
    ms = ya_s.shape[0]
    d = w.shape[1]
    n_prompt = m // OUT_TM
    row = lambda i: (jnp.minimum(i, n_prompt - 1), 0)
    const = lambda i: (0, 0)
    once = pl.Buffered(1)
    shapes = [jax.ShapeDtypeStruct((m, d), F32), jax.ShapeDtypeStruct((ms, d), F32)]
    specs = [pl.BlockSpec((OUT_TM, d), row), pl.BlockSpec((ms, d), const)]
    if not final:
        shapes = [shapes[0], jax.ShapeDtypeStruct((m, d), BF16), shapes[1], jax.ShapeDtypeStruct((ms, d), BF16)]
        specs = [specs[0], pl.BlockSpec((OUT_TM, d), row), specs[1], pl.BlockSpec((ms, d), const)]
    return pl.pallas_call(
        functools.partial(_outproj_kernel, final=final, n_prompt=n_prompt),
        grid=(n_prompt + 1,),
        in_specs=[pl.BlockSpec((OUT_TM, half), row),
                  pl.BlockSpec((OUT_TM, half), row),
                  pl.BlockSpec((OUT_TM, d), row),
                  pl.BlockSpec((ms, half), const, pipeline_mode=once),
                  pl.BlockSpec((ms, half), const, pipeline_mode=once),
                  pl.BlockSpec((ms, d), const, pipeline_mode=once),
                  pl.BlockSpec((2 * half, d), const, pipeline_mode=once),
                  pl.BlockSpec((1, d), const)],
        out_specs=tuple(specs),
        out_shape=tuple(shapes),
        compiler_params=_params(("arbitrary",)),
        name="out_proj_final" if final else "out_proj",
    )(ya, yb, x, ya_s, yb_s, x_s, w, g)


MIX_ROWS = 512


def _rope(x, cosf, sins):
    return x * cosf + pltpu.roll(x, DH_B // 2, 1) * sins


def _even_prompt_kernel(p_ref, cw_ref, gret_ref, cos_ref, sin_ref,
                        ya_ref, yb_ref, conv_ref, s_ref, ubuf):
    L = CHUNK

    @pl.when(pl.program_id(1) == 0)
    def _():
        ubuf[0:8, :] = jnp.zeros((8, W_A), F32)
        s_ref[...] = jnp.zeros_like(s_ref)

    def chunk(ci, carry):
        rs = pl.ds(pl.multiple_of(ci * L, L), L)

        for j in range(W_A // 128):
            sl = slice(j * 128, (j + 1) * 128)
            a_b = p_ref[rs, j * 128:(j + 1) * 128]
            a_c = p_ref[rs, W_A + j * 128:W_A + (j + 1) * 128]
            a_x = p_ref[rs, 2 * W_A + j * 128:2 * W_A + (j + 1) * 128]
            a_z = p_ref[rs, 3 * W_A + j * 128:3 * W_A + (j + 1) * 128]
            u = a_c * a_x
            ubuf[8:8 + L, sl] = u
            t0 = ubuf[6:6 + L, sl]
            t1 = ubuf[7:7 + L, sl]
            conv = cw_ref[0:1, sl] * t0 + cw_ref[1:2, sl] * t1 + cw_ref[2:3, sl] * u
            ya_ref[rs, sl] = (a_b * conv * _silu(a_z)).astype(BF16)
            ubuf[0:8, sl] = u[L - 8:L, :]

        cosf = cos_ref[rs, :]
        sins = sin_ref[rs, :]
        row = lax.broadcasted_iota(jnp.int32, (L, L), 0)
        col = lax.broadcasted_iota(jnp.int32, (L, L), 1)
        causal = row >= col
        diff = jnp.maximum(row - col, 0).astype(F32)
        ti = lax.broadcasted_iota(jnp.int32, (L, 1), 0).astype(F32)
        base = 4 * W_A
        for h in range(H_B):
            lg = LOG_GAMMA[h]
            sl = slice(h * DH_B, (h + 1) * DH_B)
            q = p_ref[rs, base + h * DH_B:base + (h + 1) * DH_B]
            k = p_ref[rs, base + W_B + h * DH_B:base + W_B + (h + 1) * DH_B]
            v = p_ref[rs, base + 2 * W_B + h * DH_B:base + 2 * W_B + (h + 1) * DH_B]
            z = p_ref[rs, base + 3 * W_B + h * DH_B:base + 3 * W_B + (h + 1) * DH_B]
            qr = _rope(q, cosf, sins)
            kr = _rope(k, cosf, sins) * (DH_B ** -0.5)
            decay = jnp.where(causal, jnp.exp(lg * diff), 0.0)
            qb = qr.astype(BF16)
            kb = kr.astype(BF16)
            vb = v.astype(BF16)
            sc = _dot_nt(qb, kb) * decay
            inner = _dot(sc.astype(BF16), vb)
            s_old = s_ref[0, h]
            cross = _dot(qb, s_old.astype(BF16)) * jnp.exp(lg * (ti + 1.0))
            kd = (kr * jnp.exp(lg * (L - 1.0 - ti))).astype(BF16)
            s_ref[0, h] = math.exp(lg * L) * s_old + _dot_tn(kd, vb)
            o = inner + cross
            yb_ref[rs, sl] = (_head_norm(o, gret_ref[0:1, sl]) * _silu(z)).astype(BF16)
        return carry

    lax.fori_loop(0, MIX_ROWS // L, chunk, 0)
    conv_ref[0] = ubuf[6:8, :]


def _even_prompt(p, conv_w, g_ret, cosf, sins):
    nc = SEQ // MIX_ROWS
    rows = lambda b, c: (b * nc + c, 0)
    const2 = lambda b, c: (0, 0)
    return pl.pallas_call(
        _even_prompt_kernel,
        grid=(BATCH, nc),
        in_specs=[pl.BlockSpec((MIX_ROWS, E_IN), rows),
                  pl.BlockSpec((CONV_W, W_A), const2),
                  pl.BlockSpec((1, W_B), const2),
                  pl.BlockSpec((MIX_ROWS, DH_B), lambda b, c: (c, 0)),
                  pl.BlockSpec((MIX_ROWS, DH_B), lambda b, c: (c, 0))],
        out_specs=(pl.BlockSpec((MIX_ROWS, W_A), rows),
                   pl.BlockSpec((MIX_ROWS, W_B), rows),
                   pl.BlockSpec((1, CONV_W - 1, W_A), lambda b, c: (b, 0, 0)),
                   pl.BlockSpec((1, H_B, DH_B, DH_B), lambda b, c: (b, 0, 0, 0))),
        out_shape=(jax.ShapeDtypeStruct((BATCH * SEQ, W_A), BF16),
                   jax.ShapeDtypeStruct((BATCH * SEQ, W_B), BF16),
                   jax.ShapeDtypeStruct((BATCH, CONV_W - 1, W_A), F32),
                   jax.ShapeDtypeStruct((BATCH, H_B, DH_B, DH_B), F32)),
        scratch_shapes=[pltpu.VMEM((CHUNK + 8, W_A), F32)],
        compiler_params=_params(("arbitrary", "arbitrary")),
        name="even_prompt",
    )(p, conv_w, g_ret, cosf, sins)


FT = 1024
FN = BATCH * SEQ // FT
FPB = SEQ // FT
HG = 2
GW = HG * 128
PCH = 2
CONV_PCH = 4


def _chunk_pipeline(n, piece, stages, gate=None):
    sa, sb, sc, sd, se = stages
    npieces = n // PCH
    for j in range(PCH):
        piece(0, j)
    if gate is not None:
        gate(0)
    for c in range(n + 2):
        k, j = c // PCH + 1, c % PCH
        if k < npieces:
            piece(k, j)
        if c < n:
            sa(c)
        if 1 <= c <= n:
            sc(c - 1)
        if c < n:
            sb(c)
        if 1 <= c <= n:
            sd(c - 1)
        if c >= 2:
            se(c - 2)
        if gate is not None and j == PCH - 1 and k < npieces:
            gate(k)


def _even_heads_kernel(hp_ref, hs_ref, wq_ref, wk_ref, wv_ref, wz_ref, gret_ref, cos_ref, sin_ref, lg_ref,
                       wo_ref, yb_ref, s_ref, sq_ref, sk_ref, sv_ref, sz_ref, wob_ref, wb, pt, s_scr):
    s = pl.program_id(1)
    L = CHUNK

    @pl.when(s == 0)
    def _():
        for part, w_ref in enumerate((wq_ref, wk_ref, wv_ref, wz_ref)):
            wb[:, part * GW:(part + 1) * GW] = w_ref[...].astype(BF16)
        ps = _dot(hs_ref[...], wb[...])
        for part, o_ref in enumerate((sq_ref, sk_ref, sv_ref, sz_ref)):
            o_ref[...] = ps[:, part * GW:(part + 1) * GW]
        wob_ref[...] = wo_ref[...].astype(BF16)

    @pl.when(s > 0)
    def _():
        t = s - 1
        n = FT // L
        rows = lambda c: slice(c * L, (c + 1) * L)
        cols = lambda part, i: slice(part * GW + i * DH_B, part * GW + (i + 1) * DH_B)

        def piece(k, j):
            pr = slice(k * PCH * L, (k + 1) * PCH * L)
            pc = slice(j * 2 * GW, (j + 1) * 2 * GW)
            pt[pr, pc] = _dot(hp_ref[pr, :], wb[:, pc])

        row = lax.broadcasted_iota(jnp.int32, (L, L), 0)
        col = lax.broadcasted_iota(jnp.int32, (L, L), 1)
        causal = row >= col
        diff = jnp.maximum(row - col, 0).astype(F32)
        ti = lax.broadcasted_iota(jnp.int32, (L, 1), 0).astype(F32)
        lgs = [lg_ref[i][:, 0:1] for i in range(HG)]
        decay = [jnp.where(causal, jnp.exp(lg * diff), 0.0) for lg in lgs]
        q_decay = [jnp.exp(lg * (ti + 1.0)) for lg in lgs]
        k_decay = [jnp.exp(lg * (L - 1.0 - ti)) for lg in lgs]
        gamma_l = [jnp.exp(lg * float(L)) for lg in lgs]
        state = {0: [jnp.where(t % FPB == 0, 0.0, s_scr[i]) for i in range(HG)]}
        v = {}

        def stage_a(c):
            cosf = cos_ref[rows(c), :]
            sins = sin_ref[rows(c), :]
            v[c] = []
            for i in range(HG):
                kr = _rope(pt[rows(c), cols(1, i)], cosf, sins) * (DH_B ** -0.5)
                v[c].append(dict(qb=_rope(pt[rows(c), cols(0, i)], cosf, sins).astype(BF16),
                                 kb=kr.astype(BF16),
                                 kd=(kr * k_decay[i]).astype(BF16),
                                 vb=pt[rows(c), cols(2, i)].astype(BF16)))

        def stage_b(c):
            for i, d in enumerate(v[c]):
                d["sc"] = _dot_nt(d["qb"], d["kb"])
                d["cross"] = _dot(d["qb"], state[c][i].astype(BF16))
                d["upd"] = _dot_tn(d["kd"], d["vb"])

        def stage_c(c):
            state[c + 1] = []
            for i, d in enumerate(v[c]):
                d["sc"] = (d["sc"] * decay[i]).astype(BF16)
                state[c + 1].append(gamma_l[i] * state[c][i] + d["upd"])

        def stage_d(c):
            for d in v[c]:
                d["inner"] = _dot(d["sc"], d["vb"])

        def stage_e(c):
            for i, d in enumerate(v.pop(c)):
                o = d["inner"] + d["cross"] * q_decay[i]
                g = gret_ref[0:1, i * DH_B:(i + 1) * DH_B]
                z = pt[rows(c), cols(3, i)]
                yb_ref[rows(c), i * DH_B:(i + 1) * DH_B] = (_head_norm(o, g) * _silu(z)).astype(BF16)

        _chunk_pipeline(n, piece, (stage_a, stage_b, stage_c, stage_d, stage_e))
        for i in range(HG):
            s_scr[i] = state[n][i]
            s_ref[0, i] = state[n][i]


def _even_heads(hp, hs, w, g_ret, cosf, sins, lg_tab, w_out):
    k = hp.shape[1]
    ms = hs.shape[0]
    ng = H_B // HG
    base = 4 * W_A // GW
    tile = lambda s: jnp.maximum(s - 1, 0)
    wspec = lambda part: pl.BlockSpec((k, GW), lambda g, s: (0, base + part * ng + g))
    sspec = pl.BlockSpec((ms, GW), lambda g, s: (0, g))
    sshape = jax.ShapeDtypeStruct((ms, W_B), F32)
    return pl.pallas_call(
        _even_heads_kernel,
        grid=(ng, FN + 1),
        in_specs=[pl.BlockSpec((FT, k), lambda g, s: (tile(s), 0)),
                  pl.BlockSpec((ms, k), lambda g, s: (0, 0)),
                  wspec(0), wspec(1), wspec(2), wspec(3),
                  pl.BlockSpec((1, GW), lambda g, s: (0, g)),
                  pl.BlockSpec((FT, DH_B), lambda g, s: (tile(s) % FPB, 0)),
                  pl.BlockSpec((FT, DH_B), lambda g, s: (tile(s) % FPB, 0)),
                  pl.BlockSpec((HG, 1, 128), lambda g, s: (g, 0, 0)),
                  pl.BlockSpec((w_out.shape[0] // ng, w_out.shape[1]), lambda g, s: (g, 0),
                               pipeline_mode=pl.Buffered(1))],
        out_specs=(pl.BlockSpec((FT, GW), lambda g, s: (tile(s), g)),
                   pl.BlockSpec((1, HG, DH_B, DH_B), lambda g, s: (tile(s) // FPB, g, 0, 0)),
                   sspec, sspec, sspec, sspec,
                   pl.BlockSpec((w_out.shape[0] // ng, w_out.shape[1]), lambda g, s: (g, 0))),
        out_shape=(jax.ShapeDtypeStruct((BATCH * SEQ, W_B), BF16),
                   jax.ShapeDtypeStruct((BATCH, H_B, DH_B, DH_B), F32),
                   sshape, sshape, sshape, sshape,
                   jax.ShapeDtypeStruct(w_out.shape, BF16)),
        scratch_shapes=[pltpu.VMEM((k, 4 * GW), BF16),
                        pltpu.VMEM((FT, 4 * GW), F32),
                        pltpu.VMEM((HG, DH_B, DH_B), F32)],
        compiler_params=_params(("arbitrary", "arbitrary")),
        name="even_heads",
    )(hp, hs, w, w, w, w, g_ret, cosf, sins, lg_tab, w_out)


def _even_conv_kernel(hp_ref, hs_ref, wb_ref, wc_ref, wx_ref, wz_ref, cw_ref, wo_ref,
                      ya_ref, conv_ref, sb_ref, sc_ref, sx_ref, sz_ref, wob_ref, wb, pt, ubuf):
    s = pl.program_id(1)
    L = CHUNK

    @pl.when(s == 0)
    def _():
        for part, w_ref in enumerate((wb_ref, wc_ref, wx_ref, wz_ref)):
            wb[:, part * GW:(part + 1) * GW] = w_ref[...].astype(BF16)
        ps = _dot(hs_ref[...], wb[...])
        for part, o_ref in enumerate((sb_ref, sc_ref, sx_ref, sz_ref)):
            o_ref[...] = ps[:, part * GW:(part + 1) * GW]
        wob_ref[...] = wo_ref[...].astype(BF16)

    @pl.when(s > 0)
    def _():
        t = s - 1
        n = FT // L
        rows = lambda c: slice(c * L, (c + 1) * L)
        part = lambda p, c: pt[rows(c), p * GW:(p + 1) * GW]

        @pl.when(t % FPB == 0)
        def _():
            ubuf[0:8, :] = jnp.zeros((8, GW), F32)

        @pl.when(t % FPB != 0)
        def _():
            ubuf[0:8, :] = ubuf[FT:FT + 8, :]

        def piece(k, j):
            pr = slice(k * CONV_PCH * L, (k + 1) * CONV_PCH * L)
            pc = slice(j * 2 * GW, (j + 1) * 2 * GW)
            pt[pr, pc] = _dot(hp_ref[pr, :], wb[:, pc])

        todo = [(k, j) for k in range(n // CONV_PCH) for j in range(2)]
        piece(*todo.pop(0))
        piece(*todo.pop(0))
        for c in range(n):
            if todo:
                piece(*todo.pop(0))
            u = part(1, c) * part(2, c)
            ubuf[8 + c * L:8 + (c + 1) * L, :] = u
            t0 = ubuf[6 + c * L:6 + (c + 1) * L, :]
            t1 = ubuf[7 + c * L:7 + (c + 1) * L, :]
            conv = cw_ref[0:1, :] * t0 + cw_ref[1:2, :] * t1 + cw_ref[2:3, :] * u
            ya_ref[rows(c), :] = (part(0, c) * conv * _silu(part(3, c))).astype(BF16)
        conv_ref[0] = ubuf[FT + 6:FT + 8, :]


def _even_conv(hp, hs, w, conv_w, w_out):
    k = hp.shape[1]
    ms = hs.shape[0]
    ng = W_A // GW
    tile = lambda s: jnp.maximum(s - 1, 0)
    wspec = lambda part: pl.BlockSpec((k, GW), lambda g, s: (0, part * ng + g))
    sspec = pl.BlockSpec((ms, GW), lambda g, s: (0, g))
    sshape = jax.ShapeDtypeStruct((ms, W_A), F32)
    return pl.pallas_call(
        _even_conv_kernel,
        grid=(ng, FN + 1),
        in_specs=[pl.BlockSpec((FT, k), lambda g, s: (tile(s), 0)),
                  pl.BlockSpec((ms, k), lambda g, s: (0, 0)),
                  wspec(0), wspec(1), wspec(2), wspec(3),
                  pl.BlockSpec((CONV_W, GW), lambda g, s: (0, g)),
                  pl.BlockSpec((w_out.shape[0] // ng, w_out.shape[1]), lambda g, s: (g, 0),
                               pipeline_mode=pl.Buffered(1))],
        out_specs=(pl.BlockSpec((FT, GW), lambda g, s: (tile(s), g)),
                   pl.BlockSpec((1, CONV_W - 1, GW), lambda g, s: (tile(s) // FPB, 0, g)),
                   sspec, sspec, sspec, sspec,
                   pl.BlockSpec((w_out.shape[0] // ng, w_out.shape[1]), lambda g, s: (g, 0))),
        out_shape=(jax.ShapeDtypeStruct((BATCH * SEQ, W_A), BF16),
                   jax.ShapeDtypeStruct((BATCH, CONV_W - 1, W_A), F32),
                   sshape, sshape, sshape, sshape,
                   jax.ShapeDtypeStruct(w_out.shape, BF16)),
        scratch_shapes=[pltpu.VMEM((k, 4 * GW), BF16),
                        pltpu.VMEM((FT, 4 * GW), F32),
                        pltpu.VMEM((FT + 8, GW), F32)],
        compiler_params=_params(("arbitrary", "arbitrary")),
        name="even_conv",
    )(hp, hs, w, w, w, w, conv_w, w_out)


SB = 32
SR = SB * DEC_SEQ


def _even_sample_kernel(ab_ref, ac_ref, ax_ref, az_ref, pq_ref, pk_ref, pv_ref, pz_ref, st_ref, s_ref,
                        cw_ref, gret_ref, cos_ref, sin_ref, lg_ref,
                        ya_ref, u_ref, yb_ref, so_ref, cross_scr):
    h = pl.program_id(1)
    row = lax.broadcasted_iota(jnp.int32, (SR, SR), 0)
    col = lax.broadcasted_iota(jnp.int32, (SR, SR), 1)
    trow = row & 3

    @pl.when(h == 0)
    def _():
        for j in range(W_A // 128):
            sl = slice(j * 128, (j + 1) * 128)
            a_b = ab_ref[:, sl]
            a_c = ac_ref[:, sl]
            a_x = ax_ref[:, sl]
            a_z = az_ref[:, sl]
            u = a_c * a_x
            e = st_ref[:, sl]
            tap1 = jnp.where(trow >= 1, pltpu.roll(u, 1, 0), pltpu.roll(e, SR - 1, 0))
            tap0 = jnp.where(trow >= 2, pltpu.roll(u, 2, 0), e)
            conv = cw_ref[0:1, sl] * tap0 + cw_ref[1:2, sl] * tap1 + cw_ref[2:3, sl] * u
            ya_ref[:, sl] = (a_b * conv * _silu(a_z)).astype(BF16)
            u_ref[:, sl] = u

    lg = lg_ref[0][:, 0:1]
    same = (row >> 2) == (col >> 2)
    dd = trow - (col & 3)
    mask = jnp.where(same, dd, -1) >= 0
    decay = jnp.where(mask, jnp.exp(lg * jnp.maximum(dd, 0).astype(F32)), 0.0)
    tcol = (lax.broadcasted_iota(jnp.int32, (SR, 1), 0) & 3).astype(F32)
    cosf = cos_ref[...]
    sins = sin_ref[...]
    qr = _rope(pq_ref[...], cosf, sins)
    kr = _rope(pk_ref[...], cosf, sins) * (DH_B ** -0.5)
    qb = qr.astype(BF16)
    kb = kr.astype(BF16)
    vb = pv_ref[...].astype(BF16)
    sc = _dot_nt(qb, kb) * decay
    inner = _dot(sc.astype(BF16), vb)
    kdt = (kr * jnp.exp(lg * (DEC_SEQ - 1.0 - tcol))).T
    gamma_l = jnp.exp(lg * float(DEC_SEQ))
    lane_b = col >> 2
    sub = lax.broadcasted_iota(jnp.int32, (8, DH_B), 0)
    for g in range(SR // 8):
        q8 = qr[8 * g:8 * g + 8, :]
        q2 = jnp.concatenate([jnp.where(sub < DEC_SEQ, q8, 0.0), jnp.where(sub < DEC_SEQ, 0.0, q8)], axis=1)
        s_pair = [s_ref[2 * g + beta, 0] for beta in range(2)]
        cross_scr[8 * g:8 * g + 8, :] = _dot(
            q2.astype(BF16), jnp.concatenate([sp.astype(BF16) for sp in s_pair], axis=0))
        for beta in range(2):
            b = 2 * g + beta
            lhs = jnp.where(lane_b == b, kdt, 0.0).astype(BF16)
            so_ref[b, 0] = gamma_l * s_pair[beta] + _dot(lhs, vb)
    o = inner + cross_scr[...] * jnp.exp(lg * (tcol + 1.0))
    yb_ref[...] = (_head_norm(o, gret_ref[...]) * _silu(pz_ref[...])).astype(BF16)


def _even_sample(pa, pb, st_exp, s_state, conv_w, g_ret, cosf, sins, lg_tab):
    nb = DEC_BATCH // SB
    const2 = lambda i, h: (0, 0)
    aspec = pl.BlockSpec((SR, W_A), lambda i, h: (i, 0))
    hspec = pl.BlockSpec((SR, DH_B), lambda i, h: (i, h))
    return pl.pallas_call(
        _even_sample_kernel,
        grid=(nb, H_B),
        in_specs=[aspec, aspec, aspec, aspec,
                  hspec, hspec, hspec, hspec,
                  pl.BlockSpec((SR, W_A), lambda i, h: (i, 0)),
                  pl.BlockSpec((SB, 1, DH_B, DH_B), lambda i, h: (i, h, 0, 0)),
                  pl.BlockSpec((CONV_W, W_A), const2),
                  pl.BlockSpec((1, DH_B), lambda i, h: (0, h)),
                  pl.BlockSpec((SR, DH_B), const2),
                  pl.BlockSpec((SR, DH_B), const2),
                  pl.BlockSpec((1, 1, 128), lambda i, h: (h, 0, 0))],
        out_specs=(pl.BlockSpec((SR, W_A), lambda i, h: (i, 0)),
                   pl.BlockSpec((SR, W_A), lambda i, h: (i, 0)),
                   pl.BlockSpec((SR, DH_B), lambda i, h: (i, h)),
                   pl.BlockSpec((SB, 1, DH_B, DH_B), lambda i, h: (i, h, 0, 0))),
        out_shape=(jax.ShapeDtypeStruct((DEC_BATCH * DEC_SEQ, W_A), BF16),
                   jax.ShapeDtypeStruct((DEC_BATCH * DEC_SEQ, W_A), F32),
                   jax.ShapeDtypeStruct((DEC_BATCH * DEC_SEQ, W_B), BF16),
                   jax.ShapeDtypeStruct((DEC_BATCH, H_B, DH_B, DH_B), F32)),
        scratch_shapes=[pltpu.VMEM((SR, DH_B), F32)],
        compiler_params=_params(("arbitrary", "arbitrary")),
        name="even_sample",
    )(*pa, *pb, st_exp, s_state, conv_w, g_ret, cosf, sins, lg_tab)


def _odd_prompt_kernel(p_ref, h_ref, wg_ref, bg_ref, gm_ref, lng_ref, lnb_ref, ws_ref, bst_ref,
                       yc_ref, yd_ref, c_ref, n_ref, m_ref, wgb, wsb):
    L = CHUNK

    @pl.when(jnp.logical_and(pl.program_id(0) == 0, pl.program_id(1) == 0))
    def _():
        wgb[...] = wg_ref[...].astype(BF16)
        keep = (lax.broadcasted_iota(jnp.int32, (L, L), 0) >= lax.broadcasted_iota(jnp.int32, (L, L), 1))
        for g in range(G_D):
            wsb[g] = jnp.where(keep, ws_ref[g], 0.0).astype(BF16)

    @pl.when(pl.program_id(1) == 0)
    def _():
        c_ref[...] = jnp.zeros_like(c_ref)
        n_ref[...] = jnp.zeros_like(n_ref)
        m_ref[...] = jnp.zeros_like(m_ref)

    def chunk(ci, carry):
        rs = pl.ds(pl.multiple_of(ci * L, L), L)
        row = lax.broadcasted_iota(jnp.int32, (L, L), 0)
        col = lax.broadcasted_iota(jnp.int32, (L, L), 1)
        tri = row >= col

        pre = _dot_nt(h_ref[rs, :], wgb[...]) + bg_ref[...]
        lf = _log_sigmoid(pre)
        b_c = _dot_hi(jnp.where(tri, 1.0, 0.0), lf)
        b_r = b_c.T
        pre_r = pre.T
        for h in range(H_C):
            bc = b_c[:, H_C + h:H_C + h + 1]
            br = b_r[H_C + h:H_C + h + 1, :]
            igr = pre_r[h:h + 1, :]
            igc = pre[:, h:h + 1]
            m_prev = m_ref[0, h:h + 1, 0:1]
            log_d = jnp.where(tri, bc - br + igr, NEG_INF)
            log_inter = bc + m_prev
            m_t = jnp.maximum(log_inter, jnp.max(log_d, axis=-1, keepdims=True))
            w = jnp.exp(log_d - m_t)
            w_inter = jnp.exp(log_inter - m_t)
            q = p_ref[rs, h * DQK_C:(h + 1) * DQK_C] * (DQK_C ** -0.5)
            k = p_ref[rs, H_C * DQK_C + h * DQK_C:H_C * DQK_C + (h + 1) * DQK_C]
            v = p_ref[rs, 2 * H_C * DQK_C + h * DV_C:2 * H_C * DQK_C + (h + 1) * DV_C]
            z = p_ref[rs, O_GATE - W_C + h * DV_C:O_GATE - W_C + (h + 1) * DV_C]
            qb = q.astype(BF16)
            kb = k.astype(BF16)
            vb = v.astype(BF16)
            sc = _dot_nt(qb, kb) * w
            c_old = c_ref[0, h]
            n_old = n_ref[0, h:h + 1, :]
            num = _dot(sc.astype(BF16), vb) + w_inter * _dot_nt(qb, c_old.astype(BF16))
            den = jnp.sum(sc, axis=-1, keepdims=True) + w_inter * jnp.sum(q * n_old, axis=-1, keepdims=True)
            hh = num / jnp.maximum(jnp.abs(den), jnp.exp(-m_t))
            m_new = m_t[L - 1:L, :]
            b_last = bc[L - 1:L, :]
            w_end = jnp.exp(b_last - bc + igc - m_new)
            cd = jnp.exp(b_last + m_prev - m_new)
            c_ref[0, h] = cd * c_old + _dot_tn((v * w_end).astype(BF16), kb)
            n_ref[0, h:h + 1, :] = cd * n_old + jnp.sum(w_end * k, axis=0, keepdims=True)
            m_ref[0, h:h + 1, :] = jnp.broadcast_to(m_new, (1, 128))
            sl = slice(h * DV_C, (h + 1) * DV_C)
            yc_ref[rs, sl] = (_head_norm(hh, gm_ref[0:1, sl]) * _silu(z)).astype(BF16)

        dv = lambda g: p_ref[rs, O_GATE + W_D + g * 128:O_GATE + W_D + (g + 1) * 128]
        tot = dv(0)
        for g in range(1, G_D):
            tot = tot + dv(g)
        mu = jnp.sum(tot, axis=-1, keepdims=True) * (1.0 / W_D)
        sq = (dv(0) - mu) * (dv(0) - mu)
        for g in range(1, G_D):
            sq = sq + (dv(g) - mu) * (dv(g) - mu)
        rstd = lax.rsqrt(jnp.sum(sq, axis=-1, keepdims=True) * (1.0 / W_D) + EPS)
        for g in range(G_D):
            sl = slice(g * 128, (g + 1) * 128)
            vn = (dv(g) - mu) * rstd * lng_ref[0:1, sl] + lnb_ref[0:1, sl]
            s = _dot(wsb[g], vn.astype(BF16)) + bst_ref[:, g:g + 1]
            d_u = p_ref[rs, O_GATE + g * 128:O_GATE + (g + 1) * 128]
            d_z = p_ref[rs, O_GATE + 2 * W_D + g * 128:O_GATE + 2 * W_D + (g + 1) * 128]
            yd_ref[rs, sl] = (d_u * s * _silu(d_z)).astype(BF16)
        return carry

    lax.fori_loop(0, MIX_ROWS // L, chunk, 0)


def _odd_prompt(p, h, w_o, bg, gm, lng, lnb, ws, bst):
    nc = SEQ // MIX_ROWS
    rows = lambda b, c: (b * nc + c, 0)
    const2 = lambda b, c: (0, 0)
    return pl.pallas_call(
        _odd_prompt_kernel,
        grid=(BATCH, nc),
        in_specs=[pl.BlockSpec((MIX_ROWS, O_N), rows),
                  pl.BlockSpec((MIX_ROWS, D_MODEL), rows),
                  pl.BlockSpec((128, D_MODEL), lambda b, c: (O_GATE // 128, 0)),
                  pl.BlockSpec((1, 128), const2),
                  pl.BlockSpec((1, W_C), const2),
                  pl.BlockSpec((1, W_D), const2),
                  pl.BlockSpec((1, W_D), const2),
                  pl.BlockSpec((G_D, CHUNK, CHUNK), lambda b, c: (0, 0, 0)),
                  pl.BlockSpec((CHUNK, G_D), const2)],
        out_specs=(pl.BlockSpec((MIX_ROWS, W_C), rows),
                   pl.BlockSpec((MIX_ROWS, W_D), rows),
                   pl.BlockSpec((1, H_C, DV_C, DQK_C), lambda b, c: (b, 0, 0, 0)),
                   pl.BlockSpec((1, H_C, DQK_C), lambda b, c: (b, 0, 0)),
                   pl.BlockSpec((1, 8, 128), lambda b, c: (b, 0, 0))),
        out_shape=(jax.ShapeDtypeStruct((BATCH * SEQ, W_C), BF16),
                   jax.ShapeDtypeStruct((BATCH * SEQ, W_D), BF16),
                   jax.ShapeDtypeStruct((BATCH, H_C, DV_C, DQK_C), F32),
                   jax.ShapeDtypeStruct((BATCH, H_C, DQK_C), F32),
                   jax.ShapeDtypeStruct((BATCH, 8, 128), F32)),
        scratch_shapes=[pltpu.VMEM((128, D_MODEL), BF16),
                        pltpu.VMEM((G_D, CHUNK, CHUNK), BF16)],
        compiler_params=_params(("arbitrary", "arbitrary")),
        name="odd_prompt",
    )(p, h, w_o, bg, gm, lng, lnb, ws, bst)


CG = 2
CW = 128 + CG * (2 * DQK_C + 2 * DV_C)
CT = 1024
CN = BATCH * SEQ // CT
CPB = SEQ // CT


def _odd_heads_kernel(hp_ref, hs_ref, wq_ref, wk_ref, wv_ref, wz_ref, wg_ref, bg_ref, gm_ref,
                      yc_ref, c_ref, n_ref, m_ref, sq_ref, sk_ref, sv_ref, sz_ref,
                      wb, pt, c_scr, n_scr, m_scr, gt_scr):
    grp = pl.program_id(0)
    s = pl.program_id(1)
    L = CHUNK
    gc = slice(0, 128)
    qc = slice(128, 128 + CG * DQK_C)
    kc = slice(qc.stop, qc.stop + CG * DQK_C)
    vc = slice(kc.stop, kc.stop + CG * DV_C)
    zc = slice(vc.stop, vc.stop + CG * DV_C)
    head = lambda sl, i, w: slice(sl.start + i * w, sl.start + (i + 1) * w)

    @pl.when(s == 0)
    def _():
        wb[qc, :] = wq_ref[...].astype(BF16)
        wb[kc, :] = wk_ref[...].astype(BF16)
        wb[vc, :] = wv_ref[...].astype(BF16)
        wb[zc, :] = wz_ref[...].astype(BF16)
        wb[gc, :] = wg_ref[...].astype(BF16)
        ps = _dot_nt(hs_ref[...], wb[qc.start:CW, :])
        off = lambda sl: slice(sl.start - qc.start, sl.stop - qc.start)
        sq_ref[...] = ps[:, off(qc)]
        sk_ref[...] = ps[:, off(kc)]
        sv_ref[...] = ps[:, off(vc)]
        sz_ref[...] = ps[:, off(zc)]

    @pl.when(s > 0)
    def _():
        t = s - 1
        n = CT // L
        rows = lambda c: slice(c * L, (c + 1) * L)

        def piece(k, j):
            pr = slice(k * PCH * L, (k + 1) * PCH * L)
            pc = (slice(0, vc.start), slice(vc.start, CW))[j]
            pt[pr, pc] = _dot_nt(hp_ref[pr, :], wb[pc, :])

        row = lax.broadcasted_iota(jnp.int32, (L, L), 0)
        col = lax.broadcasted_iota(jnp.int32, (L, L), 1)
        tri = row >= col
        fresh = t % CPB == 0
        cst = {0: [jnp.where(fresh, 0.0, c_scr[i]) for i in range(CG)]}
        nst = {0: [jnp.where(fresh, 0.0, n_scr[i]) for i in range(CG)]}
        mst = {0: [jnp.where(fresh, 0.0, m_scr[i, 0:1, 0:1]) for i in range(CG)]}
        v = {}
        gates = {}

        def gate(k):
            cs = range(k * PCH, (k + 1) * PCH)
            for c in cs:
                gt_scr[c] = (pt[rows(c), gc] + bg_ref[...]).T
            pad = jnp.zeros((8 - CG * PCH, L), F32)
            ig_rows = jnp.concatenate(
                [gt_scr[c, pl.ds(grp * CG + i, 1), :] for i in range(CG) for c in cs] + [pad], axis=0)
            lf_rows = jnp.concatenate(
                [_log_sigmoid(gt_scr[c, pl.ds(grp * CG + i + H_C, 1), :]) for i in range(CG) for c in cs]
                + [pad], axis=0)
            b_rows = _dot_hi(lf_rows, jnp.where(row <= col, 1.0, 0.0))
            tall = jnp.zeros((L - 8, L), F32)
            gates[k] = dict(ig_rows=ig_rows, b_rows=b_rows,
                            b_cols=jnp.concatenate([b_rows, tall], axis=0).T,
                            ig_cols=jnp.concatenate([ig_rows, tall], axis=0).T)

        def stage_a(c):
            gk = gates[c // PCH]
            v[c] = []
            nst[c + 1] = []
            mst[c + 1] = []
            for i in range(CG):
                r = i * PCH + c % PCH
                b_r = gk["b_rows"][r:r + 1, :]
                ig_r = gk["ig_rows"][r:r + 1, :]
                b_c = gk["b_cols"][:, r:r + 1]
                ig_c = gk["ig_cols"][:, r:r + 1]
                m_prev = mst[c][i]
                log_d = jnp.where(tri, b_c - b_r + ig_r, NEG_INF)
                log_inter = b_c + m_prev
                m_t = jnp.maximum(log_inter, jnp.max(log_d, axis=-1, keepdims=True))
                m_new = m_t[L - 1:L, :]
                b_last = b_c[L - 1:L, :]
                w_end = jnp.exp(b_last - b_c + ig_c - m_new)
                cd = jnp.exp(b_last + m_prev - m_new)
                q = pt[rows(c), head(qc, i, DQK_C)] * (DQK_C ** -0.5)
                k = pt[rows(c), head(kc, i, DQK_C)]
                vv = pt[rows(c), head(vc, i, DV_C)]
                nst[c + 1].append(cd * nst[c][i] + jnp.sum(w_end * k, axis=0, keepdims=True))
                mst[c + 1].append(m_new)
                v[c].append(dict(w=jnp.exp(log_d - m_t), w_inter=jnp.exp(log_inter - m_t),
                                 floor=jnp.exp(-m_t), cd=cd, qb=q.astype(BF16), kb=k.astype(BF16),
                                 vb=vv.astype(BF16), vw=(vv * w_end).astype(BF16),
                                 qn=jnp.sum(q * nst[c][i], axis=-1, keepdims=True)))

        def stage_b(c):
            for i, d in enumerate(v[c]):
                d["sc"] = _dot_nt(d["qb"], d["kb"])
                d["upd"] = _dot_tn(d["vw"], d["kb"])
            for i, d in enumerate(v[c]):
                d["inter"] = _dot_nt(d["qb"], cst[c][i].astype(BF16))

        def stage_c(c):
            cst[c + 1] = []
            for i, d in enumerate(v[c]):
                sc = d["sc"] * d["w"]
                d["den"] = jnp.sum(sc, axis=-1, keepdims=True) + d["w_inter"] * d["qn"]
                d["sc"] = sc.astype(BF16)
                cst[c + 1].append(d["cd"] * cst[c][i] + d["upd"])

        def stage_d(c):
            for d in v[c]:
                d["num"] = _dot(d["sc"], d["vb"])

        def stage_e(c):
            for i, d in enumerate(v.pop(c)):
                num = d["num"] + d["w_inter"] * d["inter"]
                hh = num / jnp.maximum(jnp.abs(d["den"]), d["floor"])
                z = pt[rows(c), head(zc, i, DV_C)]
                ys = slice(i * DV_C, (i + 1) * DV_C)
                yc_ref[rows(c), ys] = (_head_norm(hh, gm_ref[0:1, ys]) * _silu(z)).astype(BF16)

        _chunk_pipeline(n, piece, (stage_a, stage_b, stage_c, stage_d, stage_e), gate)
        for i in range(CG):
            c_scr[i] = cst[n][i]
            n_scr[i] = nst[n][i]
            m_scr[i] = jnp.broadcast_to(mst[n][i], (8, 128))
            c_ref[0, i] = cst[n][i]
            n_ref[0, i] = nst[n][i]
            m_ref[0, i] = jnp.broadcast_to(mst[n][i], (1, 128))


def _odd_heads(hp, hs, w_t, bg, gm):
    k = hp.shape[1]
    ms = hs.shape[0]
    tile = lambda s: jnp.maximum(s - 1, 0)
    qw, vw = CG * DQK_C, CG * DV_C
    koff = H_C * DQK_C // qw
    voff = 2 * H_C * DQK_C // vw
    zoff = (2 * H_C * DQK_C + W_C) // vw
    seq = lambda g, s: (tile(s) // CPB, g, 0, 0)
    once = pl.Buffered(1)
    return pl.pallas_call(
        _odd_heads_kernel,
        grid=(H_C // CG, CN + 1),
        in_specs=[pl.BlockSpec((CT, k), lambda g, s: (tile(s), 0)),
                  pl.BlockSpec((ms, k), lambda g, s: (0, 0), pipeline_mode=once),
                  pl.BlockSpec((qw, k), lambda g, s: (g, 0), pipeline_mode=once),
                  pl.BlockSpec((qw, k), lambda g, s: (koff + g, 0), pipeline_mode=once),
                  pl.BlockSpec((vw, k), lambda g, s: (voff + g, 0), pipeline_mode=once),
                  pl.BlockSpec((vw, k), lambda g, s: (zoff + g, 0), pipeline_mode=once),
                  pl.BlockSpec((128, k), lambda g, s: (O_GATE // 128, 0), pipeline_mode=once),
                  pl.BlockSpec((1, 128), lambda g, s: (0, 0)),
                  pl.BlockSpec((1, vw), lambda g, s: (0, g))],
        out_specs=(pl.BlockSpec((CT, vw), lambda g, s: (tile(s), g)),
                   pl.BlockSpec((1, CG, DV_C, DQK_C), seq),
                   pl.BlockSpec((1, CG, 1, DQK_C), seq),
                   pl.BlockSpec((1, CG, 1, 128), seq),
                   pl.BlockSpec((ms, qw), lambda g, s: (0, g)),
                   pl.BlockSpec((ms, qw), lambda g, s: (0, g)),
                   pl.BlockSpec((ms, vw), lambda g, s: (0, g)),
                   pl.BlockSpec((ms, vw), lambda g, s: (0, g))),
        out_shape=(jax.ShapeDtypeStruct((BATCH * SEQ, W_C), BF16),
                   jax.ShapeDtypeStruct((BATCH, H_C, DV_C, DQK_C), F32),
                   jax.ShapeDtypeStruct((BATCH, H_C, 1, DQK_C), F32),
                   jax.ShapeDtypeStruct((BATCH, H_C, 1, 128), F32),
                   jax.ShapeDtypeStruct((ms, H_C * DQK_C), F32),
                   jax.ShapeDtypeStruct((ms, H_C * DQK_C), F32),
                   jax.ShapeDtypeStruct((ms, W_C), F32),
                   jax.ShapeDtypeStruct((ms, W_C), F32)),
        scratch_shapes=[pltpu.VMEM((CW, k), BF16),
                        pltpu.VMEM((CT, CW), F32),
                        pltpu.VMEM((CG, DV_C, DQK_C), F32),
                        pltpu.VMEM((CG, 1, DQK_C), F32),
                        pltpu.VMEM((CG, 8, 128), F32),
                        pltpu.VMEM((CT // CHUNK, CHUNK, CHUNK), F32)],
        compiler_params=_params(("arbitrary", "arbitrary")),
        name="odd_heads",
    )(hp, hs, w_t, w_t, w_t, w_t, w_t, bg, gm)


def _odd_mlp_kernel(p_ref, lng_ref, lnb_ref, ws_ref, bst_ref, yd_ref, wsb):
    L = CHUNK

    @pl.when(pl.program_id(0) == 0)
    def _():
        keep = (lax.broadcasted_iota(jnp.int32, (L, L), 0) >= lax.broadcasted_iota(jnp.int32, (L, L), 1))
        for g in range(G_D):
            wsb[g] = jnp.where(keep, ws_ref[g], 0.0).astype(BF16)

    def chunk(ci, carry):
        rs = pl.ds(pl.multiple_of(ci * L, L), L)
        dv = lambda g: p_ref[rs, W_D + g * 128:W_D + (g + 1) * 128]
        tot = dv(0)
        for g in range(1, G_D):
            tot = tot + dv(g)
        mu = jnp.sum(tot, axis=-1, keepdims=True) * (1.0 / W_D)
        sq = (dv(0) - mu) * (dv(0) - mu)
        for g in range(1, G_D):
            sq = sq + (dv(g) - mu) * (dv(g) - mu)
        rstd = lax.rsqrt(jnp.sum(sq, axis=-1, keepdims=True) * (1.0 / W_D) + EPS)
        for g in range(G_D):
            sl = slice(g * 128, (g + 1) * 128)
            vn = (dv(g) - mu) * rstd * lng_ref[0:1, sl] + lnb_ref[0:1, sl]
            s = _dot(wsb[g], vn.astype(BF16)) + bst_ref[:, g:g + 1]
            d_u = p_ref[rs, g * 128:(g + 1) * 128]
            d_z = p_ref[rs, 2 * W_D + g * 128:2 * W_D + (g + 1) * 128]
            yd_ref[rs, sl] = (d_u * s * _silu(d_z)).astype(BF16)
        return carry

    lax.fori_loop(0, p_ref.shape[0] // L, chunk, 0)


def _odd_mlp(p, lng, lnb, ws, bst, rows_per_step):
    m = p.shape[0]
    rows = lambda i: (i, 0)
    const2 = lambda i: (0, 0)
    return pl.pallas_call(
        _odd_mlp_kernel,
        grid=(m // rows_per_step,),
        in_specs=[pl.BlockSpec((rows_per_step, 3 * W_D), rows),
                  pl.BlockSpec((1, W_D), const2),
                  pl.BlockSpec((1, W_D), const2),
                  pl.BlockSpec((G_D, CHUNK, CHUNK), lambda i: (0, 0, 0)),
                  pl.BlockSpec((CHUNK, G_D), const2)],
        out_specs=pl.BlockSpec((rows_per_step, W_D), rows),
        out_shape=jax.ShapeDtypeStruct((m, W_D), BF16),
        scratch_shapes=[pltpu.VMEM((G_D, CHUNK, CHUNK), BF16)],
        compiler_params=_params(("arbitrary",)),
        name="odd_mlp",
    )(p, lng, lnb, ws, bst)


DT = 512


def _odd_mlp_fused_kernel(hp_ref, w0_ref, w1_ref, w2_ref, wt_ref, lng_ref, lnb_ref, ws_ref, bst_ref,
                          yd_ref, wb, pt, wsb):
    s = pl.program_id(0)
    L = CHUNK
    uc = slice(0, W_D)
    vc = slice(W_D, 2 * W_D)
    zc = slice(2 * W_D, 3 * W_D)

    @pl.when(s == 0)
    def _():
        sh = N_GATE
        wb[uc, :] = jnp.concatenate([w0_ref[sh:, :], w1_ref[0:sh, :]], axis=0).astype(BF16)
        wb[vc, :] = jnp.concatenate([w1_ref[sh:, :], w2_ref[0:sh, :]], axis=0).astype(BF16)
        wb[zc, :] = jnp.concatenate([w2_ref[sh:, :], wt_ref[...]], axis=0).astype(BF16)
        keep = (lax.broadcasted_iota(jnp.int32, (L, L), 0) >= lax.broadcasted_iota(jnp.int32, (L, L), 1))
        for g in range(G_D):
            wsb[g] = jnp.where(keep, ws_ref[g], 0.0).astype(BF16)

    @pl.when(s > 0)
    def _():
        n = DT // L
        rows = lambda c: slice(c * L, (c + 1) * L)
        grp = lambda sl, g: slice(sl.start + g * 128, sl.start + (g + 1) * 128)
        vn = {}
        mix = {}

        def project(pc):
            pt[:, pc] = _dot_nt(hp_ref[...], wb[pc, :])

        def stage_a(c):
            dv = lambda g: pt[rows(c), grp(vc, g)]
            tot = dv(0)
            for g in range(1, G_D):
                tot = tot + dv(g)
            mu = jnp.sum(tot, axis=-1, keepdims=True) * (1.0 / W_D)
            sq = (dv(0) - mu) * (dv(0) - mu)
            for g in range(1, G_D):
                sq = sq + (dv(g) - mu) * (dv(g) - mu)
            rstd = lax.rsqrt(jnp.sum(sq, axis=-1, keepdims=True) * (1.0 / W_D) + EPS)
            vn[c] = [((dv(g) - mu) * rstd * lng_ref[0:1, g * 128:(g + 1) * 128]
                      + lnb_ref[0:1, g * 128:(g + 1) * 128]).astype(BF16) for g in range(G_D)]

        def stage_b(c):
            mix[c] = [_dot(wsb[g], vn[c][g]) for g in range(G_D)]

        def stage_e(c):
            for g in range(G_D):
                sg = mix[c][g] + bst_ref[:, g:g + 1]
                d_u = pt[rows(c), grp(uc, g)]
                d_z = pt[rows(c), grp(zc, g)]
                yd_ref[rows(c), g * 128:(g + 1) * 128] = (d_u * sg * _silu(d_z)).astype(BF16)

        project(vc)
        for c in range(n):
            stage_a(c)
        project(uc)
        for c in range(n):
            stage_b(c)
        project(zc)
        for c in range(n):
            stage_e(c)


def _odd_mlp_fused(hp, w_t, lng, lnb, ws, bst):
    k = hp.shape[1]
    m = hp.shape[0]
    tile = lambda s: (jnp.maximum(s - 1, 0), 0)
    const2 = lambda s: (0, 0)
    t0 = O_GATE // IN_TN
    once = pl.Buffered(1)
    wspec = lambda j: pl.BlockSpec((IN_TN, k), lambda s: (t0 + j, 0), pipeline_mode=once)
    return pl.pallas_call(
        _odd_mlp_fused_kernel,
        grid=(m // DT + 1,),
        in_specs=[pl.BlockSpec((DT, k), tile),
                  wspec(0), wspec(1), wspec(2),
                  pl.BlockSpec((N_GATE, k), lambda s: ((t0 + 3) * (IN_TN // N_GATE), 0), pipeline_mode=once),
                  pl.BlockSpec((1, W_D), const2),
                  pl.BlockSpec((1, W_D), const2),
                  pl.BlockSpec((G_D, CHUNK, CHUNK), lambda s: (0, 0, 0)),
                  pl.BlockSpec((CHUNK, G_D), const2)],
        out_specs=pl.BlockSpec((DT, W_D), tile),
        out_shape=jax.ShapeDtypeStruct((m, W_D), BF16),
        scratch_shapes=[pltpu.VMEM((3 * W_D, k), BF16),
                        pltpu.VMEM((DT, 3 * W_D), F32),
                        pltpu.VMEM((G_D, CHUNK, CHUNK), BF16)],
        compiler_params=_params(("arbitrary",)),
        name="odd_mlp_fused",
    )(hp, w_t, w_t, w_t, w_t, lng, lnb, ws, bst)


def _in_proj_rows_kernel(h_ref, w_ref, wn_ref, o_ref, *, shift):
    wsh = jnp.concatenate([w_ref[shift:, :], wn_ref[...]], axis=0)
    o_ref[...] = _dot_nt(h_ref[...], wsh.astype(BF16))


def _in_proj_rows(h, w_t, n_out, shift, tile0):
    ms, k = h.shape
    return pl.pallas_call(
        functools.partial(_in_proj_rows_kernel, shift=shift),
        grid=(n_out // IN_TN,),
        in_specs=[pl.BlockSpec((ms, k), lambda j: (0, 0)),
                  pl.BlockSpec((IN_TN, k), lambda j: (j + tile0, 0)),
                  pl.BlockSpec((shift, k), lambda j: ((j + tile0 + 1) * (IN_TN // shift), 0))],
        out_specs=pl.BlockSpec((ms, IN_TN), lambda j: (0, j)),
        out_shape=jax.ShapeDtypeStruct((ms, n_out), F32),
        compiler_params=_params(("arbitrary",)),
        name="in_proj_rows",
    )(h, w_t, w_t)


def _odd_sample_kernel(pq_ref, pk_ref, pv_ref, pz_ref, h_ref, wg_ref, pd_ref,
                       c_ref, nrow_ref, mrow_ref, bg_ref, gm_ref, lng_ref, lnb_ref,
                       rtab_ref, btab_ref,
                       yc_ref, yd_ref, vn_ref, co_ref, no_ref, mo_ref,
                       inter_scr):
    h = pl.program_id(1)
    row = lax.broadcasted_iota(jnp.int32, (SR, SR), 0)
    col = lax.broadcasted_iota(jnp.int32, (SR, SR), 1)
    trow = row & 3

    @pl.when(h == 0)
    def _():
        dv = pd_ref[:, W_D:2 * W_D]
        mu = jnp.mean(dv, axis=-1, keepdims=True)
        xc = dv - mu
        var = jnp.mean(xc * xc, axis=-1, keepdims=True)
        rstd = lax.rsqrt(var + EPS)
        for g in range(G_D):
            sl = slice(g * 128, (g + 1) * 128)
            vn = xc[:, sl] * rstd * lng_ref[0:1, sl] + lnb_ref[0:1, sl]
            vn_ref[:, sl] = vn
            s = rtab_ref[0, :, sl] * vn + btab_ref[:, sl]
            for j in range(1, DEC_SEQ):
                s = s + jnp.where(trow >= j, rtab_ref[j, :, sl] * pltpu.roll(vn, j, 0), 0.0)
            d_u = pd_ref[:, g * 128:(g + 1) * 128]
            d_z = pd_ref[:, 2 * W_D + g * 128:2 * W_D + (g + 1) * 128]
            yd_ref[:, sl] = (d_u * s * _silu(d_z)).astype(BF16)

    same = (row >> 2) == (col >> 2)
    mask = jnp.where(same, trow - (col & 3), -1) >= 0
    pre = _dot_nt(h_ref[...], wg_ref[...].astype(BF16)) + bg_ref[...]
    lf = _log_sigmoid(pre)
    b_full = _dot_hi(jnp.where(mask, 1.0, 0.0), lf)
    sel_i = col == h
    sel_f = col == h + H_C
    ig_c = jnp.sum(jnp.where(sel_i, pre, 0.0), axis=-1, keepdims=True)
    b_c = jnp.sum(jnp.where(sel_f, b_full, 0.0), axis=-1, keepdims=True)
    sel_ir = row == h
    sel_fr = row == h + H_C
    ig_r = jnp.sum(jnp.where(sel_ir, pre.T, 0.0), axis=0, keepdims=True)
    b_r = jnp.sum(jnp.where(sel_fr, b_full.T, 0.0), axis=0, keepdims=True)
    m_prev = mrow_ref[0]
    log_d = jnp.where(mask, b_c - b_r + ig_r, NEG_INF)
    log_inter = b_c + m_prev
    m_t = jnp.maximum(log_inter, jnp.max(log_d, axis=-1, keepdims=True))
    w = jnp.exp(log_d - m_t)
    w_inter = jnp.exp(log_inter - m_t)
    q = pq_ref[...] * (DQK_C ** -0.5)
    k = pk_ref[...]
    v = pv_ref[...]
    qb = q.astype(BF16)
    kb = k.astype(BF16)
    vb = v.astype(BF16)
    sc = _dot_nt(qb, kb) * w
    sub = lax.broadcasted_iota(jnp.int32, (8, DV_C), 0)
    sub8 = lax.broadcasted_iota(jnp.int32, (8, DQK_C), 0)
    for g in range(SR // 8):
        q8 = q[8 * g:8 * g + 8, :]
        q2 = jnp.concatenate([jnp.where(sub8 < DEC_SEQ, q8, 0.0), jnp.where(sub8 < DEC_SEQ, 0.0, q8)], axis=1)
        c_pair = jnp.concatenate([c_ref[2 * g + beta, 0].astype(BF16) for beta in range(2)], axis=1)
        inter_scr[8 * g:8 * g + 8, :] = _dot_nt(q2.astype(BF16), c_pair)
    n_rows = nrow_ref[0]
    num = _dot(sc.astype(BF16), vb) + w_inter * inter_scr[...]
    den = jnp.sum(sc, axis=-1, keepdims=True) + w_inter * jnp.sum(q * n_rows, axis=-1, keepdims=True)
    hh = num / jnp.maximum(jnp.abs(den), jnp.exp(-m_t))
    yc_ref[...] = (_head_norm(hh, gm_ref[...]) * _silu(pz_ref[...])).astype(BF16)

    stats = jnp.where(col == 0, m_t, jnp.where(col == 1, b_c, 0.0))
    last = _dot_hi(jnp.where(col == (row | 3), 1.0, 0.0), stats)
    m_new = last[:, 0:1]
    b_last = last[:, 1:2]
    w_end = jnp.exp(b_last - b_c + ig_c - m_new)
    cd = jnp.exp(b_last + m_prev - m_new)
    mo_ref[0] = m_new
    no_ref[0] = cd * n_rows + _dot_hi(jnp.where(same, 1.0, 0.0), w_end * k)
    vwt = (v * w_end).T
    lane_b = lax.broadcasted_iota(jnp.int32, (DV_C, SR), 1) >> 2
    for b in range(SB):
        lhs = jnp.where(lane_b == b, vwt, 0.0).astype(BF16)
        cd_b = cd[4 * b + 3:4 * b + 4, :]
        co_ref[b, 0] = cd_b * c_ref[b, 0] + _dot(lhs, kb)


def _odd_sample(pc, pd, h, w_o, c_state, n_rows, m_rows, bg, gm, lng, lnb, rtab, btab):
    nb = DEC_BATCH // SB
    const2 = lambda i, h: (0, 0)
    return pl.pallas_call(
        _odd_sample_kernel,
        grid=(nb, H_C),
        in_specs=[pl.BlockSpec((SR, DQK_C), lambda i, h: (i, h)),
                  pl.BlockSpec((SR, DQK_C), lambda i, h: (i, h)),
                  pl.BlockSpec((SR, DV_C), lambda i, h: (i, h)),
                  pl.BlockSpec((SR, DV_C), lambda i, h: (i, h)),
                  pl.BlockSpec((SR, D_MODEL), lambda i, h: (i, 0)),
                  pl.BlockSpec((128, D_MODEL), lambda i, h: (O_GATE // 128, 0)),
                  pl.BlockSpec((SR, 3 * W_D), lambda i, h: (i, 0)),
                  pl.BlockSpec((SB, 1, DV_C, DQK_C), lambda i, h: (i, h, 0, 0)),
                  pl.BlockSpec((1, SR, DQK_C), lambda i, h: (h, i, 0)),
                  pl.BlockSpec((1, SR, 1), lambda i, h: (h, i, 0)),
                  pl.BlockSpec((1, 128), const2),
                  pl.BlockSpec((1, DV_C), lambda i, h: (0, h)),
                  pl.BlockSpec((1, W_D), const2),
                  pl.BlockSpec((1, W_D), const2),
                  pl.BlockSpec((DEC_SEQ, SR, W_D), lambda i, h: (0, 0, 0)),
                  pl.BlockSpec((SR, W_D), const2)],
        out_specs=(pl.BlockSpec((SR, DV_C), lambda i, h: (i, h)),
                   pl.BlockSpec((SR, W_D), lambda i, h: (i, 0)),
                   pl.BlockSpec((SR, W_D), lambda i, h: (i, 0)),
                   pl.BlockSpec((SB, 1, DV_C, DQK_C), lambda i, h: (i, h, 0, 0)),
                   pl.BlockSpec((1, SR, DQK_C), lambda i, h: (h, i, 0)),
                   pl.BlockSpec((1, SR, 1), lambda i, h: (h, i, 0))),
        out_shape=(jax.ShapeDtypeStruct((DEC_BATCH * DEC_SEQ, W_C), BF16),
                   jax.ShapeDtypeStruct((DEC_BATCH * DEC_SEQ, W_D), BF16),
                   jax.ShapeDtypeStruct((DEC_BATCH * DEC_SEQ, W_D), F32),
                   jax.ShapeDtypeStruct((DEC_BATCH, H_C, DV_C, DQK_C), F32),
                   jax.ShapeDtypeStruct((H_C, DEC_BATCH * DEC_SEQ, DQK_C), F32),
                   jax.ShapeDtypeStruct((H_C, DEC_BATCH * DEC_SEQ, 1), F32)),
        scratch_shapes=[pltpu.VMEM((SR, DV_C), F32)],
        compiler_params=_params(("arbitrary", "arbitrary")),
        name="odd_sample",
    )(*pc, h, w_o, pd, c_state, n_rows, m_rows, bg, gm, lng, lnb, rtab, btab)


def _rope_tables(pos):
    inv = ROPE_BASE ** (-jnp.arange(0, DH_B, 2, dtype=F32) / DH_B)
    ang = pos.astype(F32)[:, None] * inv[None, :]
    cos = jnp.cos(ang)
    sin = jnp.sin(ang)
    return jnp.concatenate([cos, cos], axis=-1), jnp.concatenate([-sin, sin], axis=-1)


def kernel(x_prompt, x_sample, state_conv, state_ret, state_mlstm_C, state_mlstm_n, state_mlstm_m,
           norm_even, w_in_even, conv_w, ret_norm, w_out_even,
           norm_odd, w_in_odd, b_gate_odd, mlstm_norm, ln_v_g, ln_v_b,
           w_spatial, b_spatial, w_out_odd, norm_final):
    w_in_e = w_in_even[0]
    w_o = w_in_odd[0].T
    g_even = norm_even[0][None, :]
    g_odd = norm_odd[0][None, :]
    g_fin = norm_final[None, :]
    cw = conv_w[0]
    g_ret = ret_norm[0][None, :]
    bg = jnp.concatenate([b_gate_odd[0], jnp.zeros((128 - 2 * H_C,), F32)])[None, :]
    gm = mlstm_norm[0][None, :]
    lng = ln_v_g[0][None, :]
    lnb = ln_v_b[0][None, :]
    ws = w_spatial[0]
    bst = b_spatial[0].T

    cos_p, sin_p = _rope_tables(jnp.arange(SEQ, dtype=jnp.int32))
    cos_s, sin_s = _rope_tables(PAST_LEN + jnp.arange(DEC_SEQ, dtype=jnp.int32))
    cos_s = jnp.tile(cos_s, (SB, 1))
    sin_s = jnp.tile(sin_s, (SB, 1))
    lg_tab = jnp.broadcast_to(jnp.asarray(LOG_GAMMA, F32)[:, None, None], (H_B, 1, 128))

    ws4 = ws[:, :DEC_SEQ, :DEC_SEQ]
    t_idx = jnp.arange(DEC_SEQ)
    rtab = []
    for j in range(DEC_SEQ):
        coef = ws4[:, t_idx, (t_idx - j) % DEC_SEQ]
        tab = jnp.repeat(coef.T[:, :, None], 128, axis=2).reshape(DEC_SEQ, W_D)
        rtab.append(jnp.tile(tab, (SB, 1)))
    rtab = jnp.stack(rtab)
    btab = jnp.tile(jnp.repeat(b_spatial[0][:, :DEC_SEQ].T[:, :, None], 128, axis=2)
                    .reshape(DEC_SEQ, W_D), (SB, 1))

    rs = DEC_BATCH * DEC_SEQ
    xp = x_prompt.reshape(BATCH * SEQ, D_MODEL)
    xs = x_sample.reshape(rs, D_MODEL)
    hp = _norm_cast(xp, g_even, 512)
    hs = _norm_cast(xs, g_even, 512)
    ya, conv_p, *ps_a, w_out_e = _even_conv(hp, hs, w_in_e, cw, w_out_even[0])
    yb, ret_p, *ps_b, w_out_o = _even_heads(hp, hs, w_in_e, g_ret, cos_p, sin_p, lg_tab, w_out_odd[0])
    st_exp = jnp.pad(state_conv[0], ((0, 0), (0, DEC_SEQ - (CONV_W - 1)), (0, 0))).reshape(rs, W_A)
    ya_s, u_s, yb_s, ret_s = _even_sample(ps_a, ps_b, st_exp, state_ret[0], cw, g_ret, cos_s, sin_s, lg_tab)
    x1, h1, x1s, h1s = _outproj(ya, yb, xp, ya_s, yb_s, xs, w_out_e, g_odd, final=False)

    yc, c_p, n_p, m_p, *ps_c = _odd_heads(h1, h1s, w_o, bg, gm)
    yd = _odd_mlp_fused(h1, w_o, lng, lnb, ws, bst)
    pd_s = _in_proj_rows(h1s, w_o, 3 * W_D, N_GATE, O_GATE // IN_TN)
    n_rows = jnp.repeat(jnp.transpose(state_mlstm_n[0], (1, 0, 2)), DEC_SEQ, axis=1)
    m_rows = jnp.repeat(state_mlstm_m[0].T, DEC_SEQ, axis=1)[:, :, None]
    yc_s, yd_s, vn_s, c_s, no_s, mo_s = _odd_sample(
        ps_c, pd_s, h1s, w_o, state_mlstm_C[0], n_rows, m_rows, bg, gm, lng, lnb, rtab, btab)
    y_prompt, y_sample = _outproj(yc, yd, x1, yc_s, yd_s, x1s, w_out_o, g_fin, final=True)

    conv_s = u_s.reshape(DEC_BATCH, DEC_SEQ, W_A)[:, DEC_SEQ - (CONV_W - 1):, :]
    n_s = jnp.transpose(no_s[:, DEC_SEQ - 1::DEC_SEQ, :], (1, 0, 2))
    m_s = mo_s[:, DEC_SEQ - 1::DEC_SEQ, 0].T
    return (y_prompt.reshape(BATCH, SEQ, D_MODEL),
            y_sample.reshape(DEC_BATCH, DEC_SEQ, D_MODEL),
            conv_p[None], conv_s[None],
            ret_p[None], ret_s[None],
            c_p[None], c_s[None],
            n_p[:, :, 0, :][None], n_s[None],
            m_p[:, :, 0, 0][None], m_s[None],
            vn_s.reshape(DEC_BATCH, DEC_SEQ, W_D)[None])
```

```python
import functools
import math

import jax
import jax.numpy as jnp
from jax import lax
from jax.experimental import pallas as pl
from jax.experimental.pallas import tpu as pltpu

F32 = jnp.float32
BF16 = jnp.bfloat16

D_MODEL = 2048
BATCH = 4
SEQ = 2048
DEC_BATCH = 128
DEC_SEQ = 4
PAST_LEN = 16384
W_A = 1024
CONV_W = 3
W_B = 1024
H_B = 8
DH_B = 128
E_IN = 8192
W_C = 1024
H_C = 4
DV_C = 256
DQK_C = 128
W_D = 1024
G_D = 8
CHUNK = 128
O_GATE = 2 * H_C * DQK_C + 2 * W_C
N_GATE = 2 * H_C
O_N = O_GATE + 3 * W_D
ROPE_BASE = 10000.0
EPS = 1e-6
LOG_GAMMA = tuple(math.log(1.0 - 2.0 ** (-5.0 - h)) for h in range(H_B))
NEG_INF = float("-inf")
VMEM_LIMIT = 56 * 1024 * 1024

NT_DIMS = (((1,), (1,)), ((), ()))
TN_DIMS = (((0,), (0,)), ((), ()))


def _silu(z):
    return z * (1.0 / (1.0 + jnp.exp(-z)))


def _log_sigmoid(x):
    return jnp.minimum(x, 0.0) - jnp.log1p(jnp.exp(-jnp.abs(x)))


def _dot(a, b):
    return jnp.dot(a, b, preferred_element_type=F32)


def _dot_nt(a, b):
    return lax.dot_general(a, b, NT_DIMS, preferred_element_type=F32)


def _dot_tn(a, b):
    return lax.dot_general(a, b, TN_DIMS, preferred_element_type=F32)


def _dot_hi(a, b):
    return jnp.dot(a, b, preferred_element_type=F32, precision=lax.Precision.HIGHEST)


def _head_norm(o, g):
    mu = jnp.mean(o, axis=-1, keepdims=True)
    oc = o - mu
    var = jnp.mean(oc * oc, axis=-1, keepdims=True)
    return oc * lax.rsqrt(var + EPS) * g


def _params(sem):
    return pltpu.CompilerParams(dimension_semantics=sem, vmem_limit_bytes=VMEM_LIMIT)


def _norm_cast_kernel(x_ref, g_ref, h_ref):
    x = x_ref[...]
    ms = jnp.mean(x * x, axis=-1, keepdims=True)
    h_ref[...] = (x * lax.rsqrt(ms + EPS) * g_ref[...]).astype(BF16)


def _norm_cast(x, g, tm):
    m, d = x.shape
    return pl.pallas_call(
        _norm_cast_kernel,
        grid=(m // tm,),
        in_specs=[pl.BlockSpec((tm, d), lambda i: (i, 0)),
                  pl.BlockSpec((1, d), lambda i: (0, 0))],
        out_specs=pl.BlockSpec((tm, d), lambda i: (i, 0)),
        out_shape=jax.ShapeDtypeStruct((m, d), BF16),
        compiler_params=_params(("arbitrary",)),
        name="norm_cast",
    )(x, g)


IN_TM = 1024
IN_TN = 1024


def _in_proj_kernel(*refs, shift_from, shift):
    if shift:
        hp_ref, hs_ref, w_ref, wn_ref, op_ref, os_ref, wb = refs
    else:
        hp_ref, hs_ref, w_ref, op_ref, os_ref, wb = refs
    j = pl.program_id(0)
    i = pl.program_id(1)

    if shift:
        @pl.when(jnp.logical_and(i == 0, j < shift_from))
        def _():
            wb[...] = w_ref[...].astype(BF16)

        @pl.when(jnp.logical_and(i == 0, j >= shift_from))
        def _():
            wb[...] = jnp.concatenate([w_ref[shift:IN_TN, :], wn_ref[...]], axis=0).astype(BF16)

        mm = _dot_nt
    else:
        @pl.when(i == 0)
        def _():
            wb[...] = w_ref[...].astype(BF16)

        mm = _dot

    @pl.when(i == 0)
    def _():
        os_ref[...] = mm(hs_ref[...], wb[...])

    @pl.when(i > 0)
    def _():
        op_ref[...] = mm(hp_ref[...], wb[...])


def _in_proj(hp, hs, w, n_out, shift_from=0, shift=0, tile0=0):
    mp, k = hp.shape
    ms = hs.shape[0]
    n_prompt = mp // IN_TM
    prow = lambda j, i: (jnp.maximum(i - 1, 0), 0)
    in_specs = [pl.BlockSpec((IN_TM, k), prow),
                pl.BlockSpec((ms, k), lambda j, i: (0, 0))]
    args = [hp, hs, w]
    if shift:
        in_specs.append(pl.BlockSpec((IN_TN, k), lambda j, i: (j + tile0, 0)))
        in_specs.append(pl.BlockSpec((shift, k), lambda j, i: ((j + tile0 + 1) * (IN_TN // shift), 0)))
        args.append(w)
        wb_shape = (IN_TN, k)
    else:
        in_specs.append(pl.BlockSpec((k, IN_TN), lambda j, i: (0, j)))
        wb_shape = (k, IN_TN)
    return pl.pallas_call(
        functools.partial(_in_proj_kernel, shift_from=shift_from, shift=shift),
        grid=(n_out // IN_TN, n_prompt + 1),
        in_specs=in_specs,
        out_specs=(pl.BlockSpec((IN_TM, IN_TN), lambda j, i: (jnp.maximum(i - 1, 0), j)),
                   pl.BlockSpec((ms, IN_TN), lambda j, i: (0, j))),
        out_shape=(jax.ShapeDtypeStruct((mp, n_out), F32),
                   jax.ShapeDtypeStruct((ms, n_out), F32)),
        scratch_shapes=[pltpu.VMEM(wb_shape, BF16)],
        compiler_params=_params(("arbitrary", "arbitrary")),
        name="in_proj",
    )(*args)


OUT_TM = 512


def _outproj_kernel(ya_ref, yb_ref, x_ref, yas_ref, ybs_ref, xs_ref, w_ref, g_ref, *out_refs,
                    final, n_prompt):
    i = pl.program_id(0)
    half = ya_ref.shape[1]
    n_out = 1 if final else 2

    def tile(ya, yb, x, outs):
        acc = _dot(ya[...], w_ref[0:half, :]) + _dot(yb[...], w_ref[half:2 * half, :])
        x1 = x[...] + acc
        ms = jnp.mean(x1 * x1, axis=-1, keepdims=True)
        hn = x1 * lax.rsqrt(ms + EPS) * g_ref[...]
        if final:
            outs[0][...] = hn
        else:
            outs[0][...] = x1
            outs[1][...] = hn.astype(BF16)

    @pl.when(i < n_prompt)
    def _():
        tile(ya_ref, yb_ref, x_ref, out_refs[:n_out])

    @pl.when(i == n_prompt)
    def _():
        tile(yas_ref, ybs_ref, xs_ref, out_refs[n_out:])


def _outproj(ya, yb, x, ya_s, yb_s, x_s, w, g, final):
    m, half = ya.shape
    ms = ya_s.shape[0]
    d = w.shape[1]
    n_prompt = m // OUT_TM
    row = lambda i: (jnp.minimum(i, n_prompt - 1), 0)
    const = lambda i: (0, 0)
    once = pl.Buffered(1)
    shapes = [jax.ShapeDtypeStruct((m, d), F32), jax.ShapeDtypeStruct((ms, d), F32)]
    specs = [pl.BlockSpec((OUT_TM, d), row), pl.BlockSpec((ms, d), const)]
    if not final:
        shapes = [shapes[0], jax.ShapeDtypeStruct((m, d), BF16), shapes[1], jax.ShapeDtypeStruct((ms, d), BF16)]
        specs = [specs[0], pl.BlockSpec((OUT_TM, d), row), specs[1], pl.BlockSpec((ms, d), const)]
    return pl.pallas_call(
        functools.partial(_outproj_kernel, final=final, n_prompt=n_prompt),
        grid=(n_prompt + 1,),
        in_specs=[pl.BlockSpec((OUT_TM, half), row),
                  pl.BlockSpec((OUT_TM, half), row),
                  pl.BlockSpec((OUT_TM, d), row),
                  pl.BlockSpec((ms, half), const, pipeline_mode=once),
                  pl.BlockSpec((ms, half), const, pipeline_mode=once),
                  pl.BlockSpec((ms, d), const, pipeline_mode=once),
                  pl.BlockSpec((2 * half, d), const, pipeline_mode=once),
                  pl.BlockSpec((1, d), const)],
        out_specs=tuple(specs),
        out_shape=tuple(shapes),
        compiler_params=_params(("arbitrary",)),
        name="out_proj_final" if final else "out_proj",
    )(ya, yb, x, ya_s, yb_s, x_s, w, g)


MIX_ROWS = 512


def _rope(x, cosf, sins):
    return x * cosf + pltpu.roll(x, DH_B // 2, 1) * sins


def _even_prompt_kernel(p_ref, cw_ref, gret_ref, cos_ref, sin_ref,
                        ya_ref, yb_ref, conv_ref, s_ref, ubuf):
    L = CHUNK

    @pl.when(pl.program_id(1) == 0)
    def _():
        ubuf[0:8, :] = jnp.zeros((8, W_A), F32)
        s_ref[...] = jnp.zeros_like(s_ref)

    def chunk(ci, carry):
        rs = pl.ds(pl.multiple_of(ci * L, L), L)

        for j in range(W_A // 128):
            sl = slice(j * 128, (j + 1) * 128)
            a_b = p_ref[rs, j * 128:(j + 1) * 128]
            a_c = p_ref[rs, W_A + j * 128:W_A + (j + 1) * 128]
            a_x = p_ref[rs, 2 * W_A + j * 128:2 * W_A + (j + 1) * 128]
            a_z = p_ref[rs, 3 * W_A + j * 128:3 * W_A + (j + 1) * 128]
            u = a_c * a_x
            ubuf[8:8 + L, sl] = u
            t0 = ubuf[6:6 + L, sl]
            t1 = ubuf[7:7 + L, sl]
            conv = cw_ref[0:1, sl] * t0 + cw_ref[1:2, sl] * t1 + cw_ref[2:3, sl] * u
            ya_ref[rs, sl] = (a_b * conv * _silu(a_z)).astype(BF16)
            ubuf[0:8, sl] = u[L - 8:L, :]

        cosf = cos_ref[rs, :]
        sins = sin_ref[rs, :]
        row = lax.broadcasted_iota(jnp.int32, (L, L), 0)
        col = lax.broadcasted_iota(jnp.int32, (L, L), 1)
        causal = row >= col
        diff = jnp.maximum(row - col, 0).astype(F32)
        ti = lax.broadcasted_iota(jnp.int32, (L, 1), 0).astype(F32)
        base = 4 * W_A
        for h in range(H_B):
            lg = LOG_GAMMA[h]
            sl = slice(h * DH_B, (h + 1) * DH_B)
            q = p_ref[rs, base + h * DH_B:base + (h + 1) * DH_B]
            k = p_ref[rs, base + W_B + h * DH_B:base + W_B + (h + 1) * DH_B]
            v = p_ref[rs, base + 2 * W_B + h * DH_B:base + 2 * W_B + (h + 1) * DH_B]
            z = p_ref[rs, base + 3 * W_B + h * DH_B:base + 3 * W_B + (h + 1) * DH_B]
            qr = _rope(q, cosf, sins)
            kr = _rope(k, cosf, sins) * (DH_B ** -0.5)
            decay = jnp.where(causal, jnp.exp(lg * diff), 0.0)
            qb = qr.astype(BF16)
            kb = kr.astype(BF16)
            vb = v.astype(BF16)
            sc = _dot_nt(qb, kb) * decay
            inner = _dot(sc.astype(BF16), vb)
            s_old = s_ref[0, h]
            cross = _dot(qb, s_old.astype(BF16)) * jnp.exp(lg * (ti + 1.0))
            kd = (kr * jnp.exp(lg * (L - 1.0 - ti))).astype(BF16)
            s_ref[0, h] = math.exp(lg * L) * s_old + _dot_tn(kd, vb)
            o = inner + cross
            yb_ref[rs, sl] = (_head_norm(o, gret_ref[0:1, sl]) * _silu(z)).astype(BF16)
        return carry

    lax.fori_loop(0, MIX_ROWS // L, chunk, 0)
    conv_ref[0] = ubuf[6:8, :]


def _even_prompt(p, conv_w, g_ret, cosf, sins):
    nc = SEQ // MIX_ROWS
    rows = lambda b, c: (b * nc + c, 0)
    const2 = lambda b, c: (0, 0)
    return pl.pallas_call(
        _even_prompt_kernel,
        grid=(BATCH, nc),
        in_specs=[pl.BlockSpec((MIX_ROWS, E_IN), rows),
                  pl.BlockSpec((CONV_W, W_A), const2),
                  pl.BlockSpec((1, W_B), const2),
                  pl.BlockSpec((MIX_ROWS, DH_B), lambda b, c: (c, 0)),
                  pl.BlockSpec((MIX_ROWS, DH_B), lambda b, c: (c, 0))],
        out_specs=(pl.BlockSpec((MIX_ROWS, W_A), rows),
                   pl.BlockSpec((MIX_ROWS, W_B), rows),
                   pl.BlockSpec((1, CONV_W - 1, W_A), lambda b, c: (b, 0, 0)),
                   pl.BlockSpec((1, H_B, DH_B, DH_B), lambda b, c: (b, 0, 0, 0))),
        out_shape=(jax.ShapeDtypeStruct((BATCH * SEQ, W_A), BF16),
                   jax.ShapeDtypeStruct((BATCH * SEQ, W_B), BF16),
                   jax.ShapeDtypeStruct((BATCH, CONV_W - 1, W_A), F32),
                   jax.ShapeDtypeStruct((BATCH, H_B, DH_B, DH_B), F32)),
        scratch_shapes=[pltpu.VMEM((CHUNK + 8, W_A), F32)],
        compiler_params=_params(("arbitrary", "arbitrary")),
        name="even_prompt",
    )(p, conv_w, g_ret, cosf, sins)


FT = 1024
FN = BATCH * SEQ // FT
FPB = SEQ // FT
HG = 2
GW = HG * 128
PCH = 2
CONV_PCH = 4


def _chunk_pipeline(n, piece, stages, gate=None):
    sa, sb, sc, sd, se = stages
    npieces = n // PCH
    for j in range(PCH):
        piece(0, j)
    if gate is not None:
        gate(0)
    for c in range(n + 2):
        k, j = c // PCH + 1, c % PCH
        if k < npieces:
            piece(k, j)
        if c < n:
            sa(c)
        if 1 <= c <= n:
            sc(c - 1)
        if c < n:
            sb(c)
        if 1 <= c <= n:
            sd(c - 1)
        if c >= 2:
            se(c - 2)
        if gate is not None and j == PCH - 1 and k < npieces:
            gate(k)


def _even_heads_kernel(hp_ref, hs_ref, wq_ref, wk_ref, wv_ref, wz_ref, gret_ref, cos_ref, sin_ref, lg_ref,
                       yb_ref, s_ref, sq_ref, sk_ref, sv_ref, sz_ref, wb, pt, s_scr):
    s = pl.program_id(1)
    L = CHUNK

    @pl.when(s == 0)
    def _():
        for part, w_ref in enumerate((wq_ref, wk_ref, wv_ref, wz_ref)):
            wb[:, part * GW:(part + 1) * GW] = w_ref[...].astype(BF16)
        ps = _dot(hs_ref[...], wb[...])
        for part, o_ref in enumerate((sq_ref, sk_ref, sv_ref, sz_ref)):
            o_ref[...] = ps[:, part * GW:(part + 1) * GW]

    @pl.when(s > 0)
    def _():
        t = s - 1
        n = FT // L
        rows = lambda c: slice(c * L, (c + 1) * L)
        cols = lambda part, i: slice(part * GW + i * DH_B, part * GW + (i + 1) * DH_B)

        def piece(k, j):
            pr = slice(k * PCH * L, (k + 1) * PCH * L)
            pc = slice(j * 2 * GW, (j + 1) * 2 * GW)
            pt[pr, pc] = _dot(hp_ref[pr, :], wb[:, pc])

        row = lax.broadcasted_iota(jnp.int32, (L, L), 0)
        col = lax.broadcasted_iota(jnp.int32, (L, L), 1)
        causal = row >= col
        diff = jnp.maximum(row - col, 0).astype(F32)
        ti = lax.broadcasted_iota(jnp.int32, (L, 1), 0).astype(F32)
        lgs = [lg_ref[i][:, 0:1] for i in range(HG)]
        decay = [jnp.where(causal, jnp.exp(lg * diff), 0.0) for lg in lgs]
        q_decay = [jnp.exp(lg * (ti + 1.0)) for lg in lgs]
        k_decay = [jnp.exp(lg * (L - 1.0 - ti)) for lg in lgs]
        gamma_l = [jnp.exp(lg * float(L)) for lg in lgs]
        state = {0: [jnp.where(t % FPB == 0, 0.0, s_scr[i]) for i in range(HG)]}
        v = {}

        def stage_a(c):
            cosf = cos_ref[rows(c), :]
            sins = sin_ref[rows(c), :]
            v[c] = []
            for i in range(HG):
                kr = _rope(pt[rows(c), cols(1, i)], cosf, sins) * (DH_B ** -0.5)
                v[c].append(dict(qb=_rope(pt[rows(c), cols(0, i)], cosf, sins).astype(BF16),
                                 kb=kr.astype(BF16),
                                 kd=(kr * k_decay[i]).astype(BF16),
                                 vb=pt[rows(c), cols(2, i)].astype(BF16)))

        def stage_b(c):
            for i, d in enumerate(v[c]):
                d["sc"] = _dot_nt(d["qb"], d["kb"])
                d["upd"] = _dot_tn(d["kd"], d["vb"])
            for i, d in enumerate(v[c]):
                d["cross"] = _dot(d["qb"], state[c][i].astype(BF16))

        def stage_c(c):
            state[c + 1] = []
            for i, d in enumerate(v[c]):
                d["sc"] = (d["sc"] * decay[i]).astype(BF16)
                state[c + 1].append(gamma_l[i] * state[c][i] + d["upd"])

        def stage_d(c):
            for d in v[c]:
                d["inner"] = _dot(d["sc"], d["vb"])

        def stage_e(c):
            for i, d in enumerate(v.pop(c)):
                o = d["inner"] + d["cross"] * q_decay[i]
                g = gret_ref[0:1, i * DH_B:(i + 1) * DH_B]
                z = pt[rows(c), cols(3, i)]
                yb_ref[rows(c), i * DH_B:(i + 1) * DH_B] = (_head_norm(o, g) * _silu(z)).astype(BF16)

        _chunk_pipeline(n, piece, (stage_a, stage_b, stage_c, stage_d, stage_e))
        for i in range(HG):
            s_scr[i] = state[n][i]
            s_ref[0, i] = state[n][i]


def _even_heads(hp, hs, w, g_ret, cosf, sins, lg_tab):
    k = hp.shape[1]
    ms = hs.shape[0]
    ng = H_B // HG
    base = 4 * W_A // GW
    tile = lambda s: jnp.maximum(s - 1, 0)
    wspec = lambda part: pl.BlockSpec((k, GW), lambda g, s: (0, base + part * ng + g))
    sspec = pl.BlockSpec((ms, GW), lambda g, s: (0, g))
    sshape = jax.ShapeDtypeStruct((ms, W_B), F32)
    return pl.pallas_call(
        _even_heads_kernel,
        grid=(ng, FN + 1),
        in_specs=[pl.BlockSpec((FT, k), lambda g, s: (tile(s), 0)),
                  pl.BlockSpec((ms, k), lambda g, s: (0, 0)),
                  wspec(0), wspec(1), wspec(2), wspec(3),
                  pl.BlockSpec((1, GW), lambda g, s: (0, g)),
                  pl.BlockSpec((FT, DH_B), lambda g, s: (tile(s) % FPB, 0)),
                  pl.BlockSpec((FT, DH_B), lambda g, s: (tile(s) % FPB, 0)),
                  pl.BlockSpec((HG, 1, 128), lambda g, s: (g, 0, 0))],
        out_specs=(pl.BlockSpec((FT, GW), lambda g, s: (tile(s), g)),
                   pl.BlockSpec((1, HG, DH_B, DH_B), lambda g, s: (tile(s) // FPB, g, 0, 0)),
                   sspec, sspec, sspec, sspec),
        out_shape=(jax.ShapeDtypeStruct((BATCH * SEQ, W_B), BF16),
                   jax.ShapeDtypeStruct((BATCH, H_B, DH_B, DH_B), F32),
                   sshape, sshape, sshape, sshape),
        scratch_shapes=[pltpu.VMEM((k, 4 * GW), BF16),
                        pltpu.VMEM((FT, 4 * GW), F32),
                        pltpu.VMEM((HG, DH_B, DH_B), F32)],
        compiler_params=_params(("arbitrary", "arbitrary")),
        name="even_heads",
    )(hp, hs, w, w, w, w, g_ret, cosf, sins, lg_tab)


def _even_conv_kernel(hp_ref, hs_ref, wb_ref, wc_ref, wx_ref, wz_ref, cw_ref,
                      ya_ref, conv_ref, sb_ref, sc_ref, sx_ref, sz_ref, wb, pt, ubuf):
    s = pl.program_id(1)
    L = CHUNK

    @pl.when(s == 0)
    def _():
        for part, w_ref in enumerate((wb_ref, wc_ref, wx_ref, wz_ref)):
            wb[:, part * GW:(part + 1) * GW] = w_ref[...].astype(BF16)
        ps = _dot(hs_ref[...], wb[...])
        for part, o_ref in enumerate((sb_ref, sc_ref, sx_ref, sz_ref)):
            o_ref[...] = ps[:, part * GW:(part + 1) * GW]

    @pl.when(s > 0)
    def _():
        t = s - 1
        n = FT // L
        rows = lambda c: slice(c * L, (c + 1) * L)
        part = lambda p, c: pt[rows(c), p * GW:(p + 1) * GW]

        @pl.when(t % FPB == 0)
        def _():
            ubuf[0:8, :] = jnp.zeros((8, GW), F32)

        @pl.when(t % FPB != 0)
        def _():
            ubuf[0:8, :] = ubuf[FT:FT + 8, :]

        def piece(k, j):
            pr = slice(k * CONV_PCH * L, (k + 1) * CONV_PCH * L)
            pc = slice(j * 2 * GW, (j + 1) * 2 * GW)
            pt[pr, pc] = _dot(hp_ref[pr, :], wb[:, pc])

        todo = [(k, j) for k in range(n // CONV_PCH) for j in range(2)]
        piece(*todo.pop(0))
        piece(*todo.pop(0))
        for c in range(n):
            if todo:
                piece(*todo.pop(0))
            u = part(1, c) * part(2, c)
            ubuf[8 + c * L:8 + (c + 1) * L, :] = u
            t0 = ubuf[6 + c * L:6 + (c + 1) * L, :]
            t1 = ubuf[7 + c * L:7 + (c + 1) * L, :]
            conv = cw_ref[0:1, :] * t0 + cw_ref[1:2, :] * t1 + cw_ref[2:3, :] * u
            ya_ref[rows(c), :] = (part(0, c) * conv * _silu(part(3, c))).astype(BF16)
        conv_ref[0] = ubuf[FT + 6:FT + 8, :]


def _even_conv(hp, hs, w, conv_w):
    k = hp.shape[1]
    ms = hs.shape[0]
    ng = W_A // GW
    tile = lambda s: jnp.maximum(s - 1, 0)
    wspec = lambda part: pl.BlockSpec((k, GW), lambda g, s: (0, part * ng + g))
    sspec = pl.BlockSpec((ms, GW), lambda g, s: (0, g))
    sshape = jax.ShapeDtypeStruct((ms, W_A), F32)
    return pl.pallas_call(
        _even_conv_kernel,
        grid=(ng, FN + 1),
        in_specs=[pl.BlockSpec((FT, k), lambda g, s: (tile(s), 0)),
                  pl.BlockSpec((ms, k), lambda g, s: (0, 0)),
                  wspec(0), wspec(1), wspec(2), wspec(3),
                  pl.BlockSpec((CONV_W, GW), lambda g, s: (0, g))],
        out_specs=(pl.BlockSpec((FT, GW), lambda g, s: (tile(s), g)),
                   pl.BlockSpec((1, CONV_W - 1, GW), lambda g, s: (tile(s) // FPB, 0, g)),
                   sspec, sspec, sspec, sspec),
        out_shape=(jax.ShapeDtypeStruct((BATCH * SEQ, W_A), BF16),
                   jax.ShapeDtypeStruct((BATCH, CONV_W - 1, W_A), F32),
                   sshape, sshape, sshape, sshape),
        scratch_shapes=[pltpu.VMEM((k, 4 * GW), BF16),
                        pltpu.VMEM((FT, 4 * GW), F32),
                        pltpu.VMEM((FT + 8, GW), F32)],
        compiler_params=_params(("arbitrary", "arbitrary")),
        name="even_conv",
    )(hp, hs, w, w, w, w, conv_w)


SB = 32
SR = SB * DEC_SEQ


def _even_sample_kernel(ab_ref, ac_ref, ax_ref, az_ref, pq_ref, pk_ref, pv_ref, pz_ref, st_ref, s_ref,
                        cw_ref, gret_ref, cos_ref, sin_ref, lg_ref,
                        ya_ref, u_ref, yb_ref, so_ref, cross_scr):
    h = pl.program_id(1)
    row = lax.broadcasted_iota(jnp.int32, (SR, SR), 0)
    col = lax.broadcasted_iota(jnp.int32, (SR, SR), 1)
    trow = row & 3

    @pl.when(h == 0)
    def _():
        for j in range(W_A // 128):
            sl = slice(j * 128, (j + 1) * 128)
            a_b = ab_ref[:, sl]
            a_c = ac_ref[:, sl]
            a_x = ax_ref[:, sl]
            a_z = az_ref[:, sl]
            u = a_c * a_x
            e = st_ref[:, sl]
            tap1 = jnp.where(trow >= 1, pltpu.roll(u, 1, 0), pltpu.roll(e, SR - 1, 0))
            tap0 = jnp.where(trow >= 2, pltpu.roll(u, 2, 0), e)
            conv = cw_ref[0:1, sl] * tap0 + cw_ref[1:2, sl] * tap1 + cw_ref[2:3, sl] * u
            ya_ref[:, sl] = (a_b * conv * _silu(a_z)).astype(BF16)
            u_ref[:, sl] = u

    lg = lg_ref[0][:, 0:1]
    same = (row >> 2) == (col >> 2)
    dd = trow - (col & 3)
    mask = jnp.where(same, dd, -1) >= 0
    decay = jnp.where(mask, jnp.exp(lg * jnp.maximum(dd, 0).astype(F32)), 0.0)
    tcol = (lax.broadcasted_iota(jnp.int32, (SR, 1), 0) & 3).astype(F32)
    cosf = cos_ref[...]
    sins = sin_ref[...]
    qr = _rope(pq_ref[...], cosf, sins)
    kr = _rope(pk_ref[...], cosf, sins) * (DH_B ** -0.5)
    qb = qr.astype(BF16)
    kb = kr.astype(BF16)
    vb = pv_ref[...].astype(BF16)
    sc = _dot_nt(qb, kb) * decay
    inner = _dot(sc.astype(BF16), vb)
    kdt = (kr * jnp.exp(lg * (DEC_SEQ - 1.0 - tcol))).T
    gamma_l = jnp.exp(lg * float(DEC_SEQ))
    lane_b = col >> 2
    sub = lax.broadcasted_iota(jnp.int32, (8, DH_B), 0)
    for g in range(SR // 8):
        q8 = qr[8 * g:8 * g + 8, :]
        q2 = jnp.concatenate([jnp.where(sub < DEC_SEQ, q8, 0.0), jnp.where(sub < DEC_SEQ, 0.0, q8)], axis=1)
        s_pair = [s_ref[2 * g + beta, 0] for beta in range(2)]
        cross_scr[8 * g:8 * g + 8, :] = _dot(
            q2.astype(BF16), jnp.concatenate([sp.astype(BF16) for sp in s_pair], axis=0))
        for beta in range(2):
            b = 2 * g + beta
            lhs = jnp.where(lane_b == b, kdt, 0.0).astype(BF16)
            so_ref[b, 0] = gamma_l * s_pair[beta] + _dot(lhs, vb)
    o = inner + cross_scr[...] * jnp.exp(lg * (tcol + 1.0))
    yb_ref[...] = (_head_norm(o, gret_ref[...]) * _silu(pz_ref[...])).astype(BF16)


def _even_sample(pa, pb, st_exp, s_state, conv_w, g_ret, cosf, sins, lg_tab):
    nb = DEC_BATCH // SB
    const2 = lambda i, h: (0, 0)
    aspec = pl.BlockSpec((SR, W_A), lambda i, h: (i, 0))
    hspec = pl.BlockSpec((SR, DH_B), lambda i, h: (i, h))
    return pl.pallas_call(
        _even_sample_kernel,
        grid=(nb, H_B),
        in_specs=[aspec, aspec, aspec, aspec,
                  hspec, hspec, hspec, hspec,
                  pl.BlockSpec((SR, W_A), lambda i, h: (i, 0)),
                  pl.BlockSpec((SB, 1, DH_B, DH_B), lambda i, h: (i, h, 0, 0)),
                  pl.BlockSpec((CONV_W, W_A), const2),
                  pl.BlockSpec((1, DH_B), lambda i, h: (0, h)),
                  pl.BlockSpec((SR, DH_B), const2),
                  pl.BlockSpec((SR, DH_B), const2),
                  pl.BlockSpec((1, 1, 128), lambda i, h: (h, 0, 0))],
        out_specs=(pl.BlockSpec((SR, W_A), lambda i, h: (i, 0)),
                   pl.BlockSpec((SR, W_A), lambda i, h: (i, 0)),
                   pl.BlockSpec((SR, DH_B), lambda i, h: (i, h)),
                   pl.BlockSpec((SB, 1, DH_B, DH_B), lambda i, h: (i, h, 0, 0))),
        out_shape=(jax.ShapeDtypeStruct((DEC_BATCH * DEC_SEQ, W_A), BF16),
                   jax.ShapeDtypeStruct((DEC_BATCH * DEC_SEQ, W_A), F32),
                   jax.ShapeDtypeStruct((DEC_BATCH * DEC_SEQ, W_B), BF16),
                   jax.ShapeDtypeStruct((DEC_BATCH, H_B, DH_B, DH_B), F32)),
        scratch_shapes=[pltpu.VMEM((SR, DH_B), F32)],
        compiler_params=_params(("arbitrary", "arbitrary")),
        name="even_sample",
    )(*pa, *pb, st_exp, s_state, conv_w, g_ret, cosf, sins, lg_tab)


def _odd_prompt_kernel(p_ref, h_ref, wg_ref, bg_ref, gm_ref, lng_ref, lnb_ref, ws_ref, bst_ref,
                       yc_ref, yd_ref, c_ref, n_ref, m_ref, wgb, wsb):
    L = CHUNK

    @pl.when(jnp.logical_and(pl.program_id(0) == 0, pl.program_id(1) == 0))
    def _():
        wgb[...] = wg_ref[...].astype(BF16)
        keep = (lax.broadcasted_iota(jnp.int32, (L, L), 0) >= lax.broadcasted_iota(jnp.int32, (L, L), 1))
        for g in range(G_D):
            wsb[g] = jnp.where(keep, ws_ref[g], 0.0).astype(BF16)

    @pl.when(pl.program_id(1) == 0)
    def _():
        c_ref[...] = jnp.zeros_like(c_ref)
        n_ref[...] = jnp.zeros_like(n_ref)
        m_ref[...] = jnp.zeros_like(m_ref)

    def chunk(ci, carry):
        rs = pl.ds(pl.multiple_of(ci * L, L), L)
        row = lax.broadcasted_iota(jnp.int32, (L, L), 0)
        col = lax.broadcasted_iota(jnp.int32, (L, L), 1)
        tri = row >= col

        pre = _dot_nt(h_ref[rs, :], wgb[...]) + bg_ref[...]
        lf = _log_sigmoid(pre)
        b_c = _dot_hi(jnp.where(tri, 1.0, 0.0), lf)
        b_r = b_c.T
        pre_r = pre.T
        for h in range(H_C):
            bc = b_c[:, H_C + h:H_C + h + 1]
            br = b_r[H_C + h:H_C + h + 1, :]
            igr = pre_r[h:h + 1, :]
            igc = pre[:, h:h + 1]
            m_prev = m_ref[0, h:h + 1, 0:1]
            log_d = jnp.where(tri, bc - br + igr, NEG_INF)
            log_inter = bc + m_prev
            m_t = jnp.maximum(log_inter, jnp.max(log_d, axis=-1, keepdims=True))
            w = jnp.exp(log_d - m_t)
            w_inter = jnp.exp(log_inter - m_t)
            q = p_ref[rs, h * DQK_C:(h + 1) * DQK_C] * (DQK_C ** -0.5)
            k = p_ref[rs, H_C * DQK_C + h * DQK_C:H_C * DQK_C + (h + 1) * DQK_C]
            v = p_ref[rs, 2 * H_C * DQK_C + h * DV_C:2 * H_C * DQK_C + (h + 1) * DV_C]
            z = p_ref[rs, O_GATE - W_C + h * DV_C:O_GATE - W_C + (h + 1) * DV_C]
            qb = q.astype(BF16)
            kb = k.astype(BF16)
            vb = v.astype(BF16)
            sc = _dot_nt(qb, kb) * w
            c_old = c_ref[0, h]
            n_old = n_ref[0, h:h + 1, :]
            num = _dot(sc.astype(BF16), vb) + w_inter * _dot_nt(qb, c_old.astype(BF16))
            den = jnp.sum(sc, axis=-1, keepdims=True) + w_inter * jnp.sum(q * n_old, axis=-1, keepdims=True)
            hh = num / jnp.maximum(jnp.abs(den), jnp.exp(-m_t))
            m_new = m_t[L - 1:L, :]
            b_last = bc[L - 1:L, :]
            w_end = jnp.exp(b_last - bc + igc - m_new)
            cd = jnp.exp(b_last + m_prev - m_new)
            c_ref[0, h] = cd * c_old + _dot_tn((v * w_end).astype(BF16), kb)
            n_ref[0, h:h + 1, :] = cd * n_old + jnp.sum(w_end * k, axis=0, keepdims=True)
            m_ref[0, h:h + 1, :] = jnp.broadcast_to(m_new, (1, 128))
            sl = slice(h * DV_C, (h + 1) * DV_C)
            yc_ref[rs, sl] = (_head_norm(hh, gm_ref[0:1, sl]) * _silu(z)).astype(BF16)

        dv = lambda g: p_ref[rs, O_GATE + W_D + g * 128:O_GATE + W_D + (g + 1) * 128]
        tot = dv(0)
        for g in range(1, G_D):
            tot = tot + dv(g)
        mu = jnp.sum(tot, axis=-1, keepdims=True) * (1.0 / W_D)
        sq = (dv(0) - mu) * (dv(0) - mu)
        for g in range(1, G_D):
            sq = sq + (dv(g) - mu) * (dv(g) - mu)
        rstd = lax.rsqrt(jnp.sum(sq, axis=-1, keepdims=True) * (1.0 / W_D) + EPS)
        for g in range(G_D):
            sl = slice(g * 128, (g + 1) * 128)
            vn = (dv(g) - mu) * rstd * lng_ref[0:1, sl] + lnb_ref[0:1, sl]
            s = _dot(wsb[g], vn.astype(BF16)) + bst_ref[:, g:g + 1]
            d_u = p_ref[rs, O_GATE + g * 128:O_GATE + (g + 1) * 128]
            d_z = p_ref[rs, O_GATE + 2 * W_D + g * 128:O_GATE + 2 * W_D + (g + 1) * 128]
            yd_ref[rs, sl] = (d_u * s * _silu(d_z)).astype(BF16)
        return carry

    lax.fori_loop(0, MIX_ROWS // L, chunk, 0)


def _odd_prompt(p, h, w_o, bg, gm, lng, lnb, ws, bst):
    nc = SEQ // MIX_ROWS
    rows = lambda b, c: (b * nc + c, 0)
    const2 = lambda b, c: (0, 0)
    return pl.pallas_call(
        _odd_prompt_kernel,
        grid=(BATCH, nc),
        in_specs=[pl.BlockSpec((MIX_ROWS, O_N), rows),
                  pl.BlockSpec((MIX_ROWS, D_MODEL), rows),
                  pl.BlockSpec((128, D_MODEL), lambda b, c: (O_GATE // 128, 0)),
                  pl.BlockSpec((1, 128), const2),
                  pl.BlockSpec((1, W_C), const2),
                  pl.BlockSpec((1, W_D), const2),
                  pl.BlockSpec((1, W_D), const2),
                  pl.BlockSpec((G_D, CHUNK, CHUNK), lambda b, c: (0, 0, 0)),
                  pl.BlockSpec((CHUNK, G_D), const2)],
        out_specs=(pl.BlockSpec((MIX_ROWS, W_C), rows),
                   pl.BlockSpec((MIX_ROWS, W_D), rows),
                   pl.BlockSpec((1, H_C, DV_C, DQK_C), lambda b, c: (b, 0, 0, 0)),
                   pl.BlockSpec((1, H_C, DQK_C), lambda b, c: (b, 0, 0)),
                   pl.BlockSpec((1, 8, 128), lambda b, c: (b, 0, 0))),
        out_shape=(jax.ShapeDtypeStruct((BATCH * SEQ, W_C), BF16),
                   jax.ShapeDtypeStruct((BATCH * SEQ, W_D), BF16),
                   jax.ShapeDtypeStruct((BATCH, H_C, DV_C, DQK_C), F32),
                   jax.ShapeDtypeStruct((BATCH, H_C, DQK_C), F32),
                   jax.ShapeDtypeStruct((BATCH, 8, 128), F32)),
        scratch_shapes=[pltpu.VMEM((128, D_MODEL), BF16),
                        pltpu.VMEM((G_D, CHUNK, CHUNK), BF16)],
        compiler_params=_params(("arbitrary", "arbitrary")),
        name="odd_prompt",
    )(p, h, w_o, bg, gm, lng, lnb, ws, bst)


CG = 2
CW = 128 + CG * (2 * DQK_C + 2 * DV_C)
CT = 1024
CN = BATCH * SEQ // CT
CPB = SEQ // CT


def _odd_heads_kernel(hp_ref, hs_ref, wq_ref, wk_ref, wv_ref, wz_ref, wg_ref, bg_ref, gm_ref,
                      yc_ref, c_ref, n_ref, m_ref, sq_ref, sk_ref, sv_ref, sz_ref,
                      wb, pt, c_scr, n_scr, m_scr, gt_scr):
    grp = pl.program_id(0)
    s = pl.program_id(1)
    L = CHUNK
    gc = slice(0, 128)
    qc = slice(128, 128 + CG * DQK_C)
    kc = slice(qc.stop, qc.stop + CG * DQK_C)
    vc = slice(kc.stop, kc.stop + CG * DV_C)
    zc = slice(vc.stop, vc.stop + CG * DV_C)
    head = lambda sl, i, w: slice(sl.start + i * w, sl.start + (i + 1) * w)

    @pl.when(s == 0)
    def _():
        wb[qc, :] = wq_ref[...].astype(BF16)
        wb[kc, :] = wk_ref[...].astype(BF16)
        wb[vc, :] = wv_ref[...].astype(BF16)
        wb[zc, :] = wz_ref[...].astype(BF16)
        wb[gc, :] = wg_ref[...].astype(BF16)
        ps = _dot_nt(hs_ref[...], wb[qc.start:CW, :])
        off = lambda sl: slice(sl.start - qc.start, sl.stop - qc.start)
        sq_ref[...] = ps[:, off(qc)]
        sk_ref[...] = ps[:, off(kc)]
        sv_ref[...] = ps[:, off(vc)]
        sz_ref[...] = ps[:, off(zc)]

    @pl.when(s > 0)
    def _():
        t = s - 1
        n = CT // L
        rows = lambda c: slice(c * L, (c + 1) * L)

        def piece(k, j):
            pr = slice(k * PCH * L, (k + 1) * PCH * L)
            pc = (slice(0, vc.start), slice(vc.start, CW))[j]
            pt[pr, pc] = _dot_nt(hp_ref[pr, :], wb[pc, :])

        row = lax.broadcasted_iota(jnp.int32, (L, L), 0)
        col = lax.broadcasted_iota(jnp.int32, (L, L), 1)
        tri = row >= col
        fresh = t % CPB == 0
        cst = {0: [jnp.where(fresh, 0.0, c_scr[i]) for i in range(CG)]}
        nst = {0: [jnp.where(fresh, 0.0, n_scr[i]) for i in range(CG)]}
        mst = {0: [jnp.where(fresh, 0.0, m_scr[i, 0:1, 0:1]) for i in range(CG)]}
        v = {}
        gates = {}

        def gate(k):
            cs = range(k * PCH, (k + 1) * PCH)
            for c in cs:
                gt_scr[c] = (pt[rows(c), gc] + bg_ref[...]).T
            pad = jnp.zeros((8 - CG * PCH, L), F32)
            ig_rows = jnp.concatenate(
                [gt_scr[c, pl.ds(grp * CG + i, 1), :] for i in range(CG) for c in cs] + [pad], axis=0)
            lf_rows = jnp.concatenate(
                [_log_sigmoid(gt_scr[c, pl.ds(grp * CG + i + H_C, 1), :]) for i in range(CG) for c in cs]
                + [pad], axis=0)
            b_rows = _dot_hi(lf_rows, jnp.where(row <= col, 1.0, 0.0))
            tall = jnp.zeros((L - 8, L), F32)
            gates[k] = dict(ig_rows=ig_rows, b_rows=b_rows,
                            b_cols=jnp.concatenate([b_rows, tall], axis=0).T,
                            ig_cols=jnp.concatenate([ig_rows, tall], axis=0).T)

        def stage_a(c):
            gk = gates[c // PCH]
            v[c] = []
            nst[c + 1] = []
            mst[c + 1] = []
            for i in range(CG):
                r = i * PCH + c % PCH
                b_r = gk["b_rows"][r:r + 1, :]
                ig_r = gk["ig_rows"][r:r + 1, :]
                b_c = gk["b_cols"][:, r:r + 1]
                ig_c = gk["ig_cols"][:, r:r + 1]
                m_prev = mst[c][i]
                log_d = jnp.where(tri, b_c - b_r + ig_r, NEG_INF)
                log_inter = b_c + m_prev
                m_t = jnp.maximum(log_inter, jnp.max(log_d, axis=-1, keepdims=True))
                m_new = m_t[L - 1:L, :]
                b_last = b_c[L - 1:L, :]
                w_end = jnp.exp(b_last - b_c + ig_c - m_new)
                cd = jnp.exp(b_last + m_prev - m_new)
                q = pt[rows(c), head(qc, i, DQK_C)] * (DQK_C ** -0.5)
                k = pt[rows(c), head(kc, i, DQK_C)]
                vv = pt[rows(c), head(vc, i, DV_C)]
                nst[c + 1].append(cd * nst[c][i] + jnp.sum(w_end * k, axis=0, keepdims=True))
                mst[c + 1].append(m_new)
                v[c].append(dict(w=jnp.exp(log_d - m_t), w_inter=jnp.exp(log_inter - m_t),
                                 floor=jnp.exp(-m_t), cd=cd, qb=q.astype(BF16), kb=k.astype(BF16),
                                 vb=vv.astype(BF16), vw=(vv * w_end).astype(BF16),
                                 qn=jnp.sum(q * nst[c][i], axis=-1, keepdims=True)))

        def stage_b(c):
            for i, d in enumerate(v[c]):
                d["sc"] = _dot_nt(d["qb"], d["kb"])
                d["upd"] = _dot_tn(d["vw"], d["kb"])
            for i, d in enumerate(v[c]):
                d["inter"] = _dot_nt(d["qb"], cst[c][i].astype(BF16))

        def stage_c(c):
            cst[c + 1] = []
            for i, d in enumerate(v[c]):
                sc = d["sc"] * d["w"]
                d["den"] = jnp.sum(sc, axis=-1, keepdims=True) + d["w_inter"] * d["qn"]
                d["sc"] = sc.astype(BF16)
                cst[c + 1].append(d["cd"] * cst[c][i] + d["upd"])

        def stage_d(c):
            for d in v[c]:
                d["num"] = _dot(d["sc"], d["vb"])

        def stage_e(c):
            for i, d in enumerate(v.pop(c)):
                num = d["num"] + d["w_inter"] * d["inter"]
                hh = num / jnp.maximum(jnp.abs(d["den"]), d["floor"])
                z = pt[rows(c), head(zc, i, DV_C)]
                ys = slice(i * DV_C, (i + 1) * DV_C)
                yc_ref[rows(c), ys] = (_head_norm(hh, gm_ref[0:1, ys]) * _silu(z)).astype(BF16)

        _chunk_pipeline(n, piece, (stage_a, stage_b, stage_c, stage_d, stage_e), gate)
        for i in range(CG):
            c_scr[i] = cst[n][i]
            n_scr[i] = nst[n][i]
            m_scr[i] = jnp.broadcast_to(mst[n][i], (8, 128))
            c_ref[0, i] = cst[n][i]
            n_ref[0, i] = nst[n][i]
            m_ref[0, i] = jnp.broadcast_to(mst[n][i], (1, 128))


def _odd_heads(hp, hs, w_t, bg, gm):
    k = hp.shape[1]
    ms = hs.shape[0]
    tile = lambda s: jnp.maximum(s - 1, 0)
    qw, vw = CG * DQK_C, CG * DV_C
    koff = H_C * DQK_C // qw
    voff = 2 * H_C * DQK_C // vw
    zoff = (2 * H_C * DQK_C + W_C) // vw
    seq = lambda g, s: (tile(s) // CPB, g, 0, 0)
    once = pl.Buffered(1)
    return pl.pallas_call(
        _odd_heads_kernel,
        grid=(H_C // CG, CN + 1),
        in_specs=[pl.BlockSpec((CT, k), lambda g, s: (tile(s), 0)),
                  pl.BlockSpec((ms, k), lambda g, s: (0, 0), pipeline_mode=once),
                  pl.BlockSpec((qw, k), lambda g, s: (g, 0), pipeline_mode=once),
                  pl.BlockSpec((qw, k), lambda g, s: (koff + g, 0), pipeline_mode=once),
                  pl.BlockSpec((vw, k), lambda g, s: (voff + g, 0), pipeline_mode=once),
                  pl.BlockSpec((vw, k), lambda g, s: (zoff + g, 0), pipeline_mode=once),
                  pl.BlockSpec((128, k), lambda g, s: (O_GATE // 128, 0), pipeline_mode=once),
                  pl.BlockSpec((1, 128), lambda g, s: (0, 0)),
                  pl.BlockSpec((1, vw), lambda g, s: (0, g))],
        out_specs=(pl.BlockSpec((CT, vw), lambda g, s: (tile(s), g)),
                   pl.BlockSpec((1, CG, DV_C, DQK_C), seq),
                   pl.BlockSpec((1, CG, 1, DQK_C), seq),
                   pl.BlockSpec((1, CG, 1, 128), seq),
                   pl.BlockSpec((ms, qw), lambda g, s: (0, g)),
                   pl.BlockSpec((ms, qw), lambda g, s: (0, g)),
                   pl.BlockSpec((ms, vw), lambda g, s: (0, g)),
                   pl.BlockSpec((ms, vw), lambda g, s: (0, g))),
        out_shape=(jax.ShapeDtypeStruct((BATCH * SEQ, W_C), BF16),
                   jax.ShapeDtypeStruct((BATCH, H_C, DV_C, DQK_C), F32),
                   jax.ShapeDtypeStruct((BATCH, H_C, 1, DQK_C), F32),
                   jax.ShapeDtypeStruct((BATCH, H_C, 1, 128), F32),
                   jax.ShapeDtypeStruct((ms, H_C * DQK_C), F32),
                   jax.ShapeDtypeStruct((ms, H_C * DQK_C), F32),
                   jax.ShapeDtypeStruct((ms, W_C), F32),
                   jax.ShapeDtypeStruct((ms, W_C), F32)),
        scratch_shapes=[pltpu.VMEM((CW, k), BF16),
                        pltpu.VMEM((CT, CW), F32),
                        pltpu.VMEM((CG, DV_C, DQK_C), F32),
                        pltpu.VMEM((CG, 1, DQK_C), F32),
                        pltpu.VMEM((CG, 8, 128), F32),
                        pltpu.VMEM((CT // CHUNK, CHUNK, CHUNK), F32)],
        compiler_params=_params(("arbitrary", "arbitrary")),
        name="odd_heads",
    )(hp, hs, w_t, w_t, w_t, w_t, w_t, bg, gm)


def _odd_mlp_kernel(p_ref, lng_ref, lnb_ref, ws_ref, bst_ref, yd_ref, wsb):
    L = CHUNK

    @pl.when(pl.program_id(0) == 0)
    def _():
        keep = (lax.broadcasted_iota(jnp.int32, (L, L), 0) >= lax.broadcasted_iota(jnp.int32, (L, L), 1))
        for g in range(G_D):
            wsb[g] = jnp.where(keep, ws_ref[g], 0.0).astype(BF16)

    def chunk(ci, carry):
        rs = pl.ds(pl.multiple_of(ci * L, L), L)
        dv = lambda g: p_ref[rs, W_D + g * 128:W_D + (g + 1) * 128]
        tot = dv(0)
        for g in range(1, G_D):
            tot = tot + dv(g)
        mu = jnp.sum(tot, axis=-1, keepdims=True) * (1.0 / W_D)
        sq = (dv(0) - mu) * (dv(0) - mu)
        for g in range(1, G_D):
            sq = sq + (dv(g) - mu) * (dv(g) - mu)
        rstd = lax.rsqrt(jnp.sum(sq, axis=-1, keepdims=True) * (1.0 / W_D) + EPS)
        for g in range(G_D):
            sl = slice(g * 128, (g + 1) * 128)
            vn = (dv(g) - mu) * rstd * lng_ref[0:1, sl] + lnb_ref[0:1, sl]
            s = _dot(wsb[g], vn.astype(BF16)) + bst_ref[:, g:g + 1]
            d_u = p_ref[rs, g * 128:(g + 1) * 128]
            d_z = p_ref[rs, 2 * W_D + g * 128:2 * W_D + (g + 1) * 128]
            yd_ref[rs, sl] = (d_u * s * _silu(d_z)).astype(BF16)
        return carry

    lax.fori_loop(0, p_ref.shape[0] // L, chunk, 0)


def _odd_mlp(p, lng, lnb, ws, bst, rows_per_step):
    m = p.shape[0]
    rows = lambda i: (i, 0)
    const2 = lambda i: (0, 0)
    return pl.pallas_call(
        _odd_mlp_kernel,
        grid=(m // rows_per_step,),
        in_specs=[pl.BlockSpec((rows_per_step, 3 * W_D), rows),
                  pl.BlockSpec((1, W_D), const2),
                  pl.BlockSpec((1, W_D), const2),
                  pl.BlockSpec((G_D, CHUNK, CHUNK), lambda i: (0, 0, 0)),
                  pl.BlockSpec((CHUNK, G_D), const2)],
        out_specs=pl.BlockSpec((rows_per_step, W_D), rows),
        out_shape=jax.ShapeDtypeStruct((m, W_D), BF16),
        scratch_shapes=[pltpu.VMEM((G_D, CHUNK, CHUNK), BF16)],
        compiler_params=_params(("arbitrary",)),
        name="odd_mlp",
    )(p, lng, lnb, ws, bst)


DT = 512


def _odd_mlp_fused_kernel(hp_ref, w0_ref, w1_ref, w2_ref, wt_ref, lng_ref, lnb_ref, ws_ref, bst_ref,
                          yd_ref, wb, pt, wsb):
    s = pl.program_id(0)
    L = CHUNK
    uc = slice(0, W_D)
    vc = slice(W_D, 2 * W_D)
    zc = slice(2 * W_D, 3 * W_D)

    @pl.when(s == 0)
    def _():
        sh = N_GATE
        wb[uc, :] = jnp.concatenate([w0_ref[sh:, :], w1_ref[0:sh, :]], axis=0).astype(BF16)
        wb[vc, :] = jnp.concatenate([w1_ref[sh:, :], w2_ref[0:sh, :]], axis=0).astype(BF16)
        wb[zc, :] = jnp.concatenate([w2_ref[sh:, :], wt_ref[...]], axis=0).astype(BF16)
        keep = (lax.broadcasted_iota(jnp.int32, (L, L), 0) >= lax.broadcasted_iota(jnp.int32, (L, L), 1))
        for g in range(G_D):
            wsb[g] = jnp.where(keep, ws_ref[g], 0.0).astype(BF16)

    @pl.when(s > 0)
    def _():
        n = DT // L
        rows = lambda c: slice(c * L, (c + 1) * L)
        grp = lambda sl, g: slice(sl.start + g * 128, sl.start + (g + 1) * 128)
        vn = {}
        mix = {}

        def project(pc):
            pt[:, pc] = _dot_nt(hp_ref[...], wb[pc, :])

        def stage_a(c):
            dv = lambda g: pt[rows(c), grp(vc, g)]
            tot = dv(0)
            for g in range(1, G_D):
                tot = tot + dv(g)
            mu = jnp.sum(tot, axis=-1, keepdims=True) * (1.0 / W_D)
            sq = (dv(0) - mu) * (dv(0) - mu)
            for g in range(1, G_D):
                sq = sq + (dv(g) - mu) * (dv(g) - mu)
            rstd = lax.rsqrt(jnp.sum(sq, axis=-1, keepdims=True) * (1.0 / W_D) + EPS)
            vn[c] = [((dv(g) - mu) * rstd * lng_ref[0:1, g * 128:(g + 1) * 128]
                      + lnb_ref[0:1, g * 128:(g + 1) * 128]).astype(BF16) for g in range(G_D)]

        def stage_b(c):
            mix[c] = [_dot(wsb[g], vn[c][g]) for g in range(G_D)]

        def stage_e(c):
            for g in range(G_D):
                sg = mix[c][g] + bst_ref[:, g:g + 1]
                d_u = pt[rows(c), grp(uc, g)]
                d_z = pt[rows(c), grp(zc, g)]
                yd_ref[rows(c), g * 128:(g + 1) * 128] = (d_u * sg * _silu(d_z)).astype(BF16)

        project(vc)
        for c in range(n):
            stage_a(c)
        project(uc)
        for c in range(n):
            stage_b(c)
        project(zc)
        for c in range(n):
            stage_e(c)


def _odd_mlp_fused(hp, w_t, lng, lnb, ws, bst):
    k = hp.shape[1]
    m = hp.shape[0]
    tile = lambda s: (jnp.maximum(s - 1, 0), 0)
    const2 = lambda s: (0, 0)
    t0 = O_GATE // IN_TN
    once = pl.Buffered(1)
    wspec = lambda j: pl.BlockSpec((IN_TN, k), lambda s: (t0 + j, 0), pipeline_mode=once)
    return pl.pallas_call(
        _odd_mlp_fused_kernel,
        grid=(m // DT + 1,),
        in_specs=[pl.BlockSpec((DT, k), tile),
                  wspec(0), wspec(1), wspec(2),
                  pl.BlockSpec((N_GATE, k), lambda s: ((t0 + 3) * (IN_TN // N_GATE), 0), pipeline_mode=once),
                  pl.BlockSpec((1, W_D), const2),
                  pl.BlockSpec((1, W_D), const2),
                  pl.BlockSpec((G_D, CHUNK, CHUNK), lambda s: (0, 0, 0)),
                  pl.BlockSpec((CHUNK, G_D), const2)],
        out_specs=pl.BlockSpec((DT, W_D), tile),
        out_shape=jax.ShapeDtypeStruct((m, W_D), BF16),
        scratch_shapes=[pltpu.VMEM((3 * W_D, k), BF16),
                        pltpu.VMEM((DT, 3 * W_D), F32),
                        pltpu.VMEM((G_D, CHUNK, CHUNK), BF16)],
        compiler_params=_params(("arbitrary",)),
        name="odd_mlp_fused",
    )(hp, w_t, w_t, w_t, w_t, lng, lnb, ws, bst)


def _in_proj_rows_kernel(h_ref, w_ref, wn_ref, o_ref, *, shift):
    wsh = jnp.concatenate([w_ref[shift:, :], wn_ref[...]], axis=0)
    o_ref[...] = _dot_nt(h_ref[...], wsh.astype(BF16))


def _in_proj_rows(h, w_t, n_out, shift, tile0):
    ms, k = h.shape
    return pl.pallas_call(
        functools.partial(_in_proj_rows_kernel, shift=shift),
        grid=(n_out // IN_TN,),
        in_specs=[pl.BlockSpec((ms, k), lambda j: (0, 0)),
                  pl.BlockSpec((IN_TN, k), lambda j: (j + tile0, 0)),
                  pl.BlockSpec((shift, k), lambda j: ((j + tile0 + 1) * (IN_TN // shift), 0))],
        out_specs=pl.BlockSpec((ms, IN_TN), lambda j: (0, j)),
        out_shape=jax.ShapeDtypeStruct((ms, n_out), F32),
        compiler_params=_params(("arbitrary",)),
        name="in_proj_rows",
    )(h, w_t, w_t)


def _odd_sample_kernel(pq_ref, pk_ref, pv_ref, pz_ref, h_ref, wg_ref, pd_ref,
                       c_ref, nrow_ref, mrow_ref, bg_ref, gm_ref, lng_ref, lnb_ref,
                       rtab_ref, btab_ref,
                       yc_ref, yd_ref, vn_ref, co_ref, no_ref, mo_ref,
                       inter_scr):
    h = pl.program_id(1)
    row = lax.broadcasted_iota(jnp.int32, (SR, SR), 0)
    col = lax.broadcasted_iota(jnp.int32, (SR, SR), 1)
    trow = row & 3

    @pl.when(h == 0)
    def _():
        dv = pd_ref[:, W_D:2 * W_D]
        mu = jnp.mean(dv, axis=-1, keepdims=True)
        xc = dv - mu
        var = jnp.mean(xc * xc, axis=-1, keepdims=True)
        rstd = lax.rsqrt(var + EPS)
        for g in range(G_D):
            sl = slice(g * 128, (g + 1) * 128)
            vn = xc[:, sl] * rstd * lng_ref[0:1, sl] + lnb_ref[0:1, sl]
            vn_ref[:, sl] = vn
            s = rtab_ref[0, :, sl] * vn + btab_ref[:, sl]
            for j in range(1, DEC_SEQ):
                s = s + jnp.where(trow >= j, rtab_ref[j, :, sl] * pltpu.roll(vn, j, 0), 0.0)
            d_u = pd_ref[:, g * 128:(g + 1) * 128]
            d_z = pd_ref[:, 2 * W_D + g * 128:2 * W_D + (g + 1) * 128]
            yd_ref[:, sl] = (d_u * s * _silu(d_z)).astype(BF16)

    same = (row >> 2) == (col >> 2)
    mask = jnp.where(same, trow - (col & 3), -1) >= 0
    pre = _dot_nt(h_ref[...], wg_ref[...].astype(BF16)) + bg_ref[...]
    lf = _log_sigmoid(pre)
    b_full = _dot_hi(jnp.where(mask, 1.0, 0.0), lf)
    sel_i = col == h
    sel_f = col == h + H_C
    ig_c = jnp.sum(jnp.where(sel_i, pre, 0.0), axis=-1, keepdims=True)
    b_c = jnp.sum(jnp.where(sel_f, b_full, 0.0), axis=-1, keepdims=True)
    sel_ir = row == h
    sel_fr = row == h + H_C
    ig_r = jnp.sum(jnp.where(sel_ir, pre.T, 0.0), axis=0, keepdims=True)
    b_r = jnp.sum(jnp.where(sel_fr, b_full.T, 0.0), axis=0, keepdims=True)
    m_prev = mrow_ref[0]
    log_d = jnp.where(mask, b_c - b_r + ig_r, NEG_INF)
    log_inter = b_c + m_prev
    m_t = jnp.maximum(log_inter, jnp.max(log_d, axis=-1, keepdims=True))
    w = jnp.exp(log_d - m_t)
    w_inter = jnp.exp(log_inter - m_t)
    q = pq_ref[...] * (DQK_C ** -0.5)
    k = pk_ref[...]
    v = pv_ref[...]
    qb = q.astype(BF16)
    kb = k.astype(BF16)
    vb = v.astype(BF16)
    sc = _dot_nt(qb, kb) * w
    sub = lax.broadcasted_iota(jnp.int32, (8, DV_C), 0)
    sub8 = lax.broadcasted_iota(jnp.int32, (8, DQK_C), 0)
    for g in range(SR // 8):
        q8 = q[8 * g:8 * g + 8, :]
        q2 = jnp.concatenate([jnp.where(sub8 < DEC_SEQ, q8, 0.0), jnp.where(sub8 < DEC_SEQ, 0.0, q8)], axis=1)
        c_pair = jnp.concatenate([c_ref[2 * g + beta, 0].astype(BF16) for beta in range(2)], axis=1)
        inter_scr[8 * g:8 * g + 8, :] = _dot_nt(q2.astype(BF16), c_pair)
    n_rows = nrow_ref[0]
    num = _dot(sc.astype(BF16), vb) + w_inter * inter_scr[...]
    den = jnp.sum(sc, axis=-1, keepdims=True) + w_inter * jnp.sum(q * n_rows, axis=-1, keepdims=True)
    hh = num / jnp.maximum(jnp.abs(den), jnp.exp(-m_t))
    yc_ref[...] = (_head_norm(hh, gm_ref[...]) * _silu(pz_ref[...])).astype(BF16)

    stats = jnp.where(col == 0, m_t, jnp.where(col == 1, b_c, 0.0))
    last = _dot_hi(jnp.where(col == (row | 3), 1.0, 0.0), stats)
    m_new = last[:, 0:1]
    b_last = last[:, 1:2]
    w_end = jnp.exp(b_last - b_c + ig_c - m_new)
    cd = jnp.exp(b_last + m_prev - m_new)
    mo_ref[0] = m_new
    no_ref[0] = cd * n_rows + _dot_hi(jnp.where(same, 1.0, 0.0), w_end * k)
    vwt = (v * w_end).T
    lane_b = lax.broadcasted_iota(jnp.int32, (DV_C, SR), 1) >> 2
    for b in range(SB):
        lhs = jnp.where(lane_b == b, vwt, 0.0).astype(BF16)
        cd_b = cd[4 * b + 3:4 * b + 4, :]
        co_ref[b, 0] = cd_b * c_ref[b, 0] + _dot(lhs, kb)


def _odd_sample(pc, pd, h, w_o, c_state, n_rows, m_rows, bg, gm, lng, lnb, rtab, btab):
    nb = DEC_BATCH // SB
    const2 = lambda i, h: (0, 0)
    return pl.pallas_call(
        _odd_sample_kernel,
        grid=(nb, H_C),
        in_specs=[pl.BlockSpec((SR, DQK_C), lambda i, h: (i, h)),
                  pl.BlockSpec((SR, DQK_C), lambda i, h: (i, h)),
                  pl.BlockSpec((SR, DV_C), lambda i, h: (i, h)),
                  pl.BlockSpec((SR, DV_C), lambda i, h: (i, h)),
                  pl.BlockSpec((SR, D_MODEL), lambda i, h: (i, 0)),
                  pl.BlockSpec((128, D_MODEL), lambda i, h: (O_GATE // 128, 0)),
                  pl.BlockSpec((SR, 3 * W_D), lambda i, h: (i, 0)),
                  pl.BlockSpec((SB, 1, DV_C, DQK_C), lambda i, h: (i, h, 0, 0)),
                  pl.BlockSpec((1, SR, DQK_C), lambda i, h: (h, i, 0)),
                  pl.BlockSpec((1, SR, 1), lambda i, h: (h, i, 0)),
                  pl.BlockSpec((1, 128), const2),
                  pl.BlockSpec((1, DV_C), lambda i, h: (0, h)),
                  pl.BlockSpec((1, W_D), const2),
                  pl.BlockSpec((1, W_D), const2),
                  pl.BlockSpec((DEC_SEQ, SR, W_D), lambda i, h: (0, 0, 0)),
                  pl.BlockSpec((SR, W_D), const2)],
        out_specs=(pl.BlockSpec((SR, DV_C), lambda i, h: (i, h)),
                   pl.BlockSpec((SR, W_D), lambda i, h: (i, 0)),
                   pl.BlockSpec((SR, W_D), lambda i, h: (i, 0)),
                   pl.BlockSpec((SB, 1, DV_C, DQK_C), lambda i, h: (i, h, 0, 0)),
                   pl.BlockSpec((1, SR, DQK_C), lambda i, h: (h, i, 0)),
                   pl.BlockSpec((1, SR, 1), lambda i, h: (h, i, 0))),
        out_shape=(jax.ShapeDtypeStruct((DEC_BATCH * DEC_SEQ, W_C), BF16),
                   jax.ShapeDtypeStruct((DEC_BATCH * DEC_SEQ, W_D), BF16),
                   jax.ShapeDtypeStruct((DEC_BATCH * DEC_SEQ, W_D), F32),
                   jax.ShapeDtypeStruct((DEC_BATCH, H_C, DV_C, DQK_C), F32),
                   jax.ShapeDtypeStruct((H_C, DEC_BATCH * DEC_SEQ, DQK_C), F32),
                   jax.ShapeDtypeStruct((H_C, DEC_BATCH * DEC_SEQ, 1), F32)),
        scratch_shapes=[pltpu.VMEM((SR, DV_C), F32)],
        compiler_params=_params(("arbitrary", "arbitrary")),
        name="odd_sample",
    )(*pc, h, w_o, pd, c_state, n_rows, m_rows, bg, gm, lng, lnb, rtab, btab)


def _rope_tables(pos):
    inv = ROPE_BASE ** (-jnp.arange(0, DH_B, 2, dtype=F32) / DH_B)
    ang = pos.astype(F32)[:, None] * inv[None, :]
    cos = jnp.cos(ang)
    sin = jnp.sin(ang)
    return jnp.concatenate([cos, cos], axis=-1), jnp.concatenate([-sin, sin], axis=-1)


def kernel(x_prompt, x_sample, state_conv, state_ret, state_mlstm_C, state_mlstm_n, state_mlstm_m,
           norm_even, w_in_even, conv_w, ret_norm, w_out_even,
           norm_odd, w_in_odd, b_gate_odd, mlstm_norm, ln_v_g, ln_v_b,
           w_spatial, b_spatial, w_out_odd, norm_final):
    w_in_e = w_in_even[0]
    w_out_e = w_out_even[0].astype(BF16)
    w_o = w_in_odd[0].T
    w_out_o = w_out_odd[0].astype(BF16)
    g_even = norm_even[0][None, :]
    g_odd = norm_odd[0][None, :]
    g_fin = norm_final[None, :]
    cw = conv_w[0]
    g_ret = ret_norm[0][None, :]
    bg = jnp.concatenate([b_gate_odd[0], jnp.zeros((128 - 2 * H_C,), F32)])[None, :]
    gm = mlstm_norm[0][None, :]
    lng = ln_v_g[0][None, :]
    lnb = ln_v_b[0][None, :]
    ws = w_spatial[0]
    bst = b_spatial[0].T

    cos_p, sin_p = _rope_tables(jnp.arange(SEQ, dtype=jnp.int32))
    cos_s, sin_s = _rope_tables(PAST_LEN + jnp.arange(DEC_SEQ, dtype=jnp.int32))
    cos_s = jnp.tile(cos_s, (SB, 1))
    sin_s = jnp.tile(sin_s, (SB, 1))
    lg_tab = jnp.broadcast_to(jnp.asarray(LOG_GAMMA, F32)[:, None, None], (H_B, 1, 128))

    ws4 = ws[:, :DEC_SEQ, :DEC_SEQ]
    t_idx = jnp.arange(DEC_SEQ)
    rtab = []
    for j in range(DEC_SEQ):
        coef = ws4[:, t_idx, (t_idx - j) % DEC_SEQ]
        tab = jnp.repeat(coef.T[:, :, None], 128, axis=2).reshape(DEC_SEQ, W_D)
        rtab.append(jnp.tile(tab, (SB, 1)))
    rtab = jnp.stack(rtab)
    btab = jnp.tile(jnp.repeat(b_spatial[0][:, :DEC_SEQ].T[:, :, None], 128, axis=2)
                    .reshape(DEC_SEQ, W_D), (SB, 1))

    rs = DEC_BATCH * DEC_SEQ
    xp = x_prompt.reshape(BATCH * SEQ, D_MODEL)
    xs = x_sample.reshape(rs, D_MODEL)
    hp = _norm_cast(xp, g_even, 512)
    hs = _norm_cast(xs, g_even, 512)
    ya, conv_p, *ps_a = _even_conv(hp, hs, w_in_e, cw)
    yb, ret_p, *ps_b = _even_heads(hp, hs, w_in_e, g_ret, cos_p, sin_p, lg_tab)
    st_exp = jnp.pad(state_conv[0], ((0, 0), (0, DEC_SEQ - (CONV_W - 1)), (0, 0))).reshape(rs, W_A)
    ya_s, u_s, yb_s, ret_s = _even_sample(ps_a, ps_b, st_exp, state_ret[0], cw, g_ret, cos_s, sin_s, lg_tab)
    x1, h1, x1s, h1s = _outproj(ya, yb, xp, ya_s, yb_s, xs, w_out_e, g_odd, final=False)

    yc, c_p, n_p, m_p, *ps_c = _odd_heads(h1, h1s, w_o, bg, gm)
    yd = _odd_mlp_fused(h1, w_o, lng, lnb, ws, bst)
    pd_s = _in_proj_rows(h1s, w_o, 3 * W_D, N_GATE, O_GATE // IN_TN)
    n_rows = jnp.repeat(jnp.transpose(state_mlstm_n[0], (1, 0, 2)), DEC_SEQ, axis=1)
    m_rows = jnp.repeat(state_mlstm_m[0].T, DEC_SEQ, axis=1)[:, :, None]
    yc_s, yd_s, vn_s, c_s, no_s, mo_s = _odd_sample(
        ps_c, pd_s, h1s, w_o, state_mlstm_C[0], n_rows, m_rows, bg, gm, lng, lnb, rtab, btab)
    y_prompt, y_sample = _outproj(yc, yd, x1, yc_s, yd_s, x1s, w_out_o, g_fin, final=True)

    conv_s = u_s.reshape(DEC_BATCH, DEC_SEQ, W_A)[:, DEC_SEQ - (CONV_W - 1):, :]
    n_s = jnp.transpose(no_s[:, DEC_SEQ - 1::DEC_SEQ, :], (1, 0, 2))
    m_s = mo_s[:, DEC_SEQ - 1::DEC_SEQ, 0].T
    return (y_prompt.reshape(BATCH, SEQ, D_MODEL),
            y_sample.reshape(DEC_BATCH, DEC_SEQ, D_MODEL),
            conv_p[None], conv_s[None],
            ret_p[None], ret_s[None],
            c_p[None], c_s[None],
            n_p[:, :, 0, :][None], n_s[None],
            m_p[:, :, 0, 0][None], m_s[None],
            vn_s.reshape(DEC_BATCH, DEC_SEQ, W_D)[None])
```

```python
import functools
import math

import jax
import jax.numpy as jnp
from jax import lax
from jax.experimental import pallas as pl
from jax.experimental.pallas import tpu as pltpu

F32 = jnp.float32
BF16 = jnp.bfloat16

D_MODEL = 2048
BATCH = 4
SEQ = 2048
DEC_BATCH = 128
DEC_SEQ = 4
PAST_LEN = 16384
W_A = 1024
CONV_W = 3
W_B = 1024
H_B = 8
DH_B = 128
E_IN = 8192
W_C = 1024
H_C = 4
DV_C = 256
DQK_C = 128
W_D = 1024
G_D = 8
CHUNK = 128
O_GATE = 2 * H_C * DQK_C + 2 * W_C
N_GATE = 2 * H_C
O_N = O_GATE + 3 * W_D
ROPE_BASE = 10000.0
EPS = 1e-6
LOG_GAMMA = tuple(math.log(1.0 - 2.0 ** (-5.0 - h)) for h in range(H_B))
NEG_INF = float("-inf")
VMEM_LIMIT = 56 * 1024 * 1024

NT_DIMS = (((1,), (1,)), ((), ()))
TN_DIMS = (((0,), (0,)), ((), ()))


def _silu(z):
    return z * (1.0 / (1.0 + jnp.exp(-z)))


def _log_sigmoid(x):
    return jnp.minimum(x, 0.0) - jnp.log1p(jnp.exp(-jnp.abs(x)))


def _dot(a, b):
    return jnp.dot(a, b, preferred_element_type=F32)


def _dot_nt(a, b):
    return lax.dot_general(a, b, NT_DIMS, preferred_element_type=F32)


def _dot_tn(a, b):
    return lax.dot_general(a, b, TN_DIMS, preferred_element_type=F32)


def _dot_hi(a, b):
    return jnp.dot(a, b, preferred_element_type=F32, precision=lax.Precision.HIGHEST)


def _head_norm(o, g):
    mu = jnp.mean(o, axis=-1, keepdims=True)
    oc = o - mu
    var = jnp.mean(oc * oc, axis=-1, keepdims=True)
    return oc * lax.rsqrt(var + EPS) * g


def _params(sem):
    return pltpu.CompilerParams(dimension_semantics=sem, vmem_limit_bytes=VMEM_LIMIT)


def _norm_cast_kernel(x_ref, g_ref, h_ref):
    x = x_ref[...]
    ms = jnp.mean(x * x, axis=-1, keepdims=True)
    h_ref[...] = (x * lax.rsqrt(ms + EPS) * g_ref[...]).astype(BF16)


def _norm_cast(x, g, tm):
    m, d = x.shape
    return pl.pallas_call(
        _norm_cast_kernel,
        grid=(m // tm,),
        in_specs=[pl.BlockSpec((tm, d), lambda i: (i, 0)),
                  pl.BlockSpec((1, d), lambda i: (0, 0))],
        out_specs=pl.BlockSpec((tm, d), lambda i: (i, 0)),
        out_shape=jax.ShapeDtypeStruct((m, d), BF16),
        compiler_params=_params(("arbitrary",)),
        name="norm_cast",
    )(x, g)


IN_TM = 1024
IN_TN = 1024


def _in_proj_kernel(*refs, shift_from, shift):
    if shift:
        hp_ref, hs_ref, w_ref, wn_ref, op_ref, os_ref, wb = refs
    else:
        hp_ref, hs_ref, w_ref, op_ref, os_ref, wb = refs
    j = pl.program_id(0)
    i = pl.program_id(1)

    if shift:
        @pl.when(jnp.logical_and(i == 0, j < shift_from))
        def _():
            wb[...] = w_ref[...].astype(BF16)

        @pl.when(jnp.logical_and(i == 0, j >= shift_from))
        def _():
            wb[...] = jnp.concatenate([w_ref[shift:IN_TN, :], wn_ref[...]], axis=0).astype(BF16)

        mm = _dot_nt
    else:
        @pl.when(i == 0)
        def _():
            wb[...] = w_ref[...].astype(BF16)

        mm = _dot

    @pl.when(i == 0)
    def _():
        os_ref[...] = mm(hs_ref[...], wb[...])

    @pl.when(i > 0)
    def _():
        op_ref[...] = mm(hp_ref[...], wb[...])


def _in_proj(hp, hs, w, n_out, shift_from=0, shift=0, tile0=0):
    mp, k = hp.shape
    ms = hs.shape[0]
    n_prompt = mp // IN_TM
    prow = lambda j, i: (jnp.maximum(i - 1, 0), 0)
    in_specs = [pl.BlockSpec((IN_TM, k), prow),
                pl.BlockSpec((ms, k), lambda j, i: (0, 0))]
    args = [hp, hs, w]
    if shift:
        in_specs.append(pl.BlockSpec((IN_TN, k), lambda j, i: (j + tile0, 0)))
        in_specs.append(pl.BlockSpec((shift, k), lambda j, i: ((j + tile0 + 1) * (IN_TN // shift), 0)))
        args.append(w)
        wb_shape = (IN_TN, k)
    else:
        in_specs.append(pl.BlockSpec((k, IN_TN), lambda j, i: (0, j)))
        wb_shape = (k, IN_TN)
    return pl.pallas_call(
        functools.partial(_in_proj_kernel, shift_from=shift_from, shift=shift),
        grid=(n_out // IN_TN, n_prompt + 1),
        in_specs=in_specs,
        out_specs=(pl.BlockSpec((IN_TM, IN_TN), lambda j, i: (jnp.maximum(i - 1, 0), j)),
                   pl.BlockSpec((ms, IN_TN), lambda j, i: (0, j))),
        out_shape=(jax.ShapeDtypeStruct((mp, n_out), F32),
                   jax.ShapeDtypeStruct((ms, n_out), F32)),
        scratch_shapes=[pltpu.VMEM(wb_shape, BF16)],
        compiler_params=_params(("arbitrary", "arbitrary")),
        name="in_proj",
    )(*args)


OUT_TM = 512


def _outproj_kernel(ya_ref, yb_ref, x_ref, yas_ref, ybs_ref, xs_ref, w_ref, g_ref, *out_refs,
                    final, n_prompt):
    i = pl.program_id(0)
    half = ya_ref.shape[1]
    n_out = 1 if final else 2

    def tile(ya, yb, x, outs):
        hr = OUT_TM // 2
        accs = [_dot(ya[r:r + hr, :], w_ref[0:half, :]) + _dot(yb[r:r + hr, :], w_ref[half:2 * half, :])
                for r in (0, hr)]
        for r, acc in zip((0, hr), accs):
            x1 = x[r:r + hr, :] + acc
            ms = jnp.mean(x1 * x1, axis=-1, keepdims=True)
            hn = x1 * lax.rsqrt(ms + EPS) * g_ref[...]
            if final:
                outs[0][r:r + hr, :] = hn
            else:
                outs[0][r:r + hr, :] = x1
                outs[1][r:r + hr, :] = hn.astype(BF16)

    @pl.when(i < n_prompt)
    def _():
        tile(ya_ref, yb_ref, x_ref, out_refs[:n_out])

    @pl.when(i == n_prompt)
    def _():
        tile(yas_ref, ybs_ref, xs_ref, out_refs[n_out:])


def _outproj(ya, yb, x, ya_s, yb_s, x_s, w, g, final):
    m, half = ya.shape
    ms = ya_s.shape[0]
    d = w.shape[1]
    n_prompt = m // OUT_TM
    row = lambda i: (jnp.minimum(i, n_prompt - 1), 0)
    const = lambda i: (0, 0)
    once = pl.Buffered(1)
    shapes = [jax.ShapeDtypeStruct((m, d), F32), jax.ShapeDtypeStruct((ms, d), F32)]
    specs = [pl.BlockSpec((OUT_TM, d), row), pl.BlockSpec((ms, d), const)]
    if not final:
        shapes = [shapes[0], jax.ShapeDtypeStruct((m, d), BF16), shapes[1], jax.ShapeDtypeStruct((ms, d), BF16)]
        specs = [specs[0], pl.BlockSpec((OUT_TM, d), row), specs[1], pl.BlockSpec((ms, d), const)]
    return pl.pallas_call(
        functools.partial(_outproj_kernel, final=final, n_prompt=n_prompt),
        grid=(n_prompt + 1,),
        in_specs=[pl.BlockSpec((OUT_TM, half), row),
                  pl.BlockSpec((OUT_TM, half), row),
                  pl.BlockSpec((OUT_TM, d), row),
                  pl.BlockSpec((ms, half), const, pipeline_mode=once),
                  pl.BlockSpec((ms, half), const, pipeline_mode=once),
                  pl.BlockSpec((ms, d), const, pipeline_mode=once),
                  pl.BlockSpec((2 * half, d), const, pipeline_mode=once),
                  pl.BlockSpec((1, d), const)],
        out_specs=tuple(specs),
        out_shape=tuple(shapes),
        compiler_params=_params(("arbitrary",)),
        name="out_proj_final" if final else "out_proj",
    )(ya, yb, x, ya_s, yb_s, x_s, w, g)


MIX_ROWS = 512


def _rope(x, cosf, sins):
    return x * cosf + pltpu.roll(x, DH_B // 2, 1) * sins


def _even_prompt_kernel(p_ref, cw_ref, gret_ref, cos_ref, sin_ref,
                        ya_ref, yb_ref, conv_ref, s_ref, ubuf):
    L = CHUNK

    @pl.when(pl.program_id(1) == 0)
    def _():
        ubuf[0:8, :] = jnp.zeros((8, W_A), F32)
        s_ref[...] = jnp.zeros_like(s_ref)

    def chunk(ci, carry):
        rs = pl.ds(pl.multiple_of(ci * L, L), L)

        for j in range(W_A // 128):
            sl = slice(j * 128, (j + 1) * 128)
            a_b = p_ref[rs, j * 128:(j + 1) * 128]
            a_c = p_ref[rs, W_A + j * 128:W_A + (j + 1) * 128]
            a_x = p_ref[rs, 2 * W_A + j * 128:2 * W_A + (j + 1) * 128]
            a_z = p_ref[rs, 3 * W_A + j * 128:3 * W_A + (j + 1) * 128]
            u = a_c * a_x
            ubuf[8:8 + L, sl] = u
            t0 = ubuf[6:6 + L, sl]
            t1 = ubuf[7:7 + L, sl]
            conv = cw_ref[0:1, sl] * t0 + cw_ref[1:2, sl] * t1 + cw_ref[2:3, sl] * u
            ya_ref[rs, sl] = (a_b * conv * _silu(a_z)).astype(BF16)
            ubuf[0:8, sl] = u[L - 8:L, :]

        cosf = cos_ref[rs, :]
        sins = sin_ref[rs, :]
        row = lax.broadcasted_iota(jnp.int32, (L, L), 0)
        col = lax.broadcasted_iota(jnp.int32, (L, L), 1)
        causal = row >= col
        diff = jnp.maximum(row - col, 0).astype(F32)
        ti = lax.broadcasted_iota(jnp.int32, (L, 1), 0).astype(F32)
        base = 4 * W_A
        for h in range(H_B):
            lg = LOG_GAMMA[h]
            sl = slice(h * DH_B, (h + 1) * DH_B)
            q = p_ref[rs, base + h * DH_B:base + (h + 1) * DH_B]
            k = p_ref[rs, base + W_B + h * DH_B:base + W_B + (h + 1) * DH_B]
            v = p_ref[rs, base + 2 * W_B + h * DH_B:base + 2 * W_B + (h + 1) * DH_B]
            z = p_ref[rs, base + 3 * W_B + h * DH_B:base + 3 * W_B + (h + 1) * DH_B]
            qr = _rope(q, cosf, sins)
            kr = _rope(k, cosf, sins) * (DH_B ** -0.5)
            decay = jnp.where(causal, jnp.exp(lg * diff), 0.0)
            qb = qr.astype(BF16)
            kb = kr.astype(BF16)
            vb = v.astype(BF16)
            sc = _dot_nt(qb, kb) * decay
            inner = _dot(sc.astype(BF16), vb)
            s_old = s_ref[0, h]
            cross = _dot(qb, s_old.astype(BF16)) * jnp.exp(lg * (ti + 1.0))
            kd = (kr * jnp.exp(lg * (L - 1.0 - ti))).astype(BF16)
            s_ref[0, h] = math.exp(lg * L) * s_old + _dot_tn(kd, vb)
            o = inner + cross
            yb_ref[rs, sl] = (_head_norm(o, gret_ref[0:1, sl]) * _silu(z)).astype(BF16)
        return carry

    lax.fori_loop(0, MIX_ROWS // L, chunk, 0)
    conv_ref[0] = ubuf[6:8, :]


def _even_prompt(p, conv_w, g_ret, cosf, sins):
    nc = SEQ // MIX_ROWS
    rows = lambda b, c: (b * nc + c, 0)
    const2 = lambda b, c: (0, 0)
    return pl.pallas_call(
        _even_prompt_kernel,
        grid=(BATCH, nc),
        in_specs=[pl.BlockSpec((MIX_ROWS, E_IN), rows),
                  pl.BlockSpec((CONV_W, W_A), const2),
                  pl.BlockSpec((1, W_B), const2),
                  pl.BlockSpec((MIX_ROWS, DH_B), lambda b, c: (c, 0)),
                  pl.BlockSpec((MIX_ROWS, DH_B), lambda b, c: (c, 0))],
        out_specs=(pl.BlockSpec((MIX_ROWS, W_A), rows),
                   pl.BlockSpec((MIX_ROWS, W_B), rows),
                   pl.BlockSpec((1, CONV_W - 1, W_A), lambda b, c: (b, 0, 0)),
                   pl.BlockSpec((1, H_B, DH_B, DH_B), lambda b, c: (b, 0, 0, 0))),
        out_shape=(jax.ShapeDtypeStruct((BATCH * SEQ, W_A), BF16),
                   jax.ShapeDtypeStruct((BATCH * SEQ, W_B), BF16),
                   jax.ShapeDtypeStruct((BATCH, CONV_W - 1, W_A), F32),
                   jax.ShapeDtypeStruct((BATCH, H_B, DH_B, DH_B), F32)),
        scratch_shapes=[pltpu.VMEM((CHUNK + 8, W_A), F32)],
        compiler_params=_params(("arbitrary", "arbitrary")),
        name="even_prompt",
    )(p, conv_w, g_ret, cosf, sins)


FT = 1024
FN = BATCH * SEQ // FT
FPB = SEQ // FT
HG = 2
GW = HG * 128
PCH = 2
CONV_PCH = 4


def _chunk_pipeline(n, piece, stages, gate=None):
    sa, sb, sc, sd, se = stages
    npieces = n // PCH
    for j in range(PCH):
        piece(0, j)
    if gate is not None:
        gate(0)
    for c in range(n + 2):
        k, j = c // PCH + 1, c % PCH
        if k < npieces:
            piece(k, j)
        if c < n:
            sa(c)
        if 1 <= c <= n:
            sc(c - 1)
        if c < n:
            sb(c)
        if 1 <= c <= n:
            sd(c - 1)
        if c >= 2:
            se(c - 2)
        if gate is not None and j == PCH - 1 and k < npieces:
            gate(k)


def _even_heads_kernel(hp_ref, hs_ref, wq_ref, wk_ref, wv_ref, wz_ref, gret_ref, cos_ref, sin_ref, lg_ref,
                       yb_ref, s_ref, sq_ref, sk_ref, sv_ref, sz_ref, wb, pt, s_scr):
    s = pl.program_id(1)
    L = CHUNK

    @pl.when(s == 0)
    def _():
        for part, w_ref in enumerate((wq_ref, wk_ref, wv_ref, wz_ref)):
            wb[:, part * GW:(part + 1) * GW] = w_ref[...].astype(BF16)
        ps = _dot(hs_ref[...], wb[...])
        for part, o_ref in enumerate((sq_ref, sk_ref, sv_ref, sz_ref)):
            o_ref[...] = ps[:, part * GW:(part + 1) * GW]

    @pl.when(s > 0)
    def _():
        t = s - 1
        n = FT // L
        rows = lambda c: slice(c * L, (c + 1) * L)
        cols = lambda part, i: slice(part * GW + i * DH_B, part * GW + (i + 1) * DH_B)

        def piece(k, j):
            pr = slice(k * PCH * L, (k + 1) * PCH * L)
            pc = slice(j * 2 * GW, (j + 1) * 2 * GW)
            pt[pr, pc] = _dot(hp_ref[pr, :], wb[:, pc])

        row = lax.broadcasted_iota(jnp.int32, (L, L), 0)
        col = lax.broadcasted_iota(jnp.int32, (L, L), 1)
        causal = row >= col
        diff = jnp.maximum(row - col, 0).astype(F32)
        ti = lax.broadcasted_iota(jnp.int32, (L, 1), 0).astype(F32)
        lgs = [lg_ref[i][:, 0:1] for i in range(HG)]
        decay = [jnp.where(causal, jnp.exp(lg * diff), 0.0) for lg in lgs]
        q_decay = [jnp.exp(lg * (ti + 1.0)) for lg in lgs]
        k_decay = [jnp.exp(lg * (L - 1.0 - ti)) for lg in lgs]
        gamma_l = [jnp.exp(lg * float(L)) for lg in lgs]
        state = {0: [jnp.where(t % FPB == 0, 0.0, s_scr[i]) for i in range(HG)]}
        v = {}

        def stage_a(c):
            cosf = cos_ref[rows(c), :]
            sins = sin_ref[rows(c), :]
            v[c] = []
            for i in range(HG):
                kr = _rope(pt[rows(c), cols(1, i)], cosf, sins) * (DH_B ** -0.5)
                v[c].append(dict(qb=_rope(pt[rows(c), cols(0, i)], cosf, sins).astype(BF16),
                                 kb=kr.astype(BF16),
                                 kd=(kr * k_decay[i]).astype(BF16),
                                 vb=pt[rows(c), cols(2, i)].astype(BF16)))

        def stage_b(c):
            for i, d in enumerate(v[c]):
                d["sc"] = _dot_nt(d["qb"], d["kb"])
                d["upd"] = _dot_tn(d["kd"], d["vb"])
            for i, d in enumerate(v[c]):
                d["cross"] = _dot(d["qb"], state[c][i].astype(BF16))

        def stage_c(c):
            state[c + 1] = []
            for i, d in enumerate(v[c]):
                d["sc"] = (d["sc"] * decay[i]).astype(BF16)
                state[c + 1].append(gamma_l[i] * state[c][i] + d["upd"])

        def stage_d(c):
            for d in v[c]:
                d["inner"] = _dot(d["sc"], d["vb"])

        def stage_e(c):
            for i, d in enumerate(v.pop(c)):
                o = d["inner"] + d["cross"] * q_decay[i]
                g = gret_ref[0:1, i * DH_B:(i + 1) * DH_B]
                z = pt[rows(c), cols(3, i)]
                yb_ref[rows(c), i * DH_B:(i + 1) * DH_B] = (_head_norm(o, g) * _silu(z)).astype(BF16)

        _chunk_pipeline(n, piece, (stage_a, stage_b, stage_c, stage_d, stage_e))
        for i in range(HG):
            s_scr[i] = state[n][i]
            s_ref[0, i] = state[n][i]


def _even_heads(hp, hs, w, g_ret, cosf, sins, lg_tab):
    k = hp.shape[1]
    ms = hs.shape[0]
    ng = H_B // HG
    base = 4 * W_A // GW
    tile = lambda s: jnp.maximum(s - 1, 0)
    wspec = lambda part: pl.BlockSpec((k, GW), lambda g, s: (0, base + part * ng + g))
    sspec = pl.BlockSpec((ms, GW), lambda g, s: (0, g))
    sshape = jax.ShapeDtypeStruct((ms, W_B), F32)
    return pl.pallas_call(
        _even_heads_kernel,
        grid=(ng, FN + 1),
        in_specs=[pl.BlockSpec((FT, k), lambda g, s: (tile(s), 0)),
                  pl.BlockSpec((ms, k), lambda g, s: (0, 0)),
                  wspec(0), wspec(1), wspec(2), wspec(3),
                  pl.BlockSpec((1, GW), lambda g, s: (0, g)),
                  pl.BlockSpec((FT, DH_B), lambda g, s: (tile(s) % FPB, 0)),
                  pl.BlockSpec((FT, DH_B), lambda g, s: (tile(s) % FPB, 0)),
                  pl.BlockSpec((HG, 1, 128), lambda g, s: (g, 0, 0))],
        out_specs=(pl.BlockSpec((FT, GW), lambda g, s: (tile(s), g)),
                   pl.BlockSpec((1, HG, DH_B, DH_B), lambda g, s: (tile(s) // FPB, g, 0, 0)),
                   sspec, sspec, sspec, sspec),
        out_shape=(jax.ShapeDtypeStruct((BATCH * SEQ, W_B), BF16),
                   jax.ShapeDtypeStruct((BATCH, H_B, DH_B, DH_B), F32),
                   sshape, sshape, sshape, sshape),
        scratch_shapes=[pltpu.VMEM((k, 4 * GW), BF16),
                        pltpu.VMEM((FT, 4 * GW), F32),
                        pltpu.VMEM((HG, DH_B, DH_B), F32)],
        compiler_params=_params(("arbitrary", "arbitrary")),
        name="even_heads",
    )(hp, hs, w, w, w, w, g_ret, cosf, sins, lg_tab)


def _even_conv_kernel(hp_ref, hs_ref, wb_ref, wc_ref, wx_ref, wz_ref, cw_ref,
                      ya_ref, conv_ref, sb_ref, sc_ref, sx_ref, sz_ref, wb, pt, ubuf):
    s = pl.program_id(1)
    L = CHUNK

    @pl.when(s == 0)
    def _():
        for part, w_ref in enumerate((wb_ref, wc_ref, wx_ref, wz_ref)):
            wb[:, part * GW:(part + 1) * GW] = w_ref[...].astype(BF16)
        ps = _dot(hs_ref[...], wb[...])
        for part, o_ref in enumerate((sb_ref, sc_ref, sx_ref, sz_ref)):
            o_ref[...] = ps[:, part * GW:(part + 1) * GW]

    @pl.when(s > 0)
    def _():
        t = s - 1
        n = FT // L
        rows = lambda c: slice(c * L, (c + 1) * L)
        part = lambda p, c: pt[rows(c), p * GW:(p + 1) * GW]

        @pl.when(t % FPB == 0)
        def _():
            ubuf[0:8, :] = jnp.zeros((8, GW), F32)

        @pl.when(t % FPB != 0)
        def _():
            ubuf[0:8, :] = ubuf[FT:FT + 8, :]

        def piece(k, j):
            pr = slice(k * CONV_PCH * L, (k + 1) * CONV_PCH * L)
            pc = slice(j * 2 * GW, (j + 1) * 2 * GW)
            pt[pr, pc] = _dot(hp_ref[pr, :], wb[:, pc])

        todo = [(k, j) for k in range(n // CONV_PCH) for j in range(2)]
        piece(*todo.pop(0))
        piece(*todo.pop(0))
        for c in range(n):
            if todo:
                piece(*todo.pop(0))
            u = part(1, c) * part(2, c)
            ubuf[8 + c * L:8 + (c + 1) * L, :] = u
            t0 = ubuf[6 + c * L:6 + (c + 1) * L, :]
            t1 = ubuf[7 + c * L:7 + (c + 1) * L, :]
            conv = cw_ref[0:1, :] * t0 + cw_ref[1:2, :] * t1 + cw_ref[2:3, :] * u
            ya_ref[rows(c), :] = (part(0, c) * conv * _silu(part(3, c))).astype(BF16)
        conv_ref[0] = ubuf[FT + 6:FT + 8, :]


def _even_conv(hp, hs, w, conv_w):
    k = hp.shape[1]
    ms = hs.shape[0]
    ng = W_A // GW
    tile = lambda s: jnp.maximum(s - 1, 0)
    wspec = lambda part: pl.BlockSpec((k, GW), lambda g, s: (0, part * ng + g))
    sspec = pl.BlockSpec((ms, GW), lambda g, s: (0, g))
    sshape = jax.ShapeDtypeStruct((ms, W_A), F32)
    return pl.pallas_call(
        _even_conv_kernel,
        grid=(ng, FN + 1),
        in_specs=[pl.BlockSpec((FT, k), lambda g, s: (tile(s), 0)),
                  pl.BlockSpec((ms, k), lambda g, s: (0, 0)),
                  wspec(0), wspec(1), wspec(2), wspec(3),
                  pl.BlockSpec((CONV_W, GW), lambda g, s: (0, g))],
        out_specs=(pl.BlockSpec((FT, GW), lambda g, s: (tile(s), g)),
                   pl.BlockSpec((1, CONV_W - 1, GW), lambda g, s: (tile(s) // FPB, 0, g)),
                   sspec, sspec, sspec, sspec),
        out_shape=(jax.ShapeDtypeStruct((BATCH * SEQ, W_A), BF16),
                   jax.ShapeDtypeStruct((BATCH, CONV_W - 1, W_A), F32),
                   sshape, sshape, sshape, sshape),
        scratch_shapes=[pltpu.VMEM((k, 4 * GW), BF16),
                        pltpu.VMEM((FT, 4 * GW), F32),
                        pltpu.VMEM((FT + 8, GW), F32)],
        compiler_params=_params(("arbitrary", "arbitrary")),
        name="even_conv",
    )(hp, hs, w, w, w, w, conv_w)


SB = 32
SR = SB * DEC_SEQ


def _even_sample_kernel(ab_ref, ac_ref, ax_ref, az_ref, pq_ref, pk_ref, pv_ref, pz_ref, st_ref, s_ref,
                        cw_ref, gret_ref, cos_ref, sin_ref, lg_ref,
                        ya_ref, u_ref, yb_ref, so_ref, cross_scr, lhs_scr, upd_scr):
    h = pl.program_id(1)
    row = lax.broadcasted_iota(jnp.int32, (SR, SR), 0)
    col = lax.broadcasted_iota(jnp.int32, (SR, SR), 1)
    trow = row & 3

    @pl.when(h == 0)
    def _():
        for j in range(W_A // 128):
            sl = slice(j * 128, (j + 1) * 128)
            a_b = ab_ref[:, sl]
            a_c = ac_ref[:, sl]
            a_x = ax_ref[:, sl]
            a_z = az_ref[:, sl]
            u = a_c * a_x
            e = st_ref[:, sl]
            tap1 = jnp.where(trow >= 1, pltpu.roll(u, 1, 0), pltpu.roll(e, SR - 1, 0))
            tap0 = jnp.where(trow >= 2, pltpu.roll(u, 2, 0), e)
            conv = cw_ref[0:1, sl] * tap0 + cw_ref[1:2, sl] * tap1 + cw_ref[2:3, sl] * u
            ya_ref[:, sl] = (a_b * conv * _silu(a_z)).astype(BF16)
            u_ref[:, sl] = u

    lg = lg_ref[0][:, 0:1]
    same = (row >> 2) == (col >> 2)
    dd = trow - (col & 3)
    mask = jnp.where(same, dd, -1) >= 0
    decay = jnp.where(mask, jnp.exp(lg * jnp.maximum(dd, 0).astype(F32)), 0.0)
    tcol = (lax.broadcasted_iota(jnp.int32, (SR, 1), 0) & 3).astype(F32)
    cosf = cos_ref[...]
    sins = sin_ref[...]
    qr = _rope(pq_ref[...], cosf, sins)
    kr = _rope(pk_ref[...], cosf, sins) * (DH_B ** -0.5)
    qb = qr.astype(BF16)
    kb = kr.astype(BF16)
    vb = pv_ref[...].astype(BF16)
    sc = _dot_nt(qb, kb) * decay
    inner = _dot(sc.astype(BF16), vb)
    kdt = (kr * jnp.exp(lg * (DEC_SEQ - 1.0 - tcol))).T
    gamma_l = jnp.exp(lg * float(DEC_SEQ))
    lane_b = col >> 2
    sub = lax.broadcasted_iota(jnp.int32, (8, DH_B), 0)
    for b in range(SB):
        lhs_scr[b * DH_B:(b + 1) * DH_B, :] = jnp.where(lane_b == b, kdt, 0.0).astype(BF16)
    upd_scr[...] = _dot(lhs_scr[...], vb)
    for g in range(SR // 8):
        q8 = qr[8 * g:8 * g + 8, :]
        q2 = jnp.concatenate([jnp.where(sub < DEC_SEQ, q8, 0.0), jnp.where(sub < DEC_SEQ, 0.0, q8)], axis=1)
        s_pair = [s_ref[2 * g + beta, 0] for beta in range(2)]
        cross_scr[8 * g:8 * g + 8, :] = _dot(
            q2.astype(BF16), jnp.concatenate([sp.astype(BF16) for sp in s_pair], axis=0))
        for beta in range(2):
            b = 2 * g + beta
            so_ref[b, 0] = gamma_l * s_pair[beta] + upd_scr[b * DH_B:(b + 1) * DH_B, :]
    o = inner + cross_scr[...] * jnp.exp(lg * (tcol + 1.0))
    yb_ref[...] = (_head_norm(o, gret_ref[...]) * _silu(pz_ref[...])).astype(BF16)


def _even_sample(pa, pb, st_exp, s_state, conv_w, g_ret, cosf, sins, lg_tab):
    nb = DEC_BATCH // SB
    const2 = lambda i, h: (0, 0)
    aspec = pl.BlockSpec((SR, W_A), lambda i, h: (i, 0))
    hspec = pl.BlockSpec((SR, DH_B), lambda i, h: (i, h))
    return pl.pallas_call(
        _even_sample_kernel,
        grid=(nb, H_B),
        in_specs=[aspec, aspec, aspec, aspec,
                  hspec, hspec, hspec, hspec,
                  pl.BlockSpec((SR, W_A), lambda i, h: (i, 0)),
                  pl.BlockSpec((SB, 1, DH_B, DH_B), lambda i, h: (i, h, 0, 0)),
                  pl.BlockSpec((CONV_W, W_A), const2),
                  pl.BlockSpec((1, DH_B), lambda i, h: (0, h)),
                  pl.BlockSpec((SR, DH_B), const2),
                  pl.BlockSpec((SR, DH_B), const2),
                  pl.BlockSpec((1, 1, 128), lambda i, h: (h, 0, 0))],
        out_specs=(pl.BlockSpec((SR, W_A), lambda i, h: (i, 0)),
                   pl.BlockSpec((SR, W_A), lambda i, h: (i, 0)),
                   pl.BlockSpec((SR, DH_B), lambda i, h: (i, h)),
                   pl.BlockSpec((SB, 1, DH_B, DH_B), lambda i, h: (i, h, 0, 0))),
        out_shape=(jax.ShapeDtypeStruct((DEC_BATCH * DEC_SEQ, W_A), BF16),
                   jax.ShapeDtypeStruct((DEC_BATCH * DEC_SEQ, W_A), F32),
                   jax.ShapeDtypeStruct((DEC_BATCH * DEC_SEQ, W_B), BF16),
                   jax.ShapeDtypeStruct((DEC_BATCH, H_B, DH_B, DH_B), F32)),
        scratch_shapes=[pltpu.VMEM((SR, DH_B), F32),
                        pltpu.VMEM((SB * DH_B, SR), BF16),
                        pltpu.VMEM((SB * DH_B, DH_B), F32)],
        compiler_params=_params(("arbitrary", "arbitrary")),
        name="even_sample",
    )(*pa, *pb, st_exp, s_state, conv_w, g_ret, cosf, sins, lg_tab)


def _odd_prompt_kernel(p_ref, h_ref, wg_ref, bg_ref, gm_ref, lng_ref, lnb_ref, ws_ref, bst_ref,
                       yc_ref, yd_ref, c_ref, n_ref, m_ref, wgb, wsb):
    L = CHUNK

    @pl.when(jnp.logical_and(pl.program_id(0) == 0, pl.program_id(1) == 0))
    def _():
        wgb[...] = wg_ref[...].astype(BF16)
        keep = (lax.broadcasted_iota(jnp.int32, (L, L), 0) >= lax.broadcasted_iota(jnp.int32, (L, L), 1))
        for g in range(G_D):
            wsb[g] = jnp.where(keep, ws_ref[g], 0.0).astype(BF16)

    @pl.when(pl.program_id(1) == 0)
    def _():
        c_ref[...] = jnp.zeros_like(c_ref)
        n_ref[...] = jnp.zeros_like(n_ref)
        m_ref[...] = jnp.zeros_like(m_ref)

    def chunk(ci, carry):
        rs = pl.ds(pl.multiple_of(ci * L, L), L)
        row = lax.broadcasted_iota(jnp.int32, (L, L), 0)
        col = lax.broadcasted_iota(jnp.int32, (L, L), 1)
        tri = row >= col

        pre = _dot_nt(h_ref[rs, :], wgb[...]) + bg_ref[...]
        lf = _log_sigmoid(pre)
        b_c = _dot_hi(jnp.where(tri, 1.0, 0.0), lf)
        b_r = b_c.T
        pre_r = pre.T
        for h in range(H_C):
            bc = b_c[:, H_C + h:H_C + h + 1]
            br = b_r[H_C + h:H_C + h + 1, :]
            igr = pre_r[h:h + 1, :]
            igc = pre[:, h:h + 1]
            m_prev = m_ref[0, h:h + 1, 0:1]
            log_d = jnp.where(tri, bc - br + igr, NEG_INF)
            log_inter = bc + m_prev
            m_t = jnp.maximum(log_inter, jnp.max(log_d, axis=-1, keepdims=True))
            w = jnp.exp(log_d - m_t)
            w_inter = jnp.exp(log_inter - m_t)
            q = p_ref[rs, h * DQK_C:(h + 1) * DQK_C] * (DQK_C ** -0.5)
            k = p_ref[rs, H_C * DQK_C + h * DQK_C:H_C * DQK_C + (h + 1) * DQK_C]
            v = p_ref[rs, 2 * H_C * DQK_C + h * DV_C:2 * H_C * DQK_C + (h + 1) * DV_C]
            z = p_ref[rs, O_GATE - W_C + h * DV_C:O_GATE - W_C + (h + 1) * DV_C]
            qb = q.astype(BF16)
            kb = k.astype(BF16)
            vb = v.astype(BF16)
            sc = _dot_nt(qb, kb) * w
            c_old = c_ref[0, h]
            n_old = n_ref[0, h:h + 1, :]
            num = _dot(sc.astype(BF16), vb) + w_inter * _dot_nt(qb, c_old.astype(BF16))
            den = jnp.sum(sc, axis=-1, keepdims=True) + w_inter * jnp.sum(q * n_old, axis=-1, keepdims=True)
            hh = num / jnp.maximum(jnp.abs(den), jnp.exp(-m_t))
            m_new = m_t[L - 1:L, :]
            b_last = bc[L - 1:L, :]
            w_end = jnp.exp(b_last - bc + igc - m_new)
            cd = jnp.exp(b_last + m_prev - m_new)
            c_ref[0, h] = cd * c_old + _dot_tn((v * w_end).astype(BF16), kb)
            n_ref[0, h:h + 1, :] = cd * n_old + jnp.sum(w_end * k, axis=0, keepdims=True)
            m_ref[0, h:h + 1, :] = jnp.broadcast_to(m_new, (1, 128))
            sl = slice(h * DV_C, (h + 1) * DV_C)
            yc_ref[rs, sl] = (_head_norm(hh, gm_ref[0:1, sl]) * _silu(z)).astype(BF16)

        dv = lambda g: p_ref[rs, O_GATE + W_D + g * 128:O_GATE + W_D + (g + 1) * 128]
        tot = dv(0)
        for g in range(1, G_D):
            tot = tot + dv(g)
        mu = jnp.sum(tot, axis=-1, keepdims=True) * (1.0 / W_D)
        sq = (dv(0) - mu) * (dv(0) - mu)
        for g in range(1, G_D):
            sq = sq + (dv(g) - mu) * (dv(g) - mu)
        rstd = lax.rsqrt(jnp.sum(sq, axis=-1, keepdims=True) * (1.0 / W_D) + EPS)
        for g in range(G_D):
            sl = slice(g * 128, (g + 1) * 128)
            vn = (dv(g) - mu) * rstd * lng_ref[0:1, sl] + lnb_ref[0:1, sl]
            s = _dot(wsb[g], vn.astype(BF16)) + bst_ref[:, g:g + 1]
            d_u = p_ref[rs, O_GATE + g * 128:O_GATE + (g + 1) * 128]
            d_z = p_ref[rs, O_GATE + 2 * W_D + g * 128:O_GATE + 2 * W_D + (g + 1) * 128]
            yd_ref[rs, sl] = (d_u * s * _silu(d_z)).astype(BF16)
        return carry

    lax.fori_loop(0, MIX_ROWS // L, chunk, 0)


def _odd_prompt(p, h, w_o, bg, gm, lng, lnb, ws, bst):
    nc = SEQ // MIX_ROWS
    rows = lambda b, c: (b * nc + c, 0)
    const2 = lambda b, c: (0, 0)
    return pl.pallas_call(
        _odd_prompt_kernel,
        grid=(BATCH, nc),
        in_specs=[pl.BlockSpec((MIX_ROWS, O_N), rows),
                  pl.BlockSpec((MIX_ROWS, D_MODEL), rows),
                  pl.BlockSpec((128, D_MODEL), lambda b, c: (O_GATE // 128, 0)),
                  pl.BlockSpec((1, 128), const2),
                  pl.BlockSpec((1, W_C), const2),
                  pl.BlockSpec((1, W_D), const2),
                  pl.BlockSpec((1, W_D), const2),
                  pl.BlockSpec((G_D, CHUNK, CHUNK), lambda b, c: (0, 0, 0)),
                  pl.BlockSpec((CHUNK, G_D), const2)],
        out_specs=(pl.BlockSpec((MIX_ROWS, W_C), rows),
                   pl.BlockSpec((MIX_ROWS, W_D), rows),
                   pl.BlockSpec((1, H_C, DV_C, DQK_C), lambda b, c: (b, 0, 0, 0)),
                   pl.BlockSpec((1, H_C, DQK_C), lambda b, c: (b, 0, 0)),
                   pl.BlockSpec((1, 8, 128), lambda b, c: (b, 0, 0))),
        out_shape=(jax.ShapeDtypeStruct((BATCH * SEQ, W_C), BF16),
                   jax.ShapeDtypeStruct((BATCH * SEQ, W_D), BF16),
                   jax.ShapeDtypeStruct((BATCH, H_C, DV_C, DQK_C), F32),
                   jax.ShapeDtypeStruct((BATCH, H_C, DQK_C), F32),
                   jax.ShapeDtypeStruct((BATCH, 8, 128), F32)),
        scratch_shapes=[pltpu.VMEM((128, D_MODEL), BF16),
                        pltpu.VMEM((G_D, CHUNK, CHUNK), BF16)],
        compiler_params=_params(("arbitrary", "arbitrary")),
        name="odd_prompt",
    )(p, h, w_o, bg, gm, lng, lnb, ws, bst)


CG = 2
CW = 128 + CG * (2 * DQK_C + 2 * DV_C)
CT = 1024
CN = BATCH * SEQ // CT
CPB = SEQ // CT


def _odd_heads_kernel(hp_ref, hs_ref, wq_ref, wk_ref, wv_ref, wz_ref, wg_ref, bg_ref, gm_ref,
                      yc_ref, c_ref, n_ref, m_ref, sq_ref, sk_ref, sv_ref, sz_ref,
                      wb, pt, c_scr, n_scr, m_scr, gt_scr):
    grp = pl.program_id(0)
    s = pl.program_id(1)
    L = CHUNK
    gc = slice(0, 128)
    qc = slice(128, 128 + CG * DQK_C)
    kc = slice(qc.stop, qc.stop + CG * DQK_C)
    vc = slice(kc.stop, kc.stop + CG * DV_C)
    zc = slice(vc.stop, vc.stop + CG * DV_C)
    head = lambda sl, i, w: slice(sl.start + i * w, sl.start + (i + 1) * w)

    @pl.when(s == 0)
    def _():
        wb[qc, :] = wq_ref[...].astype(BF16)
        wb[kc, :] = wk_ref[...].astype(BF16)
        wb[vc, :] = wv_ref[...].astype(BF16)
        wb[zc, :] = wz_ref[...].astype(BF16)
        wb[gc, :] = wg_ref[...].astype(BF16)
        ps = _dot_nt(hs_ref[...], wb[qc.start:CW, :])
        off = lambda sl: slice(sl.start - qc.start, sl.stop - qc.start)
        sq_ref[...] = ps[:, off(qc)]
        sk_ref[...] = ps[:, off(kc)]
        sv_ref[...] = ps[:, off(vc)]
        sz_ref[...] = ps[:, off(zc)]

    @pl.when(s > 0)
    def _():
        t = s - 1
        n = CT // L
        rows = lambda c: slice(c * L, (c + 1) * L)

        def piece(k, j):
            pr = slice(k * PCH * L, (k + 1) * PCH * L)
            pc = (slice(0, vc.start), slice(vc.start, CW))[j]
            pt[pr, pc] = _dot_nt(hp_ref[pr, :], wb[pc, :])

        row = lax.broadcasted_iota(jnp.int32, (L, L), 0)
        col = lax.broadcasted_iota(jnp.int32, (L, L), 1)
        tri = row >= col
        fresh = t % CPB == 0
        cst = {0: [jnp.where(fresh, 0.0, c_scr[i]) for i in range(CG)]}
        nst = {0: [jnp.where(fresh, 0.0, n_scr[i]) for i in range(CG)]}
        mst = {0: [jnp.where(fresh, 0.0, m_scr[i, 0:1, 0:1]) for i in range(CG)]}
        v = {}
        gates = {}

        def gate(k):
            cs = range(k * PCH, (k + 1) * PCH)
            for c in cs:
                gt_scr[c] = (pt[rows(c), gc] + bg_ref[...]).T
            pad = jnp.zeros((8 - CG * PCH, L), F32)
            ig_rows = jnp.concatenate(
                [gt_scr[c, pl.ds(grp * CG + i, 1), :] for i in range(CG) for c in cs] + [pad], axis=0)
            lf_rows = jnp.concatenate(
                [_log_sigmoid(gt_scr[c, pl.ds(grp * CG + i + H_C, 1), :]) for i in range(CG) for c in cs]
                + [pad], axis=0)
            b_rows = _dot_hi(lf_rows, jnp.where(row <= col, 1.0, 0.0))
            tall = jnp.zeros((L - 8, L), F32)
            gates[k] = dict(ig_rows=ig_rows, b_rows=b_rows,
                            b_cols=jnp.concatenate([b_rows, tall], axis=0).T,
                            ig_cols=jnp.concatenate([ig_rows, tall], axis=0).T)

        def stage_a(c):
            gk = gates[c // PCH]
            v[c] = []
            nst[c + 1] = []
            mst[c + 1] = []
            for i in range(CG):
                r = i * PCH + c % PCH
                b_r = gk["b_rows"][r:r + 1, :]
                ig_r = gk["ig_rows"][r:r + 1, :]
                b_c = gk["b_cols"][:, r:r + 1]
                ig_c = gk["ig_cols"][:, r:r + 1]
                m_prev = mst[c][i]
                log_d = jnp.where(tri, b_c - b_r + ig_r, NEG_INF)
                log_inter = b_c + m_prev
                m_t = jnp.maximum(log_inter, jnp.max(log_d, axis=-1, keepdims=True))
                m_new = m_t[L - 1:L, :]
                b_last = b_c[L - 1:L, :]
                w_end = jnp.exp(b_last - b_c + ig_c - m_new)
                cd = jnp.exp(b_last + m_prev - m_new)
                q = pt[rows(c), head(qc, i, DQK_C)] * (DQK_C ** -0.5)
                k = pt[rows(c), head(kc, i, DQK_C)]
                vv = pt[rows(c), head(vc, i, DV_C)]
                nst[c + 1].append(cd * nst[c][i] + jnp.sum(w_end * k, axis=0, keepdims=True))
                mst[c + 1].append(m_new)
                v[c].append(dict(w=jnp.exp(log_d - m_t), w_inter=jnp.exp(log_inter - m_t),
                                 floor=jnp.exp(-m_t), cd=cd, qb=q.astype(BF16), kb=k.astype(BF16),
                                 vb=vv.astype(BF16), vw=(vv * w_end).astype(BF16),
                                 qn=jnp.sum(q * nst[c][i], axis=-1, keepdims=True)))

        def stage_b(c):
            for i, d in enumerate(v[c]):
                d["sc"] = _dot_nt(d["qb"], d["kb"])
                d["upd"] = _dot_tn(d["vw"], d["kb"])
            for i, d in enumerate(v[c]):
                d["inter"] = _dot_nt(d["qb"], cst[c][i].astype(BF16))

        def stage_c(c):
            cst[c + 1] = []
            for i, d in enumerate(v[c]):
                sc = d["sc"] * d["w"]
                d["den"] = jnp.sum(sc, axis=-1, keepdims=True) + d["w_inter"] * d["qn"]
                d["sc"] = sc.astype(BF16)
                cst[c + 1].append(d["cd"] * cst[c][i] + d["upd"])

        def stage_d(c):
            for d in v[c]:
                d["num"] = _dot(d["sc"], d["vb"])

        def stage_e(c):
            for i, d in enumerate(v.pop(c)):
                num = d["num"] + d["w_inter"] * d["inter"]
                hh = num / jnp.maximum(jnp.abs(d["den"]), d["floor"])
                z = pt[rows(c), head(zc, i, DV_C)]
                ys = slice(i * DV_C, (i + 1) * DV_C)
                yc_ref[rows(c), ys] = (_head_norm(hh, gm_ref[0:1, ys]) * _silu(z)).astype(BF16)

        _chunk_pipeline(n, piece, (stage_a, stage_b, stage_c, stage_d, stage_e), gate)
        for i in range(CG):
            c_scr[i] = cst[n][i]
            n_scr[i] = nst[n][i]
            m_scr[i] = jnp.broadcast_to(mst[n][i], (8, 128))
            c_ref[0, i] = cst[n][i]
            n_ref[0, i] = nst[n][i]
            m_ref[0, i] = jnp.broadcast_to(mst[n][i], (1, 128))


def _odd_heads(hp, hs, w_t, bg, gm):
    k = hp.shape[1]
    ms = hs.shape[0]
    tile = lambda s: jnp.maximum(s - 1, 0)
    qw, vw = CG * DQK_C, CG * DV_C
    koff = H_C * DQK_C // qw
    voff = 2 * H_C * DQK_C // vw
    zoff = (2 * H_C * DQK_C + W_C) // vw
    seq = lambda g, s: (tile(s) // CPB, g, 0, 0)
    once = pl.Buffered(1)
    return pl.pallas_call(
        _odd_heads_kernel,
        grid=(H_C // CG, CN + 1),
        in_specs=[pl.BlockSpec((CT, k), lambda g, s: (tile(s), 0)),
                  pl.BlockSpec((ms, k), lambda g, s: (0, 0), pipeline_mode=once),
                  pl.BlockSpec((qw, k), lambda g, s: (g, 0), pipeline_mode=once),
                  pl.BlockSpec((qw, k), lambda g, s: (koff + g, 0), pipeline_mode=once),
                  pl.BlockSpec((vw, k), lambda g, s: (voff + g, 0), pipeline_mode=once),
                  pl.BlockSpec((vw, k), lambda g, s: (zoff + g, 0), pipeline_mode=once),
                  pl.BlockSpec((128, k), lambda g, s: (O_GATE // 128, 0), pipeline_mode=once),
                  pl.BlockSpec((1, 128), lambda g, s: (0, 0)),
                  pl.BlockSpec((1, vw), lambda g, s: (0, g))],
        out_specs=(pl.BlockSpec((CT, vw), lambda g, s: (tile(s), g)),
                   pl.BlockSpec((1, CG, DV_C, DQK_C), seq),
                   pl.BlockSpec((1, CG, 1, DQK_C), seq),
                   pl.BlockSpec((1, CG, 1, 128), seq),
                   pl.BlockSpec((ms, qw), lambda g, s: (0, g)),
                   pl.BlockSpec((ms, qw), lambda g, s: (0, g)),
                   pl.BlockSpec((ms, vw), lambda g, s: (0, g)),
                   pl.BlockSpec((ms, vw), lambda g, s: (0, g))),
        out_shape=(jax.ShapeDtypeStruct((BATCH * SEQ, W_C), BF16),
                   jax.ShapeDtypeStruct((BATCH, H_C, DV_C, DQK_C), F32),
                   jax.ShapeDtypeStruct((BATCH, H_C, 1, DQK_C), F32),
                   jax.ShapeDtypeStruct((BATCH, H_C, 1, 128), F32),
                   jax.ShapeDtypeStruct((ms, H_C * DQK_C), F32),
                   jax.ShapeDtypeStruct((ms, H_C * DQK_C), F32),
                   jax.ShapeDtypeStruct((ms, W_C), F32),
                   jax.ShapeDtypeStruct((ms, W_C), F32)),
        scratch_shapes=[pltpu.VMEM((CW, k), BF16),
                        pltpu.VMEM((CT, CW), F32),
                        pltpu.VMEM((CG, DV_C, DQK_C), F32),
                        pltpu.VMEM((CG, 1, DQK_C), F32),
                        pltpu.VMEM((CG, 8, 128), F32),
                        pltpu.VMEM((CT // CHUNK, CHUNK, CHUNK), F32)],
        compiler_params=_params(("arbitrary", "arbitrary")),
        name="odd_heads",
    )(hp, hs, w_t, w_t, w_t, w_t, w_t, bg, gm)


def _odd_mlp_kernel(p_ref, lng_ref, lnb_ref, ws_ref, bst_ref, yd_ref, wsb):
    L = CHUNK

    @pl.when(pl.program_id(0) == 0)
    def _():
        keep = (lax.broadcasted_iota(jnp.int32, (L, L), 0) >= lax.broadcasted_iota(jnp.int32, (L, L), 1))
        for g in range(G_D):
            wsb[g] = jnp.where(keep, ws_ref[g], 0.0).astype(BF16)

    def chunk(ci, carry):
        rs = pl.ds(pl.multiple_of(ci * L, L), L)
        dv = lambda g: p_ref[rs, W_D + g * 128:W_D + (g + 1) * 128]
        tot = dv(0)
        for g in range(1, G_D):
            tot = tot + dv(g)
        mu = jnp.sum(tot, axis=-1, keepdims=True) * (1.0 / W_D)
        sq = (dv(0) - mu) * (dv(0) - mu)
        for g in range(1, G_D):
            sq = sq + (dv(g) - mu) * (dv(g) - mu)
        rstd = lax.rsqrt(jnp.sum(sq, axis=-1, keepdims=True) * (1.0 / W_D) + EPS)
        for g in range(G_D):
            sl = slice(g * 128, (g + 1) * 128)
            vn = (dv(g) - mu) * rstd * lng_ref[0:1, sl] + lnb_ref[0:1, sl]
            s = _dot(wsb[g], vn.astype(BF16)) + bst_ref[:, g:g + 1]
            d_u = p_ref[rs, g * 128:(g + 1) * 128]
            d_z = p_ref[rs, 2 * W_D + g * 128:2 * W_D + (g + 1) * 128]
            yd_ref[rs, sl] = (d_u * s * _silu(d_z)).astype(BF16)
        return carry

    lax.fori_loop(0, p_ref.shape[0] // L, chunk, 0)


def _odd_mlp(p, lng, lnb, ws, bst, rows_per_step):
    m = p.shape[0]
    rows = lambda i: (i, 0)
    const2 = lambda i: (0, 0)
    return pl.pallas_call(
        _odd_mlp_kernel,
        grid=(m // rows_per_step,),
        in_specs=[pl.BlockSpec((rows_per_step, 3 * W_D), rows),
                  pl.BlockSpec((1, W_D), const2),
                  pl.BlockSpec((1, W_D), const2),
                  pl.BlockSpec((G_D, CHUNK, CHUNK), lambda i: (0, 0, 0)),
                  pl.BlockSpec((CHUNK, G_D), const2)],
        out_specs=pl.BlockSpec((rows_per_step, W_D), rows),
        out_shape=jax.ShapeDtypeStruct((m, W_D), BF16),
        scratch_shapes=[pltpu.VMEM((G_D, CHUNK, CHUNK), BF16)],
        compiler_params=_params(("arbitrary",)),
        name="odd_mlp",
    )(p, lng, lnb, ws, bst)


DT = 512


def _odd_mlp_fused_kernel(hp_ref, w0_ref, w1_ref, w2_ref, wt_ref, lng_ref, lnb_ref, ws_ref, bst_ref,
                          yd_ref, wb, pt, wsb):
    s = pl.program_id(0)
    L = CHUNK
    uc = slice(0, W_D)
    vc = slice(W_D, 2 * W_D)
    zc = slice(2 * W_D, 3 * W_D)

    @pl.when(s == 0)
    def _():
        sh = N_GATE
        wb[uc, :] = jnp.concatenate([w0_ref[sh:, :], w1_ref[0:sh, :]], axis=0).astype(BF16)
        wb[vc, :] = jnp.concatenate([w1_ref[sh:, :], w2_ref[0:sh, :]], axis=0).astype(BF16)
        wb[zc, :] = jnp.concatenate([w2_ref[sh:, :], wt_ref[...]], axis=0).astype(BF16)
        keep = (lax.broadcasted_iota(jnp.int32, (L, L), 0) >= lax.broadcasted_iota(jnp.int32, (L, L), 1))
        for g in range(G_D):
            wsb[g] = jnp.where(keep, ws_ref[g], 0.0).astype(BF16)

    @pl.when(s > 0)
    def _():
        n = DT // L
        rows = lambda c: slice(c * L, (c + 1) * L)
        grp = lambda sl, g: slice(sl.start + g * 128, sl.start + (g + 1) * 128)
        vn = {}
        mix = {}

        def project(pc):
            pt[:, pc] = _dot_nt(hp_ref[...], wb[pc, :])

        def stage_a(c):
            dv = lambda g: pt[rows(c), grp(vc, g)]
            tot = dv(0)
            for g in range(1, G_D):
                tot = tot + dv(g)
            mu = jnp.sum(tot, axis=-1, keepdims=True) * (1.0 / W_D)
            sq = (dv(0) - mu) * (dv(0) - mu)
            for g in range(1, G_D):
                sq = sq + (dv(g) - mu) * (dv(g) - mu)
            rstd = lax.rsqrt(jnp.sum(sq, axis=-1, keepdims=True) * (1.0 / W_D) + EPS)
            vn[c] = [((dv(g) - mu) * rstd * lng_ref[0:1, g * 128:(g + 1) * 128]
                      + lnb_ref[0:1, g * 128:(g + 1) * 128]).astype(BF16) for g in range(G_D)]

        def stage_b(c):
            mix[c] = [_dot(wsb[g], vn[c][g]) for g in range(G_D)]

        def stage_e(c):
            for g in range(G_D):
                sg = mix[c][g] + bst_ref[:, g:g + 1]
                d_u = pt[rows(c), grp(uc, g)]
                d_z = pt[rows(c), grp(zc, g)]
                yd_ref[rows(c), g * 128:(g + 1) * 128] = (d_u * sg * _silu(d_z)).astype(BF16)

        project(vc)
        for c in range(n):
            stage_a(c)
        project(uc)
        for c in range(n):
            stage_b(c)
        project(zc)
        for c in range(n):
            stage_e(c)


def _odd_mlp_fused(hp, w_t, lng, lnb, ws, bst):
    k = hp.shape[1]
    m = hp.shape[0]
    tile = lambda s: (jnp.maximum(s - 1, 0), 0)
    const2 = lambda s: (0, 0)
    t0 = O_GATE // IN_TN
    once = pl.Buffered(1)
    wspec = lambda j: pl.BlockSpec((IN_TN, k), lambda s: (t0 + j, 0), pipeline_mode=once)
    return pl.pallas_call(
        _odd_mlp_fused_kernel,
        grid=(m // DT + 1,),
        in_specs=[pl.BlockSpec((DT, k), tile),
                  wspec(0), wspec(1), wspec(2),
                  pl.BlockSpec((N_GATE, k), lambda s: ((t0 + 3) * (IN_TN // N_GATE), 0), pipeline_mode=once),
                  pl.BlockSpec((1, W_D), const2),
                  pl.BlockSpec((1, W_D), const2),
                  pl.BlockSpec((G_D, CHUNK, CHUNK), lambda s: (0, 0, 0)),
                  pl.BlockSpec((CHUNK, G_D), const2)],
        out_specs=pl.BlockSpec((DT, W_D), tile),
        out_shape=jax.ShapeDtypeStruct((m, W_D), BF16),
        scratch_shapes=[pltpu.VMEM((3 * W_D, k), BF16),
                        pltpu.VMEM((DT, 3 * W_D), F32),
                        pltpu.VMEM((G_D, CHUNK, CHUNK), BF16)],
        compiler_params=_params(("arbitrary",)),
        name="odd_mlp_fused",
    )(hp, w_t, w_t, w_t, w_t, lng, lnb, ws, bst)


def _in_proj_rows_kernel(h_ref, w_ref, wn_ref, o_ref, *, shift):
    wsh = jnp.concatenate([w_ref[shift:, :], wn_ref[...]], axis=0)
    o_ref[...] = _dot_nt(h_ref[...], wsh.astype(BF16))


def _in_proj_rows(h, w_t, n_out, shift, tile0):
    ms, k = h.shape
    return pl.pallas_call(
        functools.partial(_in_proj_rows_kernel, shift=shift),
        grid=(n_out // IN_TN,),
        in_specs=[pl.BlockSpec((ms, k), lambda j: (0, 0)),
                  pl.BlockSpec((IN_TN, k), lambda j: (j + tile0, 0)),
                  pl.BlockSpec((shift, k), lambda j: ((j + tile0 + 1) * (IN_TN // shift), 0))],
        out_specs=pl.BlockSpec((ms, IN_TN), lambda j: (0, j)),
        out_shape=jax.ShapeDtypeStruct((ms, n_out), F32),
        compiler_params=_params(("arbitrary",)),
        name="in_proj_rows",
    )(h, w_t, w_t)


def _odd_sample_kernel(pq_ref, pk_ref, pv_ref, pz_ref, h_ref, wg_ref, pd_ref,
                       c_ref, nrow_ref, mrow_ref, bg_ref, gm_ref, lng_ref, lnb_ref,
                       rtab_ref, btab_ref,
                       yc_ref, yd_ref, vn_ref, co_ref, no_ref, mo_ref,
                       inter_scr):
    h = pl.program_id(1)
    row = lax.broadcasted_iota(jnp.int32, (SR, SR), 0)
    col = lax.broadcasted_iota(jnp.int32, (SR, SR), 1)
    trow = row & 3

    @pl.when(h == 0)
    def _():
        dv = pd_ref[:, W_D:2 * W_D]
        mu = jnp.mean(dv, axis=-1, keepdims=True)
        xc = dv - mu
        var = jnp.mean(xc * xc, axis=-1, keepdims=True)
        rstd = lax.rsqrt(var + EPS)
        for g in range(G_D):
            sl = slice(g * 128, (g + 1) * 128)
            vn = xc[:, sl] * rstd * lng_ref[0:1, sl] + lnb_ref[0:1, sl]
            vn_ref[:, sl] = vn
            s = rtab_ref[0, :, sl] * vn + btab_ref[:, sl]
            for j in range(1, DEC_SEQ):
                s = s + jnp.where(trow >= j, rtab_ref[j, :, sl] * pltpu.roll(vn, j, 0), 0.0)
            d_u = pd_ref[:, g * 128:(g + 1) * 128]
            d_z = pd_ref[:, 2 * W_D + g * 128:2 * W_D + (g + 1) * 128]
            yd_ref[:, sl] = (d_u * s * _silu(d_z)).astype(BF16)

    same = (row >> 2) == (col >> 2)
    mask = jnp.where(same, trow - (col & 3), -1) >= 0
    pre = _dot_nt(h_ref[...], wg_ref[...].astype(BF16)) + bg_ref[...]
    lf = _log_sigmoid(pre)
    b_full = _dot_hi(jnp.where(mask, 1.0, 0.0), lf)
    sel_i = col == h
    sel_f = col == h + H_C
    ig_c = jnp.sum(jnp.where(sel_i, pre, 0.0), axis=-1, keepdims=True)
    b_c = jnp.sum(jnp.where(sel_f, b_full, 0.0), axis=-1, keepdims=True)
    sel_ir = row == h
    sel_fr = row == h + H_C
    ig_r = jnp.sum(jnp.where(sel_ir, pre.T, 0.0), axis=0, keepdims=True)
    b_r = jnp.sum(jnp.where(sel_fr, b_full.T, 0.0), axis=0, keepdims=True)
    m_prev = mrow_ref[0]
    log_d = jnp.where(mask, b_c - b_r + ig_r, NEG_INF)
    log_inter = b_c + m_prev
    m_t = jnp.maximum(log_inter, jnp.max(log_d, axis=-1, keepdims=True))
    w = jnp.exp(log_d - m_t)
    w_inter = jnp.exp(log_inter - m_t)
    q = pq_ref[...] * (DQK_C ** -0.5)
    k = pk_ref[...]
    v = pv_ref[...]
    qb = q.astype(BF16)
    kb = k.astype(BF16)
    vb = v.astype(BF16)
    sc = _dot_nt(qb, kb) * w
    sub = lax.broadcasted_iota(jnp.int32, (8, DV_C), 0)
    sub8 = lax.broadcasted_iota(jnp.int32, (8, DQK_C), 0)
    for g in range(SR // 8):
        q8 = q[8 * g:8 * g + 8, :]
        q2 = jnp.concatenate([jnp.where(sub8 < DEC_SEQ, q8, 0.0), jnp.where(sub8 < DEC_SEQ, 0.0, q8)], axis=1)
        c_pair = jnp.concatenate([c_ref[2 * g + beta, 0].astype(BF16) for beta in range(2)], axis=1)
        inter_scr[8 * g:8 * g + 8, :] = _dot_nt(q2.astype(BF16), c_pair)
    n_rows = nrow_ref[0]
    num = _dot(sc.astype(BF16), vb) + w_inter * inter_scr[...]
    den = jnp.sum(sc, axis=-1, keepdims=True) + w_inter * jnp.sum(q * n_rows, axis=-1, keepdims=True)
    hh = num / jnp.maximum(jnp.abs(den), jnp.exp(-m_t))
    yc_ref[...] = (_head_norm(hh, gm_ref[...]) * _silu(pz_ref[...])).astype(BF16)

    stats = jnp.where(col == 0, m_t, jnp.where(col == 1, b_c, 0.0))
    last = _dot_hi(jnp.where(col == (row | 3), 1.0, 0.0), stats)
    m_new = last[:, 0:1]
    b_last = last[:, 1:2]
    w_end = jnp.exp(b_last - b_c + ig_c - m_new)
    cd = jnp.exp(b_last + m_prev - m_new)
    mo_ref[0] = m_new
    no_ref[0] = cd * n_rows + _dot_hi(jnp.where(same, 1.0, 0.0), w_end * k)
    vwt = (v * w_end).T
    lane_b = lax.broadcasted_iota(jnp.int32, (DV_C, SR), 1) >> 2
    for b in range(SB):
        lhs = jnp.where(lane_b == b, vwt, 0.0).astype(BF16)
        cd_b = cd[4 * b + 3:4 * b + 4, :]
        co_ref[b, 0] = cd_b * c_ref[b, 0] + _dot(lhs, kb)


def _odd_sample(pc, pd, h, w_o, c_state, n_rows, m_rows, bg, gm, lng, lnb, rtab, btab):
    nb = DEC_BATCH // SB
    const2 = lambda i, h: (0, 0)
    return pl.pallas_call(
        _odd_sample_kernel,
        grid=(nb, H_C),
        in_specs=[pl.BlockSpec((SR, DQK_C), lambda i, h: (i, h)),
                  pl.BlockSpec((SR, DQK_C), lambda i, h: (i, h)),
                  pl.BlockSpec((SR, DV_C), lambda i, h: (i, h)),
                  pl.BlockSpec((SR, DV_C), lambda i, h: (i, h)),
                  pl.BlockSpec((SR, D_MODEL), lambda i, h: (i, 0)),
                  pl.BlockSpec((128, D_MODEL), lambda i, h: (O_GATE // 128, 0)),
                  pl.BlockSpec((SR, 3 * W_D), lambda i, h: (i, 0)),
                  pl.BlockSpec((SB, 1, DV_C, DQK_C), lambda i, h: (i, h, 0, 0)),
                  pl.BlockSpec((1, SR, DQK_C), lambda i, h: (h, i, 0)),
                  pl.BlockSpec((1, SR, 1), lambda i, h: (h, i, 0)),
                  pl.BlockSpec((1, 128), const2),
                  pl.BlockSpec((1, DV_C), lambda i, h: (0, h)),
                  pl.BlockSpec((1, W_D), const2),
                  pl.BlockSpec((1, W_D), const2),
                  pl.BlockSpec((DEC_SEQ, SR, W_D), lambda i, h: (0, 0, 0)),
                  pl.BlockSpec((SR, W_D), const2)],
        out_specs=(pl.BlockSpec((SR, DV_C), lambda i, h: (i, h)),
                   pl.BlockSpec((SR, W_D), lambda i, h: (i, 0)),
                   pl.BlockSpec((SR, W_D), lambda i, h: (i, 0)),
                   pl.BlockSpec((SB, 1, DV_C, DQK_C), lambda i, h: (i, h, 0, 0)),
                   pl.BlockSpec((1, SR, DQK_C), lambda i, h: (h, i, 0)),
                   pl.BlockSpec((1, SR, 1), lambda i, h: (h, i, 0))),
        out_shape=(jax.ShapeDtypeStruct((DEC_BATCH * DEC_SEQ, W_C), BF16),
                   jax.ShapeDtypeStruct((DEC_BATCH * DEC_SEQ, W_D), BF16),
                   jax.ShapeDtypeStruct((DEC_BATCH * DEC_SEQ, W_D), F32),
                   jax.ShapeDtypeStruct((DEC_BATCH, H_C, DV_C, DQK_C), F32),
                   jax.ShapeDtypeStruct((H_C, DEC_BATCH * DEC_SEQ, DQK_C), F32),
                   jax.ShapeDtypeStruct((H_C, DEC_BATCH * DEC_SEQ, 1), F32)),
        scratch_shapes=[pltpu.VMEM((SR, DV_C), F32)],
        compiler_params=_params(("arbitrary", "arbitrary")),
        name="odd_sample",
    )(*pc, h, w_o, pd, c_state, n_rows, m_rows, bg, gm, lng, lnb, rtab, btab)


def _rope_tables(pos):
    inv = ROPE_BASE ** (-jnp.arange(0, DH_B, 2, dtype=F32) / DH_B)
    ang = pos.astype(F32)[:, None] * inv[None, :]
    cos = jnp.cos(ang)
    sin = jnp.sin(ang)
    return jnp.concatenate([cos, cos], axis=-1), jnp.concatenate([-sin, sin], axis=-1)


def kernel(x_prompt, x_sample, state_conv, state_ret, state_mlstm_C, state_mlstm_n, state_mlstm_m,
           norm_even, w_in_even, conv_w, ret_norm, w_out_even,
           norm_odd, w_in_odd, b_gate_odd, mlstm_norm, ln_v_g, ln_v_b,
           w_spatial, b_spatial, w_out_odd, norm_final):
    w_in_e = w_in_even[0]
    w_out_e = w_out_even[0].astype(BF16)
    w_o = w_in_odd[0].T
    w_out_o = w_out_odd[0].astype(BF16)
    g_even = norm_even[0][None, :]
    g_odd = norm_odd[0][None, :]
    g_fin = norm_final[None, :]
    cw = conv_w[0]
    g_ret = ret_norm[0][None, :]
    bg = jnp.concatenate([b_gate_odd[0], jnp.zeros((128 - 2 * H_C,), F32)])[None, :]
    gm = mlstm_norm[0][None, :]
    lng = ln_v_g[0][None, :]
    lnb = ln_v_b[0][None, :]
    ws = w_spatial[0]
    bst = b_spatial[0].T

    cos_p, sin_p = _rope_tables(jnp.arange(SEQ, dtype=jnp.int32))
    cos_s, sin_s = _rope_tables(PAST_LEN + jnp.arange(DEC_SEQ, dtype=jnp.int32))
    cos_s = jnp.tile(cos_s, (SB, 1))
    sin_s = jnp.tile(sin_s, (SB, 1))
    lg_tab = jnp.broadcast_to(jnp.asarray(LOG_GAMMA, F32)[:, None, None], (H_B, 1, 128))

    ws4 = ws[:, :DEC_SEQ, :DEC_SEQ]
    t_idx = jnp.arange(DEC_SEQ)
    rtab = []
    for j in range(DEC_SEQ):
        coef = ws4[:, t_idx, (t_idx - j) % DEC_SEQ]
        tab = jnp.repeat(coef.T[:, :, None], 128, axis=2).reshape(DEC_SEQ, W_D)
        rtab.append(jnp.tile(tab, (SB, 1)))
    rtab = jnp.stack(rtab)
    btab = jnp.tile(jnp.repeat(b_spatial[0][:, :DEC_SEQ].T[:, :, None], 128, axis=2)
                    .reshape(DEC_SEQ, W_D), (SB, 1))

    rs = DEC_BATCH * DEC_SEQ
    xp = x_prompt.reshape(BATCH * SEQ, D_MODEL)
    xs = x_sample.reshape(rs, D_MODEL)
    hp = _norm_cast(xp, g_even, 512)
    hs = _norm_cast(xs, g_even, 512)
    ya, conv_p, *ps_a = _even_conv(hp, hs, w_in_e, cw)
    yb, ret_p, *ps_b = _even_heads(hp, hs, w_in_e, g_ret, cos_p, sin_p, lg_tab)
    st_exp = jnp.pad(state_conv[0], ((0, 0), (0, DEC_SEQ - (CONV_W - 1)), (0, 0))).reshape(rs, W_A)
    ya_s, u_s, yb_s, ret_s = _even_sample(ps_a, ps_b, st_exp, state_ret[0], cw, g_ret, cos_s, sin_s, lg_tab)
    x1, h1, x1s, h1s = _outproj(ya, yb, xp, ya_s, yb_s, xs, w_out_e, g_odd, final=False)

    yc, c_p, n_p, m_p, *ps_c = _odd_heads(h1, h1s, w_o, bg, gm)
    yd = _odd_mlp_fused(h1, w_o, lng, lnb, ws, bst)
    pd_s = _in_proj_rows(h1s, w_o, 3 * W_D, N_GATE, O_GATE // IN_TN)
    n_rows = jnp.repeat(jnp.transpose(state_mlstm_n[0], (1, 0, 2)), DEC_SEQ, axis=1)
    m_rows = jnp.repeat(state_mlstm_m[0].T, DEC_SEQ, axis=1)[:, :, None]
    yc_s, yd_s, vn_s, c_s, no_s, mo_s = _odd_sample(
        ps_c, pd_s, h1s, w_o, state_mlstm_C[0], n_rows, m_rows, bg, gm, lng, lnb, rtab, btab)
    y_prompt, y_sample = _outproj(yc, yd, x1, yc_s, yd_s, x1s, w_out_o, g_fin, final=True)

    conv_s = u_s.reshape(DEC_BATCH, DEC_SEQ, W_A)[:, DEC_SEQ - (CONV_W - 1):, :]
    n_s = jnp.transpose(no_s[:, DEC_SEQ - 1::DEC_SEQ, :], (1, 0, 2))
    m_s = mo_s[:, DEC_SEQ - 1::DEC_SEQ, 0].T
    return (y_prompt.reshape(BATCH, SEQ, D_MODEL),
            y_sample.reshape(DEC_BATCH, DEC_SEQ, D_MODEL),
            conv_p[None], conv_s[None],
            ret_p[None], ret_s[None],
            c_p[None], c_s[None],
            n_p[:, :, 0, :][None], n_s[None],
            m_p[:, :, 0, 0][None], m_s[None],
            vn_s.reshape(DEC_BATCH, DEC_SEQ, W_D)[None])
```

```python
import functools
import math

import jax
import jax.numpy as jnp
from jax import lax
from jax.experimental import pallas as pl
from jax.experimental.pallas import tpu as pltpu

F32 = jnp.float32
BF16 = jnp.bfloat16

D_MODEL = 2048
BATCH = 4
SEQ = 2048
DEC_BATCH = 128
DEC_SEQ = 4
PAST_LEN = 16384
W_A = 1024
CONV_W = 3
W_B = 1024
H_B = 8
DH_B = 128
E_IN = 8192
W_C = 1024
H_C = 4
DV_C = 256
DQK_C = 128
W_D = 1024
G_D = 8
CHUNK = 128
O_GATE = 2 * H_C * DQK_C + 2 * W_C
N_GATE = 2 * H_C
O_N = O_GATE + 3 * W_D
ROPE_BASE = 10000.0
EPS = 1e-6
LOG_GAMMA = tuple(math.log(1.0 - 2.0 ** (-5.0 - h)) for h in range(H_B))
NEG_INF = float("-inf")
VMEM_LIMIT = 56 * 1024 * 1024

NT_DIMS = (((1,), (1,)), ((), ()))
TN_DIMS = (((0,), (0,)), ((), ()))


def _silu(z):
    return z * (1.0 / (1.0 + jnp.exp(-z)))


def _log_sigmoid(x):
    return jnp.minimum(x, 0.0) - jnp.log1p(jnp.exp(-jnp.abs(x)))


def _dot(a, b):
    return jnp.dot(a, b, preferred_element_type=F32)


def _dot_nt(a, b):
    return lax.dot_general(a, b, NT_DIMS, preferred_element_type=F32)


def _dot_tn(a, b):
    return lax.dot_general(a, b, TN_DIMS, preferred_element_type=F32)


def _dot_hi(a, b):
    return jnp.dot(a, b, preferred_element_type=F32, precision=lax.Precision.HIGHEST)


def _head_norm(o, g):
    mu = jnp.mean(o, axis=-1, keepdims=True)
    oc = o - mu
    var = jnp.mean(oc * oc, axis=-1, keepdims=True)
    return oc * lax.rsqrt(var + EPS) * g


def _params(sem):
    return pltpu.CompilerParams(dimension_semantics=sem, vmem_limit_bytes=VMEM_LIMIT)


def _norm_cast_kernel(x_ref, g_ref, h_ref):
    x = x_ref[...]
    ms = jnp.mean(x * x, axis=-1, keepdims=True)
    h_ref[...] = (x * lax.rsqrt(ms + EPS) * g_ref[...]).astype(BF16)


def _norm_cast(x, g, tm):
    m, d = x.shape
    return pl.pallas_call(
        _norm_cast_kernel,
        grid=(m // tm,),
        in_specs=[pl.BlockSpec((tm, d), lambda i: (i, 0)),
                  pl.BlockSpec((1, d), lambda i: (0, 0))],
        out_specs=pl.BlockSpec((tm, d), lambda i: (i, 0)),
        out_shape=jax.ShapeDtypeStruct((m, d), BF16),
        compiler_params=_params(("arbitrary",)),
        name="norm_cast",
    )(x, g)


IN_TM = 1024
IN_TN = 1024


def _in_proj_kernel(*refs, shift_from, shift):
    if shift:
        hp_ref, hs_ref, w_ref, wn_ref, op_ref, os_ref, wb = refs
    else:
        hp_ref, hs_ref, w_ref, op_ref, os_ref, wb = refs
    j = pl.program_id(0)
    i = pl.program_id(1)

    if shift:
        @pl.when(jnp.logical_and(i == 0, j < shift_from))
        def _():
            wb[...] = w_ref[...].astype(BF16)

        @pl.when(jnp.logical_and(i == 0, j >= shift_from))
        def _():
            wb[...] = jnp.concatenate([w_ref[shift:IN_TN, :], wn_ref[...]], axis=0).astype(BF16)

        mm = _dot_nt
    else:
        @pl.when(i == 0)
        def _():
            wb[...] = w_ref[...].astype(BF16)

        mm = _dot

    @pl.when(i == 0)
    def _():
        os_ref[...] = mm(hs_ref[...], wb[...])

    @pl.when(i > 0)
    def _():
        op_ref[...] = mm(hp_ref[...], wb[...])


def _in_proj(hp, hs, w, n_out, shift_from=0, shift=0, tile0=0):
    mp, k = hp.shape
    ms = hs.shape[0]
    n_prompt = mp // IN_TM
    prow = lambda j, i: (jnp.maximum(i - 1, 0), 0)
    in_specs = [pl.BlockSpec((IN_TM, k), prow),
                pl.BlockSpec((ms, k), lambda j, i: (0, 0))]
    args = [hp, hs, w]
    if shift:
        in_specs.append(pl.BlockSpec((IN_TN, k), lambda j, i: (j + tile0, 0)))
        in_specs.append(pl.BlockSpec((shift, k), lambda j, i: ((j + tile0 + 1) * (IN_TN // shift), 0)))
        args.append(w)
        wb_shape = (IN_TN, k)
    else:
        in_specs.append(pl.BlockSpec((k, IN_TN), lambda j, i: (0, j)))
        wb_shape = (k, IN_TN)
    return pl.pallas_call(
        functools.partial(_in_proj_kernel, shift_from=shift_from, shift=shift),
        grid=(n_out // IN_TN, n_prompt + 1),
        in_specs=in_specs,
        out_specs=(pl.BlockSpec((IN_TM, IN_TN), lambda j, i: (jnp.maximum(i - 1, 0), j)),
                   pl.BlockSpec((ms, IN_TN), lambda j, i: (0, j))),
        out_shape=(jax.ShapeDtypeStruct((mp, n_out), F32),
                   jax.ShapeDtypeStruct((ms, n_out), F32)),
        scratch_shapes=[pltpu.VMEM(wb_shape, BF16)],
        compiler_params=_params(("arbitrary", "arbitrary")),
        name="in_proj",
    )(*args)


OUT_TM = 512


def _outproj_kernel(ya_ref, yb_ref, x_ref, yas_ref, ybs_ref, xs_ref, w_ref, g_ref, *out_refs,
                    final, n_prompt):
    i = pl.program_id(0)
    half = ya_ref.shape[1]
    n_out = 1 if final else 2

    def tile(ya, yb, x, outs):
        acc = _dot(ya[...], w_ref[0:half, :]) + _dot(yb[...], w_ref[half:2 * half, :])
        x1 = x[...] + acc
        ms = jnp.mean(x1 * x1, axis=-1, keepdims=True)
        hn = x1 * lax.rsqrt(ms + EPS) * g_ref[...]
        if final:
            outs[0][...] = hn
        else:
            outs[0][...] = x1
            outs[1][...] = hn.astype(BF16)

    @pl.when(i < n_prompt)
    def _():
        tile(ya_ref, yb_ref, x_ref, out_refs[:n_out])

    @pl.when(i == n_prompt)
    def _():
        tile(yas_ref, ybs_ref, xs_ref, out_refs[n_out:])


def _outproj(ya, yb, x, ya_s, yb_s, x_s, w, g, final):
    m, half = ya.shape
    ms = ya_s.shape[0]
    d = w.shape[1]
    n_prompt = m // OUT_TM
    row = lambda i: (jnp.minimum(i, n_prompt - 1), 0)
    const = lambda i: (0, 0)
    once = pl.Buffered(1)
    shapes = [jax.ShapeDtypeStruct((m, d), F32), jax.ShapeDtypeStruct((ms, d), F32)]
    specs = [pl.BlockSpec((OUT_TM, d), row), pl.BlockSpec((ms, d), const)]
    if not final:
        shapes = [shapes[0], jax.ShapeDtypeStruct((m, d), BF16), shapes[1], jax.ShapeDtypeStruct((ms, d), BF16)]
        specs = [specs[0], pl.BlockSpec((OUT_TM, d), row), specs[1], pl.BlockSpec((ms, d), const)]
    return pl.pallas_call(
        functools.partial(_outproj_kernel, final=final, n_prompt=n_prompt),
        grid=(n_prompt + 1,),
        in_specs=[pl.BlockSpec((OUT_TM, half), row),
                  pl.BlockSpec((OUT_TM, half), row),
                  pl.BlockSpec((OUT_TM, d), row),
                  pl.BlockSpec((ms, half), const, pipeline_mode=once),
                  pl.BlockSpec((ms, half), const, pipeline_mode=once),
                  pl.BlockSpec((ms, d), const, pipeline_mode=once),
                  pl.BlockSpec((2 * half, d), const, pipeline_mode=once),
                  pl.BlockSpec((1, d), const)],
        out_specs=tuple(specs),
        out_shape=tuple(shapes),
        compiler_params=_params(("arbitrary",)),
        name="out_proj_final" if final else "out_proj",
    )(ya, yb, x, ya_s, yb_s, x_s, w, g)


MIX_ROWS = 512


def _rope(x, cosf, sins):
    return x * cosf + pltpu.roll(x, DH_B // 2, 1) * sins


def _even_prompt_kernel(p_ref, cw_ref, gret_ref, cos_ref, sin_ref,
                        ya_ref, yb_ref, conv_ref, s_ref, ubuf):
    L = CHUNK

    @pl.when(pl.program_id(1) == 0)
    def _():
        ubuf[0:8, :] = jnp.zeros((8, W_A), F32)
        s_ref[...] = jnp.zeros_like(s_ref)

    def chunk(ci, carry):
        rs = pl.ds(pl.multiple_of(ci * L, L), L)

        for j in range(W_A // 128):
            sl = slice(j * 128, (j + 1) * 128)
            a_b = p_ref[rs, j * 128:(j + 1) * 128]
            a_c = p_ref[rs, W_A + j * 128:W_A + (j + 1) * 128]
            a_x = p_ref[rs, 2 * W_A + j * 128:2 * W_A + (j + 1) * 128]
            a_z = p_ref[rs, 3 * W_A + j * 128:3 * W_A + (j + 1) * 128]
            u = a_c * a_x
            ubuf[8:8 + L, sl] = u
            t0 = ubuf[6:6 + L, sl]
            t1 = ubuf[7:7 + L, sl]
            conv = cw_ref[0:1, sl] * t0 + cw_ref[1:2, sl] * t1 + cw_ref[2:3, sl] * u
            ya_ref[rs, sl] = (a_b * conv * _silu(a_z)).astype(BF16)
            ubuf[0:8, sl] = u[L - 8:L, :]

        cosf = cos_ref[rs, :]
        sins = sin_ref[rs, :]
        row = lax.broadcasted_iota(jnp.int32, (L, L), 0)
        col = lax.broadcasted_iota(jnp.int32, (L, L), 1)
        causal = row >= col
        diff = jnp.maximum(row - col, 0).astype(F32)
        ti = lax.broadcasted_iota(jnp.int32, (L, 1), 0).astype(F32)
        base = 4 * W_A
        for h in range(H_B):
            lg = LOG_GAMMA[h]
            sl = slice(h * DH_B, (h + 1) * DH_B)
            q = p_ref[rs, base + h * DH_B:base + (h + 1) * DH_B]
            k = p_ref[rs, base + W_B + h * DH_B:base + W_B + (h + 1) * DH_B]
            v = p_ref[rs, base + 2 * W_B + h * DH_B:base + 2 * W_B + (h + 1) * DH_B]
            z = p_ref[rs, base + 3 * W_B + h * DH_B:base + 3 * W_B + (h + 1) * DH_B]
            qr = _rope(q, cosf, sins)
            kr = _rope(k, cosf, sins) * (DH_B ** -0.5)
            decay = jnp.where(causal, jnp.exp(lg * diff), 0.0)
            qb = qr.astype(BF16)
            kb = kr.astype(BF16)
            vb = v.astype(BF16)
            sc = _dot_nt(qb, kb) * decay
            inner = _dot(sc.astype(BF16), vb)
            s_old = s_ref[0, h]
            cross = _dot(qb, s_old.astype(BF16)) * jnp.exp(lg * (ti + 1.0))
            kd = (kr * jnp.exp(lg * (L - 1.0 - ti))).astype(BF16)
            s_ref[0, h] = math.exp(lg * L) * s_old + _dot_tn(kd, vb)
            o = inner + cross
            yb_ref[rs, sl] = (_head_norm(o, gret_ref[0:1, sl]) * _silu(z)).astype(BF16)
        return carry

    lax.fori_loop(0, MIX_ROWS // L, chunk, 0)
    conv_ref[0] = ubuf[6:8, :]


def _even_prompt(p, conv_w, g_ret, cosf, sins):
    nc = SEQ // MIX_ROWS
    rows = lambda b, c: (b * nc + c, 0)
    const2 = lambda b, c: (0, 0)
    return pl.pallas_call(
        _even_prompt_kernel,
        grid=(BATCH, nc),
        in_specs=[pl.BlockSpec((MIX_ROWS, E_IN), rows),
                  pl.BlockSpec((CONV_W, W_A), const2),
                  pl.BlockSpec((1, W_B), const2),
                  pl.BlockSpec((MIX_ROWS, DH_B), lambda b, c: (c, 0)),
                  pl.BlockSpec((MIX_ROWS, DH_B), lambda b, c: (c, 0))],
        out_specs=(pl.BlockSpec((MIX_ROWS, W_A), rows),
                   pl.BlockSpec((MIX_ROWS, W_B), rows),
                   pl.BlockSpec((1, CONV_W - 1, W_A), lambda b, c: (b, 0, 0)),
                   pl.BlockSpec((1, H_B, DH_B, DH_B), lambda b, c: (b, 0, 0, 0))),
        out_shape=(jax.ShapeDtypeStruct((BATCH * SEQ, W_A), BF16),
                   jax.ShapeDtypeStruct((BATCH * SEQ, W_B), BF16),
                   jax.ShapeDtypeStruct((BATCH, CONV_W - 1, W_A), F32),
                   jax.ShapeDtypeStruct((BATCH, H_B, DH_B, DH_B), F32)),
        scratch_shapes=[pltpu.VMEM((CHUNK + 8, W_A), F32)],
        compiler_params=_params(("arbitrary", "arbitrary")),
        name="even_prompt",
    )(p, conv_w, g_ret, cosf, sins)


FT = 1024
FN = BATCH * SEQ // FT
FPB = SEQ // FT
HG = 2
GW = HG * 128
PCH = 2
CONV_PCH = 4


def _chunk_pipeline(n, piece, stages, gate=None):
    sa, sb, sc, sd, se = stages
    npieces = n // PCH
    for j in range(PCH):
        piece(0, j)
    if gate is not None:
        gate(0)
    for c in range(n + 2):
        k, j = c // PCH + 1, c % PCH
        if k < npieces:
            piece(k, j)
        if c < n:
            sa(c)
        if 1 <= c <= n:
            sc(c - 1)
        if c < n:
            sb(c)
        if 1 <= c <= n:
            sd(c - 1)
        if c >= 2:
            se(c - 2)
        if gate is not None and j == PCH - 1 and k < npieces:
            gate(k)


def _even_heads_kernel(hp_ref, hs_ref, wq_ref, wk_ref, wv_ref, wz_ref, gret_ref, cos_ref, sin_ref, lg_ref,
                       yb_ref, s_ref, sq_ref, sk_ref, sv_ref, sz_ref, wb, pt, s_scr):
    s = pl.program_id(1)
    L = CHUNK

    @pl.when(s == 0)
    def _():
        for part, w_ref in enumerate((wq_ref, wk_ref, wv_ref, wz_ref)):
            wb[:, part * GW:(part + 1) * GW] = w_ref[...].astype(BF16)
        ps = _dot(hs_ref[...], wb[...])
        for part, o_ref in enumerate((sq_ref, sk_ref, sv_ref, sz_ref)):
            o_ref[...] = ps[:, part * GW:(part + 1) * GW]

    @pl.when(s > 0)
    def _():
        t = s - 1
        n = FT // L
        rows = lambda c: slice(c * L, (c + 1) * L)
        cols = lambda part, i: slice(part * GW + i * DH_B, part * GW + (i + 1) * DH_B)

        def piece(k, j):
            pr = slice(k * PCH * L, (k + 1) * PCH * L)
            pc = slice(j * 2 * GW, (j + 1) * 2 * GW)
            pt[pr, pc] = _dot(hp_ref[pr, :], wb[:, pc])

        row = lax.broadcasted_iota(jnp.int32, (L, L), 0)
        col = lax.broadcasted_iota(jnp.int32, (L, L), 1)
        causal = row >= col
        diff = jnp.maximum(row - col, 0).astype(F32)
        ti = lax.broadcasted_iota(jnp.int32, (L, 1), 0).astype(F32)
        lgs = [lg_ref[i][:, 0:1] for i in range(HG)]
        decay = [jnp.where(causal, jnp.exp(lg * diff), 0.0) for lg in lgs]
        q_decay = [jnp.exp(lg * (ti + 1.0)) for lg in lgs]
        k_decay = [jnp.exp(lg * (L - 1.0 - ti)) for lg in lgs]
        gamma_l = [jnp.exp(lg * float(L)) for lg in lgs]
        state = {0: [jnp.where(t % FPB == 0, 0.0, s_scr[i]) for i in range(HG)]}
        v = {}

        def stage_a(c):
            cosf = cos_ref[rows(c), :]
            sins = sin_ref[rows(c), :]
            v[c] = []
            for i in range(HG):
                kr = _rope(pt[rows(c), cols(1, i)], cosf, sins) * (DH_B ** -0.5)
                v[c].append(dict(qb=_rope(pt[rows(c), cols(0, i)], cosf, sins).astype(BF16),
                                 kb=kr.astype(BF16),
                                 kd=(kr * k_decay[i]).astype(BF16),
                                 vb=pt[rows(c), cols(2, i)].astype(BF16)))

        def stage_b(c):
            for i, d in enumerate(v[c]):
                d["upd"] = _dot_tn(d["vb"], d["kd"])
            for i, d in enumerate(v[c]):
                d["scx"] = _dot_nt(d["qb"], jnp.concatenate([d["kb"], state[c][i].astype(BF16)], axis=0))

        def stage_c(c):
            state[c + 1] = []
            for i, d in enumerate(v[c]):
                d["sc"] = (d["scx"][:, 0:L] * decay[i]).astype(BF16)
                state[c + 1].append(gamma_l[i] * state[c][i] + d["upd"])

        def stage_d(c):
            for d in v[c]:
                d["inner"] = _dot(d["sc"], d["vb"])

        def stage_e(c):
            for i, d in enumerate(v.pop(c)):
                o = d["inner"] + d["scx"][:, L:L + DH_B] * q_decay[i]
                g = gret_ref[0:1, i * DH_B:(i + 1) * DH_B]
                z = pt[rows(c), cols(3, i)]
                yb_ref[rows(c), i * DH_B:(i + 1) * DH_B] = (_head_norm(o, g) * _silu(z)).astype(BF16)

        _chunk_pipeline(n, piece, (stage_a, stage_b, stage_c, stage_d, stage_e))
        for i in range(HG):
            s_scr[i] = state[n][i]
            s_ref[0, i] = state[n][i].T


def _even_heads(hp, hs, w, g_ret, cosf, sins, lg_tab):
    k = hp.shape[1]
    ms = hs.shape[0]
    ng = H_B // HG
    base = 4 * W_A // GW
    tile = lambda s: jnp.maximum(s - 1, 0)
    wspec = lambda part: pl.BlockSpec((k, GW), lambda g, s: (0, base + part * ng + g))
    sspec = pl.BlockSpec((ms, GW), lambda g, s: (0, g))
    sshape = jax.ShapeDtypeStruct((ms, W_B), F32)
    return pl.pallas_call(
        _even_heads_kernel,
        grid=(ng, FN + 1),
        in_specs=[pl.BlockSpec((FT, k), lambda g, s: (tile(s), 0)),
                  pl.BlockSpec((ms, k), lambda g, s: (0, 0)),
                  wspec(0), wspec(1), wspec(2), wspec(3),
                  pl.BlockSpec((1, GW), lambda g, s: (0, g)),
                  pl.BlockSpec((FT, DH_B), lambda g, s: (tile(s) % FPB, 0)),
                  pl.BlockSpec((FT, DH_B), lambda g, s: (tile(s) % FPB, 0)),
                  pl.BlockSpec((HG, 1, 128), lambda g, s: (g, 0, 0))],
        out_specs=(pl.BlockSpec((FT, GW), lambda g, s: (tile(s), g)),
                   pl.BlockSpec((1, HG, DH_B, DH_B), lambda g, s: (tile(s) // FPB, g, 0, 0)),
                   sspec, sspec, sspec, sspec),
        out_shape=(jax.ShapeDtypeStruct((BATCH * SEQ, W_B), BF16),
                   jax.ShapeDtypeStruct((BATCH, H_B, DH_B, DH_B), F32),
                   sshape, sshape, sshape, sshape),
        scratch_shapes=[pltpu.VMEM((k, 4 * GW), BF16),
                        pltpu.VMEM((FT, 4 * GW), F32),
                        pltpu.VMEM((HG, DH_B, DH_B), F32)],
        compiler_params=_params(("arbitrary", "arbitrary")),
        name="even_heads",
    )(hp, hs, w, w, w, w, g_ret, cosf, sins, lg_tab)


def _even_conv_kernel(hp_ref, hs_ref, wb_ref, wc_ref, wx_ref, wz_ref, cw_ref,
                      ya_ref, conv_ref, sb_ref, sc_ref, sx_ref, sz_ref, wb, pt, ubuf):
    s = pl.program_id(1)
    L = CHUNK

    @pl.when(s == 0)
    def _():
        for part, w_ref in enumerate((wb_ref, wc_ref, wx_ref, wz_ref)):
            wb[:, part * GW:(part + 1) * GW] = w_ref[...].astype(BF16)
        ps = _dot(hs_ref[...], wb[...])
        for part, o_ref in enumerate((sb_ref, sc_ref, sx_ref, sz_ref)):
            o_ref[...] = ps[:, part * GW:(part + 1) * GW]

    @pl.when(s > 0)
    def _():
        t = s - 1
        n = FT // L
        rows = lambda c: slice(c * L, (c + 1) * L)
        part = lambda p, c: pt[rows(c), p * GW:(p + 1) * GW]

        @pl.when(t % FPB == 0)
        def _():
            ubuf[0:8, :] = jnp.zeros((8, GW), F32)

        @pl.when(t % FPB != 0)
        def _():
            ubuf[0:8, :] = ubuf[FT:FT + 8, :]

        def piece(k, j):
            pr = slice(k * CONV_PCH * L, (k + 1) * CONV_PCH * L)
            pc = slice(j * 2 * GW, (j + 1) * 2 * GW)
            pt[pr, pc] = _dot(hp_ref[pr, :], wb[:, pc])

        todo = [(k, j) for k in range(n // CONV_PCH) for j in range(2)]
        piece(*todo.pop(0))
        piece(*todo.pop(0))
        for c in range(n):
            if todo:
                piece(*todo.pop(0))
            u = part(1, c) * part(2, c)
            ubuf[8 + c * L:8 + (c + 1) * L, :] = u
            t0 = ubuf[6 + c * L:6 + (c + 1) * L, :]
            t1 = ubuf[7 + c * L:7 + (c + 1) * L, :]
            conv = cw_ref[0:1, :] * t0 + cw_ref[1:2, :] * t1 + cw_ref[2:3, :] * u
            ya_ref[rows(c), :] = (part(0, c) * conv * _silu(part(3, c))).astype(BF16)
        conv_ref[0] = ubuf[FT + 6:FT + 8, :]


def _even_conv(hp, hs, w, conv_w):
    k = hp.shape[1]
    ms = hs.shape[0]
    ng = W_A // GW
    tile = lambda s: jnp.maximum(s - 1, 0)
    wspec = lambda part: pl.BlockSpec((k, GW), lambda g, s: (0, part * ng + g))
    sspec = pl.BlockSpec((ms, GW), lambda g, s: (0, g))
    sshape = jax.ShapeDtypeStruct((ms, W_A), F32)
    return pl.pallas_call(
        _even_conv_kernel,
        grid=(ng, FN + 1),
        in_specs=[pl.BlockSpec((FT, k), lambda g, s: (tile(s), 0)),
                  pl.BlockSpec((ms, k), lambda g, s: (0, 0)),
                  wspec(0), wspec(1), wspec(2), wspec(3),
                  pl.BlockSpec((CONV_W, GW), lambda g, s: (0, g))],
        out_specs=(pl.BlockSpec((FT, GW), lambda g, s: (tile(s), g)),
                   pl.BlockSpec((1, CONV_W - 1, GW), lambda g, s: (tile(s) // FPB, 0, g)),
                   sspec, sspec, sspec, sspec),
        out_shape=(jax.ShapeDtypeStruct((BATCH * SEQ, W_A), BF16),
                   jax.ShapeDtypeStruct((BATCH, CONV_W - 1, W_A), F32),
                   sshape, sshape, sshape, sshape),
        scratch_shapes=[pltpu.VMEM((k, 4 * GW), BF16),
                        pltpu.VMEM((FT, 4 * GW), F32),
                        pltpu.VMEM((FT + 8, GW), F32)],
        compiler_params=_params(("arbitrary", "arbitrary")),
        name="even_conv",
    )(hp, hs, w, w, w, w, conv_w)


SB = 32
SR = SB * DEC_SEQ


def _even_sample_kernel(ab_ref, ac_ref, ax_ref, az_ref, pq_ref, pk_ref, pv_ref, pz_ref, st_ref, s_ref,
                        cw_ref, gret_ref, cos_ref, sin_ref, lg_ref,
                        ya_ref, u_ref, yb_ref, so_ref, cross_scr):
    h = pl.program_id(1)
    row = lax.broadcasted_iota(jnp.int32, (SR, SR), 0)
    col = lax.broadcasted_iota(jnp.int32, (SR, SR), 1)
    trow = row & 3

    @pl.when(h == 0)
    def _():
        for j in range(W_A // 128):
            sl = slice(j * 128, (j + 1) * 128)
            a_b = ab_ref[:, sl]
            a_c = ac_ref[:, sl]
            a_x = ax_ref[:, sl]
            a_z = az_ref[:, sl]
            u = a_c * a_x
            e = st_ref[:, sl]
            tap1 = jnp.where(trow >= 1, pltpu.roll(u, 1, 0), pltpu.roll(e, SR - 1, 0))
            tap0 = jnp.where(trow >= 2, pltpu.roll(u, 2, 0), e)
            conv = cw_ref[0:1, sl] * tap0 + cw_ref[1:2, sl] * tap1 + cw_ref[2:3, sl] * u
            ya_ref[:, sl] = (a_b * conv * _silu(a_z)).astype(BF16)
            u_ref[:, sl] = u

    lg = lg_ref[0][:, 0:1]
    same = (row >> 2) == (col >> 2)
    dd = trow - (col & 3)
    mask = jnp.where(same, dd, -1) >= 0
    decay = jnp.where(mask, jnp.exp(lg * jnp.maximum(dd, 0).astype(F32)), 0.0)
    tcol = (lax.broadcasted_iota(jnp.int32, (SR, 1), 0) & 3).astype(F32)
    cosf = cos_ref[...]
    sins = sin_ref[...]
    qr = _rope(pq_ref[...], cosf, sins)
    kr = _rope(pk_ref[...], cosf, sins) * (DH_B ** -0.5)
    qb = qr.astype(BF16)
    kb = kr.astype(BF16)
    vb = pv_ref[...].astype(BF16)
    sc = _dot_nt(qb, kb) * decay
    inner = _dot(sc.astype(BF16), vb)
    kdt = (kr * jnp.exp(lg * (DEC_SEQ - 1.0 - tcol))).T
    gamma_l = jnp.exp(lg * float(DEC_SEQ))
    lane_b = col >> 2
    sub = lax.broadcasted_iota(jnp.int32, (8, DH_B), 0)
    for g in range(SR // 8):
        q8 = qr[8 * g:8 * g + 8, :]
        q2 = jnp.concatenate([jnp.where(sub < DEC_SEQ, q8, 0.0), jnp.where(sub < DEC_SEQ, 0.0, q8)], axis=1)
        s_pair = [s_ref[2 * g + beta, 0] for beta in range(2)]
        cross_scr[8 * g:8 * g + 8, :] = _dot(
            q2.astype(BF16), jnp.concatenate([sp.astype(BF16) for sp in s_pair], axis=0))
        for beta in range(2):
            b = 2 * g + beta
            lhs = jnp.where(lane_b == b, kdt, 0.0).astype(BF16)
            so_ref[b, 0] = gamma_l * s_pair[beta] + _dot(lhs, vb)
    o = inner + cross_scr[...] * jnp.exp(lg * (tcol + 1.0))
    yb_ref[...] = (_head_norm(o, gret_ref[...]) * _silu(pz_ref[...])).astype(BF16)


def _even_sample(pa, pb, st_exp, s_state, conv_w, g_ret, cosf, sins, lg_tab):
    nb = DEC_BATCH // SB
    const2 = lambda i, h: (0, 0)
    aspec = pl.BlockSpec((SR, W_A), lambda i, h: (i, 0))
    hspec = pl.BlockSpec((SR, DH_B), lambda i, h: (i, h))
    return pl.pallas_call(
        _even_sample_kernel,
        grid=(nb, H_B),
        in_specs=[aspec, aspec, aspec, aspec,
                  hspec, hspec, hspec, hspec,
                  pl.BlockSpec((SR, W_A), lambda i, h: (i, 0)),
                  pl.BlockSpec((SB, 1, DH_B, DH_B), lambda i, h: (i, h, 0, 0)),
                  pl.BlockSpec((CONV_W, W_A), const2),
                  pl.BlockSpec((1, DH_B), lambda i, h: (0, h)),
                  pl.BlockSpec((SR, DH_B), const2),
                  pl.BlockSpec((SR, DH_B), const2),
                  pl.BlockSpec((1, 1, 128), lambda i, h: (h, 0, 0))],
        out_specs=(pl.BlockSpec((SR, W_A), lambda i, h: (i, 0)),
                   pl.BlockSpec((SR, W_A), lambda i, h: (i, 0)),
                   pl.BlockSpec((SR, DH_B), lambda i, h: (i, h)),
                   pl.BlockSpec((SB, 1, DH_B, DH_B), lambda i, h: (i, h, 0, 0))),
        out_shape=(jax.ShapeDtypeStruct((DEC_BATCH * DEC_SEQ, W_A), BF16),
                   jax.ShapeDtypeStruct((DEC_BATCH * DEC_SEQ, W_A), F32),
                   jax.ShapeDtypeStruct((DEC_BATCH * DEC_SEQ, W_B), BF16),
                   jax.ShapeDtypeStruct((DEC_BATCH, H_B, DH_B, DH_B), F32)),
        scratch_shapes=[pltpu.VMEM((SR, DH_B), F32)],
        compiler_params=_params(("arbitrary", "arbitrary")),
        name="even_sample",
    )(*pa, *pb, st_exp, s_state, conv_w, g_ret, cosf, sins, lg_tab)


def _odd_prompt_kernel(p_ref, h_ref, wg_ref, bg_ref, gm_ref, lng_ref, lnb_ref, ws_ref, bst_ref,
                       yc_ref, yd_ref, c_ref, n_ref, m_ref, wgb, wsb):
    L = CHUNK

    @pl.when(jnp.logical_and(pl.program_id(0) == 0, pl.program_id(1) == 0))
    def _():
        wgb[...] = wg_ref[...].astype(BF16)
        keep = (lax.broadcasted_iota(jnp.int32, (L, L), 0) >= lax.broadcasted_iota(jnp.int32, (L, L), 1))
        for g in range(G_D):
            wsb[g] = jnp.where(keep, ws_ref[g], 0.0).astype(BF16)

    @pl.when(pl.program_id(1) == 0)
    def _():
        c_ref[...] = jnp.zeros_like(c_ref)
        n_ref[...] = jnp.zeros_like(n_ref)
        m_ref[...] = jnp.zeros_like(m_ref)

    def chunk(ci, carry):
        rs = pl.ds(pl.multiple_of(ci * L, L), L)
        row = lax.broadcasted_iota(jnp.int32, (L, L), 0)
        col = lax.broadcasted_iota(jnp.int32, (L, L), 1)
        tri = row >= col

        pre = _dot_nt(h_ref[rs, :], wgb[...]) + bg_ref[...]
        lf = _log_sigmoid(pre)
        b_c = _dot_hi(jnp.where(tri, 1.0, 0.0), lf)
        b_r = b_c.T
        pre_r = pre.T
        for h in range(H_C):
            bc = b_c[:, H_C + h:H_C + h + 1]
            br = b_r[H_C + h:H_C + h + 1, :]
            igr = pre_r[h:h + 1, :]
            igc = pre[:, h:h + 1]
            m_prev = m_ref[0, h:h + 1, 0:1]
            log_d = jnp.where(tri, bc - br + igr, NEG_INF)
            log_inter = bc + m_prev
            m_t = jnp.maximum(log_inter, jnp.max(log_d, axis=-1, keepdims=True))
            w = jnp.exp(log_d - m_t)
            w_inter = jnp.exp(log_inter - m_t)
            q = p_ref[rs, h * DQK_C:(h + 1) * DQK_C] * (DQK_C ** -0.5)
            k = p_ref[rs, H_C * DQK_C + h * DQK_C:H_C * DQK_C + (h + 1) * DQK_C]
            v = p_ref[rs, 2 * H_C * DQK_C + h * DV_C:2 * H_C * DQK_C + (h + 1) * DV_C]
            z = p_ref[rs, O_GATE - W_C + h * DV_C:O_GATE - W_C + (h + 1) * DV_C]
            qb = q.astype(BF16)
            kb = k.astype(BF16)
            vb = v.astype(BF16)
            sc = _dot_nt(qb, kb) * w
            c_old = c_ref[0, h]
            n_old = n_ref[0, h:h + 1, :]
            num = _dot(sc.astype(BF16), vb) + w_inter * _dot_nt(qb, c_old.astype(BF16))
            den = jnp.sum(sc, axis=-1, keepdims=True) + w_inter * jnp.sum(q * n_old, axis=-1, keepdims=True)
            hh = num / jnp.maximum(jnp.abs(den), jnp.exp(-m_t))
            m_new = m_t[L - 1:L, :]
            b_last = bc[L - 1:L, :]
            w_end = jnp.exp(b_last - bc + igc - m_new)
            cd = jnp.exp(b_last + m_prev - m_new)
            c_ref[0, h] = cd * c_old + _dot_tn((v * w_end).astype(BF16), kb)
            n_ref[0, h:h + 1, :] = cd * n_old + jnp.sum(w_end * k, axis=0, keepdims=True)
            m_ref[0, h:h + 1, :] = jnp.broadcast_to(m_new, (1, 128))
            sl = slice(h * DV_C, (h + 1) * DV_C)
            yc_ref[rs, sl] = (_head_norm(hh, gm_ref[0:1, sl]) * _silu(z)).astype(BF16)

        dv = lambda g: p_ref[rs, O_GATE + W_D + g * 128:O_GATE + W_D + (g + 1) * 128]
        tot = dv(0)
        for g in range(1, G_D):
            tot = tot + dv(g)
        mu = jnp.sum(tot, axis=-1, keepdims=True) * (1.0 / W_D)
        sq = (dv(0) - mu) * (dv(0) - mu)
        for g in range(1, G_D):
            sq = sq + (dv(g) - mu) * (dv(g) - mu)
        rstd = lax.rsqrt(jnp.sum(sq, axis=-1, keepdims=True) * (1.0 / W_D) + EPS)
        for g in range(G_D):
            sl = slice(g * 128, (g + 1) * 128)
            vn = (dv(g) - mu) * rstd * lng_ref[0:1, sl] + lnb_ref[0:1, sl]
            s = _dot(wsb[g], vn.astype(BF16)) + bst_ref[:, g:g + 1]
            d_u = p_ref[rs, O_GATE + g * 128:O_GATE + (g + 1) * 128]
            d_z = p_ref[rs, O_GATE + 2 * W_D + g * 128:O_GATE + 2 * W_D + (g + 1) * 128]
            yd_ref[rs, sl] = (d_u * s * _silu(d_z)).astype(BF16)
        return carry

    lax.fori_loop(0, MIX_ROWS // L, chunk, 0)


def _odd_prompt(p, h, w_o, bg, gm, lng, lnb, ws, bst):
    nc = SEQ // MIX_ROWS
    rows = lambda b, c: (b * nc + c, 0)
    const2 = lambda b, c: (0, 0)
    return pl.pallas_call(
        _odd_prompt_kernel,
        grid=(BATCH, nc),
        in_specs=[pl.BlockSpec((MIX_ROWS, O_N), rows),
                  pl.BlockSpec((MIX_ROWS, D_MODEL), rows),
                  pl.BlockSpec((128, D_MODEL), lambda b, c: (O_GATE // 128, 0)),
                  pl.BlockSpec((1, 128), const2),
                  pl.BlockSpec((1, W_C), const2),
                  pl.BlockSpec((1, W_D), const2),
                  pl.BlockSpec((1, W_D), const2),
                  pl.BlockSpec((G_D, CHUNK, CHUNK), lambda b, c: (0, 0, 0)),
                  pl.BlockSpec((CHUNK, G_D), const2)],
        out_specs=(pl.BlockSpec((MIX_ROWS, W_C), rows),
                   pl.BlockSpec((MIX_ROWS, W_D), rows),
                   pl.BlockSpec((1, H_C, DV_C, DQK_C), lambda b, c: (b, 0, 0, 0)),
                   pl.BlockSpec((1, H_C, DQK_C), lambda b, c: (b, 0, 0)),
                   pl.BlockSpec((1, 8, 128), lambda b, c: (b, 0, 0))),
        out_shape=(jax.ShapeDtypeStruct((BATCH * SEQ, W_C), BF16),
                   jax.ShapeDtypeStruct((BATCH * SEQ, W_D), BF16),
                   jax.ShapeDtypeStruct((BATCH, H_C, DV_C, DQK_C), F32),
                   jax.ShapeDtypeStruct((BATCH, H_C, DQK_C), F32),
                   jax.ShapeDtypeStruct((BATCH, 8, 128), F32)),
        scratch_shapes=[pltpu.VMEM((128, D_MODEL), BF16),
                        pltpu.VMEM((G_D, CHUNK, CHUNK), BF16)],
        compiler_params=_params(("arbitrary", "arbitrary")),
        name="odd_prompt",
    )(p, h, w_o, bg, gm, lng, lnb, ws, bst)


CG = 2
CW = 128 + CG * (2 * DQK_C + 2 * DV_C)
CT = 1024
CN = BATCH * SEQ // CT
CPB = SEQ // CT


def _odd_heads_kernel(hp_ref, hs_ref, wq_ref, wk_ref, wv_ref, wz_ref, wg_ref, bg_ref, gm_ref,
                      yc_ref, c_ref, n_ref, m_ref, sq_ref, sk_ref, sv_ref, sz_ref,
                      wb, pt, c_scr, n_scr, m_scr, gt_scr):
    grp = pl.program_id(0)
    s = pl.program_id(1)
    L = CHUNK
    gc = slice(0, 128)
    qc = slice(128, 128 + CG * DQK_C)
    kc = slice(qc.stop, qc.stop + CG * DQK_C)
    vc = slice(kc.stop, kc.stop + CG * DV_C)
    zc = slice(vc.stop, vc.stop + CG * DV_C)
    head = lambda sl, i, w: slice(sl.start + i * w, sl.start + (i + 1) * w)

    @pl.when(s == 0)
    def _():
        wb[qc, :] = wq_ref[...].astype(BF16)
        wb[kc, :] = wk_ref[...].astype(BF16)
        wb[vc, :] = wv_ref[...].astype(BF16)
        wb[zc, :] = wz_ref[...].astype(BF16)
        wb[gc, :] = wg_ref[...].astype(BF16)
        ps = _dot_nt(hs_ref[...], wb[qc.start:CW, :])
        off = lambda sl: slice(sl.start - qc.start, sl.stop - qc.start)
        sq_ref[...] = ps[:, off(qc)]
        sk_ref[...] = ps[:, off(kc)]
        sv_ref[...] = ps[:, off(vc)]
        sz_ref[...] = ps[:, off(zc)]

    @pl.when(s > 0)
    def _():
        t = s - 1
        n = CT // L
        rows = lambda c: slice(c * L, (c + 1) * L)

        def piece(k, j):
            pr = slice(k * PCH * L, (k + 1) * PCH * L)
            pc = (slice(0, vc.start), slice(vc.start, CW))[j]
            pt[pr, pc] = _dot_nt(hp_ref[pr, :], wb[pc, :])

        row = lax.broadcasted_iota(jnp.int32, (L, L), 0)
        col = lax.broadcasted_iota(jnp.int32, (L, L), 1)
        tri = row >= col
        fresh = t % CPB == 0
        cst = {0: [jnp.where(fresh, 0.0, c_scr[i]) for i in range(CG)]}
        nst = {0: [jnp.where(fresh, 0.0, n_scr[i]) for i in range(CG)]}
        mst = {0: [jnp.where(fresh, 0.0, m_scr[i, 0:1, 0:1]) for i in range(CG)]}
        v = {}
        gates = {}

        def gate(k):
            cs = range(k * PCH, (k + 1) * PCH)
            for c in cs:
                gt_scr[c] = (pt[rows(c), gc] + bg_ref[...]).T
            pad = jnp.zeros((8 - CG * PCH, L), F32)
            ig_rows = jnp.concatenate(
                [gt_scr[c, pl.ds(grp * CG + i, 1), :] for i in range(CG) for c in cs] + [pad], axis=0)
            lf_rows = jnp.concatenate(
                [_log_sigmoid(gt_scr[c, pl.ds(grp * CG + i + H_C, 1), :]) for i in range(CG) for c in cs]
                + [pad], axis=0)
            b_rows = _dot_hi(lf_rows, jnp.where(row <= col, 1.0, 0.0))
            tall = jnp.zeros((L - 8, L), F32)
            gates[k] = dict(ig_rows=ig_rows, b_rows=b_rows,
                            b_cols=jnp.concatenate([b_rows, tall], axis=0).T,
                            ig_cols=jnp.concatenate([ig_rows, tall], axis=0).T)

        def stage_a(c):
            gk = gates[c // PCH]
            v[c] = []
            nst[c + 1] = []
            mst[c + 1] = []
            for i in range(CG):
                r = i * PCH + c % PCH
                b_r = gk["b_rows"][r:r + 1, :]
                ig_r = gk["ig_rows"][r:r + 1, :]
                b_c = gk["b_cols"][:, r:r + 1]
                ig_c = gk["ig_cols"][:, r:r + 1]
                m_prev = mst[c][i]
                log_d = jnp.where(tri, b_c - b_r + ig_r, NEG_INF)
                log_inter = b_c + m_prev
                m_t = jnp.maximum(log_inter, jnp.max(log_d, axis=-1, keepdims=True))
                m_new = m_t[L - 1:L, :]
                b_last = b_c[L - 1:L, :]
                w_end = jnp.exp(b_last - b_c + ig_c - m_new)
                cd = jnp.exp(b_last + m_prev - m_new)
                q = pt[rows(c), head(qc, i, DQK_C)] * (DQK_C ** -0.5)
                k = pt[rows(c), head(kc, i, DQK_C)]
                vv = pt[rows(c), head(vc, i, DV_C)]
                nst[c + 1].append(cd * nst[c][i] + jnp.sum(w_end * k, axis=0, keepdims=True))
                mst[c + 1].append(m_new)
                v[c].append(dict(w=jnp.exp(log_d - m_t), w_inter=jnp.exp(log_inter - m_t),
                                 floor=jnp.exp(-m_t), cd=cd, qb=q.astype(BF16), kb=k.astype(BF16),
                                 vb=vv.astype(BF16), vw=(vv * w_end).astype(BF16),
                                 qn=jnp.sum(q * nst[c][i], axis=-1, keepdims=True)))

        def stage_b(c):
            for i, d in enumerate(v[c]):
                d["sc"] = _dot_nt(d["qb"], d["kb"])
                d["upd"] = _dot_tn(d["vw"], d["kb"])
            for i, d in enumerate(v[c]):
                d["inter"] = _dot_nt(d["qb"], cst[c][i].astype(BF16))

        def stage_c(c):
            cst[c + 1] = []
            for i, d in enumerate(v[c]):
                sc = d["sc"] * d["w"]
                d["den"] = jnp.sum(sc, axis=-1, keepdims=True) + d["w_inter"] * d["qn"]
                d["sc"] = sc.astype(BF16)
                cst[c + 1].append(d["cd"] * cst[c][i] + d["upd"])

        def stage_d(c):
            for d in v[c]:
                d["num"] = _dot(d["sc"], d["vb"])

        def stage_e(c):
            for i, d in enumerate(v.pop(c)):
                num = d["num"] + d["w_inter"] * d["inter"]
                hh = num / jnp.maximum(jnp.abs(d["den"]), d["floor"])
                z = pt[rows(c), head(zc, i, DV_C)]
                ys = slice(i * DV_C, (i + 1) * DV_C)
                yc_ref[rows(c), ys] = (_head_norm(hh, gm_ref[0:1, ys]) * _silu(z)).astype(BF16)

        _chunk_pipeline(n, piece, (stage_a, stage_b, stage_c, stage_d, stage_e), gate)
        for i in range(CG):
            c_scr[i] = cst[n][i]
            n_scr[i] = nst[n][i]
            m_scr[i] = jnp.broadcast_to(mst[n][i], (8, 128))
            c_ref[0, i] = cst[n][i]
            n_ref[0, i] = nst[n][i]
            m_ref[0, i] = jnp.broadcast_to(mst[n][i], (1, 128))


def _odd_heads(hp, hs, w_t, bg, gm):
    k = hp.shape[1]
    ms = hs.shape[0]
    tile = lambda s: jnp.maximum(s - 1, 0)
    qw, vw = CG * DQK_C, CG * DV_C
    koff = H_C * DQK_C // qw
    voff = 2 * H_C * DQK_C // vw
    zoff = (2 * H_C * DQK_C + W_C) // vw
    seq = lambda g, s: (tile(s) // CPB, g, 0, 0)
    once = pl.Buffered(1)
    return pl.pallas_call(
        _odd_heads_kernel,
        grid=(H_C // CG, CN + 1),
        in_specs=[pl.BlockSpec((CT, k), lambda g, s: (tile(s), 0)),
                  pl.BlockSpec((ms, k), lambda g, s: (0, 0), pipeline_mode=once),
                  pl.BlockSpec((qw, k), lambda g, s: (g, 0), pipeline_mode=once),
                  pl.BlockSpec((qw, k), lambda g, s: (koff + g, 0), pipeline_mode=once),
                  pl.BlockSpec((vw, k), lambda g, s: (voff + g, 0), pipeline_mode=once),
                  pl.BlockSpec((vw, k), lambda g, s: (zoff + g, 0), pipeline_mode=once),
                  pl.BlockSpec((128, k), lambda g, s: (O_GATE // 128, 0), pipeline_mode=once),
                  pl.BlockSpec((1, 128), lambda g, s: (0, 0)),
                  pl.BlockSpec((1, vw), lambda g, s: (0, g))],
        out_specs=(pl.BlockSpec((CT, vw), lambda g, s: (tile(s), g)),
                   pl.BlockSpec((1, CG, DV_C, DQK_C), seq),
                   pl.BlockSpec((1, CG, 1, DQK_C), seq),
                   pl.BlockSpec((1, CG, 1, 128), seq),
                   pl.BlockSpec((ms, qw), lambda g, s: (0, g)),
                   pl.BlockSpec((ms, qw), lambda g, s: (0, g)),
                   pl.BlockSpec((ms, vw), lambda g, s: (0, g)),
                   pl.BlockSpec((ms, vw), lambda g, s: (0, g))),
        out_shape=(jax.ShapeDtypeStruct((BATCH * SEQ, W_C), BF16),
                   jax.ShapeDtypeStruct((BATCH, H_C, DV_C, DQK_C), F32),
                   jax.ShapeDtypeStruct((BATCH, H_C, 1, DQK_C), F32),
                   jax.ShapeDtypeStruct((BATCH, H_C, 1, 128), F32),
                   jax.ShapeDtypeStruct((ms, H_C * DQK_C), F32),
                   jax.ShapeDtypeStruct((ms, H_C * DQK_C), F32),
                   jax.ShapeDtypeStruct((ms, W_C), F32),
                   jax.ShapeDtypeStruct((ms, W_C), F32)),
        scratch_shapes=[pltpu.VMEM((CW, k), BF16),
                        pltpu.VMEM((CT, CW), F32),
                        pltpu.VMEM((CG, DV_C, DQK_C), F32),
                        pltpu.VMEM((CG, 1, DQK_C), F32),
                        pltpu.VMEM((CG, 8, 128), F32),
                        pltpu.VMEM((CT // CHUNK, CHUNK, CHUNK), F32)],
        compiler_params=_params(("arbitrary", "arbitrary")),
        name="odd_heads",
    )(hp, hs, w_t, w_t, w_t, w_t, w_t, bg, gm)


def _odd_mlp_kernel(p_ref, lng_ref, lnb_ref, ws_ref, bst_ref, yd_ref, wsb):
    L = CHUNK

    @pl.when(pl.program_id(0) == 0)
    def _():
        keep = (lax.broadcasted_iota(jnp.int32, (L, L), 0) >= lax.broadcasted_iota(jnp.int32, (L, L), 1))
        for g in range(G_D):
            wsb[g] = jnp.where(keep, ws_ref[g], 0.0).astype(BF16)

    def chunk(ci, carry):
        rs = pl.ds(pl.multiple_of(ci * L, L), L)
        dv = lambda g: p_ref[rs, W_D + g * 128:W_D + (g + 1) * 128]
        tot = dv(0)
        for g in range(1, G_D):
            tot = tot + dv(g)
        mu = jnp.sum(tot, axis=-1, keepdims=True) * (1.0 / W_D)
        sq = (dv(0) - mu) * (dv(0) - mu)
        for g in range(1, G_D):
            sq = sq + (dv(g) - mu) * (dv(g) - mu)
        rstd = lax.rsqrt(jnp.sum(sq, axis=-1, keepdims=True) * (1.0 / W_D) + EPS)
        for g in range(G_D):
            sl = slice(g * 128, (g + 1) * 128)
            vn = (dv(g) - mu) * rstd * lng_ref[0:1, sl] + lnb_ref[0:1, sl]
            s = _dot(wsb[g], vn.astype(BF16)) + bst_ref[:, g:g + 1]
            d_u = p_ref[rs, g * 128:(g + 1) * 128]
            d_z = p_ref[rs, 2 * W_D + g * 128:2 * W_D + (g + 1) * 128]
            yd_ref[rs, sl] = (d_u * s * _silu(d_z)).astype(BF16)
        return carry

    lax.fori_loop(0, p_ref.shape[0] // L, chunk, 0)


def _odd_mlp(p, lng, lnb, ws, bst, rows_per_step):
    m = p.shape[0]
    rows = lambda i: (i, 0)
    const2 = lambda i: (0, 0)
    return pl.pallas_call(
        _odd_mlp_kernel,
        grid=(m // rows_per_step,),
        in_specs=[pl.BlockSpec((rows_per_step, 3 * W_D), rows),
                  pl.BlockSpec((1, W_D), const2),
                  pl.BlockSpec((1, W_D), const2),
                  pl.BlockSpec((G_D, CHUNK, CHUNK), lambda i: (0, 0, 0)),
                  pl.BlockSpec((CHUNK, G_D), const2)],
        out_specs=pl.BlockSpec((rows_per_step, W_D), rows),
        out_shape=jax.ShapeDtypeStruct((m, W_D), BF16),
        scratch_shapes=[pltpu.VMEM((G_D, CHUNK, CHUNK), BF16)],
        compiler_params=_params(("arbitrary",)),
        name="odd_mlp",
    )(p, lng, lnb, ws, bst)


DT = 512


def _odd_mlp_fused_kernel(hp_ref, w0_ref, w1_ref, w2_ref, wt_ref, lng_ref, lnb_ref, ws_ref, bst_ref,
                          yd_ref, wb, pt, wsb):
    s = pl.program_id(0)
    L = CHUNK
    uc = slice(0, W_D)
    vc = slice(W_D, 2 * W_D)
    zc = slice(2 * W_D, 3 * W_D)

    @pl.when(s == 0)
    def _():
        sh = N_GATE
        wb[uc, :] = jnp.concatenate([w0_ref[sh:, :], w1_ref[0:sh, :]], axis=0).astype(BF16)
        wb[vc, :] = jnp.concatenate([w1_ref[sh:, :], w2_ref[0:sh, :]], axis=0).astype(BF16)
        wb[zc, :] = jnp.concatenate([w2_ref[sh:, :], wt_ref[...]], axis=0).astype(BF16)
        keep = (lax.broadcasted_iota(jnp.int32, (L, L), 0) >= lax.broadcasted_iota(jnp.int32, (L, L), 1))
        for g in range(G_D):
            wsb[g] = jnp.where(keep, ws_ref[g], 0.0).astype(BF16)

    @pl.when(s > 0)
    def _():
        n = DT // L
        rows = lambda c: slice(c * L, (c + 1) * L)
        grp = lambda sl, g: slice(sl.start + g * 128, sl.start + (g + 1) * 128)
        vn = {}
        mix = {}

        def project(pc):
            pt[:, pc] = _dot_nt(hp_ref[...], wb[pc, :])

        def stage_a(c):
            dv = lambda g: pt[rows(c), grp(vc, g)]
            tot = dv(0)
            for g in range(1, G_D):
                tot = tot + dv(g)
            mu = jnp.sum(tot, axis=-1, keepdims=True) * (1.0 / W_D)
            sq = (dv(0) - mu) * (dv(0) - mu)
            for g in range(1, G_D):
                sq = sq + (dv(g) - mu) * (dv(g) - mu)
            rstd = lax.rsqrt(jnp.sum(sq, axis=-1, keepdims=True) * (1.0 / W_D) + EPS)
            vn[c] = [((dv(g) - mu) * rstd * lng_ref[0:1, g * 128:(g + 1) * 128]
                      + lnb_ref[0:1, g * 128:(g + 1) * 128]).astype(BF16) for g in range(G_D)]

        def stage_b(c):
            mix[c] = [_dot(wsb[g], vn[c][g]) for g in range(G_D)]

        def stage_e(c):
            for g in range(G_D):
                sg = mix[c][g] + bst_ref[:, g:g + 1]
                d_u = pt[rows(c), grp(uc, g)]
                d_z = pt[rows(c), grp(zc, g)]
                yd_ref[rows(c), g * 128:(g + 1) * 128] = (d_u * sg * _silu(d_z)).astype(BF16)

        project(vc)
        for c in range(n):
            stage_a(c)
        project(uc)
        for c in range(n):
            stage_b(c)
        project(zc)
        for c in range(n):
            stage_e(c)


def _odd_mlp_fused(hp, w_t, lng, lnb, ws, bst):
    k = hp.shape[1]
    m = hp.shape[0]
    tile = lambda s: (jnp.maximum(s - 1, 0), 0)
    const2 = lambda s: (0, 0)
    t0 = O_GATE // IN_TN
    once = pl.Buffered(1)
    wspec = lambda j: pl.BlockSpec((IN_TN, k), lambda s: (t0 + j, 0), pipeline_mode=once)
    return pl.pallas_call(
        _odd_mlp_fused_kernel,
        grid=(m // DT + 1,),
        in_specs=[pl.BlockSpec((DT, k), tile),
                  wspec(0), wspec(1), wspec(2),
                  pl.BlockSpec((N_GATE, k), lambda s: ((t0 + 3) * (IN_TN // N_GATE), 0), pipeline_mode=once),
                  pl.BlockSpec((1, W_D), const2),
                  pl.BlockSpec((1, W_D), const2),
                  pl.BlockSpec((G_D, CHUNK, CHUNK), lambda s: (0, 0, 0)),
                  pl.BlockSpec((CHUNK, G_D), const2)],
        out_specs=pl.BlockSpec((DT, W_D), tile),
        out_shape=jax.ShapeDtypeStruct((m, W_D), BF16),
        scratch_shapes=[pltpu.VMEM((3 * W_D, k), BF16),
                        pltpu.VMEM((DT, 3 * W_D), F32),
                        pltpu.VMEM((G_D, CHUNK, CHUNK), BF16)],
        compiler_params=_params(("arbitrary",)),
        name="odd_mlp_fused",
    )(hp, w_t, w_t, w_t, w_t, lng, lnb, ws, bst)


def _in_proj_rows_kernel(h_ref, w_ref, wn_ref, o_ref, *, shift):
    wsh = jnp.concatenate([w_ref[shift:, :], wn_ref[...]], axis=0)
    o_ref[...] = _dot_nt(h_ref[...], wsh.astype(BF16))


def _in_proj_rows(h, w_t, n_out, shift, tile0):
    ms, k = h.shape
    return pl.pallas_call(
        functools.partial(_in_proj_rows_kernel, shift=shift),
        grid=(n_out // IN_TN,),
        in_specs=[pl.BlockSpec((ms, k), lambda j: (0, 0)),
                  pl.BlockSpec((IN_TN, k), lambda j: (j + tile0, 0)),
                  pl.BlockSpec((shift, k), lambda j: ((j + tile0 + 1) * (IN_TN // shift), 0))],
        out_specs=pl.BlockSpec((ms, IN_TN), lambda j: (0, j)),
        out_shape=jax.ShapeDtypeStruct((ms, n_out), F32),
        compiler_params=_params(("arbitrary",)),
        name="in_proj_rows",
    )(h, w_t, w_t)


def _odd_sample_kernel(pq_ref, pk_ref, pv_ref, pz_ref, h_ref, wg_ref, pd_ref,
                       c_ref, nrow_ref, mrow_ref, bg_ref, gm_ref, lng_ref, lnb_ref,
                       rtab_ref, btab_ref,
                       yc_ref, yd_ref, vn_ref, co_ref, no_ref, mo_ref,
                       inter_scr):
    h = pl.program_id(1)
    row = lax.broadcasted_iota(jnp.int32, (SR, SR), 0)
    col = lax.broadcasted_iota(jnp.int32, (SR, SR), 1)
    trow = row & 3

    @pl.when(h == 0)
    def _():
        dv = pd_ref[:, W_D:2 * W_D]
        mu = jnp.mean(dv, axis=-1, keepdims=True)
        xc = dv - mu
        var = jnp.mean(xc * xc, axis=-1, keepdims=True)
        rstd = lax.rsqrt(var + EPS)
        for g in range(G_D):
            sl = slice(g * 128, (g + 1) * 128)
            vn = xc[:, sl] * rstd * lng_ref[0:1, sl] + lnb_ref[0:1, sl]
            vn_ref[:, sl] = vn
            s = rtab_ref[0, :, sl] * vn + btab_ref[:, sl]
            for j in range(1, DEC_SEQ):
                s = s + jnp.where(trow >= j, rtab_ref[j, :, sl] * pltpu.roll(vn, j, 0), 0.0)
            d_u = pd_ref[:, g * 128:(g + 1) * 128]
            d_z = pd_ref[:, 2 * W_D + g * 128:2 * W_D + (g + 1) * 128]
            yd_ref[:, sl] = (d_u * s * _silu(d_z)).astype(BF16)

    same = (row >> 2) == (col >> 2)
    mask = jnp.where(same, trow - (col & 3), -1) >= 0
    pre = _dot_nt(h_ref[...], wg_ref[...].astype(BF16)) + bg_ref[...]
    lf = _log_sigmoid(pre)
    b_full = _dot_hi(jnp.where(mask, 1.0, 0.0), lf)
    sel_i = col == h
    sel_f = col == h + H_C
    ig_c = jnp.sum(jnp.where(sel_i, pre, 0.0), axis=-1, keepdims=True)
    b_c = jnp.sum(jnp.where(sel_f, b_full, 0.0), axis=-1, keepdims=True)
    sel_ir = row == h
    sel_fr = row == h + H_C
    ig_r = jnp.sum(jnp.where(sel_ir, pre.T, 0.0), axis=0, keepdims=True)
    b_r = jnp.sum(jnp.where(sel_fr, b_full.T, 0.0), axis=0, keepdims=True)
    m_prev = mrow_ref[0]
    log_d = jnp.where(mask, b_c - b_r + ig_r, NEG_INF)
    log_inter = b_c + m_prev
    m_t = jnp.maximum(log_inter, jnp.max(log_d, axis=-1, keepdims=True))
    w = jnp.exp(log_d - m_t)
    w_inter = jnp.exp(log_inter - m_t)
    q = pq_ref[...] * (DQK_C ** -0.5)
    k = pk_ref[...]
    v = pv_ref[...]
    qb = q.astype(BF16)
    kb = k.astype(BF16)
    vb = v.astype(BF16)
    sc = _dot_nt(qb, kb) * w
    sub = lax.broadcasted_iota(jnp.int32, (8, DV_C), 0)
    sub8 = lax.broadcasted_iota(jnp.int32, (8, DQK_C), 0)
    for g in range(SR // 8):
        q8 = q[8 * g:8 * g + 8, :]
        q2 = jnp.concatenate([jnp.where(sub8 < DEC_SEQ, q8, 0.0), jnp.where(sub8 < DEC_SEQ, 0.0, q8)], axis=1)
        c_pair = jnp.concatenate([c_ref[2 * g + beta, 0].astype(BF16) for beta in range(2)], axis=1)
        inter_scr[8 * g:8 * g + 8, :] = _dot_nt(q2.astype(BF16), c_pair)
    n_rows = nrow_ref[0]
    num = _dot(sc.astype(BF16), vb) + w_inter * inter_scr[...]
    den = jnp.sum(sc, axis=-1, keepdims=True) + w_inter * jnp.sum(q * n_rows, axis=-1, keepdims=True)
    hh = num / jnp.maximum(jnp.abs(den), jnp.exp(-m_t))
    yc_ref[...] = (_head_norm(hh, gm_ref[...]) * _silu(pz_ref[...])).astype(BF16)

    stats = jnp.where(col == 0, m_t, jnp.where(col == 1, b_c, 0.0))
    last = _dot_hi(jnp.where(col == (row | 3), 1.0, 0.0), stats)
    m_new = last[:, 0:1]
    b_last = last[:, 1:2]
    w_end = jnp.exp(b_last - b_c + ig_c - m_new)
    cd = jnp.exp(b_last + m_prev - m_new)
    mo_ref[0] = m_new
    no_ref[0] = cd * n_rows + _dot_hi(jnp.where(same, 1.0, 0.0), w_end * k)
    vwt = (v * w_end).T
    lane_b = lax.broadcasted_iota(jnp.int32, (DV_C, SR), 1) >> 2
    for b in range(SB):
        lhs = jnp.where(lane_b == b, vwt, 0.0).astype(BF16)
        cd_b = cd[4 * b + 3:4 * b + 4, :]
        co_ref[b, 0] = cd_b * c_ref[b, 0] + _dot(lhs, kb)


def _odd_sample(pc, pd, h, w_o, c_state, n_rows, m_rows, bg, gm, lng, lnb, rtab, btab):
    nb = DEC_BATCH // SB
    const2 = lambda i, h: (0, 0)
    return pl.pallas_call(
        _odd_sample_kernel,
        grid=(nb, H_C),
        in_specs=[pl.BlockSpec((SR, DQK_C), lambda i, h: (i, h)),
                  pl.BlockSpec((SR, DQK_C), lambda i, h: (i, h)),
                  pl.BlockSpec((SR, DV_C), lambda i, h: (i, h)),
                  pl.BlockSpec((SR, DV_C), lambda i, h: (i, h)),
                  pl.BlockSpec((SR, D_MODEL), lambda i, h: (i, 0)),
                  pl.BlockSpec((128, D_MODEL), lambda i, h: (O_GATE // 128, 0)),
                  pl.BlockSpec((SR, 3 * W_D), lambda i, h: (i, 0)),
                  pl.BlockSpec((SB, 1, DV_C, DQK_C), lambda i, h: (i, h, 0, 0)),
                  pl.BlockSpec((1, SR, DQK_C), lambda i, h: (h, i, 0)),
                  pl.BlockSpec((1, SR, 1), lambda i, h: (h, i, 0)),
                  pl.BlockSpec((1, 128), const2),
                  pl.BlockSpec((1, DV_C), lambda i, h: (0, h)),
                  pl.BlockSpec((1, W_D), const2),
                  pl.BlockSpec((1, W_D), const2),
                  pl.BlockSpec((DEC_SEQ, SR, W_D), lambda i, h: (0, 0, 0)),
                  pl.BlockSpec((SR, W_D), const2)],
        out_specs=(pl.BlockSpec((SR, DV_C), lambda i, h: (i, h)),
                   pl.BlockSpec((SR, W_D), lambda i, h: (i, 0)),
                   pl.BlockSpec((SR, W_D), lambda i, h: (i, 0)),
                   pl.BlockSpec((SB, 1, DV_C, DQK_C), lambda i, h: (i, h, 0, 0)),
                   pl.BlockSpec((1, SR, DQK_C), lambda i, h: (h, i, 0)),
                   pl.BlockSpec((1, SR, 1), lambda i, h: (h, i, 0))),
        out_shape=(jax.ShapeDtypeStruct((DEC_BATCH * DEC_SEQ, W_C), BF16),
                   jax.ShapeDtypeStruct((DEC_BATCH * DEC_SEQ, W_D), BF16),
                   jax.ShapeDtypeStruct((DEC_BATCH * DEC_SEQ, W_D), F32),
                   jax.ShapeDtypeStruct((DEC_BATCH, H_C, DV_C, DQK_C), F32),
                   jax.ShapeDtypeStruct((H_C, DEC_BATCH * DEC_SEQ, DQK_C), F32),
                   jax.ShapeDtypeStruct((H_C, DEC_BATCH * DEC_SEQ, 1), F32)),
        scratch_shapes=[pltpu.VMEM((SR, DV_C), F32)],
        compiler_params=_params(("arbitrary", "arbitrary")),
        name="odd_sample",
    )(*pc, h, w_o, pd, c_state, n_rows, m_rows, bg, gm, lng, lnb, rtab, btab)


def _rope_tables(pos):
    inv = ROPE_BASE ** (-jnp.arange(0, DH_B, 2, dtype=F32) / DH_B)
    ang = pos.astype(F32)[:, None] * inv[None, :]
    cos = jnp.cos(ang)
    sin = jnp.sin(ang)
    return jnp.concatenate([cos, cos], axis=-1), jnp.concatenate([-sin, sin], axis=-1)


def kernel(x_prompt, x_sample, state_conv, state_ret, state_mlstm_C, state_mlstm_n, state_mlstm_m,
           norm_even, w_in_even, conv_w, ret_norm, w_out_even,
           norm_odd, w_in_odd, b_gate_odd, mlstm_norm, ln_v_g, ln_v_b,
           w_spatial, b_spatial, w_out_odd, norm_final):
    w_in_e = w_in_even[0]
    w_out_e = w_out_even[0].astype(BF16)
    w_o = w_in_odd[0].T
    w_out_o = w_out_odd[0].astype(BF16)
    g_even = norm_even[0][None, :]
    g_odd = norm_odd[0][None, :]
    g_fin = norm_final[None, :]
    cw = conv_w[0]
    g_ret = ret_norm[0][None, :]
    bg = jnp.concatenate([b_gate_odd[0], jnp.zeros((128 - 2 * H_C,), F32)])[None, :]
    gm = mlstm_norm[0][None, :]
    lng = ln_v_g[0][None, :]
    lnb = ln_v_b[0][None, :]
    ws = w_spatial[0]
    bst = b_spatial[0].T

    cos_p, sin_p = _rope_tables(jnp.arange(SEQ, dtype=jnp.int32))
    cos_s, sin_s = _rope_tables(PAST_LEN + jnp.arange(DEC_SEQ, dtype=jnp.int32))
    cos_s = jnp.tile(cos_s, (SB, 1))
    sin_s = jnp.tile(sin_s, (SB, 1))
    lg_tab = jnp.broadcast_to(jnp.asarray(LOG_GAMMA, F32)[:, None, None], (H_B, 1, 128))

    ws4 = ws[:, :DEC_SEQ, :DEC_SEQ]
    t_idx = jnp.arange(DEC_SEQ)
    rtab = []
    for j in range(DEC_SEQ):
        coef = ws4[:, t_idx, (t_idx - j) % DEC_SEQ]
        tab = jnp.repeat(coef.T[:, :, None], 128, axis=2).reshape(DEC_SEQ, W_D)
        rtab.append(jnp.tile(tab, (SB, 1)))
    rtab = jnp.stack(rtab)
    btab = jnp.tile(jnp.repeat(b_spatial[0][:, :DEC_SEQ].T[:, :, None], 128, axis=2)
                    .reshape(DEC_SEQ, W_D), (SB, 1))

    rs = DEC_BATCH * DEC_SEQ
    xp = x_prompt.reshape(BATCH * SEQ, D_MODEL)
    xs = x_sample.reshape(rs, D_MODEL)
    hp = _norm_cast(xp, g_even, 512)
    hs = _norm_cast(xs, g_even, 512)
    ya, conv_p, *ps_a = _even_conv(hp, hs, w_in_e, cw)
    yb, ret_p, *ps_b = _even_heads(hp, hs, w_in_e, g_ret, cos_p, sin_p, lg_tab)
    st_exp = jnp.pad(state_conv[0], ((0, 0), (0, DEC_SEQ - (CONV_W - 1)), (0, 0))).reshape(rs, W_A)
    ya_s, u_s, yb_s, ret_s = _even_sample(ps_a, ps_b, st_exp, state_ret[0], cw, g_ret, cos_s, sin_s, lg_tab)
    x1, h1, x1s, h1s = _outproj(ya, yb, xp, ya_s, yb_s, xs, w_out_e, g_odd, final=False)

    yc, c_p, n_p, m_p, *ps_c = _odd_heads(h1, h1s, w_o, bg, gm)
    yd = _odd_mlp_fused(h1, w_o, lng, lnb, ws, bst)
    pd_s = _in_proj_rows(h1s, w_o, 3 * W_D, N_GATE, O_GATE // IN_TN)
    n_rows = jnp.repeat(jnp.transpose(state_mlstm_n[0], (1, 0, 2)), DEC_SEQ, axis=1)
    m_rows = jnp.repeat(state_mlstm_m[0].T, DEC_SEQ, axis=1)[:, :, None]
    yc_s, yd_s, vn_s, c_s, no_s, mo_s = _odd_sample(
        ps_c, pd_s, h1s, w_o, state_mlstm_C[0], n_rows, m_rows, bg, gm, lng, lnb, rtab, btab)
    y_prompt, y_sample = _outproj(yc, yd, x1, yc_s, yd_s, x1s, w_out_o, g_fin, final=True)

    conv_s = u_s.reshape(DEC_BATCH, DEC_SEQ, W_A)[:, DEC_SEQ - (CONV_W - 1):, :]
    n_s = jnp.transpose(no_s[:, DEC_SEQ - 1::DEC_SEQ, :], (1, 0, 2))
    m_s = mo_s[:, DEC_SEQ - 1::DEC_SEQ, 0].T
    return (y_prompt.reshape(BATCH, SEQ, D_MODEL),
            y_sample.reshape(DEC_BATCH, DEC_SEQ, D_MODEL),
            conv_p[None], conv_s[None],
            ret_p[None], ret_s[None],
            c_p[None], c_s[None],
            n_p[:, :, 0, :][None], n_s[None],
            m_p[:, :, 0, 0][None], m_s[None],
            vn_s.reshape(DEC_BATCH, DEC_SEQ, W_D)[None])
```

```python
import functools
import math

import jax
import jax.numpy as jnp
from jax import lax
from jax.experimental import pallas as pl
from jax.experimental.pallas import tpu as pltpu

F32 = jnp.float32
BF16 = jnp.bfloat16

D_MODEL = 2048
BATCH = 4
SEQ = 2048
DEC_BATCH = 128
DEC_SEQ = 4
PAST_LEN = 16384
W_A = 1024
CONV_W = 3
W_B = 1024
H_B = 8
DH_B = 128
W_C = 1024
H_C = 4
DV_C = 256
DQK_C = 128
W_D = 1024
G_D = 8
CHUNK = 128
O_GATE = 2 * H_C * DQK_C + 2 * W_C
N_GATE = 2 * H_C
ROPE_BASE = 10000.0
EPS = 1e-6
LOG_GAMMA = tuple(math.log(1.0 - 2.0 ** (-5.0 - h)) for h in range(H_B))
NEG_INF = float("-inf")
VMEM_LIMIT = 56 * 1024 * 1024

NT_DIMS = (((1,), (1,)), ((), ()))
TN_DIMS = (((0,), (0,)), ((), ()))


def _silu(z):
    return z * (1.0 / (1.0 + jnp.exp(-z)))


def _log_sigmoid(x):
    return jnp.minimum(x, 0.0) - jnp.log1p(jnp.exp(-jnp.abs(x)))


def _dot(a, b):
    return jnp.dot(a, b, preferred_element_type=F32)


def _dot_nt(a, b):
    return lax.dot_general(a, b, NT_DIMS, preferred_element_type=F32)


def _dot_tn(a, b):
    return lax.dot_general(a, b, TN_DIMS, preferred_element_type=F32)


def _dot_hi(a, b):
    return jnp.dot(a, b, preferred_element_type=F32, precision=lax.Precision.HIGHEST)


def _head_norm(o, g):
    mu = jnp.mean(o, axis=-1, keepdims=True)
    oc = o - mu
    var = jnp.mean(oc * oc, axis=-1, keepdims=True)
    return oc * lax.rsqrt(var + EPS) * g


def _params(sem):
    return pltpu.CompilerParams(dimension_semantics=sem, vmem_limit_bytes=VMEM_LIMIT)


def _norm_cast_kernel(x_ref, g_ref, h_ref):
    x = x_ref[...]
    ms = jnp.mean(x * x, axis=-1, keepdims=True)
    h_ref[...] = (x * lax.rsqrt(ms + EPS) * g_ref[...]).astype(BF16)


def _norm_cast(x, g, tm):
    m, d = x.shape
    return pl.pallas_call(
        _norm_cast_kernel,
        grid=(m // tm,),
        in_specs=[pl.BlockSpec((tm, d), lambda i: (i, 0)),
                  pl.BlockSpec((1, d), lambda i: (0, 0))],
        out_specs=pl.BlockSpec((tm, d), lambda i: (i, 0)),
        out_shape=jax.ShapeDtypeStruct((m, d), BF16),
        compiler_params=_params(("arbitrary",)),
        name="norm_cast",
    )(x, g)


IN_TN = 1024


OUT_TM = 512


def _outproj_kernel(ya_ref, yb_ref, x_ref, yas_ref, ybs_ref, xs_ref, w_ref, g_ref, *out_refs,
                    final, n_prompt):
    i = pl.program_id(0)
    half = ya_ref.shape[1]
    n_out = 1 if final else 2

    def tile(ya, yb, x, outs):
        acc = _dot(ya[...], w_ref[0:half, :]) + _dot(yb[...], w_ref[half:2 * half, :])
        x1 = x[...] + acc
        ms = jnp.mean(x1 * x1, axis=-1, keepdims=True)
        hn = x1 * lax.rsqrt(ms + EPS) * g_ref[...]
        if final:
            outs[0][...] = hn
        else:
            outs[0][...] = x1
            outs[1][...] = hn.astype(BF16)

    @pl.when(i < n_prompt)
    def _():
        tile(ya_ref, yb_ref, x_ref, out_refs[:n_out])

    @pl.when(i == n_prompt)
    def _():
        tile(yas_ref, ybs_ref, xs_ref, out_refs[n_out:])


def _outproj(ya, yb, x, ya_s, yb_s, x_s, w, g, final):
    m, half = ya.shape
    ms = ya_s.shape[0]
    d = w.shape[1]
    n_prompt = m // OUT_TM
    row = lambda i: (jnp.minimum(i, n_prompt - 1), 0)
    const = lambda i: (0, 0)
    once = pl.Buffered(1)
    shapes = [jax.ShapeDtypeStruct((m, d), F32), jax.ShapeDtypeStruct((ms, d), F32)]
    specs = [pl.BlockSpec((OUT_TM, d), row), pl.BlockSpec((ms, d), const)]
    if not final:
        shapes = [shapes[0], jax.ShapeDtypeStruct((m, d), BF16), shapes[1], jax.ShapeDtypeStruct((ms, d), BF16)]
        specs = [specs[0], pl.BlockSpec((OUT_TM, d), row), specs[1], pl.BlockSpec((ms, d), const)]
    return pl.pallas_call(
        functools.partial(_outproj_kernel, final=final, n_prompt=n_prompt),
        grid=(n_prompt + 1,),
        in_specs=[pl.BlockSpec((OUT_TM, half), row),
                  pl.BlockSpec((OUT_TM, half), row),
                  pl.BlockSpec((OUT_TM, d), row),
                  pl.BlockSpec((ms, half), const, pipeline_mode=once),
                  pl.BlockSpec((ms, half), const, pipeline_mode=once),
                  pl.BlockSpec((ms, d), const, pipeline_mode=once),
                  pl.BlockSpec((2 * half, d), const, pipeline_mode=once),
                  pl.BlockSpec((1, d), const)],
        out_specs=tuple(specs),
        out_shape=tuple(shapes),
        compiler_params=_params(("arbitrary",)),
        name="out_proj_final" if final else "out_proj",
    )(ya, yb, x, ya_s, yb_s, x_s, w, g)


def _rope(x, cosf, sins):
    return x * cosf + pltpu.roll(x, DH_B // 2, 1) * sins


FT = 1024
FN = BATCH * SEQ // FT
FPB = SEQ // FT
HG = 2
GW = HG * 128
PCH = 2
CONV_PCH = 4


def _chunk_pipeline(n, piece, stages, gate=None):
    sa, sb, sc, sd, se = stages
    npieces = n // PCH
    for j in range(PCH):
        piece(0, j)
    if gate is not None:
        gate(0)
    for c in range(n + 2):
        k, j = c // PCH + 1, c % PCH
        if k < npieces:
            piece(k, j)
        if c < n:
            sa(c)
        if 1 <= c <= n:
            sc(c - 1)
        if c < n:
            sb(c)
        if 1 <= c <= n:
            sd(c - 1)
        if c >= 2:
            se(c - 2)
        if gate is not None and j == PCH - 1 and k < npieces:
            gate(k)


def _even_heads_kernel(hp_ref, hs_ref, wq_ref, wk_ref, wv_ref, wz_ref, gret_ref, cos_ref, sin_ref, lg_ref,
                       yb_ref, s_ref, sq_ref, sk_ref, sv_ref, sz_ref, wb, pt, s_scr):
    s = pl.program_id(1)
    L = CHUNK

    @pl.when(s == 0)
    def _():
        for part, w_ref in enumerate((wq_ref, wk_ref, wv_ref, wz_ref)):
            wb[:, part * GW:(part + 1) * GW] = w_ref[...].astype(BF16)
        ps = _dot(hs_ref[...], wb[...])
        for part, o_ref in enumerate((sq_ref, sk_ref, sv_ref, sz_ref)):
            o_ref[...] = ps[:, part * GW:(part + 1) * GW]

    @pl.when(s > 0)
    def _():
        t = s - 1
        n = FT // L
        rows = lambda c: slice(c * L, (c + 1) * L)
        cols = lambda part, i: slice(part * GW + i * DH_B, part * GW + (i + 1) * DH_B)

        def piece(k, j):
            pr = slice(k * PCH * L, (k + 1) * PCH * L)
            pc = slice(j * 2 * GW, (j + 1) * 2 * GW)
            pt[pr, pc] = _dot(hp_ref[pr, :], wb[:, pc])

        row = lax.broadcasted_iota(jnp.int32, (L, L), 0)
        col = lax.broadcasted_iota(jnp.int32, (L, L), 1)
        causal = row >= col
        diff = jnp.maximum(row - col, 0).astype(F32)
        ti = lax.broadcasted_iota(jnp.int32, (L, 1), 0).astype(F32)
        lgs = [lg_ref[i][:, 0:1] for i in range(HG)]
        decay = [jnp.where(causal, jnp.exp(lg * diff), 0.0) for lg in lgs]
        q_decay = [jnp.exp(lg * (ti + 1.0)) for lg in lgs]
        k_decay = [jnp.exp(lg * (L - 1.0 - ti)) for lg in lgs]
        gamma_l = [jnp.exp(lg * float(L)) for lg in lgs]
        state = {0: [jnp.where(t % FPB == 0, 0.0, s_scr[i]) for i in range(HG)]}
        v = {}

        def stage_a(c):
            cosf = cos_ref[rows(c), :]
            sins = sin_ref[rows(c), :]
            v[c] = []
            for i in range(HG):
                kr = _rope(pt[rows(c), cols(1, i)], cosf, sins) * (DH_B ** -0.5)
                v[c].append(dict(qb=_rope(pt[rows(c), cols(0, i)], cosf, sins).astype(BF16),
                                 kb=kr.astype(BF16),
                                 kd=(kr * k_decay[i]).astype(BF16),
                                 vb=pt[rows(c), cols(2, i)].astype(BF16)))

        def stage_b(c):
            for i, d in enumerate(v[c]):
                d["sc"] = _dot_nt(d["qb"], d["kb"])
                d["upd"] = _dot_tn(d["kd"], d["vb"])
            for i, d in enumerate(v[c]):
                d["cross"] = _dot(d["qb"], state[c][i].astype(BF16))

        def stage_c(c):
            state[c + 1] = []
            for i, d in enumerate(v[c]):
                d["sc"] = (d["sc"] * decay[i]).astype(BF16)
                state[c + 1].append(gamma_l[i] * state[c][i] + d["upd"])

        def stage_d(c):
            for d in v[c]:
                d["inner"] = _dot(d["sc"], d["vb"])

        def stage_e(c):
            for i, d in enumerate(v.pop(c)):
                o = d["inner"] + d["cross"] * q_decay[i]
                g = gret_ref[0:1, i * DH_B:(i + 1) * DH_B]
                z = pt[rows(c), cols(3, i)]
                yb_ref[rows(c), i * DH_B:(i + 1) * DH_B] = (_head_norm(o, g) * _silu(z)).astype(BF16)

        _chunk_pipeline(n, piece, (stage_a, stage_b, stage_c, stage_d, stage_e))
        for i in range(HG):
            s_scr[i] = state[n][i]
            s_ref[0, i] = state[n][i]


def _even_heads(hp, hs, w, g_ret, cosf, sins, lg_tab):
    k = hp.shape[1]
    ms = hs.shape[0]
    ng = H_B // HG
    base = 4 * W_A // GW
    tile = lambda s: jnp.maximum(s - 1, 0)
    wspec = lambda part: pl.BlockSpec((k, GW), lambda g, s: (0, base + part * ng + g))
    sspec = pl.BlockSpec((ms, GW), lambda g, s: (0, g))
    sshape = jax.ShapeDtypeStruct((ms, W_B), F32)
    return pl.pallas_call(
        _even_heads_kernel,
        grid=(ng, FN + 1),
        in_specs=[pl.BlockSpec((FT, k), lambda g, s: (tile(s), 0)),
                  pl.BlockSpec((ms, k), lambda g, s: (0, 0)),
                  wspec(0), wspec(1), wspec(2), wspec(3),
                  pl.BlockSpec((1, GW), lambda g, s: (0, g)),
                  pl.BlockSpec((FT, DH_B), lambda g, s: (tile(s) % FPB, 0)),
                  pl.BlockSpec((FT, DH_B), lambda g, s: (tile(s) % FPB, 0)),
                  pl.BlockSpec((HG, 1, 128), lambda g, s: (g, 0, 0))],
        out_specs=(pl.BlockSpec((FT, GW), lambda g, s: (tile(s), g)),
                   pl.BlockSpec((1, HG, DH_B, DH_B), lambda g, s: (tile(s) // FPB, g, 0, 0)),
                   sspec, sspec, sspec, sspec),
        out_shape=(jax.ShapeDtypeStruct((BATCH * SEQ, W_B), BF16),
                   jax.ShapeDtypeStruct((BATCH, H_B, DH_B, DH_B), F32),
                   sshape, sshape, sshape, sshape),
        scratch_shapes=[pltpu.VMEM((k, 4 * GW), BF16),
                        pltpu.VMEM((FT, 4 * GW), F32),
                        pltpu.VMEM((HG, DH_B, DH_B), F32)],
        compiler_params=_params(("arbitrary", "arbitrary")),
        name="even_heads",
    )(hp, hs, w, w, w, w, g_ret, cosf, sins, lg_tab)


def _even_conv_kernel(hp_ref, hs_ref, wb_ref, wc_ref, wx_ref, wz_ref, cw_ref,
                      ya_ref, conv_ref, sb_ref, sc_ref, sx_ref, sz_ref, wb, pt, ubuf):
    s = pl.program_id(1)
    L = CHUNK

    @pl.when(s == 0)
    def _():
        for part, w_ref in enumerate((wb_ref, wc_ref, wx_ref, wz_ref)):
            wb[:, part * GW:(part + 1) * GW] = w_ref[...].astype(BF16)
        ps = _dot(hs_ref[...], wb[...])
        for part, o_ref in enumerate((sb_ref, sc_ref, sx_ref, sz_ref)):
            o_ref[...] = ps[:, part * GW:(part + 1) * GW]

    @pl.when(s > 0)
    def _():
        t = s - 1
        n = FT // L
        rows = lambda c: slice(c * L, (c + 1) * L)
        part = lambda p, c: pt[rows(c), p * GW:(p + 1) * GW]

        @pl.when(t % FPB == 0)
        def _():
            ubuf[0:8, :] = jnp.zeros((8, GW), F32)

        @pl.when(t % FPB != 0)
        def _():
            ubuf[0:8, :] = ubuf[FT:FT + 8, :]

        def piece(k, j):
            pr = slice(k * CONV_PCH * L, (k + 1) * CONV_PCH * L)
            pc = slice(j * 2 * GW, (j + 1) * 2 * GW)
            pt[pr, pc] = _dot(hp_ref[pr, :], wb[:, pc])

        todo = [(k, j) for k in range(n // CONV_PCH) for j in range(2)]
        piece(*todo.pop(0))
        piece(*todo.pop(0))
        for c in range(n):
            if todo:
                piece(*todo.pop(0))
            u = part(1, c) * part(2, c)
            ubuf[8 + c * L:8 + (c + 1) * L, :] = u
            t0 = ubuf[6 + c * L:6 + (c + 1) * L, :]
            t1 = ubuf[7 + c * L:7 + (c + 1) * L, :]
            conv = cw_ref[0:1, :] * t0 + cw_ref[1:2, :] * t1 + cw_ref[2:3, :] * u
            ya_ref[rows(c), :] = (part(0, c) * conv * _silu(part(3, c))).astype(BF16)
        conv_ref[0] = ubuf[FT + 6:FT + 8, :]


def _even_conv(hp, hs, w, conv_w):
    k = hp.shape[1]
    ms = hs.shape[0]
    ng = W_A // GW
    tile = lambda s: jnp.maximum(s - 1, 0)
    wspec = lambda part: pl.BlockSpec((k, GW), lambda g, s: (0, part * ng + g))
    sspec = pl.BlockSpec((ms, GW), lambda g, s: (0, g))
    sshape = jax.ShapeDtypeStruct((ms, W_A), F32)
    return pl.pallas_call(
        _even_conv_kernel,
        grid=(ng, FN + 1),
        in_specs=[pl.BlockSpec((FT, k), lambda g, s: (tile(s), 0)),
                  pl.BlockSpec((ms, k), lambda g, s: (0, 0)),
                  wspec(0), wspec(1), wspec(2), wspec(3),
                  pl.BlockSpec((CONV_W, GW), lambda g, s: (0, g))],
        out_specs=(pl.BlockSpec((FT, GW), lambda g, s: (tile(s), g)),
                   pl.BlockSpec((1, CONV_W - 1, GW), lambda g, s: (tile(s) // FPB, 0, g)),
                   sspec, sspec, sspec, sspec),
        out_shape=(jax.ShapeDtypeStruct((BATCH * SEQ, W_A), BF16),
                   jax.ShapeDtypeStruct((BATCH, CONV_W - 1, W_A), F32),
                   sshape, sshape, sshape, sshape),
        scratch_shapes=[pltpu.VMEM((k, 4 * GW), BF16),
                        pltpu.VMEM((FT, 4 * GW), F32),
                        pltpu.VMEM((FT + 8, GW), F32)],
        compiler_params=_params(("arbitrary", "arbitrary")),
        name="even_conv",
    )(hp, hs, w, w, w, w, conv_w)


SB = 32
SR = SB * DEC_SEQ


def _even_sample_kernel(ab_ref, ac_ref, ax_ref, az_ref, pq_ref, pk_ref, pv_ref, pz_ref, st_ref, s_ref,
                        cw_ref, gret_ref, cos_ref, sin_ref, lg_ref,
                        ya_ref, u_ref, yb_ref, so_ref, cross_scr):
    h = pl.program_id(1)
    row = lax.broadcasted_iota(jnp.int32, (SR, SR), 0)
    col = lax.broadcasted_iota(jnp.int32, (SR, SR), 1)
    trow = row & 3

    @pl.when(h == 0)
    def _():
        for j in range(W_A // 128):
            sl = slice(j * 128, (j + 1) * 128)
            a_b = ab_ref[:, sl]
            a_c = ac_ref[:, sl]
            a_x = ax_ref[:, sl]
            a_z = az_ref[:, sl]
            u = a_c * a_x
            e = st_ref[:, sl]
            tap1 = jnp.where(trow >= 1, pltpu.roll(u, 1, 0), pltpu.roll(e, SR - 1, 0))
            tap0 = jnp.where(trow >= 2, pltpu.roll(u, 2, 0), e)
            conv = cw_ref[0:1, sl] * tap0 + cw_ref[1:2, sl] * tap1 + cw_ref[2:3, sl] * u
            ya_ref[:, sl] = (a_b * conv * _silu(a_z)).astype(BF16)
            u_ref[:, sl] = u

    lg = lg_ref[0][:, 0:1]
    same = (row >> 2) == (col >> 2)
    dd = trow - (col & 3)
    mask = jnp.where(same, dd, -1) >= 0
    decay = jnp.where(mask, jnp.exp(lg * jnp.maximum(dd, 0).astype(F32)), 0.0)
    tcol = (lax.broadcasted_iota(jnp.int32, (SR, 1), 0) & 3).astype(F32)
    cosf = cos_ref[...]
    sins = sin_ref[...]
    qr = _rope(pq_ref[...], cosf, sins)
    kr = _rope(pk_ref[...], cosf, sins) * (DH_B ** -0.5)
    qb = qr.astype(BF16)
    kb = kr.astype(BF16)
    vb = pv_ref[...].astype(BF16)
    sc = _dot_nt(qb, kb) * decay
    inner = _dot(sc.astype(BF16), vb)
    kdt = (kr * jnp.exp(lg * (DEC_SEQ - 1.0 - tcol))).T
    gamma_l = jnp.exp(lg * float(DEC_SEQ))
    lane_b = col >> 2
    sub = lax.broadcasted_iota(jnp.int32, (8, DH_B), 0)
    for g in range(SR // 8):
        q8 = qr[8 * g:8 * g + 8, :]
        q2 = jnp.concatenate([jnp.where(sub < DEC_SEQ, q8, 0.0), jnp.where(sub < DEC_SEQ, 0.0, q8)], axis=1)
        s_pair = [s_ref[2 * g + beta, 0] for beta in range(2)]
        cross_scr[8 * g:8 * g + 8, :] = _dot(
            q2.astype(BF16), jnp.concatenate([sp.astype(BF16) for sp in s_pair], axis=0))
        for beta in range(2):
            b = 2 * g + beta
            lhs = jnp.where(lane_b == b, kdt, 0.0).astype(BF16)
            so_ref[b, 0] = gamma_l * s_pair[beta] + _dot(lhs, vb)
    o = inner + cross_scr[...] * jnp.exp(lg * (tcol + 1.0))
    yb_ref[...] = (_head_norm(o, gret_ref[...]) * _silu(pz_ref[...])).astype(BF16)


def _even_sample(pa, pb, st_exp, s_state, conv_w, g_ret, cosf, sins, lg_tab):
    nb = DEC_BATCH // SB
    const2 = lambda i, h: (0, 0)
    aspec = pl.BlockSpec((SR, W_A), lambda i, h: (i, 0))
    hspec = pl.BlockSpec((SR, DH_B), lambda i, h: (i, h))
    return pl.pallas_call(
        _even_sample_kernel,
        grid=(nb, H_B),
        in_specs=[aspec, aspec, aspec, aspec,
                  hspec, hspec, hspec, hspec,
                  pl.BlockSpec((SR, W_A), lambda i, h: (i, 0)),
                  pl.BlockSpec((SB, 1, DH_B, DH_B), lambda i, h: (i, h, 0, 0)),
                  pl.BlockSpec((CONV_W, W_A), const2),
                  pl.BlockSpec((1, DH_B), lambda i, h: (0, h)),
                  pl.BlockSpec((SR, DH_B), const2),
                  pl.BlockSpec((SR, DH_B), const2),
                  pl.BlockSpec((1, 1, 128), lambda i, h: (h, 0, 0))],
        out_specs=(pl.BlockSpec((SR, W_A), lambda i, h: (i, 0)),
                   pl.BlockSpec((SR, W_A), lambda i, h: (i, 0)),
                   pl.BlockSpec((SR, DH_B), lambda i, h: (i, h)),
                   pl.BlockSpec((SB, 1, DH_B, DH_B), lambda i, h: (i, h, 0, 0))),
        out_shape=(jax.ShapeDtypeStruct((DEC_BATCH * DEC_SEQ, W_A), BF16),
                   jax.ShapeDtypeStruct((DEC_BATCH * DEC_SEQ, W_A), F32),
                   jax.ShapeDtypeStruct((DEC_BATCH * DEC_SEQ, W_B), BF16),
                   jax.ShapeDtypeStruct((DEC_BATCH, H_B, DH_B, DH_B), F32)),
        scratch_shapes=[pltpu.VMEM((SR, DH_B), F32)],
        compiler_params=_params(("arbitrary", "arbitrary")),
        name="even_sample",
    )(*pa, *pb, st_exp, s_state, conv_w, g_ret, cosf, sins, lg_tab)


CG = 2
CW = 128 + CG * (2 * DQK_C + 2 * DV_C)
CT = 1024
CN = BATCH * SEQ // CT
CPB = SEQ // CT


def _odd_heads_kernel(hp_ref, hs_ref, wq_ref, wk_ref, wv_ref, wz_ref, wg_ref, bg_ref, gm_ref,
                      yc_ref, c_ref, n_ref, m_ref, sq_ref, sk_ref, sv_ref, sz_ref,
                      wb, pt, c_scr, n_scr, m_scr, gt_scr):
    grp = pl.program_id(0)
    s = pl.program_id(1)
    L = CHUNK
    gc = slice(0, 128)
    qc = slice(128, 128 + CG * DQK_C)
    kc = slice(qc.stop, qc.stop + CG * DQK_C)
    vc = slice(kc.stop, kc.stop + CG * DV_C)
    zc = slice(vc.stop, vc.stop + CG * DV_C)
    head = lambda sl, i, w: slice(sl.start + i * w, sl.start + (i + 1) * w)

    @pl.when(s == 0)
    def _():
        wb[qc, :] = wq_ref[...].astype(BF16)
        wb[kc, :] = wk_ref[...].astype(BF16)
        wb[vc, :] = wv_ref[...].astype(BF16)
        wb[zc, :] = wz_ref[...].astype(BF16)
        wb[gc, :] = wg_ref[...].astype(BF16)
        ps = _dot_nt(hs_ref[...], wb[qc.start:CW, :])
        off = lambda sl: slice(sl.start - qc.start, sl.stop - qc.start)
        sq_ref[...] = ps[:, off(qc)]
        sk_ref[...] = ps[:, off(kc)]
        sv_ref[...] = ps[:, off(vc)]
        sz_ref[...] = ps[:, off(zc)]

    @pl.when(s > 0)
    def _():
        t = s - 1
        n = CT // L
        rows = lambda c: slice(c * L, (c + 1) * L)

        def piece(k, j):
            pr = slice(k * PCH * L, (k + 1) * PCH * L)
            pc = (slice(0, vc.start), slice(vc.start, CW))[j]
            pt[pr, pc] = _dot_nt(hp_ref[pr, :], wb[pc, :])

        row = lax.broadcasted_iota(jnp.int32, (L, L), 0)
        col = lax.broadcasted_iota(jnp.int32, (L, L), 1)
        tri = row >= col
        fresh = t % CPB == 0
        cst = {0: [jnp.where(fresh, 0.0, c_scr[i]) for i in range(CG)]}
        nst = {0: [jnp.where(fresh, 0.0, n_scr[i]) for i in range(CG)]}
        mst = {0: [jnp.where(fresh, 0.0, m_scr[i, 0:1, 0:1]) for i in range(CG)]}
        v = {}
        gates = {}

        def gate(k):
            cs = range(k * PCH, (k + 1) * PCH)
            for c in cs:
                gt_scr[c] = (pt[rows(c), gc] + bg_ref[...]).T
            pad = jnp.zeros((8 - CG * PCH, L), F32)
            ig_rows = jnp.concatenate(
                [gt_scr[c, pl.ds(grp * CG + i, 1), :] for i in range(CG) for c in cs] + [pad], axis=0)
            lf_rows = jnp.concatenate(
                [_log_sigmoid(gt_scr[c, pl.ds(grp * CG + i + H_C, 1), :]) for i in range(CG) for c in cs]
                + [pad], axis=0)
            b_rows = _dot_hi(lf_rows, jnp.where(row <= col, 1.0, 0.0))
            tall = jnp.zeros((L - 8, L), F32)
            gates[k] = dict(ig_rows=ig_rows, b_rows=b_rows,
                            b_cols=jnp.concatenate([b_rows, tall], axis=0).T,
                            ig_cols=jnp.concatenate([ig_rows, tall], axis=0).T)

        def stage_a(c):
            gk = gates[c // PCH]
            v[c] = []
            nst[c + 1] = []
            mst[c + 1] = []
            for i in range(CG):
                r = i * PCH + c % PCH
                b_r = gk["b_rows"][r:r + 1, :]
                ig_r = gk["ig_rows"][r:r + 1, :]
                b_c = gk["b_cols"][:, r:r + 1]
                ig_c = gk["ig_cols"][:, r:r + 1]
                m_prev = mst[c][i]
                log_d = jnp.where(tri, b_c - b_r + ig_r, NEG_INF)
                log_inter = b_c + m_prev
                m_t = jnp.maximum(log_inter, jnp.max(log_d, axis=-1, keepdims=True))
                m_new = m_t[L - 1:L, :]
                b_last = b_c[L - 1:L, :]
                w_end = jnp.exp(b_last - b_c + ig_c - m_new)
                cd = jnp.exp(b_last + m_prev - m_new)
                q = pt[rows(c), head(qc, i, DQK_C)] * (DQK_C ** -0.5)
                k = pt[rows(c), head(kc, i, DQK_C)]
                vv = pt[rows(c), head(vc, i, DV_C)]
                nst[c + 1].append(cd * nst[c][i] + jnp.sum(w_end * k, axis=0, keepdims=True))
                mst[c + 1].append(m_new)
                v[c].append(dict(w=jnp.exp(log_d - m_t), w_inter=jnp.exp(log_inter - m_t),
                                 floor=jnp.exp(-m_t), cd=cd, qb=q.astype(BF16), kb=k.astype(BF16),
                                 vb=vv.astype(BF16), vw=(vv * w_end).astype(BF16),
                                 qn=jnp.sum(q * nst[c][i], axis=-1, keepdims=True)))

        def stage_b(c):
            for i, d in enumerate(v[c]):
                d["sc"] = _dot_nt(d["qb"], d["kb"])
                d["upd"] = _dot_tn(d["vw"], d["kb"])
            for i, d in enumerate(v[c]):
                d["inter"] = _dot_nt(d["qb"], cst[c][i].astype(BF16))

        def stage_c(c):
            cst[c + 1] = []
            for i, d in enumerate(v[c]):
                sc = d["sc"] * d["w"]
                d["den"] = jnp.sum(sc, axis=-1, keepdims=True) + d["w_inter"] * d["qn"]
                d["sc"] = sc.astype(BF16)
                cst[c + 1].append(d["cd"] * cst[c][i] + d["upd"])

        def stage_d(c):
            for d in v[c]:
                d["num"] = _dot(d["sc"], d["vb"])

        def stage_e(c):
            for i, d in enumerate(v.pop(c)):
                num = d["num"] + d["w_inter"] * d["inter"]
                hh = num / jnp.maximum(jnp.abs(d["den"]), d["floor"])
                z = pt[rows(c), head(zc, i, DV_C)]
                ys = slice(i * DV_C, (i + 1) * DV_C)
                yc_ref[rows(c), ys] = (_head_norm(hh, gm_ref[0:1, ys]) * _silu(z)).astype(BF16)

        _chunk_pipeline(n, piece, (stage_a, stage_b, stage_c, stage_d, stage_e), gate)
        for i in range(CG):
            c_scr[i] = cst[n][i]
            n_scr[i] = nst[n][i]
            m_scr[i] = jnp.broadcast_to(mst[n][i], (8, 128))
            c_ref[0, i] = cst[n][i]
            n_ref[0, i] = nst[n][i]
            m_ref[0, i] = jnp.broadcast_to(mst[n][i], (1, 128))


def _odd_heads(hp, hs, w_t, bg, gm):
    k = hp.shape[1]
    ms = hs.shape[0]
    tile = lambda s: jnp.maximum(s - 1, 0)
    qw, vw = CG * DQK_C, CG * DV_C
    koff = H_C * DQK_C // qw
    voff = 2 * H_C * DQK_C // vw
    zoff = (2 * H_C * DQK_C + W_C) // vw
    seq = lambda g, s: (tile(s) // CPB, g, 0, 0)
    once = pl.Buffered(1)
    return pl.pallas_call(
        _odd_heads_kernel,
        grid=(H_C // CG, CN + 1),
        in_specs=[pl.BlockSpec((CT, k), lambda g, s: (tile(s), 0)),
                  pl.BlockSpec((ms, k), lambda g, s: (0, 0), pipeline_mode=once),
                  pl.BlockSpec((qw, k), lambda g, s: (g, 0), pipeline_mode=once),
                  pl.BlockSpec((qw, k), lambda g, s: (koff + g, 0), pipeline_mode=once),
                  pl.BlockSpec((vw, k), lambda g, s: (voff + g, 0), pipeline_mode=once),
                  pl.BlockSpec((vw, k), lambda g, s: (zoff + g, 0), pipeline_mode=once),
                  pl.BlockSpec((128, k), lambda g, s: (O_GATE // 128, 0), pipeline_mode=once),
                  pl.BlockSpec((1, 128), lambda g, s: (0, 0)),
                  pl.BlockSpec((1, vw), lambda g, s: (0, g))],
        out_specs=(pl.BlockSpec((CT, vw), lambda g, s: (tile(s), g)),
                   pl.BlockSpec((1, CG, DV_C, DQK_C), seq),
                   pl.BlockSpec((1, CG, 1, DQK_C), seq),
                   pl.BlockSpec((1, CG, 1, 128), seq),
                   pl.BlockSpec((ms, qw), lambda g, s: (0, g)),
                   pl.BlockSpec((ms, qw), lambda g, s: (0, g)),
                   pl.BlockSpec((ms, vw), lambda g, s: (0, g)),
                   pl.BlockSpec((ms, vw), lambda g, s: (0, g))),
        out_shape=(jax.ShapeDtypeStruct((BATCH * SEQ, W_C), BF16),
                   jax.ShapeDtypeStruct((BATCH, H_C, DV_C, DQK_C), F32),
                   jax.ShapeDtypeStruct((BATCH, H_C, 1, DQK_C), F32),
                   jax.ShapeDtypeStruct((BATCH, H_C, 1, 128), F32),
                   jax.ShapeDtypeStruct((ms, H_C * DQK_C), F32),
                   jax.ShapeDtypeStruct((ms, H_C * DQK_C), F32),
                   jax.ShapeDtypeStruct((ms, W_C), F32),
                   jax.ShapeDtypeStruct((ms, W_C), F32)),
        scratch_shapes=[pltpu.VMEM((CW, k), BF16),
                        pltpu.VMEM((CT, CW), F32),
                        pltpu.VMEM((CG, DV_C, DQK_C), F32),
                        pltpu.VMEM((CG, 1, DQK_C), F32),
                        pltpu.VMEM((CG, 8, 128), F32),
                        pltpu.VMEM((CT // CHUNK, CHUNK, CHUNK), F32)],
        compiler_params=_params(("arbitrary", "arbitrary")),
        name="odd_heads",
    )(hp, hs, w_t, w_t, w_t, w_t, w_t, bg, gm)


DT = 512


def _odd_mlp_fused_kernel(hp_ref, w0_ref, w1_ref, w2_ref, wt_ref, lng_ref, lnb_ref, ws_ref, bst_ref,
                          yd_ref, wb, pt, wsb):
    s = pl.program_id(0)
    L = CHUNK
    uc = slice(0, W_D)
    vc = slice(W_D, 2 * W_D)
    zc = slice(2 * W_D, 3 * W_D)

    @pl.when(s == 0)
    def _():
        sh = N_GATE
        wb[uc, :] = jnp.concatenate([w0_ref[sh:, :], w1_ref[0:sh, :]], axis=0).astype(BF16)
        wb[vc, :] = jnp.concatenate([w1_ref[sh:, :], w2_ref[0:sh, :]], axis=0).astype(BF16)
        wb[zc, :] = jnp.concatenate([w2_ref[sh:, :], wt_ref[...]], axis=0).astype(BF16)
        keep = (lax.broadcasted_iota(jnp.int32, (L, L), 0) >= lax.broadcasted_iota(jnp.int32, (L, L), 1))
        for g in range(G_D):
            wsb[g] = jnp.where(keep, ws_ref[g], 0.0).astype(BF16)

    @pl.when(s > 0)
    def _():
        n = DT // L
        rows = lambda c: slice(c * L, (c + 1) * L)
        grp = lambda sl, g: slice(sl.start + g * 128, sl.start + (g + 1) * 128)
        vn = {}
        mix = {}

        def project(pc):
            pt[:, pc] = _dot_nt(hp_ref[...], wb[pc, :])

        def stage_a(c):
            dv = lambda g: pt[rows(c), grp(vc, g)]
            tot = dv(0)
            for g in range(1, G_D):
                tot = tot + dv(g)
            mu = jnp.sum(tot, axis=-1, keepdims=True) * (1.0 / W_D)
            sq = (dv(0) - mu) * (dv(0) - mu)
            for g in range(1, G_D):
                sq = sq + (dv(g) - mu) * (dv(g) - mu)
            rstd = lax.rsqrt(jnp.sum(sq, axis=-1, keepdims=True) * (1.0 / W_D) + EPS)
            vn[c] = [((dv(g) - mu) * rstd * lng_ref[0:1, g * 128:(g + 1) * 128]
                      + lnb_ref[0:1, g * 128:(g + 1) * 128]).astype(BF16) for g in range(G_D)]

        def stage_b(c):
            mix[c] = [_dot(wsb[g], vn[c][g]) for g in range(G_D)]

        def stage_e(c):
            for g in range(G_D):
                sg = mix[c][g] + bst_ref[:, g:g + 1]
                d_u = pt[rows(c), grp(uc, g)]
                d_z = pt[rows(c), grp(zc, g)]
                yd_ref[rows(c), g * 128:(g + 1) * 128] = (d_u * sg * _silu(d_z)).astype(BF16)

        project(vc)
        for c in range(n):
            stage_a(c)
        project(uc)
        for c in range(n):
            stage_b(c)
        project(zc)
        for c in range(n):
            stage_e(c)


def _odd_mlp_fused(hp, w_t, lng, lnb, ws, bst):
    k = hp.shape[1]
    m = hp.shape[0]
    tile = lambda s: (jnp.maximum(s - 1, 0), 0)
    const2 = lambda s: (0, 0)
    t0 = O_GATE // IN_TN
    once = pl.Buffered(1)
    wspec = lambda j: pl.BlockSpec((IN_TN, k), lambda s: (t0 + j, 0), pipeline_mode=once)
    return pl.pallas_call(
        _odd_mlp_fused_kernel,
        grid=(m // DT + 1,),
        in_specs=[pl.BlockSpec((DT, k), tile),
                  wspec(0), wspec(1), wspec(2),
                  pl.BlockSpec((N_GATE, k), lambda s: ((t0 + 3) * (IN_TN // N_GATE), 0), pipeline_mode=once),
                  pl.BlockSpec((1, W_D), const2),
                  pl.BlockSpec((1, W_D), const2),
                  pl.BlockSpec((G_D, CHUNK, CHUNK), lambda s: (0, 0, 0)),
                  pl.BlockSpec((CHUNK, G_D), const2)],
        out_specs=pl.BlockSpec((DT, W_D), tile),
        out_shape=jax.ShapeDtypeStruct((m, W_D), BF16),
        scratch_shapes=[pltpu.VMEM((3 * W_D, k), BF16),
                        pltpu.VMEM((DT, 3 * W_D), F32),
                        pltpu.VMEM((G_D, CHUNK, CHUNK), BF16)],
        compiler_params=_params(("arbitrary",)),
        name="odd_mlp_fused",
    )(hp, w_t, w_t, w_t, w_t, lng, lnb, ws, bst)


def _in_proj_rows_kernel(h_ref, w_ref, wn_ref, o_ref, *, shift):
    wsh = jnp.concatenate([w_ref[shift:, :], wn_ref[...]], axis=0)
    o_ref[...] = _dot_nt(h_ref[...], wsh.astype(BF16))


def _in_proj_rows(h, w_t, n_out, shift, tile0):
    ms, k = h.shape
    return pl.pallas_call(
        functools.partial(_in_proj_rows_kernel, shift=shift),
        grid=(n_out // IN_TN,),
        in_specs=[pl.BlockSpec((ms, k), lambda j: (0, 0)),
                  pl.BlockSpec((IN_TN, k), lambda j: (j + tile0, 0)),
                  pl.BlockSpec((shift, k), lambda j: ((j + tile0 + 1) * (IN_TN // shift), 0))],
        out_specs=pl.BlockSpec((ms, IN_TN), lambda j: (0, j)),
        out_shape=jax.ShapeDtypeStruct((ms, n_out), F32),
        compiler_params=_params(("arbitrary",)),
        name="in_proj_rows",
    )(h, w_t, w_t)


def _odd_sample_kernel(pq_ref, pk_ref, pv_ref, pz_ref, h_ref, wg_ref, pd_ref,
                       c_ref, nrow_ref, mrow_ref, bg_ref, gm_ref, lng_ref, lnb_ref,
                       rtab_ref, btab_ref,
                       yc_ref, yd_ref, vn_ref, co_ref, no_ref, mo_ref,
                       inter_scr):
    h = pl.program_id(1)
    row = lax.broadcasted_iota(jnp.int32, (SR, SR), 0)
    col = lax.broadcasted_iota(jnp.int32, (SR, SR), 1)
    trow = row & 3

    @pl.when(h == 0)
    def _():
        dv = pd_ref[:, W_D:2 * W_D]
        mu = jnp.mean(dv, axis=-1, keepdims=True)
        xc = dv - mu
        var = jnp.mean(xc * xc, axis=-1, keepdims=True)
        rstd = lax.rsqrt(var + EPS)
        for g in range(G_D):
            sl = slice(g * 128, (g + 1) * 128)
            vn = xc[:, sl] * rstd * lng_ref[0:1, sl] + lnb_ref[0:1, sl]
            vn_ref[:, sl] = vn
            s = rtab_ref[0, :, sl] * vn + btab_ref[:, sl]
            for j in range(1, DEC_SEQ):
                s = s + jnp.where(trow >= j, rtab_ref[j, :, sl] * pltpu.roll(vn, j, 0), 0.0)
            d_u = pd_ref[:, g * 128:(g + 1) * 128]
            d_z = pd_ref[:, 2 * W_D + g * 128:2 * W_D + (g + 1) * 128]
            yd_ref[:, sl] = (d_u * s * _silu(d_z)).astype(BF16)

    same = (row >> 2) == (col >> 2)
    mask = jnp.where(same, trow - (col & 3), -1) >= 0
    pre = _dot_nt(h_ref[...], wg_ref[...].astype(BF16)) + bg_ref[...]
    lf = _log_sigmoid(pre)
    b_full = _dot_hi(jnp.where(mask, 1.0, 0.0), lf)
    sel_i = col == h
    sel_f = col == h + H_C
    ig_c = jnp.sum(jnp.where(sel_i, pre, 0.0), axis=-1, keepdims=True)
    b_c = jnp.sum(jnp.where(sel_f, b_full, 0.0), axis=-1, keepdims=True)
    sel_ir = row == h
    sel_fr = row == h + H_C
    ig_r = jnp.sum(jnp.where(sel_ir, pre.T, 0.0), axis=0, keepdims=True)
    b_r = jnp.sum(jnp.where(sel_fr, b_full.T, 0.0), axis=0, keepdims=True)
    m_prev = mrow_ref[0]
    log_d = jnp.where(mask, b_c - b_r + ig_r, NEG_INF)
    log_inter = b_c + m_prev
    m_t = jnp.maximum(log_inter, jnp.max(log_d, axis=-1, keepdims=True))
    w = jnp.exp(log_d - m_t)
    w_inter = jnp.exp(log_inter - m_t)
    q = pq_ref[...] * (DQK_C ** -0.5)
    k = pk_ref[...]
    v = pv_ref[...]
    qb = q.astype(BF16)
    kb = k.astype(BF16)
    vb = v.astype(BF16)
    sc = _dot_nt(qb, kb) * w
    sub8 = lax.broadcasted_iota(jnp.int32, (8, DQK_C), 0)
    for g in range(SR // 8):
        q8 = q[8 * g:8 * g + 8, :]
        q2 = jnp.concatenate([jnp.where(sub8 < DEC_SEQ, q8, 0.0), jnp.where(sub8 < DEC_SEQ, 0.0, q8)], axis=1)
        c_pair = jnp.concatenate([c_ref[2 * g + beta, 0].astype(BF16) for beta in range(2)], axis=1)
        inter_scr[8 * g:8 * g + 8, :] = _dot_nt(q2.astype(BF16), c_pair)
    n_rows = nrow_ref[0]
    num = _dot(sc.astype(BF16), vb) + w_inter * inter_scr[...]
    den = jnp.sum(sc, axis=-1, keepdims=True) + w_inter * jnp.sum(q * n_rows, axis=-1, keepdims=True)
    hh = num / jnp.maximum(jnp.abs(den), jnp.exp(-m_t))
    yc_ref[...] = (_head_norm(hh, gm_ref[...]) * _silu(pz_ref[...])).astype(BF16)

    stats = jnp.where(col == 0, m_t, jnp.where(col == 1, b_c, 0.0))
    last = _dot_hi(jnp.where(col == (row | 3), 1.0, 0.0), stats)
    m_new = last[:, 0:1]
    b_last = last[:, 1:2]
    w_end = jnp.exp(b_last - b_c + ig_c - m_new)
    cd = jnp.exp(b_last + m_prev - m_new)
    mo_ref[0] = m_new
    no_ref[0] = cd * n_rows + _dot_hi(jnp.where(same, 1.0, 0.0), w_end * k)
    vwt = (v * w_end).T
    lane_b = lax.broadcasted_iota(jnp.int32, (DV_C, SR), 1) >> 2
    for b in range(SB):
        lhs = jnp.where(lane_b == b, vwt, 0.0).astype(BF16)
        cd_b = cd[4 * b + 3:4 * b + 4, :]
        co_ref[b, 0] = cd_b * c_ref[b, 0] + _dot(lhs, kb)


def _odd_sample(pc, pd, h, w_o, c_state, n_rows, m_rows, bg, gm, lng, lnb, rtab, btab):
    nb = DEC_BATCH // SB
    const2 = lambda i, h: (0, 0)
    return pl.pallas_call(
        _odd_sample_kernel,
        grid=(nb, H_C),
        in_specs=[pl.BlockSpec((SR, DQK_C), lambda i, h: (i, h)),
                  pl.BlockSpec((SR, DQK_C), lambda i, h: (i, h)),
                  pl.BlockSpec((SR, DV_C), lambda i, h: (i, h)),
                  pl.BlockSpec((SR, DV_C), lambda i, h: (i, h)),
                  pl.BlockSpec((SR, D_MODEL), lambda i, h: (i, 0)),
                  pl.BlockSpec((128, D_MODEL), lambda i, h: (O_GATE // 128, 0)),
                  pl.BlockSpec((SR, 3 * W_D), lambda i, h: (i, 0)),
                  pl.BlockSpec((SB, 1, DV_C, DQK_C), lambda i, h: (i, h, 0, 0)),
                  pl.BlockSpec((1, SR, DQK_C), lambda i, h: (h, i, 0)),
                  pl.BlockSpec((1, SR, 1), lambda i, h: (h, i, 0)),
                  pl.BlockSpec((1, 128), const2),
                  pl.BlockSpec((1, DV_C), lambda i, h: (0, h)),
                  pl.BlockSpec((1, W_D), const2),
                  pl.BlockSpec((1, W_D), const2),
                  pl.BlockSpec((DEC_SEQ, SR, W_D), lambda i, h: (0, 0, 0)),
                  pl.BlockSpec((SR, W_D), const2)],
        out_specs=(pl.BlockSpec((SR, DV_C), lambda i, h: (i, h)),
                   pl.BlockSpec((SR, W_D), lambda i, h: (i, 0)),
                   pl.BlockSpec((SR, W_D), lambda i, h: (i, 0)),
                   pl.BlockSpec((SB, 1, DV_C, DQK_C), lambda i, h: (i, h, 0, 0)),
                   pl.BlockSpec((1, SR, DQK_C), lambda i, h: (h, i, 0)),
                   pl.BlockSpec((1, SR, 1), lambda i, h: (h, i, 0))),
        out_shape=(jax.ShapeDtypeStruct((DEC_BATCH * DEC_SEQ, W_C), BF16),
                   jax.ShapeDtypeStruct((DEC_BATCH * DEC_SEQ, W_D), BF16),
                   jax.ShapeDtypeStruct((DEC_BATCH * DEC_SEQ, W_D), F32),
                   jax.ShapeDtypeStruct((DEC_BATCH, H_C, DV_C, DQK_C), F32),
                   jax.ShapeDtypeStruct((H_C, DEC_BATCH * DEC_SEQ, DQK_C), F32),
                   jax.ShapeDtypeStruct((H_C, DEC_BATCH * DEC_SEQ, 1), F32)),
        scratch_shapes=[pltpu.VMEM((SR, DV_C), F32)],
        compiler_params=_params(("arbitrary", "arbitrary")),
        name="odd_sample",
    )(*pc, h, w_o, pd, c_state, n_rows, m_rows, bg, gm, lng, lnb, rtab, btab)


def _rope_tables(pos):
    inv = ROPE_BASE ** (-jnp.arange(0, DH_B, 2, dtype=F32) / DH_B)
    ang = pos.astype(F32)[:, None] * inv[None, :]
    cos = jnp.cos(ang)
    sin = jnp.sin(ang)
    return jnp.concatenate([cos, cos], axis=-1), jnp.concatenate([-sin, sin], axis=-1)


def kernel(x_prompt, x_sample, state_conv, state_ret, state_mlstm_C, state_mlstm_n, state_mlstm_m,
           norm_even, w_in_even, conv_w, ret_norm, w_out_even,
           norm_odd, w_in_odd, b_gate_odd, mlstm_norm, ln_v_g, ln_v_b,
           w_spatial, b_spatial, w_out_odd, norm_final):
    w_in_e = w_in_even[0]
    w_out_e = w_out_even[0].astype(BF16)
    w_o = w_in_odd[0].T
    w_out_o = w_out_odd[0].astype(BF16)
    g_even = norm_even[0][None, :]
    g_odd = norm_odd[0][None, :]
    g_fin = norm_final[None, :]
    cw = conv_w[0]
    g_ret = ret_norm[0][None, :]
    bg = jnp.concatenate([b_gate_odd[0], jnp.zeros((128 - 2 * H_C,), F32)])[None, :]
    gm = mlstm_norm[0][None, :]
    lng = ln_v_g[0][None, :]
    lnb = ln_v_b[0][None, :]
    ws = w_spatial[0]
    bst = b_spatial[0].T

    cos_p, sin_p = _rope_tables(jnp.arange(SEQ, dtype=jnp.int32))
    cos_s, sin_s = _rope_tables(PAST_LEN + jnp.arange(DEC_SEQ, dtype=jnp.int32))
    cos_s = jnp.tile(cos_s, (SB, 1))
    sin_s = jnp.tile(sin_s, (SB, 1))
    lg_tab = jnp.broadcast_to(jnp.asarray(LOG_GAMMA, F32)[:, None, None], (H_B, 1, 128))

    ws4 = ws[:, :DEC_SEQ, :DEC_SEQ]
    t_idx = jnp.arange(DEC_SEQ)
    rtab = []
    for j in range(DEC_SEQ):
        coef = ws4[:, t_idx, (t_idx - j) % DEC_SEQ]
        tab = jnp.repeat(coef.T[:, :, None], 128, axis=2).reshape(DEC_SEQ, W_D)
        rtab.append(jnp.tile(tab, (SB, 1)))
    rtab = jnp.stack(rtab)
    btab = jnp.tile(jnp.repeat(b_spatial[0][:, :DEC_SEQ].T[:, :, None], 128, axis=2)
                    .reshape(DEC_SEQ, W_D), (SB, 1))

    rs = DEC_BATCH * DEC_SEQ
    xp = x_prompt.reshape(BATCH * SEQ, D_MODEL)
    xs = x_sample.reshape(rs, D_MODEL)
    hp = _norm_cast(xp, g_even, 512)
    hs = _norm_cast(xs, g_even, 512)
    ya, conv_p, *ps_a = _even_conv(hp, hs, w_in_e, cw)
    yb, ret_p, *ps_b = _even_heads(hp, hs, w_in_e, g_ret, cos_p, sin_p, lg_tab)
    st_exp = jnp.pad(state_conv[0], ((0, 0), (0, DEC_SEQ - (CONV_W - 1)), (0, 0))).reshape(rs, W_A)
    ya_s, u_s, yb_s, ret_s = _even_sample(ps_a, ps_b, st_exp, state_ret[0], cw, g_ret, cos_s, sin_s, lg_tab)
    x1, h1, x1s, h1s = _outproj(ya, yb, xp, ya_s, yb_s, xs, w_out_e, g_odd, final=False)

    yc, c_p, n_p, m_p, *ps_c = _odd_heads(h1, h1s, w_o, bg, gm)
    yd = _odd_mlp_fused(h1, w_o, lng, lnb, ws, bst)
    pd_s = _in_proj_rows(h1s, w_o, 3 * W_D, N_GATE, O_GATE // IN_TN)
    n_rows = jnp.repeat(jnp.transpose(state_mlstm_n[0], (1, 0, 2)), DEC_SEQ, axis=1)
    m_rows = jnp.repeat(state_mlstm_m[0].T, DEC_SEQ, axis=1)[:, :, None]
    yc_s, yd_s, vn_s, c_s, no_s, mo_s = _odd_sample(
        ps_c, pd_s, h1s, w_o, state_mlstm_C[0], n_rows, m_rows, bg, gm, lng, lnb, rtab, btab)
    y_prompt, y_sample = _outproj(yc, yd, x1, yc_s, yd_s, x1s, w_out_o, g_fin, final=True)

    conv_s = u_s.reshape(DEC_BATCH, DEC_SEQ, W_A)[:, DEC_SEQ - (CONV_W - 1):, :]
    n_s = jnp.transpose(no_s[:, DEC_SEQ - 1::DEC_SEQ, :], (1, 0, 2))
    m_s = mo_s[:, DEC_SEQ - 1::DEC_SEQ, 0].T
    return (y_prompt.reshape(BATCH, SEQ, D_MODEL),
            y_sample.reshape(DEC_BATCH, DEC_SEQ, D_MODEL),
            conv_p[None], conv_s[None],
            ret_p[None], ret_s[None],
            c_p[None], c_s[None],
            n_p[:, :, 0, :][None], n_s[None],
            m_p[:, :, 0, 0][None], m_s[None],
            vn_s.reshape(DEC_BATCH, DEC_SEQ, W_D)[None])
```

```python
import functools
import math

import jax
import jax.numpy as jnp
from jax import lax
from jax.experimental import pallas as pl
from jax.experimental.pallas import tpu as pltpu

F32 = jnp.float32
BF16 = jnp.bfloat16

D_MODEL = 2048
BATCH = 4
SEQ = 2048
DEC_BATCH = 128
DEC_SEQ = 4
PAST_LEN = 16384
W_A = 1024
CONV_W = 3
W_B = 1024
H_B = 8
DH_B = 128
W_C = 1024
H_C = 4
DV_C = 256
DQK_C = 128
W_D = 1024
G_D = 8
CHUNK = 128
O_GATE = 2 * H_C * DQK_C + 2 * W_C
N_GATE = 2 * H_C
ROPE_BASE = 10000.0
EPS = 1e-6
LOG_GAMMA = tuple(math.log(1.0 - 2.0 ** (-5.0 - h)) for h in range(H_B))
NEG_INF = float("-inf")
VMEM_LIMIT = 56 * 1024 * 1024

NT_DIMS = (((1,), (1,)), ((), ()))
TN_DIMS = (((0,), (0,)), ((), ()))


def _silu(z):
    return z * (1.0 / (1.0 + jnp.exp(-z)))


def _log_sigmoid(x):
    return jnp.minimum(x, 0.0) - jnp.log1p(jnp.exp(-jnp.abs(x)))


def _dot(a, b):
    return jnp.dot(a, b, preferred_element_type=F32)


def _dot_nt(a, b):
    return lax.dot_general(a, b, NT_DIMS, preferred_element_type=F32)


def _dot_tn(a, b):
    return lax.dot_general(a, b, TN_DIMS, preferred_element_type=F32)


def _dot_hi(a, b):
    return jnp.dot(a, b, preferred_element_type=F32, precision=lax.Precision.HIGHEST)


def _head_norm(o, g):
    mu = jnp.mean(o, axis=-1, keepdims=True)
    oc = o - mu
    var = jnp.mean(oc * oc, axis=-1, keepdims=True)
    return oc * lax.rsqrt(var + EPS) * g


def _params(sem):
    return pltpu.CompilerParams(dimension_semantics=sem, vmem_limit_bytes=VMEM_LIMIT)


def _norm_cast_kernel(x_ref, xs_ref, g_ref, h_ref, hs_ref, *, n_prompt):
    i = pl.program_id(0)

    def tile(x_in, h_out):
        x = x_in[...]
        ms = jnp.mean(x * x, axis=-1, keepdims=True)
        h_out[...] = (x * lax.rsqrt(ms + EPS) * g_ref[...]).astype(BF16)

    @pl.when(i < n_prompt)
    def _():
        tile(x_ref, h_ref)

    @pl.when(i == n_prompt)
    def _():
        tile(xs_ref, hs_ref)


def _norm_cast(x, xs, g, tm):
    m, d = x.shape
    ms = xs.shape[0]
    n_prompt = m // tm
    row = lambda i: (jnp.minimum(i, n_prompt - 1), 0)
    const = lambda i: (0, 0)
    return pl.pallas_call(
        functools.partial(_norm_cast_kernel, n_prompt=n_prompt),
        grid=(n_prompt + 1,),
        in_specs=[pl.BlockSpec((tm, d), row),
                  pl.BlockSpec((ms, d), const),
                  pl.BlockSpec((1, d), const)],
        out_specs=(pl.BlockSpec((tm, d), row), pl.BlockSpec((ms, d), const)),
        out_shape=(jax.ShapeDtypeStruct((m, d), BF16), jax.ShapeDtypeStruct((ms, d), BF16)),
        compiler_params=_params(("arbitrary",)),
        name="norm_cast",
    )(x, xs, g)


IN_TN = 1024


OUT_TM = 512


def _outproj_kernel(ya_ref, yb_ref, x_ref, yas_ref, ybs_ref, xs_ref, w_ref, g_ref, *out_refs,
                    final, n_prompt):
    i = pl.program_id(0)
    half = ya_ref.shape[1]
    n_out = 1 if final else 2

    def tile(ya, yb, x, outs):
        acc = _dot(ya[...], w_ref[0:half, :]) + _dot(yb[...], w_ref[half:2 * half, :])
        x1 = x[...] + acc
        ms = jnp.mean(x1 * x1, axis=-1, keepdims=True)
        hn = x1 * lax.rsqrt(ms + EPS) * g_ref[...]
        if final:
            outs[0][...] = hn
        else:
            outs[0][...] = x1
            outs[1][...] = hn.astype(BF16)

    @pl.when(i < n_prompt)
    def _():
        tile(ya_ref, yb_ref, x_ref, out_refs[:n_out])

    @pl.when(i == n_prompt)
    def _():
        tile(yas_ref, ybs_ref, xs_ref, out_refs[n_out:])


def _outproj(ya, yb, x, ya_s, yb_s, x_s, w, g, final):
    m, half = ya.shape
    ms = ya_s.shape[0]
    d = w.shape[1]
    n_prompt = m // OUT_TM
    row = lambda i: (jnp.minimum(i, n_prompt - 1), 0)
    const = lambda i: (0, 0)
    once = pl.Buffered(1)
    shapes = [jax.ShapeDtypeStruct((m, d), F32), jax.ShapeDtypeStruct((ms, d), F32)]
    specs = [pl.BlockSpec((OUT_TM, d), row), pl.BlockSpec((ms, d), const)]
    if not final:
        shapes = [shapes[0], jax.ShapeDtypeStruct((m, d), BF16), shapes[1], jax.ShapeDtypeStruct((ms, d), BF16)]
        specs = [specs[0], pl.BlockSpec((OUT_TM, d), row), specs[1], pl.BlockSpec((ms, d), const)]
    return pl.pallas_call(
        functools.partial(_outproj_kernel, final=final, n_prompt=n_prompt),
        grid=(n_prompt + 1,),
        in_specs=[pl.BlockSpec((OUT_TM, half), row),
                  pl.BlockSpec((OUT_TM, half), row),
                  pl.BlockSpec((OUT_TM, d), row),
                  pl.BlockSpec((ms, half), const, pipeline_mode=once),
                  pl.BlockSpec((ms, half), const, pipeline_mode=once),
                  pl.BlockSpec((ms, d), const, pipeline_mode=once),
                  pl.BlockSpec((2 * half, d), const, pipeline_mode=once),
                  pl.BlockSpec((1, d), const)],
        out_specs=tuple(specs),
        out_shape=tuple(shapes),
        compiler_params=_params(("arbitrary",)),
        name="out_proj_final" if final else "out_proj",
    )(ya, yb, x, ya_s, yb_s, x_s, w, g)


def _rope(x, cosf, sins):
    return x * cosf + pltpu.roll(x, DH_B // 2, 1) * sins


FT = 1024
FN = BATCH * SEQ // FT
FPB = SEQ // FT
HG = 2
GW = HG * 128
PCH = 2
CONV_PCH = 4


def _chunk_pipeline(n, piece, stages, gate=None):
    sa, sb, sc, sd, se = stages
    npieces = n // PCH
    for j in range(PCH):
        piece(0, j)
    if gate is not None:
        gate(0)
    for c in range(n + 2):
        k, j = c // PCH + 1, c % PCH
        if k < npieces:
            piece(k, j)
        if c < n:
            sa(c)
        if 1 <= c <= n:
            sc(c - 1)
        if c < n:
            sb(c)
        if 1 <= c <= n:
            sd(c - 1)
        if c >= 2:
            se(c - 2)
        if gate is not None and j == PCH - 1 and k < npieces:
            gate(k)


def _even_heads_kernel(hp_ref, hs_ref, wq_ref, wk_ref, wv_ref, wz_ref, gret_ref, cos_ref, sin_ref, lg_ref,
                       yb_ref, s_ref, sq_ref, sk_ref, sv_ref, sz_ref, wb, pt, s_scr):
    s = pl.program_id(1)
    L = CHUNK

    @pl.when(s == 0)
    def _():
        for part, w_ref in enumerate((wq_ref, wk_ref, wv_ref, wz_ref)):
            wb[:, part * GW:(part + 1) * GW] = w_ref[...].astype(BF16)
        ps = _dot(hs_ref[...], wb[...])
        for part, o_ref in enumerate((sq_ref, sk_ref, sv_ref, sz_ref)):
            o_ref[...] = ps[:, part * GW:(part + 1) * GW]

    @pl.when(s > 0)
    def _():
        t = s - 1
        n = FT // L
        rows = lambda c: slice(c * L, (c + 1) * L)
        cols = lambda part, i: slice(part * GW + i * DH_B, part * GW + (i + 1) * DH_B)

        def piece(k, j):
            pr = slice(k * PCH * L, (k + 1) * PCH * L)
            pc = slice(j * 2 * GW, (j + 1) * 2 * GW)
            pt[pr, pc] = _dot(hp_ref[pr, :], wb[:, pc])

        row = lax.broadcasted_iota(jnp.int32, (L, L), 0)
        col = lax.broadcasted_iota(jnp.int32, (L, L), 1)
        causal = row >= col
        diff = jnp.maximum(row - col, 0).astype(F32)
        ti = lax.broadcasted_iota(jnp.int32, (L, 1), 0).astype(F32)
        lgs = [lg_ref[i][:, 0:1] for i in range(HG)]
        decay = [jnp.where(causal, jnp.exp(lg * diff), 0.0) for lg in lgs]
        q_decay = [jnp.exp(lg * (ti + 1.0)) for lg in lgs]
        k_decay = [jnp.exp(lg * (L - 1.0 - ti)) for lg in lgs]
        gamma_l = [jnp.exp(lg * float(L)) for lg in lgs]
        state = {0: [jnp.where(t % FPB == 0, 0.0, s_scr[i]) for i in range(HG)]}
        v = {}

        def stage_a(c):
            cosf = cos_ref[rows(c), :]
            sins = sin_ref[rows(c), :]
            v[c] = []
            for i in range(HG):
                kr = _rope(pt[rows(c), cols(1, i)], cosf, sins) * (DH_B ** -0.5)
                v[c].append(dict(qb=_rope(pt[rows(c), cols(0, i)], cosf, sins).astype(BF16),
                                 kb=kr.astype(BF16),
                                 kd=(kr * k_decay[i]).astype(BF16),
                                 vb=pt[rows(c), cols(2, i)].astype(BF16)))

        def stage_b(c):
            for i, d in enumerate(v[c]):
                d["sc"] = _dot_nt(d["qb"], d["kb"])
                d["upd"] = _dot_tn(d["kd"], d["vb"])
            for i, d in enumerate(v[c]):
                d["cross"] = _dot(d["qb"], state[c][i].astype(BF16))

        def stage_c(c):
            state[c + 1] = []
            for i, d in enumerate(v[c]):
                d["sc"] = (d["sc"] * decay[i]).astype(BF16)
                state[c + 1].append(gamma_l[i] * state[c][i] + d["upd"])

        def stage_d(c):
            for d in v[c]:
                d["inner"] = _dot(d["sc"], d["vb"])

        def stage_e(c):
            for i, d in enumerate(v.pop(c)):
                o = d["inner"] + d["cross"] * q_decay[i]
                g = gret_ref[0:1, i * DH_B:(i + 1) * DH_B]
                z = pt[rows(c), cols(3, i)]
                yb_ref[rows(c), i * DH_B:(i + 1) * DH_B] = (_head_norm(o, g) * _silu(z)).astype(BF16)

        _chunk_pipeline(n, piece, (stage_a, stage_b, stage_c, stage_d, stage_e))
        for i in range(HG):
            s_scr[i] = state[n][i]
            s_ref[0, i] = state[n][i]


def _even_heads(hp, hs, w, g_ret, cosf, sins, lg_tab):
    k = hp.shape[1]
    ms = hs.shape[0]
    ng = H_B // HG
    base = 4 * W_A // GW
    tile = lambda s: jnp.maximum(s - 1, 0)
    wspec = lambda part: pl.BlockSpec((k, GW), lambda g, s: (0, base + part * ng + g))
    sspec = pl.BlockSpec((ms, GW), lambda g, s: (0, g))
    sshape = jax.ShapeDtypeStruct((ms, W_B), F32)
    return pl.pallas_call(
        _even_heads_kernel,
        grid=(ng, FN + 1),
        in_specs=[pl.BlockSpec((FT, k), lambda g, s: (tile(s), 0)),
                  pl.BlockSpec((ms, k), lambda g, s: (0, 0)),
                  wspec(0), wspec(1), wspec(2), wspec(3),
                  pl.BlockSpec((1, GW), lambda g, s: (0, g)),
                  pl.BlockSpec((FT, DH_B), lambda g, s: (tile(s) % FPB, 0)),
                  pl.BlockSpec((FT, DH_B), lambda g, s: (tile(s) % FPB, 0)),
                  pl.BlockSpec((HG, 1, 128), lambda g, s: (g, 0, 0))],
        out_specs=(pl.BlockSpec((FT, GW), lambda g, s: (tile(s), g)),
                   pl.BlockSpec((1, HG, DH_B, DH_B), lambda g, s: (tile(s) // FPB, g, 0, 0)),
                   sspec, sspec, sspec, sspec),
        out_shape=(jax.ShapeDtypeStruct((BATCH * SEQ, W_B), BF16),
                   jax.ShapeDtypeStruct((BATCH, H_B, DH_B, DH_B), F32),
                   sshape, sshape, sshape, sshape),
        scratch_shapes=[pltpu.VMEM((k, 4 * GW), BF16),
                        pltpu.VMEM((FT, 4 * GW), F32),
                        pltpu.VMEM((HG, DH_B, DH_B), F32)],
        compiler_params=_params(("arbitrary", "arbitrary")),
        name="even_heads",
    )(hp, hs, w, w, w, w, g_ret, cosf, sins, lg_tab)


def _even_conv_kernel(hp_ref, hs_ref, wb_ref, wc_ref, wx_ref, wz_ref, cw_ref,
                      ya_ref, conv_ref, sb_ref, sc_ref, sx_ref, sz_ref, wb, pt, ubuf):
    s = pl.program_id(1)
    L = CHUNK

    @pl.when(s == 0)
    def _():
        for part, w_ref in enumerate((wb_ref, wc_ref, wx_ref, wz_ref)):
            wb[:, part * GW:(part + 1) * GW] = w_ref[...].astype(BF16)
        ps = _dot(hs_ref[...], wb[...])
        for part, o_ref in enumerate((sb_ref, sc_ref, sx_ref, sz_ref)):
            o_ref[...] = ps[:, part * GW:(part + 1) * GW]

    @pl.when(s > 0)
    def _():
        t = s - 1
        n = FT // L
        rows = lambda c: slice(c * L, (c + 1) * L)
        part = lambda p, c: pt[rows(c), p * GW:(p + 1) * GW]

        @pl.when(t % FPB == 0)
        def _():
            ubuf[0:8, :] = jnp.zeros((8, GW), F32)

        @pl.when(t % FPB != 0)
        def _():
            ubuf[0:8, :] = ubuf[FT:FT + 8, :]

        def piece(k, j):
            pr = slice(k * CONV_PCH * L, (k + 1) * CONV_PCH * L)
            pc = slice(j * 2 * GW, (j + 1) * 2 * GW)
            pt[pr, pc] = _dot(hp_ref[pr, :], wb[:, pc])

        todo = [(k, j) for k in range(n // CONV_PCH) for j in range(2)]
        piece(*todo.pop(0))
        piece(*todo.pop(0))
        for c in range(n):
            if todo:
                piece(*todo.pop(0))
            u = part(1, c) * part(2, c)
            ubuf[8 + c * L:8 + (c + 1) * L, :] = u
            t0 = ubuf[6 + c * L:6 + (c + 1) * L, :]
            t1 = ubuf[7 + c * L:7 + (c + 1) * L, :]
            conv = cw_ref[0:1, :] * t0 + cw_ref[1:2, :] * t1 + cw_ref[2:3, :] * u
            ya_ref[rows(c), :] = (part(0, c) * conv * _silu(part(3, c))).astype(BF16)
        conv_ref[0] = ubuf[FT + 6:FT + 8, :]


def _even_conv(hp, hs, w, conv_w):
    k = hp.shape[1]
    ms = hs.shape[0]
    ng = W_A // GW
    tile = lambda s: jnp.maximum(s - 1, 0)
    wspec = lambda part: pl.BlockSpec((k, GW), lambda g, s: (0, part * ng + g))
    sspec = pl.BlockSpec((ms, GW), lambda g, s: (0, g))
    sshape = jax.ShapeDtypeStruct((ms, W_A), F32)
    return pl.pallas_call(
        _even_conv_kernel,
        grid=(ng, FN + 1),
        in_specs=[pl.BlockSpec((FT, k), lambda g, s: (tile(s), 0)),
                  pl.BlockSpec((ms, k), lambda g, s: (0, 0)),
                  wspec(0), wspec(1), wspec(2), wspec(3),
                  pl.BlockSpec((CONV_W, GW), lambda g, s: (0, g))],
        out_specs=(pl.BlockSpec((FT, GW), lambda g, s: (tile(s), g)),
                   pl.BlockSpec((1, CONV_W - 1, GW), lambda g, s: (tile(s) // FPB, 0, g)),
                   sspec, sspec, sspec, sspec),
        out_shape=(jax.ShapeDtypeStruct((BATCH * SEQ, W_A), BF16),
                   jax.ShapeDtypeStruct((BATCH, CONV_W - 1, W_A), F32),
                   sshape, sshape, sshape, sshape),
        scratch_shapes=[pltpu.VMEM((k, 4 * GW), BF16),
                        pltpu.VMEM((FT, 4 * GW), F32),
                        pltpu.VMEM((FT + 8, GW), F32)],
        compiler_params=_params(("arbitrary", "arbitrary")),
        name="even_conv",
    )(hp, hs, w, w, w, w, conv_w)


SB = 32
SR = SB * DEC_SEQ


def _even_sample_kernel(ab_ref, ac_ref, ax_ref, az_ref, pq_ref, pk_ref, pv_ref, pz_ref, st_ref, s_ref,
                        cw_ref, gret_ref, cos_ref, sin_ref, lg_ref,
                        ya_ref, u_ref, yb_ref, so_ref, cross_scr):
    h = pl.program_id(1)
    row = lax.broadcasted_iota(jnp.int32, (SR, SR), 0)
    col = lax.broadcasted_iota(jnp.int32, (SR, SR), 1)
    trow = row & 3

    @pl.when(h == 0)
    def _():
        for j in range(W_A // 128):
            sl = slice(j * 128, (j + 1) * 128)
            a_b = ab_ref[:, sl]
            a_c = ac_ref[:, sl]
            a_x = ax_ref[:, sl]
            a_z = az_ref[:, sl]
            u = a_c * a_x
            e = st_ref[:, sl]
            tap1 = jnp.where(trow >= 1, pltpu.roll(u, 1, 0), pltpu.roll(e, SR - 1, 0))
            tap0 = jnp.where(trow >= 2, pltpu.roll(u, 2, 0), e)
            conv = cw_ref[0:1, sl] * tap0 + cw_ref[1:2, sl] * tap1 + cw_ref[2:3, sl] * u
            ya_ref[:, sl] = (a_b * conv * _silu(a_z)).astype(BF16)
            u_ref[:, sl] = u

    lg = lg_ref[0][:, 0:1]
    same = (row >> 2) == (col >> 2)
    dd = trow - (col & 3)
    mask = jnp.where(same, dd, -1) >= 0
    decay = jnp.where(mask, jnp.exp(lg * jnp.maximum(dd, 0).astype(F32)), 0.0)
    tcol = (lax.broadcasted_iota(jnp.int32, (SR, 1), 0) & 3).astype(F32)
    cosf = cos_ref[...]
    sins = sin_ref[...]
    qr = _rope(pq_ref[...], cosf, sins)
    kr = _rope(pk_ref[...], cosf, sins) * (DH_B ** -0.5)
    qb = qr.astype(BF16)
    kb = kr.astype(BF16)
    vb = pv_ref[...].astype(BF16)
    sc = _dot_nt(qb, kb) * decay
    inner = _dot(sc.astype(BF16), vb)
    kdt = (kr * jnp.exp(lg * (DEC_SEQ - 1.0 - tcol))).T
    gamma_l = jnp.exp(lg * float(DEC_SEQ))
    lane_b = col >> 2
    sub = lax.broadcasted_iota(jnp.int32, (8, DH_B), 0)
    for g in range(SR // 8):
        q8 = qr[8 * g:8 * g + 8, :]
        q2 = jnp.concatenate([jnp.where(sub < DEC_SEQ, q8, 0.0), jnp.where(sub < DEC_SEQ, 0.0, q8)], axis=1)
        s_pair = [s_ref[2 * g + beta, 0] for beta in range(2)]
        cross_scr[8 * g:8 * g + 8, :] = _dot(
            q2.astype(BF16), jnp.concatenate([sp.astype(BF16) for sp in s_pair], axis=0))
        for beta in range(2):
            b = 2 * g + beta
            lhs = jnp.where(lane_b == b, kdt, 0.0).astype(BF16)
            so_ref[b, 0] = gamma_l * s_pair[beta] + _dot(lhs, vb)
    o = inner + cross_scr[...] * jnp.exp(lg * (tcol + 1.0))
    yb_ref[...] = (_head_norm(o, gret_ref[...]) * _silu(pz_ref[...])).astype(BF16)


def _even_sample(pa, pb, st_exp, s_state, conv_w, g_ret, cosf, sins, lg_tab):
    nb = DEC_BATCH // SB
    const2 = lambda i, h: (0, 0)
    aspec = pl.BlockSpec((SR, W_A), lambda i, h: (i, 0))
    hspec = pl.BlockSpec((SR, DH_B), lambda i, h: (i, h))
    return pl.pallas_call(
        _even_sample_kernel,
        grid=(nb, H_B),
        in_specs=[aspec, aspec, aspec, aspec,
                  hspec, hspec, hspec, hspec,
                  pl.BlockSpec((SR, W_A), lambda i, h: (i, 0)),
                  pl.BlockSpec((SB, 1, DH_B, DH_B), lambda i, h: (i, h, 0, 0)),
                  pl.BlockSpec((CONV_W, W_A), const2),
                  pl.BlockSpec((1, DH_B), lambda i, h: (0, h)),
                  pl.BlockSpec((SR, DH_B), const2),
                  pl.BlockSpec((SR, DH_B), const2),
                  pl.BlockSpec((1, 1, 128), lambda i, h: (h, 0, 0))],
        out_specs=(pl.BlockSpec((SR, W_A), lambda i, h: (i, 0)),
                   pl.BlockSpec((SR, W_A), lambda i, h: (i, 0)),
                   pl.BlockSpec((SR, DH_B), lambda i, h: (i, h)),
                   pl.BlockSpec((SB, 1, DH_B, DH_B), lambda i, h: (i, h, 0, 0))),
        out_shape=(jax.ShapeDtypeStruct((DEC_BATCH * DEC_SEQ, W_A), BF16),
                   jax.ShapeDtypeStruct((DEC_BATCH * DEC_SEQ, W_A), F32),
                   jax.ShapeDtypeStruct((DEC_BATCH * DEC_SEQ, W_B), BF16),
                   jax.ShapeDtypeStruct((DEC_BATCH, H_B, DH_B, DH_B), F32)),
        scratch_shapes=[pltpu.VMEM((SR, DH_B), F32)],
        compiler_params=_params(("arbitrary", "arbitrary")),
        name="even_sample",
    )(*pa, *pb, st_exp, s_state, conv_w, g_ret, cosf, sins, lg_tab)


CG = 2
CW = 128 + CG * (2 * DQK_C + 2 * DV_C)
CT = 1024
CN = BATCH * SEQ // CT
CPB = SEQ // CT


def _odd_heads_kernel(hp_ref, hs_ref, wq_ref, wk_ref, wv_ref, wz_ref, wg_ref, bg_ref, gm_ref,
                      yc_ref, c_ref, n_ref, m_ref, sq_ref, sk_ref, sv_ref, sz_ref,
                      wb, pt, c_scr, n_scr, m_scr, gt_scr):
    grp = pl.program_id(0)
    s = pl.program_id(1)
    L = CHUNK
    gc = slice(0, 128)
    qc = slice(128, 128 + CG * DQK_C)
    kc = slice(qc.stop, qc.stop + CG * DQK_C)
    vc = slice(kc.stop, kc.stop + CG * DV_C)
    zc = slice(vc.stop, vc.stop + CG * DV_C)
    head = lambda sl, i, w: slice(sl.start + i * w, sl.start + (i + 1) * w)

    @pl.when(s == 0)
    def _():
        wb[qc, :] = wq_ref[...].astype(BF16)
        wb[kc, :] = wk_ref[...].astype(BF16)
        wb[vc, :] = wv_ref[...].astype(BF16)
        wb[zc, :] = wz_ref[...].astype(BF16)
        wb[gc, :] = wg_ref[...].astype(BF16)
        ps = _dot_nt(hs_ref[...], wb[qc.start:CW, :])
        off = lambda sl: slice(sl.start - qc.start, sl.stop - qc.start)
        sq_ref[...] = ps[:, off(qc)]
        sk_ref[...] = ps[:, off(kc)]
        sv_ref[...] = ps[:, off(vc)]
        sz_ref[...] = ps[:, off(zc)]

    @pl.when(s > 0)
    def _():
        t = s - 1
        n = CT // L
        rows = lambda c: slice(c * L, (c + 1) * L)

        def piece(k, j):
            pr = slice(k * PCH * L, (k + 1) * PCH * L)
            pc = (slice(0, vc.start), slice(vc.start, CW))[j]
            pt[pr, pc] = _dot_nt(hp_ref[pr, :], wb[pc, :])

        row = lax.broadcasted_iota(jnp.int32, (L, L), 0)
        col = lax.broadcasted_iota(jnp.int32, (L, L), 1)
        tri = row >= col
        fresh = t % CPB == 0
        cst = {0: [jnp.where(fresh, 0.0, c_scr[i]) for i in range(CG)]}
        nst = {0: [jnp.where(fresh, 0.0, n_scr[i]) for i in range(CG)]}
        mst = {0: [jnp.where(fresh, 0.0, m_scr[i, 0:1, 0:1]) for i in range(CG)]}
        v = {}
        gates = {}

        def gate(k):
            cs = range(k * PCH, (k + 1) * PCH)
            for c in cs:
                gt_scr[c] = (pt[rows(c), gc] + bg_ref[...]).T
            pad = jnp.zeros((8 - CG * PCH, L), F32)
            ig_rows = jnp.concatenate(
                [gt_scr[c, pl.ds(grp * CG + i, 1), :] for i in range(CG) for c in cs] + [pad], axis=0)
            lf_rows = jnp.concatenate(
                [_log_sigmoid(gt_scr[c, pl.ds(grp * CG + i + H_C, 1), :]) for i in range(CG) for c in cs]
                + [pad], axis=0)
            b_rows = _dot_hi(lf_rows, jnp.where(row <= col, 1.0, 0.0))
            tall = jnp.zeros((L - 8, L), F32)
            gates[k] = dict(ig_rows=ig_rows, b_rows=b_rows,
                            b_cols=jnp.concatenate([b_rows, tall], axis=0).T,
                            ig_cols=jnp.concatenate([ig_rows, tall], axis=0).T)

        def stage_a(c):
            gk = gates[c // PCH]
            v[c] = []
            nst[c + 1] = []
            mst[c + 1] = []
            for i in range(CG):
                r = i * PCH + c % PCH
                b_r = gk["b_rows"][r:r + 1, :]
                ig_r = gk["ig_rows"][r:r + 1, :]
                b_c = gk["b_cols"][:, r:r + 1]
                ig_c = gk["ig_cols"][:, r:r + 1]
                m_prev = mst[c][i]
                log_d = jnp.where(tri, b_c - b_r + ig_r, NEG_INF)
                log_inter = b_c + m_prev
                m_t = jnp.maximum(log_inter, jnp.max(log_d, axis=-1, keepdims=True))
                m_new = m_t[L - 1:L, :]
                b_last = b_c[L - 1:L, :]
                w_end = jnp.exp(b_last - b_c + ig_c - m_new)
                cd = jnp.exp(b_last + m_prev - m_new)
                q = pt[rows(c), head(qc, i, DQK_C)] * (DQK_C ** -0.5)
                k = pt[rows(c), head(kc, i, DQK_C)]
                vv = pt[rows(c), head(vc, i, DV_C)]
                nst[c + 1].append(cd * nst[c][i] + jnp.sum(w_end * k, axis=0, keepdims=True))
                mst[c + 1].append(m_new)
                v[c].append(dict(w=jnp.exp(log_d - m_t), w_inter=jnp.exp(log_inter - m_t),
                                 floor=jnp.exp(-m_t), cd=cd, qb=q.astype(BF16), kb=k.astype(BF16),
                                 vb=vv.astype(BF16), vw=(vv * w_end).astype(BF16),
                                 qn=jnp.sum(q * nst[c][i], axis=-1, keepdims=True)))

        def stage_b(c):
            for i, d in enumerate(v[c]):
                d["sc"] = _dot_nt(d["qb"], d["kb"])
                d["upd"] = _dot_tn(d["vw"], d["kb"])
            for i, d in enumerate(v[c]):
                d["inter"] = _dot_nt(d["qb"], cst[c][i].astype(BF16))

        def stage_c(c):
            cst[c + 1] = []
            for i, d in enumerate(v[c]):
                sc = d["sc"] * d["w"]
                d["den"] = jnp.sum(sc, axis=-1, keepdims=True) + d["w_inter"] * d["qn"]
                d["sc"] = sc.astype(BF16)
                cst[c + 1].append(d["cd"] * cst[c][i] + d["upd"])

        def stage_d(c):
            for d in v[c]:
                d["num"] = _dot(d["sc"], d["vb"])

        def stage_e(c):
            for i, d in enumerate(v.pop(c)):
                num = d["num"] + d["w_inter"] * d["inter"]
                hh = num / jnp.maximum(jnp.abs(d["den"]), d["floor"])
                z = pt[rows(c), head(zc, i, DV_C)]
                ys = slice(i * DV_C, (i + 1) * DV_C)
                yc_ref[rows(c), ys] = (_head_norm(hh, gm_ref[0:1, ys]) * _silu(z)).astype(BF16)

        _chunk_pipeline(n, piece, (stage_a, stage_b, stage_c, stage_d, stage_e), gate)
        for i in range(CG):
            c_scr[i] = cst[n][i]
            n_scr[i] = nst[n][i]
            m_scr[i] = jnp.broadcast_to(mst[n][i], (8, 128))
            c_ref[0, i] = cst[n][i]
            n_ref[0, i] = nst[n][i]
            m_ref[0, i] = jnp.broadcast_to(mst[n][i], (1, 128))


def _odd_heads(hp, hs, w_t, bg, gm):
    k = hp.shape[1]
    ms = hs.shape[0]
    tile = lambda s: jnp.maximum(s - 1, 0)
    qw, vw = CG * DQK_C, CG * DV_C
    koff = H_C * DQK_C // qw
    voff = 2 * H_C * DQK_C // vw
    zoff = (2 * H_C * DQK_C + W_C) // vw
    seq = lambda g, s: (tile(s) // CPB, g, 0, 0)
    once = pl.Buffered(1)
    return pl.pallas_call(
        _odd_heads_kernel,
        grid=(H_C // CG, CN + 1),
        in_specs=[pl.BlockSpec((CT, k), lambda g, s: (tile(s), 0)),
                  pl.BlockSpec((ms, k), lambda g, s: (0, 0), pipeline_mode=once),
                  pl.BlockSpec((qw, k), lambda g, s: (g, 0), pipeline_mode=once),
                  pl.BlockSpec((qw, k), lambda g, s: (koff + g, 0), pipeline_mode=once),
                  pl.BlockSpec((vw, k), lambda g, s: (voff + g, 0), pipeline_mode=once),
                  pl.BlockSpec((vw, k), lambda g, s: (zoff + g, 0), pipeline_mode=once),
                  pl.BlockSpec((128, k), lambda g, s: (O_GATE // 128, 0), pipeline_mode=once),
                  pl.BlockSpec((1, 128), lambda g, s: (0, 0)),
                  pl.BlockSpec((1, vw), lambda g, s: (0, g))],
        out_specs=(pl.BlockSpec((CT, vw), lambda g, s: (tile(s), g)),
                   pl.BlockSpec((1, CG, DV_C, DQK_C), seq),
                   pl.BlockSpec((1, CG, 1, DQK_C), seq),
                   pl.BlockSpec((1, CG, 1, 128), seq),
                   pl.BlockSpec((ms, qw), lambda g, s: (0, g)),
                   pl.BlockSpec((ms, qw), lambda g, s: (0, g)),
                   pl.BlockSpec((ms, vw), lambda g, s: (0, g)),
                   pl.BlockSpec((ms, vw), lambda g, s: (0, g))),
        out_shape=(jax.ShapeDtypeStruct((BATCH * SEQ, W_C), BF16),
                   jax.ShapeDtypeStruct((BATCH, H_C, DV_C, DQK_C), F32),
                   jax.ShapeDtypeStruct((BATCH, H_C, 1, DQK_C), F32),
                   jax.ShapeDtypeStruct((BATCH, H_C, 1, 128), F32),
                   jax.ShapeDtypeStruct((ms, H_C * DQK_C), F32),
                   jax.ShapeDtypeStruct((ms, H_C * DQK_C), F32),
                   jax.ShapeDtypeStruct((ms, W_C), F32),
                   jax.ShapeDtypeStruct((ms, W_C), F32)),
        scratch_shapes=[pltpu.VMEM((CW, k), BF16),
                        pltpu.VMEM((CT, CW), F32),
                        pltpu.VMEM((CG, DV_C, DQK_C), F32),
                        pltpu.VMEM((CG, 1, DQK_C), F32),
                        pltpu.VMEM((CG, 8, 128), F32),
                        pltpu.VMEM((CT // CHUNK, CHUNK, CHUNK), F32)],
        compiler_params=_params(("arbitrary", "arbitrary")),
        name="odd_heads",
    )(hp, hs, w_t, w_t, w_t, w_t, w_t, bg, gm)


DT = 512


def _odd_mlp_fused_kernel(hp_ref, w0_ref, w1_ref, w2_ref, wt_ref, lng_ref, lnb_ref, ws_ref, bst_ref,
                          yd_ref, wb, pt, wsb):
    s = pl.program_id(0)
    L = CHUNK
    uc = slice(0, W_D)
    vc = slice(W_D, 2 * W_D)
    zc = slice(2 * W_D, 3 * W_D)

    @pl.when(s == 0)
    def _():
        sh = N_GATE
        wb[uc, :] = jnp.concatenate([w0_ref[sh:, :], w1_ref[0:sh, :]], axis=0).astype(BF16)
        wb[vc, :] = jnp.concatenate([w1_ref[sh:, :], w2_ref[0:sh, :]], axis=0).astype(BF16)
        wb[zc, :] = jnp.concatenate([w2_ref[sh:, :], wt_ref[...]], axis=0).astype(BF16)
        keep = (lax.broadcasted_iota(jnp.int32, (L, L), 0) >= lax.broadcasted_iota(jnp.int32, (L, L), 1))
        for g in range(G_D):
            wsb[g] = jnp.where(keep, ws_ref[g], 0.0).astype(BF16)

    @pl.when(s > 0)
    def _():
        n = DT // L
        rows = lambda c: slice(c * L, (c + 1) * L)
        grp = lambda sl, g: slice(sl.start + g * 128, sl.start + (g + 1) * 128)
        vn = {}
        mix = {}

        def project(pc):
            pt[:, pc] = _dot_nt(hp_ref[...], wb[pc, :])

        def stage_a(c):
            dv = lambda g: pt[rows(c), grp(vc, g)]
            tot = dv(0)
            for g in range(1, G_D):
                tot = tot + dv(g)
            mu = jnp.sum(tot, axis=-1, keepdims=True) * (1.0 / W_D)
            sq = (dv(0) - mu) * (dv(0) - mu)
            for g in range(1, G_D):
                sq = sq + (dv(g) - mu) * (dv(g) - mu)
            rstd = lax.rsqrt(jnp.sum(sq, axis=-1, keepdims=True) * (1.0 / W_D) + EPS)
            vn[c] = [((dv(g) - mu) * rstd * lng_ref[0:1, g * 128:(g + 1) * 128]
                      + lnb_ref[0:1, g * 128:(g + 1) * 128]).astype(BF16) for g in range(G_D)]

        def stage_b(c):
            mix[c] = [_dot(wsb[g], vn[c][g]) for g in range(G_D)]

        def stage_e(c):
            for g in range(G_D):
                sg = mix[c][g] + bst_ref[:, g:g + 1]
                d_u = pt[rows(c), grp(uc, g)]
                d_z = pt[rows(c), grp(zc, g)]
                yd_ref[rows(c), g * 128:(g + 1) * 128] = (d_u * sg * _silu(d_z)).astype(BF16)

        project(vc)
        for c in range(n):
            stage_a(c)
        project(uc)
        for c in range(n):
            stage_b(c)
        project(zc)
        for c in range(n):
            stage_e(c)


def _odd_mlp_fused(hp, w_t, lng, lnb, ws, bst):
    k = hp.shape[1]
    m = hp.shape[0]
    tile = lambda s: (jnp.maximum(s - 1, 0), 0)
    const2 = lambda s: (0, 0)
    t0 = O_GATE // IN_TN
    once = pl.Buffered(1)
    wspec = lambda j: pl.BlockSpec((IN_TN, k), lambda s: (t0 + j, 0), pipeline_mode=once)
    return pl.pallas_call(
        _odd_mlp_fused_kernel,
        grid=(m // DT + 1,),
        in_specs=[pl.BlockSpec((DT, k), tile),
                  wspec(0), wspec(1), wspec(2),
                  pl.BlockSpec((N_GATE, k), lambda s: ((t0 + 3) * (IN_TN // N_GATE), 0), pipeline_mode=once),
                  pl.BlockSpec((1, W_D), const2),
                  pl.BlockSpec((1, W_D), const2),
                  pl.BlockSpec((G_D, CHUNK, CHUNK), lambda s: (0, 0, 0)),
                  pl.BlockSpec((CHUNK, G_D), const2)],
        out_specs=pl.BlockSpec((DT, W_D), tile),
        out_shape=jax.ShapeDtypeStruct((m, W_D), BF16),
        scratch_shapes=[pltpu.VMEM((3 * W_D, k), BF16),
                        pltpu.VMEM((DT, 3 * W_D), F32),
                        pltpu.VMEM((G_D, CHUNK, CHUNK), BF16)],
        compiler_params=_params(("arbitrary",)),
        name="odd_mlp_fused",
    )(hp, w_t, w_t, w_t, w_t, lng, lnb, ws, bst)


def _in_proj_rows_kernel(h_ref, w_ref, wn_ref, o_ref, *, shift):
    wsh = jnp.concatenate([w_ref[shift:, :], wn_ref[...]], axis=0)
    o_ref[...] = _dot_nt(h_ref[...], wsh.astype(BF16))


def _in_proj_rows(h, w_t, n_out, shift, tile0):
    ms, k = h.shape
    return pl.pallas_call(
        functools.partial(_in_proj_rows_kernel, shift=shift),
        grid=(n_out // IN_TN,),
        in_specs=[pl.BlockSpec((ms, k), lambda j: (0, 0)),
                  pl.BlockSpec((IN_TN, k), lambda j: (j + tile0, 0)),
                  pl.BlockSpec((shift, k), lambda j: ((j + tile0 + 1) * (IN_TN // shift), 0))],
        out_specs=pl.BlockSpec((ms, IN_TN), lambda j: (0, j)),
        out_shape=jax.ShapeDtypeStruct((ms, n_out), F32),
        compiler_params=_params(("arbitrary",)),
        name="in_proj_rows",
    )(h, w_t, w_t)


def _odd_sample_kernel(pq_ref, pk_ref, pv_ref, pz_ref, h_ref, wg_ref, pd_ref,
                       c_ref, nrow_ref, mrow_ref, bg_ref, gm_ref, lng_ref, lnb_ref,
                       wt_ref, bt_ref,
                       yc_ref, yd_ref, vn_ref, co_ref, no_ref, mo_ref,
                       inter_scr):
    h = pl.program_id(1)
    row = lax.broadcasted_iota(jnp.int32, (SR, SR), 0)
    col = lax.broadcasted_iota(jnp.int32, (SR, SR), 1)
    trow = row & 3

    @pl.when(h == 0)
    def _():
        mask_d = jnp.where((row >> 2) == (col >> 2), trow - (col & 3), -1) >= 0
        dv = pd_ref[:, W_D:2 * W_D]
        mu = jnp.mean(dv, axis=-1, keepdims=True)
        xc = dv - mu
        var = jnp.mean(xc * xc, axis=-1, keepdims=True)
        rstd = lax.rsqrt(var + EPS)
        for g in range(G_D):
            sl = slice(g * 128, (g + 1) * 128)
            vn = xc[:, sl] * rstd * lng_ref[0:1, sl] + lnb_ref[0:1, sl]
            vn_ref[:, sl] = vn
            wmix = jnp.where(mask_d, wt_ref[g], 0.0).astype(BF16)
            s = _dot(wmix, vn.astype(BF16)) + bt_ref[:, g:g + 1]
            d_u = pd_ref[:, g * 128:(g + 1) * 128]
            d_z = pd_ref[:, 2 * W_D + g * 128:2 * W_D + (g + 1) * 128]
            yd_ref[:, sl] = (d_u * s * _silu(d_z)).astype(BF16)

    same = (row >> 2) == (col >> 2)
    mask = jnp.where(same, trow - (col & 3), -1) >= 0
    pre = _dot_nt(h_ref[...], wg_ref[...].astype(BF16)) + bg_ref[...]
    lf = _log_sigmoid(pre)
    b_full = _dot_hi(jnp.where(mask, 1.0, 0.0), lf)
    sel_i = col == h
    sel_f = col == h + H_C
    ig_c = jnp.sum(jnp.where(sel_i, pre, 0.0), axis=-1, keepdims=True)
    b_c = jnp.sum(jnp.where(sel_f, b_full, 0.0), axis=-1, keepdims=True)
    sel_ir = row == h
    sel_fr = row == h + H_C
    ig_r = jnp.sum(jnp.where(sel_ir, pre.T, 0.0), axis=0, keepdims=True)
    b_r = jnp.sum(jnp.where(sel_fr, b_full.T, 0.0), axis=0, keepdims=True)
    m_prev = mrow_ref[0]
    log_d = jnp.where(mask, b_c - b_r + ig_r, NEG_INF)
    log_inter = b_c + m_prev
    m_t = jnp.maximum(log_inter, jnp.max(log_d, axis=-1, keepdims=True))
    w = jnp.exp(log_d - m_t)
    w_inter = jnp.exp(log_inter - m_t)
    q = pq_ref[...] * (DQK_C ** -0.5)
    k = pk_ref[...]
    v = pv_ref[...]
    qb = q.astype(BF16)
    kb = k.astype(BF16)
    vb = v.astype(BF16)
    sc = _dot_nt(qb, kb) * w
    sub8 = lax.broadcasted_iota(jnp.int32, (8, DQK_C), 0)
    for g in range(SR // 8):
        q8 = q[8 * g:8 * g + 8, :]
        q2 = jnp.concatenate([jnp.where(sub8 < DEC_SEQ, q8, 0.0), jnp.where(sub8 < DEC_SEQ, 0.0, q8)], axis=1)
        c_pair = jnp.concatenate([c_ref[2 * g + beta, 0].astype(BF16) for beta in range(2)], axis=1)
        inter_scr[8 * g:8 * g + 8, :] = _dot_nt(q2.astype(BF16), c_pair)
    n_rows = nrow_ref[0]
    num = _dot(sc.astype(BF16), vb) + w_inter * inter_scr[...]
    den = jnp.sum(sc, axis=-1, keepdims=True) + w_inter * jnp.sum(q * n_rows, axis=-1, keepdims=True)
    hh = num / jnp.maximum(jnp.abs(den), jnp.exp(-m_t))
    yc_ref[...] = (_head_norm(hh, gm_ref[...]) * _silu(pz_ref[...])).astype(BF16)

    stats = jnp.where(col == 0, m_t, jnp.where(col == 1, b_c, 0.0))
    last = _dot_hi(jnp.where(col == (row | 3), 1.0, 0.0), stats)
    m_new = last[:, 0:1]
    b_last = last[:, 1:2]
    w_end = jnp.exp(b_last - b_c + ig_c - m_new)
    cd = jnp.exp(b_last + m_prev - m_new)
    mo_ref[0] = m_new
    no_ref[0] = cd * n_rows + _dot_hi(jnp.where(same, 1.0, 0.0), w_end * k)
    vwt = (v * w_end).T
    lane_b = lax.broadcasted_iota(jnp.int32, (DV_C, SR), 1) >> 2
    for b in range(SB):
        lhs = jnp.where(lane_b == b, vwt, 0.0).astype(BF16)
        cd_b = cd[4 * b + 3:4 * b + 4, :]
        co_ref[b, 0] = cd_b * c_ref[b, 0] + _dot(lhs, kb)


def _odd_sample(pc, pd, h, w_o, c_state, n_rows, m_rows, bg, gm, lng, lnb, wt, bt):
    nb = DEC_BATCH // SB
    const2 = lambda i, h: (0, 0)
    return pl.pallas_call(
        _odd_sample_kernel,
        grid=(nb, H_C),
        in_specs=[pl.BlockSpec((SR, DQK_C), lambda i, h: (i, h)),
                  pl.BlockSpec((SR, DQK_C), lambda i, h: (i, h)),
                  pl.BlockSpec((SR, DV_C), lambda i, h: (i, h)),
                  pl.BlockSpec((SR, DV_C), lambda i, h: (i, h)),
                  pl.BlockSpec((SR, D_MODEL), lambda i, h: (i, 0)),
                  pl.BlockSpec((128, D_MODEL), lambda i, h: (O_GATE // 128, 0)),
                  pl.BlockSpec((SR, 3 * W_D), lambda i, h: (i, 0)),
                  pl.BlockSpec((SB, 1, DV_C, DQK_C), lambda i, h: (i, h, 0, 0)),
                  pl.BlockSpec((1, SR, DQK_C), lambda i, h: (h, i, 0)),
                  pl.BlockSpec((1, SR, 1), lambda i, h: (h, i, 0)),
                  pl.BlockSpec((1, 128), const2),
                  pl.BlockSpec((1, DV_C), lambda i, h: (0, h)),
                  pl.BlockSpec((1, W_D), const2),
                  pl.BlockSpec((1, W_D), const2),
                  pl.BlockSpec((G_D, SR, SR), lambda i, h: (0, 0, 0)),
                  pl.BlockSpec((SR, G_D), const2)],
        out_specs=(pl.BlockSpec((SR, DV_C), lambda i, h: (i, h)),
                   pl.BlockSpec((SR, W_D), lambda i, h: (i, 0)),
                   pl.BlockSpec((SR, W_D), lambda i, h: (i, 0)),
                   pl.BlockSpec((SB, 1, DV_C, DQK_C), lambda i, h: (i, h, 0, 0)),
                   pl.BlockSpec((1, SR, DQK_C), lambda i, h: (h, i, 0)),
                   pl.BlockSpec((1, SR, 1), lambda i, h: (h, i, 0))),
        out_shape=(jax.ShapeDtypeStruct((DEC_BATCH * DEC_SEQ, W_C), BF16),
                   jax.ShapeDtypeStruct((DEC_BATCH * DEC_SEQ, W_D), BF16),
                   jax.ShapeDtypeStruct((DEC_BATCH * DEC_SEQ, W_D), F32),
                   jax.ShapeDtypeStruct((DEC_BATCH, H_C, DV_C, DQK_C), F32),
                   jax.ShapeDtypeStruct((H_C, DEC_BATCH * DEC_SEQ, DQK_C), F32),
                   jax.ShapeDtypeStruct((H_C, DEC_BATCH * DEC_SEQ, 1), F32)),
        scratch_shapes=[pltpu.VMEM((SR, DV_C), F32)],
        compiler_params=_params(("arbitrary", "arbitrary")),
        name="odd_sample",
    )(*pc, h, w_o, pd, c_state, n_rows, m_rows, bg, gm, lng, lnb, wt, bt)


def _rope_tables(pos):
    inv = ROPE_BASE ** (-jnp.arange(0, DH_B, 2, dtype=F32) / DH_B)
    ang = pos.astype(F32)[:, None] * inv[None, :]
    cos = jnp.cos(ang)
    sin = jnp.sin(ang)
    return jnp.concatenate([cos, cos], axis=-1), jnp.concatenate([-sin, sin], axis=-1)


def kernel(x_prompt, x_sample, state_conv, state_ret, state_mlstm_C, state_mlstm_n, state_mlstm_m,
           norm_even, w_in_even, conv_w, ret_norm, w_out_even,
           norm_odd, w_in_odd, b_gate_odd, mlstm_norm, ln_v_g, ln_v_b,
           w_spatial, b_spatial, w_out_odd, norm_final):
    w_in_e = w_in_even[0]
    w_out_e = w_out_even[0].astype(BF16)
    w_o = w_in_odd[0].T
    w_out_o = w_out_odd[0].astype(BF16)
    g_even = norm_even[0][None, :]
    g_odd = norm_odd[0][None, :]
    g_fin = norm_final[None, :]
    cw = conv_w[0]
    g_ret = ret_norm[0][None, :]
    bg = jnp.concatenate([b_gate_odd[0], jnp.zeros((128 - 2 * H_C,), F32)])[None, :]
    gm = mlstm_norm[0][None, :]
    lng = ln_v_g[0][None, :]
    lnb = ln_v_b[0][None, :]
    ws = w_spatial[0]
    bst = b_spatial[0].T

    cos_p, sin_p = _rope_tables(jnp.arange(SEQ, dtype=jnp.int32))
    cos_s, sin_s = _rope_tables(PAST_LEN + jnp.arange(DEC_SEQ, dtype=jnp.int32))
    cos_s = jnp.tile(cos_s, (SB, 1))
    sin_s = jnp.tile(sin_s, (SB, 1))
    lg_tab = jnp.broadcast_to(jnp.asarray(LOG_GAMMA, F32)[:, None, None], (H_B, 1, 128))

    wt_s = jnp.tile(ws[:, :DEC_SEQ, :DEC_SEQ], (1, SB, SB))
    bt_s = jnp.tile(b_spatial[0][:, :DEC_SEQ].T, (SB, 1))

    rs = DEC_BATCH * DEC_SEQ
    xp = x_prompt.reshape(BATCH * SEQ, D_MODEL)
    xs = x_sample.reshape(rs, D_MODEL)
    hp, hs = _norm_cast(xp, xs, g_even, 512)
    ya, conv_p, *ps_a = _even_conv(hp, hs, w_in_e, cw)
    yb, ret_p, *ps_b = _even_heads(hp, hs, w_in_e, g_ret, cos_p, sin_p, lg_tab)
    st_exp = jnp.pad(state_conv[0], ((0, 0), (0, DEC_SEQ - (CONV_W - 1)), (0, 0))).reshape(rs, W_A)
    ya_s, u_s, yb_s, ret_s = _even_sample(ps_a, ps_b, st_exp, state_ret[0], cw, g_ret, cos_s, sin_s, lg_tab)
    x1, h1, x1s, h1s = _outproj(ya, yb, xp, ya_s, yb_s, xs, w_out_e, g_odd, final=False)

    yc, c_p, n_p, m_p, *ps_c = _odd_heads(h1, h1s, w_o, bg, gm)
    yd = _odd_mlp_fused(h1, w_o, lng, lnb, ws, bst)
    pd_s = _in_proj_rows(h1s, w_o, 3 * W_D, N_GATE, O_GATE // IN_TN)
    n_rows = jnp.repeat(jnp.transpose(state_mlstm_n[0], (1, 0, 2)), DEC_SEQ, axis=1)
    m_rows = jnp.repeat(state_mlstm_m[0].T, DEC_SEQ, axis=1)[:, :, None]
    yc_s, yd_s, vn_s, c_s, no_s, mo_s = _odd_sample(
        ps_c, pd_s, h1s, w_o, state_mlstm_C[0], n_rows, m_rows, bg, gm, lng, lnb, wt_s, bt_s)
    y_prompt, y_sample = _outproj(yc, yd, x1, yc_s, yd_s, x1s, w_out_o, g_fin, final=True)

    conv_s = u_s.reshape(DEC_BATCH, DEC_SEQ, W_A)[:, DEC_SEQ - (CONV_W - 1):, :]
    n_s = jnp.transpose(no_s[:, DEC_SEQ - 1::DEC_SEQ, :], (1, 0, 2))
    m_s = mo_s[:, DEC_SEQ - 1::DEC_SEQ, 0].T
    return (y_prompt.reshape(BATCH, SEQ, D_MODEL),
            y_sample.reshape(DEC_BATCH, DEC_SEQ, D_MODEL),
            conv_p[None], conv_s[None],
            ret_p[None], ret_s[None],
            c_p[None], c_s[None],
            n_p[:, :, 0, :][None], n_s[None],
            m_p[:, :, 0, 0][None], m_s[None],
            vn_s.reshape(DEC_BATCH, DEC_SEQ, W_D)[None])
```

```python
import functools
import math

import jax
import jax.numpy as jnp
from jax import lax
from jax.experimental import pallas as pl
from jax.experimental.pallas import tpu as pltpu

F32 = jnp.float32
BF16 = jnp.bfloat16

D_MODEL = 2048
BATCH = 4
SEQ = 2048
DEC_BATCH = 128
DEC_SEQ = 4
PAST_LEN = 16384
W_A = 1024
CONV_W = 3
W_B = 1024
H_B = 8
DH_B = 128
W_C = 1024
H_C = 4
DV_C = 256
DQK_C = 128
W_D = 1024
G_D = 8
CHUNK = 128
O_GATE = 2 * H_C * DQK_C + 2 * W_C
N_GATE = 2 * H_C
ROPE_BASE = 10000.0
EPS = 1e-6
LOG_GAMMA = tuple(math.log(1.0 - 2.0 ** (-5.0 - h)) for h in range(H_B))
NEG_INF = float("-inf")
VMEM_LIMIT = 56 * 1024 * 1024

NT_DIMS = (((1,), (1,)), ((), ()))
TN_DIMS = (((0,), (0,)), ((), ()))


def _silu(z):
    return z * (1.0 / (1.0 + jnp.exp(-z)))


def _log_sigmoid(x):
    return jnp.minimum(x, 0.0) - jnp.log1p(jnp.exp(-jnp.abs(x)))


def _dot(a, b):
    return jnp.dot(a, b, preferred_element_type=F32)


def _dot_nt(a, b):
    return lax.dot_general(a, b, NT_DIMS, preferred_element_type=F32)


def _dot_tn(a, b):
    return lax.dot_general(a, b, TN_DIMS, preferred_element_type=F32)


def _dot_hi(a, b):
    return jnp.dot(a, b, preferred_element_type=F32, precision=lax.Precision.HIGHEST)


def _head_norm(o, g):
    mu = jnp.mean(o, axis=-1, keepdims=True)
    oc = o - mu
    var = jnp.mean(oc * oc, axis=-1, keepdims=True)
    return oc * lax.rsqrt(var + EPS) * g


def _params(sem):
    return pltpu.CompilerParams(dimension_semantics=sem, vmem_limit_bytes=VMEM_LIMIT)


def _norm_cast_kernel(x_ref, xs_ref, g_ref, h_ref, hs_ref, *, n_prompt):
    i = pl.program_id(0)

    def tile(x_in, h_out):
        x = x_in[...]
        ms = jnp.mean(x * x, axis=-1, keepdims=True)
        h_out[...] = (x * lax.rsqrt(ms + EPS) * g_ref[...]).astype(BF16)

    @pl.when(i < n_prompt)
    def _():
        tile(x_ref, h_ref)

    @pl.when(i == n_prompt)
    def _():
        tile(xs_ref, hs_ref)


def _norm_cast(x, xs, g, tm):
    m, d = x.shape
    ms = xs.shape[0]
    n_prompt = m // tm
    row = lambda i: (jnp.minimum(i, n_prompt - 1), 0)
    const = lambda i: (0, 0)
    return pl.pallas_call(
        functools.partial(_norm_cast_kernel, n_prompt=n_prompt),
        grid=(n_prompt + 1,),
        in_specs=[pl.BlockSpec((tm, d), row),
                  pl.BlockSpec((ms, d), const),
                  pl.BlockSpec((1, d), const)],
        out_specs=(pl.BlockSpec((tm, d), row), pl.BlockSpec((ms, d), const)),
        out_shape=(jax.ShapeDtypeStruct((m, d), BF16), jax.ShapeDtypeStruct((ms, d), BF16)),
        compiler_params=_params(("arbitrary",)),
        name="norm_cast",
    )(x, xs, g)


IN_TN = 1024


OUT_TM = 512


def _outproj_kernel(ya_ref, yb_ref, x_ref, yas_ref, ybs_ref, xs_ref, w_ref, g_ref, *out_refs,
                    final, n_prompt):
    i = pl.program_id(0)
    half = ya_ref.shape[1]
    n_out = 1 if final else 2

    def tile(ya, yb, x, outs):
        acc = _dot(ya[...], w_ref[0:half, :]) + _dot(yb[...], w_ref[half:2 * half, :])
        x1 = x[...] + acc
        ms = jnp.mean(x1 * x1, axis=-1, keepdims=True)
        hn = x1 * lax.rsqrt(ms + EPS) * g_ref[...]
        if final:
            outs[0][...] = hn
        else:
            outs[0][...] = x1
            outs[1][...] = hn.astype(BF16)

    @pl.when(i < n_prompt)
    def _():
        tile(ya_ref, yb_ref, x_ref, out_refs[:n_out])

    @pl.when(i == n_prompt)
    def _():
        tile(yas_ref, ybs_ref, xs_ref, out_refs[n_out:])


def _outproj(ya, yb, x, ya_s, yb_s, x_s, w, g, final):
    m, half = ya.shape
    ms = ya_s.shape[0]
    d = w.shape[1]
    n_prompt = m // OUT_TM
    row = lambda i: (jnp.minimum(i, n_prompt - 1), 0)
    const = lambda i: (0, 0)
    once = pl.Buffered(1)
    shapes = [jax.ShapeDtypeStruct((m, d), F32), jax.ShapeDtypeStruct((ms, d), F32)]
    specs = [pl.BlockSpec((OUT_TM, d), row), pl.BlockSpec((ms, d), const)]
    if not final:
        shapes = [shapes[0], jax.ShapeDtypeStruct((m, d), BF16), shapes[1], jax.ShapeDtypeStruct((ms, d), BF16)]
        specs = [specs[0], pl.BlockSpec((OUT_TM, d), row), specs[1], pl.BlockSpec((ms, d), const)]
    return pl.pallas_call(
        functools.partial(_outproj_kernel, final=final, n_prompt=n_prompt),
        grid=(n_prompt + 1,),
        in_specs=[pl.BlockSpec((OUT_TM, half), row),
                  pl.BlockSpec((OUT_TM, half), row),
                  pl.BlockSpec((OUT_TM, d), row),
                  pl.BlockSpec((ms, half), const, pipeline_mode=once),
                  pl.BlockSpec((ms, half), const, pipeline_mode=once),
                  pl.BlockSpec((ms, d), const, pipeline_mode=once),
                  pl.BlockSpec((2 * half, d), const, pipeline_mode=once),
                  pl.BlockSpec((1, d), const)],
        out_specs=tuple(specs),
        out_shape=tuple(shapes),
        compiler_params=_params(("arbitrary",)),
        name="out_proj_final" if final else "out_proj",
    )(ya, yb, x, ya_s, yb_s, x_s, w, g)


def _rope(x, cosf, sins):
    return x * cosf + pltpu.roll(x, DH_B // 2, 1) * sins


FT = 1024
FN = BATCH * SEQ // FT
FPB = SEQ // FT
HG = 2
GW = HG * 128
PCH = 2
CONV_PCH = 4


def _chunk_pipeline(n, piece, stages, gate=None):
    sa, sb, sc, sd, se = stages
    npieces = n // PCH
    for j in range(PCH):
        piece(0, j)
    if gate is not None:
        gate(0)
    for c in range(n + 2):
        k, j = c // PCH + 1, c % PCH
        if k < npieces:
            piece(k, j)
        if c < n:
            sa(c)
        if 1 <= c <= n:
            sc(c - 1)
        if c < n:
            sb(c)
        if 1 <= c <= n:
            sd(c - 1)
        if c >= 2:
            se(c - 2)
        if gate is not None and j == PCH - 1 and k < npieces:
            gate(k)


def _even_heads_kernel(hp_ref, hs_ref, wq_ref, wk_ref, wv_ref, wz_ref, gret_ref, cos_ref, sin_ref, lg_ref,
                       yb_ref, s_ref, sq_ref, sk_ref, sv_ref, sz_ref, wb, pt, s_scr):
    s = pl.program_id(1)
    L = CHUNK

    @pl.when(s == 0)
    def _():
        for part, w_ref in enumerate((wq_ref, wk_ref, wv_ref, wz_ref)):
            wb[:, part * GW:(part + 1) * GW] = w_ref[...].astype(BF16)
        ps = _dot(hs_ref[...], wb[...])
        for part, o_ref in enumerate((sq_ref, sk_ref, sv_ref, sz_ref)):
            o_ref[...] = ps[:, part * GW:(part + 1) * GW]

    @pl.when(s > 0)
    def _():
        t = s - 1
        n = FT // L
        rows = lambda c: slice(c * L, (c + 1) * L)
        cols = lambda part, i: slice(part * GW + i * DH_B, part * GW + (i + 1) * DH_B)

        def piece(k, j):
            pr = slice(k * PCH * L, (k + 1) * PCH * L)
            pc = slice(j * 2 * GW, (j + 1) * 2 * GW)
            pt[pr, pc] = _dot(hp_ref[pr, :], wb[:, pc])

        row = lax.broadcasted_iota(jnp.int32, (L, L), 0)
        col = lax.broadcasted_iota(jnp.int32, (L, L), 1)
        causal = row >= col
        diff = jnp.maximum(row - col, 0).astype(F32)
        ti = lax.broadcasted_iota(jnp.int32, (L, 1), 0).astype(F32)
        lgs = [lg_ref[i][:, 0:1] for i in range(HG)]
        decay = [jnp.where(causal, jnp.exp(lg * diff), 0.0) for lg in lgs]
        q_decay = [jnp.exp(lg * (ti + 1.0)) for lg in lgs]
        k_decay = [jnp.exp(lg * (L - 1.0 - ti)) for lg in lgs]
        gamma_l = [jnp.exp(lg * float(L)) for lg in lgs]
        state = {0: [jnp.where(t % FPB == 0, 0.0, s_scr[i]) for i in range(HG)]}
        v = {}

        def stage_a(c):
            cosf = cos_ref[rows(c), :]
            sins = sin_ref[rows(c), :]
            v[c] = []
            for i in range(HG):
                kr = _rope(pt[rows(c), cols(1, i)], cosf, sins) * (DH_B ** -0.5)
                v[c].append(dict(qb=_rope(pt[rows(c), cols(0, i)], cosf, sins).astype(BF16),
                                 kb=kr.astype(BF16),
                                 kd=(kr * k_decay[i]).astype(BF16),
                                 vb=pt[rows(c), cols(2, i)].astype(BF16)))

        def stage_b(c):
            for i, d in enumerate(v[c]):
                d["sc"] = _dot_nt(d["qb"], d["kb"])
                d["upd"] = _dot_tn(d["kd"], d["vb"])
            for i, d in enumerate(v[c]):
                d["cross"] = _dot(d["qb"], state[c][i].astype(BF16))

        def stage_c(c):
            state[c + 1] = []
            for i, d in enumerate(v[c]):
                d["sc"] = (d["sc"] * decay[i]).astype(BF16)
                state[c + 1].append(gamma_l[i] * state[c][i] + d["upd"])

        def stage_d(c):
            for d in v[c]:
                d["inner"] = _dot(d["sc"], d["vb"])

        def stage_e(c):
            for i, d in enumerate(v.pop(c)):
                o = d["inner"] + d["cross"] * q_decay[i]
                g = gret_ref[0:1, i * DH_B:(i + 1) * DH_B]
                z = pt[rows(c), cols(3, i)]
                yb_ref[rows(c), i * DH_B:(i + 1) * DH_B] = (_head_norm(o, g) * _silu(z)).astype(BF16)

        _chunk_pipeline(n, piece, (stage_a, stage_b, stage_c, stage_d, stage_e))
        for i in range(HG):
            s_scr[i] = state[n][i]
            s_ref[0, i] = state[n][i]


def _even_heads(hp, hs, w, g_ret, cosf, sins, lg_tab):
    k = hp.shape[1]
    ms = hs.shape[0]
    ng = H_B // HG
    base = 4 * W_A // GW
    tile = lambda s: jnp.maximum(s - 1, 0)
    wspec = lambda part: pl.BlockSpec((k, GW), lambda g, s: (0, base + part * ng + g))
    sspec = pl.BlockSpec((ms, GW), lambda g, s: (0, g))
    sshape = jax.ShapeDtypeStruct((ms, W_B), F32)
    return pl.pallas_call(
        _even_heads_kernel,
        grid=(ng, FN + 1),
        in_specs=[pl.BlockSpec((FT, k), lambda g, s: (tile(s), 0)),
                  pl.BlockSpec((ms, k), lambda g, s: (0, 0)),
                  wspec(0), wspec(1), wspec(2), wspec(3),
                  pl.BlockSpec((1, GW), lambda g, s: (0, g)),
                  pl.BlockSpec((FT, DH_B), lambda g, s: (tile(s) % FPB, 0)),
                  pl.BlockSpec((FT, DH_B), lambda g, s: (tile(s) % FPB, 0)),
                  pl.BlockSpec((HG, 1, 128), lambda g, s: (g, 0, 0))],
        out_specs=(pl.BlockSpec((FT, GW), lambda g, s: (tile(s), g)),
                   pl.BlockSpec((1, HG, DH_B, DH_B), lambda g, s: (tile(s) // FPB, g, 0, 0)),
                   sspec, sspec, sspec, sspec),
        out_shape=(jax.ShapeDtypeStruct((BATCH * SEQ, W_B), BF16),
                   jax.ShapeDtypeStruct((BATCH, H_B, DH_B, DH_B), F32),
                   sshape, sshape, sshape, sshape),
        scratch_shapes=[pltpu.VMEM((k, 4 * GW), BF16),
                        pltpu.VMEM((FT, 4 * GW), F32),
                        pltpu.VMEM((HG, DH_B, DH_B), F32)],
        compiler_params=_params(("arbitrary", "arbitrary")),
        name="even_heads",
    )(hp, hs, w, w, w, w, g_ret, cosf, sins, lg_tab)


def _even_conv_kernel(hp_ref, hs_ref, wb_ref, wc_ref, wx_ref, wz_ref, cw_ref,
                      ya_ref, conv_ref, sb_ref, sc_ref, sx_ref, sz_ref, wb, pt, ubuf):
    s = pl.program_id(1)
    L = CHUNK

    @pl.when(s == 0)
    def _():
        for part, w_ref in enumerate((wb_ref, wc_ref, wx_ref, wz_ref)):
            wb[:, part * GW:(part + 1) * GW] = w_ref[...].astype(BF16)
        ps = _dot(hs_ref[...], wb[...])
        for part, o_ref in enumerate((sb_ref, sc_ref, sx_ref, sz_ref)):
            o_ref[...] = ps[:, part * GW:(part + 1) * GW]

    @pl.when(s > 0)
    def _():
        t = s - 1
        n = FT // L
        rows = lambda c: slice(c * L, (c + 1) * L)
        part = lambda p, c: pt[rows(c), p * GW:(p + 1) * GW]

        @pl.when(t % FPB == 0)
        def _():
            ubuf[0:8, :] = jnp.zeros((8, GW), F32)

        @pl.when(t % FPB != 0)
        def _():
            ubuf[0:8, :] = ubuf[FT:FT + 8, :]

        def piece(k, j):
            pr = slice(k * CONV_PCH * L, (k + 1) * CONV_PCH * L)
            pc = slice(j * 2 * GW, (j + 1) * 2 * GW)
            pt[pr, pc] = _dot(hp_ref[pr, :], wb[:, pc])

        todo = [(k, j) for k in range(n // CONV_PCH) for j in range(2)]
        piece(*todo.pop(0))
        piece(*todo.pop(0))
        for c in range(n):
            if todo:
                piece(*todo.pop(0))
            u = part(1, c) * part(2, c)
            ubuf[8 + c * L:8 + (c + 1) * L, :] = u
            t0 = ubuf[6 + c * L:6 + (c + 1) * L, :]
            t1 = ubuf[7 + c * L:7 + (c + 1) * L, :]
            conv = cw_ref[0:1, :] * t0 + cw_ref[1:2, :] * t1 + cw_ref[2:3, :] * u
            ya_ref[rows(c), :] = (part(0, c) * conv * _silu(part(3, c))).astype(BF16)
        conv_ref[0] = ubuf[FT + 6:FT + 8, :]


def _even_conv(hp, hs, w, conv_w):
    k = hp.shape[1]
    ms = hs.shape[0]
    ng = W_A // GW
    tile = lambda s: jnp.maximum(s - 1, 0)
    wspec = lambda part: pl.BlockSpec((k, GW), lambda g, s: (0, part * ng + g))
    sspec = pl.BlockSpec((ms, GW), lambda g, s: (0, g))
    sshape = jax.ShapeDtypeStruct((ms, W_A), F32)
    return pl.pallas_call(
        _even_conv_kernel,
        grid=(ng, FN + 1),
        in_specs=[pl.BlockSpec((FT, k), lambda g, s: (tile(s), 0)),
                  pl.BlockSpec((ms, k), lambda g, s: (0, 0)),
                  wspec(0), wspec(1), wspec(2), wspec(3),
                  pl.BlockSpec((CONV_W, GW), lambda g, s: (0, g))],
        out_specs=(pl.BlockSpec((FT, GW), lambda g, s: (tile(s), g)),
                   pl.BlockSpec((1, CONV_W - 1, GW), lambda g, s: (tile(s) // FPB, 0, g)),
                   sspec, sspec, sspec, sspec),
        out_shape=(jax.ShapeDtypeStruct((BATCH * SEQ, W_A), BF16),
                   jax.ShapeDtypeStruct((BATCH, CONV_W - 1, W_A), F32),
                   sshape, sshape, sshape, sshape),
        scratch_shapes=[pltpu.VMEM((k, 4 * GW), BF16),
                        pltpu.VMEM((FT, 4 * GW), F32),
                        pltpu.VMEM((FT + 8, GW), F32)],
        compiler_params=_params(("arbitrary", "arbitrary")),
        name="even_conv",
    )(hp, hs, w, w, w, w, conv_w)


SB = 32
SR = SB * DEC_SEQ


def _even_sample_kernel(ab_ref, ac_ref, ax_ref, az_ref, pq_ref, pk_ref, pv_ref, pz_ref, st_ref, s_ref,
                        cw_ref, gret_ref, cos_ref, sin_ref, lg_ref,
                        ya_ref, u_ref, yb_ref, so_ref, cross_scr):
    h = pl.program_id(1)
    row = lax.broadcasted_iota(jnp.int32, (SR, SR), 0)
    col = lax.broadcasted_iota(jnp.int32, (SR, SR), 1)
    trow = row & 3

    @pl.when(h == 0)
    def _():
        for j in range(W_A // 128):
            sl = slice(j * 128, (j + 1) * 128)
            a_b = ab_ref[:, sl]
            a_c = ac_ref[:, sl]
            a_x = ax_ref[:, sl]
            a_z = az_ref[:, sl]
            u = a_c * a_x
            e = st_ref[:, sl]
            tap1 = jnp.where(trow >= 1, pltpu.roll(u, 1, 0), pltpu.roll(e, SR - 1, 0))
            tap0 = jnp.where(trow >= 2, pltpu.roll(u, 2, 0), e)
            conv = cw_ref[0:1, sl] * tap0 + cw_ref[1:2, sl] * tap1 + cw_ref[2:3, sl] * u
            ya_ref[:, sl] = (a_b * conv * _silu(a_z)).astype(BF16)
            u_ref[:, sl] = u

    lg = lg_ref[0][:, 0:1]
    same = (row >> 2) == (col >> 2)
    dd = trow - (col & 3)
    mask = jnp.where(same, dd, -1) >= 0
    decay = jnp.where(mask, jnp.exp(lg * jnp.maximum(dd, 0).astype(F32)), 0.0)
    tcol = (lax.broadcasted_iota(jnp.int32, (SR, 1), 0) & 3).astype(F32)
    cosf = cos_ref[...]
    sins = sin_ref[...]
    qr = _rope(pq_ref[...], cosf, sins)
    kr = _rope(pk_ref[...], cosf, sins) * (DH_B ** -0.5)
    qb = qr.astype(BF16)
    kb = kr.astype(BF16)
    vb = pv_ref[...].astype(BF16)
    sc = _dot_nt(qb, kb) * decay
    inner = _dot(sc.astype(BF16), vb)
    kdt = (kr * jnp.exp(lg * (DEC_SEQ - 1.0 - tcol))).T
    gamma_l = jnp.exp(lg * float(DEC_SEQ))
    lane_b = col >> 2
    sub = lax.broadcasted_iota(jnp.int32, (8, DH_B), 0)
    for g in range(SR // 8):
        q8 = qr[8 * g:8 * g + 8, :]
        q2 = jnp.concatenate([jnp.where(sub < DEC_SEQ, q8, 0.0), jnp.where(sub < DEC_SEQ, 0.0, q8)], axis=1)
        s_pair = [s_ref[2 * g + beta, 0] for beta in range(2)]
        cross_scr[8 * g:8 * g + 8, :] = _dot(
            q2.astype(BF16), jnp.concatenate([sp.astype(BF16) for sp in s_pair], axis=0))
        for beta in range(2):
            b = 2 * g + beta
            lhs = jnp.where(lane_b == b, kdt, 0.0).astype(BF16)
            so_ref[b, 0] = gamma_l * s_pair[beta] + _dot(lhs, vb)
    o = inner + cross_scr[...] * jnp.exp(lg * (tcol + 1.0))
    yb_ref[...] = (_head_norm(o, gret_ref[...]) * _silu(pz_ref[...])).astype(BF16)


def _even_sample(pa, pb, st_exp, s_state, conv_w, g_ret, cosf, sins, lg_tab):
    nb = DEC_BATCH // SB
    const2 = lambda i, h: (0, 0)
    aspec = pl.BlockSpec((SR, W_A), lambda i, h: (i, 0))
    hspec = pl.BlockSpec((SR, DH_B), lambda i, h: (i, h))
    return pl.pallas_call(
        _even_sample_kernel,
        grid=(nb, H_B),
        in_specs=[aspec, aspec, aspec, aspec,
                  hspec, hspec, hspec, hspec,
                  pl.BlockSpec((SR, W_A), lambda i, h: (i, 0)),
                  pl.BlockSpec((SB, 1, DH_B, DH_B), lambda i, h: (i, h, 0, 0)),
                  pl.BlockSpec((CONV_W, W_A), const2),
                  pl.BlockSpec((1, DH_B), lambda i, h: (0, h)),
                  pl.BlockSpec((SR, DH_B), const2),
                  pl.BlockSpec((SR, DH_B), const2),
                  pl.BlockSpec((1, 1, 128), lambda i, h: (h, 0, 0))],
        out_specs=(pl.BlockSpec((SR, W_A), lambda i, h: (i, 0)),
                   pl.BlockSpec((SR, W_A), lambda i, h: (i, 0)),
                   pl.BlockSpec((SR, DH_B), lambda i, h: (i, h)),
                   pl.BlockSpec((SB, 1, DH_B, DH_B), lambda i, h: (i, h, 0, 0))),
        out_shape=(jax.ShapeDtypeStruct((DEC_BATCH * DEC_SEQ, W_A), BF16),
                   jax.ShapeDtypeStruct((DEC_BATCH * DEC_SEQ, W_A), F32),
                   jax.ShapeDtypeStruct((DEC_BATCH * DEC_SEQ, W_B), BF16),
                   jax.ShapeDtypeStruct((DEC_BATCH, H_B, DH_B, DH_B), F32)),
        scratch_shapes=[pltpu.VMEM((SR, DH_B), F32)],
        compiler_params=_params(("arbitrary", "arbitrary")),
        name="even_sample",
    )(*pa, *pb, st_exp, s_state, conv_w, g_ret, cosf, sins, lg_tab)


CG = 2
CW = 128 + CG * (2 * DQK_C + 2 * DV_C)
CT = 1024
CN = BATCH * SEQ // CT
CPB = SEQ // CT


def _odd_heads_kernel(hp_ref, hs_ref, wq_ref, wk_ref, wv_ref, wz_ref, wg_ref, bg_ref, gm_ref,
                      yc_ref, c_ref, n_ref, m_ref, sq_ref, sk_ref, sv_ref, sz_ref,
                      wb, pt, c_scr, n_scr, m_scr, gt_scr):
    grp = pl.program_id(0)
    s = pl.program_id(1)
    L = CHUNK
    gc = slice(0, 128)
    qc = slice(128, 128 + CG * DQK_C)
    kc = slice(qc.stop, qc.stop + CG * DQK_C)
    vc = slice(kc.stop, kc.stop + CG * DV_C)
    zc = slice(vc.stop, vc.stop + CG * DV_C)
    head = lambda sl, i, w: slice(sl.start + i * w, sl.start + (i + 1) * w)

    @pl.when(s == 0)
    def _():
        wb[qc, :] = wq_ref[...].astype(BF16)
        wb[kc, :] = wk_ref[...].astype(BF16)
        wb[vc, :] = wv_ref[...].astype(BF16)
        wb[zc, :] = wz_ref[...].astype(BF16)
        wb[gc, :] = wg_ref[...].astype(BF16)
        ps = _dot_nt(hs_ref[...], wb[qc.start:CW, :])
        off = lambda sl: slice(sl.start - qc.start, sl.stop - qc.start)
        sq_ref[...] = ps[:, off(qc)]
        sk_ref[...] = ps[:, off(kc)]
        sv_ref[...] = ps[:, off(vc)]
        sz_ref[...] = ps[:, off(zc)]

    @pl.when(s > 0)
    def _():
        t = s - 1
        n = CT // L
        rows = lambda c: slice(c * L, (c + 1) * L)

        def piece(k, j):
            pr = slice(k * PCH * L, (k + 1) * PCH * L)
            pc = (slice(0, vc.start), slice(vc.start, CW))[j]
            pt[pr, pc] = _dot_nt(hp_ref[pr, :], wb[pc, :])

        row = lax.broadcasted_iota(jnp.int32, (L, L), 0)
        col = lax.broadcasted_iota(jnp.int32, (L, L), 1)
        tri = row >= col
        fresh = t % CPB == 0
        cst = {0: [jnp.where(fresh, 0.0, c_scr[i]) for i in range(CG)]}
        nst = {0: [jnp.where(fresh, 0.0, n_scr[i]) for i in range(CG)]}
        mst = {0: [jnp.where(fresh, 0.0, m_scr[i, 0:1, 0:1]) for i in range(CG)]}
        v = {}
        gates = {}

        def gate(k):
            cs = range(k * PCH, (k + 1) * PCH)
            for c in cs:
                gt_scr[c] = (pt[rows(c), gc] + bg_ref[...]).T
            pad = jnp.zeros((8 - CG * PCH, L), F32)
            ig_rows = jnp.concatenate(
                [gt_scr[c, pl.ds(grp * CG + i, 1), :] for i in range(CG) for c in cs] + [pad], axis=0)
            lf_rows = jnp.concatenate(
                [_log_sigmoid(gt_scr[c, pl.ds(grp * CG + i + H_C, 1), :]) for i in range(CG) for c in cs]
                + [pad], axis=0)
            b_rows = _dot_hi(lf_rows, jnp.where(row <= col, 1.0, 0.0))
            tall = jnp.zeros((L - 8, L), F32)
            gates[k] = dict(ig_rows=ig_rows, b_rows=b_rows,
                            b_cols=jnp.concatenate([b_rows, tall], axis=0).T,
                            ig_cols=jnp.concatenate([ig_rows, tall], axis=0).T)

        def stage_a(c):
            gk = gates[c // PCH]
            v[c] = []
            nst[c + 1] = []
            mst[c + 1] = []
            for i in range(CG):
                r = i * PCH + c % PCH
                b_r = gk["b_rows"][r:r + 1, :]
                ig_r = gk["ig_rows"][r:r + 1, :]
                b_c = gk["b_cols"][:, r:r + 1]
                ig_c = gk["ig_cols"][:, r:r + 1]
                m_prev = mst[c][i]
                log_d = jnp.where(tri, b_c - b_r + ig_r, NEG_INF)
                log_inter = b_c + m_prev
                m_t = jnp.maximum(log_inter, jnp.max(log_d, axis=-1, keepdims=True))
                m_new = m_t[L - 1:L, :]
                b_last = b_c[L - 1:L, :]
                w_end = jnp.exp(b_last - b_c + ig_c - m_new)
                cd = jnp.exp(b_last + m_prev - m_new)
                q = pt[rows(c), head(qc, i, DQK_C)] * (DQK_C ** -0.5)
                k = pt[rows(c), head(kc, i, DQK_C)]
                vv = pt[rows(c), head(vc, i, DV_C)]
                nst[c + 1].append(cd * nst[c][i] + jnp.sum(w_end * k, axis=0, keepdims=True))
                mst[c + 1].append(m_new)
                v[c].append(dict(w=jnp.exp(log_d - m_t), w_inter=jnp.exp(log_inter - m_t),
                                 floor=jnp.exp(-m_t), cd=cd, qb=q.astype(BF16), kb=k.astype(BF16),
                                 vb=vv.astype(BF16), vw=(vv * w_end).astype(BF16),
                                 qn=jnp.sum(q * nst[c][i], axis=-1, keepdims=True)))

        def stage_b(c):
            for i, d in enumerate(v[c]):
                d["sc"] = _dot_nt(d["qb"], d["kb"])
                d["upd"] = _dot_tn(d["vw"], d["kb"])
            for i, d in enumerate(v[c]):
                d["inter"] = _dot_nt(d["qb"], cst[c][i].astype(BF16))

        def stage_c(c):
            cst[c + 1] = []
            for i, d in enumerate(v[c]):
                sc = d["sc"] * d["w"]
                d["den"] = jnp.sum(sc, axis=-1, keepdims=True) + d["w_inter"] * d["qn"]
                d["sc"] = sc.astype(BF16)
                cst[c + 1].append(d["cd"] * cst[c][i] + d["upd"])

        def stage_d(c):
            for d in v[c]:
                d["num"] = _dot(d["sc"], d["vb"])

        def stage_e(c):
            for i, d in enumerate(v.pop(c)):
                num = d["num"] + d["w_inter"] * d["inter"]
                hh = num / jnp.maximum(jnp.abs(d["den"]), d["floor"])
                z = pt[rows(c), head(zc, i, DV_C)]
                ys = slice(i * DV_C, (i + 1) * DV_C)
                yc_ref[rows(c), ys] = (_head_norm(hh, gm_ref[0:1, ys]) * _silu(z)).astype(BF16)

        _chunk_pipeline(n, piece, (stage_a, stage_b, stage_c, stage_d, stage_e), gate)
        for i in range(CG):
            c_scr[i] = cst[n][i]
            n_scr[i] = nst[n][i]
            m_scr[i] = jnp.broadcast_to(mst[n][i], (8, 128))
            c_ref[0, i] = cst[n][i]
            n_ref[0, i] = nst[n][i]
            m_ref[0, i] = jnp.broadcast_to(mst[n][i], (1, 128))


def _odd_heads(hp, hs, w_t, bg, gm):
    k = hp.shape[1]
    ms = hs.shape[0]
    tile = lambda s: jnp.maximum(s - 1, 0)
    qw, vw = CG * DQK_C, CG * DV_C
    koff = H_C * DQK_C // qw
    voff = 2 * H_C * DQK_C // vw
    zoff = (2 * H_C * DQK_C + W_C) // vw
    seq = lambda g, s: (tile(s) // CPB, g, 0, 0)
    once = pl.Buffered(1)
    return pl.pallas_call(
        _odd_heads_kernel,
        grid=(H_C // CG, CN + 1),
        in_specs=[pl.BlockSpec((CT, k), lambda g, s: (tile(s), 0)),
                  pl.BlockSpec((ms, k), lambda g, s: (0, 0), pipeline_mode=once),
                  pl.BlockSpec((qw, k), lambda g, s: (g, 0), pipeline_mode=once),
                  pl.BlockSpec((qw, k), lambda g, s: (koff + g, 0), pipeline_mode=once),
                  pl.BlockSpec((vw, k), lambda g, s: (voff + g, 0), pipeline_mode=once),
                  pl.BlockSpec((vw, k), lambda g, s: (zoff + g, 0), pipeline_mode=once),
                  pl.BlockSpec((128, k), lambda g, s: (O_GATE // 128, 0), pipeline_mode=once),
                  pl.BlockSpec((1, 128), lambda g, s: (0, 0)),
                  pl.BlockSpec((1, vw), lambda g, s: (0, g))],
        out_specs=(pl.BlockSpec((CT, vw), lambda g, s: (tile(s), g)),
                   pl.BlockSpec((1, CG, DV_C, DQK_C), seq),
                   pl.BlockSpec((1, CG, 1, DQK_C), seq),
                   pl.BlockSpec((1, CG, 1, 128), seq),
                   pl.BlockSpec((ms, qw), lambda g, s: (0, g)),
                   pl.BlockSpec((ms, qw), lambda g, s: (0, g)),
                   pl.BlockSpec((ms, vw), lambda g, s: (0, g)),
                   pl.BlockSpec((ms, vw), lambda g, s: (0, g))),
        out_shape=(jax.ShapeDtypeStruct((BATCH * SEQ, W_C), BF16),
                   jax.ShapeDtypeStruct((BATCH, H_C, DV_C, DQK_C), F32),
                   jax.ShapeDtypeStruct((BATCH, H_C, 1, DQK_C), F32),
                   jax.ShapeDtypeStruct((BATCH, H_C, 1, 128), F32),
                   jax.ShapeDtypeStruct((ms, H_C * DQK_C), F32),
                   jax.ShapeDtypeStruct((ms, H_C * DQK_C), F32),
                   jax.ShapeDtypeStruct((ms, W_C), F32),
                   jax.ShapeDtypeStruct((ms, W_C), F32)),
        scratch_shapes=[pltpu.VMEM((CW, k), BF16),
                        pltpu.VMEM((CT, CW), F32),
                        pltpu.VMEM((CG, DV_C, DQK_C), F32),
                        pltpu.VMEM((CG, 1, DQK_C), F32),
                        pltpu.VMEM((CG, 8, 128), F32),
                        pltpu.VMEM((CT // CHUNK, CHUNK, CHUNK), F32)],
        compiler_params=_params(("arbitrary", "arbitrary")),
        name="odd_heads",
    )(hp, hs, w_t, w_t, w_t, w_t, w_t, bg, gm)


DT = 512


def _odd_mlp_fused_kernel(hp_ref, w0_ref, w1_ref, w2_ref, wt_ref, lng_ref, lnb_ref, ws_ref, bst_ref,
                          yd_ref, wb, pt, wsb):
    s = pl.program_id(0)
    L = CHUNK
    uc = slice(0, W_D)
    vc = slice(W_D, 2 * W_D)
    zc = slice(2 * W_D, 3 * W_D)

    @pl.when(s == 0)
    def _():
        sh = N_GATE
        wb[uc, :] = jnp.concatenate([w0_ref[sh:, :], w1_ref[0:sh, :]], axis=0).astype(BF16)
        wb[vc, :] = jnp.concatenate([w1_ref[sh:, :], w2_ref[0:sh, :]], axis=0).astype(BF16)
        wb[zc, :] = jnp.concatenate([w2_ref[sh:, :], wt_ref[...]], axis=0).astype(BF16)
        keep = (lax.broadcasted_iota(jnp.int32, (L, L), 0) >= lax.broadcasted_iota(jnp.int32, (L, L), 1))
        for g in range(G_D):
            wsb[g] = jnp.where(keep, ws_ref[g], 0.0).astype(BF16)

    @pl.when(s > 0)
    def _():
        n = DT // L
        rows = lambda c: slice(c * L, (c + 1) * L)
        grp = lambda sl, g: slice(sl.start + g * 128, sl.start + (g + 1) * 128)
        vn = {}
        mix = {}

        def project(pc):
            pt[:, pc] = _dot_nt(hp_ref[...], wb[pc, :])

        def stage_a(c):
            dv = lambda g: pt[rows(c), grp(vc, g)]
            tot = dv(0)
            for g in range(1, G_D):
                tot = tot + dv(g)
            mu = jnp.sum(tot, axis=-1, keepdims=True) * (1.0 / W_D)
            sq = (dv(0) - mu) * (dv(0) - mu)
            for g in range(1, G_D):
                sq = sq + (dv(g) - mu) * (dv(g) - mu)
            rstd = lax.rsqrt(jnp.sum(sq, axis=-1, keepdims=True) * (1.0 / W_D) + EPS)
            vn[c] = [((dv(g) - mu) * rstd * lng_ref[0:1, g * 128:(g + 1) * 128]
                      + lnb_ref[0:1, g * 128:(g + 1) * 128]).astype(BF16) for g in range(G_D)]

        def stage_b(c):
            mix[c] = [_dot(wsb[g], vn[c][g]) for g in range(G_D)]

        def stage_e(c):
            for g in range(G_D):
                sg = mix[c][g] + bst_ref[:, g:g + 1]
                d_u = pt[rows(c), grp(uc, g)]
                d_z = pt[rows(c), grp(zc, g)]
                yd_ref[rows(c), g * 128:(g + 1) * 128] = (d_u * sg * _silu(d_z)).astype(BF16)

        project(vc)
        for c in range(n):
            stage_a(c)
        project(uc)
        for c in range(n):
            stage_b(c)
        project(zc)
        for c in range(n):
            stage_e(c)


def _odd_mlp_fused(hp, w_t, lng, lnb, ws, bst):
    k = hp.shape[1]
    m = hp.shape[0]
    tile = lambda s: (jnp.maximum(s - 1, 0), 0)
    const2 = lambda s: (0, 0)
    t0 = O_GATE // IN_TN
    once = pl.Buffered(1)
    wspec = lambda j: pl.BlockSpec((IN_TN, k), lambda s: (t0 + j, 0), pipeline_mode=once)
    return pl.pallas_call(
        _odd_mlp_fused_kernel,
        grid=(m // DT + 1,),
        in_specs=[pl.BlockSpec((DT, k), tile),
                  wspec(0), wspec(1), wspec(2),
                  pl.BlockSpec((N_GATE, k), lambda s: ((t0 + 3) * (IN_TN // N_GATE), 0), pipeline_mode=once),
                  pl.BlockSpec((1, W_D), const2),
                  pl.BlockSpec((1, W_D), const2),
                  pl.BlockSpec((G_D, CHUNK, CHUNK), lambda s: (0, 0, 0)),
                  pl.BlockSpec((CHUNK, G_D), const2)],
        out_specs=pl.BlockSpec((DT, W_D), tile),
        out_shape=jax.ShapeDtypeStruct((m, W_D), BF16),
        scratch_shapes=[pltpu.VMEM((3 * W_D, k), BF16),
                        pltpu.VMEM((DT, 3 * W_D), F32),
                        pltpu.VMEM((G_D, CHUNK, CHUNK), BF16)],
        compiler_params=_params(("arbitrary",)),
        name="odd_mlp_fused",
    )(hp, w_t, w_t, w_t, w_t, lng, lnb, ws, bst)


def _in_proj_rows_kernel(h_ref, w_ref, wn_ref, o_ref, *, shift):
    wsh = jnp.concatenate([w_ref[shift:, :], wn_ref[...]], axis=0)
    o_ref[...] = _dot_nt(h_ref[...], wsh.astype(BF16))


def _in_proj_rows(h, w_t, n_out, shift, tile0):
    ms, k = h.shape
    return pl.pallas_call(
        functools.partial(_in_proj_rows_kernel, shift=shift),
        grid=(n_out // IN_TN,),
        in_specs=[pl.BlockSpec((ms, k), lambda j: (0, 0)),
                  pl.BlockSpec((IN_TN, k), lambda j: (j + tile0, 0)),
                  pl.BlockSpec((shift, k), lambda j: ((j + tile0 + 1) * (IN_TN // shift), 0))],
        out_specs=pl.BlockSpec((ms, IN_TN), lambda j: (0, j)),
        out_shape=jax.ShapeDtypeStruct((ms, n_out), F32),
        compiler_params=_params(("arbitrary",)),
        name="in_proj_rows",
    )(h, w_t, w_t)


def _odd_sample_kernel(pq_ref, pk_ref, pv_ref, pz_ref, h_ref, wg_ref, pd_ref,
                       c_ref, nrow_ref, mrow_ref, bg_ref, gm_ref, lng_ref, lnb_ref,
                       wt_ref, bt_ref,
                       yc_ref, yd_ref, vn_ref, co_ref, no_ref, mo_ref,
                       inter_scr):
    h = pl.program_id(1)
    row = lax.broadcasted_iota(jnp.int32, (SR, SR), 0)
    col = lax.broadcasted_iota(jnp.int32, (SR, SR), 1)
    trow = row & 3

    @pl.when(h == 0)
    def _():
        mask_d = jnp.where((row >> 2) == (col >> 2), trow - (col & 3), -1) >= 0
        rep = jnp.where(trow == col, 1.0, 0.0).astype(BF16)
        dv = pd_ref[:, W_D:2 * W_D]
        mu = jnp.mean(dv, axis=-1, keepdims=True)
        xc = dv - mu
        var = jnp.mean(xc * xc, axis=-1, keepdims=True)
        rstd = lax.rsqrt(var + EPS)
        for g in range(G_D):
            sl = slice(g * 128, (g + 1) * 128)
            vn = xc[:, sl] * rstd * lng_ref[0:1, sl] + lnb_ref[0:1, sl]
            vn_ref[:, sl] = vn
            wtile = _dot_nt(_dot(rep, wt_ref[g].astype(BF16)).astype(BF16), rep)
            wmix = jnp.where(mask_d, wtile, 0.0).astype(BF16)
            s = _dot(wmix, vn.astype(BF16)) + bt_ref[:, g:g + 1]
            d_u = pd_ref[:, g * 128:(g + 1) * 128]
            d_z = pd_ref[:, 2 * W_D + g * 128:2 * W_D + (g + 1) * 128]
            yd_ref[:, sl] = (d_u * s * _silu(d_z)).astype(BF16)

    same = (row >> 2) == (col >> 2)
    mask = jnp.where(same, trow - (col & 3), -1) >= 0
    pre = _dot_nt(h_ref[...], wg_ref[...].astype(BF16)) + bg_ref[...]
    lf = _log_sigmoid(pre)
    b_full = _dot_hi(jnp.where(mask, 1.0, 0.0), lf)
    sel_i = col == h
    sel_f = col == h + H_C
    ig_c = jnp.sum(jnp.where(sel_i, pre, 0.0), axis=-1, keepdims=True)
    b_c = jnp.sum(jnp.where(sel_f, b_full, 0.0), axis=-1, keepdims=True)
    sel_ir = row == h
    sel_fr = row == h + H_C
    ig_r = jnp.sum(jnp.where(sel_ir, pre.T, 0.0), axis=0, keepdims=True)
    b_r = jnp.sum(jnp.where(sel_fr, b_full.T, 0.0), axis=0, keepdims=True)
    m_prev = mrow_ref[0]
    log_d = jnp.where(mask, b_c - b_r + ig_r, NEG_INF)
    log_inter = b_c + m_prev
    m_t = jnp.maximum(log_inter, jnp.max(log_d, axis=-1, keepdims=True))
    w = jnp.exp(log_d - m_t)
    w_inter = jnp.exp(log_inter - m_t)
    q = pq_ref[...] * (DQK_C ** -0.5)
    k = pk_ref[...]
    v = pv_ref[...]
    qb = q.astype(BF16)
    kb = k.astype(BF16)
    vb = v.astype(BF16)
    sc = _dot_nt(qb, kb) * w
    sub8 = lax.broadcasted_iota(jnp.int32, (8, DQK_C), 0)
    for g in range(SR // 8):
        q8 = q[8 * g:8 * g + 8, :]
        q2 = jnp.concatenate([jnp.where(sub8 < DEC_SEQ, q8, 0.0), jnp.where(sub8 < DEC_SEQ, 0.0, q8)], axis=1)
        c_pair = jnp.concatenate([c_ref[2 * g + beta, 0].astype(BF16) for beta in range(2)], axis=1)
        inter_scr[8 * g:8 * g + 8, :] = _dot_nt(q2.astype(BF16), c_pair)
    n_rows = nrow_ref[0]
    num = _dot(sc.astype(BF16), vb) + w_inter * inter_scr[...]
    den = jnp.sum(sc, axis=-1, keepdims=True) + w_inter * jnp.sum(q * n_rows, axis=-1, keepdims=True)
    hh = num / jnp.maximum(jnp.abs(den), jnp.exp(-m_t))
    yc_ref[...] = (_head_norm(hh, gm_ref[...]) * _silu(pz_ref[...])).astype(BF16)

    stats = jnp.where(col == 0, m_t, jnp.where(col == 1, b_c, 0.0))
    last = _dot_hi(jnp.where(col == (row | 3), 1.0, 0.0), stats)
    m_new = last[:, 0:1]
    b_last = last[:, 1:2]
    w_end = jnp.exp(b_last - b_c + ig_c - m_new)
    cd = jnp.exp(b_last + m_prev - m_new)
    mo_ref[0] = m_new
    no_ref[0] = cd * n_rows + _dot_hi(jnp.where(same, 1.0, 0.0), w_end * k)
    vwt = (v * w_end).T
    lane_b = lax.broadcasted_iota(jnp.int32, (DV_C, SR), 1) >> 2
    for b in range(SB):
        lhs = jnp.where(lane_b == b, vwt, 0.0).astype(BF16)
        cd_b = cd[4 * b + 3:4 * b + 4, :]
        co_ref[b, 0] = cd_b * c_ref[b, 0] + _dot(lhs, kb)


def _odd_sample(pc, pd, h, w_o, c_state, n_rows, m_rows, bg, gm, lng, lnb, wt, bt):
    nb = DEC_BATCH // SB
    const2 = lambda i, h: (0, 0)
    return pl.pallas_call(
        _odd_sample_kernel,
        grid=(nb, H_C),
        in_specs=[pl.BlockSpec((SR, DQK_C), lambda i, h: (i, h)),
                  pl.BlockSpec((SR, DQK_C), lambda i, h: (i, h)),
                  pl.BlockSpec((SR, DV_C), lambda i, h: (i, h)),
                  pl.BlockSpec((SR, DV_C), lambda i, h: (i, h)),
                  pl.BlockSpec((SR, D_MODEL), lambda i, h: (i, 0)),
                  pl.BlockSpec((128, D_MODEL), lambda i, h: (O_GATE // 128, 0)),
                  pl.BlockSpec((SR, 3 * W_D), lambda i, h: (i, 0)),
                  pl.BlockSpec((SB, 1, DV_C, DQK_C), lambda i, h: (i, h, 0, 0)),
                  pl.BlockSpec((1, SR, DQK_C), lambda i, h: (h, i, 0)),
                  pl.BlockSpec((1, SR, 1), lambda i, h: (h, i, 0)),
                  pl.BlockSpec((1, 128), const2),
                  pl.BlockSpec((1, DV_C), lambda i, h: (0, h)),
                  pl.BlockSpec((1, W_D), const2),
                  pl.BlockSpec((1, W_D), const2),
                  pl.BlockSpec((G_D, SR, SR), lambda i, h: (0, 0, 0)),
                  pl.BlockSpec((SR, G_D), const2)],
        out_specs=(pl.BlockSpec((SR, DV_C), lambda i, h: (i, h)),
                   pl.BlockSpec((SR, W_D), lambda i, h: (i, 0)),
                   pl.BlockSpec((SR, W_D), lambda i, h: (i, 0)),
                   pl.BlockSpec((SB, 1, DV_C, DQK_C), lambda i, h: (i, h, 0, 0)),
                   pl.BlockSpec((1, SR, DQK_C), lambda i, h: (h, i, 0)),
                   pl.BlockSpec((1, SR, 1), lambda i, h: (h, i, 0))),
        out_shape=(jax.ShapeDtypeStruct((DEC_BATCH * DEC_SEQ, W_C), BF16),
                   jax.ShapeDtypeStruct((DEC_BATCH * DEC_SEQ, W_D), BF16),
                   jax.ShapeDtypeStruct((DEC_BATCH * DEC_SEQ, W_D), F32),
                   jax.ShapeDtypeStruct((DEC_BATCH, H_C, DV_C, DQK_C), F32),
                   jax.ShapeDtypeStruct((H_C, DEC_BATCH * DEC_SEQ, DQK_C), F32),
                   jax.ShapeDtypeStruct((H_C, DEC_BATCH * DEC_SEQ, 1), F32)),
        scratch_shapes=[pltpu.VMEM((SR, DV_C), F32)],
        compiler_params=_params(("arbitrary", "arbitrary")),
        name="odd_sample",
    )(*pc, h, w_o, pd, c_state, n_rows, m_rows, bg, gm, lng, lnb, wt, bt)


def _rope_tables(pos):
    inv = ROPE_BASE ** (-jnp.arange(0, DH_B, 2, dtype=F32) / DH_B)
    ang = pos.astype(F32)[:, None] * inv[None, :]
    cos = jnp.cos(ang)
    sin = jnp.sin(ang)
    return jnp.concatenate([cos, cos], axis=-1), jnp.concatenate([-sin, sin], axis=-1)


def kernel(x_prompt, x_sample, state_conv, state_ret, state_mlstm_C, state_mlstm_n, state_mlstm_m,
           norm_even, w_in_even, conv_w, ret_norm, w_out_even,
           norm_odd, w_in_odd, b_gate_odd, mlstm_norm, ln_v_g, ln_v_b,
           w_spatial, b_spatial, w_out_odd, norm_final):
    w_in_e = w_in_even[0]
    w_out_e = w_out_even[0].astype(BF16)
    w_o = w_in_odd[0].T
    w_out_o = w_out_odd[0].astype(BF16)
    g_even = norm_even[0][None, :]
    g_odd = norm_odd[0][None, :]
    g_fin = norm_final[None, :]
    cw = conv_w[0]
    g_ret = ret_norm[0][None, :]
    bg = jnp.concatenate([b_gate_odd[0], jnp.zeros((128 - 2 * H_C,), F32)])[None, :]
    gm = mlstm_norm[0][None, :]
    lng = ln_v_g[0][None, :]
    lnb = ln_v_b[0][None, :]
    ws = w_spatial[0]
    bst = b_spatial[0].T

    cos_p, sin_p = _rope_tables(jnp.arange(SEQ, dtype=jnp.int32))
    cos_s, sin_s = _rope_tables(PAST_LEN + jnp.arange(DEC_SEQ, dtype=jnp.int32))
    cos_s = jnp.tile(cos_s, (SB, 1))
    sin_s = jnp.tile(sin_s, (SB, 1))
    lg_tab = jnp.broadcast_to(jnp.asarray(LOG_GAMMA, F32)[:, None, None], (H_B, 1, 128))

    bt_s = jnp.tile(b_spatial[0][:, :DEC_SEQ].T, (SB, 1))

    rs = DEC_BATCH * DEC_SEQ
    xp = x_prompt.reshape(BATCH * SEQ, D_MODEL)
    xs = x_sample.reshape(rs, D_MODEL)
    hp, hs = _norm_cast(xp, xs, g_even, 512)
    ya, conv_p, *ps_a = _even_conv(hp, hs, w_in_e, cw)
    yb, ret_p, *ps_b = _even_heads(hp, hs, w_in_e, g_ret, cos_p, sin_p, lg_tab)
    st_exp = jnp.pad(state_conv[0], ((0, 0), (0, DEC_SEQ - (CONV_W - 1)), (0, 0))).reshape(rs, W_A)
    ya_s, u_s, yb_s, ret_s = _even_sample(ps_a, ps_b, st_exp, state_ret[0], cw, g_ret, cos_s, sin_s, lg_tab)
    x1, h1, x1s, h1s = _outproj(ya, yb, xp, ya_s, yb_s, xs, w_out_e, g_odd, final=False)

    yc, c_p, n_p, m_p, *ps_c = _odd_heads(h1, h1s, w_o, bg, gm)
    yd = _odd_mlp_fused(h1, w_o, lng, lnb, ws, bst)
    pd_s = _in_proj_rows(h1s, w_o, 3 * W_D, N_GATE, O_GATE // IN_TN)
    n_rows = jnp.repeat(jnp.transpose(state_mlstm_n[0], (1, 0, 2)), DEC_SEQ, axis=1)
    m_rows = jnp.repeat(state_mlstm_m[0].T, DEC_SEQ, axis=1)[:, :, None]
    yc_s, yd_s, vn_s, c_s, no_s, mo_s = _odd_sample(
        ps_c, pd_s, h1s, w_o, state_mlstm_C[0], n_rows, m_rows, bg, gm, lng, lnb, ws, bt_s)
    y_prompt, y_sample = _outproj(yc, yd, x1, yc_s, yd_s, x1s, w_out_o, g_fin, final=True)

    conv_s = u_s.reshape(DEC_BATCH, DEC_SEQ, W_A)[:, DEC_SEQ - (CONV_W - 1):, :]
    n_s = jnp.transpose(no_s[:, DEC_SEQ - 1::DEC_SEQ, :], (1, 0, 2))
    m_s = mo_s[:, DEC_SEQ - 1::DEC_SEQ, 0].T
    return (y_prompt.reshape(BATCH, SEQ, D_MODEL),
            y_sample.reshape(DEC_BATCH, DEC_SEQ, D_MODEL),
            conv_p[None], conv_s[None],
            ret_p[None], ret_s[None],
            c_p[None], c_s[None],
            n_p[:, :, 0, :][None], n_s[None],
            m_p[:, :, 0, 0][None], m_s[None],
            vn_s.reshape(DEC_BATCH, DEC_SEQ, W_D)[None])
```

```python
import functools
import math

import jax
import jax.numpy as jnp
from jax import lax
from jax.experimental import pallas as pl
from jax.experimental.pallas import tpu as pltpu

F32 = jnp.float32
BF16 = jnp.bfloat16

D_MODEL = 2048
BATCH = 4
SEQ = 2048
DEC_BATCH = 128
DEC_SEQ = 4
PAST_LEN = 16384
W_A = 1024
CONV_W = 3
W_B = 1024
H_B = 8
DH_B = 128
W_C = 1024
H_C = 4
DV_C = 256
DQK_C = 128
W_D = 1024
G_D = 8
CHUNK = 128
O_GATE = 2 * H_C * DQK_C + 2 * W_C
N_GATE = 2 * H_C
ROPE_BASE = 10000.0
EPS = 1e-6
LOG_GAMMA = tuple(math.log(1.0 - 2.0 ** (-5.0 - h)) for h in range(H_B))
NEG_INF = float("-inf")
VMEM_LIMIT = 56 * 1024 * 1024

NT_DIMS = (((1,), (1,)), ((), ()))
TN_DIMS = (((0,), (0,)), ((), ()))


def _silu(z):
    return z * (1.0 / (1.0 + jnp.exp(-z)))


def _log_sigmoid(x):
    return jnp.minimum(x, 0.0) - jnp.log1p(jnp.exp(-jnp.abs(x)))


def _dot(a, b):
    return jnp.dot(a, b, preferred_element_type=F32)


def _dot_nt(a, b):
    return lax.dot_general(a, b, NT_DIMS, preferred_element_type=F32)


def _dot_tn(a, b):
    return lax.dot_general(a, b, TN_DIMS, preferred_element_type=F32)


def _dot_hi(a, b):
    return jnp.dot(a, b, preferred_element_type=F32, precision=lax.Precision.HIGHEST)


def _head_norm(o, g):
    mu = jnp.mean(o, axis=-1, keepdims=True)
    oc = o - mu
    var = jnp.mean(oc * oc, axis=-1, keepdims=True)
    return oc * lax.rsqrt(var + EPS) * g


def _params(sem):
    return pltpu.CompilerParams(dimension_semantics=sem, vmem_limit_bytes=VMEM_LIMIT)


def _norm_cast_kernel(x_ref, xs_ref, g_ref, h_ref, hs_ref, *, n_prompt):
    i = pl.program_id(0)

    def tile(x_in, h_out):
        x = x_in[...]
        ms = jnp.mean(x * x, axis=-1, keepdims=True)
        h_out[...] = (x * lax.rsqrt(ms + EPS) * g_ref[...]).astype(BF16)

    @pl.when(i < n_prompt)
    def _():
        tile(x_ref, h_ref)

    @pl.when(i == n_prompt)
    def _():
        tile(xs_ref, hs_ref)


def _norm_cast(x, xs, g, tm):
    m, d = x.shape
    ms = xs.shape[0]
    n_prompt = m // tm
    row = lambda i: (jnp.minimum(i, n_prompt - 1), 0)
    const = lambda i: (0, 0)
    return pl.pallas_call(
        functools.partial(_norm_cast_kernel, n_prompt=n_prompt),
        grid=(n_prompt + 1,),
        in_specs=[pl.BlockSpec((tm, d), row),
                  pl.BlockSpec((ms, d), const),
                  pl.BlockSpec((1, d), const)],
        out_specs=(pl.BlockSpec((tm, d), row), pl.BlockSpec((ms, d), const)),
        out_shape=(jax.ShapeDtypeStruct((m, d), BF16), jax.ShapeDtypeStruct((ms, d), BF16)),
        compiler_params=_params(("arbitrary",)),
        name="norm_cast",
    )(x, xs, g)


IN_TN = 1024


OUT_TM = 512


def _outproj_kernel(ya_ref, yb_ref, x_ref, yas_ref, ybs_ref, xs_ref, w_ref, g_ref, *out_refs,
                    final, n_prompt):
    i = pl.program_id(0)
    half = ya_ref.shape[1]
    n_out = 1 if final else 2

    def tile(ya, yb, x, outs):
        acc = _dot(ya[...], w_ref[0:half, :]) + _dot(yb[...], w_ref[half:2 * half, :])
        x1 = x[...] + acc
        ms = jnp.mean(x1 * x1, axis=-1, keepdims=True)
        hn = x1 * lax.rsqrt(ms + EPS) * g_ref[...]
        if final:
            outs[0][...] = hn
        else:
            outs[0][...] = x1
            outs[1][...] = hn.astype(BF16)

    @pl.when(i < n_prompt)
    def _():
        tile(ya_ref, yb_ref, x_ref, out_refs[:n_out])

    @pl.when(i == n_prompt)
    def _():
        tile(yas_ref, ybs_ref, xs_ref, out_refs[n_out:])


def _outproj(ya, yb, x, ya_s, yb_s, x_s, w, g, final):
    m, half = ya.shape
    ms = ya_s.shape[0]
    d = w.shape[1]
    n_prompt = m // OUT_TM
    row = lambda i: (jnp.minimum(i, n_prompt - 1), 0)
    const = lambda i: (0, 0)
    once = pl.Buffered(1)
    shapes = [jax.ShapeDtypeStruct((m, d), F32), jax.ShapeDtypeStruct((ms, d), F32)]
    specs = [pl.BlockSpec((OUT_TM, d), row), pl.BlockSpec((ms, d), const)]
    if not final:
        shapes = [shapes[0], jax.ShapeDtypeStruct((m, d), BF16), shapes[1], jax.ShapeDtypeStruct((ms, d), BF16)]
        specs = [specs[0], pl.BlockSpec((OUT_TM, d), row), specs[1], pl.BlockSpec((ms, d), const)]
    return pl.pallas_call(
        functools.partial(_outproj_kernel, final=final, n_prompt=n_prompt),
        grid=(n_prompt + 1,),
        in_specs=[pl.BlockSpec((OUT_TM, half), row),
                  pl.BlockSpec((OUT_TM, half), row),
                  pl.BlockSpec((OUT_TM, d), row),
                  pl.BlockSpec((ms, half), const, pipeline_mode=once),
                  pl.BlockSpec((ms, half), const, pipeline_mode=once),
                  pl.BlockSpec((ms, d), const, pipeline_mode=once),
                  pl.BlockSpec((2 * half, d), const, pipeline_mode=once),
                  pl.BlockSpec((1, d), const)],
        out_specs=tuple(specs),
        out_shape=tuple(shapes),
        compiler_params=_params(("arbitrary",)),
        name="out_proj_final" if final else "out_proj",
    )(ya, yb, x, ya_s, yb_s, x_s, w, g)


def _rope(x, cosf, sins):
    return x * cosf + pltpu.roll(x, DH_B // 2, 1) * sins


FT = 1024
FN = BATCH * SEQ // FT
FPB = SEQ // FT
HG = 2
GW = HG * 128
PCH = 2


def _chunk_pipeline(n, piece, stages, gate=None):
    sa, sb, sc, sd, se = stages
    npieces = n // PCH
    for j in range(PCH):
        piece(0, j)
    if gate is not None:
        gate(0)
    for c in range(n + 2):
        k, j = c // PCH + 1, c % PCH
        if k < npieces:
            piece(k, j)
        if c < n:
            sa(c)
        if 1 <= c <= n:
            sc(c - 1)
        if c < n:
            sb(c)
        if 1 <= c <= n:
            sd(c - 1)
        if c >= 2:
            se(c - 2)
        if gate is not None and j == PCH - 1 and k < npieces:
            gate(k)


def _even_heads_kernel(hp_ref, hs_ref, wq_ref, wk_ref, wv_ref, wz_ref, gret_ref, cos_ref, sin_ref, lg_ref,
                       yb_ref, s_ref, sq_ref, sk_ref, sv_ref, sz_ref, wb, pt, s_scr):
    s = pl.program_id(1)
    L = CHUNK

    @pl.when(s == 0)
    def _():
        for part, w_ref in enumerate((wq_ref, wk_ref, wv_ref, wz_ref)):
            wb[:, part * GW:(part + 1) * GW] = w_ref[...].astype(BF16)
        ps = _dot(hs_ref[...], wb[...])
        for part, o_ref in enumerate((sq_ref, sk_ref, sv_ref, sz_ref)):
            o_ref[...] = ps[:, part * GW:(part + 1) * GW]

    @pl.when(s > 0)
    def _():
        t = s - 1
        n = FT // L
        rows = lambda c: slice(c * L, (c + 1) * L)
        cols = lambda part, i: slice(part * GW + i * DH_B, part * GW + (i + 1) * DH_B)

        def piece(k, j):
            pr = slice(k * PCH * L, (k + 1) * PCH * L)
            pc = slice(j * 2 * GW, (j + 1) * 2 * GW)
            pt[pr, pc] = _dot(hp_ref[pr, :], wb[:, pc])

        row = lax.broadcasted_iota(jnp.int32, (L, L), 0)
        col = lax.broadcasted_iota(jnp.int32, (L, L), 1)
        causal = row >= col
        diff = jnp.maximum(row - col, 0).astype(F32)
        ti = lax.broadcasted_iota(jnp.int32, (L, 1), 0).astype(F32)
        lgs = [lg_ref[i][:, 0:1] for i in range(HG)]
        decay = [jnp.where(causal, jnp.exp(lg * diff), 0.0) for lg in lgs]
        q_decay = [jnp.exp(lg * (ti + 1.0)) for lg in lgs]
        k_decay = [jnp.exp(lg * (L - 1.0 - ti)) for lg in lgs]
        gamma_l = [jnp.exp(lg * float(L)) for lg in lgs]
        state = {0: [jnp.where(t % FPB == 0, 0.0, s_scr[i]) for i in range(HG)]}
        v = {}

        def stage_a(c):
            cosf = cos_ref[rows(c), :]
            sins = sin_ref[rows(c), :]
            v[c] = []
            for i in range(HG):
                kr = _rope(pt[rows(c), cols(1, i)], cosf, sins) * (DH_B ** -0.5)
                v[c].append(dict(qb=_rope(pt[rows(c), cols(0, i)], cosf, sins).astype(BF16),
                                 kb=kr.astype(BF16),
                                 kd=(kr * k_decay[i]).astype(BF16),
                                 vb=pt[rows(c), cols(2, i)].astype(BF16)))

        def stage_b(c):
            for i, d in enumerate(v[c]):
                d["sc"] = _dot_nt(d["qb"], d["kb"])
                d["upd"] = _dot_tn(d["kd"], d["vb"])
            for i, d in enumerate(v[c]):
                d["cross"] = _dot(d["qb"], state[c][i].astype(BF16))

        def stage_c(c):
            state[c + 1] = []
            for i, d in enumerate(v[c]):
                d["sc"] = (d["sc"] * decay[i]).astype(BF16)
                state[c + 1].append(gamma_l[i] * state[c][i] + d["upd"])

        def stage_d(c):
            for d in v[c]:
                d["inner"] = _dot(d["sc"], d["vb"])

        def stage_e(c):
            for i, d in enumerate(v.pop(c)):
                o = d["inner"] + d["cross"] * q_decay[i]
                g = gret_ref[0:1, i * DH_B:(i + 1) * DH_B]
                z = pt[rows(c), cols(3, i)]
                yb_ref[rows(c), i * DH_B:(i + 1) * DH_B] = (_head_norm(o, g) * _silu(z)).astype(BF16)

        _chunk_pipeline(n, piece, (stage_a, stage_b, stage_c, stage_d, stage_e))
        for i in range(HG):
            s_scr[i] = state[n][i]
            s_ref[0, i] = state[n][i]


def _even_heads(hp, hs, w, g_ret, cosf, sins, lg_tab):
    k = hp.shape[1]
    ms = hs.shape[0]
    ng = H_B // HG
    base = 4 * W_A // GW
    tile = lambda s: jnp.maximum(s - 1, 0)
    wspec = lambda part: pl.BlockSpec((k, GW), lambda g, s: (0, base + part * ng + g))
    sspec = pl.BlockSpec((ms, GW), lambda g, s: (0, g))
    sshape = jax.ShapeDtypeStruct((ms, W_B), F32)
    return pl.pallas_call(
        _even_heads_kernel,
        grid=(ng, FN + 1),
        in_specs=[pl.BlockSpec((FT, k), lambda g, s: (tile(s), 0)),
                  pl.BlockSpec((ms, k), lambda g, s: (0, 0)),
                  wspec(0), wspec(1), wspec(2), wspec(3),
                  pl.BlockSpec((1, GW), lambda g, s: (0, g)),
                  pl.BlockSpec((FT, DH_B), lambda g, s: (tile(s) % FPB, 0)),
                  pl.BlockSpec((FT, DH_B), lambda g, s: (tile(s) % FPB, 0)),
                  pl.BlockSpec((HG, 1, 128), lambda g, s: (g, 0, 0))],
        out_specs=(pl.BlockSpec((FT, GW), lambda g, s: (tile(s), g)),
                   pl.BlockSpec((1, HG, DH_B, DH_B), lambda g, s: (tile(s) // FPB, g, 0, 0)),
                   sspec, sspec, sspec, sspec),
        out_shape=(jax.ShapeDtypeStruct((BATCH * SEQ, W_B), BF16),
                   jax.ShapeDtypeStruct((BATCH, H_B, DH_B, DH_B), F32),
                   sshape, sshape, sshape, sshape),
        scratch_shapes=[pltpu.VMEM((k, 4 * GW), BF16),
                        pltpu.VMEM((FT, 4 * GW), F32),
                        pltpu.VMEM((HG, DH_B, DH_B), F32)],
        compiler_params=_params(("arbitrary", "arbitrary")),
        name="even_heads",
    )(hp, hs, w, w, w, w, g_ret, cosf, sins, lg_tab)


def _even_conv_kernel(hp_ref, hs_ref, wb_ref, wc_ref, wx_ref, wz_ref, cw_ref,
                      ya_ref, conv_ref, sb_ref, sc_ref, sx_ref, sz_ref, wb, pt, ubuf):
    s = pl.program_id(1)
    L = CHUNK
    sub = lambda j: slice(j * 4 * 128, (j + 1) * 4 * 128)
    cs = lambda j, p: slice(j * 4 * 128 + p * 128, j * 4 * 128 + (p + 1) * 128)

    @pl.when(s == 0)
    def _():
        for p, w_ref in enumerate((wb_ref, wc_ref, wx_ref, wz_ref)):
            for j in range(HG):
                wb[:, cs(j, p)] = w_ref[:, j * 128:(j + 1) * 128].astype(BF16)
        ps = _dot(hs_ref[...], wb[...])
        for p, o_ref in enumerate((sb_ref, sc_ref, sx_ref, sz_ref)):
            for j in range(HG):
                o_ref[:, j * 128:(j + 1) * 128] = ps[:, cs(j, p)]

    @pl.when(s > 0)
    def _():
        t = s - 1
        n = FT // L
        rows = lambda c: slice(c * L, (c + 1) * L)

        @pl.when(t % FPB == 0)
        def _():
            ubuf[0:8, :] = jnp.zeros((8, GW), F32)

        @pl.when(t % FPB != 0)
        def _():
            ubuf[0:8, :] = ubuf[FT:FT + 8, :]

        def project(j):
            pt[:, sub(j)] = _dot(hp_ref[...], wb[:, sub(j)])

        def mix(j):
            ch = slice(j * 128, (j + 1) * 128)
            for c in range(n):
                u = pt[rows(c), cs(j, 1)] * pt[rows(c), cs(j, 2)]
                ubuf[8 + c * L:8 + (c + 1) * L, ch] = u
                t0 = ubuf[6 + c * L:6 + (c + 1) * L, ch]
                t1 = ubuf[7 + c * L:7 + (c + 1) * L, ch]
                conv = cw_ref[0:1, ch] * t0 + cw_ref[1:2, ch] * t1 + cw_ref[2:3, ch] * u
                ya_ref[rows(c), ch] = (pt[rows(c), cs(j, 0)] * conv * _silu(pt[rows(c), cs(j, 3)])).astype(BF16)

        project(0)
        for j in range(HG):
            if j + 1 < HG:
                project(j + 1)
            mix(j)
        conv_ref[0] = ubuf[FT + 6:FT + 8, :]


def _even_conv(hp, hs, w, conv_w):
    k = hp.shape[1]
    ms = hs.shape[0]
    ng = W_A // GW
    tile = lambda s: jnp.maximum(s - 1, 0)
    wspec = lambda part: pl.BlockSpec((k, GW), lambda g, s: (0, part * ng + g))
    sspec = pl.BlockSpec((ms, GW), lambda g, s: (0, g))
    sshape = jax.ShapeDtypeStruct((ms, W_A), F32)
    return pl.pallas_call(
        _even_conv_kernel,
        grid=(ng, FN + 1),
        in_specs=[pl.BlockSpec((FT, k), lambda g, s: (tile(s), 0)),
                  pl.BlockSpec((ms, k), lambda g, s: (0, 0)),
                  wspec(0), wspec(1), wspec(2), wspec(3),
                  pl.BlockSpec((CONV_W, GW), lambda g, s: (0, g))],
        out_specs=(pl.BlockSpec((FT, GW), lambda g, s: (tile(s), g)),
                   pl.BlockSpec((1, CONV_W - 1, GW), lambda g, s: (tile(s) // FPB, 0, g)),
                   sspec, sspec, sspec, sspec),
        out_shape=(jax.ShapeDtypeStruct((BATCH * SEQ, W_A), BF16),
                   jax.ShapeDtypeStruct((BATCH, CONV_W - 1, W_A), F32),
                   sshape, sshape, sshape, sshape),
        scratch_shapes=[pltpu.VMEM((k, 4 * GW), BF16),
                        pltpu.VMEM((FT, 4 * GW), F32),
                        pltpu.VMEM((FT + 8, GW), F32)],
        compiler_params=_params(("arbitrary", "arbitrary")),
        name="even_conv",
    )(hp, hs, w, w, w, w, conv_w)


SB = 32
SR = SB * DEC_SEQ


def _even_sample_kernel(ab_ref, ac_ref, ax_ref, az_ref, pq_ref, pk_ref, pv_ref, pz_ref, st_ref, s_ref,
                        cw_ref, gret_ref, cos_ref, sin_ref, lg_ref,
                        ya_ref, u_ref, yb_ref, so_ref, cross_scr):
    h = pl.program_id(1)
    row = lax.broadcasted_iota(jnp.int32, (SR, SR), 0)
    col = lax.broadcasted_iota(jnp.int32, (SR, SR), 1)
    trow = row & 3

    @pl.when(h == 0)
    def _():
        for j in range(W_A // 128):
            sl = slice(j * 128, (j + 1) * 128)
            a_b = ab_ref[:, sl]
            a_c = ac_ref[:, sl]
            a_x = ax_ref[:, sl]
            a_z = az_ref[:, sl]
            u = a_c * a_x
            e = st_ref[:, sl]
            tap1 = jnp.where(trow >= 1, pltpu.roll(u, 1, 0), pltpu.roll(e, SR - 1, 0))
            tap0 = jnp.where(trow >= 2, pltpu.roll(u, 2, 0), e)
            conv = cw_ref[0:1, sl] * tap0 + cw_ref[1:2, sl] * tap1 + cw_ref[2:3, sl] * u
            ya_ref[:, sl] = (a_b * conv * _silu(a_z)).astype(BF16)
            u_ref[:, sl] = u

    lg = lg_ref[0][:, 0:1]
    same = (row >> 2) == (col >> 2)
    dd = trow - (col & 3)
    mask = jnp.where(same, dd, -1) >= 0
    decay = jnp.where(mask, jnp.exp(lg * jnp.maximum(dd, 0).astype(F32)), 0.0)
    tcol = (lax.broadcasted_iota(jnp.int32, (SR, 1), 0) & 3).astype(F32)
    cosf = cos_ref[...]
    sins = sin_ref[...]
    qr = _rope(pq_ref[...], cosf, sins)
    kr = _rope(pk_ref[...], cosf, sins) * (DH_B ** -0.5)
    qb = qr.astype(BF16)
    kb = kr.astype(BF16)
    vb = pv_ref[...].astype(BF16)
    sc = _dot_nt(qb, kb) * decay
    inner = _dot(sc.astype(BF16), vb)
    kdt = (kr * jnp.exp(lg * (DEC_SEQ - 1.0 - tcol))).T
    gamma_l = jnp.exp(lg * float(DEC_SEQ))
    lane_b = col >> 2
    sub = lax.broadcasted_iota(jnp.int32, (8, DH_B), 0)
    for g in range(SR // 8):
        q8 = qr[8 * g:8 * g + 8, :]
        q2 = jnp.concatenate([jnp.where(sub < DEC_SEQ, q8, 0.0), jnp.where(sub < DEC_SEQ, 0.0, q8)], axis=1)
        s_pair = [s_ref[2 * g + beta, 0] for beta in range(2)]
        cross_scr[8 * g:8 * g + 8, :] = _dot(
            q2.astype(BF16), jnp.concatenate([sp.astype(BF16) for sp in s_pair], axis=0))
        for beta in range(2):
            b = 2 * g + beta
            lhs = jnp.where(lane_b == b, kdt, 0.0).astype(BF16)
            so_ref[b, 0] = gamma_l * s_pair[beta] + _dot(lhs, vb)
    o = inner + cross_scr[...] * jnp.exp(lg * (tcol + 1.0))
    yb_ref[...] = (_head_norm(o, gret_ref[...]) * _silu(pz_ref[...])).astype(BF16)


def _even_sample(pa, pb, st_exp, s_state, conv_w, g_ret, cosf, sins, lg_tab):
    nb = DEC_BATCH // SB
    const2 = lambda i, h: (0, 0)
    aspec = pl.BlockSpec((SR, W_A), lambda i, h: (i, 0))
    hspec = pl.BlockSpec((SR, DH_B), lambda i, h: (i, h))
    return pl.pallas_call(
        _even_sample_kernel,
        grid=(nb, H_B),
        in_specs=[aspec, aspec, aspec, aspec,
                  hspec, hspec, hspec, hspec,
                  pl.BlockSpec((SR, W_A), lambda i, h: (i, 0)),
                  pl.BlockSpec((SB, 1, DH_B, DH_B), lambda i, h: (i, h, 0, 0)),
                  pl.BlockSpec((CONV_W, W_A), const2),
                  pl.BlockSpec((1, DH_B), lambda i, h: (0, h)),
                  pl.BlockSpec((SR, DH_B), const2),
                  pl.BlockSpec((SR, DH_B), const2),
                  pl.BlockSpec((1, 1, 128), lambda i, h: (h, 0, 0))],
        out_specs=(pl.BlockSpec((SR, W_A), lambda i, h: (i, 0)),
                   pl.BlockSpec((SR, W_A), lambda i, h: (i, 0)),
                   pl.BlockSpec((SR, DH_B), lambda i, h: (i, h)),
                   pl.BlockSpec((SB, 1, DH_B, DH_B), lambda i, h: (i, h, 0, 0))),
        out_shape=(jax.ShapeDtypeStruct((DEC_BATCH * DEC_SEQ, W_A), BF16),
                   jax.ShapeDtypeStruct((DEC_BATCH * DEC_SEQ, W_A), F32),
                   jax.ShapeDtypeStruct((DEC_BATCH * DEC_SEQ, W_B), BF16),
                   jax.ShapeDtypeStruct((DEC_BATCH, H_B, DH_B, DH_B), F32)),
        scratch_shapes=[pltpu.VMEM((SR, DH_B), F32)],
        compiler_params=_params(("arbitrary", "arbitrary")),
        name="even_sample",
    )(*pa, *pb, st_exp, s_state, conv_w, g_ret, cosf, sins, lg_tab)


CG = 2
CW = 128 + CG * (2 * DQK_C + 2 * DV_C)
CT = 1024
CN = BATCH * SEQ // CT
CPB = SEQ // CT


def _odd_heads_kernel(hp_ref, hs_ref, wq_ref, wk_ref, wv_ref, wz_ref, wg_ref, bg_ref, gm_ref,
                      yc_ref, c_ref, n_ref, m_ref, sq_ref, sk_ref, sv_ref, sz_ref,
                      wb, pt, c_scr, n_scr, m_scr, gt_scr):
    grp = pl.program_id(0)
    s = pl.program_id(1)
    L = CHUNK
    gc = slice(0, 128)
    qc = slice(128, 128 + CG * DQK_C)
    kc = slice(qc.stop, qc.stop + CG * DQK_C)
    vc = slice(kc.stop, kc.stop + CG * DV_C)
    zc = slice(vc.stop, vc.stop + CG * DV_C)
    head = lambda sl, i, w: slice(sl.start + i * w, sl.start + (i + 1) * w)

    @pl.when(s == 0)
    def _():
        wb[qc, :] = wq_ref[...].astype(BF16)
        wb[kc, :] = wk_ref[...].astype(BF16)
        wb[vc, :] = wv_ref[...].astype(BF16)
        wb[zc, :] = wz_ref[...].astype(BF16)
        wb[gc, :] = wg_ref[...].astype(BF16)
        ps = _dot_nt(hs_ref[...], wb[qc.start:CW, :])
        off = lambda sl: slice(sl.start - qc.start, sl.stop - qc.start)
        sq_ref[...] = ps[:, off(qc)]
        sk_ref[...] = ps[:, off(kc)]
        sv_ref[...] = ps[:, off(vc)]
        sz_ref[...] = ps[:, off(zc)]

    @pl.when(s > 0)
    def _():
        t = s - 1
        n = CT // L
        rows = lambda c: slice(c * L, (c + 1) * L)

        def piece(k, j):
            pr = slice(k * PCH * L, (k + 1) * PCH * L)
            pc = (slice(0, vc.start), slice(vc.start, CW))[j]
            pt[pr, pc] = _dot_nt(hp_ref[pr, :], wb[pc, :])

        row = lax.broadcasted_iota(jnp.int32, (L, L), 0)
        col = lax.broadcasted_iota(jnp.int32, (L, L), 1)
        tri = row >= col
        fresh = t % CPB == 0
        cst = {0: [jnp.where(fresh, 0.0, c_scr[i]) for i in range(CG)]}
        nst = {0: [jnp.where(fresh, 0.0, n_scr[i]) for i in range(CG)]}
        mst = {0: [jnp.where(fresh, 0.0, m_scr[i, 0:1, 0:1]) for i in range(CG)]}
        v = {}
        gates = {}

        def gate(k):
            cs = range(k * PCH, (k + 1) * PCH)
            for c in cs:
                gt_scr[c] = (pt[rows(c), gc] + bg_ref[...]).T
            pad = jnp.zeros((8 - CG * PCH, L), F32)
            ig_rows = jnp.concatenate(
                [gt_scr[c, pl.ds(grp * CG + i, 1), :] for i in range(CG) for c in cs] + [pad], axis=0)
            lf_rows = jnp.concatenate(
                [_log_sigmoid(gt_scr[c, pl.ds(grp * CG + i + H_C, 1), :]) for i in range(CG) for c in cs]
                + [pad], axis=0)
            b_rows = _dot_hi(lf_rows, jnp.where(row <= col, 1.0, 0.0))
            tall = jnp.zeros((L - 8, L), F32)
            gates[k] = dict(ig_rows=ig_rows, b_rows=b_rows,
                            b_cols=jnp.concatenate([b_rows, tall], axis=0).T,
                            ig_cols=jnp.concatenate([ig_rows, tall], axis=0).T)

        def stage_a(c):
            gk = gates[c // PCH]
            v[c] = []
            nst[c + 1] = []
            mst[c + 1] = []
            for i in range(CG):
                r = i * PCH + c % PCH
                b_r = gk["b_rows"][r:r + 1, :]
                ig_r = gk["ig_rows"][r:r + 1, :]
                b_c = gk["b_cols"][:, r:r + 1]
                ig_c = gk["ig_cols"][:, r:r + 1]
                m_prev = mst[c][i]
                log_d = jnp.where(tri, b_c - b_r + ig_r, NEG_INF)
                log_inter = b_c + m_prev
                m_t = jnp.maximum(log_inter, jnp.max(log_d, axis=-1, keepdims=True))
                m_new = m_t[L - 1:L, :]
                b_last = b_c[L - 1:L, :]
                w_end = jnp.exp(b_last - b_c + ig_c - m_new)
                cd = jnp.exp(b_last + m_prev - m_new)
                q = pt[rows(c), head(qc, i, DQK_C)] * (DQK_C ** -0.5)
                k = pt[rows(c), head(kc, i, DQK_C)]
                vv = pt[rows(c), head(vc, i, DV_C)]
                nst[c + 1].append(cd * nst[c][i] + jnp.sum(w_end * k, axis=0, keepdims=True))
                mst[c + 1].append(m_new)
                v[c].append(dict(w=jnp.exp(log_d - m_t), w_inter=jnp.exp(log_inter - m_t),
                                 floor=jnp.exp(-m_t), cd=cd, qb=q.astype(BF16), kb=k.astype(BF16),
                                 vb=vv.astype(BF16), vw=(vv * w_end).astype(BF16),
                                 qn=jnp.sum(q * nst[c][i], axis=-1, keepdims=True)))

        def stage_b(c):
            for i, d in enumerate(v[c]):
                d["sc"] = _dot_nt(d["qb"], d["kb"])
                d["upd"] = _dot_tn(d["vw"], d["kb"])
            for i, d in enumerate(v[c]):
                d["inter"] = _dot_nt(d["qb"], cst[c][i].astype(BF16))

        def stage_c(c):
            cst[c + 1] = []
            for i, d in enumerate(v[c]):
                sc = d["sc"] * d["w"]
                d["den"] = jnp.sum(sc, axis=-1, keepdims=True) + d["w_inter"] * d["qn"]
                d["sc"] = sc.astype(BF16)
                cst[c + 1].append(d["cd"] * cst[c][i] + d["upd"])

        def stage_d(c):
            for d in v[c]:
                d["num"] = _dot(d["sc"], d["vb"])

        def stage_e(c):
            for i, d in enumerate(v.pop(c)):
                num = d["num"] + d["w_inter"] * d["inter"]
                hh = num / jnp.maximum(jnp.abs(d["den"]), d["floor"])
                z = pt[rows(c), head(zc, i, DV_C)]
                ys = slice(i * DV_C, (i + 1) * DV_C)
                yc_ref[rows(c), ys] = (_head_norm(hh, gm_ref[0:1, ys]) * _silu(z)).astype(BF16)

        _chunk_pipeline(n, piece, (stage_a, stage_b, stage_c, stage_d, stage_e), gate)
        for i in range(CG):
            c_scr[i] = cst[n][i]
            n_scr[i] = nst[n][i]
            m_scr[i] = jnp.broadcast_to(mst[n][i], (8, 128))
            c_ref[0, i] = cst[n][i]
            n_ref[0, i] = nst[n][i]
            m_ref[0, i] = jnp.broadcast_to(mst[n][i], (1, 128))


def _odd_heads(hp, hs, w_t, bg, gm):
    k = hp.shape[1]
    ms = hs.shape[0]
    tile = lambda s: jnp.maximum(s - 1, 0)
    qw, vw = CG * DQK_C, CG * DV_C
    koff = H_C * DQK_C // qw
    voff = 2 * H_C * DQK_C // vw
    zoff = (2 * H_C * DQK_C + W_C) // vw
    seq = lambda g, s: (tile(s) // CPB, g, 0, 0)
    once = pl.Buffered(1)
    return pl.pallas_call(
        _odd_heads_kernel,
        grid=(H_C // CG, CN + 1),
        in_specs=[pl.BlockSpec((CT, k), lambda g, s: (tile(s), 0)),
                  pl.BlockSpec((ms, k), lambda g, s: (0, 0), pipeline_mode=once),
                  pl.BlockSpec((qw, k), lambda g, s: (g, 0), pipeline_mode=once),
                  pl.BlockSpec((qw, k), lambda g, s: (koff + g, 0), pipeline_mode=once),
                  pl.BlockSpec((vw, k), lambda g, s: (voff + g, 0), pipeline_mode=once),
                  pl.BlockSpec((vw, k), lambda g, s: (zoff + g, 0), pipeline_mode=once),
                  pl.BlockSpec((128, k), lambda g, s: (O_GATE // 128, 0), pipeline_mode=once),
                  pl.BlockSpec((1, 128), lambda g, s: (0, 0)),
                  pl.BlockSpec((1, vw), lambda g, s: (0, g))],
        out_specs=(pl.BlockSpec((CT, vw), lambda g, s: (tile(s), g)),
                   pl.BlockSpec((1, CG, DV_C, DQK_C), seq),
                   pl.BlockSpec((1, CG, 1, DQK_C), seq),
                   pl.BlockSpec((1, CG, 1, 128), seq),
                   pl.BlockSpec((ms, qw), lambda g, s: (0, g)),
                   pl.BlockSpec((ms, qw), lambda g, s: (0, g)),
                   pl.BlockSpec((ms, vw), lambda g, s: (0, g)),
                   pl.BlockSpec((ms, vw), lambda g, s: (0, g))),
        out_shape=(jax.ShapeDtypeStruct((BATCH * SEQ, W_C), BF16),
                   jax.ShapeDtypeStruct((BATCH, H_C, DV_C, DQK_C), F32),
                   jax.ShapeDtypeStruct((BATCH, H_C, 1, DQK_C), F32),
                   jax.ShapeDtypeStruct((BATCH, H_C, 1, 128), F32),
                   jax.ShapeDtypeStruct((ms, H_C * DQK_C), F32),
                   jax.ShapeDtypeStruct((ms, H_C * DQK_C), F32),
                   jax.ShapeDtypeStruct((ms, W_C), F32),
                   jax.ShapeDtypeStruct((ms, W_C), F32)),
        scratch_shapes=[pltpu.VMEM((CW, k), BF16),
                        pltpu.VMEM((CT, CW), F32),
                        pltpu.VMEM((CG, DV_C, DQK_C), F32),
                        pltpu.VMEM((CG, 1, DQK_C), F32),
                        pltpu.VMEM((CG, 8, 128), F32),
                        pltpu.VMEM((CT // CHUNK, CHUNK, CHUNK), F32)],
        compiler_params=_params(("arbitrary", "arbitrary")),
        name="odd_heads",
    )(hp, hs, w_t, w_t, w_t, w_t, w_t, bg, gm)


DT = 512


def _odd_mlp_fused_kernel(hp_ref, w0_ref, w1_ref, w2_ref, wt_ref, lng_ref, lnb_ref, ws_ref, bst_ref,
                          yd_ref, wb, pt, wsb):
    s = pl.program_id(0)
    L = CHUNK
    uc = slice(0, W_D)
    vc = slice(W_D, 2 * W_D)
    zc = slice(2 * W_D, 3 * W_D)

    @pl.when(s == 0)
    def _():
        sh = N_GATE
        wb[uc, :] = jnp.concatenate([w0_ref[sh:, :], w1_ref[0:sh, :]], axis=0).astype(BF16)
        wb[vc, :] = jnp.concatenate([w1_ref[sh:, :], w2_ref[0:sh, :]], axis=0).astype(BF16)
        wb[zc, :] = jnp.concatenate([w2_ref[sh:, :], wt_ref[...]], axis=0).astype(BF16)
        keep = (lax.broadcasted_iota(jnp.int32, (L, L), 0) >= lax.broadcasted_iota(jnp.int32, (L, L), 1))
        for g in range(G_D):
            wsb[g] = jnp.where(keep, ws_ref[g], 0.0).astype(BF16)

    @pl.when(s > 0)
    def _():
        n = DT // L
        rows = lambda c: slice(c * L, (c + 1) * L)
        grp = lambda sl, g: slice(sl.start + g * 128, sl.start + (g + 1) * 128)
        vn = {}
        mix = {}

        def project(pc):
            pt[:, pc] = _dot_nt(hp_ref[...], wb[pc, :])

        def stage_a(c):
            dv = lambda g: pt[rows(c), grp(vc, g)]
            tot = dv(0)
            for g in range(1, G_D):
                tot = tot + dv(g)
            mu = jnp.sum(tot, axis=-1, keepdims=True) * (1.0 / W_D)
            sq = (dv(0) - mu) * (dv(0) - mu)
            for g in range(1, G_D):
                sq = sq + (dv(g) - mu) * (dv(g) - mu)
            rstd = lax.rsqrt(jnp.sum(sq, axis=-1, keepdims=True) * (1.0 / W_D) + EPS)
            vn[c] = [((dv(g) - mu) * rstd * lng_ref[0:1, g * 128:(g + 1) * 128]
                      + lnb_ref[0:1, g * 128:(g + 1) * 128]).astype(BF16) for g in range(G_D)]

        def stage_b(c):
            mix[c] = [_dot(wsb[g], vn[c][g]) for g in range(G_D)]

        def stage_e(c):
            for g in range(G_D):
                sg = mix[c][g] + bst_ref[:, g:g + 1]
                d_u = pt[rows(c), grp(uc, g)]
                d_z = pt[rows(c), grp(zc, g)]
                yd_ref[rows(c), g * 128:(g + 1) * 128] = (d_u * sg * _silu(d_z)).astype(BF16)

        project(vc)
        for c in range(n):
            stage_a(c)
        project(uc)
        for c in range(n):
            stage_b(c)
        project(zc)
        for c in range(n):
            stage_e(c)


def _odd_mlp_fused(hp, w_t, lng, lnb, ws, bst):
    k = hp.shape[1]
    m = hp.shape[0]
    tile = lambda s: (jnp.maximum(s - 1, 0), 0)
    const2 = lambda s: (0, 0)
    t0 = O_GATE // IN_TN
    once = pl.Buffered(1)
    wspec = lambda j: pl.BlockSpec((IN_TN, k), lambda s: (t0 + j, 0), pipeline_mode=once)
    return pl.pallas_call(
        _odd_mlp_fused_kernel,
        grid=(m // DT + 1,),
        in_specs=[pl.BlockSpec((DT, k), tile),
                  wspec(0), wspec(1), wspec(2),
                  pl.BlockSpec((N_GATE, k), lambda s: ((t0 + 3) * (IN_TN // N_GATE), 0), pipeline_mode=once),
                  pl.BlockSpec((1, W_D), const2),
                  pl.BlockSpec((1, W_D), const2),
                  pl.BlockSpec((G_D, CHUNK, CHUNK), lambda s: (0, 0, 0)),
                  pl.BlockSpec((CHUNK, G_D), const2)],
        out_specs=pl.BlockSpec((DT, W_D), tile),
        out_shape=jax.ShapeDtypeStruct((m, W_D), BF16),
        scratch_shapes=[pltpu.VMEM((3 * W_D, k), BF16),
                        pltpu.VMEM((DT, 3 * W_D), F32),
                        pltpu.VMEM((G_D, CHUNK, CHUNK), BF16)],
        compiler_params=_params(("arbitrary",)),
        name="odd_mlp_fused",
    )(hp, w_t, w_t, w_t, w_t, lng, lnb, ws, bst)


def _in_proj_rows_kernel(h_ref, w_ref, wn_ref, o_ref, *, shift):
    wsh = jnp.concatenate([w_ref[shift:, :], wn_ref[...]], axis=0)
    o_ref[...] = _dot_nt(h_ref[...], wsh.astype(BF16))


def _in_proj_rows(h, w_t, n_out, shift, tile0):
    ms, k = h.shape
    return pl.pallas_call(
        functools.partial(_in_proj_rows_kernel, shift=shift),
        grid=(n_out // IN_TN,),
        in_specs=[pl.BlockSpec((ms, k), lambda j: (0, 0)),
                  pl.BlockSpec((IN_TN, k), lambda j: (j + tile0, 0)),
                  pl.BlockSpec((shift, k), lambda j: ((j + tile0 + 1) * (IN_TN // shift), 0))],
        out_specs=pl.BlockSpec((ms, IN_TN), lambda j: (0, j)),
        out_shape=jax.ShapeDtypeStruct((ms, n_out), F32),
        compiler_params=_params(("arbitrary",)),
        name="in_proj_rows",
    )(h, w_t, w_t)


def _odd_sample_kernel(pq_ref, pk_ref, pv_ref, pz_ref, h_ref, wg_ref, pd_ref,
                       c_ref, nrow_ref, mrow_ref, bg_ref, gm_ref, lng_ref, lnb_ref,
                       wt_ref, bt_ref,
                       yc_ref, yd_ref, vn_ref, co_ref, no_ref, mo_ref,
                       inter_scr, wmix_scr):
    h = pl.program_id(1)
    row = lax.broadcasted_iota(jnp.int32, (SR, SR), 0)
    col = lax.broadcasted_iota(jnp.int32, (SR, SR), 1)
    trow = row & 3

    @pl.when(jnp.logical_and(pl.program_id(0) == 0, h == 0))
    def _():
        mask_d = jnp.where((row >> 2) == (col >> 2), trow - (col & 3), -1) >= 0
        rep = jnp.where(trow == col, 1.0, 0.0).astype(BF16)
        for g in range(G_D):
            wtile = _dot_nt(_dot(rep, wt_ref[g].astype(BF16)).astype(BF16), rep)
            wmix_scr[g] = jnp.where(mask_d, wtile, 0.0).astype(BF16)

    @pl.when(h == 0)
    def _():
        dv = pd_ref[:, W_D:2 * W_D]
        mu = jnp.mean(dv, axis=-1, keepdims=True)
        xc = dv - mu
        var = jnp.mean(xc * xc, axis=-1, keepdims=True)
        rstd = lax.rsqrt(var + EPS)
        for g in range(G_D):
            sl = slice(g * 128, (g + 1) * 128)
            vn = xc[:, sl] * rstd * lng_ref[0:1, sl] + lnb_ref[0:1, sl]
            vn_ref[:, sl] = vn
            s = _dot(wmix_scr[g], vn.astype(BF16)) + bt_ref[:, g:g + 1]
            d_u = pd_ref[:, g * 128:(g + 1) * 128]
            d_z = pd_ref[:, 2 * W_D + g * 128:2 * W_D + (g + 1) * 128]
            yd_ref[:, sl] = (d_u * s * _silu(d_z)).astype(BF16)

    same = (row >> 2) == (col >> 2)
    mask = jnp.where(same, trow - (col & 3), -1) >= 0
    pre = _dot_nt(h_ref[...], wg_ref[...].astype(BF16)) + bg_ref[...]
    lf = _log_sigmoid(pre)
    b_full = _dot_hi(jnp.where(mask, 1.0, 0.0), lf)
    sel_i = col == h
    sel_f = col == h + H_C
    ig_c = jnp.sum(jnp.where(sel_i, pre, 0.0), axis=-1, keepdims=True)
    b_c = jnp.sum(jnp.where(sel_f, b_full, 0.0), axis=-1, keepdims=True)
    sel_ir = row == h
    sel_fr = row == h + H_C
    ig_r = jnp.sum(jnp.where(sel_ir, pre.T, 0.0), axis=0, keepdims=True)
    b_r = jnp.sum(jnp.where(sel_fr, b_full.T, 0.0), axis=0, keepdims=True)
    m_prev = mrow_ref[0]
    log_d = jnp.where(mask, b_c - b_r + ig_r, NEG_INF)
    log_inter = b_c + m_prev
    m_t = jnp.maximum(log_inter, jnp.max(log_d, axis=-1, keepdims=True))
    w = jnp.exp(log_d - m_t)
    w_inter = jnp.exp(log_inter - m_t)
    q = pq_ref[...] * (DQK_C ** -0.5)
    k = pk_ref[...]
    v = pv_ref[...]
    qb = q.astype(BF16)
    kb = k.astype(BF16)
    vb = v.astype(BF16)
    sc = _dot_nt(qb, kb) * w
    sub8 = lax.broadcasted_iota(jnp.int32, (8, DQK_C), 0)
    for g in range(SR // 8):
        q8 = q[8 * g:8 * g + 8, :]
        q2 = jnp.concatenate([jnp.where(sub8 < DEC_SEQ, q8, 0.0), jnp.where(sub8 < DEC_SEQ, 0.0, q8)], axis=1)
        c_pair = jnp.concatenate([c_ref[2 * g + beta, 0].astype(BF16) for beta in range(2)], axis=1)
        inter_scr[8 * g:8 * g + 8, :] = _dot_nt(q2.astype(BF16), c_pair)
    n_rows = nrow_ref[0]
    num = _dot(sc.astype(BF16), vb) + w_inter * inter_scr[...]
    den = jnp.sum(sc, axis=-1, keepdims=True) + w_inter * jnp.sum(q * n_rows, axis=-1, keepdims=True)
    hh = num / jnp.maximum(jnp.abs(den), jnp.exp(-m_t))
    yc_ref[...] = (_head_norm(hh, gm_ref[...]) * _silu(pz_ref[...])).astype(BF16)

    stats = jnp.where(col == 0, m_t, jnp.where(col == 1, b_c, 0.0))
    last = _dot_hi(jnp.where(col == (row | 3), 1.0, 0.0), stats)
    m_new = last[:, 0:1]
    b_last = last[:, 1:2]
    w_end = jnp.exp(b_last - b_c + ig_c - m_new)
    cd = jnp.exp(b_last + m_prev - m_new)
    mo_ref[0] = m_new
    no_ref[0] = cd * n_rows + _dot_hi(jnp.where(same, 1.0, 0.0), w_end * k)
    vwt = (v * w_end).T
    lane_b = lax.broadcasted_iota(jnp.int32, (DV_C, SR), 1) >> 2
    for b in range(SB):
        lhs = jnp.where(lane_b == b, vwt, 0.0).astype(BF16)
        cd_b = cd[4 * b + 3:4 * b + 4, :]
        co_ref[b, 0] = cd_b * c_ref[b, 0] + _dot(lhs, kb)


def _odd_sample(pc, pd, h, w_o, c_state, n_rows, m_rows, bg, gm, lng, lnb, wt, bt):
    nb = DEC_BATCH // SB
    const2 = lambda i, h: (0, 0)
    return pl.pallas_call(
        _odd_sample_kernel,
        grid=(nb, H_C),
        in_specs=[pl.BlockSpec((SR, DQK_C), lambda i, h: (i, h)),
                  pl.BlockSpec((SR, DQK_C), lambda i, h: (i, h)),
                  pl.BlockSpec((SR, DV_C), lambda i, h: (i, h)),
                  pl.BlockSpec((SR, DV_C), lambda i, h: (i, h)),
                  pl.BlockSpec((SR, D_MODEL), lambda i, h: (i, 0)),
                  pl.BlockSpec((128, D_MODEL), lambda i, h: (O_GATE // 128, 0)),
                  pl.BlockSpec((SR, 3 * W_D), lambda i, h: (i, 0)),
                  pl.BlockSpec((SB, 1, DV_C, DQK_C), lambda i, h: (i, h, 0, 0)),
                  pl.BlockSpec((1, SR, DQK_C), lambda i, h: (h, i, 0)),
                  pl.BlockSpec((1, SR, 1), lambda i, h: (h, i, 0)),
                  pl.BlockSpec((1, 128), const2),
                  pl.BlockSpec((1, DV_C), lambda i, h: (0, h)),
                  pl.BlockSpec((1, W_D), const2),
                  pl.BlockSpec((1, W_D), const2),
                  pl.BlockSpec((G_D, SR, SR), lambda i, h: (0, 0, 0)),
                  pl.BlockSpec((SR, G_D), const2)],
        out_specs=(pl.BlockSpec((SR, DV_C), lambda i, h: (i, h)),
                   pl.BlockSpec((SR, W_D), lambda i, h: (i, 0)),
                   pl.BlockSpec((SR, W_D), lambda i, h: (i, 0)),
                   pl.BlockSpec((SB, 1, DV_C, DQK_C), lambda i, h: (i, h, 0, 0)),
                   pl.BlockSpec((1, SR, DQK_C), lambda i, h: (h, i, 0)),
                   pl.BlockSpec((1, SR, 1), lambda i, h: (h, i, 0))),
        out_shape=(jax.ShapeDtypeStruct((DEC_BATCH * DEC_SEQ, W_C), BF16),
                   jax.ShapeDtypeStruct((DEC_BATCH * DEC_SEQ, W_D), BF16),
                   jax.ShapeDtypeStruct((DEC_BATCH * DEC_SEQ, W_D), F32),
                   jax.ShapeDtypeStruct((DEC_BATCH, H_C, DV_C, DQK_C), F32),
                   jax.ShapeDtypeStruct((H_C, DEC_BATCH * DEC_SEQ, DQK_C), F32),
                   jax.ShapeDtypeStruct((H_C, DEC_BATCH * DEC_SEQ, 1), F32)),
        scratch_shapes=[pltpu.VMEM((SR, DV_C), F32),
                        pltpu.VMEM((G_D, SR, SR), BF16)],
        compiler_params=_params(("arbitrary", "arbitrary")),
        name="odd_sample",
    )(*pc, h, w_o, pd, c_state, n_rows, m_rows, bg, gm, lng, lnb, wt, bt)


def _rope_tables(pos):
    inv = ROPE_BASE ** (-jnp.arange(0, DH_B, 2, dtype=F32) / DH_B)
    ang = pos.astype(F32)[:, None] * inv[None, :]
    cos = jnp.cos(ang)
    sin = jnp.sin(ang)
    return jnp.concatenate([cos, cos], axis=-1), jnp.concatenate([-sin, sin], axis=-1)


def kernel(x_prompt, x_sample, state_conv, state_ret, state_mlstm_C, state_mlstm_n, state_mlstm_m,
           norm_even, w_in_even, conv_w, ret_norm, w_out_even,
           norm_odd, w_in_odd, b_gate_odd, mlstm_norm, ln_v_g, ln_v_b,
           w_spatial, b_spatial, w_out_odd, norm_final):
    w_in_e = w_in_even[0]
    w_out_e = w_out_even[0].astype(BF16)
    w_o = w_in_odd[0].T
    w_out_o = w_out_odd[0].astype(BF16)
    g_even = norm_even[0][None, :]
    g_odd = norm_odd[0][None, :]
    g_fin = norm_final[None, :]
    cw = conv_w[0]
    g_ret = ret_norm[0][None, :]
    bg = jnp.concatenate([b_gate_odd[0], jnp.zeros((128 - 2 * H_C,), F32)])[None, :]
    gm = mlstm_norm[0][None, :]
    lng = ln_v_g[0][None, :]
    lnb = ln_v_b[0][None, :]
    ws = w_spatial[0]
    bst = b_spatial[0].T

    cos_p, sin_p = _rope_tables(jnp.arange(SEQ, dtype=jnp.int32))
    cos_s, sin_s = _rope_tables(PAST_LEN + jnp.arange(DEC_SEQ, dtype=jnp.int32))
    cos_s = jnp.tile(cos_s, (SB, 1))
    sin_s = jnp.tile(sin_s, (SB, 1))
    lg_tab = jnp.broadcast_to(jnp.asarray(LOG_GAMMA, F32)[:, None, None], (H_B, 1, 128))

    bt_s = jnp.tile(b_spatial[0][:, :DEC_SEQ].T, (SB, 1))

    rs = DEC_BATCH * DEC_SEQ
    xp = x_prompt.reshape(BATCH * SEQ, D_MODEL)
    xs = x_sample.reshape(rs, D_MODEL)
    hp, hs = _norm_cast(xp, xs, g_even, 512)
    ya, conv_p, *ps_a = _even_conv(hp, hs, w_in_e, cw)
    yb, ret_p, *ps_b = _even_heads(hp, hs, w_in_e, g_ret, cos_p, sin_p, lg_tab)
    st_exp = jnp.pad(state_conv[0], ((0, 0), (0, DEC_SEQ - (CONV_W - 1)), (0, 0))).reshape(rs, W_A)
    ya_s, u_s, yb_s, ret_s = _even_sample(ps_a, ps_b, st_exp, state_ret[0], cw, g_ret, cos_s, sin_s, lg_tab)
    x1, h1, x1s, h1s = _outproj(ya, yb, xp, ya_s, yb_s, xs, w_out_e, g_odd, final=False)

    yc, c_p, n_p, m_p, *ps_c = _odd_heads(h1, h1s, w_o, bg, gm)
    yd = _odd_mlp_fused(h1, w_o, lng, lnb, ws, bst)
    pd_s = _in_proj_rows(h1s, w_o, 3 * W_D, N_GATE, O_GATE // IN_TN)
    n_rows = jnp.repeat(jnp.transpose(state_mlstm_n[0], (1, 0, 2)), DEC_SEQ, axis=1)
    m_rows = jnp.repeat(state_mlstm_m[0].T, DEC_SEQ, axis=1)[:, :, None]
    yc_s, yd_s, vn_s, c_s, no_s, mo_s = _odd_sample(
        ps_c, pd_s, h1s, w_o, state_mlstm_C[0], n_rows, m_rows, bg, gm, lng, lnb, ws, bt_s)
    y_prompt, y_sample = _outproj(yc, yd, x1, yc_s, yd_s, x1s, w_out_o, g_fin, final=True)

    conv_s = u_s.reshape(DEC_BATCH, DEC_SEQ, W_A)[:, DEC_SEQ - (CONV_W - 1):, :]
    n_s = jnp.transpose(no_s[:, DEC_SEQ - 1::DEC_SEQ, :], (1, 0, 2))
    m_s = mo_s[:, DEC_SEQ - 1::DEC_SEQ, 0].T
    return (y_prompt.reshape(BATCH, SEQ, D_MODEL),
            y_sample.reshape(DEC_BATCH, DEC_SEQ, D_MODEL),
            conv_p[None], conv_s[None],
            ret_p[None], ret_s[None],
            c_p[None], c_s[None],
            n_p[:, :, 0, :][None], n_s[None],
            m_p[:, :, 0, 0][None], m_s[None],
            vn_s.reshape(DEC_BATCH, DEC_SEQ, W_D)[None])
```

```python
import functools
import math

import jax
import jax.numpy as jnp
from jax import lax
from jax.experimental import pallas as pl
from jax.experimental.pallas import tpu as pltpu

F32 = jnp.float32
BF16 = jnp.bfloat16

D_MODEL = 2048
BATCH = 4
SEQ = 2048
DEC_BATCH = 128
DEC_SEQ = 4
PAST_LEN = 16384
W_A = 1024
CONV_W = 3
W_B = 1024
H_B = 8
DH_B = 128
W_C = 1024
H_C = 4
DV_C = 256
DQK_C = 128
W_D = 1024
G_D = 8
CHUNK = 128
O_GATE = 2 * H_C * DQK_C + 2 * W_C
N_GATE = 2 * H_C
ROPE_BASE = 10000.0
EPS = 1e-6
LOG_GAMMA = tuple(math.log(1.0 - 2.0 ** (-5.0 - h)) for h in range(H_B))
NEG_INF = float("-inf")
VMEM_LIMIT = 56 * 1024 * 1024

NT_DIMS = (((1,), (1,)), ((), ()))
TN_DIMS = (((0,), (0,)), ((), ()))


def _silu(z):
    return z * (1.0 / (1.0 + jnp.exp(-z)))


def _log_sigmoid(x):
    return jnp.minimum(x, 0.0) - jnp.log1p(jnp.exp(-jnp.abs(x)))


def _dot(a, b):
    return jnp.dot(a, b, preferred_element_type=F32)


def _dot_nt(a, b):
    return lax.dot_general(a, b, NT_DIMS, preferred_element_type=F32)


def _dot_tn(a, b):
    return lax.dot_general(a, b, TN_DIMS, preferred_element_type=F32)


def _dot_hi(a, b):
    return jnp.dot(a, b, preferred_element_type=F32, precision=lax.Precision.HIGHEST)


def _head_norm(o, g):
    mu = jnp.mean(o, axis=-1, keepdims=True)
    oc = o - mu
    var = jnp.mean(oc * oc, axis=-1, keepdims=True)
    return oc * lax.rsqrt(var + EPS) * g


def _block_diag(blocks):
    n = len(blocks)
    zero = jnp.zeros_like(blocks[0])
    return jnp.concatenate(
        [jnp.concatenate([blk if j == i else zero for j in range(n)], axis=1) for i, blk in enumerate(blocks)],
        axis=0)


def _params(sem):
    return pltpu.CompilerParams(dimension_semantics=sem, vmem_limit_bytes=VMEM_LIMIT)


def _norm_cast_kernel(x_ref, xs_ref, g_ref, h_ref, hs_ref, *, n_prompt):
    i = pl.program_id(0)

    def tile(x_in, h_out):
        x = x_in[...]
        ms = jnp.mean(x * x, axis=-1, keepdims=True)
        h_out[...] = (x * lax.rsqrt(ms + EPS) * g_ref[...]).astype(BF16)

    @pl.when(i < n_prompt)
    def _():
        tile(x_ref, h_ref)

    @pl.when(i == n_prompt)
    def _():
        tile(xs_ref, hs_ref)


def _norm_cast(x, xs, g, tm):
    m, d = x.shape
    ms = xs.shape[0]
    n_prompt = m // tm
    row = lambda i: (jnp.minimum(i, n_prompt - 1), 0)
    const = lambda i: (0, 0)
    return pl.pallas_call(
        functools.partial(_norm_cast_kernel, n_prompt=n_prompt),
        grid=(n_prompt + 1,),
        in_specs=[pl.BlockSpec((tm, d), row),
                  pl.BlockSpec((ms, d), const),
                  pl.BlockSpec((1, d), const)],
        out_specs=(pl.BlockSpec((tm, d), row), pl.BlockSpec((ms, d), const)),
        out_shape=(jax.ShapeDtypeStruct((m, d), BF16), jax.ShapeDtypeStruct((ms, d), BF16)),
        compiler_params=_params(("arbitrary",)),
        name="norm_cast",
    )(x, xs, g)


IN_TN = 1024


OUT_TM = 512


def _outproj_kernel(ya_ref, yb_ref, x_ref, yas_ref, ybs_ref, xs_ref, w_ref, g_ref, *out_refs,
                    final, n_prompt):
    i = pl.program_id(0)
    half = ya_ref.shape[1]
    n_out = 1 if final else 2

    def tile(ya, yb, x, outs):
        acc = _dot(ya[...], w_ref[0:half, :]) + _dot(yb[...], w_ref[half:2 * half, :])
        x1 = x[...] + acc
        ms = jnp.mean(x1 * x1, axis=-1, keepdims=True)
        hn = x1 * lax.rsqrt(ms + EPS) * g_ref[...]
        if final:
            outs[0][...] = hn
        else:
            outs[0][...] = x1
            outs[1][...] = hn.astype(BF16)

    @pl.when(i < n_prompt)
    def _():
        tile(ya_ref, yb_ref, x_ref, out_refs[:n_out])

    @pl.when(i == n_prompt)
    def _():
        tile(yas_ref, ybs_ref, xs_ref, out_refs[n_out:])


def _outproj(ya, yb, x, ya_s, yb_s, x_s, w, g, final):
    m, half = ya.shape
    ms = ya_s.shape[0]
    d = w.shape[1]
    n_prompt = m // OUT_TM
    row = lambda i: (jnp.minimum(i, n_prompt - 1), 0)
    const = lambda i: (0, 0)
    once = pl.Buffered(1)
    shapes = [jax.ShapeDtypeStruct((m, d), F32), jax.ShapeDtypeStruct((ms, d), F32)]
    specs = [pl.BlockSpec((OUT_TM, d), row), pl.BlockSpec((ms, d), const)]
    if not final:
        shapes = [shapes[0], jax.ShapeDtypeStruct((m, d), BF16), shapes[1], jax.ShapeDtypeStruct((ms, d), BF16)]
        specs = [specs[0], pl.BlockSpec((OUT_TM, d), row), specs[1], pl.BlockSpec((ms, d), const)]
    return pl.pallas_call(
        functools.partial(_outproj_kernel, final=final, n_prompt=n_prompt),
        grid=(n_prompt + 1,),
        in_specs=[pl.BlockSpec((OUT_TM, half), row),
                  pl.BlockSpec((OUT_TM, half), row),
                  pl.BlockSpec((OUT_TM, d), row),
                  pl.BlockSpec((ms, half), const, pipeline_mode=once),
                  pl.BlockSpec((ms, half), const, pipeline_mode=once),
                  pl.BlockSpec((ms, d), const, pipeline_mode=once),
                  pl.BlockSpec((2 * half, d), const, pipeline_mode=once),
                  pl.BlockSpec((1, d), const)],
        out_specs=tuple(specs),
        out_shape=tuple(shapes),
        compiler_params=_params(("arbitrary",)),
        name="out_proj_final" if final else "out_proj",
    )(ya, yb, x, ya_s, yb_s, x_s, w, g)


def _rope(x, cosf, sins):
    return x * cosf + pltpu.roll(x, DH_B // 2, 1) * sins


FT = 1024
FN = BATCH * SEQ // FT
FPB = SEQ // FT
HG = 2
GW = HG * 128
PCH = 2


def _chunk_pipeline(n, piece, stages, gate=None):
    sa, sb, sc, sd, se = stages
    npieces = n // PCH
    for j in range(PCH):
        piece(0, j)
    if gate is not None:
        gate(0)
    for c in range(n + 2):
        k, j = c // PCH + 1, c % PCH
        if k < npieces:
            piece(k, j)
        if c < n:
            sa(c)
        if 1 <= c <= n:
            sc(c - 1)
        if c < n:
            sb(c)
        if 1 <= c <= n:
            sd(c - 1)
        if c >= 2:
            se(c - 2)
        if gate is not None and j == PCH - 1 and k < npieces:
            gate(k)


def _even_heads_kernel(hp_ref, hs_ref, wq_ref, wk_ref, wv_ref, wz_ref, gret_ref, cos_ref, sin_ref, lg_ref,
                       yb_ref, s_ref, sq_ref, sk_ref, sv_ref, sz_ref, wb, pt, s_scr):
    s = pl.program_id(1)
    L = CHUNK

    @pl.when(s == 0)
    def _():
        for part, w_ref in enumerate((wq_ref, wk_ref, wv_ref, wz_ref)):
            wb[:, part * GW:(part + 1) * GW] = w_ref[...].astype(BF16)
        ps = _dot(hs_ref[...], wb[...])
        for part, o_ref in enumerate((sq_ref, sk_ref, sv_ref, sz_ref)):
            o_ref[...] = ps[:, part * GW:(part + 1) * GW]

    @pl.when(s > 0)
    def _():
        t = s - 1
        n = FT // L
        rows = lambda c: slice(c * L, (c + 1) * L)
        cols = lambda part, i: slice(part * GW + i * DH_B, part * GW + (i + 1) * DH_B)

        def piece(k, j):
            pr = slice(k * PCH * L, (k + 1) * PCH * L)
            pc = slice(j * 2 * GW, (j + 1) * 2 * GW)
            pt[pr, pc] = _dot(hp_ref[pr, :], wb[:, pc])

        row = lax.broadcasted_iota(jnp.int32, (L, L), 0)
        col = lax.broadcasted_iota(jnp.int32, (L, L), 1)
        causal = row >= col
        diff = jnp.maximum(row - col, 0).astype(F32)
        ti = lax.broadcasted_iota(jnp.int32, (L, 1), 0).astype(F32)
        lgs = [lg_ref[i][:, 0:1] for i in range(HG)]
        decay = [jnp.where(causal, jnp.exp(lg * diff), 0.0) for lg in lgs]
        q_decay = [jnp.exp(lg * (ti + 1.0)) for lg in lgs]
        k_decay = [jnp.exp(lg * (L - 1.0 - ti)) for lg in lgs]
        gamma_l = [jnp.exp(lg * float(L)) for lg in lgs]
        state = {0: [jnp.where(t % FPB == 0, 0.0, s_scr[i]) for i in range(HG)]}
        v = {}

        def stage_a(c):
            cosf = cos_ref[rows(c), :]
            sins = sin_ref[rows(c), :]
            v[c] = []
            for i in range(HG):
                kr = _rope(pt[rows(c), cols(1, i)], cosf, sins) * (DH_B ** -0.5)
                v[c].append(dict(qb=_rope(pt[rows(c), cols(0, i)], cosf, sins).astype(BF16),
                                 kb=kr.astype(BF16),
                                 kd=(kr * k_decay[i]).astype(BF16),
                                 vb=pt[rows(c), cols(2, i)].astype(BF16)))

        def stage_b(c):
            for i, d in enumerate(v[c]):
                d["sc"] = _dot_nt(d["qb"], d["kb"])
                d["upd"] = _dot_tn(d["kd"], d["vb"])
            cross = _dot(jnp.concatenate([d["qb"] for d in v[c]], axis=1),
                         _block_diag([st.astype(BF16) for st in state[c]]))
            for i, d in enumerate(v[c]):
                d["cross"] = cross[:, i * DH_B:(i + 1) * DH_B]

        def stage_c(c):
            state[c + 1] = []
            for i, d in enumerate(v[c]):
                d["sc"] = (d["sc"] * decay[i]).astype(BF16)
                state[c + 1].append(gamma_l[i] * state[c][i] + d["upd"])

        def stage_d(c):
            inner = _dot(jnp.concatenate([d["sc"] for d in v[c]], axis=1),
                         _block_diag([d["vb"] for d in v[c]]))
            for i, d in enumerate(v[c]):
                d["inner"] = inner[:, i * DH_B:(i + 1) * DH_B]

        def stage_e(c):
            for i, d in enumerate(v.pop(c)):
                o = d["inner"] + d["cross"] * q_decay[i]
                g = gret_ref[0:1, i * DH_B:(i + 1) * DH_B]
                z = pt[rows(c), cols(3, i)]
                yb_ref[rows(c), i * DH_B:(i + 1) * DH_B] = (_head_norm(o, g) * _silu(z)).astype(BF16)

        _chunk_pipeline(n, piece, (stage_a, stage_b, stage_c, stage_d, stage_e))
        for i in range(HG):
            s_scr[i] = state[n][i]
            s_ref[0, i] = state[n][i]


def _even_heads(hp, hs, w, g_ret, cosf, sins, lg_tab):
    k = hp.shape[1]
    ms = hs.shape[0]
    ng = H_B // HG
    base = 4 * W_A // GW
    tile = lambda s: jnp.maximum(s - 1, 0)
    wspec = lambda part: pl.BlockSpec((k, GW), lambda g, s: (0, base + part * ng + g))
    sspec = pl.BlockSpec((ms, GW), lambda g, s: (0, g))
    sshape = jax.ShapeDtypeStruct((ms, W_B), F32)
    return pl.pallas_call(
        _even_heads_kernel,
        grid=(ng, FN + 1),
        in_specs=[pl.BlockSpec((FT, k), lambda g, s: (tile(s), 0)),
                  pl.BlockSpec((ms, k), lambda g, s: (0, 0)),
                  wspec(0), wspec(1), wspec(2), wspec(3),
                  pl.BlockSpec((1, GW), lambda g, s: (0, g)),
                  pl.BlockSpec((FT, DH_B), lambda g, s: (tile(s) % FPB, 0)),
                  pl.BlockSpec((FT, DH_B), lambda g, s: (tile(s) % FPB, 0)),
                  pl.BlockSpec((HG, 1, 128), lambda g, s: (g, 0, 0))],
        out_specs=(pl.BlockSpec((FT, GW), lambda g, s: (tile(s), g)),
                   pl.BlockSpec((1, HG, DH_B, DH_B), lambda g, s: (tile(s) // FPB, g, 0, 0)),
                   sspec, sspec, sspec, sspec),
        out_shape=(jax.ShapeDtypeStruct((BATCH * SEQ, W_B), BF16),
                   jax.ShapeDtypeStruct((BATCH, H_B, DH_B, DH_B), F32),
                   sshape, sshape, sshape, sshape),
        scratch_shapes=[pltpu.VMEM((k, 4 * GW), BF16),
                        pltpu.VMEM((FT, 4 * GW), F32),
                        pltpu.VMEM((HG, DH_B, DH_B), F32)],
        compiler_params=_params(("arbitrary", "arbitrary")),
        name="even_heads",
    )(hp, hs, w, w, w, w, g_ret, cosf, sins, lg_tab)


def _even_conv_kernel(hp_ref, hs_ref, wb_ref, wc_ref, wx_ref, wz_ref, cw_ref,
                      ya_ref, conv_ref, sb_ref, sc_ref, sx_ref, sz_ref, wb, pt, ubuf):
    s = pl.program_id(1)
    L = CHUNK
    sub = lambda j: slice(j * 4 * 128, (j + 1) * 4 * 128)
    cs = lambda j, p: slice(j * 4 * 128 + p * 128, j * 4 * 128 + (p + 1) * 128)

    @pl.when(s == 0)
    def _():
        for p, w_ref in enumerate((wb_ref, wc_ref, wx_ref, wz_ref)):
            for j in range(HG):
                wb[:, cs(j, p)] = w_ref[:, j * 128:(j + 1) * 128].astype(BF16)
        ps = _dot(hs_ref[...], wb[...])
        for p, o_ref in enumerate((sb_ref, sc_ref, sx_ref, sz_ref)):
            for j in range(HG):
                o_ref[:, j * 128:(j + 1) * 128] = ps[:, cs(j, p)]

    @pl.when(s > 0)
    def _():
        t = s - 1
        n = FT // L
        rows = lambda c: slice(c * L, (c + 1) * L)

        @pl.when(t % FPB == 0)
        def _():
            ubuf[0:8, :] = jnp.zeros((8, GW), F32)

        @pl.when(t % FPB != 0)
        def _():
            ubuf[0:8, :] = ubuf[FT:FT + 8, :]

        def project(j):
            pt[:, sub(j)] = _dot(hp_ref[...], wb[:, sub(j)])

        def mix(j):
            ch = slice(j * 128, (j + 1) * 128)
            for c in range(n):
                u = pt[rows(c), cs(j, 1)] * pt[rows(c), cs(j, 2)]
                ubuf[8 + c * L:8 + (c + 1) * L, ch] = u
                t0 = ubuf[6 + c * L:6 + (c + 1) * L, ch]
                t1 = ubuf[7 + c * L:7 + (c + 1) * L, ch]
                conv = cw_ref[0:1, ch] * t0 + cw_ref[1:2, ch] * t1 + cw_ref[2:3, ch] * u
                ya_ref[rows(c), ch] = (pt[rows(c), cs(j, 0)] * conv * _silu(pt[rows(c), cs(j, 3)])).astype(BF16)

        project(0)
        for j in range(HG):
            if j + 1 < HG:
                project(j + 1)
            mix(j)
        conv_ref[0] = ubuf[FT + 6:FT + 8, :]


def _even_conv(hp, hs, w, conv_w):
    k = hp.shape[1]
    ms = hs.shape[0]
    ng = W_A // GW
    tile = lambda s: jnp.maximum(s - 1, 0)
    wspec = lambda part: pl.BlockSpec((k, GW), lambda g, s: (0, part * ng + g))
    sspec = pl.BlockSpec((ms, GW), lambda g, s: (0, g))
    sshape = jax.ShapeDtypeStruct((ms, W_A), F32)
    return pl.pallas_call(
        _even_conv_kernel,
        grid=(ng, FN + 1),
        in_specs=[pl.BlockSpec((FT, k), lambda g, s: (tile(s), 0)),
                  pl.BlockSpec((ms, k), lambda g, s: (0, 0)),
                  wspec(0), wspec(1), wspec(2), wspec(3),
                  pl.BlockSpec((CONV_W, GW), lambda g, s: (0, g))],
        out_specs=(pl.BlockSpec((FT, GW), lambda g, s: (tile(s), g)),
                   pl.BlockSpec((1, CONV_W - 1, GW), lambda g, s: (tile(s) // FPB, 0, g)),
                   sspec, sspec, sspec, sspec),
        out_shape=(jax.ShapeDtypeStruct((BATCH * SEQ, W_A), BF16),
                   jax.ShapeDtypeStruct((BATCH, CONV_W - 1, W_A), F32),
                   sshape, sshape, sshape, sshape),
        scratch_shapes=[pltpu.VMEM((k, 4 * GW), BF16),
                        pltpu.VMEM((FT, 4 * GW), F32),
                        pltpu.VMEM((FT + 8, GW), F32)],
        compiler_params=_params(("arbitrary", "arbitrary")),
        name="even_conv",
    )(hp, hs, w, w, w, w, conv_w)


SB = 32
SR = SB * DEC_SEQ


def _even_sample_kernel(ab_ref, ac_ref, ax_ref, az_ref, pq_ref, pk_ref, pv_ref, pz_ref, st_ref, s_ref,
                        cw_ref, gret_ref, cos_ref, sin_ref, lg_ref,
                        ya_ref, u_ref, yb_ref, so_ref, cross_scr):
    h = pl.program_id(1)
    row = lax.broadcasted_iota(jnp.int32, (SR, SR), 0)
    col = lax.broadcasted_iota(jnp.int32, (SR, SR), 1)
    trow = row & 3

    @pl.when(h == 0)
    def _():
        for j in range(W_A // 128):
            sl = slice(j * 128, (j + 1) * 128)
            a_b = ab_ref[:, sl]
            a_c = ac_ref[:, sl]
            a_x = ax_ref[:, sl]
            a_z = az_ref[:, sl]
            u = a_c * a_x
            e = st_ref[:, sl]
            tap1 = jnp.where(trow >= 1, pltpu.roll(u, 1, 0), pltpu.roll(e, SR - 1, 0))
            tap0 = jnp.where(trow >= 2, pltpu.roll(u, 2, 0), e)
            conv = cw_ref[0:1, sl] * tap0 + cw_ref[1:2, sl] * tap1 + cw_ref[2:3, sl] * u
            ya_ref[:, sl] = (a_b * conv * _silu(a_z)).astype(BF16)
            u_ref[:, sl] = u

    lg = lg_ref[0][:, 0:1]
    same = (row >> 2) == (col >> 2)
    dd = trow - (col & 3)
    mask = jnp.where(same, dd, -1) >= 0
    decay = jnp.where(mask, jnp.exp(lg * jnp.maximum(dd, 0).astype(F32)), 0.0)
    tcol = (lax.broadcasted_iota(jnp.int32, (SR, 1), 0) & 3).astype(F32)
    cosf = cos_ref[...]
    sins = sin_ref[...]
    qr = _rope(pq_ref[...], cosf, sins)
    kr = _rope(pk_ref[...], cosf, sins) * (DH_B ** -0.5)
    qb = qr.astype(BF16)
    kb = kr.astype(BF16)
    vb = pv_ref[...].astype(BF16)
    sc = _dot_nt(qb, kb) * decay
    inner = _dot(sc.astype(BF16), vb)
    kdt = (kr * jnp.exp(lg * (DEC_SEQ - 1.0 - tcol))).T
    gamma_l = jnp.exp(lg * float(DEC_SEQ))
    lane_b = col >> 2
    sub = lax.broadcasted_iota(jnp.int32, (8, DH_B), 0)
    for g in range(SR // 8):
        q8 = qr[8 * g:8 * g + 8, :]
        q2 = jnp.concatenate([jnp.where(sub < DEC_SEQ, q8, 0.0), jnp.where(sub < DEC_SEQ, 0.0, q8)], axis=1)
        s_pair = [s_ref[2 * g + beta, 0] for beta in range(2)]
        cross_scr[8 * g:8 * g + 8, :] = _dot(
            q2.astype(BF16), jnp.concatenate([sp.astype(BF16) for sp in s_pair], axis=0))
        for beta in range(2):
            b = 2 * g + beta
            lhs = jnp.where(lane_b == b, kdt, 0.0).astype(BF16)
            so_ref[b, 0] = gamma_l * s_pair[beta] + _dot(lhs, vb)
    o = inner + cross_scr[...] * jnp.exp(lg * (tcol + 1.0))
    yb_ref[...] = (_head_norm(o, gret_ref[...]) * _silu(pz_ref[...])).astype(BF16)


def _even_sample(pa, pb, st_exp, s_state, conv_w, g_ret, cosf, sins, lg_tab):
    nb = DEC_BATCH // SB
    const2 = lambda i, h: (0, 0)
    aspec = pl.BlockSpec((SR, W_A), lambda i, h: (i, 0))
    hspec = pl.BlockSpec((SR, DH_B), lambda i, h: (i, h))
    return pl.pallas_call(
        _even_sample_kernel,
        grid=(nb, H_B),
        in_specs=[aspec, aspec, aspec, aspec,
                  hspec, hspec, hspec, hspec,
                  pl.BlockSpec((SR, W_A), lambda i, h: (i, 0)),
                  pl.BlockSpec((SB, 1, DH_B, DH_B), lambda i, h: (i, h, 0, 0)),
                  pl.BlockSpec((CONV_W, W_A), const2),
                  pl.BlockSpec((1, DH_B), lambda i, h: (0, h)),
                  pl.BlockSpec((SR, DH_B), const2),
                  pl.BlockSpec((SR, DH_B), const2),
                  pl.BlockSpec((1, 1, 128), lambda i, h: (h, 0, 0))],
        out_specs=(pl.BlockSpec((SR, W_A), lambda i, h: (i, 0)),
                   pl.BlockSpec((SR, W_A), lambda i, h: (i, 0)),
                   pl.BlockSpec((SR, DH_B), lambda i, h: (i, h)),
                   pl.BlockSpec((SB, 1, DH_B, DH_B), lambda i, h: (i, h, 0, 0))),
        out_shape=(jax.ShapeDtypeStruct((DEC_BATCH * DEC_SEQ, W_A), BF16),
                   jax.ShapeDtypeStruct((DEC_BATCH * DEC_SEQ, W_A), F32),
                   jax.ShapeDtypeStruct((DEC_BATCH * DEC_SEQ, W_B), BF16),
                   jax.ShapeDtypeStruct((DEC_BATCH, H_B, DH_B, DH_B), F32)),
        scratch_shapes=[pltpu.VMEM((SR, DH_B), F32)],
        compiler_params=_params(("arbitrary", "arbitrary")),
        name="even_sample",
    )(*pa, *pb, st_exp, s_state, conv_w, g_ret, cosf, sins, lg_tab)


CG = 2
CW = 128 + CG * (2 * DQK_C + 2 * DV_C)
CT = 1024
CN = BATCH * SEQ // CT
CPB = SEQ // CT


def _odd_heads_kernel(hp_ref, hs_ref, wq_ref, wk_ref, wv_ref, wz_ref, wg_ref, bg_ref, gm_ref,
                      yc_ref, c_ref, n_ref, m_ref, sq_ref, sk_ref, sv_ref, sz_ref,
                      wb, pt, c_scr, n_scr, m_scr, gt_scr):
    grp = pl.program_id(0)
    s = pl.program_id(1)
    L = CHUNK
    gc = slice(0, 128)
    qc = slice(128, 128 + CG * DQK_C)
    kc = slice(qc.stop, qc.stop + CG * DQK_C)
    vc = slice(kc.stop, kc.stop + CG * DV_C)
    zc = slice(vc.stop, vc.stop + CG * DV_C)
    head = lambda sl, i, w: slice(sl.start + i * w, sl.start + (i + 1) * w)

    @pl.when(s == 0)
    def _():
        wb[qc, :] = wq_ref[...].astype(BF16)
        wb[kc, :] = wk_ref[...].astype(BF16)
        wb[vc, :] = wv_ref[...].astype(BF16)
        wb[zc, :] = wz_ref[...].astype(BF16)
        wb[gc, :] = wg_ref[...].astype(BF16)
        ps = _dot_nt(hs_ref[...], wb[qc.start:CW, :])
        off = lambda sl: slice(sl.start - qc.start, sl.stop - qc.start)
        sq_ref[...] = ps[:, off(qc)]
        sk_ref[...] = ps[:, off(kc)]
        sv_ref[...] = ps[:, off(vc)]
        sz_ref[...] = ps[:, off(zc)]

    @pl.when(s > 0)
    def _():
        t = s - 1
        n = CT // L
        rows = lambda c: slice(c * L, (c + 1) * L)

        def piece(k, j):
            pr = slice(k * PCH * L, (k + 1) * PCH * L)
            pc = (slice(0, vc.start), slice(vc.start, CW))[j]
            pt[pr, pc] = _dot_nt(hp_ref[pr, :], wb[pc, :])

        row = lax.broadcasted_iota(jnp.int32, (L, L), 0)
        col = lax.broadcasted_iota(jnp.int32, (L, L), 1)
        tri = row >= col
        fresh = t % CPB == 0
        cst = {0: [jnp.where(fresh, 0.0, c_scr[i]) for i in range(CG)]}
        nst = {0: [jnp.where(fresh, 0.0, n_scr[i]) for i in range(CG)]}
        mst = {0: [jnp.where(fresh, 0.0, m_scr[i, 0:1, 0:1]) for i in range(CG)]}
        v = {}
        gates = {}

        def gate(k):
            cs = range(k * PCH, (k + 1) * PCH)
            for c in cs:
                gt_scr[c] = (pt[rows(c), gc] + bg_ref[...]).T
            pad = jnp.zeros((8 - CG * PCH, L), F32)
            ig_rows = jnp.concatenate(
                [gt_scr[c, pl.ds(grp * CG + i, 1), :] for i in range(CG) for c in cs] + [pad], axis=0)
            lf_rows = jnp.concatenate(
                [_log_sigmoid(gt_scr[c, pl.ds(grp * CG + i + H_C, 1), :]) for i in range(CG) for c in cs]
                + [pad], axis=0)
            b_rows = _dot_hi(lf_rows, jnp.where(row <= col, 1.0, 0.0))
            tall = jnp.zeros((L - 8, L), F32)
            gates[k] = dict(ig_rows=ig_rows, b_rows=b_rows,
                            b_cols=jnp.concatenate([b_rows, tall], axis=0).T,
                            ig_cols=jnp.concatenate([ig_rows, tall], axis=0).T)

        def stage_a(c):
            gk = gates[c // PCH]
            v[c] = []
            nst[c + 1] = []
            mst[c + 1] = []
            for i in range(CG):
                r = i * PCH + c % PCH
                b_r = gk["b_rows"][r:r + 1, :]
                ig_r = gk["ig_rows"][r:r + 1, :]
                b_c = gk["b_cols"][:, r:r + 1]
                ig_c = gk["ig_cols"][:, r:r + 1]
                m_prev = mst[c][i]
                log_d = jnp.where(tri, b_c - b_r + ig_r, NEG_INF)
                log_inter = b_c + m_prev
                m_t = jnp.maximum(log_inter, jnp.max(log_d, axis=-1, keepdims=True))
                m_new = m_t[L - 1:L, :]
                b_last = b_c[L - 1:L, :]
                w_end = jnp.exp(b_last - b_c + ig_c - m_new)
                cd = jnp.exp(b_last + m_prev - m_new)
                q = pt[rows(c), head(qc, i, DQK_C)] * (DQK_C ** -0.5)
                k = pt[rows(c), head(kc, i, DQK_C)]
                vv = pt[rows(c), head(vc, i, DV_C)]
                nst[c + 1].append(cd * nst[c][i] + jnp.sum(w_end * k, axis=0, keepdims=True))
                mst[c + 1].append(m_new)
                v[c].append(dict(w=jnp.exp(log_d - m_t), w_inter=jnp.exp(log_inter - m_t),
                                 floor=jnp.exp(-m_t), cd=cd, qb=q.astype(BF16), kb=k.astype(BF16),
                                 vb=vv.astype(BF16), vw=(vv * w_end).astype(BF16),
                                 qn=jnp.sum(q * nst[c][i], axis=-1, keepdims=True)))

        def stage_b(c):
            for i, d in enumerate(v[c]):
                d["sc"] = _dot_nt(d["qb"], d["kb"])
                d["upd"] = _dot_tn(d["vw"], d["kb"])
            for i, d in enumerate(v[c]):
                d["inter"] = _dot_nt(d["qb"], cst[c][i].astype(BF16))

        def stage_c(c):
            cst[c + 1] = []
            for i, d in enumerate(v[c]):
                sc = d["sc"] * d["w"]
                d["den"] = jnp.sum(sc, axis=-1, keepdims=True) + d["w_inter"] * d["qn"]
                d["sc"] = sc.astype(BF16)
                cst[c + 1].append(d["cd"] * cst[c][i] + d["upd"])

        def stage_d(c):
            for d in v[c]:
                d["num"] = _dot(d["sc"], d["vb"])

        def stage_e(c):
            for i, d in enumerate(v.pop(c)):
                num = d["num"] + d["w_inter"] * d["inter"]
                hh = num / jnp.maximum(jnp.abs(d["den"]), d["floor"])
                z = pt[rows(c), head(zc, i, DV_C)]
                ys = slice(i * DV_C, (i + 1) * DV_C)
                yc_ref[rows(c), ys] = (_head_norm(hh, gm_ref[0:1, ys]) * _silu(z)).astype(BF16)

        _chunk_pipeline(n, piece, (stage_a, stage_b, stage_c, stage_d, stage_e), gate)
        for i in range(CG):
            c_scr[i] = cst[n][i]
            n_scr[i] = nst[n][i]
            m_scr[i] = jnp.broadcast_to(mst[n][i], (8, 128))
            c_ref[0, i] = cst[n][i]
            n_ref[0, i] = nst[n][i]
            m_ref[0, i] = jnp.broadcast_to(mst[n][i], (1, 128))


def _odd_heads(hp, hs, w_t, bg, gm):
    k = hp.shape[1]
    ms = hs.shape[0]
    tile = lambda s: jnp.maximum(s - 1, 0)
    qw, vw = CG * DQK_C, CG * DV_C
    koff = H_C * DQK_C // qw
    voff = 2 * H_C * DQK_C // vw
    zoff = (2 * H_C * DQK_C + W_C) // vw
    seq = lambda g, s: (tile(s) // CPB, g, 0, 0)
    once = pl.Buffered(1)
    return pl.pallas_call(
        _odd_heads_kernel,
        grid=(H_C // CG, CN + 1),
        in_specs=[pl.BlockSpec((CT, k), lambda g, s: (tile(s), 0)),
                  pl.BlockSpec((ms, k), lambda g, s: (0, 0), pipeline_mode=once),
                  pl.BlockSpec((qw, k), lambda g, s: (g, 0), pipeline_mode=once),
                  pl.BlockSpec((qw, k), lambda g, s: (koff + g, 0), pipeline_mode=once),
                  pl.BlockSpec((vw, k), lambda g, s: (voff + g, 0), pipeline_mode=once),
                  pl.BlockSpec((vw, k), lambda g, s: (zoff + g, 0), pipeline_mode=once),
                  pl.BlockSpec((128, k), lambda g, s: (O_GATE // 128, 0), pipeline_mode=once),
                  pl.BlockSpec((1, 128), lambda g, s: (0, 0)),
                  pl.BlockSpec((1, vw), lambda g, s: (0, g))],
        out_specs=(pl.BlockSpec((CT, vw), lambda g, s: (tile(s), g)),
                   pl.BlockSpec((1, CG, DV_C, DQK_C), seq),
                   pl.BlockSpec((1, CG, 1, DQK_C), seq),
                   pl.BlockSpec((1, CG, 1, 128), seq),
                   pl.BlockSpec((ms, qw), lambda g, s: (0, g)),
                   pl.BlockSpec((ms, qw), lambda g, s: (0, g)),
                   pl.BlockSpec((ms, vw), lambda g, s: (0, g)),
                   pl.BlockSpec((ms, vw), lambda g, s: (0, g))),
        out_shape=(jax.ShapeDtypeStruct((BATCH * SEQ, W_C), BF16),
                   jax.ShapeDtypeStruct((BATCH, H_C, DV_C, DQK_C), F32),
                   jax.ShapeDtypeStruct((BATCH, H_C, 1, DQK_C), F32),
                   jax.ShapeDtypeStruct((BATCH, H_C, 1, 128), F32),
                   jax.ShapeDtypeStruct((ms, H_C * DQK_C), F32),
                   jax.ShapeDtypeStruct((ms, H_C * DQK_C), F32),
                   jax.ShapeDtypeStruct((ms, W_C), F32),
                   jax.ShapeDtypeStruct((ms, W_C), F32)),
        scratch_shapes=[pltpu.VMEM((CW, k), BF16),
                        pltpu.VMEM((CT, CW), F32),
                        pltpu.VMEM((CG, DV_C, DQK_C), F32),
                        pltpu.VMEM((CG, 1, DQK_C), F32),
                        pltpu.VMEM((CG, 8, 128), F32),
                        pltpu.VMEM((CT // CHUNK, CHUNK, CHUNK), F32)],
        compiler_params=_params(("arbitrary", "arbitrary")),
        name="odd_heads",
    )(hp, hs, w_t, w_t, w_t, w_t, w_t, bg, gm)


DT = 512


def _odd_mlp_fused_kernel(hp_ref, w0_ref, w1_ref, w2_ref, wt_ref, lng_ref, lnb_ref, ws_ref, bst_ref,
                          yd_ref, wb, pt, wsb):
    s = pl.program_id(0)
    L = CHUNK
    uc = slice(0, W_D)
    vc = slice(W_D, 2 * W_D)
    zc = slice(2 * W_D, 3 * W_D)

    @pl.when(s == 0)
    def _():
        sh = N_GATE
        wb[uc, :] = jnp.concatenate([w0_ref[sh:, :], w1_ref[0:sh, :]], axis=0).astype(BF16)
        wb[vc, :] = jnp.concatenate([w1_ref[sh:, :], w2_ref[0:sh, :]], axis=0).astype(BF16)
        wb[zc, :] = jnp.concatenate([w2_ref[sh:, :], wt_ref[...]], axis=0).astype(BF16)
        keep = (lax.broadcasted_iota(jnp.int32, (L, L), 0) >= lax.broadcasted_iota(jnp.int32, (L, L), 1))
        for g in range(G_D):
            wsb[g] = jnp.where(keep, ws_ref[g], 0.0).astype(BF16)

    @pl.when(s > 0)
    def _():
        n = DT // L
        rows = lambda c: slice(c * L, (c + 1) * L)
        grp = lambda sl, g: slice(sl.start + g * 128, sl.start + (g + 1) * 128)
        vn = {}
        mix = {}

        def project(pc):
            pt[:, pc] = _dot_nt(hp_ref[...], wb[pc, :])

        def stage_a(c):
            dv = lambda g: pt[rows(c), grp(vc, g)]
            tot = dv(0)
            for g in range(1, G_D):
                tot = tot + dv(g)
            mu = jnp.sum(tot, axis=-1, keepdims=True) * (1.0 / W_D)
            sq = (dv(0) - mu) * (dv(0) - mu)
            for g in range(1, G_D):
                sq = sq + (dv(g) - mu) * (dv(g) - mu)
            rstd = lax.rsqrt(jnp.sum(sq, axis=-1, keepdims=True) * (1.0 / W_D) + EPS)
            vn[c] = [((dv(g) - mu) * rstd * lng_ref[0:1, g * 128:(g + 1) * 128]
                      + lnb_ref[0:1, g * 128:(g + 1) * 128]).astype(BF16) for g in range(G_D)]

        def stage_b(c):
            mix[c] = [_dot(wsb[g], vn[c][g]) for g in range(G_D)]

        def stage_e(c):
            for g in range(G_D):
                sg = mix[c][g] + bst_ref[:, g:g + 1]
                d_u = pt[rows(c), grp(uc, g)]
                d_z = pt[rows(c), grp(zc, g)]
                yd_ref[rows(c), g * 128:(g + 1) * 128] = (d_u * sg * _silu(d_z)).astype(BF16)

        project(vc)
        for c in range(n):
            stage_a(c)
        project(uc)
        for c in range(n):
            stage_b(c)
        project(zc)
        for c in range(n):
            stage_e(c)


def _odd_mlp_fused(hp, w_t, lng, lnb, ws, bst):
    k = hp.shape[1]
    m = hp.shape[0]
    tile = lambda s: (jnp.maximum(s - 1, 0), 0)
    const2 = lambda s: (0, 0)
    t0 = O_GATE // IN_TN
    once = pl.Buffered(1)
    wspec = lambda j: pl.BlockSpec((IN_TN, k), lambda s: (t0 + j, 0), pipeline_mode=once)
    return pl.pallas_call(
        _odd_mlp_fused_kernel,
        grid=(m // DT + 1,),
        in_specs=[pl.BlockSpec((DT, k), tile),
                  wspec(0), wspec(1), wspec(2),
                  pl.BlockSpec((N_GATE, k), lambda s: ((t0 + 3) * (IN_TN // N_GATE), 0), pipeline_mode=once),
                  pl.BlockSpec((1, W_D), const2),
                  pl.BlockSpec((1, W_D), const2),
                  pl.BlockSpec((G_D, CHUNK, CHUNK), lambda s: (0, 0, 0)),
                  pl.BlockSpec((CHUNK, G_D), const2)],
        out_specs=pl.BlockSpec((DT, W_D), tile),
        out_shape=jax.ShapeDtypeStruct((m, W_D), BF16),
        scratch_shapes=[pltpu.VMEM((3 * W_D, k), BF16),
                        pltpu.VMEM((DT, 3 * W_D), F32),
                        pltpu.VMEM((G_D, CHUNK, CHUNK), BF16)],
        compiler_params=_params(("arbitrary",)),
        name="odd_mlp_fused",
    )(hp, w_t, w_t, w_t, w_t, lng, lnb, ws, bst)


def _in_proj_rows_kernel(h_ref, w_ref, wn_ref, o_ref, *, shift):
    wsh = jnp.concatenate([w_ref[shift:, :], wn_ref[...]], axis=0)
    o_ref[...] = _dot_nt(h_ref[...], wsh.astype(BF16))


def _in_proj_rows(h, w_t, n_out, shift, tile0):
    ms, k = h.shape
    return pl.pallas_call(
        functools.partial(_in_proj_rows_kernel, shift=shift),
        grid=(n_out // IN_TN,),
        in_specs=[pl.BlockSpec((ms, k), lambda j: (0, 0)),
                  pl.BlockSpec((IN_TN, k), lambda j: (j + tile0, 0)),
                  pl.BlockSpec((shift, k), lambda j: ((j + tile0 + 1) * (IN_TN // shift), 0))],
        out_specs=pl.BlockSpec((ms, IN_TN), lambda j: (0, j)),
        out_shape=jax.ShapeDtypeStruct((ms, n_out), F32),
        compiler_params=_params(("arbitrary",)),
        name="in_proj_rows",
    )(h, w_t, w_t)


def _odd_sample_kernel(pq_ref, pk_ref, pv_ref, pz_ref, h_ref, wg_ref, pd_ref,
                       c_ref, nrow_ref, mrow_ref, bg_ref, gm_ref, lng_ref, lnb_ref,
                       wt_ref, bt_ref,
                       yc_ref, yd_ref, vn_ref, co_ref, no_ref, mo_ref,
                       inter_scr, wmix_scr):
    h = pl.program_id(1)
    row = lax.broadcasted_iota(jnp.int32, (SR, SR), 0)
    col = lax.broadcasted_iota(jnp.int32, (SR, SR), 1)
    trow = row & 3

    @pl.when(jnp.logical_and(pl.program_id(0) == 0, h == 0))
    def _():
        mask_d = jnp.where((row >> 2) == (col >> 2), trow - (col & 3), -1) >= 0
        rep = jnp.where(trow == col, 1.0, 0.0).astype(BF16)
        for g in range(G_D):
            wtile = _dot_nt(_dot(rep, wt_ref[g].astype(BF16)).astype(BF16), rep)
            wmix_scr[g] = jnp.where(mask_d, wtile, 0.0).astype(BF16)

    @pl.when(h == 0)
    def _():
        dv = pd_ref[:, W_D:2 * W_D]
        mu = jnp.mean(dv, axis=-1, keepdims=True)
        xc = dv - mu
        var = jnp.mean(xc * xc, axis=-1, keepdims=True)
        rstd = lax.rsqrt(var + EPS)
        for g in range(G_D):
            sl = slice(g * 128, (g + 1) * 128)
            vn = xc[:, sl] * rstd * lng_ref[0:1, sl] + lnb_ref[0:1, sl]
            vn_ref[:, sl] = vn
            s = _dot(wmix_scr[g], vn.astype(BF16)) + bt_ref[:, g:g + 1]
            d_u = pd_ref[:, g * 128:(g + 1) * 128]
            d_z = pd_ref[:, 2 * W_D + g * 128:2 * W_D + (g + 1) * 128]
            yd_ref[:, sl] = (d_u * s * _silu(d_z)).astype(BF16)

    same = (row >> 2) == (col >> 2)
    mask = jnp.where(same, trow - (col & 3), -1) >= 0
    pre = _dot_nt(h_ref[...], wg_ref[...].astype(BF16)) + bg_ref[...]
    lf = _log_sigmoid(pre)
    b_full = _dot_hi(jnp.where(mask, 1.0, 0.0), lf)
    sel_i = col == h
    sel_f = col == h + H_C
    ig_c = jnp.sum(jnp.where(sel_i, pre, 0.0), axis=-1, keepdims=True)
    b_c = jnp.sum(jnp.where(sel_f, b_full, 0.0), axis=-1, keepdims=True)
    sel_ir = row == h
    sel_fr = row == h + H_C
    ig_r = jnp.sum(jnp.where(sel_ir, pre.T, 0.0), axis=0, keepdims=True)
    b_r = jnp.sum(jnp.where(sel_fr, b_full.T, 0.0), axis=0, keepdims=True)
    m_prev = mrow_ref[0]
    log_d = jnp.where(mask, b_c - b_r + ig_r, NEG_INF)
    log_inter = b_c + m_prev
    m_t = jnp.maximum(log_inter, jnp.max(log_d, axis=-1, keepdims=True))
    w = jnp.exp(log_d - m_t)
    w_inter = jnp.exp(log_inter - m_t)
    q = pq_ref[...] * (DQK_C ** -0.5)
    k = pk_ref[...]
    v = pv_ref[...]
    qb = q.astype(BF16)
    kb = k.astype(BF16)
    vb = v.astype(BF16)
    sc = _dot_nt(qb, kb) * w
    sub8 = lax.broadcasted_iota(jnp.int32, (8, DQK_C), 0)
    for g in range(SR // 8):
        q8 = q[8 * g:8 * g + 8, :]
        q2 = jnp.concatenate([jnp.where(sub8 < DEC_SEQ, q8, 0.0), jnp.where(sub8 < DEC_SEQ, 0.0, q8)], axis=1)
        c_pair = jnp.concatenate([c_ref[2 * g + beta, 0].astype(BF16) for beta in range(2)], axis=1)
        inter_scr[8 * g:8 * g + 8, :] = _dot_nt(q2.astype(BF16), c_pair)
    n_rows = nrow_ref[0]
    num = _dot(sc.astype(BF16), vb) + w_inter * inter_scr[...]
    den = jnp.sum(sc, axis=-1, keepdims=True) + w_inter * jnp.sum(q * n_rows, axis=-1, keepdims=True)
    hh = num / jnp.maximum(jnp.abs(den), jnp.exp(-m_t))
    yc_ref[...] = (_head_norm(hh, gm_ref[...]) * _silu(pz_ref[...])).astype(BF16)

    stats = jnp.where(col == 0, m_t, jnp.where(col == 1, b_c, 0.0))
    last = _dot_hi(jnp.where(col == (row | 3), 1.0, 0.0), stats)
    m_new = last[:, 0:1]
    b_last = last[:, 1:2]
    w_end = jnp.exp(b_last - b_c + ig_c - m_new)
    cd = jnp.exp(b_last + m_prev - m_new)
    mo_ref[0] = m_new
    no_ref[0] = cd * n_rows + _dot_hi(jnp.where(same, 1.0, 0.0), w_end * k)
    vwt = (v * w_end).T
    lane_b = lax.broadcasted_iota(jnp.int32, (DV_C, SR), 1) >> 2
    for b in range(SB):
        lhs = jnp.where(lane_b == b, vwt, 0.0).astype(BF16)
        cd_b = cd[4 * b + 3:4 * b + 4, :]
        co_ref[b, 0] = cd_b * c_ref[b, 0] + _dot(lhs, kb)


def _odd_sample(pc, pd, h, w_o, c_state, n_rows, m_rows, bg, gm, lng, lnb, wt, bt):
    nb = DEC_BATCH // SB
    const2 = lambda i, h: (0, 0)
    return pl.pallas_call(
        _odd_sample_kernel,
        grid=(nb, H_C),
        in_specs=[pl.BlockSpec((SR, DQK_C), lambda i, h: (i, h)),
                  pl.BlockSpec((SR, DQK_C), lambda i, h: (i, h)),
                  pl.BlockSpec((SR, DV_C), lambda i, h: (i, h)),
                  pl.BlockSpec((SR, DV_C), lambda i, h: (i, h)),
                  pl.BlockSpec((SR, D_MODEL), lambda i, h: (i, 0)),
                  pl.BlockSpec((128, D_MODEL), lambda i, h: (O_GATE // 128, 0)),
                  pl.BlockSpec((SR, 3 * W_D), lambda i, h: (i, 0)),
                  pl.BlockSpec((SB, 1, DV_C, DQK_C), lambda i, h: (i, h, 0, 0)),
                  pl.BlockSpec((1, SR, DQK_C), lambda i, h: (h, i, 0)),
                  pl.BlockSpec((1, SR, 1), lambda i, h: (h, i, 0)),
                  pl.BlockSpec((1, 128), const2),
                  pl.BlockSpec((1, DV_C), lambda i, h: (0, h)),
                  pl.BlockSpec((1, W_D), const2),
                  pl.BlockSpec((1, W_D), const2),
                  pl.BlockSpec((G_D, SR, SR), lambda i, h: (0, 0, 0)),
                  pl.BlockSpec((SR, G_D), const2)],
        out_specs=(pl.BlockSpec((SR, DV_C), lambda i, h: (i, h)),
                   pl.BlockSpec((SR, W_D), lambda i, h: (i, 0)),
                   pl.BlockSpec((SR, W_D), lambda i, h: (i, 0)),
                   pl.BlockSpec((SB, 1, DV_C, DQK_C), lambda i, h: (i, h, 0, 0)),
                   pl.BlockSpec((1, SR, DQK_C), lambda i, h: (h, i, 0)),
                   pl.BlockSpec((1, SR, 1), lambda i, h: (h, i, 0))),
        out_shape=(jax.ShapeDtypeStruct((DEC_BATCH * DEC_SEQ, W_C), BF16),
                   jax.ShapeDtypeStruct((DEC_BATCH * DEC_SEQ, W_D), BF16),
                   jax.ShapeDtypeStruct((DEC_BATCH * DEC_SEQ, W_D), F32),
                   jax.ShapeDtypeStruct((DEC_BATCH, H_C, DV_C, DQK_C), F32),
                   jax.ShapeDtypeStruct((H_C, DEC_BATCH * DEC_SEQ, DQK_C), F32),
                   jax.ShapeDtypeStruct((H_C, DEC_BATCH * DEC_SEQ, 1), F32)),
        scratch_shapes=[pltpu.VMEM((SR, DV_C), F32),
                        pltpu.VMEM((G_D, SR, SR), BF16)],
        compiler_params=_params(("arbitrary", "arbitrary")),
        name="odd_sample",
    )(*pc, h, w_o, pd, c_state, n_rows, m_rows, bg, gm, lng, lnb, wt, bt)


def _rope_tables(pos):
    inv = ROPE_BASE ** (-jnp.arange(0, DH_B, 2, dtype=F32) / DH_B)
    ang = pos.astype(F32)[:, None] * inv[None, :]
    cos = jnp.cos(ang)
    sin = jnp.sin(ang)
    return jnp.concatenate([cos, cos], axis=-1), jnp.concatenate([-sin, sin], axis=-1)


def kernel(x_prompt, x_sample, state_conv, state_ret, state_mlstm_C, state_mlstm_n, state_mlstm_m,
           norm_even, w_in_even, conv_w, ret_norm, w_out_even,
           norm_odd, w_in_odd, b_gate_odd, mlstm_norm, ln_v_g, ln_v_b,
           w_spatial, b_spatial, w_out_odd, norm_final):
    w_in_e = w_in_even[0]
    w_out_e = w_out_even[0].astype(BF16)
    w_o = w_in_odd[0].T
    w_out_o = w_out_odd[0].astype(BF16)
    g_even = norm_even[0][None, :]
    g_odd = norm_odd[0][None, :]
    g_fin = norm_final[None, :]
    cw = conv_w[0]
    g_ret = ret_norm[0][None, :]
    bg = jnp.concatenate([b_gate_odd[0], jnp.zeros((128 - 2 * H_C,), F32)])[None, :]
    gm = mlstm_norm[0][None, :]
    lng = ln_v_g[0][None, :]
    lnb = ln_v_b[0][None, :]
    ws = w_spatial[0]
    bst = b_spatial[0].T

    cos_p, sin_p = _rope_tables(jnp.arange(SEQ, dtype=jnp.int32))
    cos_s, sin_s = _rope_tables(PAST_LEN + jnp.arange(DEC_SEQ, dtype=jnp.int32))
    cos_s = jnp.tile(cos_s, (SB, 1))
    sin_s = jnp.tile(sin_s, (SB, 1))
    lg_tab = jnp.broadcast_to(jnp.asarray(LOG_GAMMA, F32)[:, None, None], (H_B, 1, 128))

    bt_s = jnp.tile(b_spatial[0][:, :DEC_SEQ].T, (SB, 1))

    rs = DEC_BATCH * DEC_SEQ
    xp = x_prompt.reshape(BATCH * SEQ, D_MODEL)
    xs = x_sample.reshape(rs, D_MODEL)
    hp, hs = _norm_cast(xp, xs, g_even, 1024)
    ya, conv_p, *ps_a = _even_conv(hp, hs, w_in_e, cw)
    yb, ret_p, *ps_b = _even_heads(hp, hs, w_in_e, g_ret, cos_p, sin_p, lg_tab)
    st_exp = jnp.pad(state_conv[0], ((0, 0), (0, DEC_SEQ - (CONV_W - 1)), (0, 0))).reshape(rs, W_A)
    ya_s, u_s, yb_s, ret_s = _even_sample(ps_a, ps_b, st_exp, state_ret[0], cw, g_ret, cos_s, sin_s, lg_tab)
    x1, h1, x1s, h1s = _outproj(ya, yb, xp, ya_s, yb_s, xs, w_out_e, g_odd, final=False)

    yc, c_p, n_p, m_p, *ps_c = _odd_heads(h1, h1s, w_o, bg, gm)
    yd = _odd_mlp_fused(h1, w_o, lng, lnb, ws, bst)
    pd_s = _in_proj_rows(h1s, w_o, 3 * W_D, N_GATE, O_GATE // IN_TN)
    n_rows = jnp.repeat(jnp.transpose(state_mlstm_n[0], (1, 0, 2)), DEC_SEQ, axis=1)
    m_rows = jnp.repeat(state_mlstm_m[0].T, DEC_SEQ, axis=1)[:, :, None]
    yc_s, yd_s, vn_s, c_s, no_s, mo_s = _odd_sample(
        ps_c, pd_s, h1s, w_o, state_mlstm_C[0], n_rows, m_rows, bg, gm, lng, lnb, ws, bt_s)
    y_prompt, y_sample = _outproj(yc, yd, x1, yc_s, yd_s, x1s, w_out_o, g_fin, final=True)

    conv_s = u_s.reshape(DEC_BATCH, DEC_SEQ, W_A)[:, DEC_SEQ - (CONV_W - 1):, :]
    n_s = jnp.transpose(no_s[:, DEC_SEQ - 1::DEC_SEQ, :], (1, 0, 2))
    m_s = mo_s[:, DEC_SEQ - 1::DEC_SEQ, 0].T
    return (y_prompt.reshape(BATCH, SEQ, D_MODEL),
            y_sample.reshape(DEC_BATCH, DEC_SEQ, D_MODEL),
            conv_p[None], conv_s[None],
            ret_p[None], ret_s[None],
            c_p[None], c_s[None],
            n_p[:, :, 0, :][None], n_s[None],
            m_p[:, :, 0, 0][None], m_s[None],
            vn_s.reshape(DEC_BATCH, DEC_SEQ, W_D)[None])
```

```python
import functools
import math

import jax
import jax.numpy as jnp
from jax import lax
from jax.experimental import pallas as pl
from jax.experimental.pallas import tpu as pltpu

F32 = jnp.float32
BF16 = jnp.bfloat16

D_MODEL = 2048
BATCH = 4
SEQ = 2048
DEC_BATCH = 128
DEC_SEQ = 4
PAST_LEN = 16384
W_A = 1024
CONV_W = 3
W_B = 1024
H_B = 8
DH_B = 128
W_C = 1024
H_C = 4
DV_C = 256
DQK_C = 128
W_D = 1024
G_D = 8
CHUNK = 128
O_GATE = 2 * H_C * DQK_C + 2 * W_C
N_GATE = 2 * H_C
ROPE_BASE = 10000.0
EPS = 1e-6
LOG_GAMMA = tuple(math.log(1.0 - 2.0 ** (-5.0 - h)) for h in range(H_B))
NEG_INF = float("-inf")
VMEM_LIMIT = 56 * 1024 * 1024

NT_DIMS = (((1,), (1,)), ((), ()))
TN_DIMS = (((0,), (0,)), ((), ()))


def _silu(z):
    return z * (1.0 / (1.0 + jnp.exp(-z)))


def _log_sigmoid(x):
    return jnp.minimum(x, 0.0) - jnp.log1p(jnp.exp(-jnp.abs(x)))


def _dot(a, b):
    return jnp.dot(a, b, preferred_element_type=F32)


def _dot_nt(a, b):
    return lax.dot_general(a, b, NT_DIMS, preferred_element_type=F32)


def _dot_tn(a, b):
    return lax.dot_general(a, b, TN_DIMS, preferred_element_type=F32)


def _dot_hi(a, b):
    return jnp.dot(a, b, preferred_element_type=F32, precision=lax.Precision.HIGHEST)


def _head_norm(o, g):
    mu = jnp.mean(o, axis=-1, keepdims=True)
    oc = o - mu
    var = jnp.mean(oc * oc, axis=-1, keepdims=True)
    return oc * lax.rsqrt(var + EPS) * g


def _params(sem):
    return pltpu.CompilerParams(dimension_semantics=sem, vmem_limit_bytes=VMEM_LIMIT)


def _norm_cast_kernel(x_ref, xs_ref, g_ref, h_ref, hs_ref, *, n_prompt):
    i = pl.program_id(0)

    def tile(x_in, h_out):
        x = x_in[...]
        ms = jnp.mean(x * x, axis=-1, keepdims=True)
        h_out[...] = (x * lax.rsqrt(ms + EPS) * g_ref[...]).astype(BF16)

    @pl.when(i < n_prompt)
    def _():
        tile(x_ref, h_ref)

    @pl.when(i == n_prompt)
    def _():
        tile(xs_ref, hs_ref)


def _norm_cast(x, xs, g, tm):
    m, d = x.shape
    ms = xs.shape[0]
    n_prompt = m // tm
    row = lambda i: (jnp.minimum(i, n_prompt - 1), 0)
    const = lambda i: (0, 0)
    return pl.pallas_call(
        functools.partial(_norm_cast_kernel, n_prompt=n_prompt),
        grid=(n_prompt + 1,),
        in_specs=[pl.BlockSpec((tm, d), row),
                  pl.BlockSpec((ms, d), const),
                  pl.BlockSpec((1, d), const)],
        out_specs=(pl.BlockSpec((tm, d), row), pl.BlockSpec((ms, d), const)),
        out_shape=(jax.ShapeDtypeStruct((m, d), BF16), jax.ShapeDtypeStruct((ms, d), BF16)),
        compiler_params=_params(("arbitrary",)),
        name="norm_cast",
    )(x, xs, g)


IN_TN = 1024


OUT_TM = 512


def _outproj_kernel(ya_ref, yb_ref, x_ref, yas_ref, ybs_ref, xs_ref, w_ref, g_ref, *out_refs,
                    final, n_prompt):
    i = pl.program_id(0)
    half = ya_ref.shape[1]
    n_out = 1 if final else 2

    def tile(ya, yb, x, outs):
        acc = _dot(ya[...], w_ref[0:half, :]) + _dot(yb[...], w_ref[half:2 * half, :])
        x1 = x[...] + acc
        ms = jnp.mean(x1 * x1, axis=-1, keepdims=True)
        hn = x1 * lax.rsqrt(ms + EPS) * g_ref[...]
        if final:
            outs[0][...] = hn
        else:
            outs[0][...] = x1
            outs[1][...] = hn.astype(BF16)

    @pl.when(i < n_prompt)
    def _():
        tile(ya_ref, yb_ref, x_ref, out_refs[:n_out])

    @pl.when(i == n_prompt)
    def _():
        tile(yas_ref, ybs_ref, xs_ref, out_refs[n_out:])


def _outproj(ya, yb, x, ya_s, yb_s, x_s, w, g, final):
    m, half = ya.shape
    ms = ya_s.shape[0]
    d = w.shape[1]
    n_prompt = m // OUT_TM
    row = lambda i: (jnp.minimum(i, n_prompt - 1), 0)
    const = lambda i: (0, 0)
    once = pl.Buffered(1)
    shapes = [jax.ShapeDtypeStruct((m, d), F32), jax.ShapeDtypeStruct((ms, d), F32)]
    specs = [pl.BlockSpec((OUT_TM, d), row), pl.BlockSpec((ms, d), const)]
    if not final:
        shapes = [shapes[0], jax.ShapeDtypeStruct((m, d), BF16), shapes[1], jax.ShapeDtypeStruct((ms, d), BF16)]
        specs = [specs[0], pl.BlockSpec((OUT_TM, d), row), specs[1], pl.BlockSpec((ms, d), const)]
    return pl.pallas_call(
        functools.partial(_outproj_kernel, final=final, n_prompt=n_prompt),
        grid=(n_prompt + 1,),
        in_specs=[pl.BlockSpec((OUT_TM, half), row),
                  pl.BlockSpec((OUT_TM, half), row),
                  pl.BlockSpec((OUT_TM, d), row),
                  pl.BlockSpec((ms, half), const, pipeline_mode=once),
                  pl.BlockSpec((ms, half), const, pipeline_mode=once),
                  pl.BlockSpec((ms, d), const, pipeline_mode=once),
                  pl.BlockSpec((2 * half, d), const, pipeline_mode=once),
                  pl.BlockSpec((1, d), const)],
        out_specs=tuple(specs),
        out_shape=tuple(shapes),
        compiler_params=_params(("arbitrary",)),
        name="out_proj_final" if final else "out_proj",
    )(ya, yb, x, ya_s, yb_s, x_s, w, g)


def _rope(x, cosf, sins):
    return x * cosf + pltpu.roll(x, DH_B // 2, 1) * sins


FT = 1024
FN = BATCH * SEQ // FT
FPB = SEQ // FT
HG = 2
GW = HG * 128
PCH = 2


def _chunk_pipeline(n, piece, stages, gate=None):
    sa, sb, sc, sd, se = stages
    npieces = n // PCH
    for j in range(PCH):
        piece(0, j)
    if gate is not None:
        gate(0)
    for c in range(n + 2):
        k, j = c // PCH + 1, c % PCH
        if k < npieces:
            piece(k, j)
        if c < n:
            sa(c)
        if 1 <= c <= n:
            sc(c - 1)
        if c < n:
            sb(c)
        if 1 <= c <= n:
            sd(c - 1)
        if c >= 2:
            se(c - 2)
        if gate is not None and j == PCH - 1 and k < npieces:
            gate(k)


def _even_heads_kernel(hp_ref, hs_ref, wq_ref, wk_ref, wv_ref, wz_ref, gret_ref, cos_ref, sin_ref, lg_ref,
                       yb_ref, s_ref, sq_ref, sk_ref, sv_ref, sz_ref, wb, pt, s_scr):
    s = pl.program_id(1)
    L = CHUNK

    @pl.when(s == 0)
    def _():
        for part, w_ref in enumerate((wq_ref, wk_ref, wv_ref, wz_ref)):
            wb[:, part * GW:(part + 1) * GW] = w_ref[...].astype(BF16)
        ps = _dot(hs_ref[...], wb[...])
        for part, o_ref in enumerate((sq_ref, sk_ref, sv_ref, sz_ref)):
            o_ref[...] = ps[:, part * GW:(part + 1) * GW]

    @pl.when(s > 0)
    def _():
        t = s - 1
        n = FT // L
        rows = lambda c: slice(c * L, (c + 1) * L)
        cols = lambda part, i: slice(part * GW + i * DH_B, part * GW + (i + 1) * DH_B)

        def piece(k, j):
            pr = slice(k * PCH * L, (k + 1) * PCH * L)
            pc = slice(j * 2 * GW, (j + 1) * 2 * GW)
            pt[pr, pc] = _dot(hp_ref[pr, :], wb[:, pc])

        row = lax.broadcasted_iota(jnp.int32, (L, L), 0)
        col = lax.broadcasted_iota(jnp.int32, (L, L), 1)
        causal = row >= col
        diff = jnp.maximum(row - col, 0).astype(F32)
        ti = lax.broadcasted_iota(jnp.int32, (L, 1), 0).astype(F32)
        lgs = [lg_ref[i][:, 0:1] for i in range(HG)]
        decay = [jnp.where(causal, jnp.exp(lg * diff), 0.0) for lg in lgs]
        q_decay = [jnp.exp(lg * (ti + 1.0)) for lg in lgs]
        k_decay = [jnp.exp(lg * (L - 1.0 - ti)) for lg in lgs]
        gamma_l = [jnp.exp(lg * float(L)) for lg in lgs]
        state = {0: [jnp.where(t % FPB == 0, 0.0, s_scr[i]) for i in range(HG)]}
        v = {}

        def stage_a(c):
            cosf = cos_ref[rows(c), :]
            sins = sin_ref[rows(c), :]
            v[c] = []
            for i in range(HG):
                kr = _rope(pt[rows(c), cols(1, i)], cosf, sins) * (DH_B ** -0.5)
                v[c].append(dict(qb=_rope(pt[rows(c), cols(0, i)], cosf, sins).astype(BF16),
                                 kb=kr.astype(BF16),
                                 kd=(kr * k_decay[i]).astype(BF16),
                                 vb=pt[rows(c), cols(2, i)].astype(BF16)))

        def stage_b(c):
            for i, d in enumerate(v[c]):
                d["sc"] = _dot_nt(d["qb"], d["kb"])
                d["upd"] = _dot_tn(d["kd"], d["vb"])
            for i, d in enumerate(v[c]):
                d["cross"] = _dot(d["qb"], state[c][i].astype(BF16))

        def stage_c(c):
            state[c + 1] = []
            for i, d in enumerate(v[c]):
                d["sc"] = (d["sc"] * decay[i]).astype(BF16)
                state[c + 1].append(gamma_l[i] * state[c][i] + d["upd"])

        def stage_d(c):
            for d in v[c]:
                d["inner"] = _dot(d["sc"], d["vb"])

        def stage_e(c):
            for i, d in enumerate(v.pop(c)):
                o = d["inner"] + d["cross"] * q_decay[i]
                g = gret_ref[0:1, i * DH_B:(i + 1) * DH_B]
                z = pt[rows(c), cols(3, i)]
                yb_ref[rows(c), i * DH_B:(i + 1) * DH_B] = (_head_norm(o, g) * _silu(z)).astype(BF16)

        _chunk_pipeline(n, piece, (stage_a, stage_b, stage_c, stage_d, stage_e))
        for i in range(HG):
            s_scr[i] = state[n][i]
            s_ref[0, i] = state[n][i]


def _even_heads(hp, hs, w, g_ret, cosf, sins, lg_tab):
    k = hp.shape[1]
    ms = hs.shape[0]
    ng = H_B // HG
    base = 4 * W_A // GW
    tile = lambda s: jnp.maximum(s - 1, 0)
    wspec = lambda part: pl.BlockSpec((k, GW), lambda g, s: (0, base + part * ng + g))
    sspec = pl.BlockSpec((ms, GW), lambda g, s: (0, g))
    sshape = jax.ShapeDtypeStruct((ms, W_B), F32)
    return pl.pallas_call(
        _even_heads_kernel,
        grid=(ng, FN + 1),
        in_specs=[pl.BlockSpec((FT, k), lambda g, s: (tile(s), 0)),
                  pl.BlockSpec((ms, k), lambda g, s: (0, 0)),
                  wspec(0), wspec(1), wspec(2), wspec(3),
                  pl.BlockSpec((1, GW), lambda g, s: (0, g)),
                  pl.BlockSpec((FT, DH_B), lambda g, s: (tile(s) % FPB, 0)),
                  pl.BlockSpec((FT, DH_B), lambda g, s: (tile(s) % FPB, 0)),
                  pl.BlockSpec((HG, 1, 128), lambda g, s: (g, 0, 0))],
        out_specs=(pl.BlockSpec((FT, GW), lambda g, s: (tile(s), g)),
                   pl.BlockSpec((1, HG, DH_B, DH_B), lambda g, s: (tile(s) // FPB, g, 0, 0)),
                   sspec, sspec, sspec, sspec),
        out_shape=(jax.ShapeDtypeStruct((BATCH * SEQ, W_B), BF16),
                   jax.ShapeDtypeStruct((BATCH, H_B, DH_B, DH_B), F32),
                   sshape, sshape, sshape, sshape),
        scratch_shapes=[pltpu.VMEM((k, 4 * GW), BF16),
                        pltpu.VMEM((FT, 4 * GW), F32),
                        pltpu.VMEM((HG, DH_B, DH_B), F32)],
        compiler_params=_params(("arbitrary", "arbitrary")),
        name="even_heads",
    )(hp, hs, w, w, w, w, g_ret, cosf, sins, lg_tab)


def _even_conv_kernel(hp_ref, hs_ref, wb_ref, wc_ref, wx_ref, wz_ref, cw_ref,
                      ya_ref, conv_ref, sb_ref, sc_ref, sx_ref, sz_ref, wb, pt, ubuf):
    s = pl.program_id(1)
    L = CHUNK
    sub = lambda j: slice(j * 4 * 128, (j + 1) * 4 * 128)
    cs = lambda j, p: slice(j * 4 * 128 + p * 128, j * 4 * 128 + (p + 1) * 128)

    @pl.when(s == 0)
    def _():
        for p, w_ref in enumerate((wb_ref, wc_ref, wx_ref, wz_ref)):
            for j in range(HG):
                wb[:, cs(j, p)] = w_ref[:, j * 128:(j + 1) * 128].astype(BF16)
        ps = _dot(hs_ref[...], wb[...])
        for p, o_ref in enumerate((sb_ref, sc_ref, sx_ref, sz_ref)):
            for j in range(HG):
                o_ref[:, j * 128:(j + 1) * 128] = ps[:, cs(j, p)]

    @pl.when(s > 0)
    def _():
        t = s - 1
        n = FT // L
        rows = lambda c: slice(c * L, (c + 1) * L)

        @pl.when(t % FPB == 0)
        def _():
            ubuf[0:8, :] = jnp.zeros((8, GW), F32)

        @pl.when(t % FPB != 0)
        def _():
            ubuf[0:8, :] = ubuf[FT:FT + 8, :]

        def project(j):
            pt[:, sub(j)] = _dot(hp_ref[...], wb[:, sub(j)])

        def mix(j):
            ch = slice(j * 128, (j + 1) * 128)
            for c in range(n):
                u = pt[rows(c), cs(j, 1)] * pt[rows(c), cs(j, 2)]
                ubuf[8 + c * L:8 + (c + 1) * L, ch] = u
                t0 = ubuf[6 + c * L:6 + (c + 1) * L, ch]
                t1 = ubuf[7 + c * L:7 + (c + 1) * L, ch]
                conv = cw_ref[0:1, ch] * t0 + cw_ref[1:2, ch] * t1 + cw_ref[2:3, ch] * u
                ya_ref[rows(c), ch] = (pt[rows(c), cs(j, 0)] * conv * _silu(pt[rows(c), cs(j, 3)])).astype(BF16)

        project(0)
        for j in range(HG):
            if j + 1 < HG:
                project(j + 1)
            mix(j)
        conv_ref[0] = ubuf[FT + 6:FT + 8, :]


def _even_conv(hp, hs, w, conv_w):
    k = hp.shape[1]
    ms = hs.shape[0]
    ng = W_A // GW
    tile = lambda s: jnp.maximum(s - 1, 0)
    wspec = lambda part: pl.BlockSpec((k, GW), lambda g, s: (0, part * ng + g))
    sspec = pl.BlockSpec((ms, GW), lambda g, s: (0, g))
    sshape = jax.ShapeDtypeStruct((ms, W_A), F32)
    return pl.pallas_call(
        _even_conv_kernel,
        grid=(ng, FN + 1),
        in_specs=[pl.BlockSpec((FT, k), lambda g, s: (tile(s), 0)),
                  pl.BlockSpec((ms, k), lambda g, s: (0, 0)),
                  wspec(0), wspec(1), wspec(2), wspec(3),
                  pl.BlockSpec((CONV_W, GW), lambda g, s: (0, g))],
        out_specs=(pl.BlockSpec((FT, GW), lambda g, s: (tile(s), g)),
                   pl.BlockSpec((1, CONV_W - 1, GW), lambda g, s: (tile(s) // FPB, 0, g)),
                   sspec, sspec, sspec, sspec),
        out_shape=(jax.ShapeDtypeStruct((BATCH * SEQ, W_A), BF16),
                   jax.ShapeDtypeStruct((BATCH, CONV_W - 1, W_A), F32),
                   sshape, sshape, sshape, sshape),
        scratch_shapes=[pltpu.VMEM((k, 4 * GW), BF16),
                        pltpu.VMEM((FT, 4 * GW), F32),
                        pltpu.VMEM((FT + 8, GW), F32)],
        compiler_params=_params(("arbitrary", "arbitrary")),
        name="even_conv",
    )(hp, hs, w, w, w, w, conv_w)


SB = 32
SR = SB * DEC_SEQ
HS = 2


def _even_sample_kernel(ab_ref, ac_ref, ax_ref, az_ref, pq_ref, pk_ref, pv_ref, pz_ref, st_ref, s_ref,
                        cw_ref, gret_ref, cos_ref, sin_ref, lg_ref,
                        ya_ref, u_ref, yb_ref, so_ref, cross_scr):
    h = pl.program_id(1)
    row = lax.broadcasted_iota(jnp.int32, (SR, SR), 0)
    col = lax.broadcasted_iota(jnp.int32, (SR, SR), 1)
    trow = row & 3

    @pl.when(h == 0)
    def _():
        for j in range(W_A // 128):
            sl = slice(j * 128, (j + 1) * 128)
            a_b = ab_ref[:, sl]
            a_c = ac_ref[:, sl]
            a_x = ax_ref[:, sl]
            a_z = az_ref[:, sl]
            u = a_c * a_x
            e = st_ref[:, sl]
            tap1 = jnp.where(trow >= 1, pltpu.roll(u, 1, 0), pltpu.roll(e, SR - 1, 0))
            tap0 = jnp.where(trow >= 2, pltpu.roll(u, 2, 0), e)
            conv = cw_ref[0:1, sl] * tap0 + cw_ref[1:2, sl] * tap1 + cw_ref[2:3, sl] * u
            ya_ref[:, sl] = (a_b * conv * _silu(a_z)).astype(BF16)
            u_ref[:, sl] = u

    same = (row >> 2) == (col >> 2)
    dd = trow - (col & 3)
    mask = jnp.where(same, dd, -1) >= 0
    tcol = (lax.broadcasted_iota(jnp.int32, (SR, 1), 0) & 3).astype(F32)
    cosf = cos_ref[...]
    sins = sin_ref[...]
    lane_b = col >> 2
    sub = lax.broadcasted_iota(jnp.int32, (8, DH_B), 0)
    for hh in range(HS):
        hc = slice(hh * DH_B, (hh + 1) * DH_B)
        lg = lg_ref[hh][:, 0:1]
        decay = jnp.where(mask, jnp.exp(lg * jnp.maximum(dd, 0).astype(F32)), 0.0)
        qr = _rope(pq_ref[:, hc], cosf, sins)
        kr = _rope(pk_ref[:, hc], cosf, sins) * (DH_B ** -0.5)
        qb = qr.astype(BF16)
        kb = kr.astype(BF16)
        vb = pv_ref[:, hc].astype(BF16)
        sc = _dot_nt(qb, kb) * decay
        inner = _dot(sc.astype(BF16), vb)
        kdt = (kr * jnp.exp(lg * (DEC_SEQ - 1.0 - tcol))).T
        gamma_l = jnp.exp(lg * float(DEC_SEQ))
        for g in range(SR // 8):
            q8 = qr[8 * g:8 * g + 8, :]
            q2 = jnp.concatenate([jnp.where(sub < DEC_SEQ, q8, 0.0), jnp.where(sub < DEC_SEQ, 0.0, q8)], axis=1)
            s_pair = [s_ref[2 * g + beta, hh] for beta in range(2)]
            cross_scr[hh, 8 * g:8 * g + 8, :] = _dot(
                q2.astype(BF16), jnp.concatenate([sp.astype(BF16) for sp in s_pair], axis=0))
            for beta in range(2):
                b = 2 * g + beta
                lhs = jnp.where(lane_b == b, kdt, 0.0).astype(BF16)
                so_ref[b, hh] = gamma_l * s_pair[beta] + _dot(lhs, vb)
        o = inner + cross_scr[hh] * jnp.exp(lg * (tcol + 1.0))
        yb_ref[:, hc] = (_head_norm(o, gret_ref[:, hc]) * _silu(pz_ref[:, hc])).astype(BF16)


def _even_sample(pa, pb, st_exp, s_state, conv_w, g_ret, cosf, sins, lg_tab):
    nb = DEC_BATCH // SB
    const2 = lambda i, h: (0, 0)
    aspec = pl.BlockSpec((SR, W_A), lambda i, h: (i, 0))
    hspec = pl.BlockSpec((SR, HS * DH_B), lambda i, h: (i, h))
    return pl.pallas_call(
        _even_sample_kernel,
        grid=(nb, H_B // HS),
        in_specs=[aspec, aspec, aspec, aspec,
                  hspec, hspec, hspec, hspec,
                  pl.BlockSpec((SR, W_A), lambda i, h: (i, 0)),
                  pl.BlockSpec((SB, HS, DH_B, DH_B), lambda i, h: (i, h, 0, 0)),
                  pl.BlockSpec((CONV_W, W_A), const2),
                  pl.BlockSpec((1, HS * DH_B), lambda i, h: (0, h)),
                  pl.BlockSpec((SR, DH_B), const2),
                  pl.BlockSpec((SR, DH_B), const2),
                  pl.BlockSpec((HS, 1, 128), lambda i, h: (h, 0, 0))],
        out_specs=(pl.BlockSpec((SR, W_A), lambda i, h: (i, 0)),
                   pl.BlockSpec((SR, W_A), lambda i, h: (i, 0)),
                   pl.BlockSpec((SR, HS * DH_B), lambda i, h: (i, h)),
                   pl.BlockSpec((SB, HS, DH_B, DH_B), lambda i, h: (i, h, 0, 0))),
        out_shape=(jax.ShapeDtypeStruct((DEC_BATCH * DEC_SEQ, W_A), BF16),
                   jax.ShapeDtypeStruct((DEC_BATCH * DEC_SEQ, W_A), F32),
                   jax.ShapeDtypeStruct((DEC_BATCH * DEC_SEQ, W_B), BF16),
                   jax.ShapeDtypeStruct((DEC_BATCH, H_B, DH_B, DH_B), F32)),
        scratch_shapes=[pltpu.VMEM((HS, SR, DH_B), F32)],
        compiler_params=_params(("arbitrary", "arbitrary")),
        name="even_sample",
    )(*pa, *pb, st_exp, s_state, conv_w, g_ret, cosf, sins, lg_tab)


CG = 2
CW = 128 + CG * (2 * DQK_C + 2 * DV_C)
CT = 1024
CN = BATCH * SEQ // CT
CPB = SEQ // CT


def _odd_heads_kernel(hp_ref, hs_ref, wq_ref, wk_ref, wv_ref, wz_ref, wg_ref, bg_ref, gm_ref,
                      yc_ref, c_ref, n_ref, m_ref, sq_ref, sk_ref, sv_ref, sz_ref,
                      wb, pt, c_scr, n_scr, m_scr, gt_scr):
    grp = pl.program_id(0)
    s = pl.program_id(1)
    L = CHUNK
    gc = slice(0, 128)
    qc = slice(128, 128 + CG * DQK_C)
    kc = slice(qc.stop, qc.stop + CG * DQK_C)
    vc = slice(kc.stop, kc.stop + CG * DV_C)
    zc = slice(vc.stop, vc.stop + CG * DV_C)
    head = lambda sl, i, w: slice(sl.start + i * w, sl.start + (i + 1) * w)

    @pl.when(s == 0)
    def _():
        wb[qc, :] = wq_ref[...].astype(BF16)
        wb[kc, :] = wk_ref[...].astype(BF16)
        wb[vc, :] = wv_ref[...].astype(BF16)
        wb[zc, :] = wz_ref[...].astype(BF16)
        wb[gc, :] = wg_ref[...].astype(BF16)
        ps = _dot_nt(hs_ref[...], wb[qc.start:CW, :])
        off = lambda sl: slice(sl.start - qc.start, sl.stop - qc.start)
        sq_ref[...] = ps[:, off(qc)]
        sk_ref[...] = ps[:, off(kc)]
        sv_ref[...] = ps[:, off(vc)]
        sz_ref[...] = ps[:, off(zc)]

    @pl.when(s > 0)
    def _():
        t = s - 1
        n = CT // L
        rows = lambda c: slice(c * L, (c + 1) * L)

        def piece(k, j):
            pr = slice(k * PCH * L, (k + 1) * PCH * L)
            pc = (slice(0, vc.start), slice(vc.start, CW))[j]
            pt[pr, pc] = _dot_nt(hp_ref[pr, :], wb[pc, :])

        row = lax.broadcasted_iota(jnp.int32, (L, L), 0)
        col = lax.broadcasted_iota(jnp.int32, (L, L), 1)
        tri = row >= col
        fresh = t % CPB == 0
        cst = {0: [jnp.where(fresh, 0.0, c_scr[i]) for i in range(CG)]}
        nst = {0: [jnp.where(fresh, 0.0, n_scr[i]) for i in range(CG)]}
        mst = {0: [jnp.where(fresh, 0.0, m_scr[i, 0:1, 0:1]) for i in range(CG)]}
        v = {}
        gates = {}

        def gate(k):
            cs = range(k * PCH, (k + 1) * PCH)
            for c in cs:
                gt_scr[c] = (pt[rows(c), gc] + bg_ref[...]).T
            pad = jnp.zeros((8 - CG * PCH, L), F32)
            ig_rows = jnp.concatenate(
                [gt_scr[c, pl.ds(grp * CG + i, 1), :] for i in range(CG) for c in cs] + [pad], axis=0)
            lf_rows = jnp.concatenate(
                [_log_sigmoid(gt_scr[c, pl.ds(grp * CG + i + H_C, 1), :]) for i in range(CG) for c in cs]
                + [pad], axis=0)
            b_rows = _dot_hi(lf_rows, jnp.where(row <= col, 1.0, 0.0))
            tall = jnp.zeros((L - 8, L), F32)
            gates[k] = dict(ig_rows=ig_rows, b_rows=b_rows,
                            b_cols=jnp.concatenate([b_rows, tall], axis=0).T,
                            ig_cols=jnp.concatenate([ig_rows, tall], axis=0).T)

        def stage_a(c):
            gk = gates[c // PCH]
            v[c] = []
            nst[c + 1] = []
            mst[c + 1] = []
            for i in range(CG):
                r = i * PCH + c % PCH
                b_r = gk["b_rows"][r:r + 1, :]
                ig_r = gk["ig_rows"][r:r + 1, :]
                b_c = gk["b_cols"][:, r:r + 1]
                ig_c = gk["ig_cols"][:, r:r + 1]
                m_prev = mst[c][i]
                log_d = jnp.where(tri, b_c - b_r + ig_r, NEG_INF)
                log_inter = b_c + m_prev
                m_t = jnp.maximum(log_inter, jnp.max(log_d, axis=-1, keepdims=True))
                m_new = m_t[L - 1:L, :]
                b_last = b_c[L - 1:L, :]
                w_end = jnp.exp(b_last - b_c + ig_c - m_new)
                cd = jnp.exp(b_last + m_prev - m_new)
                q = pt[rows(c), head(qc, i, DQK_C)] * (DQK_C ** -0.5)
                k = pt[rows(c), head(kc, i, DQK_C)]
                vv = pt[rows(c), head(vc, i, DV_C)]
                nst[c + 1].append(cd * nst[c][i] + jnp.sum(w_end * k, axis=0, keepdims=True))
                mst[c + 1].append(m_new)
                v[c].append(dict(w=jnp.exp(log_d - m_t), w_inter=jnp.exp(log_inter - m_t),
                                 floor=jnp.exp(-m_t), cd=cd, qb=q.astype(BF16), kb=k.astype(BF16),
                                 vb=vv.astype(BF16), vw=(vv * w_end).astype(BF16),
                                 qn=jnp.sum(q * nst[c][i], axis=-1, keepdims=True)))

        def stage_b(c):
            for i, d in enumerate(v[c]):
                d["sc"] = _dot_nt(d["qb"], d["kb"])
                d["upd"] = _dot_tn(d["vw"], d["kb"])
            for i, d in enumerate(v[c]):
                d["inter"] = _dot_nt(d["qb"], cst[c][i].astype(BF16))

        def stage_c(c):
            cst[c + 1] = []
            for i, d in enumerate(v[c]):
                sc = d["sc"] * d["w"]
                d["den"] = jnp.sum(sc, axis=-1, keepdims=True) + d["w_inter"] * d["qn"]
                d["sc"] = sc.astype(BF16)
                cst[c + 1].append(d["cd"] * cst[c][i] + d["upd"])

        def stage_d(c):
            for d in v[c]:
                d["num"] = _dot(d["sc"], d["vb"])

        def stage_e(c):
            for i, d in enumerate(v.pop(c)):
                num = d["num"] + d["w_inter"] * d["inter"]
                hh = num / jnp.maximum(jnp.abs(d["den"]), d["floor"])
                z = pt[rows(c), head(zc, i, DV_C)]
                ys = slice(i * DV_C, (i + 1) * DV_C)
                yc_ref[rows(c), ys] = (_head_norm(hh, gm_ref[0:1, ys]) * _silu(z)).astype(BF16)

        _chunk_pipeline(n, piece, (stage_a, stage_b, stage_c, stage_d, stage_e), gate)
        for i in range(CG):
            c_scr[i] = cst[n][i]
            n_scr[i] = nst[n][i]
            m_scr[i] = jnp.broadcast_to(mst[n][i], (8, 128))
            c_ref[0, i] = cst[n][i]
            n_ref[0, i] = nst[n][i]
            m_ref[0, i] = jnp.broadcast_to(mst[n][i], (1, 128))


def _odd_heads(hp, hs, w_t, bg, gm):
    k = hp.shape[1]
    ms = hs.shape[0]
    tile = lambda s: jnp.maximum(s - 1, 0)
    qw, vw = CG * DQK_C, CG * DV_C
    koff = H_C * DQK_C // qw
    voff = 2 * H_C * DQK_C // vw
    zoff = (2 * H_C * DQK_C + W_C) // vw
    seq = lambda g, s: (tile(s) // CPB, g, 0, 0)
    once = pl.Buffered(1)
    return pl.pallas_call(
        _odd_heads_kernel,
        grid=(H_C // CG, CN + 1),
        in_specs=[pl.BlockSpec((CT, k), lambda g, s: (tile(s), 0)),
                  pl.BlockSpec((ms, k), lambda g, s: (0, 0), pipeline_mode=once),
                  pl.BlockSpec((qw, k), lambda g, s: (g, 0), pipeline_mode=once),
                  pl.BlockSpec((qw, k), lambda g, s: (koff + g, 0), pipeline_mode=once),
                  pl.BlockSpec((vw, k), lambda g, s: (voff + g, 0), pipeline_mode=once),
                  pl.BlockSpec((vw, k), lambda g, s: (zoff + g, 0), pipeline_mode=once),
                  pl.BlockSpec((128, k), lambda g, s: (O_GATE // 128, 0), pipeline_mode=once),
                  pl.BlockSpec((1, 128), lambda g, s: (0, 0)),
                  pl.BlockSpec((1, vw), lambda g, s: (0, g))],
        out_specs=(pl.BlockSpec((CT, vw), lambda g, s: (tile(s), g)),
                   pl.BlockSpec((1, CG, DV_C, DQK_C), seq),
                   pl.BlockSpec((1, CG, 1, DQK_C), seq),
                   pl.BlockSpec((1, CG, 1, 128), seq),
                   pl.BlockSpec((ms, qw), lambda g, s: (0, g)),
                   pl.BlockSpec((ms, qw), lambda g, s: (0, g)),
                   pl.BlockSpec((ms, vw), lambda g, s: (0, g)),
                   pl.BlockSpec((ms, vw), lambda g, s: (0, g))),
        out_shape=(jax.ShapeDtypeStruct((BATCH * SEQ, W_C), BF16),
                   jax.ShapeDtypeStruct((BATCH, H_C, DV_C, DQK_C), F32),
                   jax.ShapeDtypeStruct((BATCH, H_C, 1, DQK_C), F32),
                   jax.ShapeDtypeStruct((BATCH, H_C, 1, 128), F32),
                   jax.ShapeDtypeStruct((ms, H_C * DQK_C), F32),
                   jax.ShapeDtypeStruct((ms, H_C * DQK_C), F32),
                   jax.ShapeDtypeStruct((ms, W_C), F32),
                   jax.ShapeDtypeStruct((ms, W_C), F32)),
        scratch_shapes=[pltpu.VMEM((CW, k), BF16),
                        pltpu.VMEM((CT, CW), F32),
                        pltpu.VMEM((CG, DV_C, DQK_C), F32),
                        pltpu.VMEM((CG, 1, DQK_C), F32),
                        pltpu.VMEM((CG, 8, 128), F32),
                        pltpu.VMEM((CT // CHUNK, CHUNK, CHUNK), F32)],
        compiler_params=_params(("arbitrary", "arbitrary")),
        name="odd_heads",
    )(hp, hs, w_t, w_t, w_t, w_t, w_t, bg, gm)


DT = 512


def _odd_mlp_fused_kernel(hp_ref, w0_ref, w1_ref, w2_ref, wt_ref, lng_ref, lnb_ref, ws_ref, bst_ref,
                          yd_ref, wb, pt, wsb):
    s = pl.program_id(0)
    L = CHUNK
    uc = slice(0, W_D)
    vc = slice(W_D, 2 * W_D)
    zc = slice(2 * W_D, 3 * W_D)

    @pl.when(s == 0)
    def _():
        sh = N_GATE
        wb[uc, :] = jnp.concatenate([w0_ref[sh:, :], w1_ref[0:sh, :]], axis=0).astype(BF16)
        wb[vc, :] = jnp.concatenate([w1_ref[sh:, :], w2_ref[0:sh, :]], axis=0).astype(BF16)
        wb[zc, :] = jnp.concatenate([w2_ref[sh:, :], wt_ref[...]], axis=0).astype(BF16)
        keep = (lax.broadcasted_iota(jnp.int32, (L, L), 0) >= lax.broadcasted_iota(jnp.int32, (L, L), 1))
        for g in range(G_D):
            wsb[g] = jnp.where(keep, ws_ref[g], 0.0).astype(BF16)

    @pl.when(s > 0)
    def _():
        n = DT // L
        rows = lambda c: slice(c * L, (c + 1) * L)
        grp = lambda sl, g: slice(sl.start + g * 128, sl.start + (g + 1) * 128)
        vn = {}
        mix = {}

        def project(pc):
            pt[:, pc] = _dot_nt(hp_ref[...], wb[pc, :])

        def stage_a(c):
            dv = lambda g: pt[rows(c), grp(vc, g)]
            tot = dv(0)
            for g in range(1, G_D):
                tot = tot + dv(g)
            mu = jnp.sum(tot, axis=-1, keepdims=True) * (1.0 / W_D)
            sq = (dv(0) - mu) * (dv(0) - mu)
            for g in range(1, G_D):
                sq = sq + (dv(g) - mu) * (dv(g) - mu)
            rstd = lax.rsqrt(jnp.sum(sq, axis=-1, keepdims=True) * (1.0 / W_D) + EPS)
            vn[c] = [((dv(g) - mu) * rstd * lng_ref[0:1, g * 128:(g + 1) * 128]
                      + lnb_ref[0:1, g * 128:(g + 1) * 128]).astype(BF16) for g in range(G_D)]

        def stage_b(c):
            mix[c] = [_dot(wsb[g], vn[c][g]) for g in range(G_D)]

        def stage_e(c):
            for g in range(G_D):
                sg = mix[c][g] + bst_ref[:, g:g + 1]
                d_u = pt[rows(c), grp(uc, g)]
                d_z = pt[rows(c), grp(zc, g)]
                yd_ref[rows(c), g * 128:(g + 1) * 128] = (d_u * sg * _silu(d_z)).astype(BF16)

        project(vc)
        for c in range(n):
            stage_a(c)
        project(uc)
        for c in range(n):
            stage_b(c)
        project(zc)
        for c in range(n):
            stage_e(c)


def _odd_mlp_fused(hp, w_t, lng, lnb, ws, bst):
    k = hp.shape[1]
    m = hp.shape[0]
    tile = lambda s: (jnp.maximum(s - 1, 0), 0)
    const2 = lambda s: (0, 0)
    t0 = O_GATE // IN_TN
    once = pl.Buffered(1)
    wspec = lambda j: pl.BlockSpec((IN_TN, k), lambda s: (t0 + j, 0), pipeline_mode=once)
    return pl.pallas_call(
        _odd_mlp_fused_kernel,
        grid=(m // DT + 1,),
        in_specs=[pl.BlockSpec((DT, k), tile),
                  wspec(0), wspec(1), wspec(2),
                  pl.BlockSpec((N_GATE, k), lambda s: ((t0 + 3) * (IN_TN // N_GATE), 0), pipeline_mode=once),
                  pl.BlockSpec((1, W_D), const2),
                  pl.BlockSpec((1, W_D), const2),
                  pl.BlockSpec((G_D, CHUNK, CHUNK), lambda s: (0, 0, 0)),
                  pl.BlockSpec((CHUNK, G_D), const2)],
        out_specs=pl.BlockSpec((DT, W_D), tile),
        out_shape=jax.ShapeDtypeStruct((m, W_D), BF16),
        scratch_shapes=[pltpu.VMEM((3 * W_D, k), BF16),
                        pltpu.VMEM((DT, 3 * W_D), F32),
                        pltpu.VMEM((G_D, CHUNK, CHUNK), BF16)],
        compiler_params=_params(("arbitrary",)),
        name="odd_mlp_fused",
    )(hp, w_t, w_t, w_t, w_t, lng, lnb, ws, bst)


def _in_proj_rows_kernel(h_ref, w_ref, wn_ref, o_ref, *, shift):
    wsh = jnp.concatenate([w_ref[shift:, :], wn_ref[...]], axis=0)
    o_ref[...] = _dot_nt(h_ref[...], wsh.astype(BF16))


def _in_proj_rows(h, w_t, n_out, shift, tile0):
    ms, k = h.shape
    return pl.pallas_call(
        functools.partial(_in_proj_rows_kernel, shift=shift),
        grid=(n_out // IN_TN,),
        in_specs=[pl.BlockSpec((ms, k), lambda j: (0, 0)),
                  pl.BlockSpec((IN_TN, k), lambda j: (j + tile0, 0)),
                  pl.BlockSpec((shift, k), lambda j: ((j + tile0 + 1) * (IN_TN // shift), 0))],
        out_specs=pl.BlockSpec((ms, IN_TN), lambda j: (0, j)),
        out_shape=jax.ShapeDtypeStruct((ms, n_out), F32),
        compiler_params=_params(("arbitrary",)),
        name="in_proj_rows",
    )(h, w_t, w_t)


def _odd_sample_kernel(pq_ref, pk_ref, pv_ref, pz_ref, h_ref, wg_ref, pd_ref,
                       c_ref, nrow_ref, mrow_ref, bg_ref, gm_ref, lng_ref, lnb_ref,
                       wt_ref, bt_ref,
                       yc_ref, yd_ref, vn_ref, co_ref, no_ref, mo_ref,
                       inter_scr, wmix_scr):
    h = pl.program_id(1)
    row = lax.broadcasted_iota(jnp.int32, (SR, SR), 0)
    col = lax.broadcasted_iota(jnp.int32, (SR, SR), 1)
    trow = row & 3

    @pl.when(jnp.logical_and(pl.program_id(0) == 0, h == 0))
    def _():
        mask_d = jnp.where((row >> 2) == (col >> 2), trow - (col & 3), -1) >= 0
        rep = jnp.where(trow == col, 1.0, 0.0).astype(BF16)
        for g in range(G_D):
            wtile = _dot_nt(_dot(rep, wt_ref[g].astype(BF16)).astype(BF16), rep)
            wmix_scr[g] = jnp.where(mask_d, wtile, 0.0).astype(BF16)

    @pl.when(h == 0)
    def _():
        dv = pd_ref[:, W_D:2 * W_D]
        mu = jnp.mean(dv, axis=-1, keepdims=True)
        xc = dv - mu
        var = jnp.mean(xc * xc, axis=-1, keepdims=True)
        rstd = lax.rsqrt(var + EPS)
        for g in range(G_D):
            sl = slice(g * 128, (g + 1) * 128)
            vn = xc[:, sl] * rstd * lng_ref[0:1, sl] + lnb_ref[0:1, sl]
            vn_ref[:, sl] = vn
            s = _dot(wmix_scr[g], vn.astype(BF16)) + bt_ref[:, g:g + 1]
            d_u = pd_ref[:, g * 128:(g + 1) * 128]
            d_z = pd_ref[:, 2 * W_D + g * 128:2 * W_D + (g + 1) * 128]
            yd_ref[:, sl] = (d_u * s * _silu(d_z)).astype(BF16)

    same = (row >> 2) == (col >> 2)
    mask = jnp.where(same, trow - (col & 3), -1) >= 0
    pre = _dot_nt(h_ref[...], wg_ref[...].astype(BF16)) + bg_ref[...]
    lf = _log_sigmoid(pre)
    b_full = _dot_hi(jnp.where(mask, 1.0, 0.0), lf)
    sel_i = col == h
    sel_f = col == h + H_C
    ig_c = jnp.sum(jnp.where(sel_i, pre, 0.0), axis=-1, keepdims=True)
    b_c = jnp.sum(jnp.where(sel_f, b_full, 0.0), axis=-1, keepdims=True)
    sel_ir = row == h
    sel_fr = row == h + H_C
    ig_r = jnp.sum(jnp.where(sel_ir, pre.T, 0.0), axis=0, keepdims=True)
    b_r = jnp.sum(jnp.where(sel_fr, b_full.T, 0.0), axis=0, keepdims=True)
    m_prev = mrow_ref[0]
    log_d = jnp.where(mask, b_c - b_r + ig_r, NEG_INF)
    log_inter = b_c + m_prev
    m_t = jnp.maximum(log_inter, jnp.max(log_d, axis=-1, keepdims=True))
    w = jnp.exp(log_d - m_t)
    w_inter = jnp.exp(log_inter - m_t)
    q = pq_ref[...] * (DQK_C ** -0.5)
    k = pk_ref[...]
    v = pv_ref[...]
    qb = q.astype(BF16)
    kb = k.astype(BF16)
    vb = v.astype(BF16)
    sc = _dot_nt(qb, kb) * w
    sub8 = lax.broadcasted_iota(jnp.int32, (8, DQK_C), 0)
    for g in range(SR // 8):
        q8 = q[8 * g:8 * g + 8, :]
        q2 = jnp.concatenate([jnp.where(sub8 < DEC_SEQ, q8, 0.0), jnp.where(sub8 < DEC_SEQ, 0.0, q8)], axis=1)
        c_pair = jnp.concatenate([c_ref[2 * g + beta, 0].astype(BF16) for beta in range(2)], axis=1)
        inter_scr[8 * g:8 * g + 8, :] = _dot_nt(q2.astype(BF16), c_pair)
    n_rows = nrow_ref[0]
    num = _dot(sc.astype(BF16), vb) + w_inter * inter_scr[...]
    den = jnp.sum(sc, axis=-1, keepdims=True) + w_inter * jnp.sum(q * n_rows, axis=-1, keepdims=True)
    hh = num / jnp.maximum(jnp.abs(den), jnp.exp(-m_t))
    yc_ref[...] = (_head_norm(hh, gm_ref[...]) * _silu(pz_ref[...])).astype(BF16)

    stats = jnp.where(col == 0, m_t, jnp.where(col == 1, b_c, 0.0))
    last = _dot_hi(jnp.where(col == (row | 3), 1.0, 0.0), stats)
    m_new = last[:, 0:1]
    b_last = last[:, 1:2]
    w_end = jnp.exp(b_last - b_c + ig_c - m_new)
    cd = jnp.exp(b_last + m_prev - m_new)
    mo_ref[0] = m_new
    no_ref[0] = cd * n_rows + _dot_hi(jnp.where(same, 1.0, 0.0), w_end * k)
    vwt = (v * w_end).T
    lane_b = lax.broadcasted_iota(jnp.int32, (DV_C, SR), 1) >> 2
    for b in range(SB):
        lhs = jnp.where(lane_b == b, vwt, 0.0).astype(BF16)
        cd_b = cd[4 * b + 3:4 * b + 4, :]
        co_ref[b, 0] = cd_b * c_ref[b, 0] + _dot(lhs, kb)


def _odd_sample(pc, pd, h, w_o, c_state, n_rows, m_rows, bg, gm, lng, lnb, wt, bt):
    nb = DEC_BATCH // SB
    const2 = lambda i, h: (0, 0)
    return pl.pallas_call(
        _odd_sample_kernel,
        grid=(nb, H_C),
        in_specs=[pl.BlockSpec((SR, DQK_C), lambda i, h: (i, h)),
                  pl.BlockSpec((SR, DQK_C), lambda i, h: (i, h)),
                  pl.BlockSpec((SR, DV_C), lambda i, h: (i, h)),
                  pl.BlockSpec((SR, DV_C), lambda i, h: (i, h)),
                  pl.BlockSpec((SR, D_MODEL), lambda i, h: (i, 0)),
                  pl.BlockSpec((128, D_MODEL), lambda i, h: (O_GATE // 128, 0)),
                  pl.BlockSpec((SR, 3 * W_D), lambda i, h: (i, 0)),
                  pl.BlockSpec((SB, 1, DV_C, DQK_C), lambda i, h: (i, h, 0, 0)),
                  pl.BlockSpec((1, SR, DQK_C), lambda i, h: (h, i, 0)),
                  pl.BlockSpec((1, SR, 1), lambda i, h: (h, i, 0)),
                  pl.BlockSpec((1, 128), const2),
                  pl.BlockSpec((1, DV_C), lambda i, h: (0, h)),
                  pl.BlockSpec((1, W_D), const2),
                  pl.BlockSpec((1, W_D), const2),
                  pl.BlockSpec((G_D, SR, SR), lambda i, h: (0, 0, 0)),
                  pl.BlockSpec((SR, G_D), const2)],
        out_specs=(pl.BlockSpec((SR, DV_C), lambda i, h: (i, h)),
                   pl.BlockSpec((SR, W_D), lambda i, h: (i, 0)),
                   pl.BlockSpec((SR, W_D), lambda i, h: (i, 0)),
                   pl.BlockSpec((SB, 1, DV_C, DQK_C), lambda i, h: (i, h, 0, 0)),
                   pl.BlockSpec((1, SR, DQK_C), lambda i, h: (h, i, 0)),
                   pl.BlockSpec((1, SR, 1), lambda i, h: (h, i, 0))),
        out_shape=(jax.ShapeDtypeStruct((DEC_BATCH * DEC_SEQ, W_C), BF16),
                   jax.ShapeDtypeStruct((DEC_BATCH * DEC_SEQ, W_D), BF16),
                   jax.ShapeDtypeStruct((DEC_BATCH * DEC_SEQ, W_D), F32),
                   jax.ShapeDtypeStruct((DEC_BATCH, H_C, DV_C, DQK_C), F32),
                   jax.ShapeDtypeStruct((H_C, DEC_BATCH * DEC_SEQ, DQK_C), F32),
                   jax.ShapeDtypeStruct((H_C, DEC_BATCH * DEC_SEQ, 1), F32)),
        scratch_shapes=[pltpu.VMEM((SR, DV_C), F32),
                        pltpu.VMEM((G_D, SR, SR), BF16)],
        compiler_params=_params(("arbitrary", "arbitrary")),
        name="odd_sample",
    )(*pc, h, w_o, pd, c_state, n_rows, m_rows, bg, gm, lng, lnb, wt, bt)


def _rope_tables(pos):
    inv = ROPE_BASE ** (-jnp.arange(0, DH_B, 2, dtype=F32) / DH_B)
    ang = pos.astype(F32)[:, None] * inv[None, :]
    cos = jnp.cos(ang)
    sin = jnp.sin(ang)
    return jnp.concatenate([cos, cos], axis=-1), jnp.concatenate([-sin, sin], axis=-1)


def kernel(x_prompt, x_sample, state_conv, state_ret, state_mlstm_C, state_mlstm_n, state_mlstm_m,
           norm_even, w_in_even, conv_w, ret_norm, w_out_even,
           norm_odd, w_in_odd, b_gate_odd, mlstm_norm, ln_v_g, ln_v_b,
           w_spatial, b_spatial, w_out_odd, norm_final):
    w_in_e = w_in_even[0]
    w_out_e = w_out_even[0].astype(BF16)
    w_o = w_in_odd[0].T
    w_out_o = w_out_odd[0].astype(BF16)
    g_even = norm_even[0][None, :]
    g_odd = norm_odd[0][None, :]
    g_fin = norm_final[None, :]
    cw = conv_w[0]
    g_ret = ret_norm[0][None, :]
    bg = jnp.concatenate([b_gate_odd[0], jnp.zeros((128 - 2 * H_C,), F32)])[None, :]
    gm = mlstm_norm[0][None, :]
    lng = ln_v_g[0][None, :]
    lnb = ln_v_b[0][None, :]
    ws = w_spatial[0]
    bst = b_spatial[0].T

    cos_p, sin_p = _rope_tables(jnp.arange(SEQ, dtype=jnp.int32))
    cos_s, sin_s = _rope_tables(PAST_LEN + jnp.arange(DEC_SEQ, dtype=jnp.int32))
    cos_s = jnp.tile(cos_s, (SB, 1))
    sin_s = jnp.tile(sin_s, (SB, 1))
    lg_tab = jnp.broadcast_to(jnp.asarray(LOG_GAMMA, F32)[:, None, None], (H_B, 1, 128))

    bt_s = jnp.tile(b_spatial[0][:, :DEC_SEQ].T, (SB, 1))

    rs = DEC_BATCH * DEC_SEQ
    xp = x_prompt.reshape(BATCH * SEQ, D_MODEL)
    xs = x_sample.reshape(rs, D_MODEL)
    hp, hs = _norm_cast(xp, xs, g_even, 1024)
    ya, conv_p, *ps_a = _even_conv(hp, hs, w_in_e, cw)
    yb, ret_p, *ps_b = _even_heads(hp, hs, w_in_e, g_ret, cos_p, sin_p, lg_tab)
    st_exp = jnp.pad(state_conv[0], ((0, 0), (0, DEC_SEQ - (CONV_W - 1)), (0, 0))).reshape(rs, W_A)
    ya_s, u_s, yb_s, ret_s = _even_sample(ps_a, ps_b, st_exp, state_ret[0], cw, g_ret, cos_s, sin_s, lg_tab)
    x1, h1, x1s, h1s = _outproj(ya, yb, xp, ya_s, yb_s, xs, w_out_e, g_odd, final=False)

    yc, c_p, n_p, m_p, *ps_c = _odd_heads(h1, h1s, w_o, bg, gm)
    yd = _odd_mlp_fused(h1, w_o, lng, lnb, ws, bst)
    pd_s = _in_proj_rows(h1s, w_o, 3 * W_D, N_GATE, O_GATE // IN_TN)
    n_rows = jnp.repeat(jnp.transpose(state_mlstm_n[0], (1, 0, 2)), DEC_SEQ, axis=1)
    m_rows = jnp.repeat(state_mlstm_m[0].T, DEC_SEQ, axis=1)[:, :, None]
    yc_s, yd_s, vn_s, c_s, no_s, mo_s = _odd_sample(
        ps_c, pd_s, h1s, w_o, state_mlstm_C[0], n_rows, m_rows, bg, gm, lng, lnb, ws, bt_s)
    y_prompt, y_sample = _outproj(yc, yd, x1, yc_s, yd_s, x1s, w_out_o, g_fin, final=True)

    conv_s = u_s.reshape(DEC_BATCH, DEC_SEQ, W_A)[:, DEC_SEQ - (CONV_W - 1):, :]
    n_s = jnp.transpose(no_s[:, DEC_SEQ - 1::DEC_SEQ, :], (1, 0, 2))
    m_s = mo_s[:, DEC_SEQ - 1::DEC_SEQ, 0].T
    return (y_prompt.reshape(BATCH, SEQ, D_MODEL),
            y_sample.reshape(DEC_BATCH, DEC_SEQ, D_MODEL),
            conv_p[None], conv_s[None],
            ret_p[None], ret_s[None],
            c_p[None], c_s[None],
            n_p[:, :, 0, :][None], n_s[None],
            m_p[:, :, 0, 0][None], m_s[None],
            vn_s.reshape(DEC_BATCH, DEC_SEQ, W_D)[None])
```

```python
import functools
import math

import jax
import jax.numpy as jnp
from jax import lax
from jax.experimental import pallas as pl
from jax.experimental.pallas import tpu as pltpu

F32 = jnp.float32
BF16 = jnp.bfloat16

D_MODEL = 2048
BATCH = 4
SEQ = 2048
DEC_BATCH = 128
DEC_SEQ = 4
PAST_LEN = 16384
W_A = 1024
CONV_W = 3
W_B = 1024
H_B = 8
DH_B = 128
W_C = 1024
H_C = 4
DV_C = 256
DQK_C = 128
W_D = 1024
G_D = 8
CHUNK = 128
O_GATE = 2 * H_C * DQK_C + 2 * W_C
N_GATE = 2 * H_C
ROPE_BASE = 10000.0
EPS = 1e-6
LOG_GAMMA = tuple(math.log(1.0 - 2.0 ** (-5.0 - h)) for h in range(H_B))
NEG_INF = float("-inf")
VMEM_LIMIT = 56 * 1024 * 1024

NT_DIMS = (((1,), (1,)), ((), ()))
TN_DIMS = (((0,), (0,)), ((), ()))


def _silu(z):
    return z * (1.0 / (1.0 + jnp.exp(-z)))


def _log_sigmoid(x):
    return jnp.minimum(x, 0.0) - jnp.log1p(jnp.exp(-jnp.abs(x)))


def _dot(a, b):
    return jnp.dot(a, b, preferred_element_type=F32)


def _dot_nt(a, b):
    return lax.dot_general(a, b, NT_DIMS, preferred_element_type=F32)


def _dot_tn(a, b):
    return lax.dot_general(a, b, TN_DIMS, preferred_element_type=F32)


def _dot_hi(a, b):
    return jnp.dot(a, b, preferred_element_type=F32, precision=lax.Precision.HIGHEST)


def _head_norm(o, g):
    mu = jnp.mean(o, axis=-1, keepdims=True)
    oc = o - mu
    var = jnp.mean(oc * oc, axis=-1, keepdims=True)
    return oc * lax.rsqrt(var + EPS) * g


def _params(sem):
    return pltpu.CompilerParams(dimension_semantics=sem, vmem_limit_bytes=VMEM_LIMIT)


def _norm_cast_kernel(x_ref, xs_ref, g_ref, h_ref, hs_ref, *, n_prompt):
    i = pl.program_id(0)

    def tile(x_in, h_out):
        x = x_in[...]
        ms = jnp.mean(x * x, axis=-1, keepdims=True)
        h_out[...] = (x * lax.rsqrt(ms + EPS) * g_ref[...]).astype(BF16)

    @pl.when(i < n_prompt)
    def _():
        tile(x_ref, h_ref)

    @pl.when(i == n_prompt)
    def _():
        tile(xs_ref, hs_ref)


def _norm_cast(x, xs, g, tm):
    m, d = x.shape
    ms = xs.shape[0]
    n_prompt = m // tm
    row = lambda i: (jnp.minimum(i, n_prompt - 1), 0)
    const = lambda i: (0, 0)
    return pl.pallas_call(
        functools.partial(_norm_cast_kernel, n_prompt=n_prompt),
        grid=(n_prompt + 1,),
        in_specs=[pl.BlockSpec((tm, d), row),
                  pl.BlockSpec((ms, d), const),
                  pl.BlockSpec((1, d), const)],
        out_specs=(pl.BlockSpec((tm, d), row), pl.BlockSpec((ms, d), const)),
        out_shape=(jax.ShapeDtypeStruct((m, d), BF16), jax.ShapeDtypeStruct((ms, d), BF16)),
        compiler_params=_params(("arbitrary",)),
        name="norm_cast",
    )(x, xs, g)


IN_TN = 1024


OUT_TM = 512


def _outproj_kernel(ya_ref, yb_ref, x_ref, yas_ref, ybs_ref, xs_ref, w_ref, g_ref, *out_refs,
                    final, n_prompt):
    i = pl.program_id(0)
    half = ya_ref.shape[1]
    n_out = 1 if final else 2

    def tile(ya, yb, x, outs):
        acc = _dot(ya[...], w_ref[0:half, :]) + _dot(yb[...], w_ref[half:2 * half, :])
        x1 = x[...] + acc
        ms = jnp.mean(x1 * x1, axis=-1, keepdims=True)
        hn = x1 * lax.rsqrt(ms + EPS) * g_ref[...]
        if final:
            outs[0][...] = hn
        else:
            outs[0][...] = x1
            outs[1][...] = hn.astype(BF16)

    @pl.when(i < n_prompt)
    def _():
        tile(ya_ref, yb_ref, x_ref, out_refs[:n_out])

    @pl.when(i == n_prompt)
    def _():
        tile(yas_ref, ybs_ref, xs_ref, out_refs[n_out:])


def _outproj(ya, yb, x, ya_s, yb_s, x_s, w, g, final):
    m, half = ya.shape
    ms = ya_s.shape[0]
    d = w.shape[1]
    n_prompt = m // OUT_TM
    row = lambda i: (jnp.minimum(i, n_prompt - 1), 0)
    const = lambda i: (0, 0)
    once = pl.Buffered(1)
    shapes = [jax.ShapeDtypeStruct((m, d), F32), jax.ShapeDtypeStruct((ms, d), F32)]
    specs = [pl.BlockSpec((OUT_TM, d), row), pl.BlockSpec((ms, d), const)]
    if not final:
        shapes = [shapes[0], jax.ShapeDtypeStruct((m, d), BF16), shapes[1], jax.ShapeDtypeStruct((ms, d), BF16)]
        specs = [specs[0], pl.BlockSpec((OUT_TM, d), row), specs[1], pl.BlockSpec((ms, d), const)]
    return pl.pallas_call(
        functools.partial(_outproj_kernel, final=final, n_prompt=n_prompt),
        grid=(n_prompt + 1,),
        in_specs=[pl.BlockSpec((OUT_TM, half), row),
                  pl.BlockSpec((OUT_TM, half), row),
                  pl.BlockSpec((OUT_TM, d), row),
                  pl.BlockSpec((ms, half), const, pipeline_mode=once),
                  pl.BlockSpec((ms, half), const, pipeline_mode=once),
                  pl.BlockSpec((ms, d), const, pipeline_mode=once),
                  pl.BlockSpec((2 * half, d), const, pipeline_mode=once),
                  pl.BlockSpec((1, d), const)],
        out_specs=tuple(specs),
        out_shape=tuple(shapes),
        compiler_params=_params(("arbitrary",)),
        name="out_proj_final" if final else "out_proj",
    )(ya, yb, x, ya_s, yb_s, x_s, w, g)


def _rope(x, cosf, sins):
    return x * cosf + pltpu.roll(x, DH_B // 2, 1) * sins


FT = 1024
FN = BATCH * SEQ // FT
FPB = SEQ // FT
HG = 2
GW = HG * 128
PCH = 2


def _chunk_pipeline(n, piece, stages, gate=None):
    sa, sb, sc, sd, se = stages
    npieces = n // PCH
    for j in range(PCH):
        piece(0, j)
    if gate is not None:
        gate(0)
    for c in range(n + 2):
        k, j = c // PCH + 1, c % PCH
        if k < npieces:
            piece(k, j)
        if c < n:
            sa(c)
        if 1 <= c <= n:
            sc(c - 1)
        if c < n:
            sb(c)
        if 1 <= c <= n:
            sd(c - 1)
        if c >= 2:
            se(c - 2)
        if gate is not None and j == PCH - 1 and k < npieces:
            gate(k)


def _even_heads_kernel(hp_ref, hs_ref, wq_ref, wk_ref, wv_ref, wz_ref, gret_ref, cos_ref, sin_ref, lg_ref,
                       yb_ref, s_ref, sq_ref, sk_ref, sv_ref, sz_ref, wb, pt, s_scr):
    s = pl.program_id(1)
    L = CHUNK

    @pl.when(s == 0)
    def _():
        for part, w_ref in enumerate((wq_ref, wk_ref, wv_ref, wz_ref)):
            wb[:, part * GW:(part + 1) * GW] = w_ref[...].astype(BF16)
        ps = _dot(hs_ref[...], wb[...])
        for part, o_ref in enumerate((sq_ref, sk_ref, sv_ref, sz_ref)):
            o_ref[...] = ps[:, part * GW:(part + 1) * GW]

    @pl.when(s > 0)
    def _():
        t = s - 1
        n = FT // L
        rows = lambda c: slice(c * L, (c + 1) * L)
        cols = lambda part, i: slice(part * GW + i * DH_B, part * GW + (i + 1) * DH_B)

        def piece(k, j):
            pr = slice(k * PCH * L, (k + 1) * PCH * L)
            pc = slice(j * 2 * GW, (j + 1) * 2 * GW)
            pt[pr, pc] = _dot(hp_ref[pr, :], wb[:, pc])

        row = lax.broadcasted_iota(jnp.int32, (L, L), 0)
        col = lax.broadcasted_iota(jnp.int32, (L, L), 1)
        causal = row >= col
        diff = jnp.maximum(row - col, 0).astype(F32)
        ti = lax.broadcasted_iota(jnp.int32, (L, 1), 0).astype(F32)
        lgs = [lg_ref[i][:, 0:1] for i in range(HG)]
        decay = [jnp.where(causal, jnp.exp(lg * diff), 0.0) for lg in lgs]
        q_decay = [jnp.exp(lg * (ti + 1.0)) for lg in lgs]
        k_decay = [jnp.exp(lg * (L - 1.0 - ti)) for lg in lgs]
        gamma_l = [jnp.exp(lg * float(L)) for lg in lgs]
        state = {0: [jnp.where(t % FPB == 0, 0.0, s_scr[i]) for i in range(HG)]}
        v = {}

        def stage_a(c):
            cosf = cos_ref[rows(c), :]
            sins = sin_ref[rows(c), :]
            v[c] = []
            for i in range(HG):
                kr = _rope(pt[rows(c), cols(1, i)], cosf, sins) * (DH_B ** -0.5)
                v[c].append(dict(qb=_rope(pt[rows(c), cols(0, i)], cosf, sins).astype(BF16),
                                 kb=kr.astype(BF16),
                                 kd=(kr * k_decay[i]).astype(BF16),
                                 vb=pt[rows(c), cols(2, i)].astype(BF16)))

        def stage_b(c):
            for i, d in enumerate(v[c]):
                d["sc"] = _dot_nt(d["qb"], d["kb"])
                d["upd"] = _dot_tn(d["kd"], d["vb"])
            for i, d in enumerate(v[c]):
                d["cross"] = _dot(d["qb"], state[c][i].astype(BF16))

        def stage_c(c):
            state[c + 1] = []
            for i, d in enumerate(v[c]):
                d["sc"] = (d["sc"] * decay[i]).astype(BF16)
                state[c + 1].append(gamma_l[i] * state[c][i] + d["upd"])

        def stage_d(c):
            for d in v[c]:
                d["inner"] = _dot(d["sc"], d["vb"])

        def stage_e(c):
            for i, d in enumerate(v.pop(c)):
                o = d["inner"] + d["cross"] * q_decay[i]
                g = gret_ref[0:1, i * DH_B:(i + 1) * DH_B]
                z = pt[rows(c), cols(3, i)]
                yb_ref[rows(c), i * DH_B:(i + 1) * DH_B] = (_head_norm(o, g) * _silu(z)).astype(BF16)

        _chunk_pipeline(n, piece, (stage_a, stage_b, stage_c, stage_d, stage_e))
        for i in range(HG):
            s_scr[i] = state[n][i]
            s_ref[0, i] = state[n][i]


def _even_heads(hp, hs, w, g_ret, cosf, sins, lg_tab):
    k = hp.shape[1]
    ms = hs.shape[0]
    ng = H_B // HG
    base = 4 * W_A // GW
    tile = lambda s: jnp.maximum(s - 1, 0)
    wspec = lambda part: pl.BlockSpec((k, GW), lambda g, s: (0, base + part * ng + g))
    sspec = pl.BlockSpec((ms, GW), lambda g, s: (0, g))
    sshape = jax.ShapeDtypeStruct((ms, W_B), F32)
    return pl.pallas_call(
        _even_heads_kernel,
        grid=(ng, FN + 1),
        in_specs=[pl.BlockSpec((FT, k), lambda g, s: (tile(s), 0)),
                  pl.BlockSpec((ms, k), lambda g, s: (0, 0)),
                  wspec(0), wspec(1), wspec(2), wspec(3),
                  pl.BlockSpec((1, GW), lambda g, s: (0, g)),
                  pl.BlockSpec((FT, DH_B), lambda g, s: (tile(s) % FPB, 0)),
                  pl.BlockSpec((FT, DH_B), lambda g, s: (tile(s) % FPB, 0)),
                  pl.BlockSpec((HG, 1, 128), lambda g, s: (g, 0, 0))],
        out_specs=(pl.BlockSpec((FT, GW), lambda g, s: (tile(s), g)),
                   pl.BlockSpec((1, HG, DH_B, DH_B), lambda g, s: (tile(s) // FPB, g, 0, 0)),
                   sspec, sspec, sspec, sspec),
        out_shape=(jax.ShapeDtypeStruct((BATCH * SEQ, W_B), BF16),
                   jax.ShapeDtypeStruct((BATCH, H_B, DH_B, DH_B), F32),
                   sshape, sshape, sshape, sshape),
        scratch_shapes=[pltpu.VMEM((k, 4 * GW), BF16),
                        pltpu.VMEM((FT, 4 * GW), F32),
                        pltpu.VMEM((HG, DH_B, DH_B), F32)],
        compiler_params=_params(("arbitrary", "arbitrary")),
        name="even_heads",
    )(hp, hs, w, w, w, w, g_ret, cosf, sins, lg_tab)


def _even_conv_kernel(hp_ref, hs_ref, wb_ref, wc_ref, wx_ref, wz_ref, cw_ref,
                      ya_ref, conv_ref, sb_ref, sc_ref, sx_ref, sz_ref, wb, pt, ubuf):
    s = pl.program_id(1)
    L = CHUNK
    sub = lambda j: slice(j * 4 * 128, (j + 1) * 4 * 128)
    cs = lambda j, p: slice(j * 4 * 128 + p * 128, j * 4 * 128 + (p + 1) * 128)

    @pl.when(s == 0)
    def _():
        for p, w_ref in enumerate((wb_ref, wc_ref, wx_ref, wz_ref)):
            for j in range(HG):
                wb[:, cs(j, p)] = w_ref[:, j * 128:(j + 1) * 128].astype(BF16)
        ps = _dot(hs_ref[...], wb[...])
        for p, o_ref in enumerate((sb_ref, sc_ref, sx_ref, sz_ref)):
            for j in range(HG):
                o_ref[:, j * 128:(j + 1) * 128] = ps[:, cs(j, p)]

    @pl.when(s > 0)
    def _():
        t = s - 1
        n = FT // L
        rows = lambda c: slice(c * L, (c + 1) * L)

        @pl.when(t % FPB == 0)
        def _():
            ubuf[0:8, :] = jnp.zeros((8, GW), F32)

        @pl.when(t % FPB != 0)
        def _():
            ubuf[0:8, :] = ubuf[FT:FT + 8, :]

        def project(j):
            pt[:, sub(j)] = _dot(hp_ref[...], wb[:, sub(j)])

        def mix(j):
            ch = slice(j * 128, (j + 1) * 128)
            for c in range(n):
                u = pt[rows(c), cs(j, 1)] * pt[rows(c), cs(j, 2)]
                ubuf[8 + c * L:8 + (c + 1) * L, ch] = u
                t0 = ubuf[6 + c * L:6 + (c + 1) * L, ch]
                t1 = ubuf[7 + c * L:7 + (c + 1) * L, ch]
                conv = cw_ref[0:1, ch] * t0 + cw_ref[1:2, ch] * t1 + cw_ref[2:3, ch] * u
                ya_ref[rows(c), ch] = (pt[rows(c), cs(j, 0)] * conv * _silu(pt[rows(c), cs(j, 3)])).astype(BF16)

        project(0)
        for j in range(HG):
            if j + 1 < HG:
                project(j + 1)
            mix(j)
        conv_ref[0] = ubuf[FT + 6:FT + 8, :]


def _even_conv(hp, hs, w, conv_w):
    k = hp.shape[1]
    ms = hs.shape[0]
    ng = W_A // GW
    tile = lambda s: jnp.maximum(s - 1, 0)
    wspec = lambda part: pl.BlockSpec((k, GW), lambda g, s: (0, part * ng + g))
    sspec = pl.BlockSpec((ms, GW), lambda g, s: (0, g))
    sshape = jax.ShapeDtypeStruct((ms, W_A), F32)
    return pl.pallas_call(
        _even_conv_kernel,
        grid=(ng, FN + 1),
        in_specs=[pl.BlockSpec((FT, k), lambda g, s: (tile(s), 0)),
                  pl.BlockSpec((ms, k), lambda g, s: (0, 0)),
                  wspec(0), wspec(1), wspec(2), wspec(3),
                  pl.BlockSpec((CONV_W, GW), lambda g, s: (0, g))],
        out_specs=(pl.BlockSpec((FT, GW), lambda g, s: (tile(s), g)),
                   pl.BlockSpec((1, CONV_W - 1, GW), lambda g, s: (tile(s) // FPB, 0, g)),
                   sspec, sspec, sspec, sspec),
        out_shape=(jax.ShapeDtypeStruct((BATCH * SEQ, W_A), BF16),
                   jax.ShapeDtypeStruct((BATCH, CONV_W - 1, W_A), F32),
                   sshape, sshape, sshape, sshape),
        scratch_shapes=[pltpu.VMEM((k, 4 * GW), BF16),
                        pltpu.VMEM((FT, 4 * GW), F32),
                        pltpu.VMEM((FT + 8, GW), F32)],
        compiler_params=_params(("arbitrary", "arbitrary")),
        name="even_conv",
    )(hp, hs, w, w, w, w, conv_w)


SB = 32
SR = SB * DEC_SEQ
HS = 4
CS = 2


def _even_sample_kernel(ab_ref, ac_ref, ax_ref, az_ref, pq_ref, pk_ref, pv_ref, pz_ref, st_ref, s_ref,
                        cw_ref, gret_ref, cos_ref, sin_ref, lg_ref,
                        ya_ref, u_ref, yb_ref, so_ref, cross_scr):
    h = pl.program_id(1)
    row = lax.broadcasted_iota(jnp.int32, (SR, SR), 0)
    col = lax.broadcasted_iota(jnp.int32, (SR, SR), 1)
    trow = row & 3

    @pl.when(h == 0)
    def _():
        for j in range(W_A // 128):
            sl = slice(j * 128, (j + 1) * 128)
            a_b = ab_ref[:, sl]
            a_c = ac_ref[:, sl]
            a_x = ax_ref[:, sl]
            a_z = az_ref[:, sl]
            u = a_c * a_x
            e = st_ref[:, sl]
            tap1 = jnp.where(trow >= 1, pltpu.roll(u, 1, 0), pltpu.roll(e, SR - 1, 0))
            tap0 = jnp.where(trow >= 2, pltpu.roll(u, 2, 0), e)
            conv = cw_ref[0:1, sl] * tap0 + cw_ref[1:2, sl] * tap1 + cw_ref[2:3, sl] * u
            ya_ref[:, sl] = (a_b * conv * _silu(a_z)).astype(BF16)
            u_ref[:, sl] = u

    same = (row >> 2) == (col >> 2)
    dd = trow - (col & 3)
    mask = jnp.where(same, dd, -1) >= 0
    tcol = (lax.broadcasted_iota(jnp.int32, (SR, 1), 0) & 3).astype(F32)
    cosf = cos_ref[...]
    sins = sin_ref[...]
    lane_b = col >> 2
    sub = lax.broadcasted_iota(jnp.int32, (8, DH_B), 0)
    for hh in range(HS):
        hc = slice(hh * DH_B, (hh + 1) * DH_B)
        lg = lg_ref[hh][:, 0:1]
        decay = jnp.where(mask, jnp.exp(lg * jnp.maximum(dd, 0).astype(F32)), 0.0)
        qr = _rope(pq_ref[:, hc], cosf, sins)
        kr = _rope(pk_ref[:, hc], cosf, sins) * (DH_B ** -0.5)
        qb = qr.astype(BF16)
        kb = kr.astype(BF16)
        vb = pv_ref[:, hc].astype(BF16)
        sc = _dot_nt(qb, kb) * decay
        inner = _dot(sc.astype(BF16), vb)
        kdt = (kr * jnp.exp(lg * (DEC_SEQ - 1.0 - tcol))).T
        gamma_l = jnp.exp(lg * float(DEC_SEQ))
        for g in range(SR // 8):
            q8 = qr[8 * g:8 * g + 8, :]
            q2 = jnp.concatenate([jnp.where(sub < DEC_SEQ, q8, 0.0), jnp.where(sub < DEC_SEQ, 0.0, q8)], axis=1)
            s_pair = [s_ref[2 * g + beta, hh] for beta in range(2)]
            cross_scr[hh, 8 * g:8 * g + 8, :] = _dot(
                q2.astype(BF16), jnp.concatenate([sp.astype(BF16) for sp in s_pair], axis=0))
            for beta in range(2):
                b = 2 * g + beta
                lhs = jnp.where(lane_b == b, kdt, 0.0).astype(BF16)
                so_ref[b, hh] = gamma_l * s_pair[beta] + _dot(lhs, vb)
        o = inner + cross_scr[hh] * jnp.exp(lg * (tcol + 1.0))
        yb_ref[:, hc] = (_head_norm(o, gret_ref[:, hc]) * _silu(pz_ref[:, hc])).astype(BF16)


def _even_sample(pa, pb, st_exp, s_state, conv_w, g_ret, cosf, sins, lg_tab):
    nb = DEC_BATCH // SB
    const2 = lambda i, h: (0, 0)
    aspec = pl.BlockSpec((SR, W_A), lambda i, h: (i, 0))
    hspec = pl.BlockSpec((SR, HS * DH_B), lambda i, h: (i, h))
    return pl.pallas_call(
        _even_sample_kernel,
        grid=(nb, H_B // HS),
        in_specs=[aspec, aspec, aspec, aspec,
                  hspec, hspec, hspec, hspec,
                  pl.BlockSpec((SR, W_A), lambda i, h: (i, 0)),
                  pl.BlockSpec((SB, HS, DH_B, DH_B), lambda i, h: (i, h, 0, 0)),
                  pl.BlockSpec((CONV_W, W_A), const2),
                  pl.BlockSpec((1, HS * DH_B), lambda i, h: (0, h)),
                  pl.BlockSpec((SR, DH_B), const2),
                  pl.BlockSpec((SR, DH_B), const2),
                  pl.BlockSpec((HS, 1, 128), lambda i, h: (h, 0, 0))],
        out_specs=(pl.BlockSpec((SR, W_A), lambda i, h: (i, 0)),
                   pl.BlockSpec((SR, W_A), lambda i, h: (i, 0)),
                   pl.BlockSpec((SR, HS * DH_B), lambda i, h: (i, h)),
                   pl.BlockSpec((SB, HS, DH_B, DH_B), lambda i, h: (i, h, 0, 0))),
        out_shape=(jax.ShapeDtypeStruct((DEC_BATCH * DEC_SEQ, W_A), BF16),
                   jax.ShapeDtypeStruct((DEC_BATCH * DEC_SEQ, W_A), F32),
                   jax.ShapeDtypeStruct((DEC_BATCH * DEC_SEQ, W_B), BF16),
                   jax.ShapeDtypeStruct((DEC_BATCH, H_B, DH_B, DH_B), F32)),
        scratch_shapes=[pltpu.VMEM((HS, SR, DH_B), F32)],
        compiler_params=_params(("arbitrary", "arbitrary")),
        name="even_sample",
    )(*pa, *pb, st_exp, s_state, conv_w, g_ret, cosf, sins, lg_tab)


CG = 2
CW = 128 + CG * (2 * DQK_C + 2 * DV_C)
CT = 1024
CN = BATCH * SEQ // CT
CPB = SEQ // CT


def _odd_heads_kernel(hp_ref, hs_ref, wq_ref, wk_ref, wv_ref, wz_ref, wg_ref, bg_ref, gm_ref,
                      yc_ref, c_ref, n_ref, m_ref, sq_ref, sk_ref, sv_ref, sz_ref,
                      wb, pt, c_scr, n_scr, m_scr, gt_scr):
    grp = pl.program_id(0)
    s = pl.program_id(1)
    L = CHUNK
    gc = slice(0, 128)
    qc = slice(128, 128 + CG * DQK_C)
    kc = slice(qc.stop, qc.stop + CG * DQK_C)
    vc = slice(kc.stop, kc.stop + CG * DV_C)
    zc = slice(vc.stop, vc.stop + CG * DV_C)
    head = lambda sl, i, w: slice(sl.start + i * w, sl.start + (i + 1) * w)

    @pl.when(s == 0)
    def _():
        wb[qc, :] = wq_ref[...].astype(BF16)
        wb[kc, :] = wk_ref[...].astype(BF16)
        wb[vc, :] = wv_ref[...].astype(BF16)
        wb[zc, :] = wz_ref[...].astype(BF16)
        wb[gc, :] = wg_ref[...].astype(BF16)
        ps = _dot_nt(hs_ref[...], wb[qc.start:CW, :])
        off = lambda sl: slice(sl.start - qc.start, sl.stop - qc.start)
        sq_ref[...] = ps[:, off(qc)]
        sk_ref[...] = ps[:, off(kc)]
        sv_ref[...] = ps[:, off(vc)]
        sz_ref[...] = ps[:, off(zc)]

    @pl.when(s > 0)
    def _():
        t = s - 1
        n = CT // L
        rows = lambda c: slice(c * L, (c + 1) * L)

        def piece(k, j):
            pr = slice(k * PCH * L, (k + 1) * PCH * L)
            pc = (slice(0, vc.start), slice(vc.start, CW))[j]
            pt[pr, pc] = _dot_nt(hp_ref[pr, :], wb[pc, :])

        row = lax.broadcasted_iota(jnp.int32, (L, L), 0)
        col = lax.broadcasted_iota(jnp.int32, (L, L), 1)
        tri = row >= col
        fresh = t % CPB == 0
        cst = {0: [jnp.where(fresh, 0.0, c_scr[i]) for i in range(CG)]}
        nst = {0: [jnp.where(fresh, 0.0, n_scr[i]) for i in range(CG)]}
        mst = {0: [jnp.where(fresh, 0.0, m_scr[i, 0:1, 0:1]) for i in range(CG)]}
        v = {}
        gates = {}

        def gate(k):
            cs = range(k * PCH, (k + 1) * PCH)
            for c in cs:
                gt_scr[c] = (pt[rows(c), gc] + bg_ref[...]).T
            pad = jnp.zeros((8 - CG * PCH, L), F32)
            ig_rows = jnp.concatenate(
                [gt_scr[c, pl.ds(grp * CG + i, 1), :] for i in range(CG) for c in cs] + [pad], axis=0)
            lf_rows = jnp.concatenate(
                [_log_sigmoid(gt_scr[c, pl.ds(grp * CG + i + H_C, 1), :]) for i in range(CG) for c in cs]
                + [pad], axis=0)
            b_rows = _dot_hi(lf_rows, jnp.where(row <= col, 1.0, 0.0))
            tall = jnp.zeros((L - 8, L), F32)
            gates[k] = dict(ig_rows=ig_rows, b_rows=b_rows,
                            b_cols=jnp.concatenate([b_rows, tall], axis=0).T,
                            ig_cols=jnp.concatenate([ig_rows, tall], axis=0).T)

        def stage_a(c):
            gk = gates[c // PCH]
            v[c] = []
            nst[c + 1] = []
            mst[c + 1] = []
            for i in range(CG):
                r = i * PCH + c % PCH
                b_r = gk["b_rows"][r:r + 1, :]
                ig_r = gk["ig_rows"][r:r + 1, :]
                b_c = gk["b_cols"][:, r:r + 1]
                ig_c = gk["ig_cols"][:, r:r + 1]
                m_prev = mst[c][i]
                log_d = jnp.where(tri, b_c - b_r + ig_r, NEG_INF)
                log_inter = b_c + m_prev
                m_t = jnp.maximum(log_inter, jnp.max(log_d, axis=-1, keepdims=True))
                m_new = m_t[L - 1:L, :]
                b_last = b_c[L - 1:L, :]
                w_end = jnp.exp(b_last - b_c + ig_c - m_new)
                cd = jnp.exp(b_last + m_prev - m_new)
                q = pt[rows(c), head(qc, i, DQK_C)] * (DQK_C ** -0.5)
                k = pt[rows(c), head(kc, i, DQK_C)]
                vv = pt[rows(c), head(vc, i, DV_C)]
                nst[c + 1].append(cd * nst[c][i] + jnp.sum(w_end * k, axis=0, keepdims=True))
                mst[c + 1].append(m_new)
                v[c].append(dict(w=jnp.exp(log_d - m_t), w_inter=jnp.exp(log_inter - m_t),
                                 floor=jnp.exp(-m_t), cd=cd, qb=q.astype(BF16), kb=k.astype(BF16),
                                 vb=vv.astype(BF16), vw=(vv * w_end).astype(BF16),
                                 qn=jnp.sum(q * nst[c][i], axis=-1, keepdims=True)))

        def stage_b(c):
            for i, d in enumerate(v[c]):
                d["sc"] = _dot_nt(d["qb"], d["kb"])
                d["upd"] = _dot_tn(d["vw"], d["kb"])
            for i, d in enumerate(v[c]):
                d["inter"] = _dot_nt(d["qb"], cst[c][i].astype(BF16))

        def stage_c(c):
            cst[c + 1] = []
            for i, d in enumerate(v[c]):
                sc = d["sc"] * d["w"]
                d["den"] = jnp.sum(sc, axis=-1, keepdims=True) + d["w_inter"] * d["qn"]
                d["sc"] = sc.astype(BF16)
                cst[c + 1].append(d["cd"] * cst[c][i] + d["upd"])

        def stage_d(c):
            for d in v[c]:
                d["num"] = _dot(d["sc"], d["vb"])

        def stage_e(c):
            for i, d in enumerate(v.pop(c)):
                num = d["num"] + d["w_inter"] * d["inter"]
                hh = num / jnp.maximum(jnp.abs(d["den"]), d["floor"])
                z = pt[rows(c), head(zc, i, DV_C)]
                ys = slice(i * DV_C, (i + 1) * DV_C)
                yc_ref[rows(c), ys] = (_head_norm(hh, gm_ref[0:1, ys]) * _silu(z)).astype(BF16)

        _chunk_pipeline(n, piece, (stage_a, stage_b, stage_c, stage_d, stage_e), gate)
        for i in range(CG):
            c_scr[i] = cst[n][i]
            n_scr[i] = nst[n][i]
            m_scr[i] = jnp.broadcast_to(mst[n][i], (8, 128))
            c_ref[0, i] = cst[n][i]
            n_ref[0, i] = nst[n][i]
            m_ref[0, i] = jnp.broadcast_to(mst[n][i], (1, 128))


def _odd_heads(hp, hs, w_t, bg, gm):
    k = hp.shape[1]
    ms = hs.shape[0]
    tile = lambda s: jnp.maximum(s - 1, 0)
    qw, vw = CG * DQK_C, CG * DV_C
    koff = H_C * DQK_C // qw
    voff = 2 * H_C * DQK_C // vw
    zoff = (2 * H_C * DQK_C + W_C) // vw
    seq = lambda g, s: (tile(s) // CPB, g, 0, 0)
    once = pl.Buffered(1)
    return pl.pallas_call(
        _odd_heads_kernel,
        grid=(H_C // CG, CN + 1),
        in_specs=[pl.BlockSpec((CT, k), lambda g, s: (tile(s), 0)),
                  pl.BlockSpec((ms, k), lambda g, s: (0, 0), pipeline_mode=once),
                  pl.BlockSpec((qw, k), lambda g, s: (g, 0), pipeline_mode=once),
                  pl.BlockSpec((qw, k), lambda g, s: (koff + g, 0), pipeline_mode=once),
                  pl.BlockSpec((vw, k), lambda g, s: (voff + g, 0), pipeline_mode=once),
                  pl.BlockSpec((vw, k), lambda g, s: (zoff + g, 0), pipeline_mode=once),
                  pl.BlockSpec((128, k), lambda g, s: (O_GATE // 128, 0), pipeline_mode=once),
                  pl.BlockSpec((1, 128), lambda g, s: (0, 0)),
                  pl.BlockSpec((1, vw), lambda g, s: (0, g))],
        out_specs=(pl.BlockSpec((CT, vw), lambda g, s: (tile(s), g)),
                   pl.BlockSpec((1, CG, DV_C, DQK_C), seq),
                   pl.BlockSpec((1, CG, 1, DQK_C), seq),
                   pl.BlockSpec((1, CG, 1, 128), seq),
                   pl.BlockSpec((ms, qw), lambda g, s: (0, g)),
                   pl.BlockSpec((ms, qw), lambda g, s: (0, g)),
                   pl.BlockSpec((ms, vw), lambda g, s: (0, g)),
                   pl.BlockSpec((ms, vw), lambda g, s: (0, g))),
        out_shape=(jax.ShapeDtypeStruct((BATCH * SEQ, W_C), BF16),
                   jax.ShapeDtypeStruct((BATCH, H_C, DV_C, DQK_C), F32),
                   jax.ShapeDtypeStruct((BATCH, H_C, 1, DQK_C), F32),
                   jax.ShapeDtypeStruct((BATCH, H_C, 1, 128), F32),
                   jax.ShapeDtypeStruct((ms, H_C * DQK_C), F32),
                   jax.ShapeDtypeStruct((ms, H_C * DQK_C), F32),
                   jax.ShapeDtypeStruct((ms, W_C), F32),
                   jax.ShapeDtypeStruct((ms, W_C), F32)),
        scratch_shapes=[pltpu.VMEM((CW, k), BF16),
                        pltpu.VMEM((CT, CW), F32),
                        pltpu.VMEM((CG, DV_C, DQK_C), F32),
                        pltpu.VMEM((CG, 1, DQK_C), F32),
                        pltpu.VMEM((CG, 8, 128), F32),
                        pltpu.VMEM((CT // CHUNK, CHUNK, CHUNK), F32)],
        compiler_params=_params(("arbitrary", "arbitrary")),
        name="odd_heads",
    )(hp, hs, w_t, w_t, w_t, w_t, w_t, bg, gm)


DT = 512


def _odd_mlp_fused_kernel(hp_ref, w0_ref, w1_ref, w2_ref, wt_ref, lng_ref, lnb_ref, ws_ref, bst_ref,
                          yd_ref, wb, pt, wsb):
    s = pl.program_id(0)
    L = CHUNK
    uc = slice(0, W_D)
    vc = slice(W_D, 2 * W_D)
    zc = slice(2 * W_D, 3 * W_D)

    @pl.when(s == 0)
    def _():
        sh = N_GATE
        wb[uc, :] = jnp.concatenate([w0_ref[sh:, :], w1_ref[0:sh, :]], axis=0).astype(BF16)
        wb[vc, :] = jnp.concatenate([w1_ref[sh:, :], w2_ref[0:sh, :]], axis=0).astype(BF16)
        wb[zc, :] = jnp.concatenate([w2_ref[sh:, :], wt_ref[...]], axis=0).astype(BF16)
        keep = (lax.broadcasted_iota(jnp.int32, (L, L), 0) >= lax.broadcasted_iota(jnp.int32, (L, L), 1))
        for g in range(G_D):
            wsb[g] = jnp.where(keep, ws_ref[g], 0.0).astype(BF16)

    @pl.when(s > 0)
    def _():
        n = DT // L
        rows = lambda c: slice(c * L, (c + 1) * L)
        grp = lambda sl, g: slice(sl.start + g * 128, sl.start + (g + 1) * 128)
        vn = {}
        mix = {}

        def project(pc):
            pt[:, pc] = _dot_nt(hp_ref[...], wb[pc, :])

        def stage_a(c):
            dv = lambda g: pt[rows(c), grp(vc, g)]
            tot = dv(0)
            for g in range(1, G_D):
                tot = tot + dv(g)
            mu = jnp.sum(tot, axis=-1, keepdims=True) * (1.0 / W_D)
            sq = (dv(0) - mu) * (dv(0) - mu)
            for g in range(1, G_D):
                sq = sq + (dv(g) - mu) * (dv(g) - mu)
            rstd = lax.rsqrt(jnp.sum(sq, axis=-1, keepdims=True) * (1.0 / W_D) + EPS)
            vn[c] = [((dv(g) - mu) * rstd * lng_ref[0:1, g * 128:(g + 1) * 128]
                      + lnb_ref[0:1, g * 128:(g + 1) * 128]).astype(BF16) for g in range(G_D)]

        def stage_b(c):
            mix[c] = [_dot(wsb[g], vn[c][g]) for g in range(G_D)]

        def stage_e(c):
            for g in range(G_D):
                sg = mix[c][g] + bst_ref[:, g:g + 1]
                d_u = pt[rows(c), grp(uc, g)]
                d_z = pt[rows(c), grp(zc, g)]
                yd_ref[rows(c), g * 128:(g + 1) * 128] = (d_u * sg * _silu(d_z)).astype(BF16)

        project(vc)
        for c in range(n):
            stage_a(c)
        project(uc)
        for c in range(n):
            stage_b(c)
        project(zc)
        for c in range(n):
            stage_e(c)


def _odd_mlp_fused(hp, w_t, lng, lnb, ws, bst):
    k = hp.shape[1]
    m = hp.shape[0]
    tile = lambda s: (jnp.maximum(s - 1, 0), 0)
    const2 = lambda s: (0, 0)
    t0 = O_GATE // IN_TN
    once = pl.Buffered(1)
    wspec = lambda j: pl.BlockSpec((IN_TN, k), lambda s: (t0 + j, 0), pipeline_mode=once)
    return pl.pallas_call(
        _odd_mlp_fused_kernel,
        grid=(m // DT + 1,),
        in_specs=[pl.BlockSpec((DT, k), tile),
                  wspec(0), wspec(1), wspec(2),
                  pl.BlockSpec((N_GATE, k), lambda s: ((t0 + 3) * (IN_TN // N_GATE), 0), pipeline_mode=once),
                  pl.BlockSpec((1, W_D), const2),
                  pl.BlockSpec((1, W_D), const2),
                  pl.BlockSpec((G_D, CHUNK, CHUNK), lambda s: (0, 0, 0)),
                  pl.BlockSpec((CHUNK, G_D), const2)],
        out_specs=pl.BlockSpec((DT, W_D), tile),
        out_shape=jax.ShapeDtypeStruct((m, W_D), BF16),
        scratch_shapes=[pltpu.VMEM((3 * W_D, k), BF16),
                        pltpu.VMEM((DT, 3 * W_D), F32),
                        pltpu.VMEM((G_D, CHUNK, CHUNK), BF16)],
        compiler_params=_params(("arbitrary",)),
        name="odd_mlp_fused",
    )(hp, w_t, w_t, w_t, w_t, lng, lnb, ws, bst)


def _in_proj_rows_kernel(h_ref, w_ref, wn_ref, o_ref, *, shift):
    wsh = jnp.concatenate([w_ref[shift:, :], wn_ref[...]], axis=0)
    o_ref[...] = _dot_nt(h_ref[...], wsh.astype(BF16))


def _in_proj_rows(h, w_t, n_out, shift, tile0):
    ms, k = h.shape
    return pl.pallas_call(
        functools.partial(_in_proj_rows_kernel, shift=shift),
        grid=(n_out // IN_TN,),
        in_specs=[pl.BlockSpec((ms, k), lambda j: (0, 0)),
                  pl.BlockSpec((IN_TN, k), lambda j: (j + tile0, 0)),
                  pl.BlockSpec((shift, k), lambda j: ((j + tile0 + 1) * (IN_TN // shift), 0))],
        out_specs=pl.BlockSpec((ms, IN_TN), lambda j: (0, j)),
        out_shape=jax.ShapeDtypeStruct((ms, n_out), F32),
        compiler_params=_params(("arbitrary",)),
        name="in_proj_rows",
    )(h, w_t, w_t)


def _odd_sample_kernel(pq_ref, pk_ref, pv_ref, pz_ref, h_ref, wg_ref, pd_ref,
                       c_ref, nrow_ref, mrow_ref, bg_ref, gm_ref, lng_ref, lnb_ref,
                       wt_ref, bt_ref,
                       yc_ref, yd_ref, vn_ref, co_ref, no_ref, mo_ref,
                       inter_scr, wmix_scr):
    h = pl.program_id(1)
    row = lax.broadcasted_iota(jnp.int32, (SR, SR), 0)
    col = lax.broadcasted_iota(jnp.int32, (SR, SR), 1)
    trow = row & 3

    @pl.when(jnp.logical_and(pl.program_id(0) == 0, h == 0))
    def _():
        mask_d = jnp.where((row >> 2) == (col >> 2), trow - (col & 3), -1) >= 0
        rep = jnp.where(trow == col, 1.0, 0.0).astype(BF16)
        for g in range(G_D):
            wtile = _dot_nt(_dot(rep, wt_ref[g].astype(BF16)).astype(BF16), rep)
            wmix_scr[g] = jnp.where(mask_d, wtile, 0.0).astype(BF16)

    @pl.when(h == 0)
    def _():
        dv = pd_ref[:, W_D:2 * W_D]
        mu = jnp.mean(dv, axis=-1, keepdims=True)
        xc = dv - mu
        var = jnp.mean(xc * xc, axis=-1, keepdims=True)
        rstd = lax.rsqrt(var + EPS)
        for g in range(G_D):
            sl = slice(g * 128, (g + 1) * 128)
            vn = xc[:, sl] * rstd * lng_ref[0:1, sl] + lnb_ref[0:1, sl]
            vn_ref[:, sl] = vn
            s = _dot(wmix_scr[g], vn.astype(BF16)) + bt_ref[:, g:g + 1]
            d_u = pd_ref[:, g * 128:(g + 1) * 128]
            d_z = pd_ref[:, 2 * W_D + g * 128:2 * W_D + (g + 1) * 128]
            yd_ref[:, sl] = (d_u * s * _silu(d_z)).astype(BF16)

    same = (row >> 2) == (col >> 2)
    mask = jnp.where(same, trow - (col & 3), -1) >= 0
    pre = _dot_nt(h_ref[...], wg_ref[...].astype(BF16)) + bg_ref[...]
    lf = _log_sigmoid(pre)
    b_full = _dot_hi(jnp.where(mask, 1.0, 0.0), lf)
    sub8 = lax.broadcasted_iota(jnp.int32, (8, DQK_C), 0)
    lane_b = lax.broadcasted_iota(jnp.int32, (DV_C, SR), 1) >> 2
    pre_t = pre.T
    b_full_t = b_full.T
    for hh in range(CS):
        hd = h * CS + hh
        qc = slice(hh * DQK_C, (hh + 1) * DQK_C)
        vc = slice(hh * DV_C, (hh + 1) * DV_C)
        sel_i = col == hd
        sel_f = col == hd + H_C
        ig_c = jnp.sum(jnp.where(sel_i, pre, 0.0), axis=-1, keepdims=True)
        b_c = jnp.sum(jnp.where(sel_f, b_full, 0.0), axis=-1, keepdims=True)
        sel_ir = row == hd
        sel_fr = row == hd + H_C
        ig_r = jnp.sum(jnp.where(sel_ir, pre_t, 0.0), axis=0, keepdims=True)
        b_r = jnp.sum(jnp.where(sel_fr, b_full_t, 0.0), axis=0, keepdims=True)
        m_prev = mrow_ref[hh]
        log_d = jnp.where(mask, b_c - b_r + ig_r, NEG_INF)
        log_inter = b_c + m_prev
        m_t = jnp.maximum(log_inter, jnp.max(log_d, axis=-1, keepdims=True))
        w = jnp.exp(log_d - m_t)
        w_inter = jnp.exp(log_inter - m_t)
        q = pq_ref[:, qc] * (DQK_C ** -0.5)
        k = pk_ref[:, qc]
        v = pv_ref[:, vc]
        qb = q.astype(BF16)
        kb = k.astype(BF16)
        vb = v.astype(BF16)
        sc = _dot_nt(qb, kb) * w
        for g in range(SR // 8):
            q8 = q[8 * g:8 * g + 8, :]
            q2 = jnp.concatenate([jnp.where(sub8 < DEC_SEQ, q8, 0.0), jnp.where(sub8 < DEC_SEQ, 0.0, q8)], axis=1)
            c_pair = jnp.concatenate([c_ref[2 * g + beta, hh].astype(BF16) for beta in range(2)], axis=1)
            inter_scr[8 * g:8 * g + 8, :] = _dot_nt(q2.astype(BF16), c_pair)
        n_rows = nrow_ref[hh]
        num = _dot(sc.astype(BF16), vb) + w_inter * inter_scr[...]
        den = jnp.sum(sc, axis=-1, keepdims=True) + w_inter * jnp.sum(q * n_rows, axis=-1, keepdims=True)
        hv = num / jnp.maximum(jnp.abs(den), jnp.exp(-m_t))
        yc_ref[:, vc] = (_head_norm(hv, gm_ref[:, vc]) * _silu(pz_ref[:, vc])).astype(BF16)

        stats = jnp.where(col == 0, m_t, jnp.where(col == 1, b_c, 0.0))
        last = _dot_hi(jnp.where(col == (row | 3), 1.0, 0.0), stats)
        m_new = last[:, 0:1]
        b_last = last[:, 1:2]
        w_end = jnp.exp(b_last - b_c + ig_c - m_new)
        cd = jnp.exp(b_last + m_prev - m_new)
        mo_ref[hh] = m_new
        no_ref[hh] = cd * n_rows + _dot_hi(jnp.where(same, 1.0, 0.0), w_end * k)
        vwt = (v * w_end).T
        for b in range(SB):
            lhs = jnp.where(lane_b == b, vwt, 0.0).astype(BF16)
            cd_b = cd[4 * b + 3:4 * b + 4, :]
            co_ref[b, hh] = cd_b * c_ref[b, hh] + _dot(lhs, kb)


def _odd_sample(pc, pd, h, w_o, c_state, n_rows, m_rows, bg, gm, lng, lnb, wt, bt):
    nb = DEC_BATCH // SB
    const2 = lambda i, h: (0, 0)
    return pl.pallas_call(
        _odd_sample_kernel,
        grid=(nb, H_C // CS),
        in_specs=[pl.BlockSpec((SR, CS * DQK_C), lambda i, h: (i, h)),
                  pl.BlockSpec((SR, CS * DQK_C), lambda i, h: (i, h)),
                  pl.BlockSpec((SR, CS * DV_C), lambda i, h: (i, h)),
                  pl.BlockSpec((SR, CS * DV_C), lambda i, h: (i, h)),
                  pl.BlockSpec((SR, D_MODEL), lambda i, h: (i, 0)),
                  pl.BlockSpec((128, D_MODEL), lambda i, h: (O_GATE // 128, 0)),
                  pl.BlockSpec((SR, 3 * W_D), lambda i, h: (i, 0)),
                  pl.BlockSpec((SB, CS, DV_C, DQK_C), lambda i, h: (i, h, 0, 0)),
                  pl.BlockSpec((CS, SR, DQK_C), lambda i, h: (h, i, 0)),
                  pl.BlockSpec((CS, SR, 1), lambda i, h: (h, i, 0)),
                  pl.BlockSpec((1, 128), const2),
                  pl.BlockSpec((1, CS * DV_C), lambda i, h: (0, h)),
                  pl.BlockSpec((1, W_D), const2),
                  pl.BlockSpec((1, W_D), const2),
                  pl.BlockSpec((G_D, SR, SR), lambda i, h: (0, 0, 0)),
                  pl.BlockSpec((SR, G_D), const2)],
        out_specs=(pl.BlockSpec((SR, CS * DV_C), lambda i, h: (i, h)),
                   pl.BlockSpec((SR, W_D), lambda i, h: (i, 0)),
                   pl.BlockSpec((SR, W_D), lambda i, h: (i, 0)),
                   pl.BlockSpec((SB, CS, DV_C, DQK_C), lambda i, h: (i, h, 0, 0)),
                   pl.BlockSpec((CS, SR, DQK_C), lambda i, h: (h, i, 0)),
                   pl.BlockSpec((CS, SR, 1), lambda i, h: (h, i, 0))),
        out_shape=(jax.ShapeDtypeStruct((DEC_BATCH * DEC_SEQ, W_C), BF16),
                   jax.ShapeDtypeStruct((DEC_BATCH * DEC_SEQ, W_D), BF16),
                   jax.ShapeDtypeStruct((DEC_BATCH * DEC_SEQ, W_D), F32),
                   jax.ShapeDtypeStruct((DEC_BATCH, H_C, DV_C, DQK_C), F32),
                   jax.ShapeDtypeStruct((H_C, DEC_BATCH * DEC_SEQ, DQK_C), F32),
                   jax.ShapeDtypeStruct((H_C, DEC_BATCH * DEC_SEQ, 1), F32)),
        scratch_shapes=[pltpu.VMEM((SR, DV_C), F32),
                        pltpu.VMEM((G_D, SR, SR), BF16)],
        compiler_params=_params(("arbitrary", "arbitrary")),
        name="odd_sample",
    )(*pc, h, w_o, pd, c_state, n_rows, m_rows, bg, gm, lng, lnb, wt, bt)


def _rope_tables(pos):
    inv = ROPE_BASE ** (-jnp.arange(0, DH_B, 2, dtype=F32) / DH_B)
    ang = pos.astype(F32)[:, None] * inv[None, :]
    cos = jnp.cos(ang)
    sin = jnp.sin(ang)
    return jnp.concatenate([cos, cos], axis=-1), jnp.concatenate([-sin, sin], axis=-1)


def kernel(x_prompt, x_sample, state_conv, state_ret, state_mlstm_C, state_mlstm_n, state_mlstm_m,
           norm_even, w_in_even, conv_w, ret_norm, w_out_even,
           norm_odd, w_in_odd, b_gate_odd, mlstm_norm, ln_v_g, ln_v_b,
           w_spatial, b_spatial, w_out_odd, norm_final):
    w_in_e = w_in_even[0]
    w_out_e = w_out_even[0].astype(BF16)
    w_o = w_in_odd[0].T
    w_out_o = w_out_odd[0].astype(BF16)
    g_even = norm_even[0][None, :]
    g_odd = norm_odd[0][None, :]
    g_fin = norm_final[None, :]
    cw = conv_w[0]
    g_ret = ret_norm[0][None, :]
    bg = jnp.concatenate([b_gate_odd[0], jnp.zeros((128 - 2 * H_C,), F32)])[None, :]
    gm = mlstm_norm[0][None, :]
    lng = ln_v_g[0][None, :]
    lnb = ln_v_b[0][None, :]
    ws = w_spatial[0]
    bst = b_spatial[0].T

    cos_p, sin_p = _rope_tables(jnp.arange(SEQ, dtype=jnp.int32))
    cos_s, sin_s = _rope_tables(PAST_LEN + jnp.arange(DEC_SEQ, dtype=jnp.int32))
    cos_s = jnp.tile(cos_s, (SB, 1))
    sin_s = jnp.tile(sin_s, (SB, 1))
    lg_tab = jnp.broadcast_to(jnp.asarray(LOG_GAMMA, F32)[:, None, None], (H_B, 1, 128))

    bt_s = jnp.tile(b_spatial[0][:, :DEC_SEQ].T, (SB, 1))

    rs = DEC_BATCH * DEC_SEQ
    xp = x_prompt.reshape(BATCH * SEQ, D_MODEL)
    xs = x_sample.reshape(rs, D_MODEL)
    hp, hs = _norm_cast(xp, xs, g_even, 1024)
    ya, conv_p, *ps_a = _even_conv(hp, hs, w_in_e, cw)
    yb, ret_p, *ps_b = _even_heads(hp, hs, w_in_e, g_ret, cos_p, sin_p, lg_tab)
    st_exp = jnp.pad(state_conv[0], ((0, 0), (0, DEC_SEQ - (CONV_W - 1)), (0, 0))).reshape(rs, W_A)
    ya_s, u_s, yb_s, ret_s = _even_sample(ps_a, ps_b, st_exp, state_ret[0], cw, g_ret, cos_s, sin_s, lg_tab)
    x1, h1, x1s, h1s = _outproj(ya, yb, xp, ya_s, yb_s, xs, w_out_e, g_odd, final=False)

    yc, c_p, n_p, m_p, *ps_c = _odd_heads(h1, h1s, w_o, bg, gm)
    yd = _odd_mlp_fused(h1, w_o, lng, lnb, ws, bst)
    pd_s = _in_proj_rows(h1s, w_o, 3 * W_D, N_GATE, O_GATE // IN_TN)
    n_rows = jnp.repeat(jnp.transpose(state_mlstm_n[0], (1, 0, 2)), DEC_SEQ, axis=1)
    m_rows = jnp.repeat(state_mlstm_m[0].T, DEC_SEQ, axis=1)[:, :, None]
    yc_s, yd_s, vn_s, c_s, no_s, mo_s = _odd_sample(
        ps_c, pd_s, h1s, w_o, state_mlstm_C[0], n_rows, m_rows, bg, gm, lng, lnb, ws, bt_s)
    y_prompt, y_sample = _outproj(yc, yd, x1, yc_s, yd_s, x1s, w_out_o, g_fin, final=True)

    conv_s = u_s.reshape(DEC_BATCH, DEC_SEQ, W_A)[:, DEC_SEQ - (CONV_W - 1):, :]
    n_s = jnp.transpose(no_s[:, DEC_SEQ - 1::DEC_SEQ, :], (1, 0, 2))
    m_s = mo_s[:, DEC_SEQ - 1::DEC_SEQ, 0].T
    return (y_prompt.reshape(BATCH, SEQ, D_MODEL),
            y_sample.reshape(DEC_BATCH, DEC_SEQ, D_MODEL),
            conv_p[None], conv_s[None],
            ret_p[None], ret_s[None],
            c_p[None], c_s[None],
            n_p[:, :, 0, :][None], n_s[None],
            m_p[:, :, 0, 0][None], m_s[None],
            vn_s.reshape(DEC_BATCH, DEC_SEQ, W_D)[None])
```

```python
import functools
import math

import jax
import jax.numpy as jnp
from jax import lax
from jax.experimental import pallas as pl
from jax.experimental.pallas import tpu as pltpu

F32 = jnp.float32
BF16 = jnp.bfloat16

D_MODEL = 2048
BATCH = 4
SEQ = 2048
DEC_BATCH = 128
DEC_SEQ = 4
PAST_LEN = 16384
W_A = 1024
CONV_W = 3
W_B = 1024
H_B = 8
DH_B = 128
W_C = 1024
H_C = 4
DV_C = 256
DQK_C = 128
W_D = 1024
G_D = 8
CHUNK = 128
O_GATE = 2 * H_C * DQK_C + 2 * W_C
N_GATE = 2 * H_C
ROPE_BASE = 10000.0
EPS = 1e-6
LOG_GAMMA = tuple(math.log(1.0 - 2.0 ** (-5.0 - h)) for h in range(H_B))
NEG_INF = float("-inf")
VMEM_LIMIT = 56 * 1024 * 1024

NT_DIMS = (((1,), (1,)), ((), ()))
TN_DIMS = (((0,), (0,)), ((), ()))


def _silu(z):
    return z * (1.0 / (1.0 + jnp.exp(-z)))


def _log_sigmoid(x):
    return jnp.minimum(x, 0.0) - jnp.log1p(jnp.exp(-jnp.abs(x)))


def _dot(a, b):
    return jnp.dot(a, b, preferred_element_type=F32)


def _dot_nt(a, b):
    return lax.dot_general(a, b, NT_DIMS, preferred_element_type=F32)


def _dot_tn(a, b):
    return lax.dot_general(a, b, TN_DIMS, preferred_element_type=F32)


def _dot_hi(a, b):
    return jnp.dot(a, b, preferred_element_type=F32, precision=lax.Precision.HIGHEST)


def _head_norm(o, g):
    mu = jnp.mean(o, axis=-1, keepdims=True)
    oc = o - mu
    var = jnp.mean(oc * oc, axis=-1, keepdims=True)
    return oc * lax.rsqrt(var + EPS) * g


def _params(sem):
    return pltpu.CompilerParams(dimension_semantics=sem, vmem_limit_bytes=VMEM_LIMIT)


def _norm_cast_kernel(x_ref, xs_ref, g_ref, h_ref, hs_ref, *, n_prompt):
    i = pl.program_id(0)

    def tile(x_in, h_out):
        x = x_in[...]
        ms = jnp.mean(x * x, axis=-1, keepdims=True)
        h_out[...] = (x * lax.rsqrt(ms + EPS) * g_ref[...]).astype(BF16)

    @pl.when(i < n_prompt)
    def _():
        tile(x_ref, h_ref)

    @pl.when(i == n_prompt)
    def _():
        tile(xs_ref, hs_ref)


def _norm_cast(x, xs, g, tm):
    m, d = x.shape
    ms = xs.shape[0]
    n_prompt = m // tm
    row = lambda i: (jnp.minimum(i, n_prompt - 1), 0)
    const = lambda i: (0, 0)
    return pl.pallas_call(
        functools.partial(_norm_cast_kernel, n_prompt=n_prompt),
        grid=(n_prompt + 1,),
        in_specs=[pl.BlockSpec((tm, d), row),
                  pl.BlockSpec((ms, d), const),
                  pl.BlockSpec((1, d), const)],
        out_specs=(pl.BlockSpec((tm, d), row), pl.BlockSpec((ms, d), const)),
        out_shape=(jax.ShapeDtypeStruct((m, d), BF16), jax.ShapeDtypeStruct((ms, d), BF16)),
        compiler_params=_params(("arbitrary",)),
        name="norm_cast",
    )(x, xs, g)


IN_TN = 1024


OUT_TM = 512


def _outproj_kernel(ya_ref, yb_ref, x_ref, yas_ref, ybs_ref, xs_ref, w_ref, g_ref, *out_refs,
                    final, n_prompt):
    i = pl.program_id(0)
    half = ya_ref.shape[1]
    n_out = 1 if final else 2

    def tile(ya, yb, x, outs):
        acc = _dot(ya[...], w_ref[0:half, :]) + _dot(yb[...], w_ref[half:2 * half, :])
        x1 = x[...] + acc
        ms = jnp.mean(x1 * x1, axis=-1, keepdims=True)
        hn = x1 * lax.rsqrt(ms + EPS) * g_ref[...]
        if final:
            outs[0][...] = hn
        else:
            outs[0][...] = x1
            outs[1][...] = hn.astype(BF16)

    @pl.when(i < n_prompt)
    def _():
        tile(ya_ref, yb_ref, x_ref, out_refs[:n_out])

    @pl.when(i == n_prompt)
    def _():
        tile(yas_ref, ybs_ref, xs_ref, out_refs[n_out:])


def _outproj(ya, yb, x, ya_s, yb_s, x_s, w, g, final):
    m, half = ya.shape
    ms = ya_s.shape[0]
    d = w.shape[1]
    n_prompt = m // OUT_TM
    row = lambda i: (jnp.minimum(i, n_prompt - 1), 0)
    const = lambda i: (0, 0)
    once = pl.Buffered(1)
    shapes = [jax.ShapeDtypeStruct((m, d), F32), jax.ShapeDtypeStruct((ms, d), F32)]
    specs = [pl.BlockSpec((OUT_TM, d), row), pl.BlockSpec((ms, d), const)]
    if not final:
        shapes = [shapes[0], jax.ShapeDtypeStruct((m, d), BF16), shapes[1], jax.ShapeDtypeStruct((ms, d), BF16)]
        specs = [specs[0], pl.BlockSpec((OUT_TM, d), row), specs[1], pl.BlockSpec((ms, d), const)]
    return pl.pallas_call(
        functools.partial(_outproj_kernel, final=final, n_prompt=n_prompt),
        grid=(n_prompt + 1,),
        in_specs=[pl.BlockSpec((OUT_TM, half), row),
                  pl.BlockSpec((OUT_TM, half), row),
                  pl.BlockSpec((OUT_TM, d), row),
                  pl.BlockSpec((ms, half), const, pipeline_mode=once),
                  pl.BlockSpec((ms, half), const, pipeline_mode=once),
                  pl.BlockSpec((ms, d), const, pipeline_mode=once),
                  pl.BlockSpec((2 * half, d), const, pipeline_mode=once),
                  pl.BlockSpec((1, d), const)],
        out_specs=tuple(specs),
        out_shape=tuple(shapes),
        compiler_params=_params(("arbitrary",)),
        name="out_proj_final" if final else "out_proj",
    )(ya, yb, x, ya_s, yb_s, x_s, w, g)


def _rope(x, cosf, sins):
    return x * cosf + pltpu.roll(x, DH_B // 2, 1) * sins


FT = 1024
FN = BATCH * SEQ // FT
FPB = SEQ // FT
HG = 2
GW = HG * 128
PCH = 2


def _chunk_pipeline(n, piece, stages, gate=None):
    sa, sb, sc, sd, se = stages
    npieces = n // PCH
    for j in range(PCH):
        piece(0, j)
    if gate is not None:
        gate(0)
    for c in range(n + 2):
        k, j = c // PCH + 1, c % PCH
        if k < npieces:
            piece(k, j)
        if c < n:
            sa(c)
        if 1 <= c <= n:
            sc(c - 1)
        if c < n:
            sb(c)
        if 1 <= c <= n:
            sd(c - 1)
        if c >= 2:
            se(c - 2)
        if gate is not None and j == PCH - 1 and k < npieces:
            gate(k)


def _even_heads_kernel(hp_ref, hs_ref, wq_ref, wk_ref, wv_ref, wz_ref, gret_ref, cos_ref, sin_ref, lg_ref,
                       yb_ref, s_ref, sq_ref, sk_ref, sv_ref, sz_ref, wb, pt, s_scr):
    s = pl.program_id(1)
    L = CHUNK

    @pl.when(s == 0)
    def _():
        for part, w_ref in enumerate((wq_ref, wk_ref, wv_ref, wz_ref)):
            wb[:, part * GW:(part + 1) * GW] = w_ref[...].astype(BF16)
        ps = _dot(hs_ref[...], wb[...])
        for part, o_ref in enumerate((sq_ref, sk_ref, sv_ref, sz_ref)):
            o_ref[...] = ps[:, part * GW:(part + 1) * GW]

    @pl.when(s > 0)
    def _():
        t = s - 1
        n = FT // L
        rows = lambda c: slice(c * L, (c + 1) * L)
        cols = lambda part, i: slice(part * GW + i * DH_B, part * GW + (i + 1) * DH_B)

        def piece(k, j):
            pr = slice(k * PCH * L, (k + 1) * PCH * L)
            pc = slice(j * 2 * GW, (j + 1) * 2 * GW)
            pt[pr, pc] = _dot(hp_ref[pr, :], wb[:, pc])

        row = lax.broadcasted_iota(jnp.int32, (L, L), 0)
        col = lax.broadcasted_iota(jnp.int32, (L, L), 1)
        causal = row >= col
        diff = jnp.maximum(row - col, 0).astype(F32)
        ti = lax.broadcasted_iota(jnp.int32, (L, 1), 0).astype(F32)
        lgs = [lg_ref[i][:, 0:1] for i in range(HG)]
        decay = [jnp.where(causal, jnp.exp(lg * diff), 0.0) for lg in lgs]
        q_decay = [jnp.exp(lg * (ti + 1.0)) for lg in lgs]
        k_decay = [jnp.exp(lg * (L - 1.0 - ti)) for lg in lgs]
        gamma_l = [jnp.exp(lg * float(L)) for lg in lgs]
        state = {0: [jnp.where(t % FPB == 0, 0.0, s_scr[i]) for i in range(HG)]}
        v = {}

        def stage_a(c):
            cosf = cos_ref[rows(c), :]
            sins = sin_ref[rows(c), :]
            v[c] = []
            for i in range(HG):
                kr = _rope(pt[rows(c), cols(1, i)], cosf, sins) * (DH_B ** -0.5)
                v[c].append(dict(qb=_rope(pt[rows(c), cols(0, i)], cosf, sins).astype(BF16),
                                 kb=kr.astype(BF16),
                                 kd=(kr * k_decay[i]).astype(BF16),
                                 vb=pt[rows(c), cols(2, i)].astype(BF16)))

        def stage_b(c):
            for i, d in enumerate(v[c]):
                d["sc"] = _dot_nt(d["qb"], d["kb"])
                d["upd"] = _dot_tn(d["kd"], d["vb"])
            for i, d in enumerate(v[c]):
                d["cross"] = _dot(d["qb"], state[c][i].astype(BF16))

        def stage_c(c):
            state[c + 1] = []
            for i, d in enumerate(v[c]):
                d["sc"] = (d["sc"] * decay[i]).astype(BF16)
                state[c + 1].append(gamma_l[i] * state[c][i] + d["upd"])

        def stage_d(c):
            for d in v[c]:
                d["inner"] = _dot(d["sc"], d["vb"])

        def stage_e(c):
            for i, d in enumerate(v.pop(c)):
                o = d["inner"] + d["cross"] * q_decay[i]
                g = gret_ref[0:1, i * DH_B:(i + 1) * DH_B]
                z = pt[rows(c), cols(3, i)]
                yb_ref[rows(c), i * DH_B:(i + 1) * DH_B] = (_head_norm(o, g) * _silu(z)).astype(BF16)

        _chunk_pipeline(n, piece, (stage_a, stage_b, stage_c, stage_d, stage_e))
        for i in range(HG):
            s_scr[i] = state[n][i]
            s_ref[0, i] = state[n][i]


def _even_heads(hp, hs, w, g_ret, cosf, sins, lg_tab):
    k = hp.shape[1]
    ms = hs.shape[0]
    ng = H_B // HG
    base = 4 * W_A // GW
    tile = lambda s: jnp.maximum(s - 1, 0)
    wspec = lambda part: pl.BlockSpec((k, GW), lambda g, s: (0, base + part * ng + g))
    sspec = pl.BlockSpec((ms, GW), lambda g, s: (0, g))
    sshape = jax.ShapeDtypeStruct((ms, W_B), F32)
    return pl.pallas_call(
        _even_heads_kernel,
        grid=(ng, FN + 1),
        in_specs=[pl.BlockSpec((FT, k), lambda g, s: (tile(s), 0)),
                  pl.BlockSpec((ms, k), lambda g, s: (0, 0)),
                  wspec(0), wspec(1), wspec(2), wspec(3),
                  pl.BlockSpec((1, GW), lambda g, s: (0, g)),
                  pl.BlockSpec((FT, DH_B), lambda g, s: (tile(s) % FPB, 0)),
                  pl.BlockSpec((FT, DH_B), lambda g, s: (tile(s) % FPB, 0)),
                  pl.BlockSpec((HG, 1, 128), lambda g, s: (g, 0, 0))],
        out_specs=(pl.BlockSpec((FT, GW), lambda g, s: (tile(s), g)),
                   pl.BlockSpec((1, HG, DH_B, DH_B), lambda g, s: (tile(s) // FPB, g, 0, 0)),
                   sspec, sspec, sspec, sspec),
        out_shape=(jax.ShapeDtypeStruct((BATCH * SEQ, W_B), BF16),
                   jax.ShapeDtypeStruct((BATCH, H_B, DH_B, DH_B), F32),
                   sshape, sshape, sshape, sshape),
        scratch_shapes=[pltpu.VMEM((k, 4 * GW), BF16),
                        pltpu.VMEM((FT, 4 * GW), F32),
                        pltpu.VMEM((HG, DH_B, DH_B), F32)],
        compiler_params=_params(("arbitrary", "arbitrary")),
        name="even_heads",
    )(hp, hs, w, w, w, w, g_ret, cosf, sins, lg_tab)


def _even_conv_kernel(hp_ref, hs_ref, wb_ref, wc_ref, wx_ref, wz_ref, cw_ref,
                      ya_ref, conv_ref, sb_ref, sc_ref, sx_ref, sz_ref, wb, pt, ubuf):
    s = pl.program_id(1)
    L = CHUNK
    sub = lambda j: slice(j * 4 * 128, (j + 1) * 4 * 128)
    cs = lambda j, p: slice(j * 4 * 128 + p * 128, j * 4 * 128 + (p + 1) * 128)

    @pl.when(s == 0)
    def _():
        for p, w_ref in enumerate((wb_ref, wc_ref, wx_ref, wz_ref)):
            for j in range(HG):
                wb[:, cs(j, p)] = w_ref[:, j * 128:(j + 1) * 128].astype(BF16)
        ps = _dot(hs_ref[...], wb[...])
        for p, o_ref in enumerate((sb_ref, sc_ref, sx_ref, sz_ref)):
            for j in range(HG):
                o_ref[:, j * 128:(j + 1) * 128] = ps[:, cs(j, p)]

    @pl.when(s > 0)
    def _():
        t = s - 1
        n = FT // L
        rows = lambda c: slice(c * L, (c + 1) * L)

        @pl.when(t % FPB == 0)
        def _():
            ubuf[0:8, :] = jnp.zeros((8, GW), F32)

        @pl.when(t % FPB != 0)
        def _():
            ubuf[0:8, :] = ubuf[FT:FT + 8, :]

        def project(j):
            pt[:, sub(j)] = _dot(hp_ref[...], wb[:, sub(j)])

        def mix(j):
            ch = slice(j * 128, (j + 1) * 128)
            for c in range(n):
                u = pt[rows(c), cs(j, 1)] * pt[rows(c), cs(j, 2)]
                ubuf[8 + c * L:8 + (c + 1) * L, ch] = u
                t0 = ubuf[6 + c * L:6 + (c + 1) * L, ch]
                t1 = ubuf[7 + c * L:7 + (c + 1) * L, ch]
                conv = cw_ref[0:1, ch] * t0 + cw_ref[1:2, ch] * t1 + cw_ref[2:3, ch] * u
                ya_ref[rows(c), ch] = (pt[rows(c), cs(j, 0)] * conv * _silu(pt[rows(c), cs(j, 3)])).astype(BF16)

        project(0)
        for j in range(HG):
            if j + 1 < HG:
                project(j + 1)
            mix(j)
        conv_ref[0] = ubuf[FT + 6:FT + 8, :]


def _even_conv(hp, hs, w, conv_w):
    k = hp.shape[1]
    ms = hs.shape[0]
    ng = W_A // GW
    tile = lambda s: jnp.maximum(s - 1, 0)
    wspec = lambda part: pl.BlockSpec((k, GW), lambda g, s: (0, part * ng + g))
    sspec = pl.BlockSpec((ms, GW), lambda g, s: (0, g))
    sshape = jax.ShapeDtypeStruct((ms, W_A), F32)
    return pl.pallas_call(
        _even_conv_kernel,
        grid=(ng, FN + 1),
        in_specs=[pl.BlockSpec((FT, k), lambda g, s: (tile(s), 0)),
                  pl.BlockSpec((ms, k), lambda g, s: (0, 0)),
                  wspec(0), wspec(1), wspec(2), wspec(3),
                  pl.BlockSpec((CONV_W, GW), lambda g, s: (0, g))],
        out_specs=(pl.BlockSpec((FT, GW), lambda g, s: (tile(s), g)),
                   pl.BlockSpec((1, CONV_W - 1, GW), lambda g, s: (tile(s) // FPB, 0, g)),
                   sspec, sspec, sspec, sspec),
        out_shape=(jax.ShapeDtypeStruct((BATCH * SEQ, W_A), BF16),
                   jax.ShapeDtypeStruct((BATCH, CONV_W - 1, W_A), F32),
                   sshape, sshape, sshape, sshape),
        scratch_shapes=[pltpu.VMEM((k, 4 * GW), BF16),
                        pltpu.VMEM((FT, 4 * GW), F32),
                        pltpu.VMEM((FT + 8, GW), F32)],
        compiler_params=_params(("arbitrary", "arbitrary")),
        name="even_conv",
    )(hp, hs, w, w, w, w, conv_w)


SB = 32
SR = SB * DEC_SEQ
HS = 4
CS = 2


def _even_sample_kernel(ab_ref, ac_ref, ax_ref, az_ref, pq_ref, pk_ref, pv_ref, pz_ref, st_ref, s_ref,
                        cw_ref, gret_ref, cos_ref, sin_ref, lg_ref,
                        ya_ref, u_ref, yb_ref, so_ref, cross_scr):
    h = pl.program_id(1)
    row = lax.broadcasted_iota(jnp.int32, (SR, SR), 0)
    col = lax.broadcasted_iota(jnp.int32, (SR, SR), 1)
    trow = row & 3

    @pl.when(h == 0)
    def _():
        for j in range(W_A // 128):
            sl = slice(j * 128, (j + 1) * 128)
            a_b = ab_ref[:, sl]
            a_c = ac_ref[:, sl]
            a_x = ax_ref[:, sl]
            a_z = az_ref[:, sl]
            u = a_c * a_x
            e = st_ref[:, sl]
            tap1 = jnp.where(trow >= 1, pltpu.roll(u, 1, 0), pltpu.roll(e, SR - 1, 0))
            tap0 = jnp.where(trow >= 2, pltpu.roll(u, 2, 0), e)
            conv = cw_ref[0:1, sl] * tap0 + cw_ref[1:2, sl] * tap1 + cw_ref[2:3, sl] * u
            ya_ref[:, sl] = (a_b * conv * _silu(a_z)).astype(BF16)
            u_ref[:, sl] = u

    same = (row >> 2) == (col >> 2)
    dd = trow - (col & 3)
    mask = jnp.where(same, dd, -1) >= 0
    tcol = (lax.broadcasted_iota(jnp.int32, (SR, 1), 0) & 3).astype(F32)
    cosf = cos_ref[...]
    sins = sin_ref[...]
    lane_b = col >> 2
    sub = lax.broadcasted_iota(jnp.int32, (8, DH_B), 0)
    for hh in range(HS):
        hc = slice(hh * DH_B, (hh + 1) * DH_B)
        lg = lg_ref[hh][:, 0:1]
        decay = jnp.where(mask, jnp.exp(lg * jnp.maximum(dd, 0).astype(F32)), 0.0)
        qr = _rope(pq_ref[:, hc], cosf, sins)
        kr = _rope(pk_ref[:, hc], cosf, sins) * (DH_B ** -0.5)
        qb = qr.astype(BF16)
        kb = kr.astype(BF16)
        vb = pv_ref[:, hc].astype(BF16)
        sc = _dot_nt(qb, kb) * decay
        inner = _dot(sc.astype(BF16), vb)
        kdt = (kr * jnp.exp(lg * (DEC_SEQ - 1.0 - tcol))).T
        gamma_l = jnp.exp(lg * float(DEC_SEQ))
        for g in range(SR // 8):
            q8 = qr[8 * g:8 * g + 8, :]
            q2 = jnp.concatenate([jnp.where(sub < DEC_SEQ, q8, 0.0), jnp.where(sub < DEC_SEQ, 0.0, q8)], axis=1)
            s_pair = [s_ref[2 * g + beta, hh] for beta in range(2)]
            cross_scr[hh, 8 * g:8 * g + 8, :] = _dot(
                q2.astype(BF16), jnp.concatenate([sp.astype(BF16) for sp in s_pair], axis=0))
            for beta in range(2):
                b = 2 * g + beta
                lhs = jnp.where(lane_b == b, kdt, 0.0).astype(BF16)
                so_ref[b, hh] = gamma_l * s_pair[beta] + _dot(lhs, vb)
        o = inner + cross_scr[hh] * jnp.exp(lg * (tcol + 1.0))
        yb_ref[:, hc] = (_head_norm(o, gret_ref[:, hc]) * _silu(pz_ref[:, hc])).astype(BF16)


def _even_sample(pa, pb, st_exp, s_state, conv_w, g_ret, cosf, sins, lg_tab):
    nb = DEC_BATCH // SB
    const2 = lambda i, h: (0, 0)
    aspec = pl.BlockSpec((SR, W_A), lambda i, h: (i, 0))
    hspec = pl.BlockSpec((SR, HS * DH_B), lambda i, h: (i, h))
    return pl.pallas_call(
        _even_sample_kernel,
        grid=(nb, H_B // HS),
        in_specs=[aspec, aspec, aspec, aspec,
                  hspec, hspec, hspec, hspec,
                  pl.BlockSpec((SR, W_A), lambda i, h: (i, 0)),
                  pl.BlockSpec((SB, HS, DH_B, DH_B), lambda i, h: (i, h, 0, 0)),
                  pl.BlockSpec((CONV_W, W_A), const2),
                  pl.BlockSpec((1, HS * DH_B), lambda i, h: (0, h)),
                  pl.BlockSpec((SR, DH_B), const2),
                  pl.BlockSpec((SR, DH_B), const2),
                  pl.BlockSpec((HS, 1, 128), lambda i, h: (h, 0, 0))],
        out_specs=(pl.BlockSpec((SR, W_A), lambda i, h: (i, 0)),
                   pl.BlockSpec((SR, W_A), lambda i, h: (i, 0)),
                   pl.BlockSpec((SR, HS * DH_B), lambda i, h: (i, h)),
                   pl.BlockSpec((SB, HS, DH_B, DH_B), lambda i, h: (i, h, 0, 0))),
        out_shape=(jax.ShapeDtypeStruct((DEC_BATCH * DEC_SEQ, W_A), BF16),
                   jax.ShapeDtypeStruct((DEC_BATCH * DEC_SEQ, W_A), F32),
                   jax.ShapeDtypeStruct((DEC_BATCH * DEC_SEQ, W_B), BF16),
                   jax.ShapeDtypeStruct((DEC_BATCH, H_B, DH_B, DH_B), F32)),
        scratch_shapes=[pltpu.VMEM((HS, SR, DH_B), F32)],
        compiler_params=_params(("arbitrary", "arbitrary")),
        name="even_sample",
    )(*pa, *pb, st_exp, s_state, conv_w, g_ret, cosf, sins, lg_tab)


CG = 2
CW = 128 + CG * (2 * DQK_C + 2 * DV_C)
CT = 1024
CN = BATCH * SEQ // CT
CPB = SEQ // CT


def _odd_heads_kernel(hp_ref, hs_ref, wq_ref, wk_ref, wv_ref, wz_ref, wg_ref, bg_ref, gm_ref,
                      yc_ref, c_ref, n_ref, m_ref, sq_ref, sk_ref, sv_ref, sz_ref,
                      wb, pt, c_scr, n_scr, m_scr, gt_scr):
    grp = pl.program_id(0)
    s = pl.program_id(1)
    L = CHUNK
    gc = slice(0, 128)
    qc = slice(128, 128 + CG * DQK_C)
    kc = slice(qc.stop, qc.stop + CG * DQK_C)
    vc = slice(kc.stop, kc.stop + CG * DV_C)
    zc = slice(vc.stop, vc.stop + CG * DV_C)
    head = lambda sl, i, w: slice(sl.start + i * w, sl.start + (i + 1) * w)

    @pl.when(s == 0)
    def _():
        wb[qc, :] = wq_ref[...].astype(BF16)
        wb[kc, :] = wk_ref[...].astype(BF16)
        wb[vc, :] = wv_ref[...].astype(BF16)
        wb[zc, :] = wz_ref[...].astype(BF16)
        wb[gc, :] = wg_ref[...].astype(BF16)
        ps = _dot_nt(hs_ref[...], wb[qc.start:CW, :])
        off = lambda sl: slice(sl.start - qc.start, sl.stop - qc.start)
        sq_ref[...] = ps[:, off(qc)]
        sk_ref[...] = ps[:, off(kc)]
        sv_ref[...] = ps[:, off(vc)]
        sz_ref[...] = ps[:, off(zc)]

    @pl.when(s > 0)
    def _():
        t = s - 1
        n = CT // L
        rows = lambda c: slice(c * L, (c + 1) * L)

        def piece(k, j):
            pr = slice(k * PCH * L, (k + 1) * PCH * L)
            pc = (slice(0, vc.start), slice(vc.start, CW))[j]
            pt[pr, pc] = _dot_nt(hp_ref[pr, :], wb[pc, :])

        row = lax.broadcasted_iota(jnp.int32, (L, L), 0)
        col = lax.broadcasted_iota(jnp.int32, (L, L), 1)
        tri = row >= col
        fresh = t % CPB == 0
        cst = {0: [jnp.where(fresh, 0.0, c_scr[i]) for i in range(CG)]}
        nst = {0: [jnp.where(fresh, 0.0, n_scr[i]) for i in range(CG)]}
        mst = {0: [jnp.where(fresh, 0.0, m_scr[i, 0:1, 0:1]) for i in range(CG)]}
        v = {}
        gates = {}

        def gate(k):
            cs = range(k * PCH, (k + 1) * PCH)
            for c in cs:
                gt_scr[c] = (pt[rows(c), gc] + bg_ref[...]).T
            pad = jnp.zeros((8 - CG * PCH, L), F32)
            ig_rows = jnp.concatenate(
                [gt_scr[c, pl.ds(grp * CG + i, 1), :] for i in range(CG) for c in cs] + [pad], axis=0)
            lf_rows = jnp.concatenate(
                [_log_sigmoid(gt_scr[c, pl.ds(grp * CG + i + H_C, 1), :]) for i in range(CG) for c in cs]
                + [pad], axis=0)
            b_rows = _dot_hi(lf_rows, jnp.where(row <= col, 1.0, 0.0))
            tall = jnp.zeros((L - 8, L), F32)
            gates[k] = dict(ig_rows=ig_rows, b_rows=b_rows,
                            b_cols=jnp.concatenate([b_rows, tall], axis=0).T,
                            ig_cols=jnp.concatenate([ig_rows, tall], axis=0).T)

        def stage_a(c):
            gk = gates[c // PCH]
            v[c] = []
            nst[c + 1] = []
            mst[c + 1] = []
            for i in range(CG):
                r = i * PCH + c % PCH
                b_r = gk["b_rows"][r:r + 1, :]
                ig_r = gk["ig_rows"][r:r + 1, :]
                b_c = gk["b_cols"][:, r:r + 1]
                ig_c = gk["ig_cols"][:, r:r + 1]
                m_prev = mst[c][i]
                log_d = jnp.where(tri, b_c - b_r + ig_r, NEG_INF)
                log_inter = b_c + m_prev
                m_t = jnp.maximum(log_inter, jnp.max(log_d, axis=-1, keepdims=True))
                m_new = m_t[L - 1:L, :]
                b_last = b_c[L - 1:L, :]
                w_end = jnp.exp(b_last - b_c + ig_c - m_new)
                cd = jnp.exp(b_last + m_prev - m_new)
                q = pt[rows(c), head(qc, i, DQK_C)] * (DQK_C ** -0.5)
                k = pt[rows(c), head(kc, i, DQK_C)]
                vv = pt[rows(c), head(vc, i, DV_C)]
                nst[c + 1].append(cd * nst[c][i] + jnp.sum(w_end * k, axis=0, keepdims=True))
                mst[c + 1].append(m_new)
                v[c].append(dict(w=jnp.exp(log_d - m_t), w_inter=jnp.exp(log_inter - m_t),
                                 floor=jnp.exp(-m_t), cd=cd, qb=q.astype(BF16), kb=k.astype(BF16),
                                 vb=vv.astype(BF16), vw=(vv * w_end).astype(BF16),
                                 qn=jnp.sum(q * nst[c][i], axis=-1, keepdims=True)))

        def stage_b(c):
            for i, d in enumerate(v[c]):
                d["sc"] = _dot_nt(d["qb"], d["kb"])
                d["upd"] = _dot_tn(d["vw"], d["kb"])
            for i, d in enumerate(v[c]):
                d["inter"] = _dot_nt(d["qb"], cst[c][i].astype(BF16))

        def stage_c(c):
            cst[c + 1] = []
            for i, d in enumerate(v[c]):
                sc = d["sc"] * d["w"]
                d["den"] = jnp.sum(sc, axis=-1, keepdims=True) + d["w_inter"] * d["qn"]
                d["sc"] = sc.astype(BF16)
                cst[c + 1].append(d["cd"] * cst[c][i] + d["upd"])

        def stage_d(c):
            for d in v[c]:
                d["num"] = _dot(d["sc"], d["vb"])

        def stage_e(c):
            for i, d in enumerate(v.pop(c)):
                num = d["num"] + d["w_inter"] * d["inter"]
                hh = num / jnp.maximum(jnp.abs(d["den"]), d["floor"])
                z = pt[rows(c), head(zc, i, DV_C)]
                ys = slice(i * DV_C, (i + 1) * DV_C)
                yc_ref[rows(c), ys] = (_head_norm(hh, gm_ref[0:1, ys]) * _silu(z)).astype(BF16)

        _chunk_pipeline(n, piece, (stage_a, stage_b, stage_c, stage_d, stage_e), gate)
        for i in range(CG):
            c_scr[i] = cst[n][i]
            n_scr[i] = nst[n][i]
            m_scr[i] = jnp.broadcast_to(mst[n][i], (8, 128))
            c_ref[0, i] = cst[n][i]
            n_ref[0, i] = nst[n][i]
            m_ref[0, i] = jnp.broadcast_to(mst[n][i], (1, 128))


def _odd_heads(hp, hs, w_t, bg, gm):
    k = hp.shape[1]
    ms = hs.shape[0]
    tile = lambda s: jnp.maximum(s - 1, 0)
    qw, vw = CG * DQK_C, CG * DV_C
    koff = H_C * DQK_C // qw
    voff = 2 * H_C * DQK_C // vw
    zoff = (2 * H_C * DQK_C + W_C) // vw
    seq = lambda g, s: (tile(s) // CPB, g, 0, 0)
    once = pl.Buffered(1)
    return pl.pallas_call(
        _odd_heads_kernel,
        grid=(H_C // CG, CN + 1),
        in_specs=[pl.BlockSpec((CT, k), lambda g, s: (tile(s), 0)),
                  pl.BlockSpec((ms, k), lambda g, s: (0, 0), pipeline_mode=once),
                  pl.BlockSpec((qw, k), lambda g, s: (g, 0), pipeline_mode=once),
                  pl.BlockSpec((qw, k), lambda g, s: (koff + g, 0), pipeline_mode=once),
                  pl.BlockSpec((vw, k), lambda g, s: (voff + g, 0), pipeline_mode=once),
                  pl.BlockSpec((vw, k), lambda g, s: (zoff + g, 0), pipeline_mode=once),
                  pl.BlockSpec((128, k), lambda g, s: (O_GATE // 128, 0), pipeline_mode=once),
                  pl.BlockSpec((1, 128), lambda g, s: (0, 0)),
                  pl.BlockSpec((1, vw), lambda g, s: (0, g))],
        out_specs=(pl.BlockSpec((CT, vw), lambda g, s: (tile(s), g)),
                   pl.BlockSpec((1, CG, DV_C, DQK_C), seq),
                   pl.BlockSpec((1, CG, 1, DQK_C), seq),
                   pl.BlockSpec((1, CG, 1, 128), seq),
                   pl.BlockSpec((ms, qw), lambda g, s: (0, g)),
                   pl.BlockSpec((ms, qw), lambda g, s: (0, g)),
                   pl.BlockSpec((ms, vw), lambda g, s: (0, g)),
                   pl.BlockSpec((ms, vw), lambda g, s: (0, g))),
        out_shape=(jax.ShapeDtypeStruct((BATCH * SEQ, W_C), BF16),
                   jax.ShapeDtypeStruct((BATCH, H_C, DV_C, DQK_C), F32),
                   jax.ShapeDtypeStruct((BATCH, H_C, 1, DQK_C), F32),
                   jax.ShapeDtypeStruct((BATCH, H_C, 1, 128), F32),
                   jax.ShapeDtypeStruct((ms, H_C * DQK_C), F32),
                   jax.ShapeDtypeStruct((ms, H_C * DQK_C), F32),
                   jax.ShapeDtypeStruct((ms, W_C), F32),
                   jax.ShapeDtypeStruct((ms, W_C), F32)),
        scratch_shapes=[pltpu.VMEM((CW, k), BF16),
                        pltpu.VMEM((CT, CW), F32),
                        pltpu.VMEM((CG, DV_C, DQK_C), F32),
                        pltpu.VMEM((CG, 1, DQK_C), F32),
                        pltpu.VMEM((CG, 8, 128), F32),
                        pltpu.VMEM((CT // CHUNK, CHUNK, CHUNK), F32)],
        compiler_params=_params(("arbitrary", "arbitrary")),
        name="odd_heads",
    )(hp, hs, w_t, w_t, w_t, w_t, w_t, bg, gm)


DT = 512


def _odd_mlp_fused_kernel(hp_ref, w0_ref, w1_ref, w2_ref, wt_ref, lng_ref, lnb_ref, ws_ref, bst_ref,
                          yd_ref, wb, pt, wsb):
    s = pl.program_id(0)
    L = CHUNK
    uc = slice(0, W_D)
    vc = slice(W_D, 2 * W_D)
    zc = slice(2 * W_D, 3 * W_D)

    @pl.when(s == 0)
    def _():
        sh = N_GATE
        wb[uc, :] = jnp.concatenate([w0_ref[sh:, :], w1_ref[0:sh, :]], axis=0).astype(BF16)
        wb[vc, :] = jnp.concatenate([w1_ref[sh:, :], w2_ref[0:sh, :]], axis=0).astype(BF16)
        wb[zc, :] = jnp.concatenate([w2_ref[sh:, :], wt_ref[...]], axis=0).astype(BF16)
        keep = (lax.broadcasted_iota(jnp.int32, (L, L), 0) >= lax.broadcasted_iota(jnp.int32, (L, L), 1))
        for g in range(G_D):
            wsb[g] = jnp.where(keep, ws_ref[g], 0.0).astype(BF16)

    @pl.when(s > 0)
    def _():
        n = DT // L
        rows = lambda c: slice(c * L, (c + 1) * L)
        grp = lambda sl, g: slice(sl.start + g * 128, sl.start + (g + 1) * 128)
        vn = {}
        mix = {}

        def project(pc):
            pt[:, pc] = _dot_nt(hp_ref[...], wb[pc, :])

        def stage_a(c):
            dv = lambda g: pt[rows(c), grp(vc, g)]
            tot = dv(0)
            for g in range(1, G_D):
                tot = tot + dv(g)
            mu = jnp.sum(tot, axis=-1, keepdims=True) * (1.0 / W_D)
            sq = (dv(0) - mu) * (dv(0) - mu)
            for g in range(1, G_D):
                sq = sq + (dv(g) - mu) * (dv(g) - mu)
            rstd = lax.rsqrt(jnp.sum(sq, axis=-1, keepdims=True) * (1.0 / W_D) + EPS)
            vn[c] = [((dv(g) - mu) * rstd * lng_ref[0:1, g * 128:(g + 1) * 128]
                      + lnb_ref[0:1, g * 128:(g + 1) * 128]).astype(BF16) for g in range(G_D)]

        def stage_b(c):
            mix[c] = [_dot(wsb[g], vn[c][g]) for g in range(G_D)]

        def stage_e(c, groups):
            for g in groups:
                sg = mix[c][g] + bst_ref[:, g:g + 1]
                d_u = pt[rows(c), grp(uc, g)]
                d_z = pt[rows(c), grp(zc, g)]
                yd_ref[rows(c), g * 128:(g + 1) * 128] = (d_u * sg * _silu(d_z)).astype(BF16)

        half = G_D // 2
        project(vc)
        for c in range(n):
            stage_a(c)
        project(zc)
        for c in range(n):
            stage_b(c)
        project(slice(uc.start, uc.start + half * 128))
        project(slice(uc.start + half * 128, uc.stop))
        for c in range(n):
            stage_e(c, range(half))
        for c in range(n):
            stage_e(c, range(half, G_D))


def _odd_mlp_fused(hp, w_t, lng, lnb, ws, bst):
    k = hp.shape[1]
    m = hp.shape[0]
    tile = lambda s: (jnp.maximum(s - 1, 0), 0)
    const2 = lambda s: (0, 0)
    t0 = O_GATE // IN_TN
    once = pl.Buffered(1)
    wspec = lambda j: pl.BlockSpec((IN_TN, k), lambda s: (t0 + j, 0), pipeline_mode=once)
    return pl.pallas_call(
        _odd_mlp_fused_kernel,
        grid=(m // DT + 1,),
        in_specs=[pl.BlockSpec((DT, k), tile),
                  wspec(0), wspec(1), wspec(2),
                  pl.BlockSpec((N_GATE, k), lambda s: ((t0 + 3) * (IN_TN // N_GATE), 0), pipeline_mode=once),
                  pl.BlockSpec((1, W_D), const2),
                  pl.BlockSpec((1, W_D), const2),
                  pl.BlockSpec((G_D, CHUNK, CHUNK), lambda s: (0, 0, 0)),
                  pl.BlockSpec((CHUNK, G_D), const2)],
        out_specs=pl.BlockSpec((DT, W_D), tile),
        out_shape=jax.ShapeDtypeStruct((m, W_D), BF16),
        scratch_shapes=[pltpu.VMEM((3 * W_D, k), BF16),
                        pltpu.VMEM((DT, 3 * W_D), F32),
                        pltpu.VMEM((G_D, CHUNK, CHUNK), BF16)],
        compiler_params=_params(("arbitrary",)),
        name="odd_mlp_fused",
    )(hp, w_t, w_t, w_t, w_t, lng, lnb, ws, bst)


def _in_proj_rows_kernel(h_ref, w_ref, wn_ref, o_ref, *, shift):
    wsh = jnp.concatenate([w_ref[shift:, :], wn_ref[...]], axis=0)
    o_ref[...] = _dot_nt(h_ref[...], wsh.astype(BF16))


def _in_proj_rows(h, w_t, n_out, shift, tile0):
    ms, k = h.shape
    return pl.pallas_call(
        functools.partial(_in_proj_rows_kernel, shift=shift),
        grid=(n_out // IN_TN,),
        in_specs=[pl.BlockSpec((ms, k), lambda j: (0, 0)),
                  pl.BlockSpec((IN_TN, k), lambda j: (j + tile0, 0)),
                  pl.BlockSpec((shift, k), lambda j: ((j + tile0 + 1) * (IN_TN // shift), 0))],
        out_specs=pl.BlockSpec((ms, IN_TN), lambda j: (0, j)),
        out_shape=jax.ShapeDtypeStruct((ms, n_out), F32),
        compiler_params=_params(("arbitrary",)),
        name="in_proj_rows",
    )(h, w_t, w_t)


def _odd_sample_kernel(pq_ref, pk_ref, pv_ref, pz_ref, h_ref, wg_ref, pd_ref,
                       c_ref, nrow_ref, mrow_ref, bg_ref, gm_ref, lng_ref, lnb_ref,
                       wt_ref, bt_ref,
                       yc_ref, yd_ref, vn_ref, co_ref, no_ref, mo_ref,
                       inter_scr, wmix_scr):
    h = pl.program_id(1)
    row = lax.broadcasted_iota(jnp.int32, (SR, SR), 0)
    col = lax.broadcasted_iota(jnp.int32, (SR, SR), 1)
    trow = row & 3

    @pl.when(jnp.logical_and(pl.program_id(0) == 0, h == 0))
    def _():
        mask_d = jnp.where((row >> 2) == (col >> 2), trow - (col & 3), -1) >= 0
        rep = jnp.where(trow == col, 1.0, 0.0).astype(BF16)
        for g in range(G_D):
            wtile = _dot_nt(_dot(rep, wt_ref[g].astype(BF16)).astype(BF16), rep)
            wmix_scr[g] = jnp.where(mask_d, wtile, 0.0).astype(BF16)

    @pl.when(h == 0)
    def _():
        dv = pd_ref[:, W_D:2 * W_D]
        mu = jnp.mean(dv, axis=-1, keepdims=True)
        xc = dv - mu
        var = jnp.mean(xc * xc, axis=-1, keepdims=True)
        rstd = lax.rsqrt(var + EPS)
        for g in range(G_D):
            sl = slice(g * 128, (g + 1) * 128)
            vn = xc[:, sl] * rstd * lng_ref[0:1, sl] + lnb_ref[0:1, sl]
            vn_ref[:, sl] = vn
            s = _dot(wmix_scr[g], vn.astype(BF16)) + bt_ref[:, g:g + 1]
            d_u = pd_ref[:, g * 128:(g + 1) * 128]
            d_z = pd_ref[:, 2 * W_D + g * 128:2 * W_D + (g + 1) * 128]
            yd_ref[:, sl] = (d_u * s * _silu(d_z)).astype(BF16)

    same = (row >> 2) == (col >> 2)
    mask = jnp.where(same, trow - (col & 3), -1) >= 0
    pre = _dot_nt(h_ref[...], wg_ref[...].astype(BF16)) + bg_ref[...]
    lf = _log_sigmoid(pre)
    b_full = _dot_hi(jnp.where(mask, 1.0, 0.0), lf)
    sub8 = lax.broadcasted_iota(jnp.int32, (8, DQK_C), 0)
    lane_b = lax.broadcasted_iota(jnp.int32, (DV_C, SR), 1) >> 2
    pre_t = pre.T
    b_full_t = b_full.T
    for hh in range(CS):
        hd = h * CS + hh
        qc = slice(hh * DQK_C, (hh + 1) * DQK_C)
        vc = slice(hh * DV_C, (hh + 1) * DV_C)
        sel_i = col == hd
        sel_f = col == hd + H_C
        ig_c = jnp.sum(jnp.where(sel_i, pre, 0.0), axis=-1, keepdims=True)
        b_c = jnp.sum(jnp.where(sel_f, b_full, 0.0), axis=-1, keepdims=True)
        sel_ir = row == hd
        sel_fr = row == hd + H_C
        ig_r = jnp.sum(jnp.where(sel_ir, pre_t, 0.0), axis=0, keepdims=True)
        b_r = jnp.sum(jnp.where(sel_fr, b_full_t, 0.0), axis=0, keepdims=True)
        m_prev = mrow_ref[hh]
        log_d = jnp.where(mask, b_c - b_r + ig_r, NEG_INF)
        log_inter = b_c + m_prev
        m_t = jnp.maximum(log_inter, jnp.max(log_d, axis=-1, keepdims=True))
        w = jnp.exp(log_d - m_t)
        w_inter = jnp.exp(log_inter - m_t)
        q = pq_ref[:, qc] * (DQK_C ** -0.5)
        k = pk_ref[:, qc]
        v = pv_ref[:, vc]
        qb = q.astype(BF16)
        kb = k.astype(BF16)
        vb = v.astype(BF16)
        sc = _dot_nt(qb, kb) * w
        for g in range(SR // 8):
            q8 = q[8 * g:8 * g + 8, :]
            q2 = jnp.concatenate([jnp.where(sub8 < DEC_SEQ, q8, 0.0), jnp.where(sub8 < DEC_SEQ, 0.0, q8)], axis=1)
            c_pair = jnp.concatenate([c_ref[2 * g + beta, hh].astype(BF16) for beta in range(2)], axis=1)
            inter_scr[8 * g:8 * g + 8, :] = _dot_nt(q2.astype(BF16), c_pair)
        n_rows = nrow_ref[hh]
        num = _dot(sc.astype(BF16), vb) + w_inter * inter_scr[...]
        den = jnp.sum(sc, axis=-1, keepdims=True) + w_inter * jnp.sum(q * n_rows, axis=-1, keepdims=True)
        hv = num / jnp.maximum(jnp.abs(den), jnp.exp(-m_t))
        yc_ref[:, vc] = (_head_norm(hv, gm_ref[:, vc]) * _silu(pz_ref[:, vc])).astype(BF16)

        stats = jnp.where(col == 0, m_t, jnp.where(col == 1, b_c, 0.0))
        last = _dot_hi(jnp.where(col == (row | 3), 1.0, 0.0), stats)
        m_new = last[:, 0:1]
        b_last = last[:, 1:2]
        w_end = jnp.exp(b_last - b_c + ig_c - m_new)
        cd = jnp.exp(b_last + m_prev - m_new)
        mo_ref[hh] = m_new
        no_ref[hh] = cd * n_rows + _dot_hi(jnp.where(same, 1.0, 0.0), w_end * k)
        vwt = (v * w_end).T
        for b in range(SB):
            lhs = jnp.where(lane_b == b, vwt, 0.0).astype(BF16)
            cd_b = cd[4 * b + 3:4 * b + 4, :]
            co_ref[b, hh] = cd_b * c_ref[b, hh] + _dot(lhs, kb)


def _odd_sample(pc, pd, h, w_o, c_state, n_rows, m_rows, bg, gm, lng, lnb, wt, bt):
    nb = DEC_BATCH // SB
    const2 = lambda i, h: (0, 0)
    return pl.pallas_call(
        _odd_sample_kernel,
        grid=(nb, H_C // CS),
        in_specs=[pl.BlockSpec((SR, CS * DQK_C), lambda i, h: (i, h)),
                  pl.BlockSpec((SR, CS * DQK_C), lambda i, h: (i, h)),
                  pl.BlockSpec((SR, CS * DV_C), lambda i, h: (i, h)),
                  pl.BlockSpec((SR, CS * DV_C), lambda i, h: (i, h)),
                  pl.BlockSpec((SR, D_MODEL), lambda i, h: (i, 0)),
                  pl.BlockSpec((128, D_MODEL), lambda i, h: (O_GATE // 128, 0)),
                  pl.BlockSpec((SR, 3 * W_D), lambda i, h: (i, 0)),
                  pl.BlockSpec((SB, CS, DV_C, DQK_C), lambda i, h: (i, h, 0, 0)),
                  pl.BlockSpec((CS, SR, DQK_C), lambda i, h: (h, i, 0)),
                  pl.BlockSpec((CS, SR, 1), lambda i, h: (h, i, 0)),
                  pl.BlockSpec((1, 128), const2),
                  pl.BlockSpec((1, CS * DV_C), lambda i, h: (0, h)),
                  pl.BlockSpec((1, W_D), const2),
                  pl.BlockSpec((1, W_D), const2),
                  pl.BlockSpec((G_D, SR, SR), lambda i, h: (0, 0, 0)),
                  pl.BlockSpec((SR, G_D), const2)],
        out_specs=(pl.BlockSpec((SR, CS * DV_C), lambda i, h: (i, h)),
                   pl.BlockSpec((SR, W_D), lambda i, h: (i, 0)),
                   pl.BlockSpec((SR, W_D), lambda i, h: (i, 0)),
                   pl.BlockSpec((SB, CS, DV_C, DQK_C), lambda i, h: (i, h, 0, 0)),
                   pl.BlockSpec((CS, SR, DQK_C), lambda i, h: (h, i, 0)),
                   pl.BlockSpec((CS, SR, 1), lambda i, h: (h, i, 0))),
        out_shape=(jax.ShapeDtypeStruct((DEC_BATCH * DEC_SEQ, W_C), BF16),
                   jax.ShapeDtypeStruct((DEC_BATCH * DEC_SEQ, W_D), BF16),
                   jax.ShapeDtypeStruct((DEC_BATCH * DEC_SEQ, W_D), F32),
                   jax.ShapeDtypeStruct((DEC_BATCH, H_C, DV_C, DQK_C), F32),
                   jax.ShapeDtypeStruct((H_C, DEC_BATCH * DEC_SEQ, DQK_C), F32),
                   jax.ShapeDtypeStruct((H_C, DEC_BATCH * DEC_SEQ, 1), F32)),
        scratch_shapes=[pltpu.VMEM((SR, DV_C), F32),
                        pltpu.VMEM((G_D, SR, SR), BF16)],
        compiler_params=_params(("arbitrary", "arbitrary")),
        name="odd_sample",
    )(*pc, h, w_o, pd, c_state, n_rows, m_rows, bg, gm, lng, lnb, wt, bt)


def _rope_tables(pos):
    inv = ROPE_BASE ** (-jnp.arange(0, DH_B, 2, dtype=F32) / DH_B)
    ang = pos.astype(F32)[:, None] * inv[None, :]
    cos = jnp.cos(ang)
    sin = jnp.sin(ang)
    return jnp.concatenate([cos, cos], axis=-1), jnp.concatenate([-sin, sin], axis=-1)


def kernel(x_prompt, x_sample, state_conv, state_ret, state_mlstm_C, state_mlstm_n, state_mlstm_m,
           norm_even, w_in_even, conv_w, ret_norm, w_out_even,
           norm_odd, w_in_odd, b_gate_odd, mlstm_norm, ln_v_g, ln_v_b,
           w_spatial, b_spatial, w_out_odd, norm_final):
    w_in_e = w_in_even[0]
    w_out_e = w_out_even[0].astype(BF16)
    w_o = w_in_odd[0].T
    w_out_o = w_out_odd[0].astype(BF16)
    g_even = norm_even[0][None, :]
    g_odd = norm_odd[0][None, :]
    g_fin = norm_final[None, :]
    cw = conv_w[0]
    g_ret = ret_norm[0][None, :]
    bg = jnp.concatenate([b_gate_odd[0], jnp.zeros((128 - 2 * H_C,), F32)])[None, :]
    gm = mlstm_norm[0][None, :]
    lng = ln_v_g[0][None, :]
    lnb = ln_v_b[0][None, :]
    ws = w_spatial[0]
    bst = b_spatial[0].T

    cos_p, sin_p = _rope_tables(jnp.arange(SEQ, dtype=jnp.int32))
    cos_s, sin_s = _rope_tables(PAST_LEN + jnp.arange(DEC_SEQ, dtype=jnp.int32))
    cos_s = jnp.tile(cos_s, (SB, 1))
    sin_s = jnp.tile(sin_s, (SB, 1))
    lg_tab = jnp.broadcast_to(jnp.asarray(LOG_GAMMA, F32)[:, None, None], (H_B, 1, 128))

    bt_s = jnp.tile(b_spatial[0][:, :DEC_SEQ].T, (SB, 1))

    rs = DEC_BATCH * DEC_SEQ
    xp = x_prompt.reshape(BATCH * SEQ, D_MODEL)
    xs = x_sample.reshape(rs, D_MODEL)
    hp, hs = _norm_cast(xp, xs, g_even, 2048)
    ya, conv_p, *ps_a = _even_conv(hp, hs, w_in_e, cw)
    yb, ret_p, *ps_b = _even_heads(hp, hs, w_in_e, g_ret, cos_p, sin_p, lg_tab)
    st_exp = jnp.pad(state_conv[0], ((0, 0), (0, DEC_SEQ - (CONV_W - 1)), (0, 0))).reshape(rs, W_A)
    ya_s, u_s, yb_s, ret_s = _even_sample(ps_a, ps_b, st_exp, state_ret[0], cw, g_ret, cos_s, sin_s, lg_tab)
    x1, h1, x1s, h1s = _outproj(ya, yb, xp, ya_s, yb_s, xs, w_out_e, g_odd, final=False)

    yc, c_p, n_p, m_p, *ps_c = _odd_heads(h1, h1s, w_o, bg, gm)
    yd = _odd_mlp_fused(h1, w_o, lng, lnb, ws, bst)
    pd_s = _in_proj_rows(h1s, w_o, 3 * W_D, N_GATE, O_GATE // IN_TN)
    n_rows = jnp.repeat(jnp.transpose(state_mlstm_n[0], (1, 0, 2)), DEC_SEQ, axis=1)
    m_rows = jnp.repeat(state_mlstm_m[0].T, DEC_SEQ, axis=1)[:, :, None]
    yc_s, yd_s, vn_s, c_s, no_s, mo_s = _odd_sample(
        ps_c, pd_s, h1s, w_o, state_mlstm_C[0], n_rows, m_rows, bg, gm, lng, lnb, ws, bt_s)
    y_prompt, y_sample = _outproj(yc, yd, x1, yc_s, yd_s, x1s, w_out_o, g_fin, final=True)

    conv_s = u_s.reshape(DEC_BATCH, DEC_SEQ, W_A)[:, DEC_SEQ - (CONV_W - 1):, :]
    n_s = jnp.transpose(no_s[:, DEC_SEQ - 1::DEC_SEQ, :], (1, 0, 2))
    m_s = mo_s[:, DEC_SEQ - 1::DEC_SEQ, 0].T
    return (y_prompt.reshape(BATCH, SEQ, D_MODEL),
            y_sample.reshape(DEC_BATCH, DEC_SEQ, D_MODEL),
            conv_p[None], conv_s[None],
            ret_p[None], ret_s[None],
            c_p[None], c_s[None],
            n_p[:, :, 0, :][None], n_s[None],
            m_p[:, :, 0, 0][None], m_s[None],
            vn_s.reshape(DEC_BATCH, DEC_SEQ, W_D)[None])
```

```python
import functools
import math

import jax
import jax.numpy as jnp
from jax import lax
from jax.experimental import pallas as pl
from jax.experimental.pallas import tpu as pltpu

F32 = jnp.float32
BF16 = jnp.bfloat16

D_MODEL = 2048
BATCH = 4
SEQ = 2048
DEC_BATCH = 128
DEC_SEQ = 4
PAST_LEN = 16384
W_A = 1024
CONV_W = 3
W_B = 1024
H_B = 8
DH_B = 128
W_C = 1024
H_C = 4
DV_C = 256
DQK_C = 128
W_D = 1024
G_D = 8
CHUNK = 128
O_GATE = 2 * H_C * DQK_C + 2 * W_C
N_GATE = 2 * H_C
ROPE_BASE = 10000.0
EPS = 1e-6
LOG_GAMMA = tuple(math.log(1.0 - 2.0 ** (-5.0 - h)) for h in range(H_B))
NEG_INF = float("-inf")
VMEM_LIMIT = 56 * 1024 * 1024

NT_DIMS = (((1,), (1,)), ((), ()))
TN_DIMS = (((0,), (0,)), ((), ()))


def _silu(z):
    return z * (1.0 / (1.0 + jnp.exp(-z)))


def _log_sigmoid(x):
    return jnp.minimum(x, 0.0) - jnp.log1p(jnp.exp(-jnp.abs(x)))


def _dot(a, b):
    return jnp.dot(a, b, preferred_element_type=F32)


def _dot_nt(a, b):
    return lax.dot_general(a, b, NT_DIMS, preferred_element_type=F32)


def _dot_tn(a, b):
    return lax.dot_general(a, b, TN_DIMS, preferred_element_type=F32)


def _dot_hi(a, b):
    return jnp.dot(a, b, preferred_element_type=F32, precision=lax.Precision.HIGHEST)


def _head_norm(o, g):
    mu = jnp.mean(o, axis=-1, keepdims=True)
    oc = o - mu
    var = jnp.mean(oc * oc, axis=-1, keepdims=True)
    return oc * lax.rsqrt(var + EPS) * g


def _params(sem):
    return pltpu.CompilerParams(dimension_semantics=sem, vmem_limit_bytes=VMEM_LIMIT)


def _norm_cast_kernel(x_ref, xs_ref, g_ref, h_ref, hs_ref, *, n_prompt):
    i = pl.program_id(0)

    def tile(x_in, h_out):
        x = x_in[...]
        ms = jnp.mean(x * x, axis=-1, keepdims=True)
        h_out[...] = (x * lax.rsqrt(ms + EPS) * g_ref[...]).astype(BF16)

    @pl.when(i < n_prompt)
    def _():
        tile(x_ref, h_ref)

    @pl.when(i == n_prompt)
    def _():
        tile(xs_ref, hs_ref)


def _norm_cast(x, xs, g, tm):
    m, d = x.shape
    ms = xs.shape[0]
    n_prompt = m // tm
    row = lambda i: (jnp.minimum(i, n_prompt - 1), 0)
    const = lambda i: (0, 0)
    return pl.pallas_call(
        functools.partial(_norm_cast_kernel, n_prompt=n_prompt),
        grid=(n_prompt + 1,),
        in_specs=[pl.BlockSpec((tm, d), row),
                  pl.BlockSpec((ms, d), const),
                  pl.BlockSpec((1, d), const)],
        out_specs=(pl.BlockSpec((tm, d), row), pl.BlockSpec((ms, d), const)),
        out_shape=(jax.ShapeDtypeStruct((m, d), BF16), jax.ShapeDtypeStruct((ms, d), BF16)),
        compiler_params=_params(("arbitrary",)),
        name="norm_cast",
    )(x, xs, g)


IN_TN = 1024


OUT_TM = 512


def _outproj_kernel(ya_ref, yb_ref, x_ref, yas_ref, ybs_ref, xs_ref, w_ref, g_ref, *out_refs,
                    final, n_prompt):
    i = pl.program_id(0)
    half = ya_ref.shape[1]
    n_out = 1 if final else 2

    def tile(ya, yb, x, outs):
        acc = _dot(ya[...], w_ref[0:half, :]) + _dot(yb[...], w_ref[half:2 * half, :])
        x1 = x[...] + acc
        ms = jnp.mean(x1 * x1, axis=-1, keepdims=True)
        hn = x1 * lax.rsqrt(ms + EPS) * g_ref[...]
        if final:
            outs[0][...] = hn
        else:
            outs[0][...] = x1
            outs[1][...] = hn.astype(BF16)

    @pl.when(i < n_prompt)
    def _():
        tile(ya_ref, yb_ref, x_ref, out_refs[:n_out])

    @pl.when(i == n_prompt)
    def _():
        tile(yas_ref, ybs_ref, xs_ref, out_refs[n_out:])


def _outproj(ya, yb, x, ya_s, yb_s, x_s, w, g, final):
    m, half = ya.shape
    ms = ya_s.shape[0]
    d = w.shape[1]
    n_prompt = m // OUT_TM
    row = lambda i: (jnp.minimum(i, n_prompt - 1), 0)
    const = lambda i: (0, 0)
    once = pl.Buffered(1)
    shapes = [jax.ShapeDtypeStruct((m, d), F32), jax.ShapeDtypeStruct((ms, d), F32)]
    specs = [pl.BlockSpec((OUT_TM, d), row), pl.BlockSpec((ms, d), const)]
    if not final:
        shapes = [shapes[0], jax.ShapeDtypeStruct((m, d), BF16), shapes[1], jax.ShapeDtypeStruct((ms, d), BF16)]
        specs = [specs[0], pl.BlockSpec((OUT_TM, d), row), specs[1], pl.BlockSpec((ms, d), const)]
    return pl.pallas_call(
        functools.partial(_outproj_kernel, final=final, n_prompt=n_prompt),
        grid=(n_prompt + 1,),
        in_specs=[pl.BlockSpec((OUT_TM, half), row),
                  pl.BlockSpec((OUT_TM, half), row),
                  pl.BlockSpec((OUT_TM, d), row),
                  pl.BlockSpec((ms, half), const, pipeline_mode=once),
                  pl.BlockSpec((ms, half), const, pipeline_mode=once),
                  pl.BlockSpec((ms, d), const, pipeline_mode=once),
                  pl.BlockSpec((2 * half, d), const, pipeline_mode=once),
                  pl.BlockSpec((1, d), const)],
        out_specs=tuple(specs),
        out_shape=tuple(shapes),
        compiler_params=_params(("arbitrary",)),
        name="out_proj_final" if final else "out_proj",
    )(ya, yb, x, ya_s, yb_s, x_s, w, g)


def _rope(x, cosf, sins):
    return x * cosf + pltpu.roll(x, DH_B // 2, 1) * sins


FT = 1024
FN = BATCH * SEQ // FT
FPB = SEQ // FT
HG = 2
GW = HG * 128
PCH = 2


def _chunk_pipeline(n, piece, stages, gate=None):
    sa, sb, sc, sd, se = stages
    npieces = n // PCH
    for j in range(PCH):
        piece(0, j)
    if gate is not None:
        gate(0)
    for c in range(n + 2):
        k, j = c // PCH + 1, c % PCH
        if k < npieces:
            piece(k, j)
        if c < n:
            sa(c)
        if 1 <= c <= n:
            sc(c - 1)
        if c < n:
            sb(c)
        if 1 <= c <= n:
            sd(c - 1)
        if c >= 2:
            se(c - 2)
        if gate is not None and j == PCH - 1 and k < npieces:
            gate(k)


def _even_heads_kernel(hp_ref, hs_ref, wq_ref, wk_ref, wv_ref, wz_ref, gret_ref, cos_ref, sin_ref, lg_ref,
                       yb_ref, s_ref, sq_ref, sk_ref, sv_ref, sz_ref, wb, pt, s_scr):
    s = pl.program_id(1)
    L = CHUNK

    @pl.when(s == 0)
    def _():
        for part, w_ref in enumerate((wq_ref, wk_ref, wv_ref, wz_ref)):
            wb[:, part * GW:(part + 1) * GW] = w_ref[...].astype(BF16)
        ps = _dot(hs_ref[...], wb[...])
        for part, o_ref in enumerate((sq_ref, sk_ref, sv_ref, sz_ref)):
            o_ref[...] = ps[:, part * GW:(part + 1) * GW]

    @pl.when(s > 0)
    def _():
        t = s - 1
        n = FT // L
        rows = lambda c: slice(c * L, (c + 1) * L)
        cols = lambda part, i: slice(part * GW + i * DH_B, part * GW + (i + 1) * DH_B)

        def piece(k, j):
            pr = slice(k * PCH * L, (k + 1) * PCH * L)
            pc = slice(j * 2 * GW, (j + 1) * 2 * GW)
            pt[pr, pc] = _dot(hp_ref[pr, :], wb[:, pc])

        row = lax.broadcasted_iota(jnp.int32, (L, L), 0)
        col = lax.broadcasted_iota(jnp.int32, (L, L), 1)
        causal = row >= col
        diff = jnp.maximum(row - col, 0).astype(F32)
        ti = lax.broadcasted_iota(jnp.int32, (L, 1), 0).astype(F32)
        lgs = [lg_ref[i][:, 0:1] for i in range(HG)]
        decay = [jnp.where(causal, jnp.exp(lg * diff), 0.0) for lg in lgs]
        q_decay = [jnp.exp(lg * (ti + 1.0)) for lg in lgs]
        k_decay = [jnp.exp(lg * (L - 1.0 - ti)) for lg in lgs]
        gamma_l = [jnp.exp(lg * float(L)) for lg in lgs]
        state = {0: [jnp.where(t % FPB == 0, 0.0, s_scr[i]) for i in range(HG)]}
        v = {}

        def stage_a(c):
            cosf = cos_ref[rows(c), :]
            sins = sin_ref[rows(c), :]
            v[c] = []
            for i in range(HG):
                kr = _rope(pt[rows(c), cols(1, i)], cosf, sins) * (DH_B ** -0.5)
                v[c].append(dict(qb=_rope(pt[rows(c), cols(0, i)], cosf, sins).astype(BF16),
                                 kb=kr.astype(BF16),
                                 kd=(kr * k_decay[i]).astype(BF16),
                                 vb=pt[rows(c), cols(2, i)].astype(BF16)))

        def stage_b(c):
            for i, d in enumerate(v[c]):
                d["sc"] = _dot_nt(d["qb"], d["kb"])
                d["upd"] = _dot_tn(d["kd"], d["vb"])
            for i, d in enumerate(v[c]):
                d["cross"] = _dot(d["qb"], state[c][i].astype(BF16))

        def stage_c(c):
            state[c + 1] = []
            for i, d in enumerate(v[c]):
                d["sc"] = (d["sc"] * decay[i]).astype(BF16)
                state[c + 1].append(gamma_l[i] * state[c][i] + d["upd"])

        def stage_d(c):
            for d in v[c]:
                d["inner"] = _dot(d["sc"], d["vb"])

        def stage_e(c):
            for i, d in enumerate(v.pop(c)):
                o = d["inner"] + d["cross"] * q_decay[i]
                g = gret_ref[0:1, i * DH_B:(i + 1) * DH_B]
                z = pt[rows(c), cols(3, i)]
                yb_ref[rows(c), i * DH_B:(i + 1) * DH_B] = (_head_norm(o, g) * _silu(z)).astype(BF16)

        _chunk_pipeline(n, piece, (stage_a, stage_b, stage_c, stage_d, stage_e))
        for i in range(HG):
            s_scr[i] = state[n][i]
            s_ref[0, i] = state[n][i]


def _even_heads(hp, hs, w, g_ret, cosf, sins, lg_tab):
    k = hp.shape[1]
    ms = hs.shape[0]
    ng = H_B // HG
    base = 4 * W_A // GW
    tile = lambda s: jnp.maximum(s - 1, 0)
    wspec = lambda part: pl.BlockSpec((k, GW), lambda g, s: (0, base + part * ng + g))
    sspec = pl.BlockSpec((ms, GW), lambda g, s: (0, g))
    sshape = jax.ShapeDtypeStruct((ms, W_B), F32)
    return pl.pallas_call(
        _even_heads_kernel,
        grid=(ng, FN + 1),
        in_specs=[pl.BlockSpec((FT, k), lambda g, s: (tile(s), 0)),
                  pl.BlockSpec((ms, k), lambda g, s: (0, 0)),
                  wspec(0), wspec(1), wspec(2), wspec(3),
                  pl.BlockSpec((1, GW), lambda g, s: (0, g)),
                  pl.BlockSpec((FT, DH_B), lambda g, s: (tile(s) % FPB, 0)),
                  pl.BlockSpec((FT, DH_B), lambda g, s: (tile(s) % FPB, 0)),
                  pl.BlockSpec((HG, 1, 128), lambda g, s: (g, 0, 0))],
        out_specs=(pl.BlockSpec((FT, GW), lambda g, s: (tile(s), g)),
                   pl.BlockSpec((1, HG, DH_B, DH_B), lambda g, s: (tile(s) // FPB, g, 0, 0)),
                   sspec, sspec, sspec, sspec),
        out_shape=(jax.ShapeDtypeStruct((BATCH * SEQ, W_B), BF16),
                   jax.ShapeDtypeStruct((BATCH, H_B, DH_B, DH_B), F32),
                   sshape, sshape, sshape, sshape),
        scratch_shapes=[pltpu.VMEM((k, 4 * GW), BF16),
                        pltpu.VMEM((FT, 4 * GW), F32),
                        pltpu.VMEM((HG, DH_B, DH_B), F32)],
        compiler_params=_params(("arbitrary", "arbitrary")),
        name="even_heads",
    )(hp, hs, w, w, w, w, g_ret, cosf, sins, lg_tab)


def _even_conv_kernel(hp_ref, hs_ref, wb_ref, wc_ref, wx_ref, wz_ref, cw_ref,
                      ya_ref, conv_ref, sb_ref, sc_ref, sx_ref, sz_ref, wb, pt, ubuf):
    s = pl.program_id(1)
    L = CHUNK
    sub = lambda j: slice(j * 4 * 128, (j + 1) * 4 * 128)
    cs = lambda j, p: slice(j * 4 * 128 + p * 128, j * 4 * 128 + (p + 1) * 128)

    @pl.when(s == 0)
    def _():
        for p, w_ref in enumerate((wb_ref, wc_ref, wx_ref, wz_ref)):
            for j in range(HG):
                wb[:, cs(j, p)] = w_ref[:, j * 128:(j + 1) * 128].astype(BF16)
        ps = _dot(hs_ref[...], wb[...])
        for p, o_ref in enumerate((sb_ref, sc_ref, sx_ref, sz_ref)):
            for j in range(HG):
                o_ref[:, j * 128:(j + 1) * 128] = ps[:, cs(j, p)]

    @pl.when(s > 0)
    def _():
        t = s - 1
        n = FT // L
        rows = lambda c: slice(c * L, (c + 1) * L)

        @pl.when(t % FPB == 0)
        def _():
            ubuf[0:8, :] = jnp.zeros((8, GW), F32)

        @pl.when(t % FPB != 0)
        def _():
            ubuf[0:8, :] = ubuf[FT:FT + 8, :]

        def project(j):
            pt[:, sub(j)] = _dot(hp_ref[...], wb[:, sub(j)])

        def mix(j):
            ch = slice(j * 128, (j + 1) * 128)
            for c in range(n):
                u = pt[rows(c), cs(j, 1)] * pt[rows(c), cs(j, 2)]
                ubuf[8 + c * L:8 + (c + 1) * L, ch] = u
                t0 = ubuf[6 + c * L:6 + (c + 1) * L, ch]
                t1 = ubuf[7 + c * L:7 + (c + 1) * L, ch]
                conv = cw_ref[0:1, ch] * t0 + cw_ref[1:2, ch] * t1 + cw_ref[2:3, ch] * u
                ya_ref[rows(c), ch] = (pt[rows(c), cs(j, 0)] * conv * _silu(pt[rows(c), cs(j, 3)])).astype(BF16)

        project(0)
        for j in range(HG):
            if j + 1 < HG:
                project(j + 1)
            mix(j)
        conv_ref[0] = ubuf[FT + 6:FT + 8, :]


def _even_conv(hp, hs, w, conv_w):
    k = hp.shape[1]
    ms = hs.shape[0]
    ng = W_A // GW
    tile = lambda s: jnp.maximum(s - 1, 0)
    wspec = lambda part: pl.BlockSpec((k, GW), lambda g, s: (0, part * ng + g))
    sspec = pl.BlockSpec((ms, GW), lambda g, s: (0, g))
    sshape = jax.ShapeDtypeStruct((ms, W_A), F32)
    return pl.pallas_call(
        _even_conv_kernel,
        grid=(ng, FN + 1),
        in_specs=[pl.BlockSpec((FT, k), lambda g, s: (tile(s), 0)),
                  pl.BlockSpec((ms, k), lambda g, s: (0, 0)),
                  wspec(0), wspec(1), wspec(2), wspec(3),
                  pl.BlockSpec((CONV_W, GW), lambda g, s: (0, g))],
        out_specs=(pl.BlockSpec((FT, GW), lambda g, s: (tile(s), g)),
                   pl.BlockSpec((1, CONV_W - 1, GW), lambda g, s: (tile(s) // FPB, 0, g)),
                   sspec, sspec, sspec, sspec),
        out_shape=(jax.ShapeDtypeStruct((BATCH * SEQ, W_A), BF16),
                   jax.ShapeDtypeStruct((BATCH, CONV_W - 1, W_A), F32),
                   sshape, sshape, sshape, sshape),
        scratch_shapes=[pltpu.VMEM((k, 4 * GW), BF16),
                        pltpu.VMEM((FT, 4 * GW), F32),
                        pltpu.VMEM((FT + 8, GW), F32)],
        compiler_params=_params(("arbitrary", "arbitrary")),
        name="even_conv",
    )(hp, hs, w, w, w, w, conv_w)


SB = 32
SR = SB * DEC_SEQ
HS = 4
CS = 2


def _even_sample_kernel(ab_ref, ac_ref, ax_ref, az_ref, pq_ref, pk_ref, pv_ref, pz_ref, st_ref, s_ref,
                        cw_ref, gret_ref, cos_ref, sin_ref, lg_ref,
                        ya_ref, u_ref, yb_ref, so_ref, cross_scr):
    h = pl.program_id(1)
    row = lax.broadcasted_iota(jnp.int32, (SR, SR), 0)
    col = lax.broadcasted_iota(jnp.int32, (SR, SR), 1)
    trow = row & 3

    @pl.when(h == 0)
    def _():
        for j in range(W_A // 128):
            sl = slice(j * 128, (j + 1) * 128)
            a_b = ab_ref[:, sl]
            a_c = ac_ref[:, sl]
            a_x = ax_ref[:, sl]
            a_z = az_ref[:, sl]
            u = a_c * a_x
            e = st_ref[:, sl]
            tap1 = jnp.where(trow >= 1, pltpu.roll(u, 1, 0), pltpu.roll(e, SR - 1, 0))
            tap0 = jnp.where(trow >= 2, pltpu.roll(u, 2, 0), e)
            conv = cw_ref[0:1, sl] * tap0 + cw_ref[1:2, sl] * tap1 + cw_ref[2:3, sl] * u
            ya_ref[:, sl] = (a_b * conv * _silu(a_z)).astype(BF16)
            u_ref[:, sl] = u

    same = (row >> 2) == (col >> 2)
    dd = trow - (col & 3)
    mask = jnp.where(same, dd, -1) >= 0
    tcol = (lax.broadcasted_iota(jnp.int32, (SR, 1), 0) & 3).astype(F32)
    cosf = cos_ref[...]
    sins = sin_ref[...]
    lane_b = col >> 2
    sub = lax.broadcasted_iota(jnp.int32, (8, DH_B), 0)
    for hh in range(HS):
        hc = slice(hh * DH_B, (hh + 1) * DH_B)
        lg = lg_ref[hh][:, 0:1]
        decay = jnp.where(mask, jnp.exp(lg * jnp.maximum(dd, 0).astype(F32)), 0.0)
        qr = _rope(pq_ref[:, hc], cosf, sins)
        kr = _rope(pk_ref[:, hc], cosf, sins) * (DH_B ** -0.5)
        qb = qr.astype(BF16)
        kb = kr.astype(BF16)
        vb = pv_ref[:, hc].astype(BF16)
        sc = _dot_nt(qb, kb) * decay
        inner = _dot(sc.astype(BF16), vb)
        kdt = (kr * jnp.exp(lg * (DEC_SEQ - 1.0 - tcol))).T
        gamma_l = jnp.exp(lg * float(DEC_SEQ))
        for g in range(SR // 8):
            q8 = qr[8 * g:8 * g + 8, :]
            q2 = jnp.concatenate([jnp.where(sub < DEC_SEQ, q8, 0.0), jnp.where(sub < DEC_SEQ, 0.0, q8)], axis=1)
            s_pair = [s_ref[2 * g + beta, hh] for beta in range(2)]
            cross_scr[hh, 8 * g:8 * g + 8, :] = _dot(
                q2.astype(BF16), jnp.concatenate([sp.astype(BF16) for sp in s_pair], axis=0))
            for beta in range(2):
                b = 2 * g + beta
                lhs = jnp.where(lane_b == b, kdt, 0.0).astype(BF16)
                so_ref[b, hh] = gamma_l * s_pair[beta] + _dot(lhs, vb)
        o = inner + cross_scr[hh] * jnp.exp(lg * (tcol + 1.0))
        yb_ref[:, hc] = (_head_norm(o, gret_ref[:, hc]) * _silu(pz_ref[:, hc])).astype(BF16)


def _even_sample(pa, pb, st_exp, s_state, conv_w, g_ret, cosf, sins, lg_tab):
    nb = DEC_BATCH // SB
    const2 = lambda i, h: (0, 0)
    aspec = pl.BlockSpec((SR, W_A), lambda i, h: (i, 0))
    hspec = pl.BlockSpec((SR, HS * DH_B), lambda i, h: (i, h))
    return pl.pallas_call(
        _even_sample_kernel,
        grid=(nb, H_B // HS),
        in_specs=[aspec, aspec, aspec, aspec,
                  hspec, hspec, hspec, hspec,
                  pl.BlockSpec((SR, W_A), lambda i, h: (i, 0)),
                  pl.BlockSpec((SB, HS, DH_B, DH_B), lambda i, h: (i, h, 0, 0)),
                  pl.BlockSpec((CONV_W, W_A), const2),
                  pl.BlockSpec((1, HS * DH_B), lambda i, h: (0, h)),
                  pl.BlockSpec((SR, DH_B), const2),
                  pl.BlockSpec((SR, DH_B), const2),
                  pl.BlockSpec((HS, 1, 128), lambda i, h: (h, 0, 0))],
        out_specs=(pl.BlockSpec((SR, W_A), lambda i, h: (i, 0)),
                   pl.BlockSpec((SR, W_A), lambda i, h: (i, 0)),
                   pl.BlockSpec((SR, HS * DH_B), lambda i, h: (i, h)),
                   pl.BlockSpec((SB, HS, DH_B, DH_B), lambda i, h: (i, h, 0, 0))),
        out_shape=(jax.ShapeDtypeStruct((DEC_BATCH * DEC_SEQ, W_A), BF16),
                   jax.ShapeDtypeStruct((DEC_BATCH * DEC_SEQ, W_A), F32),
                   jax.ShapeDtypeStruct((DEC_BATCH * DEC_SEQ, W_B), BF16),
                   jax.ShapeDtypeStruct((DEC_BATCH, H_B, DH_B, DH_B), F32)),
        scratch_shapes=[pltpu.VMEM((HS, SR, DH_B), F32)],
        compiler_params=_params(("arbitrary", "arbitrary")),
        name="even_sample",
    )(*pa, *pb, st_exp, s_state, conv_w, g_ret, cosf, sins, lg_tab)


CG = 2
CW = 128 + CG * (2 * DQK_C + 2 * DV_C)
CT = 1024
CN = BATCH * SEQ // CT
CPB = SEQ // CT


def _odd_heads_kernel(hp_ref, hs_ref, wq_ref, wk_ref, wv_ref, wz_ref, wg_ref, bg_ref, gm_ref,
                      yc_ref, c_ref, n_ref, m_ref, sq_ref, sk_ref, sv_ref, sz_ref,
                      wb, pt, c_scr, n_scr, m_scr, gt_scr):
    grp = pl.program_id(0)
    s = pl.program_id(1)
    L = CHUNK
    gc = slice(0, 128)
    qc = slice(128, 128 + CG * DQK_C)
    kc = slice(qc.stop, qc.stop + CG * DQK_C)
    vc = slice(kc.stop, kc.stop + CG * DV_C)
    zc = slice(vc.stop, vc.stop + CG * DV_C)
    head = lambda sl, i, w: slice(sl.start + i * w, sl.start + (i + 1) * w)

    @pl.when(s == 0)
    def _():
        wb[qc, :] = wq_ref[...].astype(BF16)
        wb[kc, :] = wk_ref[...].astype(BF16)
        wb[vc, :] = wv_ref[...].astype(BF16)
        wb[zc, :] = wz_ref[...].astype(BF16)
        wb[gc, :] = wg_ref[...].astype(BF16)
        ps = _dot_nt(hs_ref[...], wb[qc.start:CW, :])
        off = lambda sl: slice(sl.start - qc.start, sl.stop - qc.start)
        sq_ref[...] = ps[:, off(qc)]
        sk_ref[...] = ps[:, off(kc)]
        sv_ref[...] = ps[:, off(vc)]
        sz_ref[...] = ps[:, off(zc)]

    @pl.when(s > 0)
    def _():
        t = s - 1
        n = CT // L
        rows = lambda c: slice(c * L, (c + 1) * L)

        def piece(k, j):
            pr = slice(k * PCH * L, (k + 1) * PCH * L)
            mid = vc.start + DV_C
            pc = (slice(0, mid), slice(mid, CW))[j]
            pt[pr, pc] = _dot_nt(hp_ref[pr, :], wb[pc, :])

        row = lax.broadcasted_iota(jnp.int32, (L, L), 0)
        col = lax.broadcasted_iota(jnp.int32, (L, L), 1)
        tri = row >= col
        fresh = t % CPB == 0
        cst = {0: [jnp.where(fresh, 0.0, c_scr[i]) for i in range(CG)]}
        nst = {0: [jnp.where(fresh, 0.0, n_scr[i]) for i in range(CG)]}
        mst = {0: [jnp.where(fresh, 0.0, m_scr[i, 0:1, 0:1]) for i in range(CG)]}
        v = {}
        gates = {}

        def gate(k):
            cs = range(k * PCH, (k + 1) * PCH)
            for c in cs:
                gt_scr[c] = (pt[rows(c), gc] + bg_ref[...]).T
            pad = jnp.zeros((8 - CG * PCH, L), F32)
            ig_rows = jnp.concatenate(
                [gt_scr[c, pl.ds(grp * CG + i, 1), :] for i in range(CG) for c in cs] + [pad], axis=0)
            lf_rows = jnp.concatenate(
                [_log_sigmoid(gt_scr[c, pl.ds(grp * CG + i + H_C, 1), :]) for i in range(CG) for c in cs]
                + [pad], axis=0)
            b_rows = _dot_hi(lf_rows, jnp.where(row <= col, 1.0, 0.0))
            tall = jnp.zeros((L - 8, L), F32)
            gates[k] = dict(ig_rows=ig_rows, b_rows=b_rows,
                            b_cols=jnp.concatenate([b_rows, tall], axis=0).T,
                            ig_cols=jnp.concatenate([ig_rows, tall], axis=0).T)

        def stage_a(c):
            gk = gates[c // PCH]
            v[c] = []
            nst[c + 1] = []
            mst[c + 1] = []
            for i in range(CG):
                r = i * PCH + c % PCH
                b_r = gk["b_rows"][r:r + 1, :]
                ig_r = gk["ig_rows"][r:r + 1, :]
                b_c = gk["b_cols"][:, r:r + 1]
                ig_c = gk["ig_cols"][:, r:r + 1]
                m_prev = mst[c][i]
                log_d = jnp.where(tri, b_c - b_r + ig_r, NEG_INF)
                log_inter = b_c + m_prev
                m_t = jnp.maximum(log_inter, jnp.max(log_d, axis=-1, keepdims=True))
                m_new = m_t[L - 1:L, :]
                b_last = b_c[L - 1:L, :]
                w_end = jnp.exp(b_last - b_c + ig_c - m_new)
                cd = jnp.exp(b_last + m_prev - m_new)
                q = pt[rows(c), head(qc, i, DQK_C)] * (DQK_C ** -0.5)
                k = pt[rows(c), head(kc, i, DQK_C)]
                vv = pt[rows(c), head(vc, i, DV_C)]
                nst[c + 1].append(cd * nst[c][i] + jnp.sum(w_end * k, axis=0, keepdims=True))
                mst[c + 1].append(m_new)
                v[c].append(dict(w=jnp.exp(log_d - m_t), w_inter=jnp.exp(log_inter - m_t),
                                 floor=jnp.exp(-m_t), cd=cd, qb=q.astype(BF16), kb=k.astype(BF16),
                                 vb=vv.astype(BF16), vw=(vv * w_end).astype(BF16),
                                 qn=jnp.sum(q * nst[c][i], axis=-1, keepdims=True)))

        def stage_b(c):
            for i, d in enumerate(v[c]):
                d["sc"] = _dot_nt(d["qb"], d["kb"])
                d["upd"] = _dot_tn(d["vw"], d["kb"])
            for i, d in enumerate(v[c]):
                d["inter"] = _dot_nt(d["qb"], cst[c][i].astype(BF16))

        def stage_c(c):
            cst[c + 1] = []
            for i, d in enumerate(v[c]):
                sc = d["sc"] * d["w"]
                d["den"] = jnp.sum(sc, axis=-1, keepdims=True) + d["w_inter"] * d["qn"]
                d["sc"] = sc.astype(BF16)
                cst[c + 1].append(d["cd"] * cst[c][i] + d["upd"])

        def stage_d(c):
            for d in v[c]:
                d["num"] = _dot(d["sc"], d["vb"])

        def stage_e(c):
            for i, d in enumerate(v.pop(c)):
                num = d["num"] + d["w_inter"] * d["inter"]
                hh = num / jnp.maximum(jnp.abs(d["den"]), d["floor"])
                z = pt[rows(c), head(zc, i, DV_C)]
                ys = slice(i * DV_C, (i + 1) * DV_C)
                yc_ref[rows(c), ys] = (_head_norm(hh, gm_ref[0:1, ys]) * _silu(z)).astype(BF16)

        _chunk_pipeline(n, piece, (stage_a, stage_b, stage_c, stage_d, stage_e), gate)
        for i in range(CG):
            c_scr[i] = cst[n][i]
            n_scr[i] = nst[n][i]
            m_scr[i] = jnp.broadcast_to(mst[n][i], (8, 128))
            c_ref[0, i] = cst[n][i]
            n_ref[0, i] = nst[n][i]
            m_ref[0, i] = jnp.broadcast_to(mst[n][i], (1, 128))


def _odd_heads(hp, hs, w_t, bg, gm):
    k = hp.shape[1]
    ms = hs.shape[0]
    tile = lambda s: jnp.maximum(s - 1, 0)
    qw, vw = CG * DQK_C, CG * DV_C
    koff = H_C * DQK_C // qw
    voff = 2 * H_C * DQK_C // vw
    zoff = (2 * H_C * DQK_C + W_C) // vw
    seq = lambda g, s: (tile(s) // CPB, g, 0, 0)
    once = pl.Buffered(1)
    return pl.pallas_call(
        _odd_heads_kernel,
        grid=(H_C // CG, CN + 1),
        in_specs=[pl.BlockSpec((CT, k), lambda g, s: (tile(s), 0)),
                  pl.BlockSpec((ms, k), lambda g, s: (0, 0), pipeline_mode=once),
                  pl.BlockSpec((qw, k), lambda g, s: (g, 0), pipeline_mode=once),
                  pl.BlockSpec((qw, k), lambda g, s: (koff + g, 0), pipeline_mode=once),
                  pl.BlockSpec((vw, k), lambda g, s: (voff + g, 0), pipeline_mode=once),
                  pl.BlockSpec((vw, k), lambda g, s: (zoff + g, 0), pipeline_mode=once),
                  pl.BlockSpec((128, k), lambda g, s: (O_GATE // 128, 0), pipeline_mode=once),
                  pl.BlockSpec((1, 128), lambda g, s: (0, 0)),
                  pl.BlockSpec((1, vw), lambda g, s: (0, g))],
        out_specs=(pl.BlockSpec((CT, vw), lambda g, s: (tile(s), g)),
                   pl.BlockSpec((1, CG, DV_C, DQK_C), seq),
                   pl.BlockSpec((1, CG, 1, DQK_C), seq),
                   pl.BlockSpec((1, CG, 1, 128), seq),
                   pl.BlockSpec((ms, qw), lambda g, s: (0, g)),
                   pl.BlockSpec((ms, qw), lambda g, s: (0, g)),
                   pl.BlockSpec((ms, vw), lambda g, s: (0, g)),
                   pl.BlockSpec((ms, vw), lambda g, s: (0, g))),
        out_shape=(jax.ShapeDtypeStruct((BATCH * SEQ, W_C), BF16),
                   jax.ShapeDtypeStruct((BATCH, H_C, DV_C, DQK_C), F32),
                   jax.ShapeDtypeStruct((BATCH, H_C, 1, DQK_C), F32),
                   jax.ShapeDtypeStruct((BATCH, H_C, 1, 128), F32),
                   jax.ShapeDtypeStruct((ms, H_C * DQK_C), F32),
                   jax.ShapeDtypeStruct((ms, H_C * DQK_C), F32),
                   jax.ShapeDtypeStruct((ms, W_C), F32),
                   jax.ShapeDtypeStruct((ms, W_C), F32)),
        scratch_shapes=[pltpu.VMEM((CW, k), BF16),
                        pltpu.VMEM((CT, CW), F32),
                        pltpu.VMEM((CG, DV_C, DQK_C), F32),
                        pltpu.VMEM((CG, 1, DQK_C), F32),
                        pltpu.VMEM((CG, 8, 128), F32),
                        pltpu.VMEM((CT // CHUNK, CHUNK, CHUNK), F32)],
        compiler_params=_params(("arbitrary", "arbitrary")),
        name="odd_heads",
    )(hp, hs, w_t, w_t, w_t, w_t, w_t, bg, gm)


DT = 512


def _odd_mlp_fused_kernel(hp_ref, w0_ref, w1_ref, w2_ref, wt_ref, lng_ref, lnb_ref, ws_ref, bst_ref,
                          yd_ref, wb, pt, wsb):
    s = pl.program_id(0)
    L = CHUNK
    uc = slice(0, W_D)
    vc = slice(W_D, 2 * W_D)
    zc = slice(2 * W_D, 3 * W_D)

    @pl.when(s == 0)
    def _():
        sh = N_GATE
        wb[uc, :] = jnp.concatenate([w0_ref[sh:, :], w1_ref[0:sh, :]], axis=0).astype(BF16)
        wb[vc, :] = jnp.concatenate([w1_ref[sh:, :], w2_ref[0:sh, :]], axis=0).astype(BF16)
        wb[zc, :] = jnp.concatenate([w2_ref[sh:, :], wt_ref[...]], axis=0).astype(BF16)
        keep = (lax.broadcasted_iota(jnp.int32, (L, L), 0) >= lax.broadcasted_iota(jnp.int32, (L, L), 1))
        for g in range(G_D):
            wsb[g] = jnp.where(keep, ws_ref[g], 0.0).astype(BF16)

    @pl.when(s > 0)
    def _():
        n = DT // L
        rows = lambda c: slice(c * L, (c + 1) * L)
        grp = lambda sl, g: slice(sl.start + g * 128, sl.start + (g + 1) * 128)
        vn = {}
        mix = {}

        def project(pc):
            pt[:, pc] = _dot_nt(hp_ref[...], wb[pc, :])

        def stage_a(c):
            dv = lambda g: pt[rows(c), grp(vc, g)]
            tot = dv(0)
            for g in range(1, G_D):
                tot = tot + dv(g)
            mu = jnp.sum(tot, axis=-1, keepdims=True) * (1.0 / W_D)
            sq = (dv(0) - mu) * (dv(0) - mu)
            for g in range(1, G_D):
                sq = sq + (dv(g) - mu) * (dv(g) - mu)
            rstd = lax.rsqrt(jnp.sum(sq, axis=-1, keepdims=True) * (1.0 / W_D) + EPS)
            vn[c] = [((dv(g) - mu) * rstd * lng_ref[0:1, g * 128:(g + 1) * 128]
                      + lnb_ref[0:1, g * 128:(g + 1) * 128]).astype(BF16) for g in range(G_D)]

        def stage_b(c):
            mix[c] = [_dot(wsb[g], vn[c][g]) for g in range(G_D)]

        def stage_e(c, groups):
            for g in groups:
                sg = mix[c][g] + bst_ref[:, g:g + 1]
                d_u = pt[rows(c), grp(uc, g)]
                d_z = pt[rows(c), grp(zc, g)]
                yd_ref[rows(c), g * 128:(g + 1) * 128] = (d_u * sg * _silu(d_z)).astype(BF16)

        half = G_D // 2
        project(vc)
        for c in range(n):
            stage_a(c)
        project(zc)
        for c in range(n):
            stage_b(c)
        project(slice(uc.start, uc.start + half * 128))
        project(slice(uc.start + half * 128, uc.stop))
        for c in range(n):
            stage_e(c, range(half))
        for c in range(n):
            stage_e(c, range(half, G_D))


def _odd_mlp_fused(hp, w_t, lng, lnb, ws, bst):
    k = hp.shape[1]
    m = hp.shape[0]
    tile = lambda s: (jnp.maximum(s - 1, 0), 0)
    const2 = lambda s: (0, 0)
    t0 = O_GATE // IN_TN
    once = pl.Buffered(1)
    wspec = lambda j: pl.BlockSpec((IN_TN, k), lambda s: (t0 + j, 0), pipeline_mode=once)
    return pl.pallas_call(
        _odd_mlp_fused_kernel,
        grid=(m // DT + 1,),
        in_specs=[pl.BlockSpec((DT, k), tile),
                  wspec(0), wspec(1), wspec(2),
                  pl.BlockSpec((N_GATE, k), lambda s: ((t0 + 3) * (IN_TN // N_GATE), 0), pipeline_mode=once),
                  pl.BlockSpec((1, W_D), const2),
                  pl.BlockSpec((1, W_D), const2),
                  pl.BlockSpec((G_D, CHUNK, CHUNK), lambda s: (0, 0, 0)),
                  pl.BlockSpec((CHUNK, G_D), const2)],
        out_specs=pl.BlockSpec((DT, W_D), tile),
        out_shape=jax.ShapeDtypeStruct((m, W_D), BF16),
        scratch_shapes=[pltpu.VMEM((3 * W_D, k), BF16),
                        pltpu.VMEM((DT, 3 * W_D), F32),
                        pltpu.VMEM((G_D, CHUNK, CHUNK), BF16)],
        compiler_params=_params(("arbitrary",)),
        name="odd_mlp_fused",
    )(hp, w_t, w_t, w_t, w_t, lng, lnb, ws, bst)


def _in_proj_rows_kernel(h_ref, w_ref, wn_ref, o_ref, *, shift):
    wsh = jnp.concatenate([w_ref[shift:, :], wn_ref[...]], axis=0)
    o_ref[...] = _dot_nt(h_ref[...], wsh.astype(BF16))


def _in_proj_rows(h, w_t, n_out, shift, tile0):
    ms, k = h.shape
    return pl.pallas_call(
        functools.partial(_in_proj_rows_kernel, shift=shift),
        grid=(n_out // IN_TN,),
        in_specs=[pl.BlockSpec((ms, k), lambda j: (0, 0)),
                  pl.BlockSpec((IN_TN, k), lambda j: (j + tile0, 0)),
                  pl.BlockSpec((shift, k), lambda j: ((j + tile0 + 1) * (IN_TN // shift), 0))],
        out_specs=pl.BlockSpec((ms, IN_TN), lambda j: (0, j)),
        out_shape=jax.ShapeDtypeStruct((ms, n_out), F32),
        compiler_params=_params(("arbitrary",)),
        name="in_proj_rows",
    )(h, w_t, w_t)


def _odd_sample_kernel(pq_ref, pk_ref, pv_ref, pz_ref, h_ref, wg_ref, pd_ref,
                       c_ref, nrow_ref, mrow_ref, bg_ref, gm_ref, lng_ref, lnb_ref,
                       wt_ref, bt_ref,
                       yc_ref, yd_ref, vn_ref, co_ref, no_ref, mo_ref,
                       inter_scr, wmix_scr):
    h = pl.program_id(1)
    row = lax.broadcasted_iota(jnp.int32, (SR, SR), 0)
    col = lax.broadcasted_iota(jnp.int32, (SR, SR), 1)
    trow = row & 3

    @pl.when(jnp.logical_and(pl.program_id(0) == 0, h == 0))
    def _():
        mask_d = jnp.where((row >> 2) == (col >> 2), trow - (col & 3), -1) >= 0
        rep = jnp.where(trow == col, 1.0, 0.0).astype(BF16)
        for g in range(G_D):
            wtile = _dot_nt(_dot(rep, wt_ref[g].astype(BF16)).astype(BF16), rep)
            wmix_scr[g] = jnp.where(mask_d, wtile, 0.0).astype(BF16)

    @pl.when(h == 0)
    def _():
        dv = pd_ref[:, W_D:2 * W_D]
        mu = jnp.mean(dv, axis=-1, keepdims=True)
        xc = dv - mu
        var = jnp.mean(xc * xc, axis=-1, keepdims=True)
        rstd = lax.rsqrt(var + EPS)
        for g in range(G_D):
            sl = slice(g * 128, (g + 1) * 128)
            vn = xc[:, sl] * rstd * lng_ref[0:1, sl] + lnb_ref[0:1, sl]
            vn_ref[:, sl] = vn
            s = _dot(wmix_scr[g], vn.astype(BF16)) + bt_ref[:, g:g + 1]
            d_u = pd_ref[:, g * 128:(g + 1) * 128]
            d_z = pd_ref[:, 2 * W_D + g * 128:2 * W_D + (g + 1) * 128]
            yd_ref[:, sl] = (d_u * s * _silu(d_z)).astype(BF16)

    same = (row >> 2) == (col >> 2)
    mask = jnp.where(same, trow - (col & 3), -1) >= 0
    pre = _dot_nt(h_ref[...], wg_ref[...].astype(BF16)) + bg_ref[...]
    lf = _log_sigmoid(pre)
    b_full = _dot_hi(jnp.where(mask, 1.0, 0.0), lf)
    sub8 = lax.broadcasted_iota(jnp.int32, (8, DQK_C), 0)
    lane_b = lax.broadcasted_iota(jnp.int32, (DV_C, SR), 1) >> 2
    pre_t = pre.T
    b_full_t = b_full.T
    for hh in range(CS):
        hd = h * CS + hh
        qc = slice(hh * DQK_C, (hh + 1) * DQK_C)
        vc = slice(hh * DV_C, (hh + 1) * DV_C)
        sel_i = col == hd
        sel_f = col == hd + H_C
        ig_c = jnp.sum(jnp.where(sel_i, pre, 0.0), axis=-1, keepdims=True)
        b_c = jnp.sum(jnp.where(sel_f, b_full, 0.0), axis=-1, keepdims=True)
        sel_ir = row == hd
        sel_fr = row == hd + H_C
        ig_r = jnp.sum(jnp.where(sel_ir, pre_t, 0.0), axis=0, keepdims=True)
        b_r = jnp.sum(jnp.where(sel_fr, b_full_t, 0.0), axis=0, keepdims=True)
        m_prev = mrow_ref[hh]
        log_d = jnp.where(mask, b_c - b_r + ig_r, NEG_INF)
        log_inter = b_c + m_prev
        m_t = jnp.maximum(log_inter, jnp.max(log_d, axis=-1, keepdims=True))
        w = jnp.exp(log_d - m_t)
        w_inter = jnp.exp(log_inter - m_t)
        q = pq_ref[:, qc] * (DQK_C ** -0.5)
        k = pk_ref[:, qc]
        v = pv_ref[:, vc]
        qb = q.astype(BF16)
        kb = k.astype(BF16)
        vb = v.astype(BF16)
        sc = _dot_nt(qb, kb) * w
        for g in range(SR // 8):
            q8 = q[8 * g:8 * g + 8, :]
            q2 = jnp.concatenate([jnp.where(sub8 < DEC_SEQ, q8, 0.0), jnp.where(sub8 < DEC_SEQ, 0.0, q8)], axis=1)
            c_pair = jnp.concatenate([c_ref[2 * g + beta, hh].astype(BF16) for beta in range(2)], axis=1)
            inter_scr[8 * g:8 * g + 8, :] = _dot_nt(q2.astype(BF16), c_pair)
        n_rows = nrow_ref[hh]
        num = _dot(sc.astype(BF16), vb) + w_inter * inter_scr[...]
        den = jnp.sum(sc, axis=-1, keepdims=True) + w_inter * jnp.sum(q * n_rows, axis=-1, keepdims=True)
        hv = num / jnp.maximum(jnp.abs(den), jnp.exp(-m_t))
        yc_ref[:, vc] = (_head_norm(hv, gm_ref[:, vc]) * _silu(pz_ref[:, vc])).astype(BF16)

        stats = jnp.where(col == 0, m_t, jnp.where(col == 1, b_c, 0.0))
        last = _dot_hi(jnp.where(col == (row | 3), 1.0, 0.0), stats)
        m_new = last[:, 0:1]
        b_last = last[:, 1:2]
        w_end = jnp.exp(b_last - b_c + ig_c - m_new)
        cd = jnp.exp(b_last + m_prev - m_new)
        mo_ref[hh] = m_new
        no_ref[hh] = cd * n_rows + _dot_hi(jnp.where(same, 1.0, 0.0), w_end * k)
        vwt = (v * w_end).T
        for b in range(SB):
            lhs = jnp.where(lane_b == b, vwt, 0.0).astype(BF16)
            cd_b = cd[4 * b + 3:4 * b + 4, :]
            co_ref[b, hh] = cd_b * c_ref[b, hh] + _dot(lhs, kb)


def _odd_sample(pc, pd, h, w_o, c_state, n_rows, m_rows, bg, gm, lng, lnb, wt, bt):
    nb = DEC_BATCH // SB
    const2 = lambda i, h: (0, 0)
    return pl.pallas_call(
        _odd_sample_kernel,
        grid=(nb, H_C // CS),
        in_specs=[pl.BlockSpec((SR, CS * DQK_C), lambda i, h: (i, h)),
                  pl.BlockSpec((SR, CS * DQK_C), lambda i, h: (i, h)),
                  pl.BlockSpec((SR, CS * DV_C), lambda i, h: (i, h)),
                  pl.BlockSpec((SR, CS * DV_C), lambda i, h: (i, h)),
                  pl.BlockSpec((SR, D_MODEL), lambda i, h: (i, 0)),
                  pl.BlockSpec((128, D_MODEL), lambda i, h: (O_GATE // 128, 0)),
                  pl.BlockSpec((SR, 3 * W_D), lambda i, h: (i, 0)),
                  pl.BlockSpec((SB, CS, DV_C, DQK_C), lambda i, h: (i, h, 0, 0)),
                  pl.BlockSpec((CS, SR, DQK_C), lambda i, h: (h, i, 0)),
                  pl.BlockSpec((CS, SR, 1), lambda i, h: (h, i, 0)),
                  pl.BlockSpec((1, 128), const2),
                  pl.BlockSpec((1, CS * DV_C), lambda i, h: (0, h)),
                  pl.BlockSpec((1, W_D), const2),
                  pl.BlockSpec((1, W_D), const2),
                  pl.BlockSpec((G_D, SR, SR), lambda i, h: (0, 0, 0)),
                  pl.BlockSpec((SR, G_D), const2)],
        out_specs=(pl.BlockSpec((SR, CS * DV_C), lambda i, h: (i, h)),
                   pl.BlockSpec((SR, W_D), lambda i, h: (i, 0)),
                   pl.BlockSpec((SR, W_D), lambda i, h: (i, 0)),
                   pl.BlockSpec((SB, CS, DV_C, DQK_C), lambda i, h: (i, h, 0, 0)),
                   pl.BlockSpec((CS, SR, DQK_C), lambda i, h: (h, i, 0)),
                   pl.BlockSpec((CS, SR, 1), lambda i, h: (h, i, 0))),
        out_shape=(jax.ShapeDtypeStruct((DEC_BATCH * DEC_SEQ, W_C), BF16),
                   jax.ShapeDtypeStruct((DEC_BATCH * DEC_SEQ, W_D), BF16),
                   jax.ShapeDtypeStruct((DEC_BATCH * DEC_SEQ, W_D), F32),
                   jax.ShapeDtypeStruct((DEC_BATCH, H_C, DV_C, DQK_C), F32),
                   jax.ShapeDtypeStruct((H_C, DEC_BATCH * DEC_SEQ, DQK_C), F32),
                   jax.ShapeDtypeStruct((H_C, DEC_BATCH * DEC_SEQ, 1), F32)),
        scratch_shapes=[pltpu.VMEM((SR, DV_C), F32),
                        pltpu.VMEM((G_D, SR, SR), BF16)],
        compiler_params=_params(("arbitrary", "arbitrary")),
        name="odd_sample",
    )(*pc, h, w_o, pd, c_state, n_rows, m_rows, bg, gm, lng, lnb, wt, bt)


def _rope_tables(pos):
    inv = ROPE_BASE ** (-jnp.arange(0, DH_B, 2, dtype=F32) / DH_B)
    ang = pos.astype(F32)[:, None] * inv[None, :]
    cos = jnp.cos(ang)
    sin = jnp.sin(ang)
    return jnp.concatenate([cos, cos], axis=-1), jnp.concatenate([-sin, sin], axis=-1)


def kernel(x_prompt, x_sample, state_conv, state_ret, state_mlstm_C, state_mlstm_n, state_mlstm_m,
           norm_even, w_in_even, conv_w, ret_norm, w_out_even,
           norm_odd, w_in_odd, b_gate_odd, mlstm_norm, ln_v_g, ln_v_b,
           w_spatial, b_spatial, w_out_odd, norm_final):
    w_in_e = w_in_even[0]
    w_out_e = w_out_even[0].astype(BF16)
    w_o = w_in_odd[0].T
    w_out_o = w_out_odd[0].astype(BF16)
    g_even = norm_even[0][None, :]
    g_odd = norm_odd[0][None, :]
    g_fin = norm_final[None, :]
    cw = conv_w[0]
    g_ret = ret_norm[0][None, :]
    bg = jnp.concatenate([b_gate_odd[0], jnp.zeros((128 - 2 * H_C,), F32)])[None, :]
    gm = mlstm_norm[0][None, :]
    lng = ln_v_g[0][None, :]
    lnb = ln_v_b[0][None, :]
    ws = w_spatial[0]
    bst = b_spatial[0].T

    cos_p, sin_p = _rope_tables(jnp.arange(SEQ, dtype=jnp.int32))
    cos_s, sin_s = _rope_tables(PAST_LEN + jnp.arange(DEC_SEQ, dtype=jnp.int32))
    cos_s = jnp.tile(cos_s, (SB, 1))
    sin_s = jnp.tile(sin_s, (SB, 1))
    lg_tab = jnp.broadcast_to(jnp.asarray(LOG_GAMMA, F32)[:, None, None], (H_B, 1, 128))

    bt_s = jnp.tile(b_spatial[0][:, :DEC_SEQ].T, (SB, 1))

    rs = DEC_BATCH * DEC_SEQ
    xp = x_prompt.reshape(BATCH * SEQ, D_MODEL)
    xs = x_sample.reshape(rs, D_MODEL)
    hp, hs = _norm_cast(xp, xs, g_even, 2048)
    ya, conv_p, *ps_a = _even_conv(hp, hs, w_in_e, cw)
    yb, ret_p, *ps_b = _even_heads(hp, hs, w_in_e, g_ret, cos_p, sin_p, lg_tab)
    st_exp = jnp.pad(state_conv[0], ((0, 0), (0, DEC_SEQ - (CONV_W - 1)), (0, 0))).reshape(rs, W_A)
    ya_s, u_s, yb_s, ret_s = _even_sample(ps_a, ps_b, st_exp, state_ret[0], cw, g_ret, cos_s, sin_s, lg_tab)
    x1, h1, x1s, h1s = _outproj(ya, yb, xp, ya_s, yb_s, xs, w_out_e, g_odd, final=False)

    yc, c_p, n_p, m_p, *ps_c = _odd_heads(h1, h1s, w_o, bg, gm)
    yd = _odd_mlp_fused(h1, w_o, lng, lnb, ws, bst)
    pd_s = _in_proj_rows(h1s, w_o, 3 * W_D, N_GATE, O_GATE // IN_TN)
    n_rows = jnp.repeat(jnp.transpose(state_mlstm_n[0], (1, 0, 2)), DEC_SEQ, axis=1)
    m_rows = jnp.repeat(state_mlstm_m[0].T, DEC_SEQ, axis=1)[:, :, None]
    yc_s, yd_s, vn_s, c_s, no_s, mo_s = _odd_sample(
        ps_c, pd_s, h1s, w_o, state_mlstm_C[0], n_rows, m_rows, bg, gm, lng, lnb, ws, bt_s)
    y_prompt, y_sample = _outproj(yc, yd, x1, yc_s, yd_s, x1s, w_out_o, g_fin, final=True)

    conv_s = u_s.reshape(DEC_BATCH, DEC_SEQ, W_A)[:, DEC_SEQ - (CONV_W - 1):, :]
    n_s = jnp.transpose(no_s[:, DEC_SEQ - 1::DEC_SEQ, :], (1, 0, 2))
    m_s = mo_s[:, DEC_SEQ - 1::DEC_SEQ, 0].T
    return (y_prompt.reshape(BATCH, SEQ, D_MODEL),
            y_sample.reshape(DEC_BATCH, DEC_SEQ, D_MODEL),
            conv_p[None], conv_s[None],
            ret_p[None], ret_s[None],
            c_p[None], c_s[None],
            n_p[:, :, 0, :][None], n_s[None],
            m_p[:, :, 0, 0][None], m_s[None],
            vn_s.reshape(DEC_BATCH, DEC_SEQ, W_D)[None])
```
